```python
import math
import jax, jax.numpy as jnp
from jax import lax
import numpy as np

D_MODEL = 1024
BATCH = 8
SEQ = 4096
DEPTH = 2

EPS = 1e-6
A_HEADS = 4
A_DK = 128
A_DV = 128
A_CONV = 4
A_CHUNK = 64
A_W = A_HEADS * A_DK
B_HEADS = 4
B_DK = 128
B_DV = 128
B_CHUNK = 64
B_W = B_HEADS * B_DK
C_PAIRS = ((128, 1), (512, 4), (2048, 16))
C_GROUPS = 3
C_HEADS = 4
C_DH = 128
C_BLOCK = 128
C_W = C_HEADS * C_DH
REL_BUCKETS = 32
REL_MAX_DIST = 2048
D_FF = 2816
N_EXPERTS = 8
TOP_K = 2
D_FF_EXPERT = 3584
IN_SPLITS = (3 * A_W, A_HEADS, A_HEADS, A_HEADS * A_DV,
             B_W, B_W, B_HEADS * B_DV, B_HEADS * B_DV,
             3 * C_GROUPS * C_W,
             3 * D_MODEL)
D_IN = 3 * A_W + 2 * A_HEADS + A_HEADS * A_DV + 2 * B_W + 2 * B_HEADS * B_DV + 3 * C_GROUPS * C_W + 3 * D_MODEL

kernel_name = "hybrid_deltanet_hgrn2_dilated_moe"


def rms_norm(x, g):
    xf = x.astype(jnp.float32)
    y = xf * lax.rsqrt(jnp.mean(xf * xf, axis=-1, keepdims=True) + EPS)
    return (y * g.astype(jnp.float32)).astype(x.dtype)


def l2norm(x):
    return x * lax.rsqrt(jnp.sum(x * x, axis=-1, keepdims=True) + EPS)


def causal_conv(x, w):
    K, C = w.shape
    return lax.conv_general_dilated(x, w[:, None, :].astype(x.dtype), window_strides=(1,),
                                    padding=((K - 1, 0),), dimension_numbers=('NWC', 'WIO', 'NWC'),
                                    feature_group_count=C)


def to_chunks(t, C):
    B_, T = t.shape[:2]
    t = t.reshape(B_, T // C, C, *t.shape[2:])
    return jnp.moveaxis(t, 3, 1)


def gated_deltanet(qkv, beta_logit, alpha_logit, gate, conv_w, a_log, dt_bias, norm_g):
    f32 = jnp.float32
    B_, T, _ = qkv.shape
    C = A_CHUNK
    qkv = jax.nn.silu(causal_conv(qkv, conv_w)).astype(f32)
    q, k, v = jnp.split(qkv, 3, axis=-1)
    q = l2norm(q.reshape(B_, T, A_HEADS, A_DK)) * (A_DK ** -0.5)
    k = l2norm(k.reshape(B_, T, A_HEADS, A_DK))
    v = v.reshape(B_, T, A_HEADS, A_DV)
    beta = jax.nn.sigmoid(beta_logit.astype(f32))
    g = -jnp.exp(a_log.astype(f32)) * jax.nn.softplus(alpha_logit.astype(f32) + dt_bias.astype(f32))
    qc, kc, vc = to_chunks(q, C), to_chunks(k, C), to_chunks(v, C)
    bc = to_chunks(beta, C)
    gc = jnp.cumsum(to_chunks(g, C), axis=-1)
    incl = jnp.tril(jnp.ones((C, C), bool))
    strict = jnp.tril(jnp.ones((C, C), bool), -1)
    decay = jnp.exp(jnp.where(incl, gc[..., :, None] - gc[..., None, :], -jnp.inf))
    kb = kc * bc[..., None]
    m = jnp.einsum('bhnid,bhnjd->bhnij', kb, kc) * jnp.where(strict, decay, 0.0)
    tmat = m + jnp.eye(C, dtype=f32)
    rhs = jnp.concatenate([vc * bc[..., None], kb * jnp.exp(gc)[..., None]], axis=-1)
    sol = lax.linalg.triangular_solve(tmat, rhs, left_side=True, lower=True, unit_diagonal=True)
    u, w = sol[..., :A_DV], sol[..., A_DV:]
    attn = jnp.einsum('bhnid,bhnjd->bhnij', qc, kc) * decay
    qg = qc * jnp.exp(gc)[..., None]
    kg = kc * jnp.exp(gc[..., -1:] - gc)[..., None]
    gl = jnp.exp(gc[..., -1])
    xs = tuple(jnp.moveaxis(t, 2, 0) for t in (qg, kg, u, w, attn, gl))

    def step(S, inp):
        qg_n, kg_n, u_n, w_n, attn_n, gl_n = inp
        v_new = u_n - jnp.einsum('bhcd,bhde->bhce', w_n, S)
        o = jnp.einsum('bhcd,bhde->bhce', qg_n, S) + jnp.einsum('bhij,bhje->bhie', attn_n, v_new)
        S = S * gl_n[..., None, None] + jnp.einsum('bhcd,bhce->bhde', kg_n, v_new)
        return S, o

    S0 = jnp.zeros((B_, A_HEADS, A_DK, A_DV), f32)
    _, o = lax.scan(step, S0, xs)
    o = o.transpose(1, 0, 3, 2, 4).reshape(B_, T, A_HEADS, A_DV)
    o = rms_norm(o, norm_g) * jax.nn.silu(gate.astype(f32).reshape(B_, T, A_HEADS, A_DV))
    return o.reshape(B_, T, A_HEADS * A_DV)


def hgrn2(q, f_logit, i_in, gate, lb, norm_g):
    f32 = jnp.float32
    B_, T, _ = q.shape
    C = B_CHUNK
    q = jax.nn.silu(q.astype(f32))
    f = lb.astype(f32) + (1.0 - lb.astype(f32)) * jax.nn.sigmoid(f_logit.astype(f32))
    k = 1.0 - f
    g = jnp.log(f)
    qc = to_chunks(q.reshape(B_, T, B_HEADS, B_DK), C)
    kc = to_chunks(k.reshape(B_, T, B_HEADS, B_DK), C)
    vc = to_chunks(i_in.astype(f32).reshape(B_, T, B_HEADS, B_DV), C)
    gc = jnp.cumsum(to_chunks(g.reshape(B_, T, B_HEADS, B_DK), C), axis=3)
    incl = jnp.tril(jnp.ones((C, C), bool))
    qg = qc * jnp.exp(gc)
    kg = kc * jnp.exp(gc[..., -1:, :] - gc)
    gl = jnp.exp(gc[..., -1, :])
    xs = tuple(jnp.moveaxis(t, 2, 0) for t in (qc, kc, vc, gc, qg, kg, gl))

    def step(S, inp):
        q_n, k_n, v_n, g_n, qg_n, kg_n, gl_n = inp
        rel = jnp.exp(jnp.where(incl[:, :, None], g_n[:, :, :, None, :] - g_n[:, :, None, :, :], -jnp.inf))
        attn = jnp.einsum('bhijd,bhjd->bhij', q_n[:, :, :, None, :] * rel, k_n)
        o = jnp.einsum('bhid,bhde->bhie', qg_n, S) + jnp.einsum('bhij,bhje->bhie', attn, v_n)
        S = S * gl_n[..., None] + jnp.einsum('bhjd,bhje->bhde', kg_n, v_n)
        return S, o

    S0 = jnp.zeros((B_, B_HEADS, B_DK, B_DV), f32)
    _, o = lax.scan(step, S0, xs)
    o = o.transpose(1, 0, 3, 2, 4).reshape(B_, T, B_HEADS, B_DV)
    o = rms_norm(o, norm_g) * jax.nn.sigmoid(gate.astype(f32).reshape(B_, T, B_HEADS, B_DV))
    return o.reshape(B_, T, B_HEADS * B_DV)


def t5_bucket(n):
    max_exact = REL_BUCKETS // 2
    nf = jnp.maximum(n, 1).astype(jnp.float32)
    large = max_exact + (jnp.log(nf / max_exact) / math.log(REL_MAX_DIST / max_exact)
                         * (REL_BUCKETS - max_exact)).astype(jnp.int32)
    large = jnp.minimum(large, REL_BUCKETS - 1)
    return jnp.where(n < max_exact, n, large)


def dilated_group(q, k, v, table, window, dil):
    f32 = jnp.float32
    B_, T, H, D = q.shape
    span = window // dil
    L = T // dil
    nb = -(-L // C_BLOCK)
    Lp = nb * C_BLOCK

    def strided(t):
        t = t.reshape(B_, L, dil, H, D).transpose(0, 2, 3, 1, 4)
        t = jnp.pad(t, ((0, 0), (0, 0), (0, 0), (0, Lp - L), (0, 0)))
        return t.reshape(B_, dil, H, nb, C_BLOCK, D)

    def with_prev(t):
        prev = jnp.pad(t, ((0, 0), (0, 0), (0, 0), (1, 0), (0, 0), (0, 0)))[:, :, :, :-1]
        return jnp.concatenate([prev, t], axis=4)

    qb = strided(q)
    kb = with_prev(strided(k))
    vb = with_prev(strided(v))
    qi = jnp.arange(C_BLOCK)[:, None]
    kj = jnp.arange(2 * C_BLOCK)[None, :]
    dist = qi + C_BLOCK - kj
    blk = jnp.arange(nb)[:, None, None]
    valid = (dist >= 0) & (dist <= span) & (blk * C_BLOCK + kj - C_BLOCK >= 0)
    bias = table[t5_bucket(jnp.maximum(dist, 0) * dil)].transpose(2, 0, 1).astype(f32)
    s = jnp.einsum('brhnqd,brhnkd->brhnqk', qb, kb).astype(f32) * (D ** -0.5) + bias[:, None]
    s = jnp.where(valid, s, -jnp.inf)
    mx = jnp.max(s, axis=-1, keepdims=True)
    p = jnp.exp(s - mx)
    den = jnp.sum(p, axis=-1, keepdims=True)
    o = jnp.einsum('brhnqk,brhnkd->brhnqd', p, vb.astype(f32)) / den
    lse = (mx + jnp.log(den))[..., 0]

    def back(t):
        t = t.reshape(B_, dil, H, Lp, *t.shape[5:])[:, :, :, :L]
        t = jnp.moveaxis(t, 3, 1)
        return t.reshape(B_, T, H, *t.shape[4:])

    return back(o), back(lse)


def dilated_attention(qkv, q_gain, k_gain, rel_table):
    B_, T, _ = qkv.shape
    qkv = qkv.reshape(B_, T, 3, C_GROUPS, C_HEADS, C_DH)
    q = rms_norm(qkv[:, :, 0], q_gain[:, None, :])
    k = rms_norm(qkv[:, :, 1], k_gain[:, None, :])
    v = qkv[:, :, 2]
    outs, lses = [], []
    for gi, (window, dil) in enumerate(C_PAIRS):
        o, lse = dilated_group(q[:, :, gi], k[:, :, gi], v[:, :, gi],
                               rel_table[:, gi * C_HEADS:(gi + 1) * C_HEADS], window, dil)
        outs.append(o)
        lses.append(lse)
    wts = jax.nn.softmax(jnp.stack(lses, 0), axis=0)
    o = jnp.sum(wts[..., None] * jnp.stack(outs, 0), axis=0)
    return o.reshape(B_, T, C_W)


def swiglu(h, w1, w3, w2):
    return (jax.nn.silu(h @ w1) * (h @ w3)) @ w2


def moe_swiglu(h, w_router, w1, w3, w2):
    logits = (h @ w_router).astype(jnp.float32)
    top_val, top_idx = lax.top_k(logits, TOP_K)
    probs = jax.nn.softmax(top_val, axis=-1)
    combine = jnp.sum(jax.nn.one_hot(top_idx, N_EXPERTS, dtype=jnp.float32) * probs[..., None], axis=-2)
    combine = combine.astype(h.dtype)
    out = jnp.zeros_like(h)
    for e in range(N_EXPERTS):
        out = out + combine[..., e:e + 1] * swiglu(h, w1[e], w3[e], w2[e])
    return out


def setup_inputs(seed: int = 0) -> dict:
    key = jax.random.key(seed)
    ks = jax.random.split(key, 26)
    f32 = jnp.float32
    n_dense = (DEPTH + 1) // 2
    n_moe = DEPTH // 2

    def nrm(k, shape, fan_in):
        return jax.random.normal(k, shape, f32) * (fan_in ** -0.5)

    def gain(k, shape):
        return 1.0 + 0.02 * jax.random.normal(k, shape, f32)

    dt = jnp.exp(jax.random.uniform(ks[5], (DEPTH, A_HEADS), f32, math.log(0.001), math.log(0.1)))
    return {
        "x": jax.random.normal(ks[0], (BATCH, SEQ, D_MODEL), f32),
        "w_in": nrm(ks[1], (DEPTH, D_MODEL, D_IN), D_MODEL),
        "norm_mix": gain(ks[2], (DEPTH, D_MODEL)),
        "conv_a": nrm(ks[3], (DEPTH, A_CONV, 3 * A_W), A_CONV),
        "a_log": jnp.log(jax.random.uniform(ks[4], (DEPTH, A_HEADS), f32, 1.0, 16.0)),
        "dt_bias": dt + jnp.log(-jnp.expm1(-dt)),
        "gnorm_a": gain(ks[6], (DEPTH, A_DV)),
        "lb_logits": jax.random.normal(ks[7], (DEPTH, B_W), f32),
        "gnorm_b": gain(ks[8], (DEPTH, B_DV)),
        "qnorm_c": gain(ks[9], (DEPTH, C_GROUPS, C_DH)),
        "knorm_c": gain(ks[10], (DEPTH, C_GROUPS, C_DH)),
        "rel_bias": 0.2 * jax.random.normal(ks[11], (REL_BUCKETS, C_GROUPS * C_HEADS), f32),
        "w_br_a": nrm(ks[12], (DEPTH, A_HEADS * A_DV, D_MODEL), A_HEADS * A_DV),
        "w_br_b": nrm(ks[13], (DEPTH, B_HEADS * B_DV, D_MODEL), B_HEADS * B_DV),
        "w_br_c": nrm(ks[14], (DEPTH, C_W, D_MODEL), C_W),
        "w_out": nrm(ks[15], (DEPTH, D_MODEL, D_MODEL), D_MODEL),
        "norm_ffn": gain(ks[16], (DEPTH, D_MODEL)),
        "ffn_w1": nrm(ks[17], (n_dense, D_MODEL, D_FF), D_MODEL),
        "ffn_w3": nrm(ks[18], (n_dense, D_MODEL, D_FF), D_MODEL),
        "ffn_w2": nrm(ks[19], (n_dense, D_FF, D_MODEL), D_FF),
        "router": nrm(ks[20], (n_moe, D_MODEL, N_EXPERTS), D_MODEL),
        "moe_w1": nrm(ks[21], (n_moe, N_EXPERTS, D_MODEL, D_FF_EXPERT), D_MODEL),
        "moe_w3": nrm(ks[22], (n_moe, N_EXPERTS, D_MODEL, D_FF_EXPERT), D_MODEL),
        "moe_w2": nrm(ks[23], (n_moe, N_EXPERTS, D_FF_EXPERT, D_MODEL), D_FF_EXPERT),
    }


def reference(x, w_in, norm_mix, conv_a, a_log, dt_bias, gnorm_a, lb_logits, gnorm_b,
              qnorm_c, knorm_c, rel_bias, w_br_a, w_br_b, w_br_c, w_out, norm_ffn,
              ffn_w1, ffn_w3, ffn_w2, router, moe_w1, moe_w3, moe_w2):
    B_, T, _ = x.shape
    p_lb = jax.nn.softmax(lb_logits.astype(jnp.float32), axis=0)
    lower_bounds = jnp.cumsum(p_lb, axis=0) - p_lb[0:1]
    offsets = np.cumsum(IN_SPLITS)[:-1].tolist()
    for layer in range(DEPTH):
        h = rms_norm(x, norm_mix[layer])
        proj = h @ w_in[layer]
        (a_qkv, a_beta, a_alpha, a_gate, b_q, b_f, b_i, b_gate, c_qkv, br_gate) = jnp.split(proj, offsets, axis=-1)
        o_a = gated_deltanet(a_qkv, a_beta, a_alpha, a_gate, conv_a[layer], a_log[layer],
                             dt_bias[layer], gnorm_a[layer]).astype(x.dtype)
        o_b = hgrn2(b_q, b_f, b_i, b_gate, lower_bounds[layer], gnorm_b[layer]).astype(x.dtype)
        o_c = dilated_attention(c_qkv, qnorm_c[layer], knorm_c[layer], rel_bias).astype(x.dtype)
        gates = jax.nn.sigmoid(br_gate.reshape(B_, T, 3, D_MODEL))
        mix = (gates[:, :, 0] * (o_a @ w_br_a[layer])
               + gates[:, :, 1] * (o_b @ w_br_b[layer])
               + gates[:, :, 2] * (o_c @ w_br_c[layer]))
        x = x + mix @ w_out[layer]
        h = rms_norm(x, norm_ffn[layer])
        if layer % 2 == 0:
            li = layer // 2
            x = x + swiglu(h, ffn_w1[li], ffn_w3[li], ffn_w2[li])
        else:
            li = layer // 2
            x = x + moe_swiglu(h, router[li], moe_w1[li], moe_w3[li], moe_w2[li])
    return x
```

```python
import functools
import math

import numpy as np
import jax
import jax.numpy as jnp
from jax import lax
from jax.experimental import pallas as pl
from jax.experimental.pallas import tpu as pltpu

F32 = jnp.float32
BF16 = jnp.bfloat16
HIGHEST = lax.Precision.HIGHEST

D_MODEL = 1024
EPS = 1e-6
HEADS = 4
DH = 128
HW = HEADS * DH
A_CONV = 4
A_CHUNK = 64
B_ROWS = 64
B_BLK = 16
C_PAIRS = ((128, 1), (512, 4), (2048, 16))
C_GROUPS = 3
C_BLOCK = 128
REL_BUCKETS = 32
REL_MAX_DIST = 2048
N_EXPERTS = 8

OFF_BRG = 0
OFF_AQKV = 3072
OFF_AGATE = 4608
OFF_B = 5120
OFF_C = 7168
OFF_BA = 11776
NP = 12288

_R_AQKV, _R_BETA, _R_AGATE, _R_B, _R_C, _R_BRG, _R_END = 0, 1536, 1544, 2056, 4104, 8712, 11784

VMEM_LIMIT = 56 * 1024 * 1024


def _cparams(sem):
    return pltpu.CompilerParams(dimension_semantics=sem, vmem_limit_bytes=VMEM_LIMIT)


def _sigmoid(x):
    return 1.0 / (1.0 + jnp.exp(-x))


def _silu(x):
    return x * _sigmoid(x)


def _softplus(x):
    return jnp.maximum(x, 0.0) + jnp.log(1.0 + jnp.exp(-jnp.abs(x)))


def _dot(a, b):
    return jnp.dot(a, b, preferred_element_type=F32)


def _dot_nt(a, b, precision=None):
    return lax.dot_general(a, b, (((1,), (1,)), ((), ())), precision=precision,
                           preferred_element_type=F32)


def _dot_tn(a, b):
    return lax.dot_general(a, b, (((0,), (0,)), ((), ())), preferred_element_type=F32)


def _norm_proj_kernel(x_ref, g_ref, w_ref, o_ref, h_ref):
    @pl.when(pl.program_id(1) == 0)
    def _():
        x = x_ref[...]
        ms = jnp.mean(x * x, axis=-1, keepdims=True)
        h_ref[...] = (x * lax.rsqrt(ms + EPS) * g_ref[...]).astype(BF16)

    o_ref[...] = _dot(h_ref[...], w_ref[...])


def _norm_proj(x2, gain, w_bf16, tm=1024, tn=1024):
    m = x2.shape[0]
    n = w_bf16.shape[1]
    tm = min(tm, m)
    return pl.pallas_call(
        _norm_proj_kernel,
        grid=(m // tm, n // tn),
        in_specs=[pl.BlockSpec((tm, D_MODEL), lambda i, j: (i, 0)),
                  pl.BlockSpec((1, D_MODEL), lambda i, j: (0, 0)),
                  pl.BlockSpec((D_MODEL, tn), lambda i, j: (0, j))],
        out_specs=pl.BlockSpec((tm, tn), lambda i, j: (i, j)),
        out_shape=jax.ShapeDtypeStruct((m, n), F32),
        scratch_shapes=[pltpu.VMEM((tm, D_MODEL), BF16)],
        compiler_params=_cparams(("parallel", "arbitrary")),
        name="norm_proj",
    )(x2, gain.reshape(1, D_MODEL), w_bf16)


def _deltanet_kernel(qkv_ref, gate_ref, ba_ref, convw_ref, arow_ref, dtrow_ref, gn_ref,
                     o_ref, s_ref, xe_ref):
    C = A_CHUNK

    @pl.when(pl.program_id(1) == 0)
    def _():
        s_ref[...] = jnp.zeros_like(s_ref)
        xe_ref[0:8, :] = jnp.zeros((8, 3 * HW), F32)

    x = qkv_ref[0]
    xe_ref[8:8 + C, :] = x
    w = convw_ref[...]
    y = (w[3:4] * x + w[2:3] * xe_ref[7:7 + C, :] + w[1:2] * xe_ref[6:6 + C, :]
         + w[0:1] * xe_ref[5:5 + C, :])
    xe_ref[0:8, :] = x[C - 8:C]
    y = _silu(y)

    ba = ba_ref[0]
    beta_all = _sigmoid(ba)
    g_all = arow_ref[...] * _softplus(ba + dtrow_ref[...])

    row = lax.broadcasted_iota(jnp.int32, (C, C), 0)
    col = lax.broadcasted_iota(jnp.int32, (C, C), 1)
    lmat = (col <= row).astype(F32)
    rowx = lax.broadcasted_iota(jnp.int32, (C, DH + C), 0)
    colx = lax.broadcasted_iota(jnp.int32, (C, DH + C), 1)
    umask = (colx < DH) | (rowx > colx - DH)
    eye = (row == col).astype(F32)

    for h in range(HEADS):
        q = y[:, h * DH:(h + 1) * DH]
        k = y[:, HW + h * DH:HW + (h + 1) * DH]
        v = y[:, 2 * HW + h * DH:2 * HW + (h + 1) * DH]
        q = q * (lax.rsqrt(jnp.sum(q * q, axis=-1, keepdims=True) + EPS) * (DH ** -0.5))
        k = k * lax.rsqrt(jnp.sum(k * k, axis=-1, keepdims=True) + EPS)
        beta = beta_all[:, h:h + 1]
        gb = jnp.broadcast_to(g_all[:, HEADS + h:HEADS + h + 1], (C, DH + C))
        dext = jnp.dot(lmat, jnp.where(umask, gb, 0.0), precision=HIGHEST,
                       preferred_element_type=F32)
        gc = dext[:, :DH]
        dm = dext[:, DH:]
        edm = jnp.exp(dm)
        decay = jnp.where(row >= col, edm, 0.0)
        sdecay = jnp.where(row > col, edm, 0.0)
        egc = jnp.exp(gc)
        gl = gc[C - 1:C, :]
        kb = k * beta
        m = _dot_nt(kb, k, precision=HIGHEST) * sdecay
        pw = -m
        tinv = eye + pw
        for _ in range(5):
            pw = jnp.dot(pw, pw, precision=HIGHEST, preferred_element_type=F32)
            tinv = tinv + jnp.dot(tinv, pw, precision=HIGHEST, preferred_element_type=F32)
        rhs = jnp.concatenate([v * beta, kb * egc], axis=1)
        sol = jnp.dot(tinv, rhs, precision=HIGHEST, preferred_element_type=F32)
        u = sol[:, :DH]
        wk = sol[:, DH:]
        attn = _dot_nt(q.astype(BF16), k.astype(BF16)) * decay
        qg = q * egc
        kg = k * jnp.exp(gl - gc)
        s = s_ref[h]
        sb = s.astype(BF16)
        v_new = u - _dot(wk.astype(BF16), sb)
        vnb = v_new.astype(BF16)
        o = _dot(qg.astype(BF16), sb) + _dot(attn.astype(BF16), vnb)
        s_ref[h] = s * jnp.exp(gl) + _dot_tn(kg.astype(BF16), vnb)
        o = o * lax.rsqrt(jnp.mean(o * o, axis=-1, keepdims=True) + EPS) * gn_ref[...]
        o = o * _silu(gate_ref[0, :, h * DH:(h + 1) * DH])
        o_ref[0, :, h * DH:(h + 1) * DH] = o.astype(o_ref.dtype)


def _deltanet(proj3, conv_w, a_log, dt_bias, gnorm):
    b, t, _ = proj3.shape
    C = A_CHUNK
    pad = jnp.zeros((HEADS,), F32)
    arow = jnp.concatenate([pad, -jnp.exp(a_log.astype(F32)), jnp.zeros((120,), F32)]).reshape(1, 128)
    dtrow = jnp.concatenate([pad, dt_bias.astype(F32), jnp.zeros((120,), F32)]).reshape(1, 128)
    const = lambda shape: pl.BlockSpec(shape, lambda i, c: (0,) * len(shape))
    return pl.pallas_call(
        _deltanet_kernel,
        grid=(b, t // C),
        in_specs=[pl.BlockSpec((1, C, 3 * HW), lambda i, c: (i, c, OFF_AQKV // (3 * HW))),
                  pl.BlockSpec((1, C, HW), lambda i, c: (i, c, OFF_AGATE // HW)),
                  pl.BlockSpec((1, C, 128), lambda i, c: (i, c, OFF_BA // 128)),
                  const((A_CONV, 3 * HW)), const((1, 128)), const((1, 128)), const((1, DH))],
        out_specs=pl.BlockSpec((1, C, HW), lambda i, c: (i, c, 0)),
        out_shape=jax.ShapeDtypeStruct((b, t, HW), BF16),
        scratch_shapes=[pltpu.VMEM((HEADS, DH, DH), F32), pltpu.VMEM((C + 8, 3 * HW), F32)],
        compiler_params=_cparams(("parallel", "arbitrary")),
        name="deltanet",
    )(proj3, proj3, proj3, conv_w.astype(F32), arow, dtrow, gnorm.reshape(1, DH).astype(F32))


def _hgrn_kernel(q_ref, f_ref, i_ref, g_ref, lb_ref, gn_ref, o_ref, st_ref):
    R, K = B_ROWS, B_BLK

    @pl.when(pl.program_id(1) == 0)
    def _():
        st_ref[...] = jnp.zeros_like(st_ref)

    row = lax.broadcasted_iota(jnp.int32, (K, K), 0)
    col = lax.broadcasted_iota(jnp.int32, (K, K), 1)
    lmat = (col <= row).astype(F32)
    rowv = lax.broadcasted_iota(jnp.int32, (K, DH), 0)

    for h in range(HEADS):
        sl = slice(h * DH, (h + 1) * DH)
        lb = lb_ref[:, sl]
        st = st_ref[h]
        for blk in range(R // K):
            rs = slice(blk * K, (blk + 1) * K)
            q = _silu(q_ref[0, rs, sl])
            f = lb + (1.0 - lb) * _sigmoid(f_ref[0, rs, sl])
            k = 1.0 - f
            v = i_ref[0, rs, sl]
            gc = jnp.dot(lmat, jnp.log(f), precision=HIGHEST, preferred_element_type=F32)
            o = _dot_nt((q * jnp.exp(gc)).astype(BF16), st.astype(BF16))
            for j in range(K):
                rel = jnp.exp(jnp.where(rowv >= j, gc - gc[j:j + 1], -1e30))
                sj = jnp.sum(q * k[j:j + 1] * rel, axis=-1, keepdims=True)
                o = o + sj * v[j:j + 1]
            gl = gc[K - 1:K]
            kg = k * jnp.exp(gl - gc)
            st = st * jnp.exp(gl) + _dot_tn(v.astype(BF16), kg.astype(BF16))
            o = o * lax.rsqrt(jnp.mean(o * o, axis=-1, keepdims=True) + EPS) * gn_ref[...]
            o = o * _sigmoid(g_ref[0, rs, sl])
            o_ref[0, rs, sl] = o.astype(o_ref.dtype)
        st_ref[h] = st


def _hgrn(proj3, lb, gnorm):
    b, t, _ = proj3.shape
    R = B_ROWS
    seg = lambda s: pl.BlockSpec((1, R, HW), lambda i, c: (i, c, OFF_B // HW + s))
    const = lambda shape: pl.BlockSpec(shape, lambda i, c: (0,) * len(shape))
    return pl.pallas_call(
        _hgrn_kernel,
        grid=(b, t // R),
        in_specs=[seg(0), seg(1), seg(2), seg(3), const((1, HW)), const((1, DH))],
        out_specs=pl.BlockSpec((1, R, HW), lambda i, c: (i, c, 0)),
        out_shape=jax.ShapeDtypeStruct((b, t, HW), BF16),
        scratch_shapes=[pltpu.VMEM((HEADS, DH, DH), F32)],
        compiler_params=_cparams(("parallel", "arbitrary")),
        name="hgrn2",
    )(proj3, proj3, proj3, proj3, lb.reshape(1, HW).astype(F32), gnorm.reshape(1, DH).astype(F32))


def _t5_bucket_np(n):
    max_exact = REL_BUCKETS // 2
    nf = np.maximum(n, 1).astype(np.float32)
    large = max_exact + (np.log(nf / max_exact) / math.log(REL_MAX_DIST / max_exact)
                         * (REL_BUCKETS - max_exact)).astype(np.int32)
    large = np.minimum(large, REL_BUCKETS - 1)
    return np.where(n < max_exact, n, large)


def _dilated_kernel(span, q_ref, kp_ref, kc_ref, vp_ref, vc_ref, bias_ref, qg_ref, kg_ref,
                    o_ref, lse_ref):
    CB = C_BLOCK
    n = pl.program_id(2)
    qi = lax.broadcasted_iota(jnp.int32, (CB, 2 * CB), 0)
    kj = lax.broadcasted_iota(jnp.int32, (CB, 2 * CB), 1)
    dist = qi + CB - kj
    valid = (dist >= 0) & (dist <= span) & ((kj >= CB) | (n > 0))
    lane = lax.broadcasted_iota(jnp.int32, (CB, 128), 1)
    lse_tile = jnp.zeros((CB, 128), F32)

    def qk_norm(x, g):
        return x * lax.rsqrt(jnp.mean(x * x, axis=-1, keepdims=True) + EPS) * g

    for h in range(HEADS):
        sl = slice(h * DH, (h + 1) * DH)
        q = qk_norm(q_ref[0, :, sl], qg_ref[...]).astype(BF16)
        k = jnp.concatenate([kp_ref[0, :, sl], kc_ref[0, :, sl]], axis=0)
        k = qk_norm(k, kg_ref[...]).astype(BF16)
        v = jnp.concatenate([vp_ref[0, :, sl], vc_ref[0, :, sl]], axis=0).astype(BF16)
        s = _dot_nt(q, k) * (DH ** -0.5) + bias_ref[h]
        s = jnp.where(valid, s, -1e30)
        mx = jnp.max(s, axis=-1, keepdims=True)
        p = jnp.exp(s - mx)
        den = jnp.sum(p, axis=-1, keepdims=True)
        o = _dot(p.astype(BF16), v) / den
        o_ref[0, :, sl] = o
        lse_tile = jnp.where(lane == h, mx + jnp.log(den), lse_tile)
    lse_ref[0] = lse_tile


def _dilated_group(proj3, gi, rel_bias, q_gain, k_gain):
    b, t, _ = proj3.shape
    window, dil = C_PAIRS[gi]
    span = window // dil
    CB = C_BLOCK
    L = t // dil
    nb = L // CB
    projd = proj3.reshape(b, L, dil * NP)
    qi = np.arange(CB)[:, None]
    kj = np.arange(2 * CB)[None, :]
    bucket = _t5_bucket_np(np.maximum(qi + CB - kj, 0) * dil)
    bias = rel_bias[:, gi * HEADS:(gi + 1) * HEADS].astype(F32)[bucket]
    bias = jnp.transpose(bias, (2, 0, 1))

    def seg(which, prev):
        base = (OFF_C + which * C_GROUPS * HW + gi * HW) // HW
        if prev:
            return pl.BlockSpec((1, CB, HW),
                                lambda i, r, n: (i, jnp.maximum(n - 1, 0), r * (NP // HW) + base))
        return pl.BlockSpec((1, CB, HW), lambda i, r, n: (i, n, r * (NP // HW) + base))

    const = lambda shape: pl.BlockSpec(shape, lambda i, r, n: (0,) * len(shape))
    o, lse = pl.pallas_call(
        functools.partial(_dilated_kernel, span),
        grid=(b, dil, nb),
        in_specs=[seg(0, False), seg(1, True), seg(1, False), seg(2, True), seg(2, False),
                  const((HEADS, CB, 2 * CB)), const((1, DH)), const((1, DH))],
        out_specs=[pl.BlockSpec((1, CB, HW), lambda i, r, n: (i, n, r)),
                   pl.BlockSpec((1, CB, 128), lambda i, r, n: (i, n, r))],
        out_shape=[jax.ShapeDtypeStruct((b, L, dil * HW), F32),
                   jax.ShapeDtypeStruct((b, L, dil * 128), F32)],
        compiler_params=_cparams(("parallel", "parallel", "arbitrary")),
        name=f"dilated_g{gi}",
    )(projd, projd, projd, projd, projd, bias,
      q_gain.reshape(1, DH).astype(F32), k_gain.reshape(1, DH).astype(F32))
    return o.reshape(b * t, HW), lse.reshape(b * t, 128)


def _mix_kernel(x_ref, gate_ref, oa_ref, ob_ref, oc0_ref, oc1_ref, oc2_ref,
                l0_ref, l1_ref, l2_ref, wa_ref, wb_ref, wc_ref, wo_ref, out_ref):
    l0, l1, l2 = l0_ref[...], l1_ref[...], l2_ref[...]
    mx = jnp.maximum(jnp.maximum(l0, l1), l2)
    e0, e1, e2 = jnp.exp(l0 - mx), jnp.exp(l1 - mx), jnp.exp(l2 - mx)
    inv = 1.0 / (e0 + e1 + e2)
    parts = []
    for h in range(HEADS):
        sl = slice(h * DH, (h + 1) * DH)
        parts.append((e0[:, h:h + 1] * oc0_ref[:, sl] + e1[:, h:h + 1] * oc1_ref[:, sl]
                      + e2[:, h:h + 1] * oc2_ref[:, sl]) * inv[:, h:h + 1])
    oc = jnp.concatenate(parts, axis=1).astype(BF16)
    mix = (_sigmoid(gate_ref[:, 0:D_MODEL]) * _dot(oa_ref[...], wa_ref[...])
           + _sigmoid(gate_ref[:, D_MODEL:2 * D_MODEL]) * _dot(ob_ref[...], wb_ref[...])
           + _sigmoid(gate_ref[:, 2 * D_MODEL:3 * D_MODEL]) * _dot(oc, wc_ref[...]))
    out_ref[...] = x_ref[...] + _dot(mix.astype(BF16), wo_ref[...])


def _mix(x2, proj2, oa, ob, ocs, lses, wa, wb, wc, wo, tm=256):
    m = x2.shape[0]
    tm = min(tm, m)
    rowblk = lambda w: pl.BlockSpec((tm, w), lambda i: (i, 0))
    const = lambda shape: pl.BlockSpec(shape, lambda i: (0,) * len(shape))
    return pl.pallas_call(
        _mix_kernel,
        grid=(m // tm,),
        in_specs=[rowblk(D_MODEL), rowblk(3 * D_MODEL), rowblk(HW), rowblk(HW),
                  rowblk(HW), rowblk(HW), rowblk(HW), rowblk(128), rowblk(128), rowblk(128),
                  const((HW, D_MODEL)), const((HW, D_MODEL)), const((HW, D_MODEL)),
                  const((D_MODEL, D_MODEL))],
        out_specs=rowblk(D_MODEL),
        out_shape=jax.ShapeDtypeStruct((m, D_MODEL), F32),
        compiler_params=_cparams(("parallel",)),
        name="branch_mix",
    )(x2, proj2, oa, ob, ocs[0], ocs[1], ocs[2], lses[0], lses[1], lses[2],
      wa.astype(BF16), wb.astype(BF16), wc.astype(BF16), wo.astype(BF16))


def _ffn_kernel(x_ref, g_ref, w1_ref, w3_ref, w2_ref, o_ref, h_ref, acc_ref):
    j = pl.program_id(1)

    @pl.when(j == 0)
    def _():
        x = x_ref[...]
        ms = jnp.mean(x * x, axis=-1, keepdims=True)
        h_ref[...] = (x * lax.rsqrt(ms + EPS) * g_ref[...]).astype(BF16)
        acc_ref[...] = jnp.zeros_like(acc_ref)

    h = h_ref[...]
    a = _dot(h, w1_ref[...])
    b = _dot(h, w3_ref[...])
    acc_ref[...] += _dot((_silu(a) * b).astype(BF16), w2_ref[...])

    @pl.when(j == pl.num_programs(1) - 1)
    def _():
        o_ref[...] = x_ref[...] + acc_ref[...]


def _ffn(x2, gain, w1, w3, w2, tm=512, tf=1408):
    m = x2.shape[0]
    ff = w1.shape[1]
    tm = min(tm, m)
    return pl.pallas_call(
        _ffn_kernel,
        grid=(m // tm, ff // tf),
        in_specs=[pl.BlockSpec((tm, D_MODEL), lambda i, j: (i, 0)),
                  pl.BlockSpec((1, D_MODEL), lambda i, j: (0, 0)),
                  pl.BlockSpec((D_MODEL, tf), lambda i, j: (0, j)),
                  pl.BlockSpec((D_MODEL, tf), lambda i, j: (0, j)),
                  pl.BlockSpec((tf, D_MODEL), lambda i, j: (j, 0))],
        out_specs=pl.BlockSpec((tm, D_MODEL), lambda i, j: (i, 0)),
        out_shape=jax.ShapeDtypeStruct((m, D_MODEL), F32),
        scratch_shapes=[pltpu.VMEM((tm, D_MODEL), BF16), pltpu.VMEM((tm, D_MODEL), F32)],
        compiler_params=_cparams(("parallel", "arbitrary")),
        name="ffn",
    )(x2, gain.reshape(1, D_MODEL).astype(F32), w1.astype(BF16), w3.astype(BF16), w2.astype(BF16))


def _router_kernel(x_ref, g_ref, wr_ref, h_ref, comb_ref):
    x = x_ref[...]
    ms = jnp.mean(x * x, axis=-1, keepdims=True)
    h = x * lax.rsqrt(ms + EPS) * g_ref[...]
    h_ref[...] = h.astype(BF16)
    logits = jnp.dot(h, wr_ref[...], precision=HIGHEST, preferred_element_type=F32)
    lane = lax.broadcasted_iota(jnp.int32, logits.shape, 1)
    neg = jnp.float32(-jnp.inf)
    l1 = jnp.where(lane < N_EXPERTS, logits, neg)
    m1 = jnp.max(l1, axis=-1, keepdims=True)
    i1 = jnp.min(jnp.where(l1 == m1, lane, 128), axis=-1, keepdims=True)
    l2 = jnp.where(lane == i1, neg, l1)
    m2 = jnp.max(l2, axis=-1, keepdims=True)
    i2 = jnp.min(jnp.where(l2 == m2, lane, 128), axis=-1, keepdims=True)
    e = jnp.exp(m2 - m1)
    p1 = 1.0 / (1.0 + e)
    p2 = e / (1.0 + e)
    comb_ref[...] = jnp.where(lane == i1, p1, 0.0) + jnp.where(lane == i2, p2, 0.0)


def _router(x2, gain, w_router, tm=512):
    m = x2.shape[0]
    tm = min(tm, m)
    wr = jnp.pad(w_router.astype(F32), ((0, 0), (0, 128 - N_EXPERTS)))
    return pl.pallas_call(
        _router_kernel,
        grid=(m // tm,),
        in_specs=[pl.BlockSpec((tm, D_MODEL), lambda i: (i, 0)),
                  pl.BlockSpec((1, D_MODEL), lambda i: (0, 0)),
                  pl.BlockSpec((D_MODEL, 128), lambda i: (0, 0))],
        out_specs=[pl.BlockSpec((tm, D_MODEL), lambda i: (i, 0)),
                   pl.BlockSpec((tm, 128), lambda i: (i, 0))],
        out_shape=[jax.ShapeDtypeStruct((m, D_MODEL), BF16),
                   jax.ShapeDtypeStruct((m, 128), F32)],
        compiler_params=_cparams(("parallel",)),
        name="router",
    )(x2, gain.reshape(1, D_MODEL).astype(F32), wr)


def _moe_dense_kernel(x_ref, h_ref, comb_ref, w1_ref, w3_ref, w2_ref, o_ref, acc_ref):
    e = pl.program_id(1)
    j = pl.program_id(2)

    @pl.when((e == 0) & (j == 0))
    def _():
        acc_ref[...] = jnp.zeros_like(acc_ref)

    h = h_ref[...]
    a = _dot(h, w1_ref[0])
    b = _dot(h, w3_ref[0])
    y = _dot((_silu(a) * b).astype(BF16), w2_ref[0])
    lane = lax.broadcasted_iota(jnp.int32, comb_ref.shape, 1)
    c = jnp.sum(jnp.where(lane == e, comb_ref[...], 0.0), axis=-1, keepdims=True)
    acc_ref[...] += c * y

    @pl.when((e == pl.num_programs(1) - 1) & (j == pl.num_programs(2) - 1))
    def _():
        o_ref[...] = x_ref[...] + acc_ref[...]


def _moe_dense(x2, h, comb, w1, w3, w2, tm=1024, tf=1792):
    m = x2.shape[0]
    ne, _, ff = w1.shape
    tm = min(tm, m)
    return pl.pallas_call(
        _moe_dense_kernel,
        grid=(m // tm, ne, ff // tf),
        in_specs=[pl.BlockSpec((tm, D_MODEL), lambda i, e, j: (i, 0)),
                  pl.BlockSpec((tm, D_MODEL), lambda i, e, j: (i, 0)),
                  pl.BlockSpec((tm, 128), lambda i, e, j: (i, 0)),
                  pl.BlockSpec((1, D_MODEL, tf), lambda i, e, j: (e, 0, j)),
                  pl.BlockSpec((1, D_MODEL, tf), lambda i, e, j: (e, 0, j)),
                  pl.BlockSpec((1, tf, D_MODEL), lambda i, e, j: (e, j, 0))],
        out_specs=pl.BlockSpec((tm, D_MODEL), lambda i, e, j: (i, 0)),
        out_shape=jax.ShapeDtypeStruct((m, D_MODEL), F32),
        scratch_shapes=[pltpu.VMEM((tm, D_MODEL), F32)],
        compiler_params=_cparams(("parallel", "arbitrary", "arbitrary")),
        name="moe_dense",
    )(x2, h, comb, w1.astype(BF16), w3.astype(BF16), w2.astype(BF16))


def _relayout_w_in(w):
    k = w.shape[0]
    z = lambda n: jnp.zeros((k, n), w.dtype)
    cols = [w[:, _R_BRG:_R_END], w[:, _R_AQKV:_R_BETA], w[:, _R_AGATE:_R_B],
            w[:, _R_B:_R_C], w[:, _R_C:_R_BRG], w[:, _R_BETA:_R_AGATE],
            z(NP - OFF_BA - 2 * HEADS)]
    return jnp.concatenate(cols, axis=1).astype(BF16)


def _layer_mixers(x2, b, t, layer, lower_bounds, w_in, norm_mix, conv_a, a_log, dt_bias,
                  gnorm_a, gnorm_b, qnorm_c, knorm_c, rel_bias, w_br_a, w_br_b, w_br_c, w_out):
    proj2 = _norm_proj(x2, norm_mix[layer].astype(F32), _relayout_w_in(w_in[layer]))
    proj3 = proj2.reshape(b, t, NP)
    oa = _deltanet(proj3, conv_a[layer], a_log[layer], dt_bias[layer], gnorm_a[layer])
    ob = _hgrn(proj3, lower_bounds[layer], gnorm_b[layer])
    ocs, lses = [], []
    for gi in range(C_GROUPS):
        o, lse = _dilated_group(proj3, gi, rel_bias, qnorm_c[layer, gi], knorm_c[layer, gi])
        ocs.append(o)
        lses.append(lse)
    return _mix(x2, proj2, oa.reshape(b * t, HW), ob.reshape(b * t, HW), ocs, lses,
                w_br_a[layer], w_br_b[layer], w_br_c[layer], w_out[layer])


def kernel(x, w_in, norm_mix, conv_a, a_log, dt_bias, gnorm_a, lb_logits, gnorm_b, qnorm_c, knorm_c, rel_bias, w_br_a, w_br_b, w_br_c, w_out, norm_ffn, ffn_w1, ffn_w3, ffn_w2, router, moe_w1, moe_w3, moe_w2):
    b, t, _ = x.shape
    depth = w_in.shape[0]
    p_lb = jax.nn.softmax(lb_logits.astype(F32), axis=0)
    lower_bounds = jnp.cumsum(p_lb, axis=0) - p_lb[0:1]
    x2 = x.reshape(b * t, D_MODEL).astype(F32)
    for layer in range(depth):
        x2 = _layer_mixers(x2, b, t, layer, lower_bounds, w_in, norm_mix, conv_a, a_log, dt_bias,
                           gnorm_a, gnorm_b, qnorm_c, knorm_c, rel_bias,
                           w_br_a, w_br_b, w_br_c, w_out)
        li = layer // 2
        if layer % 2 == 0:
            x2 = _ffn(x2, norm_ffn[layer], ffn_w1[li], ffn_w3[li], ffn_w2[li])
        else:
            h, comb = _router(x2, norm_ffn[layer], router[li])
            x2 = _moe_dense(x2, h, comb, moe_w1[li], moe_w3[li], moe_w2[li])
    return x2.reshape(b, t, D_MODEL).astype(x.dtype)
```

```python
import functools
import math

import numpy as np
import jax
import jax.numpy as jnp
from jax import lax
from jax.experimental import pallas as pl
from jax.experimental.pallas import tpu as pltpu

F32 = jnp.float32
BF16 = jnp.bfloat16
HIGHEST = lax.Precision.HIGHEST

D_MODEL = 1024
EPS = 1e-6
HEADS = 4
DH = 128
HW = HEADS * DH
A_CONV = 4
A_CHUNK = 64
B_ROWS = 64
B_BLK = 16
C_PAIRS = ((128, 1), (512, 4), (2048, 16))
C_GROUPS = 3
C_BLOCK = 128
REL_BUCKETS = 32
REL_MAX_DIST = 2048
N_EXPERTS = 8

OFF_BRG = 0
OFF_AQKV = 3072
OFF_AGATE = 4608
OFF_B = 5120
OFF_C = 7168
OFF_BA = 11776
NP = 12288

_R_AQKV, _R_BETA, _R_AGATE, _R_B, _R_C, _R_BRG, _R_END = 0, 1536, 1544, 2056, 4104, 8712, 11784

VMEM_LIMIT = 56 * 1024 * 1024


def _cparams(sem):
    return pltpu.CompilerParams(dimension_semantics=sem, vmem_limit_bytes=VMEM_LIMIT)


def _sigmoid(x):
    return 1.0 / (1.0 + jnp.exp(-x))


def _silu(x):
    return x * _sigmoid(x)


def _softplus(x):
    return jnp.maximum(x, 0.0) + jnp.log(1.0 + jnp.exp(-jnp.abs(x)))


def _dot(a, b):
    return jnp.dot(a, b, preferred_element_type=F32)


def _dot_nt(a, b, precision=None):
    return lax.dot_general(a, b, (((1,), (1,)), ((), ())), precision=precision,
                           preferred_element_type=F32)


def _dot_tn(a, b):
    return lax.dot_general(a, b, (((0,), (0,)), ((), ())), preferred_element_type=F32)


def _norm_proj_kernel(x_ref, g_ref, w_ref, o_ref, h_ref):
    @pl.when(pl.program_id(1) == 0)
    def _():
        x = x_ref[...]
        ms = jnp.mean(x * x, axis=-1, keepdims=True)
        h_ref[...] = (x * lax.rsqrt(ms + EPS) * g_ref[...]).astype(BF16)

    o_ref[...] = _dot(h_ref[...], w_ref[...])


def _norm_proj(x2, gain, w_bf16, tm=1024, tn=1024):
    m = x2.shape[0]
    n = w_bf16.shape[1]
    tm = min(tm, m)
    return pl.pallas_call(
        _norm_proj_kernel,
        grid=(m // tm, n // tn),
        in_specs=[pl.BlockSpec((tm, D_MODEL), lambda i, j: (i, 0)),
                  pl.BlockSpec((1, D_MODEL), lambda i, j: (0, 0)),
                  pl.BlockSpec((D_MODEL, tn), lambda i, j: (0, j))],
        out_specs=pl.BlockSpec((tm, tn), lambda i, j: (i, j)),
        out_shape=jax.ShapeDtypeStruct((m, n), F32),
        scratch_shapes=[pltpu.VMEM((tm, D_MODEL), BF16)],
        compiler_params=_cparams(("parallel", "arbitrary")),
        name="norm_proj",
    )(x2, gain.reshape(1, D_MODEL), w_bf16)


def _split_bf16(a):
    hi = a.astype(BF16)
    lo = (a - hi.astype(F32)).astype(BF16)
    return hi, lo


def _dot3(a, b):
    ah, al = _split_bf16(a)
    bh, bl = _split_bf16(b)
    return _dot(ah, bh) + (_dot(ah, bl) + _dot(al, bh))


def _dot3_nt(a, b):
    ah, al = _split_bf16(a)
    bh, bl = _split_bf16(b)
    return _dot_nt(ah, bh) + (_dot_nt(ah, bl) + _dot_nt(al, bh))


def _dot_exact_lhs(a_bf16, b):
    b0 = b.astype(BF16)
    r1 = b - b0.astype(F32)
    b1 = r1.astype(BF16)
    b2 = (r1 - b1.astype(F32)).astype(BF16)
    return _dot(a_bf16, b0) + (_dot(a_bf16, b1) + _dot(a_bf16, b2))


def _deltanet_kernel(qkv_ref, gate_ref, ba_ref, convw_ref, arow_ref, dtrow_ref, gn_ref,
                     o_ref, s_ref, xe_ref):
    C = A_CHUNK
    HS = range(HEADS)

    @pl.when(pl.program_id(1) == 0)
    def _():
        s_ref[...] = jnp.zeros_like(s_ref)
        xe_ref[0:8, :] = jnp.zeros((8, 3 * HW), F32)

    x = qkv_ref[0]
    xe_ref[8:8 + C, :] = x
    w = convw_ref[...]
    y = (w[3:4] * x + w[2:3] * xe_ref[7:7 + C, :] + w[1:2] * xe_ref[6:6 + C, :]
         + w[0:1] * xe_ref[5:5 + C, :])
    xe_ref[0:8, :] = x[C - 8:C]
    y = _silu(y)

    ba = ba_ref[0]
    beta_all = _sigmoid(ba)
    g_all = arow_ref[...] * _softplus(ba + dtrow_ref[...])
    gate = _silu(gate_ref[0])
    s_old = [s_ref[h] for h in HS]

    row = lax.broadcasted_iota(jnp.int32, (C, C), 0)
    col = lax.broadcasted_iota(jnp.int32, (C, C), 1)
    lmat = (col <= row).astype(BF16)
    rowx = lax.broadcasted_iota(jnp.int32, (C, DH + C), 0)
    colx = lax.broadcasted_iota(jnp.int32, (C, DH + C), 1)
    umask = (colx < DH) | (rowx > colx - DH)
    eye = (row == col).astype(F32)
    bd8 = (row >> 3) == (col >> 3)

    def merge_mask(sh):
        return (((row >> (sh + 1)) == (col >> (sh + 1)))
                & (((row >> sh) & 1) == 1) & (((col >> sh) & 1) == 0))

    q = [y[:, h * DH:(h + 1) * DH] for h in HS]
    k = [y[:, HW + h * DH:HW + (h + 1) * DH] for h in HS]
    v = [y[:, 2 * HW + h * DH:2 * HW + (h + 1) * DH] for h in HS]
    q = [a * (lax.rsqrt(jnp.sum(a * a, axis=-1, keepdims=True) + EPS) * (DH ** -0.5)) for a in q]
    k = [a * lax.rsqrt(jnp.sum(a * a, axis=-1, keepdims=True) + EPS) for a in k]
    beta = [beta_all[:, h:h + 1] for h in HS]
    gb = [jnp.broadcast_to(g_all[:, HEADS + h:HEADS + h + 1], (C, DH + C)) for h in HS]
    dext = [_dot_exact_lhs(lmat, jnp.where(umask, gb[h], 0.0)) for h in HS]
    gc = [d[:, :DH] for d in dext]
    edm = [jnp.exp(d[:, DH:]) for d in dext]
    egc = [jnp.exp(g) for g in gc]
    gl = [g[C - 1:C, :] for g in gc]
    kb = [k[h] * beta[h] for h in HS]
    m = [_dot3_nt(kb[h], k[h]) * jnp.where(row > col, edm[h], 0.0) for h in HS]

    nd = [jnp.where(bd8, -a, 0.0) for a in m]
    p2 = [_dot3(a, a) for a in nd]
    p4 = [_dot3(a, a) for a in p2]
    x1 = [eye + nd[h] for h in HS]
    x1 = [x1[h] + _dot3(x1[h], p2[h]) for h in HS]
    xi = [x1[h] + _dot3(x1[h], p4[h]) for h in HS]
    for sh in (3, 4, 5):
        mm = merge_mask(sh)
        t = [_dot3(xi[h], jnp.where(mm, m[h], 0.0)) for h in HS]
        xi = [xi[h] - _dot3(t[h], xi[h]) for h in HS]

    rhs = [jnp.concatenate([v[h] * beta[h], kb[h] * egc[h]], axis=1) for h in HS]
    sol = [_dot3(xi[h], rhs[h]) for h in HS]
    attn = [_dot_nt(q[h].astype(BF16), k[h].astype(BF16)) * jnp.where(row >= col, edm[h], 0.0)
            for h in HS]
    qg = [(q[h] * egc[h]).astype(BF16) for h in HS]
    kg = [(k[h] * jnp.exp(gl[h] - gc[h])).astype(BF16) for h in HS]
    sb = [a.astype(BF16) for a in s_old]
    v_new = [(sol[h][:, :DH] - _dot(sol[h][:, DH:].astype(BF16), sb[h])).astype(BF16) for h in HS]
    o = [_dot(qg[h], sb[h]) + _dot(attn[h].astype(BF16), v_new[h]) for h in HS]
    s_new = [s_old[h] * jnp.exp(gl[h]) + _dot_tn(kg[h], v_new[h]) for h in HS]
    for h in HS:
        s_ref[h] = s_new[h]
    o = [a * lax.rsqrt(jnp.mean(a * a, axis=-1, keepdims=True) + EPS) * gn_ref[...] for a in o]
    o_ref[0] = (jnp.concatenate(o, axis=1) * gate).astype(o_ref.dtype)


def _deltanet(proj3, conv_w, a_log, dt_bias, gnorm):
    b, t, _ = proj3.shape
    C = A_CHUNK
    pad = jnp.zeros((HEADS,), F32)
    arow = jnp.concatenate([pad, -jnp.exp(a_log.astype(F32)), jnp.zeros((120,), F32)]).reshape(1, 128)
    dtrow = jnp.concatenate([pad, dt_bias.astype(F32), jnp.zeros((120,), F32)]).reshape(1, 128)
    const = lambda shape: pl.BlockSpec(shape, lambda i, c: (0,) * len(shape))
    return pl.pallas_call(
        _deltanet_kernel,
        grid=(b, t // C),
        in_specs=[pl.BlockSpec((1, C, 3 * HW), lambda i, c: (i, c, OFF_AQKV // (3 * HW))),
                  pl.BlockSpec((1, C, HW), lambda i, c: (i, c, OFF_AGATE // HW)),
                  pl.BlockSpec((1, C, 128), lambda i, c: (i, c, OFF_BA // 128)),
                  const((A_CONV, 3 * HW)), const((1, 128)), const((1, 128)), const((1, DH))],
        out_specs=pl.BlockSpec((1, C, HW), lambda i, c: (i, c, 0)),
        out_shape=jax.ShapeDtypeStruct((b, t, HW), BF16),
        scratch_shapes=[pltpu.VMEM((HEADS, DH, DH), F32), pltpu.VMEM((C + 8, 3 * HW), F32)],
        compiler_params=_cparams(("parallel", "arbitrary")),
        name="deltanet",
    )(proj3, proj3, proj3, conv_w.astype(F32), arow, dtrow, gnorm.reshape(1, DH).astype(F32))


def _hgrn_kernel(q_ref, f_ref, i_ref, g_ref, lb_ref, gn_ref, o_ref, st_ref):
    R, K = B_ROWS, B_BLK
    HS = range(HEADS)

    @pl.when(pl.program_id(1) == 0)
    def _():
        st_ref[...] = jnp.zeros_like(st_ref)

    row = lax.broadcasted_iota(jnp.int32, (K, K), 0)
    col = lax.broadcasted_iota(jnp.int32, (K, K), 1)
    lmat = (col <= row).astype(BF16)
    rowv = lax.broadcasted_iota(jnp.int32, (K, DH), 0)
    sls = [slice(h * DH, (h + 1) * DH) for h in HS]
    lb = [lb_ref[:, sl] for sl in sls]
    st = [st_ref[h] for h in HS]

    for blk in range(R // K):
        rs = slice(blk * K, (blk + 1) * K)
        q = [_silu(q_ref[0, rs, sl]) for sl in sls]
        f = [lb[h] + (1.0 - lb[h]) * _sigmoid(f_ref[0, rs, sls[h]]) for h in HS]
        k = [1.0 - a for a in f]
        v = [i_ref[0, rs, sl] for sl in sls]
        gc = [_dot_exact_lhs(lmat, jnp.log(a)) for a in f]
        o = [_dot_nt((q[h] * jnp.exp(gc[h])).astype(BF16), st[h].astype(BF16)) for h in HS]
        for j in range(K):
            for h in HS:
                rel = jnp.exp(jnp.where(rowv >= j, gc[h] - gc[h][j:j + 1], -1e30))
                sj = jnp.sum(q[h] * k[h][j:j + 1] * rel, axis=-1, keepdims=True)
                o[h] = o[h] + sj * v[h][j:j + 1]
        gl = [a[K - 1:K] for a in gc]
        kg = [(k[h] * jnp.exp(gl[h] - gc[h])).astype(BF16) for h in HS]
        st = [st[h] * jnp.exp(gl[h]) + _dot_tn(v[h].astype(BF16), kg[h]) for h in HS]
        o = [a * lax.rsqrt(jnp.mean(a * a, axis=-1, keepdims=True) + EPS) * gn_ref[...] for a in o]
        o_ref[0, rs, :] = (jnp.concatenate(o, axis=1) * _sigmoid(g_ref[0, rs, :])).astype(o_ref.dtype)
    for h in HS:
        st_ref[h] = st[h]


def _hgrn(proj3, lb, gnorm):
    b, t, _ = proj3.shape
    R = B_ROWS
    seg = lambda s: pl.BlockSpec((1, R, HW), lambda i, c: (i, c, OFF_B // HW + s))
    const = lambda shape: pl.BlockSpec(shape, lambda i, c: (0,) * len(shape))
    return pl.pallas_call(
        _hgrn_kernel,
        grid=(b, t // R),
        in_specs=[seg(0), seg(1), seg(2), seg(3), const((1, HW)), const((1, DH))],
        out_specs=pl.BlockSpec((1, R, HW), lambda i, c: (i, c, 0)),
        out_shape=jax.ShapeDtypeStruct((b, t, HW), BF16),
        scratch_shapes=[pltpu.VMEM((HEADS, DH, DH), F32)],
        compiler_params=_cparams(("parallel", "arbitrary")),
        name="hgrn2",
    )(proj3, proj3, proj3, proj3, lb.reshape(1, HW).astype(F32), gnorm.reshape(1, DH).astype(F32))


def _t5_bucket_np(n):
    max_exact = REL_BUCKETS // 2
    nf = np.maximum(n, 1).astype(np.float32)
    large = max_exact + (np.log(nf / max_exact) / math.log(REL_MAX_DIST / max_exact)
                         * (REL_BUCKETS - max_exact)).astype(np.int32)
    large = np.minimum(large, REL_BUCKETS - 1)
    return np.where(n < max_exact, n, large)


def _dilated_kernel(span, dil, hpb, q_ref, kp_ref, kc_ref, vp_ref, vc_ref, bias_ref, qg_ref, kg_ref,
                    o_ref, lse_ref):
    CB = C_BLOCK
    n = pl.program_id(1)
    qi = lax.broadcasted_iota(jnp.int32, (CB, 2 * CB), 0)
    kj = lax.broadcasted_iota(jnp.int32, (CB, 2 * CB), 1)
    dist = qi + CB - kj
    valid = (dist >= 0) & (dist <= span) & ((kj >= CB) | (n > 0))

    def qk_norm(x, g):
        return x * lax.rsqrt(jnp.mean(x * x, axis=-1, keepdims=True) + EPS) * g

    def residue(r):
        rows = pl.ds(r, CB, stride=dil) if dil > 1 else pl.ds(0, CB)
        for h in range(hpb):
            sl = slice(h * DH, (h + 1) * DH)
            q = qk_norm(q_ref[0, rows, sl], qg_ref[...]).astype(BF16)
            k = jnp.concatenate([kp_ref[0, rows, sl], kc_ref[0, rows, sl]], axis=0)
            k = qk_norm(k, kg_ref[...]).astype(BF16)
            v = jnp.concatenate([vp_ref[0, rows, sl], vc_ref[0, rows, sl]], axis=0).astype(BF16)
            s = _dot_nt(q, k) * (DH ** -0.5) + bias_ref[h]
            s = jnp.where(valid, s, -1e30)
            mx = jnp.max(s, axis=-1, keepdims=True)
            p = jnp.exp(s - mx)
            den = jnp.sum(p, axis=-1, keepdims=True)
            o_ref[0, rows, sl] = _dot(p.astype(BF16), v) / den
            lse_ref[0, rows, sl] = jnp.broadcast_to(mx + jnp.log(den), (CB, DH))

    if dil == 1:
        residue(0)
    else:
        def body(r, carry):
            residue(r)
            return carry
        lax.fori_loop(0, dil, body, 0)


def _dilated_group(proj3, gi, rel_bias, q_gain, k_gain):
    b, t, _ = proj3.shape
    window, dil = C_PAIRS[gi]
    span = window // dil
    CB = C_BLOCK
    rb = CB * dil
    hpb = HEADS if dil == 1 else 1
    bw = hpb * DH
    qi = np.arange(CB)[:, None]
    kj = np.arange(2 * CB)[None, :]
    bucket = _t5_bucket_np(np.maximum(qi + CB - kj, 0) * dil)
    bias = rel_bias[:, gi * HEADS:(gi + 1) * HEADS].astype(F32)[bucket]
    bias = jnp.transpose(bias, (2, 0, 1))

    def seg(which, prev):
        base = (OFF_C + which * C_GROUPS * HW + gi * HW) // bw
        if prev:
            return pl.BlockSpec((1, rb, bw), lambda i, n, h: (i, jnp.maximum(n - 1, 0), base + h))
        return pl.BlockSpec((1, rb, bw), lambda i, n, h: (i, n, base + h))

    const = lambda shape: pl.BlockSpec(shape, lambda i, n, h: (0,) * len(shape))
    o, lse = pl.pallas_call(
        functools.partial(_dilated_kernel, span, dil, hpb),
        grid=(b, t // rb, HEADS // hpb),
        in_specs=[seg(0, False), seg(1, True), seg(1, False), seg(2, True), seg(2, False),
                  pl.BlockSpec((hpb, CB, 2 * CB), lambda i, n, h: (h, 0, 0)),
                  const((1, DH)), const((1, DH))],
        out_specs=[pl.BlockSpec((1, rb, bw), lambda i, n, h: (i, n, h)),
                   pl.BlockSpec((1, rb, bw), lambda i, n, h: (i, n, h))],
        out_shape=[jax.ShapeDtypeStruct((b, t, HW), F32),
                   jax.ShapeDtypeStruct((b, t, HW), F32)],
        compiler_params=_cparams(("parallel", "arbitrary", "arbitrary")),
        name=f"dilated_g{gi}",
    )(proj3, proj3, proj3, proj3, proj3, bias,
      q_gain.reshape(1, DH).astype(F32), k_gain.reshape(1, DH).astype(F32))
    return o.reshape(b * t, HW), lse.reshape(b * t, HW)


def _mix_kernel(x_ref, gate_ref, oa_ref, ob_ref, oc0_ref, oc1_ref, oc2_ref,
                l0_ref, l1_ref, l2_ref, wa_ref, wb_ref, wc_ref, wo_ref, out_ref):
    l0, l1, l2 = l0_ref[...], l1_ref[...], l2_ref[...]
    mx = jnp.maximum(jnp.maximum(l0, l1), l2)
    e0, e1, e2 = jnp.exp(l0 - mx), jnp.exp(l1 - mx), jnp.exp(l2 - mx)
    oc = (e0 * oc0_ref[...] + e1 * oc1_ref[...] + e2 * oc2_ref[...]) / (e0 + e1 + e2)
    mix = (_sigmoid(gate_ref[:, 0:D_MODEL]) * _dot(oa_ref[...], wa_ref[...])
           + _sigmoid(gate_ref[:, D_MODEL:2 * D_MODEL]) * _dot(ob_ref[...], wb_ref[...])
           + _sigmoid(gate_ref[:, 2 * D_MODEL:3 * D_MODEL]) * _dot(oc.astype(BF16), wc_ref[...]))
    out_ref[...] = x_ref[...] + _dot(mix.astype(BF16), wo_ref[...])


def _mix(x2, proj2, oa, ob, ocs, lses, wa, wb, wc, wo, tm=256):
    m = x2.shape[0]
    tm = min(tm, m)
    rowblk = lambda w: pl.BlockSpec((tm, w), lambda i: (i, 0))
    const = lambda shape: pl.BlockSpec(shape, lambda i: (0,) * len(shape))
    return pl.pallas_call(
        _mix_kernel,
        grid=(m // tm,),
        in_specs=[rowblk(D_MODEL), rowblk(3 * D_MODEL), rowblk(HW), rowblk(HW),
                  rowblk(HW), rowblk(HW), rowblk(HW), rowblk(HW), rowblk(HW), rowblk(HW),
                  const((HW, D_MODEL)), const((HW, D_MODEL)), const((HW, D_MODEL)),
                  const((D_MODEL, D_MODEL))],
        out_specs=rowblk(D_MODEL),
        out_shape=jax.ShapeDtypeStruct((m, D_MODEL), F32),
        compiler_params=_cparams(("parallel",)),
        name="branch_mix",
    )(x2, proj2, oa, ob, ocs[0], ocs[1], ocs[2], lses[0], lses[1], lses[2],
      wa.astype(BF16), wb.astype(BF16), wc.astype(BF16), wo.astype(BF16))


def _ffn_kernel(x_ref, g_ref, w1_ref, w3_ref, w2_ref, o_ref, h_ref, acc_ref):
    j = pl.program_id(1)

    @pl.when(j == 0)
    def _():
        x = x_ref[...]
        ms = jnp.mean(x * x, axis=-1, keepdims=True)
        h_ref[...] = (x * lax.rsqrt(ms + EPS) * g_ref[...]).astype(BF16)
        acc_ref[...] = jnp.zeros_like(acc_ref)

    h = h_ref[...]
    a = _dot(h, w1_ref[...])
    b = _dot(h, w3_ref[...])
    acc_ref[...] += _dot((_silu(a) * b).astype(BF16), w2_ref[...])

    @pl.when(j == pl.num_programs(1) - 1)
    def _():
        o_ref[...] = x_ref[...] + acc_ref[...]


def _ffn(x2, gain, w1, w3, w2, tm=512, tf=1408):
    m = x2.shape[0]
    ff = w1.shape[1]
    tm = min(tm, m)
    return pl.pallas_call(
        _ffn_kernel,
        grid=(m // tm, ff // tf),
        in_specs=[pl.BlockSpec((tm, D_MODEL), lambda i, j: (i, 0)),
                  pl.BlockSpec((1, D_MODEL), lambda i, j: (0, 0)),
                  pl.BlockSpec((D_MODEL, tf), lambda i, j: (0, j)),
                  pl.BlockSpec((D_MODEL, tf), lambda i, j: (0, j)),
                  pl.BlockSpec((tf, D_MODEL), lambda i, j: (j, 0))],
        out_specs=pl.BlockSpec((tm, D_MODEL), lambda i, j: (i, 0)),
        out_shape=jax.ShapeDtypeStruct((m, D_MODEL), F32),
        scratch_shapes=[pltpu.VMEM((tm, D_MODEL), BF16), pltpu.VMEM((tm, D_MODEL), F32)],
        compiler_params=_cparams(("parallel", "arbitrary")),
        name="ffn",
    )(x2, gain.reshape(1, D_MODEL).astype(F32), w1.astype(BF16), w3.astype(BF16), w2.astype(BF16))


def _router_kernel(x_ref, g_ref, wr_ref, h_ref, comb_ref):
    x = x_ref[...]
    ms = jnp.mean(x * x, axis=-1, keepdims=True)
    h = x * lax.rsqrt(ms + EPS) * g_ref[...]
    h_ref[...] = h.astype(BF16)
    logits = jnp.dot(h, wr_ref[...], precision=HIGHEST, preferred_element_type=F32)
    lane = lax.broadcasted_iota(jnp.int32, logits.shape, 1)
    neg = jnp.float32(-jnp.inf)
    l1 = jnp.where(lane < N_EXPERTS, logits, neg)
    m1 = jnp.max(l1, axis=-1, keepdims=True)
    i1 = jnp.min(jnp.where(l1 == m1, lane, 128), axis=-1, keepdims=True)
    l2 = jnp.where(lane == i1, neg, l1)
    m2 = jnp.max(l2, axis=-1, keepdims=True)
    i2 = jnp.min(jnp.where(l2 == m2, lane, 128), axis=-1, keepdims=True)
    e = jnp.exp(m2 - m1)
    p1 = 1.0 / (1.0 + e)
    p2 = e / (1.0 + e)
    comb_ref[...] = jnp.where(lane == i1, p1, 0.0) + jnp.where(lane == i2, p2, 0.0)


def _router(x2, gain, w_router, tm=512):
    m = x2.shape[0]
    tm = min(tm, m)
    wr = jnp.pad(w_router.astype(F32), ((0, 0), (0, 128 - N_EXPERTS)))
    return pl.pallas_call(
        _router_kernel,
        grid=(m // tm,),
        in_specs=[pl.BlockSpec((tm, D_MODEL), lambda i: (i, 0)),
                  pl.BlockSpec((1, D_MODEL), lambda i: (0, 0)),
                  pl.BlockSpec((D_MODEL, 128), lambda i: (0, 0))],
        out_specs=[pl.BlockSpec((tm, D_MODEL), lambda i: (i, 0)),
                   pl.BlockSpec((tm, 128), lambda i: (i, 0))],
        out_shape=[jax.ShapeDtypeStruct((m, D_MODEL), BF16),
                   jax.ShapeDtypeStruct((m, 128), F32)],
        compiler_params=_cparams(("parallel",)),
        name="router",
    )(x2, gain.reshape(1, D_MODEL).astype(F32), wr)


def _moe_dense_kernel(x_ref, h_ref, comb_ref, w1_ref, w3_ref, w2_ref, o_ref, acc_ref):
    e = pl.program_id(1)
    j = pl.program_id(2)

    @pl.when((e == 0) & (j == 0))
    def _():
        acc_ref[...] = jnp.zeros_like(acc_ref)

    h = h_ref[...]
    a = _dot(h, w1_ref[0])
    b = _dot(h, w3_ref[0])
    y = _dot((_silu(a) * b).astype(BF16), w2_ref[0])
    lane = lax.broadcasted_iota(jnp.int32, comb_ref.shape, 1)
    c = jnp.sum(jnp.where(lane == e, comb_ref[...], 0.0), axis=-1, keepdims=True)
    acc_ref[...] += c * y

    @pl.when((e == pl.num_programs(1) - 1) & (j == pl.num_programs(2) - 1))
    def _():
        o_ref[...] = x_ref[...] + acc_ref[...]


def _moe_dense(x2, h, comb, w1, w3, w2, tm=1024, tf=1792):
    m = x2.shape[0]
    ne, _, ff = w1.shape
    tm = min(tm, m)
    return pl.pallas_call(
        _moe_dense_kernel,
        grid=(m // tm, ne, ff // tf),
        in_specs=[pl.BlockSpec((tm, D_MODEL), lambda i, e, j: (i, 0)),
                  pl.BlockSpec((tm, D_MODEL), lambda i, e, j: (i, 0)),
                  pl.BlockSpec((tm, 128), lambda i, e, j: (i, 0)),
                  pl.BlockSpec((1, D_MODEL, tf), lambda i, e, j: (e, 0, j)),
                  pl.BlockSpec((1, D_MODEL, tf), lambda i, e, j: (e, 0, j)),
                  pl.BlockSpec((1, tf, D_MODEL), lambda i, e, j: (e, j, 0))],
        out_specs=pl.BlockSpec((tm, D_MODEL), lambda i, e, j: (i, 0)),
        out_shape=jax.ShapeDtypeStruct((m, D_MODEL), F32),
        scratch_shapes=[pltpu.VMEM((tm, D_MODEL), F32)],
        compiler_params=_cparams(("parallel", "arbitrary", "arbitrary")),
        name="moe_dense",
    )(x2, h, comb, w1.astype(BF16), w3.astype(BF16), w2.astype(BF16))


def _relayout_w_in(w):
    k = w.shape[0]
    z = lambda n: jnp.zeros((k, n), w.dtype)
    cols = [w[:, _R_BRG:_R_END], w[:, _R_AQKV:_R_BETA], w[:, _R_AGATE:_R_B],
            w[:, _R_B:_R_C], w[:, _R_C:_R_BRG], w[:, _R_BETA:_R_AGATE],
            z(NP - OFF_BA - 2 * HEADS)]
    return jnp.concatenate(cols, axis=1).astype(BF16)


def _layer_mixers(x2, b, t, layer, lower_bounds, w_in, norm_mix, conv_a, a_log, dt_bias,
                  gnorm_a, gnorm_b, qnorm_c, knorm_c, rel_bias, w_br_a, w_br_b, w_br_c, w_out):
    proj2 = _norm_proj(x2, norm_mix[layer].astype(F32), _relayout_w_in(w_in[layer]))
    proj3 = proj2.reshape(b, t, NP)
    oa = _deltanet(proj3, conv_a[layer], a_log[layer], dt_bias[layer], gnorm_a[layer])
    ob = _hgrn(proj3, lower_bounds[layer], gnorm_b[layer])
    ocs, lses = [], []
    for gi in range(C_GROUPS):
        o, lse = _dilated_group(proj3, gi, rel_bias, qnorm_c[layer, gi], knorm_c[layer, gi])
        ocs.append(o)
        lses.append(lse)
    return _mix(x2, proj2, oa.reshape(b * t, HW), ob.reshape(b * t, HW), ocs, lses,
                w_br_a[layer], w_br_b[layer], w_br_c[layer], w_out[layer])


def kernel(x, w_in, norm_mix, conv_a, a_log, dt_bias, gnorm_a, lb_logits, gnorm_b, qnorm_c, knorm_c, rel_bias, w_br_a, w_br_b, w_br_c, w_out, norm_ffn, ffn_w1, ffn_w3, ffn_w2, router, moe_w1, moe_w3, moe_w2):
    b, t, _ = x.shape
    depth = w_in.shape[0]
    p_lb = jax.nn.softmax(lb_logits.astype(F32), axis=0)
    lower_bounds = jnp.cumsum(p_lb, axis=0) - p_lb[0:1]
    x2 = x.reshape(b * t, D_MODEL).astype(F32)
    for layer in range(depth):
        x2 = _layer_mixers(x2, b, t, layer, lower_bounds, w_in, norm_mix, conv_a, a_log, dt_bias,
                           gnorm_a, gnorm_b, qnorm_c, knorm_c, rel_bias,
                           w_br_a, w_br_b, w_br_c, w_out)
        li = layer // 2
        if layer % 2 == 0:
            x2 = _ffn(x2, norm_ffn[layer], ffn_w1[li], ffn_w3[li], ffn_w2[li])
        else:
            h, comb = _router(x2, norm_ffn[layer], router[li])
            x2 = _moe_dense(x2, h, comb, moe_w1[li], moe_w3[li], moe_w2[li])
    return x2.reshape(b, t, D_MODEL).astype(x.dtype)
```

```python
import functools
import math

import numpy as np
import jax
import jax.numpy as jnp
from jax import lax
from jax.experimental import pallas as pl
from jax.experimental.pallas import tpu as pltpu

F32 = jnp.float32
BF16 = jnp.bfloat16
HIGHEST = lax.Precision.HIGHEST

D_MODEL = 1024
EPS = 1e-6
HEADS = 4
DH = 128
HW = HEADS * DH
A_CONV = 4
A_CHUNK = 64
B_ROWS = 64
B_BLK = 16
C_PAIRS = ((128, 1), (512, 4), (2048, 16))
C_GROUPS = 3
C_BLOCK = 128
REL_BUCKETS = 32
REL_MAX_DIST = 2048
N_EXPERTS = 8
TOP_K = 2

OFF_BRG = 0
OFF_AQKV = 3072
OFF_AGATE = 4608
OFF_B = 5120
OFF_C = 7168
OFF_BA = 11776
NP = 12288

_R_AQKV, _R_BETA, _R_AGATE, _R_B, _R_C, _R_BRG, _R_END = 0, 1536, 1544, 2056, 4104, 8712, 11784

VMEM_LIMIT = 56 * 1024 * 1024


def _cparams(sem):
    return pltpu.CompilerParams(dimension_semantics=sem, vmem_limit_bytes=VMEM_LIMIT)


def _sigmoid(x):
    return 1.0 / (1.0 + jnp.exp(-x))


def _silu(x):
    return x * _sigmoid(x)


def _softplus(x):
    return jnp.maximum(x, 0.0) + jnp.log(1.0 + jnp.exp(-jnp.abs(x)))


def _dot(a, b):
    return jnp.dot(a, b, preferred_element_type=F32)


def _dot_nt(a, b, precision=None):
    return lax.dot_general(a, b, (((1,), (1,)), ((), ())), precision=precision,
                           preferred_element_type=F32)


def _dot_tn(a, b):
    return lax.dot_general(a, b, (((0,), (0,)), ((), ())), preferred_element_type=F32)


def _norm_proj_kernel(x_ref, g_ref, w_ref, o_ref, h_ref):
    @pl.when(pl.program_id(1) == 0)
    def _():
        x = x_ref[...]
        ms = jnp.mean(x * x, axis=-1, keepdims=True)
        h_ref[...] = (x * lax.rsqrt(ms + EPS) * g_ref[...]).astype(BF16)

    o_ref[...] = _dot(h_ref[...], w_ref[...])


def _norm_proj(x2, gain, w_bf16, tm=1024, tn=1024):
    m = x2.shape[0]
    n = w_bf16.shape[1]
    tm = min(tm, m)
    return pl.pallas_call(
        _norm_proj_kernel,
        grid=(m // tm, n // tn),
        in_specs=[pl.BlockSpec((tm, D_MODEL), lambda i, j: (i, 0)),
                  pl.BlockSpec((1, D_MODEL), lambda i, j: (0, 0)),
                  pl.BlockSpec((D_MODEL, tn), lambda i, j: (0, j))],
        out_specs=pl.BlockSpec((tm, tn), lambda i, j: (i, j)),
        out_shape=jax.ShapeDtypeStruct((m, n), F32),
        scratch_shapes=[pltpu.VMEM((tm, D_MODEL), BF16)],
        compiler_params=_cparams(("parallel", "arbitrary")),
        name="norm_proj",
    )(x2, gain.reshape(1, D_MODEL), w_bf16)


def _split_bf16(a):
    hi = a.astype(BF16)
    lo = (a - hi.astype(F32)).astype(BF16)
    return hi, lo


def _dot3(a, b):
    ah, al = _split_bf16(a)
    bh, bl = _split_bf16(b)
    return _dot(ah, bh) + (_dot(ah, bl) + _dot(al, bh))


def _dot3_nt(a, b):
    ah, al = _split_bf16(a)
    bh, bl = _split_bf16(b)
    return _dot_nt(ah, bh) + (_dot_nt(ah, bl) + _dot_nt(al, bh))


def _dot_exact_lhs(a_bf16, b):
    b0 = b.astype(BF16)
    r1 = b - b0.astype(F32)
    b1 = r1.astype(BF16)
    b2 = (r1 - b1.astype(F32)).astype(BF16)
    return _dot(a_bf16, b0) + (_dot(a_bf16, b1) + _dot(a_bf16, b2))


def _deltanet_kernel(qkv_ref, gate_ref, ba_ref, convw_ref, arow_ref, dtrow_ref, gn_ref,
                     o_ref, s_ref, xe_ref):
    C = A_CHUNK
    HS = range(HEADS)

    @pl.when(pl.program_id(1) == 0)
    def _():
        s_ref[...] = jnp.zeros_like(s_ref)
        xe_ref[0:8, :] = jnp.zeros((8, 3 * HW), F32)

    x = qkv_ref[0]
    xe_ref[8:8 + C, :] = x
    w = convw_ref[...]
    y = (w[3:4] * x + w[2:3] * xe_ref[7:7 + C, :] + w[1:2] * xe_ref[6:6 + C, :]
         + w[0:1] * xe_ref[5:5 + C, :])
    xe_ref[0:8, :] = x[C - 8:C]
    y = _silu(y)

    ba = ba_ref[0]
    beta_all = _sigmoid(ba)
    g_all = arow_ref[...] * _softplus(ba + dtrow_ref[...])
    gate = _silu(gate_ref[0])
    s_old = [s_ref[h] for h in HS]

    row = lax.broadcasted_iota(jnp.int32, (C, C), 0)
    col = lax.broadcasted_iota(jnp.int32, (C, C), 1)
    lmat = (col <= row).astype(BF16)
    rowx = lax.broadcasted_iota(jnp.int32, (C, DH + C), 0)
    colx = lax.broadcasted_iota(jnp.int32, (C, DH + C), 1)
    umask = (colx < DH) | (rowx > colx - DH)
    eye = (row == col).astype(F32)
    bd8 = (row >> 3) == (col >> 3)

    def merge_mask(sh):
        return (((row >> (sh + 1)) == (col >> (sh + 1)))
                & (((row >> sh) & 1) == 1) & (((col >> sh) & 1) == 0))

    q = [y[:, h * DH:(h + 1) * DH] for h in HS]
    k = [y[:, HW + h * DH:HW + (h + 1) * DH] for h in HS]
    v = [y[:, 2 * HW + h * DH:2 * HW + (h + 1) * DH] for h in HS]
    q = [a * (lax.rsqrt(jnp.sum(a * a, axis=-1, keepdims=True) + EPS) * (DH ** -0.5)) for a in q]
    k = [a * lax.rsqrt(jnp.sum(a * a, axis=-1, keepdims=True) + EPS) for a in k]
    beta = [beta_all[:, h:h + 1] for h in HS]
    gb = [jnp.broadcast_to(g_all[:, HEADS + h:HEADS + h + 1], (C, DH + C)) for h in HS]
    dext = [_dot_exact_lhs(lmat, jnp.where(umask, gb[h], 0.0)) for h in HS]
    gc = [d[:, :DH] for d in dext]
    edm = [jnp.exp(d[:, DH:]) for d in dext]
    egc = [jnp.exp(g) for g in gc]
    gl = [g[C - 1:C, :] for g in gc]
    kb = [k[h] * beta[h] for h in HS]
    m = [_dot3_nt(kb[h], k[h]) * jnp.where(row > col, edm[h], 0.0) for h in HS]

    nd = [jnp.where(bd8, -a, 0.0) for a in m]
    p2 = [_dot3(a, a) for a in nd]
    p4 = [_dot3(a, a) for a in p2]
    x1 = [eye + nd[h] for h in HS]
    x1 = [x1[h] + _dot3(x1[h], p2[h]) for h in HS]
    xi = [x1[h] + _dot3(x1[h], p4[h]) for h in HS]
    for sh in (3, 4, 5):
        mm = merge_mask(sh)
        t = [_dot3(xi[h], jnp.where(mm, m[h], 0.0)) for h in HS]
        xi = [xi[h] - _dot3(t[h], xi[h]) for h in HS]

    rhs = [jnp.concatenate([v[h] * beta[h], kb[h] * egc[h]], axis=1) for h in HS]
    sol = [_dot3(xi[h], rhs[h]) for h in HS]
    attn = [_dot_nt(q[h].astype(BF16), k[h].astype(BF16)) * jnp.where(row >= col, edm[h], 0.0)
            for h in HS]
    qg = [(q[h] * egc[h]).astype(BF16) for h in HS]
    kg = [(k[h] * jnp.exp(gl[h] - gc[h])).astype(BF16) for h in HS]
    sb = [a.astype(BF16) for a in s_old]
    v_new = [(sol[h][:, :DH] - _dot(sol[h][:, DH:].astype(BF16), sb[h])).astype(BF16) for h in HS]
    o = [_dot(qg[h], sb[h]) + _dot(attn[h].astype(BF16), v_new[h]) for h in HS]
    s_new = [s_old[h] * jnp.exp(gl[h]) + _dot_tn(kg[h], v_new[h]) for h in HS]
    for h in HS:
        s_ref[h] = s_new[h]
    o = [a * lax.rsqrt(jnp.mean(a * a, axis=-1, keepdims=True) + EPS) * gn_ref[...] for a in o]
    o_ref[0] = (jnp.concatenate(o, axis=1) * gate).astype(o_ref.dtype)


def _deltanet(proj3, conv_w, a_log, dt_bias, gnorm):
    b, t, _ = proj3.shape
    C = A_CHUNK
    pad = jnp.zeros((HEADS,), F32)
    arow = jnp.concatenate([pad, -jnp.exp(a_log.astype(F32)), jnp.zeros((120,), F32)]).reshape(1, 128)
    dtrow = jnp.concatenate([pad, dt_bias.astype(F32), jnp.zeros((120,), F32)]).reshape(1, 128)
    const = lambda shape: pl.BlockSpec(shape, lambda i, c: (0,) * len(shape))
    return pl.pallas_call(
        _deltanet_kernel,
        grid=(b, t // C),
        in_specs=[pl.BlockSpec((1, C, 3 * HW), lambda i, c: (i, c, OFF_AQKV // (3 * HW))),
                  pl.BlockSpec((1, C, HW), lambda i, c: (i, c, OFF_AGATE // HW)),
                  pl.BlockSpec((1, C, 128), lambda i, c: (i, c, OFF_BA // 128)),
                  const((A_CONV, 3 * HW)), const((1, 128)), const((1, 128)), const((1, DH))],
        out_specs=pl.BlockSpec((1, C, HW), lambda i, c: (i, c, 0)),
        out_shape=jax.ShapeDtypeStruct((b, t, HW), BF16),
        scratch_shapes=[pltpu.VMEM((HEADS, DH, DH), F32), pltpu.VMEM((C + 8, 3 * HW), F32)],
        compiler_params=_cparams(("parallel", "arbitrary")),
        name="deltanet",
    )(proj3, proj3, proj3, conv_w.astype(F32), arow, dtrow, gnorm.reshape(1, DH).astype(F32))


def _hgrn_kernel(q_ref, f_ref, i_ref, g_ref, lb_ref, gn_ref, o_ref, st_ref):
    R, K = B_ROWS, B_BLK
    HS = range(HEADS)

    @pl.when(pl.program_id(1) == 0)
    def _():
        st_ref[...] = jnp.zeros_like(st_ref)

    row = lax.broadcasted_iota(jnp.int32, (K, K), 0)
    col = lax.broadcasted_iota(jnp.int32, (K, K), 1)
    lmat = (col <= row).astype(BF16)
    rowv = lax.broadcasted_iota(jnp.int32, (K, DH), 0)
    sls = [slice(h * DH, (h + 1) * DH) for h in HS]
    lb = [lb_ref[:, sl] for sl in sls]
    st = [st_ref[h] for h in HS]

    for blk in range(R // K):
        rs = slice(blk * K, (blk + 1) * K)
        q = [_silu(q_ref[0, rs, sl]) for sl in sls]
        f = [lb[h] + (1.0 - lb[h]) * _sigmoid(f_ref[0, rs, sls[h]]) for h in HS]
        k = [1.0 - a for a in f]
        v = [i_ref[0, rs, sl] for sl in sls]
        gc = [_dot_exact_lhs(lmat, jnp.log(a)) for a in f]
        o = [_dot_nt((q[h] * jnp.exp(gc[h])).astype(BF16), st[h].astype(BF16)) for h in HS]
        for j in range(K):
            for h in HS:
                rel = jnp.exp(jnp.where(rowv >= j, gc[h] - gc[h][j:j + 1], -1e30))
                sj = jnp.sum(q[h] * k[h][j:j + 1] * rel, axis=-1, keepdims=True)
                o[h] = o[h] + sj * v[h][j:j + 1]
        gl = [a[K - 1:K] for a in gc]
        kg = [(k[h] * jnp.exp(gl[h] - gc[h])).astype(BF16) for h in HS]
        st = [st[h] * jnp.exp(gl[h]) + _dot_tn(v[h].astype(BF16), kg[h]) for h in HS]
        o = [a * lax.rsqrt(jnp.mean(a * a, axis=-1, keepdims=True) + EPS) * gn_ref[...] for a in o]
        o_ref[0, rs, :] = (jnp.concatenate(o, axis=1) * _sigmoid(g_ref[0, rs, :])).astype(o_ref.dtype)
    for h in HS:
        st_ref[h] = st[h]


def _hgrn(proj3, lb, gnorm):
    b, t, _ = proj3.shape
    R = B_ROWS
    seg = lambda s: pl.BlockSpec((1, R, HW), lambda i, c: (i, c, OFF_B // HW + s))
    const = lambda shape: pl.BlockSpec(shape, lambda i, c: (0,) * len(shape))
    return pl.pallas_call(
        _hgrn_kernel,
        grid=(b, t // R),
        in_specs=[seg(0), seg(1), seg(2), seg(3), const((1, HW)), const((1, DH))],
        out_specs=pl.BlockSpec((1, R, HW), lambda i, c: (i, c, 0)),
        out_shape=jax.ShapeDtypeStruct((b, t, HW), BF16),
        scratch_shapes=[pltpu.VMEM((HEADS, DH, DH), F32)],
        compiler_params=_cparams(("parallel", "arbitrary")),
        name="hgrn2",
    )(proj3, proj3, proj3, proj3, lb.reshape(1, HW).astype(F32), gnorm.reshape(1, DH).astype(F32))


def _t5_bucket_np(n):
    max_exact = REL_BUCKETS // 2
    nf = np.maximum(n, 1).astype(np.float32)
    large = max_exact + (np.log(nf / max_exact) / math.log(REL_MAX_DIST / max_exact)
                         * (REL_BUCKETS - max_exact)).astype(np.int32)
    large = np.minimum(large, REL_BUCKETS - 1)
    return np.where(n < max_exact, n, large)


def _dilated_kernel(span, dil, hpb, q_ref, kp_ref, kc_ref, vp_ref, vc_ref, bias_ref, qg_ref, kg_ref,
                    o_ref, lse_ref):
    CB = C_BLOCK
    n = pl.program_id(1)
    qi = lax.broadcasted_iota(jnp.int32, (CB, 2 * CB), 0)
    kj = lax.broadcasted_iota(jnp.int32, (CB, 2 * CB), 1)
    dist = qi + CB - kj
    valid = (dist >= 0) & (dist <= span) & ((kj >= CB) | (n > 0))

    def qk_norm(x, g):
        return x * lax.rsqrt(jnp.mean(x * x, axis=-1, keepdims=True) + EPS) * g

    def residue(r):
        rows = pl.ds(r, CB, stride=dil) if dil > 1 else pl.ds(0, CB)
        for h in range(hpb):
            sl = slice(h * DH, (h + 1) * DH)
            q = qk_norm(q_ref[0, rows, sl], qg_ref[...]).astype(BF16)
            k = jnp.concatenate([kp_ref[0, rows, sl], kc_ref[0, rows, sl]], axis=0)
            k = qk_norm(k, kg_ref[...]).astype(BF16)
            v = jnp.concatenate([vp_ref[0, rows, sl], vc_ref[0, rows, sl]], axis=0).astype(BF16)
            s = _dot_nt(q, k) * (DH ** -0.5) + bias_ref[h]
            s = jnp.where(valid, s, -1e30)
            mx = jnp.max(s, axis=-1, keepdims=True)
            p = jnp.exp(s - mx)
            den = jnp.sum(p, axis=-1, keepdims=True)
            o_ref[0, rows, sl] = _dot(p.astype(BF16), v) / den
            lse_ref[0, rows, sl] = jnp.broadcast_to(mx + jnp.log(den), (CB, DH))

    if dil == 1:
        residue(0)
    else:
        def body(r, carry):
            residue(r)
            return carry
        lax.fori_loop(0, dil, body, 0)


def _dilated_group(proj3, gi, rel_bias, q_gain, k_gain):
    b, t, _ = proj3.shape
    window, dil = C_PAIRS[gi]
    span = window // dil
    CB = C_BLOCK
    rb = CB * dil
    hpb = HEADS if dil == 1 else 1
    bw = hpb * DH
    qi = np.arange(CB)[:, None]
    kj = np.arange(2 * CB)[None, :]
    bucket = _t5_bucket_np(np.maximum(qi + CB - kj, 0) * dil)
    onehot = jnp.asarray(np.eye(REL_BUCKETS, dtype=np.float32)[bucket])
    bias = jnp.einsum("qkb,bh->hqk", onehot, rel_bias[:, gi * HEADS:(gi + 1) * HEADS].astype(F32),
                      precision=HIGHEST)

    def seg(which, prev):
        base = (OFF_C + which * C_GROUPS * HW + gi * HW) // bw
        if prev:
            return pl.BlockSpec((1, rb, bw), lambda i, n, h: (i, jnp.maximum(n - 1, 0), base + h))
        return pl.BlockSpec((1, rb, bw), lambda i, n, h: (i, n, base + h))

    const = lambda shape: pl.BlockSpec(shape, lambda i, n, h: (0,) * len(shape))
    o, lse = pl.pallas_call(
        functools.partial(_dilated_kernel, span, dil, hpb),
        grid=(b, t // rb, HEADS // hpb),
        in_specs=[seg(0, False), seg(1, True), seg(1, False), seg(2, True), seg(2, False),
                  pl.BlockSpec((hpb, CB, 2 * CB), lambda i, n, h: (h, 0, 0)),
                  const((1, DH)), const((1, DH))],
        out_specs=[pl.BlockSpec((1, rb, bw), lambda i, n, h: (i, n, h)),
                   pl.BlockSpec((1, rb, bw), lambda i, n, h: (i, n, h))],
        out_shape=[jax.ShapeDtypeStruct((b, t, HW), F32),
                   jax.ShapeDtypeStruct((b, t, HW), F32)],
        compiler_params=_cparams(("parallel", "arbitrary", "arbitrary")),
        name=f"dilated_g{gi}",
    )(proj3, proj3, proj3, proj3, proj3, bias,
      q_gain.reshape(1, DH).astype(F32), k_gain.reshape(1, DH).astype(F32))
    return o.reshape(b * t, HW), lse.reshape(b * t, HW)


def _mix_kernel(x_ref, gate_ref, oa_ref, ob_ref, oc0_ref, oc1_ref, oc2_ref,
                l0_ref, l1_ref, l2_ref, wa_ref, wb_ref, wc_ref, wo_ref, out_ref):
    l0, l1, l2 = l0_ref[...], l1_ref[...], l2_ref[...]
    mx = jnp.maximum(jnp.maximum(l0, l1), l2)
    e0, e1, e2 = jnp.exp(l0 - mx), jnp.exp(l1 - mx), jnp.exp(l2 - mx)
    oc = (e0 * oc0_ref[...] + e1 * oc1_ref[...] + e2 * oc2_ref[...]) / (e0 + e1 + e2)
    mix = (_sigmoid(gate_ref[:, 0:D_MODEL]) * _dot(oa_ref[...], wa_ref[...])
           + _sigmoid(gate_ref[:, D_MODEL:2 * D_MODEL]) * _dot(ob_ref[...], wb_ref[...])
           + _sigmoid(gate_ref[:, 2 * D_MODEL:3 * D_MODEL]) * _dot(oc.astype(BF16), wc_ref[...]))
    out_ref[...] = x_ref[...] + _dot(mix.astype(BF16), wo_ref[...])


def _mix(x2, proj2, oa, ob, ocs, lses, wa, wb, wc, wo, tm=256):
    m = x2.shape[0]
    tm = min(tm, m)
    rowblk = lambda w: pl.BlockSpec((tm, w), lambda i: (i, 0))
    const = lambda shape: pl.BlockSpec(shape, lambda i: (0,) * len(shape))
    return pl.pallas_call(
        _mix_kernel,
        grid=(m // tm,),
        in_specs=[rowblk(D_MODEL), rowblk(3 * D_MODEL), rowblk(HW), rowblk(HW),
                  rowblk(HW), rowblk(HW), rowblk(HW), rowblk(HW), rowblk(HW), rowblk(HW),
                  const((HW, D_MODEL)), const((HW, D_MODEL)), const((HW, D_MODEL)),
                  const((D_MODEL, D_MODEL))],
        out_specs=rowblk(D_MODEL),
        out_shape=jax.ShapeDtypeStruct((m, D_MODEL), F32),
        compiler_params=_cparams(("parallel",)),
        name="branch_mix",
    )(x2, proj2, oa, ob, ocs[0], ocs[1], ocs[2], lses[0], lses[1], lses[2],
      wa.astype(BF16), wb.astype(BF16), wc.astype(BF16), wo.astype(BF16))


def _ffn_kernel(x_ref, g_ref, w1_ref, w3_ref, w2_ref, o_ref, h_ref, acc_ref):
    j = pl.program_id(1)

    @pl.when(j == 0)
    def _():
        x = x_ref[...]
        ms = jnp.mean(x * x, axis=-1, keepdims=True)
        h_ref[...] = (x * lax.rsqrt(ms + EPS) * g_ref[...]).astype(BF16)
        acc_ref[...] = jnp.zeros_like(acc_ref)

    h = h_ref[...]
    a = _dot(h, w1_ref[...])
    b = _dot(h, w3_ref[...])
    acc_ref[...] += _dot((_silu(a) * b).astype(BF16), w2_ref[...])

    @pl.when(j == pl.num_programs(1) - 1)
    def _():
        o_ref[...] = x_ref[...] + acc_ref[...]


def _ffn(x2, gain, w1, w3, w2, tm=512, tf=1408):
    m = x2.shape[0]
    ff = w1.shape[1]
    tm = min(tm, m)
    return pl.pallas_call(
        _ffn_kernel,
        grid=(m // tm, ff // tf),
        in_specs=[pl.BlockSpec((tm, D_MODEL), lambda i, j: (i, 0)),
                  pl.BlockSpec((1, D_MODEL), lambda i, j: (0, 0)),
                  pl.BlockSpec((D_MODEL, tf), lambda i, j: (0, j)),
                  pl.BlockSpec((D_MODEL, tf), lambda i, j: (0, j)),
                  pl.BlockSpec((tf, D_MODEL), lambda i, j: (j, 0))],
        out_specs=pl.BlockSpec((tm, D_MODEL), lambda i, j: (i, 0)),
        out_shape=jax.ShapeDtypeStruct((m, D_MODEL), F32),
        scratch_shapes=[pltpu.VMEM((tm, D_MODEL), BF16), pltpu.VMEM((tm, D_MODEL), F32)],
        compiler_params=_cparams(("parallel", "arbitrary")),
        name="ffn",
    )(x2, gain.reshape(1, D_MODEL).astype(F32), w1.astype(BF16), w3.astype(BF16), w2.astype(BF16))


def _router_kernel(x_ref, g_ref, wr_ref, h_ref, ids_ref, ps_ref):
    x = x_ref[...]
    ms = jnp.mean(x * x, axis=-1, keepdims=True)
    h = x * lax.rsqrt(ms + EPS) * g_ref[...]
    h_ref[...] = h
    logits = jnp.dot(h, wr_ref[...], precision=HIGHEST, preferred_element_type=F32)
    lane = lax.broadcasted_iota(jnp.int32, logits.shape, 1)
    neg = jnp.float32(-jnp.inf)
    l1 = jnp.where(lane < N_EXPERTS, logits, neg)
    m1 = jnp.max(l1, axis=-1, keepdims=True)
    i1 = jnp.min(jnp.where(l1 == m1, lane, 128), axis=-1, keepdims=True)
    l2 = jnp.where(lane == i1, neg, l1)
    m2 = jnp.max(l2, axis=-1, keepdims=True)
    i2 = jnp.min(jnp.where(l2 == m2, lane, 128), axis=-1, keepdims=True)
    e = jnp.exp(m2 - m1)
    p1 = 1.0 / (1.0 + e)
    p2 = e / (1.0 + e)
    ids_ref[...] = jnp.where(lane == 0, i1, jnp.where(lane == 1, i2, 0))
    ps_ref[...] = jnp.where(lane == 0, p1, jnp.where(lane == 1, p2, 0.0))


def _router(x2, gain, w_router, tm=512):
    m = x2.shape[0]
    tm = min(tm, m)
    wr = jnp.pad(w_router.astype(F32), ((0, 0), (0, 128 - N_EXPERTS)))
    return pl.pallas_call(
        _router_kernel,
        grid=(m // tm,),
        in_specs=[pl.BlockSpec((tm, D_MODEL), lambda i: (i, 0)),
                  pl.BlockSpec((1, D_MODEL), lambda i: (0, 0)),
                  pl.BlockSpec((D_MODEL, 128), lambda i: (0, 0))],
        out_specs=[pl.BlockSpec((tm, D_MODEL), lambda i: (i, 0)),
                   pl.BlockSpec((tm, 128), lambda i: (i, 0)),
                   pl.BlockSpec((tm, 128), lambda i: (i, 0))],
        out_shape=[jax.ShapeDtypeStruct((m, D_MODEL), F32),
                   jax.ShapeDtypeStruct((m, 128), jnp.int32),
                   jax.ShapeDtypeStruct((m, 128), F32)],
        compiler_params=_cparams(("parallel",)),
        name="router",
    )(x2, gain.reshape(1, D_MODEL).astype(F32), wr)


def _route_positions(ids, tm):
    m = ids.shape[0]
    e_flat = ids[:, :TOP_K].reshape(-1)
    onehot = (e_flat[:, None] == jnp.arange(N_EXPERTS)[None, :]).astype(jnp.int32)
    csum = jnp.cumsum(onehot, axis=0)
    counts = csum[-1]
    gsz = ((counts + tm - 1) // tm) * tm
    gend = jnp.cumsum(gsz)
    pos = jnp.sum(onehot * (gend - gsz + csum - 1), axis=1)
    nt = (TOP_K * m + N_EXPERTS * tm) // tm
    n_used = gend[-1] // tm
    tile_e = jnp.sum((jnp.arange(nt)[:, None] * tm >= gend[None, :]).astype(jnp.int32), axis=1)
    last_e = jnp.sum(((n_used - 1) * tm >= gend).astype(jnp.int32))
    tile_e = jnp.minimum(tile_e, last_e)
    return pos.astype(jnp.int32), tile_e.astype(jnp.int32), n_used.reshape(1).astype(jnp.int32), nt


def _dispatch_kernel(pos_ref, h_ref, init_ref, xs_ref, sem):
    del init_ref
    td = h_ref.shape[0]

    def row_copy(t, s):
        return pltpu.make_async_copy(h_ref.at[pl.ds(t, 1)],
                                     xs_ref.at[pl.ds(pos_ref[0, 0, TOP_K * t + s], 1)], sem)

    def start(t, c):
        for s in range(TOP_K):
            row_copy(t, s).start()
        return c

    def wait(t, c):
        for s in range(TOP_K):
            row_copy(t, s).wait()
        return c

    lax.fori_loop(0, td, start, 0)
    lax.fori_loop(0, td, wait, 0)


def _dispatch(h, pos, npad, td=256):
    m = h.shape[0]
    td = min(td, m)
    return pl.pallas_call(
        _dispatch_kernel,
        grid=(m // td,),
        in_specs=[pl.BlockSpec((1, 1, TOP_K * td), lambda i: (i, 0, 0), memory_space=pltpu.SMEM),
                  pl.BlockSpec((td, D_MODEL), lambda i: (i, 0)),
                  pl.BlockSpec(memory_space=pl.ANY)],
        out_specs=pl.BlockSpec(memory_space=pl.ANY),
        out_shape=jax.ShapeDtypeStruct((npad, D_MODEL), F32),
        scratch_shapes=[pltpu.SemaphoreType.DMA(())],
        input_output_aliases={2: 0},
        compiler_params=_cparams(("arbitrary",)),
        name="moe_dispatch",
    )(pos.reshape(m // td, 1, TOP_K * td), h, jnp.zeros((npad, D_MODEL), F32))


def _experts_kernel(te_ref, nu_ref, xs_ref, w1_ref, w3_ref, w2_ref, y_ref, xb_ref, acc_ref):
    del te_ref
    i = pl.program_id(0)
    j = pl.program_id(1)

    @pl.when(i < nu_ref[0])
    def _():
        @pl.when(j == 0)
        def _():
            xb_ref[...] = xs_ref[...].astype(BF16)
            acc_ref[...] = jnp.zeros_like(acc_ref)

        xb = xb_ref[...]
        a = _dot(xb, w1_ref[0])
        b = _dot(xb, w3_ref[0])
        acc_ref[...] += _dot((_silu(a) * b).astype(BF16), w2_ref[0])

        @pl.when(j == pl.num_programs(1) - 1)
        def _():
            y_ref[...] = acc_ref[...]

    @pl.when((i >= nu_ref[0]) & (j == 0))
    def _():
        y_ref[...] = jnp.zeros_like(y_ref)


def _experts(xs, tile_e, n_used, w1, w3, w2, tm, tf=896):
    npad = xs.shape[0]
    ff = w1.shape[2]
    nj = ff // tf
    row = lambda i, j, te, nu: (jnp.minimum(i, nu[0] - 1), 0)
    jj = lambda i, j, nu: jnp.where(i < nu[0], j, nj - 1)
    return pl.pallas_call(
        _experts_kernel,
        grid_spec=pltpu.PrefetchScalarGridSpec(
            num_scalar_prefetch=2,
            grid=(npad // tm, nj),
            in_specs=[pl.BlockSpec((tm, D_MODEL), row),
                      pl.BlockSpec((1, D_MODEL, tf), lambda i, j, te, nu: (te[i], 0, jj(i, j, nu))),
                      pl.BlockSpec((1, D_MODEL, tf), lambda i, j, te, nu: (te[i], 0, jj(i, j, nu))),
                      pl.BlockSpec((1, tf, D_MODEL), lambda i, j, te, nu: (te[i], jj(i, j, nu), 0))],
            out_specs=pl.BlockSpec((tm, D_MODEL), lambda i, j, te, nu: (i, 0)),
            scratch_shapes=[pltpu.VMEM((tm, D_MODEL), BF16), pltpu.VMEM((tm, D_MODEL), F32)]),
        out_shape=jax.ShapeDtypeStruct((npad, D_MODEL), F32),
        compiler_params=_cparams(("arbitrary", "arbitrary")),
        name="moe_experts",
    )(tile_e, n_used, xs, w1.astype(BF16), w3.astype(BF16), w2.astype(BF16))


def _combine_kernel(pos_ref, x_ref, ps_ref, y_ref, o_ref, buf_ref, sem):
    td = x_ref.shape[0]

    def row_copy(t, s):
        return pltpu.make_async_copy(y_ref.at[pl.ds(pos_ref[0, 0, TOP_K * t + s], 1)],
                                     buf_ref.at[s, pl.ds(t, 1)], sem.at[s])

    def start(t, c):
        for s in range(TOP_K):
            row_copy(t, s).start()
        return c

    def wait(t, c):
        for s in range(TOP_K):
            row_copy(t, s).wait()
        return c

    lax.fori_loop(0, td, start, 0)
    lax.fori_loop(0, td, wait, 0)
    ps = ps_ref[...]
    o_ref[...] = x_ref[...] + ps[:, 0:1] * buf_ref[0] + ps[:, 1:2] * buf_ref[1]


def _combine(x2, ps, pos, y, td=256):
    m = x2.shape[0]
    td = min(td, m)
    return pl.pallas_call(
        _combine_kernel,
        grid=(m // td,),
        in_specs=[pl.BlockSpec((1, 1, TOP_K * td), lambda i: (i, 0, 0), memory_space=pltpu.SMEM),
                  pl.BlockSpec((td, D_MODEL), lambda i: (i, 0)),
                  pl.BlockSpec((td, 128), lambda i: (i, 0)),
                  pl.BlockSpec(memory_space=pl.ANY)],
        out_specs=pl.BlockSpec((td, D_MODEL), lambda i: (i, 0)),
        out_shape=jax.ShapeDtypeStruct((m, D_MODEL), F32),
        scratch_shapes=[pltpu.VMEM((TOP_K, td, D_MODEL), F32), pltpu.SemaphoreType.DMA((TOP_K,))],
        compiler_params=_cparams(("arbitrary",)),
        name="moe_combine",
    )(pos.reshape(m // td, 1, TOP_K * td), x2, ps, y)


def _moe(x2, gain, w_router, w1, w3, w2, tm=512):
    m = x2.shape[0]
    tm = min(tm, m)
    h, ids, ps = _router(x2, gain, w_router)
    pos, tile_e, n_used, nt = _route_positions(ids, tm)
    xs = _dispatch(h, pos, nt * tm)
    y = _experts(xs, tile_e, n_used, w1, w3, w2, tm)
    return _combine(x2, ps, pos, y)


def _relayout_w_in(w):
    k = w.shape[0]
    z = lambda n: jnp.zeros((k, n), w.dtype)
    cols = [w[:, _R_BRG:_R_END], w[:, _R_AQKV:_R_BETA], w[:, _R_AGATE:_R_B],
            w[:, _R_B:_R_C], w[:, _R_C:_R_BRG], w[:, _R_BETA:_R_AGATE],
            z(NP - OFF_BA - 2 * HEADS)]
    return jnp.concatenate(cols, axis=1).astype(BF16)


def _layer_mixers(x2, b, t, layer, lower_bounds, w_in, norm_mix, conv_a, a_log, dt_bias,
                  gnorm_a, gnorm_b, qnorm_c, knorm_c, rel_bias, w_br_a, w_br_b, w_br_c, w_out):
    proj2 = _norm_proj(x2, norm_mix[layer].astype(F32), _relayout_w_in(w_in[layer]))
    proj3 = proj2.reshape(b, t, NP)
    oa = _deltanet(proj3, conv_a[layer], a_log[layer], dt_bias[layer], gnorm_a[layer])
    ob = _hgrn(proj3, lower_bounds[layer], gnorm_b[layer])
    ocs, lses = [], []
    for gi in range(C_GROUPS):
        o, lse = _dilated_group(proj3, gi, rel_bias, qnorm_c[layer, gi], knorm_c[layer, gi])
        ocs.append(o)
        lses.append(lse)
    return _mix(x2, proj2, oa.reshape(b * t, HW), ob.reshape(b * t, HW), ocs, lses,
                w_br_a[layer], w_br_b[layer], w_br_c[layer], w_out[layer])


def kernel(x, w_in, norm_mix, conv_a, a_log, dt_bias, gnorm_a, lb_logits, gnorm_b, qnorm_c, knorm_c, rel_bias, w_br_a, w_br_b, w_br_c, w_out, norm_ffn, ffn_w1, ffn_w3, ffn_w2, router, moe_w1, moe_w3, moe_w2):
    b, t, _ = x.shape
    depth = w_in.shape[0]
    p_lb = jax.nn.softmax(lb_logits.astype(F32), axis=0)
    lower_bounds = jnp.cumsum(p_lb, axis=0) - p_lb[0:1]
    x2 = x.reshape(b * t, D_MODEL).astype(F32)
    for layer in range(depth):
        x2 = _layer_mixers(x2, b, t, layer, lower_bounds, w_in, norm_mix, conv_a, a_log, dt_bias,
                           gnorm_a, gnorm_b, qnorm_c, knorm_c, rel_bias,
                           w_br_a, w_br_b, w_br_c, w_out)
        li = layer // 2
        if layer % 2 == 0:
            x2 = _ffn(x2, norm_ffn[layer], ffn_w1[li], ffn_w3[li], ffn_w2[li])
        else:
            x2 = _moe(x2, norm_ffn[layer], router[li], moe_w1[li], moe_w3[li], moe_w2[li])
    return x2.reshape(b, t, D_MODEL).astype(x.dtype)
```

```python
import functools
import math

import numpy as np
import jax
import jax.numpy as jnp
from jax import lax
from jax.experimental import pallas as pl
from jax.experimental.pallas import tpu as pltpu

F32 = jnp.float32
BF16 = jnp.bfloat16
HIGHEST = lax.Precision.HIGHEST

D_MODEL = 1024
EPS = 1e-6
HEADS = 4
DH = 128
HW = HEADS * DH
A_CONV = 4
A_CHUNK = 64
B_ROWS = 64
B_BLK = 16
C_PAIRS = ((128, 1), (512, 4), (2048, 16))
C_GROUPS = 3
C_BLOCK = 128
C_TILES = 4
REL_BUCKETS = 32
REL_MAX_DIST = 2048
N_EXPERTS = 8
TOP_K = 2

OFF_BRG = 0
OFF_AQKV = 3072
OFF_AGATE = 4608
OFF_B = 5120
OFF_C = 7168
OFF_BA = 11776
NP = 12288

_R_AQKV, _R_BETA, _R_AGATE, _R_B, _R_C, _R_BRG, _R_END = 0, 1536, 1544, 2056, 4104, 8712, 11784

VMEM_LIMIT = 56 * 1024 * 1024


def _cparams(sem):
    return pltpu.CompilerParams(dimension_semantics=sem, vmem_limit_bytes=VMEM_LIMIT)


def _sigmoid(x):
    return 1.0 / (1.0 + jnp.exp(-x))


def _silu(x):
    return x * _sigmoid(x)


def _softplus(x):
    return jnp.maximum(x, 0.0) + jnp.log(1.0 + jnp.exp(-jnp.abs(x)))


def _dot(a, b):
    return jnp.dot(a, b, preferred_element_type=F32)


def _dot_nt(a, b, precision=None):
    return lax.dot_general(a, b, (((1,), (1,)), ((), ())), precision=precision,
                           preferred_element_type=F32)


def _dot_tn(a, b):
    return lax.dot_general(a, b, (((0,), (0,)), ((), ())), preferred_element_type=F32)


def _norm_proj_kernel(x_ref, g_ref, w_ref, o_ref, h_ref):
    @pl.when(pl.program_id(1) == 0)
    def _():
        x = x_ref[...]
        ms = jnp.mean(x * x, axis=-1, keepdims=True)
        h_ref[...] = (x * lax.rsqrt(ms + EPS) * g_ref[...]).astype(BF16)

    o_ref[...] = _dot(h_ref[...], w_ref[...])


def _norm_proj(x2, gain, w_bf16, tm=1024, tn=1024):
    m = x2.shape[0]
    n = w_bf16.shape[1]
    tm = min(tm, m)
    return pl.pallas_call(
        _norm_proj_kernel,
        grid=(m // tm, n // tn),
        in_specs=[pl.BlockSpec((tm, D_MODEL), lambda i, j: (i, 0)),
                  pl.BlockSpec((1, D_MODEL), lambda i, j: (0, 0)),
                  pl.BlockSpec((D_MODEL, tn), lambda i, j: (0, j))],
        out_specs=pl.BlockSpec((tm, tn), lambda i, j: (i, j)),
        out_shape=jax.ShapeDtypeStruct((m, n), F32),
        scratch_shapes=[pltpu.VMEM((tm, D_MODEL), BF16)],
        compiler_params=_cparams(("parallel", "arbitrary")),
        name="norm_proj",
    )(x2, gain.reshape(1, D_MODEL), w_bf16)


def _split_bf16(a):
    hi = a.astype(BF16)
    lo = (a - hi.astype(F32)).astype(BF16)
    return hi, lo


def _dot3(a, b):
    ah, al = _split_bf16(a)
    bh, bl = _split_bf16(b)
    return _dot(ah, bh) + (_dot(ah, bl) + _dot(al, bh))


def _dot3_nt(a, b):
    ah, al = _split_bf16(a)
    bh, bl = _split_bf16(b)
    return _dot_nt(ah, bh) + (_dot_nt(ah, bl) + _dot_nt(al, bh))


def _dot_exact_lhs(a_bf16, b):
    b0 = b.astype(BF16)
    r1 = b - b0.astype(F32)
    b1 = r1.astype(BF16)
    b2 = (r1 - b1.astype(F32)).astype(BF16)
    return _dot(a_bf16, b0) + (_dot(a_bf16, b1) + _dot(a_bf16, b2))


def _deltanet_kernel(qkv_ref, gate_ref, ba_ref, convw_ref, arow_ref, dtrow_ref, gn_ref,
                     o_ref, s_ref, xe_ref):
    C = A_CHUNK
    HS = range(HEADS)

    @pl.when(pl.program_id(1) == 0)
    def _():
        s_ref[...] = jnp.zeros_like(s_ref)
        xe_ref[0:8, :] = jnp.zeros((8, 3 * HW), F32)

    x = qkv_ref[0]
    xe_ref[8:8 + C, :] = x
    w = convw_ref[...]
    y = (w[3:4] * x + w[2:3] * xe_ref[7:7 + C, :] + w[1:2] * xe_ref[6:6 + C, :]
         + w[0:1] * xe_ref[5:5 + C, :])
    xe_ref[0:8, :] = x[C - 8:C]
    y = _silu(y)

    ba = ba_ref[0]
    beta_all = _sigmoid(ba)
    g_all = arow_ref[...] * _softplus(ba + dtrow_ref[...])
    gate = _silu(gate_ref[0])
    s_old = [s_ref[h] for h in HS]

    row = lax.broadcasted_iota(jnp.int32, (C, C), 0)
    col = lax.broadcasted_iota(jnp.int32, (C, C), 1)
    lmat = (col <= row).astype(BF16)
    rowx = lax.broadcasted_iota(jnp.int32, (C, DH + C), 0)
    colx = lax.broadcasted_iota(jnp.int32, (C, DH + C), 1)
    umask = (colx < DH) | (rowx > colx - DH)
    eye = (row == col).astype(F32)
    bd8 = (row >> 3) == (col >> 3)

    def merge_mask(sh):
        return (((row >> (sh + 1)) == (col >> (sh + 1)))
                & (((row >> sh) & 1) == 1) & (((col >> sh) & 1) == 0))

    q = [y[:, h * DH:(h + 1) * DH] for h in HS]
    k = [y[:, HW + h * DH:HW + (h + 1) * DH] for h in HS]
    v = [y[:, 2 * HW + h * DH:2 * HW + (h + 1) * DH] for h in HS]
    q = [a * (lax.rsqrt(jnp.sum(a * a, axis=-1, keepdims=True) + EPS) * (DH ** -0.5)) for a in q]
    k = [a * lax.rsqrt(jnp.sum(a * a, axis=-1, keepdims=True) + EPS) for a in k]
    beta = [beta_all[:, h:h + 1] for h in HS]
    gb = [jnp.broadcast_to(g_all[:, HEADS + h:HEADS + h + 1], (C, DH + C)) for h in HS]
    dext = [_dot_exact_lhs(lmat, jnp.where(umask, gb[h], 0.0)) for h in HS]
    gc = [d[:, :DH] for d in dext]
    edm = [jnp.exp(d[:, DH:]) for d in dext]
    egc = [jnp.exp(g) for g in gc]
    gl = [g[C - 1:C, :] for g in gc]
    kb = [k[h] * beta[h] for h in HS]
    m = [_dot3_nt(kb[h], k[h]) * jnp.where(row > col, edm[h], 0.0) for h in HS]

    nd = [jnp.where(bd8, -a, 0.0) for a in m]
    p2 = [_dot3(a, a) for a in nd]
    p4 = [_dot3(a, a) for a in p2]
    x1 = [eye + nd[h] for h in HS]
    x1 = [x1[h] + _dot3(x1[h], p2[h]) for h in HS]
    xi = [x1[h] + _dot3(x1[h], p4[h]) for h in HS]
    for sh in (3, 4, 5):
        mm = merge_mask(sh)
        t = [_dot3(xi[h], jnp.where(mm, m[h], 0.0)) for h in HS]
        xi = [xi[h] - _dot3(t[h], xi[h]) for h in HS]

    rhs = [jnp.concatenate([v[h] * beta[h], kb[h] * egc[h]], axis=1) for h in HS]
    sol = [_dot3(xi[h], rhs[h]) for h in HS]
    attn = [_dot_nt(q[h].astype(BF16), k[h].astype(BF16)) * jnp.where(row >= col, edm[h], 0.0)
            for h in HS]
    qg = [(q[h] * egc[h]).astype(BF16) for h in HS]
    kg = [(k[h] * jnp.exp(gl[h] - gc[h])).astype(BF16) for h in HS]
    sb = [a.astype(BF16) for a in s_old]
    v_new = [(sol[h][:, :DH] - _dot(sol[h][:, DH:].astype(BF16), sb[h])).astype(BF16) for h in HS]
    o = [_dot(qg[h], sb[h]) + _dot(attn[h].astype(BF16), v_new[h]) for h in HS]
    s_new = [s_old[h] * jnp.exp(gl[h]) + _dot_tn(kg[h], v_new[h]) for h in HS]
    for h in HS:
        s_ref[h] = s_new[h]
    o = [a * lax.rsqrt(jnp.mean(a * a, axis=-1, keepdims=True) + EPS) * gn_ref[...] for a in o]
    o_ref[0] = (jnp.concatenate(o, axis=1) * gate).astype(o_ref.dtype)


def _deltanet(proj3, conv_w, a_log, dt_bias, gnorm):
    b, t, _ = proj3.shape
    C = A_CHUNK
    pad = jnp.zeros((HEADS,), F32)
    arow = jnp.concatenate([pad, -jnp.exp(a_log.astype(F32)), jnp.zeros((120,), F32)]).reshape(1, 128)
    dtrow = jnp.concatenate([pad, dt_bias.astype(F32), jnp.zeros((120,), F32)]).reshape(1, 128)
    const = lambda shape: pl.BlockSpec(shape, lambda i, c: (0,) * len(shape))
    return pl.pallas_call(
        _deltanet_kernel,
        grid=(b, t // C),
        in_specs=[pl.BlockSpec((1, C, 3 * HW), lambda i, c: (i, c, OFF_AQKV // (3 * HW))),
                  pl.BlockSpec((1, C, HW), lambda i, c: (i, c, OFF_AGATE // HW)),
                  pl.BlockSpec((1, C, 128), lambda i, c: (i, c, OFF_BA // 128)),
                  const((A_CONV, 3 * HW)), const((1, 128)), const((1, 128)), const((1, DH))],
        out_specs=pl.BlockSpec((1, C, HW), lambda i, c: (i, c, 0)),
        out_shape=jax.ShapeDtypeStruct((b, t, HW), BF16),
        scratch_shapes=[pltpu.VMEM((HEADS, DH, DH), F32), pltpu.VMEM((C + 8, 3 * HW), F32)],
        compiler_params=_cparams(("parallel", "arbitrary")),
        name="deltanet",
    )(proj3, proj3, proj3, conv_w.astype(F32), arow, dtrow, gnorm.reshape(1, DH).astype(F32))


def _hgrn_kernel(q_ref, f_ref, i_ref, g_ref, lb_ref, gn_ref, o_ref, st_ref):
    R, K = B_ROWS, B_BLK
    HS = range(HEADS)

    @pl.when(pl.program_id(1) == 0)
    def _():
        st_ref[...] = jnp.zeros_like(st_ref)

    row = lax.broadcasted_iota(jnp.int32, (K, K), 0)
    col = lax.broadcasted_iota(jnp.int32, (K, K), 1)
    lmat = (col <= row).astype(BF16)
    rowv = lax.broadcasted_iota(jnp.int32, (K, DH), 0)
    sls = [slice(h * DH, (h + 1) * DH) for h in HS]
    lb = [lb_ref[:, sl] for sl in sls]
    st = [st_ref[h] for h in HS]

    for blk in range(R // K):
        rs = slice(blk * K, (blk + 1) * K)
        q = [_silu(q_ref[0, rs, sl]) for sl in sls]
        f = [lb[h] + (1.0 - lb[h]) * _sigmoid(f_ref[0, rs, sls[h]]) for h in HS]
        k = [1.0 - a for a in f]
        v = [i_ref[0, rs, sl] for sl in sls]
        gc = [_dot_exact_lhs(lmat, jnp.log(a)) for a in f]
        o = [_dot_nt((q[h] * jnp.exp(gc[h])).astype(BF16), st[h].astype(BF16)) for h in HS]
        for j in range(K):
            for h in HS:
                rel = jnp.exp(jnp.where(rowv >= j, gc[h] - gc[h][j:j + 1], -1e30))
                sj = jnp.sum(q[h] * k[h][j:j + 1] * rel, axis=-1, keepdims=True)
                o[h] = o[h] + sj * v[h][j:j + 1]
        gl = [a[K - 1:K] for a in gc]
        kg = [(k[h] * jnp.exp(gl[h] - gc[h])).astype(BF16) for h in HS]
        st = [st[h] * jnp.exp(gl[h]) + _dot_tn(v[h].astype(BF16), kg[h]) for h in HS]
        o = [a * lax.rsqrt(jnp.mean(a * a, axis=-1, keepdims=True) + EPS) * gn_ref[...] for a in o]
        o_ref[0, rs, :] = (jnp.concatenate(o, axis=1) * _sigmoid(g_ref[0, rs, :])).astype(o_ref.dtype)
    for h in HS:
        st_ref[h] = st[h]


def _hgrn(proj3, lb, gnorm):
    b, t, _ = proj3.shape
    R = B_ROWS
    seg = lambda s: pl.BlockSpec((1, R, HW), lambda i, c: (i, c, OFF_B // HW + s))
    const = lambda shape: pl.BlockSpec(shape, lambda i, c: (0,) * len(shape))
    return pl.pallas_call(
        _hgrn_kernel,
        grid=(b, t // R),
        in_specs=[seg(0), seg(1), seg(2), seg(3), const((1, HW)), const((1, DH))],
        out_specs=pl.BlockSpec((1, R, HW), lambda i, c: (i, c, 0)),
        out_shape=jax.ShapeDtypeStruct((b, t, HW), BF16),
        scratch_shapes=[pltpu.VMEM((HEADS, DH, DH), F32)],
        compiler_params=_cparams(("parallel", "arbitrary")),
        name="hgrn2",
    )(proj3, proj3, proj3, proj3, lb.reshape(1, HW).astype(F32), gnorm.reshape(1, DH).astype(F32))


def _t5_bucket_np(n):
    max_exact = REL_BUCKETS // 2
    nf = np.maximum(n, 1).astype(np.float32)
    large = max_exact + (np.log(nf / max_exact) / math.log(REL_MAX_DIST / max_exact)
                         * (REL_BUCKETS - max_exact)).astype(np.int32)
    large = np.minimum(large, REL_BUCKETS - 1)
    return np.where(n < max_exact, n, large)


def _dilated_kernel(span, dil, hpb, q_ref, kp_ref, kc_ref, vp_ref, vc_ref, bias_ref, qg_ref, kg_ref,
                    o_ref, lse_ref):
    CB = C_BLOCK
    n = pl.program_id(1)
    qi = lax.broadcasted_iota(jnp.int32, (CB, 2 * CB), 0)
    kj = lax.broadcasted_iota(jnp.int32, (CB, 2 * CB), 1)
    dist = qi + CB - kj
    valid = (dist >= 0) & (dist <= span) & ((kj >= CB) | (n > 0))

    def qk_norm(x, g):
        return x * lax.rsqrt(jnp.mean(x * x, axis=-1, keepdims=True) + EPS) * g

    def tiles(items):
        n_it = range(len(items))
        q = [qk_norm(q_ref[0, rows, sl], qg_ref[...]).astype(BF16) for rows, sl, _ in items]
        k = [qk_norm(jnp.concatenate([kp_ref[0, rows, sl], kc_ref[0, rows, sl]], axis=0),
                     kg_ref[...]).astype(BF16) for rows, sl, _ in items]
        v = [jnp.concatenate([vp_ref[0, rows, sl], vc_ref[0, rows, sl]], axis=0).astype(BF16)
             for rows, sl, _ in items]
        s = [_dot_nt(q[i], k[i]) * (DH ** -0.5) + bias_ref[items[i][2]] for i in n_it]
        s = [jnp.where(valid, a, -1e30) for a in s]
        mx = [jnp.max(a, axis=-1, keepdims=True) for a in s]
        p = [jnp.exp(s[i] - mx[i]) for i in n_it]
        den = [jnp.sum(a, axis=-1, keepdims=True) for a in p]
        o = [_dot(p[i].astype(BF16), v[i]) / den[i] for i in n_it]
        for i, (rows, sl, _) in enumerate(items):
            o_ref[0, rows, sl] = o[i]
            lse_ref[0, rows, sl] = jnp.broadcast_to(mx[i] + jnp.log(den[i]), (CB, DH))

    if dil == 1:
        for h0 in range(0, hpb, C_TILES):
            tiles([(pl.ds(0, CB), slice(h * DH, (h + 1) * DH), h) for h in range(h0, h0 + C_TILES)])
    else:
        def body(g, carry):
            tiles([(pl.ds(g * C_TILES + i, CB, stride=dil), slice(0, DH), 0) for i in range(C_TILES)])
            return carry
        lax.fori_loop(0, dil // C_TILES, body, 0)


def _dilated_group(proj3, gi, rel_bias, q_gain, k_gain):
    b, t, _ = proj3.shape
    window, dil = C_PAIRS[gi]
    span = window // dil
    CB = C_BLOCK
    rb = CB * dil
    hpb = HEADS if dil == 1 else 1
    bw = hpb * DH
    qi = np.arange(CB)[:, None]
    kj = np.arange(2 * CB)[None, :]
    bucket = _t5_bucket_np(np.maximum(qi + CB - kj, 0) * dil)
    onehot = jnp.asarray(np.eye(REL_BUCKETS, dtype=np.float32)[bucket])
    bias = jnp.einsum("qkb,bh->hqk", onehot, rel_bias[:, gi * HEADS:(gi + 1) * HEADS].astype(F32),
                      precision=HIGHEST)

    def seg(which, prev):
        base = (OFF_C + which * C_GROUPS * HW + gi * HW) // bw
        if prev:
            return pl.BlockSpec((1, rb, bw), lambda i, n, h: (i, jnp.maximum(n - 1, 0), base + h))
        return pl.BlockSpec((1, rb, bw), lambda i, n, h: (i, n, base + h))

    const = lambda shape: pl.BlockSpec(shape, lambda i, n, h: (0,) * len(shape))
    o, lse = pl.pallas_call(
        functools.partial(_dilated_kernel, span, dil, hpb),
        grid=(b, t // rb, HEADS // hpb),
        in_specs=[seg(0, False), seg(1, True), seg(1, False), seg(2, True), seg(2, False),
                  pl.BlockSpec((hpb, CB, 2 * CB), lambda i, n, h: (h, 0, 0)),
                  const((1, DH)), const((1, DH))],
        out_specs=[pl.BlockSpec((1, rb, bw), lambda i, n, h: (i, n, h)),
                   pl.BlockSpec((1, rb, bw), lambda i, n, h: (i, n, h))],
        out_shape=[jax.ShapeDtypeStruct((b, t, HW), F32),
                   jax.ShapeDtypeStruct((b, t, HW), F32)],
        compiler_params=_cparams(("parallel", "arbitrary", "arbitrary")),
        name=f"dilated_g{gi}",
    )(proj3, proj3, proj3, proj3, proj3, bias,
      q_gain.reshape(1, DH).astype(F32), k_gain.reshape(1, DH).astype(F32))
    return o.reshape(b * t, HW), lse.reshape(b * t, HW)


def _mix_kernel(x_ref, gate_ref, oa_ref, ob_ref, oc0_ref, oc1_ref, oc2_ref,
                l0_ref, l1_ref, l2_ref, wa_ref, wb_ref, wc_ref, wo_ref, out_ref):
    l0, l1, l2 = l0_ref[...], l1_ref[...], l2_ref[...]
    mx = jnp.maximum(jnp.maximum(l0, l1), l2)
    e0, e1, e2 = jnp.exp(l0 - mx), jnp.exp(l1 - mx), jnp.exp(l2 - mx)
    oc = (e0 * oc0_ref[...] + e1 * oc1_ref[...] + e2 * oc2_ref[...]) / (e0 + e1 + e2)
    mix = (_sigmoid(gate_ref[:, 0:D_MODEL]) * _dot(oa_ref[...], wa_ref[...])
           + _sigmoid(gate_ref[:, D_MODEL:2 * D_MODEL]) * _dot(ob_ref[...], wb_ref[...])
           + _sigmoid(gate_ref[:, 2 * D_MODEL:3 * D_MODEL]) * _dot(oc.astype(BF16), wc_ref[...]))
    out_ref[...] = x_ref[...] + _dot(mix.astype(BF16), wo_ref[...])


def _mix(x2, proj2, oa, ob, ocs, lses, wa, wb, wc, wo, tm=256):
    m = x2.shape[0]
    tm = min(tm, m)
    rowblk = lambda w: pl.BlockSpec((tm, w), lambda i: (i, 0))
    const = lambda shape: pl.BlockSpec(shape, lambda i: (0,) * len(shape))
    return pl.pallas_call(
        _mix_kernel,
        grid=(m // tm,),
        in_specs=[rowblk(D_MODEL), rowblk(3 * D_MODEL), rowblk(HW), rowblk(HW),
                  rowblk(HW), rowblk(HW), rowblk(HW), rowblk(HW), rowblk(HW), rowblk(HW),
                  const((HW, D_MODEL)), const((HW, D_MODEL)), const((HW, D_MODEL)),
                  const((D_MODEL, D_MODEL))],
        out_specs=rowblk(D_MODEL),
        out_shape=jax.ShapeDtypeStruct((m, D_MODEL), F32),
        compiler_params=_cparams(("parallel",)),
        name="branch_mix",
    )(x2, proj2, oa, ob, ocs[0], ocs[1], ocs[2], lses[0], lses[1], lses[2],
      wa.astype(BF16), wb.astype(BF16), wc.astype(BF16), wo.astype(BF16))


def _ffn_kernel(x_ref, g_ref, w1_ref, w3_ref, w2_ref, o_ref, h_ref, acc_ref):
    j = pl.program_id(1)

    @pl.when(j == 0)
    def _():
        x = x_ref[...]
        ms = jnp.mean(x * x, axis=-1, keepdims=True)
        h_ref[...] = (x * lax.rsqrt(ms + EPS) * g_ref[...]).astype(BF16)
        acc_ref[...] = jnp.zeros_like(acc_ref)

    h = h_ref[...]
    a = _dot(h, w1_ref[...])
    b = _dot(h, w3_ref[...])
    acc_ref[...] += _dot((_silu(a) * b).astype(BF16), w2_ref[...])

    @pl.when(j == pl.num_programs(1) - 1)
    def _():
        o_ref[...] = x_ref[...] + acc_ref[...]


def _ffn(x2, gain, w1, w3, w2, tm=512, tf=1408):
    m = x2.shape[0]
    ff = w1.shape[1]
    tm = min(tm, m)
    return pl.pallas_call(
        _ffn_kernel,
        grid=(m // tm, ff // tf),
        in_specs=[pl.BlockSpec((tm, D_MODEL), lambda i, j: (i, 0)),
                  pl.BlockSpec((1, D_MODEL), lambda i, j: (0, 0)),
                  pl.BlockSpec((D_MODEL, tf), lambda i, j: (0, j)),
                  pl.BlockSpec((D_MODEL, tf), lambda i, j: (0, j)),
                  pl.BlockSpec((tf, D_MODEL), lambda i, j: (j, 0))],
        out_specs=pl.BlockSpec((tm, D_MODEL), lambda i, j: (i, 0)),
        out_shape=jax.ShapeDtypeStruct((m, D_MODEL), F32),
        scratch_shapes=[pltpu.VMEM((tm, D_MODEL), BF16), pltpu.VMEM((tm, D_MODEL), F32)],
        compiler_params=_cparams(("parallel", "arbitrary")),
        name="ffn",
    )(x2, gain.reshape(1, D_MODEL).astype(F32), w1.astype(BF16), w3.astype(BF16), w2.astype(BF16))


def _router_kernel(x_ref, g_ref, wr_ref, h_ref, ids_ref, ps_ref):
    x = x_ref[...]
    ms = jnp.mean(x * x, axis=-1, keepdims=True)
    h = x * lax.rsqrt(ms + EPS) * g_ref[...]
    h_ref[...] = h
    logits = jnp.dot(h, wr_ref[...], precision=HIGHEST, preferred_element_type=F32)
    lane = lax.broadcasted_iota(jnp.int32, logits.shape, 1)
    neg = jnp.float32(-jnp.inf)
    l1 = jnp.where(lane < N_EXPERTS, logits, neg)
    m1 = jnp.max(l1, axis=-1, keepdims=True)
    i1 = jnp.min(jnp.where(l1 == m1, lane, 128), axis=-1, keepdims=True)
    l2 = jnp.where(lane == i1, neg, l1)
    m2 = jnp.max(l2, axis=-1, keepdims=True)
    i2 = jnp.min(jnp.where(l2 == m2, lane, 128), axis=-1, keepdims=True)
    e = jnp.exp(m2 - m1)
    p1 = 1.0 / (1.0 + e)
    p2 = e / (1.0 + e)
    ids_ref[...] = jnp.where(lane == 0, i1, jnp.where(lane == 1, i2, 0))
    ps_ref[...] = jnp.where(lane == 0, p1, jnp.where(lane == 1, p2, 0.0))


def _router(x2, gain, w_router, tm=512):
    m = x2.shape[0]
    tm = min(tm, m)
    wr = jnp.pad(w_router.astype(F32), ((0, 0), (0, 128 - N_EXPERTS)))
    return pl.pallas_call(
        _router_kernel,
        grid=(m // tm,),
        in_specs=[pl.BlockSpec((tm, D_MODEL), lambda i: (i, 0)),
                  pl.BlockSpec((1, D_MODEL), lambda i: (0, 0)),
                  pl.BlockSpec((D_MODEL, 128), lambda i: (0, 0))],
        out_specs=[pl.BlockSpec((tm, D_MODEL), lambda i: (i, 0)),
                   pl.BlockSpec((tm, 128), lambda i: (i, 0)),
                   pl.BlockSpec((tm, 128), lambda i: (i, 0))],
        out_shape=[jax.ShapeDtypeStruct((m, D_MODEL), F32),
                   jax.ShapeDtypeStruct((m, 128), jnp.int32),
                   jax.ShapeDtypeStruct((m, 128), F32)],
        compiler_params=_cparams(("parallel",)),
        name="router",
    )(x2, gain.reshape(1, D_MODEL).astype(F32), wr)


def _route_positions(ids, tm):
    m = ids.shape[0]
    e_flat = ids[:, :TOP_K].reshape(-1)
    onehot = (e_flat[:, None] == jnp.arange(N_EXPERTS)[None, :]).astype(jnp.int32)
    csum = jnp.cumsum(onehot, axis=0)
    counts = csum[-1]
    gsz = ((counts + tm - 1) // tm) * tm
    gend = jnp.cumsum(gsz)
    pos = jnp.sum(onehot * (gend - gsz + csum - 1), axis=1)
    nt = (TOP_K * m + N_EXPERTS * tm) // tm
    n_used = gend[-1] // tm
    tile_e = jnp.sum((jnp.arange(nt)[:, None] * tm >= gend[None, :]).astype(jnp.int32), axis=1)
    last_e = jnp.sum(((n_used - 1) * tm >= gend).astype(jnp.int32))
    tile_e = jnp.minimum(tile_e, last_e)
    return pos.astype(jnp.int32), tile_e.astype(jnp.int32), n_used.reshape(1).astype(jnp.int32), nt


def _dispatch_kernel(pos_ref, h_ref, init_ref, xs_ref, sem):
    del init_ref
    td = h_ref.shape[0]

    def row_copy(t, s):
        return pltpu.make_async_copy(h_ref.at[pl.ds(t, 1)],
                                     xs_ref.at[pl.ds(pos_ref[0, 0, TOP_K * t + s], 1)], sem)

    def start(t, c):
        for s in range(TOP_K):
            row_copy(t, s).start()
        return c

    def wait(t, c):
        for s in range(TOP_K):
            row_copy(t, s).wait()
        return c

    lax.fori_loop(0, td, start, 0, unroll=8)
    lax.fori_loop(0, td, wait, 0, unroll=8)


def _dispatch(h, pos, npad, td=256):
    m = h.shape[0]
    td = min(td, m)
    return pl.pallas_call(
        _dispatch_kernel,
        grid=(m // td,),
        in_specs=[pl.BlockSpec((1, 1, TOP_K * td), lambda i: (i, 0, 0), memory_space=pltpu.SMEM),
                  pl.BlockSpec((td, D_MODEL), lambda i: (i, 0)),
                  pl.BlockSpec(memory_space=pl.ANY)],
        out_specs=pl.BlockSpec(memory_space=pl.ANY),
        out_shape=jax.ShapeDtypeStruct((npad, D_MODEL), F32),
        scratch_shapes=[pltpu.SemaphoreType.DMA(())],
        input_output_aliases={2: 0},
        compiler_params=_cparams(("arbitrary",)),
        name="moe_dispatch",
    )(pos.reshape(m // td, 1, TOP_K * td), h, jnp.zeros((npad, D_MODEL), F32))


def _experts_kernel(te_ref, nu_ref, xs_ref, w1_ref, w3_ref, w2_ref, y_ref, xb_ref, acc_ref):
    del te_ref
    i = pl.program_id(0)
    j = pl.program_id(1)

    @pl.when(i < nu_ref[0])
    def _():
        @pl.when(j == 0)
        def _():
            xb_ref[...] = xs_ref[...].astype(BF16)
            acc_ref[...] = jnp.zeros_like(acc_ref)

        xb = xb_ref[...]
        a = _dot(xb, w1_ref[0])
        b = _dot(xb, w3_ref[0])
        acc_ref[...] += _dot((_silu(a) * b).astype(BF16), w2_ref[0])

        @pl.when(j == pl.num_programs(1) - 1)
        def _():
            y_ref[...] = acc_ref[...]

    @pl.when((i >= nu_ref[0]) & (j == 0))
    def _():
        y_ref[...] = jnp.zeros_like(y_ref)


def _experts(xs, tile_e, n_used, w1, w3, w2, tm, tf=896):
    npad = xs.shape[0]
    ff = w1.shape[2]
    nj = ff // tf
    row = lambda i, j, te, nu: (jnp.minimum(i, nu[0] - 1), 0)
    jj = lambda i, j, nu: jnp.where(i < nu[0], j, nj - 1)
    return pl.pallas_call(
        _experts_kernel,
        grid_spec=pltpu.PrefetchScalarGridSpec(
            num_scalar_prefetch=2,
            grid=(npad // tm, nj),
            in_specs=[pl.BlockSpec((tm, D_MODEL), row),
                      pl.BlockSpec((1, D_MODEL, tf), lambda i, j, te, nu: (te[i], 0, jj(i, j, nu))),
                      pl.BlockSpec((1, D_MODEL, tf), lambda i, j, te, nu: (te[i], 0, jj(i, j, nu))),
                      pl.BlockSpec((1, tf, D_MODEL), lambda i, j, te, nu: (te[i], jj(i, j, nu), 0))],
            out_specs=pl.BlockSpec((tm, D_MODEL), lambda i, j, te, nu: (i, 0)),
            scratch_shapes=[pltpu.VMEM((tm, D_MODEL), BF16), pltpu.VMEM((tm, D_MODEL), F32)]),
        out_shape=jax.ShapeDtypeStruct((npad, D_MODEL), F32),
        compiler_params=_cparams(("arbitrary", "arbitrary")),
        name="moe_experts",
    )(tile_e, n_used, xs, w1.astype(BF16), w3.astype(BF16), w2.astype(BF16))


def _combine_kernel(pos_ref, x_ref, ps_ref, y_ref, o_ref, buf_ref, sem):
    td = x_ref.shape[0]

    def row_copy(t, s):
        return pltpu.make_async_copy(y_ref.at[pl.ds(pos_ref[0, 0, TOP_K * t + s], 1)],
                                     buf_ref.at[s, pl.ds(t, 1)], sem.at[s])

    def start(t, c):
        for s in range(TOP_K):
            row_copy(t, s).start()
        return c

    def wait(t, c):
        for s in range(TOP_K):
            row_copy(t, s).wait()
        return c

    lax.fori_loop(0, td, start, 0, unroll=8)
    lax.fori_loop(0, td, wait, 0, unroll=8)
    ps = ps_ref[...]
    o_ref[...] = x_ref[...] + ps[:, 0:1] * buf_ref[0] + ps[:, 1:2] * buf_ref[1]


def _combine(x2, ps, pos, y, td=256):
    m = x2.shape[0]
    td = min(td, m)
    return pl.pallas_call(
        _combine_kernel,
        grid=(m // td,),
        in_specs=[pl.BlockSpec((1, 1, TOP_K * td), lambda i: (i, 0, 0), memory_space=pltpu.SMEM),
                  pl.BlockSpec((td, D_MODEL), lambda i: (i, 0)),
                  pl.BlockSpec((td, 128), lambda i: (i, 0)),
                  pl.BlockSpec(memory_space=pl.ANY)],
        out_specs=pl.BlockSpec((td, D_MODEL), lambda i: (i, 0)),
        out_shape=jax.ShapeDtypeStruct((m, D_MODEL), F32),
        scratch_shapes=[pltpu.VMEM((TOP_K, td, D_MODEL), F32), pltpu.SemaphoreType.DMA((TOP_K,))],
        compiler_params=_cparams(("arbitrary",)),
        name="moe_combine",
    )(pos.reshape(m // td, 1, TOP_K * td), x2, ps, y)


def _moe(x2, gain, w_router, w1, w3, w2, tm=512):
    m = x2.shape[0]
    tm = min(tm, m)
    h, ids, ps = _router(x2, gain, w_router)
    pos, tile_e, n_used, nt = _route_positions(ids, tm)
    xs = _dispatch(h, pos, nt * tm)
    y = _experts(xs, tile_e, n_used, w1, w3, w2, tm)
    return _combine(x2, ps, pos, y)


def _relayout_w_in(w):
    k = w.shape[0]
    z = lambda n: jnp.zeros((k, n), w.dtype)
    cols = [w[:, _R_BRG:_R_END], w[:, _R_AQKV:_R_BETA], w[:, _R_AGATE:_R_B],
            w[:, _R_B:_R_C], w[:, _R_C:_R_BRG], w[:, _R_BETA:_R_AGATE],
            z(NP - OFF_BA - 2 * HEADS)]
    return jnp.concatenate(cols, axis=1).astype(BF16)


def _layer_mixers(x2, b, t, layer, lower_bounds, w_in, norm_mix, conv_a, a_log, dt_bias,
                  gnorm_a, gnorm_b, qnorm_c, knorm_c, rel_bias, w_br_a, w_br_b, w_br_c, w_out):
    proj2 = _norm_proj(x2, norm_mix[layer].astype(F32), _relayout_w_in(w_in[layer]))
    proj3 = proj2.reshape(b, t, NP)
    oa = _deltanet(proj3, conv_a[layer], a_log[layer], dt_bias[layer], gnorm_a[layer])
    ob = _hgrn(proj3, lower_bounds[layer], gnorm_b[layer])
    ocs, lses = [], []
    for gi in range(C_GROUPS):
        o, lse = _dilated_group(proj3, gi, rel_bias, qnorm_c[layer, gi], knorm_c[layer, gi])
        ocs.append(o)
        lses.append(lse)
    return _mix(x2, proj2, oa.reshape(b * t, HW), ob.reshape(b * t, HW), ocs, lses,
                w_br_a[layer], w_br_b[layer], w_br_c[layer], w_out[layer])


def kernel(x, w_in, norm_mix, conv_a, a_log, dt_bias, gnorm_a, lb_logits, gnorm_b, qnorm_c, knorm_c, rel_bias, w_br_a, w_br_b, w_br_c, w_out, norm_ffn, ffn_w1, ffn_w3, ffn_w2, router, moe_w1, moe_w3, moe_w2):
    b, t, _ = x.shape
    depth = w_in.shape[0]
    p_lb = jax.nn.softmax(lb_logits.astype(F32), axis=0)
    lower_bounds = jnp.cumsum(p_lb, axis=0) - p_lb[0:1]
    x2 = x.reshape(b * t, D_MODEL).astype(F32)
    for layer in range(depth):
        x2 = _layer_mixers(x2, b, t, layer, lower_bounds, w_in, norm_mix, conv_a, a_log, dt_bias,
                           gnorm_a, gnorm_b, qnorm_c, knorm_c, rel_bias,
                           w_br_a, w_br_b, w_br_c, w_out)
        li = layer // 2
        if layer % 2 == 0:
            x2 = _ffn(x2, norm_ffn[layer], ffn_w1[li], ffn_w3[li], ffn_w2[li])
        else:
            x2 = _moe(x2, norm_ffn[layer], router[li], moe_w1[li], moe_w3[li], moe_w2[li])
    return x2.reshape(b, t, D_MODEL).astype(x.dtype)
```

```python
import functools
import math

import numpy as np
import jax
import jax.numpy as jnp
from jax import lax
from jax.experimental import pallas as pl
from jax.experimental.pallas import tpu as pltpu

F32 = jnp.float32
BF16 = jnp.bfloat16
HIGHEST = lax.Precision.HIGHEST

D_MODEL = 1024
EPS = 1e-6
HEADS = 4
DH = 128
HW = HEADS * DH
A_CONV = 4
A_CHUNK = 64
A_ROWS = 256
B_ROWS = 64
B_BLK = 16
C_PAIRS = ((128, 1), (512, 4), (2048, 16))
C_GROUPS = 3
C_BLOCK = 128
C_TILES = 4
REL_BUCKETS = 32
REL_MAX_DIST = 2048
N_EXPERTS = 8
TOP_K = 2

OFF_BRG = 0
OFF_AQKV = 3072
OFF_AGATE = 4608
OFF_B = 5120
NP_MAIN = 7168
OFF_C = 0
OFF_BA = 4608
NP_C = 5120

_R_AQKV, _R_BETA, _R_AGATE, _R_B, _R_C, _R_BRG, _R_END = 0, 1536, 1544, 2056, 4104, 8712, 11784

VMEM_LIMIT = 56 * 1024 * 1024


def _cparams(sem):
    return pltpu.CompilerParams(dimension_semantics=sem, vmem_limit_bytes=VMEM_LIMIT)


def _sigmoid(x):
    return 1.0 / (1.0 + jnp.exp(-x))


def _silu(x):
    return x * _sigmoid(x)


def _softplus(x):
    return jnp.maximum(x, 0.0) + jnp.log(1.0 + jnp.exp(-jnp.abs(x)))


def _dot(a, b):
    return jnp.dot(a, b, preferred_element_type=F32)


def _dot_nt(a, b, precision=None):
    return lax.dot_general(a, b, (((1,), (1,)), ((), ())), precision=precision,
                           preferred_element_type=F32)


def _dot_tn(a, b):
    return lax.dot_general(a, b, (((0,), (0,)), ((), ())), preferred_element_type=F32)


def _norm_proj_kernel(x_ref, g_ref, w_ref, o_ref, h_ref):
    @pl.when(pl.program_id(1) == 0)
    def _():
        x = x_ref[...]
        ms = jnp.mean(x * x, axis=-1, keepdims=True)
        h_ref[...] = (x * lax.rsqrt(ms + EPS) * g_ref[...]).astype(BF16)

    o_ref[...] = _dot(h_ref[...], w_ref[...]).astype(o_ref.dtype)


def _norm_proj(x2, gain, w_bf16, out_dtype, tm=1024, tn=1024):
    m = x2.shape[0]
    n = w_bf16.shape[1]
    tm = min(tm, m)
    return pl.pallas_call(
        _norm_proj_kernel,
        grid=(m // tm, n // tn),
        in_specs=[pl.BlockSpec((tm, D_MODEL), lambda i, j: (i, 0)),
                  pl.BlockSpec((1, D_MODEL), lambda i, j: (0, 0)),
                  pl.BlockSpec((D_MODEL, tn), lambda i, j: (0, j))],
        out_specs=pl.BlockSpec((tm, tn), lambda i, j: (i, j)),
        out_shape=jax.ShapeDtypeStruct((m, n), out_dtype),
        scratch_shapes=[pltpu.VMEM((tm, D_MODEL), BF16)],
        compiler_params=_cparams(("parallel", "arbitrary")),
        name="norm_proj",
    )(x2, gain.reshape(1, D_MODEL), w_bf16)


def _dotb(a, b):
    return _dot(a.astype(BF16), b.astype(BF16))


def _dot_exact_lhs(a_bf16, b):
    b0 = b.astype(BF16)
    r1 = b - b0.astype(F32)
    b1 = r1.astype(BF16)
    b2 = (r1 - b1.astype(F32)).astype(BF16)
    return _dot(a_bf16, b0) + (_dot(a_bf16, b1) + _dot(a_bf16, b2))


def _deltanet_kernel(qkv_ref, gate_ref, ba_ref, convw_ref, arow_ref, dtrow_ref, gn_ref,
                     o_ref, s_ref, xe_ref):
    C = A_CHUNK
    R = A_ROWS
    HS = range(HEADS)
    IT = range((R // C) * HEADS)

    @pl.when(pl.program_id(1) == 0)
    def _():
        s_ref[...] = jnp.zeros_like(s_ref)
        xe_ref[0:8, :] = jnp.zeros((8, 3 * HW), F32)

    x = qkv_ref[0].astype(F32)
    xe_ref[8:8 + R, :] = x
    w = convw_ref[...]
    y = (w[3:4] * x + w[2:3] * xe_ref[7:7 + R, :] + w[1:2] * xe_ref[6:6 + R, :]
         + w[0:1] * xe_ref[5:5 + R, :])
    xe_ref[0:8, :] = x[R - 8:R]
    y = _silu(y)

    ba = ba_ref[0]
    beta_all = _sigmoid(ba)
    g_all = arow_ref[...] * _softplus(ba + dtrow_ref[...])
    gate = _silu(gate_ref[0].astype(F32))
    s_old = [s_ref[h] for h in HS]

    row = lax.broadcasted_iota(jnp.int32, (C, C), 0)
    col = lax.broadcasted_iota(jnp.int32, (C, C), 1)
    lmat = (col <= row).astype(BF16)
    rowx = lax.broadcasted_iota(jnp.int32, (C, DH + C), 0)
    colx = lax.broadcasted_iota(jnp.int32, (C, DH + C), 1)
    umask = (colx < DH) | (rowx > colx - DH)
    eye = (row == col).astype(F32)
    bd8 = (row >> 3) == (col >> 3)

    def merge_mask(sh):
        return (((row >> (sh + 1)) == (col >> (sh + 1)))
                & (((row >> sh) & 1) == 1) & (((col >> sh) & 1) == 0))

    rows = [slice((i // HEADS) * C, (i // HEADS + 1) * C) for i in IT]
    hd = [i % HEADS for i in IT]
    q = [y[rows[i], hd[i] * DH:(hd[i] + 1) * DH] for i in IT]
    k = [y[rows[i], HW + hd[i] * DH:HW + (hd[i] + 1) * DH] for i in IT]
    v = [y[rows[i], 2 * HW + hd[i] * DH:2 * HW + (hd[i] + 1) * DH] for i in IT]
    q = [a * (lax.rsqrt(jnp.sum(a * a, axis=-1, keepdims=True) + EPS) * (DH ** -0.5)) for a in q]
    k = [a * lax.rsqrt(jnp.sum(a * a, axis=-1, keepdims=True) + EPS) for a in k]
    beta = [beta_all[rows[i], hd[i]:hd[i] + 1] for i in IT]
    gb = [jnp.broadcast_to(g_all[rows[i], HEADS + hd[i]:HEADS + hd[i] + 1], (C, DH + C)) for i in IT]
    dext = [_dot_exact_lhs(lmat, jnp.where(umask, gb[i], 0.0)) for i in IT]
    gc = [d[:, :DH] for d in dext]
    edm = [jnp.exp(d[:, DH:]) for d in dext]
    egc = [jnp.exp(g) for g in gc]
    gl = [g[C - 1:C, :] for g in gc]
    kb = [k[i] * beta[i] for i in IT]
    m = [_dot_nt(kb[i].astype(BF16), k[i].astype(BF16)) * jnp.where(row > col, edm[i], 0.0)
         for i in IT]

    nd = [jnp.where(bd8, -a, 0.0) for a in m]
    p2 = [_dotb(a, a) for a in nd]
    p4 = [_dotb(a, a) for a in p2]
    x1 = [eye + nd[i] for i in IT]
    x1 = [x1[i] + _dotb(x1[i], p2[i]) for i in IT]
    xi = [x1[i] + _dotb(x1[i], p4[i]) for i in IT]
    for sh in (3, 4, 5):
        mm = merge_mask(sh)
        t = [_dotb(xi[i], jnp.where(mm, m[i], 0.0)) for i in IT]
        xi = [xi[i] - _dotb(t[i], xi[i]) for i in IT]

    rhs = [jnp.concatenate([v[i] * beta[i], kb[i] * egc[i]], axis=1) for i in IT]
    sol = [_dotb(xi[i], rhs[i]) for i in IT]
    attn = [(_dot_nt(q[i].astype(BF16), k[i].astype(BF16))
             * jnp.where(row >= col, edm[i], 0.0)).astype(BF16) for i in IT]
    qg = [(q[i] * egc[i]).astype(BF16) for i in IT]
    kg = [(k[i] * jnp.exp(gl[i] - gc[i])).astype(BF16) for i in IT]
    egl = [jnp.exp(a) for a in gl]

    s_cur = s_old
    for c in range(R // C):
        it = [c * HEADS + h for h in HS]
        sb = [a.astype(BF16) for a in s_cur]
        v_new = [(sol[i][:, :DH] - _dot(sol[i][:, DH:].astype(BF16), sb[h])).astype(BF16)
                 for h, i in enumerate(it)]
        o = [_dot(qg[i], sb[h]) + _dot(attn[i], v_new[h]) for h, i in enumerate(it)]
        s_cur = [s_cur[h] * egl[i] + _dot_tn(kg[i], v_new[h]) for h, i in enumerate(it)]
        o = [a * lax.rsqrt(jnp.mean(a * a, axis=-1, keepdims=True) + EPS) * gn_ref[...] for a in o]
        o_ref[0, c * C:(c + 1) * C, :] = (jnp.concatenate(o, axis=1)
                                          * gate[c * C:(c + 1) * C]).astype(o_ref.dtype)
    for h in HS:
        s_ref[h] = s_cur[h]


def _deltanet(proj3, projc3, conv_w, a_log, dt_bias, gnorm):
    b, t, _ = proj3.shape
    C = A_ROWS
    pad = jnp.zeros((HEADS,), F32)
    arow = jnp.concatenate([pad, -jnp.exp(a_log.astype(F32)), jnp.zeros((120,), F32)]).reshape(1, 128)
    dtrow = jnp.concatenate([pad, dt_bias.astype(F32), jnp.zeros((120,), F32)]).reshape(1, 128)
    const = lambda shape: pl.BlockSpec(shape, lambda i, c: (0,) * len(shape))
    return pl.pallas_call(
        _deltanet_kernel,
        grid=(b, t // C),
        in_specs=[pl.BlockSpec((1, C, 3 * HW), lambda i, c: (i, c, OFF_AQKV // (3 * HW))),
                  pl.BlockSpec((1, C, HW), lambda i, c: (i, c, OFF_AGATE // HW)),
                  pl.BlockSpec((1, C, 128), lambda i, c: (i, c, OFF_BA // 128)),
                  const((A_CONV, 3 * HW)), const((1, 128)), const((1, 128)), const((1, DH))],
        out_specs=pl.BlockSpec((1, C, HW), lambda i, c: (i, c, 0)),
        out_shape=jax.ShapeDtypeStruct((b, t, HW), BF16),
        scratch_shapes=[pltpu.VMEM((HEADS, DH, DH), F32), pltpu.VMEM((C + 8, 3 * HW), F32)],
        compiler_params=_cparams(("parallel", "arbitrary")),
        name="deltanet",
    )(proj3, proj3, projc3, conv_w.astype(F32), arow, dtrow, gnorm.reshape(1, DH).astype(F32))


def _hgrn_kernel(q_ref, f_ref, i_ref, g_ref, lb_ref, gn_ref, o_ref, st_ref):
    R, K = B_ROWS, B_BLK
    HS = range(HEADS)

    @pl.when(pl.program_id(1) == 0)
    def _():
        st_ref[...] = jnp.zeros_like(st_ref)

    row = lax.broadcasted_iota(jnp.int32, (K, K), 0)
    col = lax.broadcasted_iota(jnp.int32, (K, K), 1)
    lmat = (col <= row).astype(BF16)
    rowv = lax.broadcasted_iota(jnp.int32, (K, DH), 0)
    sls = [slice(h * DH, (h + 1) * DH) for h in HS]
    lb = [lb_ref[:, sl] for sl in sls]
    st = [st_ref[h] for h in HS]

    for blk in range(R // K):
        rs = slice(blk * K, (blk + 1) * K)
        q = [_silu(q_ref[0, rs, sl].astype(F32)) for sl in sls]
        f = [lb[h] + (1.0 - lb[h]) * _sigmoid(f_ref[0, rs, sls[h]].astype(F32)) for h in HS]
        k = [1.0 - a for a in f]
        v = [i_ref[0, rs, sl].astype(F32) for sl in sls]
        gc = [_dot_exact_lhs(lmat, jnp.log(a)) for a in f]
        o = [_dot_nt((q[h] * jnp.exp(gc[h])).astype(BF16), st[h].astype(BF16)) for h in HS]
        for j in range(K):
            for h in HS:
                rel = jnp.exp(jnp.where(rowv >= j, gc[h] - gc[h][j:j + 1], -1e30))
                sj = jnp.sum(q[h] * k[h][j:j + 1] * rel, axis=-1, keepdims=True)
                o[h] = o[h] + sj * v[h][j:j + 1]
        gl = [a[K - 1:K] for a in gc]
        kg = [(k[h] * jnp.exp(gl[h] - gc[h])).astype(BF16) for h in HS]
        st = [st[h] * jnp.exp(gl[h]) + _dot_tn(v[h].astype(BF16), kg[h]) for h in HS]
        o = [a * lax.rsqrt(jnp.mean(a * a, axis=-1, keepdims=True) + EPS) * gn_ref[...] for a in o]
        o_ref[0, rs, :] = (jnp.concatenate(o, axis=1) * _sigmoid(g_ref[0, rs, :].astype(F32))).astype(o_ref.dtype)
    for h in HS:
        st_ref[h] = st[h]


def _hgrn(proj3, lb, gnorm):
    b, t, _ = proj3.shape
    R = B_ROWS
    seg = lambda s: pl.BlockSpec((1, R, HW), lambda i, c: (i, c, OFF_B // HW + s))
    const = lambda shape: pl.BlockSpec(shape, lambda i, c: (0,) * len(shape))
    return pl.pallas_call(
        _hgrn_kernel,
        grid=(b, t // R),
        in_specs=[seg(0), seg(1), seg(2), seg(3), const((1, HW)), const((1, DH))],
        out_specs=pl.BlockSpec((1, R, HW), lambda i, c: (i, c, 0)),
        out_shape=jax.ShapeDtypeStruct((b, t, HW), BF16),
        scratch_shapes=[pltpu.VMEM((HEADS, DH, DH), F32)],
        compiler_params=_cparams(("parallel", "arbitrary")),
        name="hgrn2",
    )(proj3, proj3, proj3, proj3, lb.reshape(1, HW).astype(F32), gnorm.reshape(1, DH).astype(F32))


def _t5_bucket_np(n):
    max_exact = REL_BUCKETS // 2
    nf = np.maximum(n, 1).astype(np.float32)
    large = max_exact + (np.log(nf / max_exact) / math.log(REL_MAX_DIST / max_exact)
                         * (REL_BUCKETS - max_exact)).astype(np.int32)
    large = np.minimum(large, REL_BUCKETS - 1)
    return np.where(n < max_exact, n, large)


def _dilated_kernel(span, dil, hpb, q_ref, kp_ref, kc_ref, vp_ref, vc_ref, bias_ref, qg_ref, kg_ref,
                    o_ref, lse_ref):
    CB = C_BLOCK
    n = pl.program_id(1)
    qi = lax.broadcasted_iota(jnp.int32, (CB, 2 * CB), 0)
    kj = lax.broadcasted_iota(jnp.int32, (CB, 2 * CB), 1)
    dist = qi + CB - kj
    valid = (dist >= 0) & (dist <= span) & ((kj >= CB) | (n > 0))

    def qk_norm(x, g):
        return x * lax.rsqrt(jnp.mean(x * x, axis=-1, keepdims=True) + EPS) * g

    def tiles(items):
        n_it = range(len(items))
        q = [qk_norm(q_ref[0, rows, sl], qg_ref[...]).astype(BF16) for rows, sl, _ in items]
        k = [qk_norm(jnp.concatenate([kp_ref[0, rows, sl], kc_ref[0, rows, sl]], axis=0),
                     kg_ref[...]).astype(BF16) for rows, sl, _ in items]
        v = [jnp.concatenate([vp_ref[0, rows, sl], vc_ref[0, rows, sl]], axis=0).astype(BF16)
             for rows, sl, _ in items]
        s = [_dot_nt(q[i], k[i]) * (DH ** -0.5) + bias_ref[items[i][2]] for i in n_it]
        s = [jnp.where(valid, a, -1e30) for a in s]
        mx = [jnp.max(a, axis=-1, keepdims=True) for a in s]
        p = [jnp.exp(s[i] - mx[i]) for i in n_it]
        den = [jnp.sum(a, axis=-1, keepdims=True) for a in p]
        o = [_dot(p[i].astype(BF16), v[i]) / den[i] for i in n_it]
        for i, (rows, sl, _) in enumerate(items):
            o_ref[0, rows, sl] = o[i]
            lse_ref[0, rows, sl] = jnp.broadcast_to(mx[i] + jnp.log(den[i]), (CB, DH))

    if dil == 1:
        for h0 in range(0, hpb, C_TILES):
            tiles([(pl.ds(0, CB), slice(h * DH, (h + 1) * DH), h) for h in range(h0, h0 + C_TILES)])
    else:
        def body(g, carry):
            tiles([(pl.ds(g * C_TILES + i, CB, stride=dil), slice(0, DH), 0) for i in range(C_TILES)])
            return carry
        lax.fori_loop(0, dil // C_TILES, body, 0)


def _dilated_group(proj3, gi, rel_bias, q_gain, k_gain):
    b, t, _ = proj3.shape
    window, dil = C_PAIRS[gi]
    span = window // dil
    CB = C_BLOCK
    rb = CB * dil
    hpb = HEADS if dil == 1 else 1
    bw = hpb * DH
    qi = np.arange(CB)[:, None]
    kj = np.arange(2 * CB)[None, :]
    bucket = _t5_bucket_np(np.maximum(qi + CB - kj, 0) * dil)
    onehot = jnp.asarray(np.eye(REL_BUCKETS, dtype=np.float32)[bucket])
    bias = jnp.einsum("qkb,bh->hqk", onehot, rel_bias[:, gi * HEADS:(gi + 1) * HEADS].astype(F32),
                      precision=HIGHEST)

    def seg(which, prev):
        base = (OFF_C + which * C_GROUPS * HW + gi * HW) // bw
        if prev:
            return pl.BlockSpec((1, rb, bw), lambda i, n, h: (i, jnp.maximum(n - 1, 0), base + h))
        return pl.BlockSpec((1, rb, bw), lambda i, n, h: (i, n, base + h))

    const = lambda shape: pl.BlockSpec(shape, lambda i, n, h: (0,) * len(shape))
    o, lse = pl.pallas_call(
        functools.partial(_dilated_kernel, span, dil, hpb),
        grid=(b, t // rb, HEADS // hpb),
        in_specs=[seg(0, False), seg(1, True), seg(1, False), seg(2, True), seg(2, False),
                  pl.BlockSpec((hpb, CB, 2 * CB), lambda i, n, h: (h, 0, 0)),
                  const((1, DH)), const((1, DH))],
        out_specs=[pl.BlockSpec((1, rb, bw), lambda i, n, h: (i, n, h)),
                   pl.BlockSpec((1, rb, bw), lambda i, n, h: (i, n, h))],
        out_shape=[jax.ShapeDtypeStruct((b, t, HW), F32),
                   jax.ShapeDtypeStruct((b, t, HW), F32)],
        compiler_params=_cparams(("parallel", "arbitrary", "arbitrary")),
        name=f"dilated_g{gi}",
    )(proj3, proj3, proj3, proj3, proj3, bias,
      q_gain.reshape(1, DH).astype(F32), k_gain.reshape(1, DH).astype(F32))
    return o.reshape(b * t, HW), lse.reshape(b * t, HW)


def _mix_kernel(x_ref, gate_ref, oa_ref, ob_ref, oc0_ref, oc1_ref, oc2_ref,
                l0_ref, l1_ref, l2_ref, wa_ref, wb_ref, wc_ref, wo_ref, out_ref):
    l0, l1, l2 = l0_ref[...], l1_ref[...], l2_ref[...]
    mx = jnp.maximum(jnp.maximum(l0, l1), l2)
    e0, e1, e2 = jnp.exp(l0 - mx), jnp.exp(l1 - mx), jnp.exp(l2 - mx)
    oc = (e0 * oc0_ref[...] + e1 * oc1_ref[...] + e2 * oc2_ref[...]) / (e0 + e1 + e2)
    mix = (_sigmoid(gate_ref[:, 0:D_MODEL].astype(F32)) * _dot(oa_ref[...], wa_ref[...])
           + _sigmoid(gate_ref[:, D_MODEL:2 * D_MODEL].astype(F32)) * _dot(ob_ref[...], wb_ref[...])
           + _sigmoid(gate_ref[:, 2 * D_MODEL:3 * D_MODEL].astype(F32)) * _dot(oc.astype(BF16), wc_ref[...]))
    out_ref[...] = x_ref[...] + _dot(mix.astype(BF16), wo_ref[...])


def _mix(x2, proj2, oa, ob, ocs, lses, wa, wb, wc, wo, tm=256):
    m = x2.shape[0]
    tm = min(tm, m)
    rowblk = lambda w: pl.BlockSpec((tm, w), lambda i: (i, 0))
    const = lambda shape: pl.BlockSpec(shape, lambda i: (0,) * len(shape))
    return pl.pallas_call(
        _mix_kernel,
        grid=(m // tm,),
        in_specs=[rowblk(D_MODEL), rowblk(3 * D_MODEL), rowblk(HW), rowblk(HW),
                  rowblk(HW), rowblk(HW), rowblk(HW), rowblk(HW), rowblk(HW), rowblk(HW),
                  const((HW, D_MODEL)), const((HW, D_MODEL)), const((HW, D_MODEL)),
                  const((D_MODEL, D_MODEL))],
        out_specs=rowblk(D_MODEL),
        out_shape=jax.ShapeDtypeStruct((m, D_MODEL), F32),
        compiler_params=_cparams(("parallel",)),
        name="branch_mix",
    )(x2, proj2, oa, ob, ocs[0], ocs[1], ocs[2], lses[0], lses[1], lses[2],
      wa.astype(BF16), wb.astype(BF16), wc.astype(BF16), wo.astype(BF16))


def _ffn_kernel(x_ref, g_ref, w1_ref, w3_ref, w2_ref, o_ref, h_ref, acc_ref):
    j = pl.program_id(1)

    @pl.when(j == 0)
    def _():
        x = x_ref[...]
        ms = jnp.mean(x * x, axis=-1, keepdims=True)
        h_ref[...] = (x * lax.rsqrt(ms + EPS) * g_ref[...]).astype(BF16)
        acc_ref[...] = jnp.zeros_like(acc_ref)

    h = h_ref[...]
    a = _dot(h, w1_ref[...])
    b = _dot(h, w3_ref[...])
    acc_ref[...] += _dot((_silu(a) * b).astype(BF16), w2_ref[...])

    @pl.when(j == pl.num_programs(1) - 1)
    def _():
        o_ref[...] = x_ref[...] + acc_ref[...]


def _ffn(x2, gain, w1, w3, w2, tm=512, tf=1408):
    m = x2.shape[0]
    ff = w1.shape[1]
    tm = min(tm, m)
    return pl.pallas_call(
        _ffn_kernel,
        grid=(m // tm, ff // tf),
        in_specs=[pl.BlockSpec((tm, D_MODEL), lambda i, j: (i, 0)),
                  pl.BlockSpec((1, D_MODEL), lambda i, j: (0, 0)),
                  pl.BlockSpec((D_MODEL, tf), lambda i, j: (0, j)),
                  pl.BlockSpec((D_MODEL, tf), lambda i, j: (0, j)),
                  pl.BlockSpec((tf, D_MODEL), lambda i, j: (j, 0))],
        out_specs=pl.BlockSpec((tm, D_MODEL), lambda i, j: (i, 0)),
        out_shape=jax.ShapeDtypeStruct((m, D_MODEL), F32),
        scratch_shapes=[pltpu.VMEM((tm, D_MODEL), BF16), pltpu.VMEM((tm, D_MODEL), F32)],
        compiler_params=_cparams(("parallel", "arbitrary")),
        name="ffn",
    )(x2, gain.reshape(1, D_MODEL).astype(F32), w1.astype(BF16), w3.astype(BF16), w2.astype(BF16))


def _router_kernel(x_ref, g_ref, wr_ref, h_ref, ids_ref, ps_ref):
    x = x_ref[...]
    ms = jnp.mean(x * x, axis=-1, keepdims=True)
    h = x * lax.rsqrt(ms + EPS) * g_ref[...]
    h_ref[...] = h
    logits = jnp.dot(h, wr_ref[...], precision=HIGHEST, preferred_element_type=F32)
    lane = lax.broadcasted_iota(jnp.int32, logits.shape, 1)
    neg = jnp.float32(-jnp.inf)
    l1 = jnp.where(lane < N_EXPERTS, logits, neg)
    m1 = jnp.max(l1, axis=-1, keepdims=True)
    i1 = jnp.min(jnp.where(l1 == m1, lane, 128), axis=-1, keepdims=True)
    l2 = jnp.where(lane == i1, neg, l1)
    m2 = jnp.max(l2, axis=-1, keepdims=True)
    i2 = jnp.min(jnp.where(l2 == m2, lane, 128), axis=-1, keepdims=True)
    e = jnp.exp(m2 - m1)
    p1 = 1.0 / (1.0 + e)
    p2 = e / (1.0 + e)
    ids_ref[...] = jnp.where(lane == 0, i1, jnp.where(lane == 1, i2, 0))
    ps_ref[...] = jnp.where(lane == 0, p1, jnp.where(lane == 1, p2, 0.0))


def _router(x2, gain, w_router, tm=512):
    m = x2.shape[0]
    tm = min(tm, m)
    wr = jnp.pad(w_router.astype(F32), ((0, 0), (0, 128 - N_EXPERTS)))
    return pl.pallas_call(
        _router_kernel,
        grid=(m // tm,),
        in_specs=[pl.BlockSpec((tm, D_MODEL), lambda i: (i, 0)),
                  pl.BlockSpec((1, D_MODEL), lambda i: (0, 0)),
                  pl.BlockSpec((D_MODEL, 128), lambda i: (0, 0))],
        out_specs=[pl.BlockSpec((tm, D_MODEL), lambda i: (i, 0)),
                   pl.BlockSpec((tm, 128), lambda i: (i, 0)),
                   pl.BlockSpec((tm, 128), lambda i: (i, 0))],
        out_shape=[jax.ShapeDtypeStruct((m, D_MODEL), F32),
                   jax.ShapeDtypeStruct((m, 128), jnp.int32),
                   jax.ShapeDtypeStruct((m, 128), F32)],
        compiler_params=_cparams(("parallel",)),
        name="router",
    )(x2, gain.reshape(1, D_MODEL).astype(F32), wr)


def _route_positions(ids, tm):
    m = ids.shape[0]
    e_flat = ids[:, :TOP_K].reshape(-1)
    onehot = (e_flat[:, None] == jnp.arange(N_EXPERTS)[None, :]).astype(jnp.int32)
    csum = jnp.cumsum(onehot, axis=0)
    counts = csum[-1]
    gsz = ((counts + tm - 1) // tm) * tm
    gend = jnp.cumsum(gsz)
    pos = jnp.sum(onehot * (gend - gsz + csum - 1), axis=1)
    nt = (TOP_K * m + N_EXPERTS * tm) // tm
    n_used = gend[-1] // tm
    tile_e = jnp.sum((jnp.arange(nt)[:, None] * tm >= gend[None, :]).astype(jnp.int32), axis=1)
    last_e = jnp.sum(((n_used - 1) * tm >= gend).astype(jnp.int32))
    tile_e = jnp.minimum(tile_e, last_e)
    return pos.astype(jnp.int32), tile_e.astype(jnp.int32), n_used.reshape(1).astype(jnp.int32), nt


def _dispatch_kernel(pos_ref, h_ref, init_ref, xs_ref, sem):
    del init_ref
    td = h_ref.shape[0]

    def row_copy(t, s):
        return pltpu.make_async_copy(h_ref.at[pl.ds(t, 1)],
                                     xs_ref.at[pl.ds(pos_ref[0, 0, TOP_K * t + s], 1)], sem)

    def start(t, c):
        for s in range(TOP_K):
            row_copy(t, s).start()
        return c

    def wait(t, c):
        for s in range(TOP_K):
            row_copy(t, s).wait()
        return c

    lax.fori_loop(0, td, start, 0, unroll=8)
    lax.fori_loop(0, td, wait, 0, unroll=8)


def _dispatch(h, pos, npad, td=256):
    m = h.shape[0]
    td = min(td, m)
    return pl.pallas_call(
        _dispatch_kernel,
        grid=(m // td,),
        in_specs=[pl.BlockSpec((1, 1, TOP_K * td), lambda i: (i, 0, 0), memory_space=pltpu.SMEM),
                  pl.BlockSpec((td, D_MODEL), lambda i: (i, 0)),
                  pl.BlockSpec(memory_space=pl.ANY)],
        out_specs=pl.BlockSpec(memory_space=pl.ANY),
        out_shape=jax.ShapeDtypeStruct((npad, D_MODEL), F32),
        scratch_shapes=[pltpu.SemaphoreType.DMA(())],
        input_output_aliases={2: 0},
        compiler_params=_cparams(("arbitrary",)),
        name="moe_dispatch",
    )(pos.reshape(m // td, 1, TOP_K * td), h, jnp.zeros((npad, D_MODEL), F32))


def _experts_kernel(te_ref, nu_ref, xs_ref, w1_ref, w3_ref, w2_ref, y_ref, xb_ref, acc_ref):
    del te_ref
    i = pl.program_id(0)
    j = pl.program_id(1)

    @pl.when(i < nu_ref[0])
    def _():
        @pl.when(j == 0)
        def _():
            xb_ref[...] = xs_ref[...].astype(BF16)
            acc_ref[...] = jnp.zeros_like(acc_ref)

        xb = xb_ref[...]
        a = _dot(xb, w1_ref[0])
        b = _dot(xb, w3_ref[0])
        acc_ref[...] += _dot((_silu(a) * b).astype(BF16), w2_ref[0])

        @pl.when(j == pl.num_programs(1) - 1)
        def _():
            y_ref[...] = acc_ref[...]

    @pl.when((i >= nu_ref[0]) & (j == 0))
    def _():
        y_ref[...] = jnp.zeros_like(y_ref)


def _experts(xs, tile_e, n_used, w1, w3, w2, tm, tf=896):
    npad = xs.shape[0]
    ff = w1.shape[2]
    nj = ff // tf
    row = lambda i, j, te, nu: (jnp.minimum(i, nu[0] - 1), 0)
    jj = lambda i, j, nu: jnp.where(i < nu[0], j, nj - 1)
    return pl.pallas_call(
        _experts_kernel,
        grid_spec=pltpu.PrefetchScalarGridSpec(
            num_scalar_prefetch=2,
            grid=(npad // tm, nj),
            in_specs=[pl.BlockSpec((tm, D_MODEL), row),
                      pl.BlockSpec((1, D_MODEL, tf), lambda i, j, te, nu: (te[i], 0, jj(i, j, nu))),
                      pl.BlockSpec((1, D_MODEL, tf), lambda i, j, te, nu: (te[i], 0, jj(i, j, nu))),
                      pl.BlockSpec((1, tf, D_MODEL), lambda i, j, te, nu: (te[i], jj(i, j, nu), 0))],
            out_specs=pl.BlockSpec((tm, D_MODEL), lambda i, j, te, nu: (i, 0)),
            scratch_shapes=[pltpu.VMEM((tm, D_MODEL), BF16), pltpu.VMEM((tm, D_MODEL), F32)]),
        out_shape=jax.ShapeDtypeStruct((npad, D_MODEL), F32),
        compiler_params=_cparams(("arbitrary", "arbitrary")),
        name="moe_experts",
    )(tile_e, n_used, xs, w1.astype(BF16), w3.astype(BF16), w2.astype(BF16))


def _combine_kernel(pos_ref, x_ref, ps_ref, y_ref, o_ref, buf_ref, sem):
    td = x_ref.shape[0]

    def row_copy(t, s):
        return pltpu.make_async_copy(y_ref.at[pl.ds(pos_ref[0, 0, TOP_K * t + s], 1)],
                                     buf_ref.at[s, pl.ds(t, 1)], sem.at[s])

    def start(t, c):
        for s in range(TOP_K):
            row_copy(t, s).start()
        return c

    def wait(t, c):
        for s in range(TOP_K):
            row_copy(t, s).wait()
        return c

    lax.fori_loop(0, td, start, 0, unroll=8)
    lax.fori_loop(0, td, wait, 0, unroll=8)
    ps = ps_ref[...]
    o_ref[...] = x_ref[...] + ps[:, 0:1] * buf_ref[0] + ps[:, 1:2] * buf_ref[1]


def _combine(x2, ps, pos, y, td=256):
    m = x2.shape[0]
    td = min(td, m)
    return pl.pallas_call(
        _combine_kernel,
        grid=(m // td,),
        in_specs=[pl.BlockSpec((1, 1, TOP_K * td), lambda i: (i, 0, 0), memory_space=pltpu.SMEM),
                  pl.BlockSpec((td, D_MODEL), lambda i: (i, 0)),
                  pl.BlockSpec((td, 128), lambda i: (i, 0)),
                  pl.BlockSpec(memory_space=pl.ANY)],
        out_specs=pl.BlockSpec((td, D_MODEL), lambda i: (i, 0)),
        out_shape=jax.ShapeDtypeStruct((m, D_MODEL), F32),
        scratch_shapes=[pltpu.VMEM((TOP_K, td, D_MODEL), F32), pltpu.SemaphoreType.DMA((TOP_K,))],
        compiler_params=_cparams(("arbitrary",)),
        name="moe_combine",
    )(pos.reshape(m // td, 1, TOP_K * td), x2, ps, y)


def _moe(x2, gain, w_router, w1, w3, w2, tm=512):
    m = x2.shape[0]
    tm = min(tm, m)
    h, ids, ps = _router(x2, gain, w_router)
    pos, tile_e, n_used, nt = _route_positions(ids, tm)
    xs = _dispatch(h, pos, nt * tm)
    y = _experts(xs, tile_e, n_used, w1, w3, w2, tm)
    return _combine(x2, ps, pos, y)


def _relayout_w_in(w):
    k = w.shape[0]
    main = [w[:, _R_BRG:_R_END], w[:, _R_AQKV:_R_BETA], w[:, _R_AGATE:_R_B], w[:, _R_B:_R_C]]
    second = [w[:, _R_C:_R_BRG], w[:, _R_BETA:_R_AGATE],
              jnp.zeros((k, NP_C - OFF_BA - 2 * HEADS), w.dtype)]
    return (jnp.concatenate(main, axis=1).astype(BF16), jnp.concatenate(second, axis=1).astype(BF16))


def _layer_mixers(x2, b, t, layer, lower_bounds, w_in, norm_mix, conv_a, a_log, dt_bias,
                  gnorm_a, gnorm_b, qnorm_c, knorm_c, rel_bias, w_br_a, w_br_b, w_br_c, w_out):
    w_main, w_second = _relayout_w_in(w_in[layer])
    gain = norm_mix[layer].astype(F32)
    proj2 = _norm_proj(x2, gain, w_main, BF16)
    proj3 = proj2.reshape(b, t, NP_MAIN)
    projc3 = _norm_proj(x2, gain, w_second, F32).reshape(b, t, NP_C)
    oa = _deltanet(proj3, projc3, conv_a[layer], a_log[layer], dt_bias[layer], gnorm_a[layer])
    ob = _hgrn(proj3, lower_bounds[layer], gnorm_b[layer])
    ocs, lses = [], []
    for gi in range(C_GROUPS):
        o, lse = _dilated_group(projc3, gi, rel_bias, qnorm_c[layer, gi], knorm_c[layer, gi])
        ocs.append(o)
        lses.append(lse)
    return _mix(x2, proj2, oa.reshape(b * t, HW), ob.reshape(b * t, HW), ocs, lses,
                w_br_a[layer], w_br_b[layer], w_br_c[layer], w_out[layer])


def kernel(x, w_in, norm_mix, conv_a, a_log, dt_bias, gnorm_a, lb_logits, gnorm_b, qnorm_c, knorm_c, rel_bias, w_br_a, w_br_b, w_br_c, w_out, norm_ffn, ffn_w1, ffn_w3, ffn_w2, router, moe_w1, moe_w3, moe_w2):
    b, t, _ = x.shape
    depth = w_in.shape[0]
    p_lb = jax.nn.softmax(lb_logits.astype(F32), axis=0)
    lower_bounds = jnp.cumsum(p_lb, axis=0) - p_lb[0:1]
    x2 = x.reshape(b * t, D_MODEL).astype(F32)
    for layer in range(depth):
        x2 = _layer_mixers(x2, b, t, layer, lower_bounds, w_in, norm_mix, conv_a, a_log, dt_bias,
                           gnorm_a, gnorm_b, qnorm_c, knorm_c, rel_bias,
                           w_br_a, w_br_b, w_br_c, w_out)
        li = layer // 2
        if layer % 2 == 0:
            x2 = _ffn(x2, norm_ffn[layer], ffn_w1[li], ffn_w3[li], ffn_w2[li])
        else:
            x2 = _moe(x2, norm_ffn[layer], router[li], moe_w1[li], moe_w3[li], moe_w2[li])
    return x2.reshape(b, t, D_MODEL).astype(x.dtype)
```

```python
import functools
import math

import numpy as np
import jax
import jax.numpy as jnp
from jax import lax
from jax.experimental import pallas as pl
from jax.experimental.pallas import tpu as pltpu

F32 = jnp.float32
BF16 = jnp.bfloat16
HIGHEST = lax.Precision.HIGHEST

D_MODEL = 1024
EPS = 1e-6
HEADS = 4
DH = 128
HW = HEADS * DH
A_CONV = 4
A_CHUNK = 64
A_ROWS = 256
B_ROWS = 64
B_BLK = 16
C_PAIRS = ((128, 1), (512, 4), (2048, 16))
C_GROUPS = 3
C_BLOCK = 128
C_TILES = 4
REL_BUCKETS = 32
REL_MAX_DIST = 2048
N_EXPERTS = 8
TOP_K = 2

OFF_BRG = 0
OFF_AQKV = 3072
OFF_AGATE = 4608
OFF_B = 5120
NP_MAIN = 7168
OFF_C = 0
OFF_BA = 4608
NP_C = 5120

_R_AQKV, _R_BETA, _R_AGATE, _R_B, _R_C, _R_BRG, _R_END = 0, 1536, 1544, 2056, 4104, 8712, 11784

VMEM_LIMIT = 56 * 1024 * 1024


def _cparams(sem):
    return pltpu.CompilerParams(dimension_semantics=sem, vmem_limit_bytes=VMEM_LIMIT)


def _sigmoid(x):
    return 1.0 / (1.0 + jnp.exp(-x))


def _silu(x):
    return x * _sigmoid(x)


def _softplus(x):
    return jnp.maximum(x, 0.0) + jnp.log(1.0 + jnp.exp(-jnp.abs(x)))


def _dot(a, b):
    return jnp.dot(a, b, preferred_element_type=F32)


def _dot_nt(a, b, precision=None):
    return lax.dot_general(a, b, (((1,), (1,)), ((), ())), precision=precision,
                           preferred_element_type=F32)


def _dot_tn(a, b):
    return lax.dot_general(a, b, (((0,), (0,)), ((), ())), preferred_element_type=F32)


def _norm_proj_kernel(tn, x_ref, g_ref, w_ref, o_ref):
    x = x_ref[...]
    ms = jnp.mean(x * x, axis=-1, keepdims=True)
    h = (x * lax.rsqrt(ms + EPS) * g_ref[...]).astype(BF16)
    for c0 in range(0, w_ref.shape[1], tn):
        o_ref[:, c0:c0 + tn] = _dot(h, w_ref[:, c0:c0 + tn]).astype(o_ref.dtype)


def _norm_proj(x2, gain, w_bf16, out_dtype, tm=512, tn=1024):
    m = x2.shape[0]
    n = w_bf16.shape[1]
    tm = min(tm, m)
    return pl.pallas_call(
        functools.partial(_norm_proj_kernel, tn),
        grid=(m // tm,),
        in_specs=[pl.BlockSpec((tm, D_MODEL), lambda i: (i, 0)),
                  pl.BlockSpec((1, D_MODEL), lambda i: (0, 0)),
                  pl.BlockSpec((D_MODEL, n), lambda i: (0, 0), pipeline_mode=pl.Buffered(1))],
        out_specs=pl.BlockSpec((tm, n), lambda i: (i, 0)),
        out_shape=jax.ShapeDtypeStruct((m, n), out_dtype),
        compiler_params=_cparams(("parallel",)),
        name="norm_proj",
    )(x2, gain.reshape(1, D_MODEL), w_bf16)


def _dotb(a, b):
    return _dot(a.astype(BF16), b.astype(BF16))


def _dot_exact_lhs(a_bf16, b):
    b0 = b.astype(BF16)
    r1 = b - b0.astype(F32)
    b1 = r1.astype(BF16)
    b2 = (r1 - b1.astype(F32)).astype(BF16)
    return _dot(a_bf16, b0) + (_dot(a_bf16, b1) + _dot(a_bf16, b2))


def _deltanet_kernel(qkv_ref, gate_ref, ba_ref, convw_ref, arow_ref, dtrow_ref, gn_ref,
                     o_ref, s_ref, xe_ref):
    C = A_CHUNK
    R = A_ROWS
    HS = range(HEADS)
    IT = range((R // C) * HEADS)

    @pl.when(pl.program_id(1) == 0)
    def _():
        s_ref[...] = jnp.zeros_like(s_ref)
        xe_ref[0:8, :] = jnp.zeros((8, 3 * HW), F32)

    x = qkv_ref[0].astype(F32)
    xe_ref[8:8 + R, :] = x
    w = convw_ref[...]
    y = (w[3:4] * x + w[2:3] * xe_ref[7:7 + R, :] + w[1:2] * xe_ref[6:6 + R, :]
         + w[0:1] * xe_ref[5:5 + R, :])
    xe_ref[0:8, :] = x[R - 8:R]
    y = _silu(y)

    ba = ba_ref[0]
    beta_all = _sigmoid(ba)
    g_all = arow_ref[...] * _softplus(ba + dtrow_ref[...])
    gate = _silu(gate_ref[0].astype(F32))
    s_old = [s_ref[h] for h in HS]

    row = lax.broadcasted_iota(jnp.int32, (C, C), 0)
    col = lax.broadcasted_iota(jnp.int32, (C, C), 1)
    lmat = (col <= row).astype(BF16)
    rowx = lax.broadcasted_iota(jnp.int32, (C, DH + C), 0)
    colx = lax.broadcasted_iota(jnp.int32, (C, DH + C), 1)
    umask = (colx < DH) | (rowx > colx - DH)
    eye = (row == col).astype(F32)
    bd8 = (row >> 3) == (col >> 3)

    def merge_mask(sh):
        return (((row >> (sh + 1)) == (col >> (sh + 1)))
                & (((row >> sh) & 1) == 1) & (((col >> sh) & 1) == 0))

    rows = [slice((i // HEADS) * C, (i // HEADS + 1) * C) for i in IT]
    hd = [i % HEADS for i in IT]
    q = [y[rows[i], hd[i] * DH:(hd[i] + 1) * DH] for i in IT]
    k = [y[rows[i], HW + hd[i] * DH:HW + (hd[i] + 1) * DH] for i in IT]
    v = [y[rows[i], 2 * HW + hd[i] * DH:2 * HW + (hd[i] + 1) * DH] for i in IT]
    q = [a * (lax.rsqrt(jnp.sum(a * a, axis=-1, keepdims=True) + EPS) * (DH ** -0.5)) for a in q]
    k = [a * lax.rsqrt(jnp.sum(a * a, axis=-1, keepdims=True) + EPS) for a in k]
    beta = [beta_all[rows[i], hd[i]:hd[i] + 1] for i in IT]
    gb = [jnp.broadcast_to(g_all[rows[i], HEADS + hd[i]:HEADS + hd[i] + 1], (C, DH + C)) for i in IT]
    dext = [_dot_exact_lhs(lmat, jnp.where(umask, gb[i], 0.0)) for i in IT]
    gc = [d[:, :DH] for d in dext]
    edm = [jnp.exp(d[:, DH:]) for d in dext]
    egc = [jnp.exp(g) for g in gc]
    gl = [g[C - 1:C, :] for g in gc]
    kb = [k[i] * beta[i] for i in IT]
    m = [_dot_nt(kb[i].astype(BF16), k[i].astype(BF16)) * jnp.where(row > col, edm[i], 0.0)
         for i in IT]

    nd = [jnp.where(bd8, -a, 0.0) for a in m]
    p2 = [_dotb(a, a) for a in nd]
    p4 = [_dotb(a, a) for a in p2]
    x1 = [eye + nd[i] for i in IT]
    x1 = [x1[i] + _dotb(x1[i], p2[i]) for i in IT]
    xi = [x1[i] + _dotb(x1[i], p4[i]) for i in IT]
    for sh in (3, 4, 5):
        mm = merge_mask(sh)
        t = [_dotb(xi[i], jnp.where(mm, m[i], 0.0)) for i in IT]
        xi = [xi[i] - _dotb(t[i], xi[i]) for i in IT]

    rhs = [jnp.concatenate([v[i] * beta[i], kb[i] * egc[i]], axis=1) for i in IT]
    sol = [_dotb(xi[i], rhs[i]) for i in IT]
    attn = [(_dot_nt(q[i].astype(BF16), k[i].astype(BF16))
             * jnp.where(row >= col, edm[i], 0.0)).astype(BF16) for i in IT]
    qg = [(q[i] * egc[i]).astype(BF16) for i in IT]
    kg = [(k[i] * jnp.exp(gl[i] - gc[i])).astype(BF16) for i in IT]
    egl = [jnp.exp(a) for a in gl]

    s_cur = s_old
    for c in range(R // C):
        it = [c * HEADS + h for h in HS]
        sb = [a.astype(BF16) for a in s_cur]
        v_new = [(sol[i][:, :DH] - _dot(sol[i][:, DH:].astype(BF16), sb[h])).astype(BF16)
                 for h, i in enumerate(it)]
        o = [_dot(qg[i], sb[h]) + _dot(attn[i], v_new[h]) for h, i in enumerate(it)]
        s_cur = [s_cur[h] * egl[i] + _dot_tn(kg[i], v_new[h]) for h, i in enumerate(it)]
        o = [a * lax.rsqrt(jnp.mean(a * a, axis=-1, keepdims=True) + EPS) * gn_ref[...] for a in o]
        o_ref[0, c * C:(c + 1) * C, :] = (jnp.concatenate(o, axis=1)
                                          * gate[c * C:(c + 1) * C]).astype(o_ref.dtype)
    for h in HS:
        s_ref[h] = s_cur[h]


def _deltanet(proj3, projc3, conv_w, a_log, dt_bias, gnorm):
    b, t, _ = proj3.shape
    C = A_ROWS
    pad = jnp.zeros((HEADS,), F32)
    arow = jnp.concatenate([pad, -jnp.exp(a_log.astype(F32)), jnp.zeros((120,), F32)]).reshape(1, 128)
    dtrow = jnp.concatenate([pad, dt_bias.astype(F32), jnp.zeros((120,), F32)]).reshape(1, 128)
    const = lambda shape: pl.BlockSpec(shape, lambda i, c: (0,) * len(shape))
    return pl.pallas_call(
        _deltanet_kernel,
        grid=(b, t // C),
        in_specs=[pl.BlockSpec((1, C, 3 * HW), lambda i, c: (i, c, OFF_AQKV // (3 * HW))),
                  pl.BlockSpec((1, C, HW), lambda i, c: (i, c, OFF_AGATE // HW)),
                  pl.BlockSpec((1, C, 128), lambda i, c: (i, c, OFF_BA // 128)),
                  const((A_CONV, 3 * HW)), const((1, 128)), const((1, 128)), const((1, DH))],
        out_specs=pl.BlockSpec((1, C, HW), lambda i, c: (i, c, 0)),
        out_shape=jax.ShapeDtypeStruct((b, t, HW), BF16),
        scratch_shapes=[pltpu.VMEM((HEADS, DH, DH), F32), pltpu.VMEM((C + 8, 3 * HW), F32)],
        compiler_params=_cparams(("parallel", "arbitrary")),
        name="deltanet",
    )(proj3, proj3, projc3, conv_w.astype(F32), arow, dtrow, gnorm.reshape(1, DH).astype(F32))


def _hgrn_kernel(q_ref, f_ref, i_ref, g_ref, lb_ref, gn_ref, o_ref, st_ref):
    R, K = B_ROWS, B_BLK
    HS = range(HEADS)

    @pl.when(pl.program_id(1) == 0)
    def _():
        st_ref[...] = jnp.zeros_like(st_ref)

    row = lax.broadcasted_iota(jnp.int32, (K, K), 0)
    col = lax.broadcasted_iota(jnp.int32, (K, K), 1)
    lmat = (col <= row).astype(BF16)
    H8 = 8
    row8 = lax.broadcasted_iota(jnp.int32, (H8, DH), 0)
    sls = [slice(h * DH, (h + 1) * DH) for h in HS]
    lb = [lb_ref[:, sl] for sl in sls]
    st = [st_ref[h] for h in HS]

    for blk in range(R // K):
        rs = slice(blk * K, (blk + 1) * K)
        q = [_silu(q_ref[0, rs, sl].astype(F32)) for sl in sls]
        f = [lb[h] + (1.0 - lb[h]) * _sigmoid(f_ref[0, rs, sls[h]].astype(F32)) for h in HS]
        k = [1.0 - a for a in f]
        v = [i_ref[0, rs, sl].astype(F32) for sl in sls]
        gc = [_dot_exact_lhs(lmat, jnp.log(a)) for a in f]
        o = [_dot_nt((q[h] * jnp.exp(gc[h])).astype(BF16), st[h].astype(BF16)) for h in HS]
        otile = [[o[h][t * H8:(t + 1) * H8] for t in range(K // H8)] for h in HS]
        for j in range(K):
            for t in range(j // H8, K // H8):
                ts = slice(t * H8, (t + 1) * H8)
                for h in HS:
                    d = gc[h][ts] - gc[h][j:j + 1]
                    rel = jnp.exp(jnp.where(row8 >= j - t * H8, d, -1e30) if t == j // H8 else d)
                    sj = jnp.sum(q[h][ts] * k[h][j:j + 1] * rel, axis=-1, keepdims=True)
                    otile[h][t] = otile[h][t] + sj * v[h][j:j + 1]
        o = [jnp.concatenate(otile[h], axis=0) for h in HS]
        gl = [a[K - 1:K] for a in gc]
        kg = [(k[h] * jnp.exp(gl[h] - gc[h])).astype(BF16) for h in HS]
        st = [st[h] * jnp.exp(gl[h]) + _dot_tn(v[h].astype(BF16), kg[h]) for h in HS]
        o = [a * lax.rsqrt(jnp.mean(a * a, axis=-1, keepdims=True) + EPS) * gn_ref[...] for a in o]
        o_ref[0, rs, :] = (jnp.concatenate(o, axis=1) * _sigmoid(g_ref[0, rs, :].astype(F32))).astype(o_ref.dtype)
    for h in HS:
        st_ref[h] = st[h]


def _hgrn(proj3, lb, gnorm):
    b, t, _ = proj3.shape
    R = B_ROWS
    seg = lambda s: pl.BlockSpec((1, R, HW), lambda i, c: (i, c, OFF_B // HW + s))
    const = lambda shape: pl.BlockSpec(shape, lambda i, c: (0,) * len(shape))
    return pl.pallas_call(
        _hgrn_kernel,
        grid=(b, t // R),
        in_specs=[seg(0), seg(1), seg(2), seg(3), const((1, HW)), const((1, DH))],
        out_specs=pl.BlockSpec((1, R, HW), lambda i, c: (i, c, 0)),
        out_shape=jax.ShapeDtypeStruct((b, t, HW), BF16),
        scratch_shapes=[pltpu.VMEM((HEADS, DH, DH), F32)],
        compiler_params=_cparams(("parallel", "arbitrary")),
        name="hgrn2",
    )(proj3, proj3, proj3, proj3, lb.reshape(1, HW).astype(F32), gnorm.reshape(1, DH).astype(F32))


def _t5_bucket_np(n):
    max_exact = REL_BUCKETS // 2
    nf = np.maximum(n, 1).astype(np.float32)
    large = max_exact + (np.log(nf / max_exact) / math.log(REL_MAX_DIST / max_exact)
                         * (REL_BUCKETS - max_exact)).astype(np.int32)
    large = np.minimum(large, REL_BUCKETS - 1)
    return np.where(n < max_exact, n, large)


def _dilated_kernel(span, dil, hpb, q_ref, kp_ref, kc_ref, vp_ref, vc_ref, bias_ref, qg_ref, kg_ref,
                    o_ref, lse_ref):
    CB = C_BLOCK
    n = pl.program_id(1)
    qi = lax.broadcasted_iota(jnp.int32, (CB, 2 * CB), 0)
    kj = lax.broadcasted_iota(jnp.int32, (CB, 2 * CB), 1)
    dist = qi + CB - kj
    valid = (dist >= 0) & (dist <= span) & ((kj >= CB) | (n > 0))

    def qk_norm(x, g):
        return x * lax.rsqrt(jnp.mean(x * x, axis=-1, keepdims=True) + EPS) * g

    def tiles(items):
        n_it = range(len(items))
        q = [qk_norm(q_ref[0, rows, sl], qg_ref[...]).astype(BF16) for rows, sl, _ in items]
        k = [qk_norm(jnp.concatenate([kp_ref[0, rows, sl], kc_ref[0, rows, sl]], axis=0),
                     kg_ref[...]).astype(BF16) for rows, sl, _ in items]
        v = [jnp.concatenate([vp_ref[0, rows, sl], vc_ref[0, rows, sl]], axis=0).astype(BF16)
             for rows, sl, _ in items]
        s = [_dot_nt(q[i], k[i]) * (DH ** -0.5) + bias_ref[items[i][2]] for i in n_it]
        s = [jnp.where(valid, a, -1e30) for a in s]
        mx = [jnp.max(a, axis=-1, keepdims=True) for a in s]
        p = [jnp.exp(s[i] - mx[i]) for i in n_it]
        den = [jnp.sum(a, axis=-1, keepdims=True) for a in p]
        o = [_dot(p[i].astype(BF16), v[i]) / den[i] for i in n_it]
        for i, (rows, sl, _) in enumerate(items):
            o_ref[0, rows, sl] = o[i]
            lse_ref[0, rows, sl] = jnp.broadcast_to(mx[i] + jnp.log(den[i]), (CB, DH))

    if dil == 1:
        for h0 in range(0, hpb, C_TILES):
            tiles([(pl.ds(0, CB), slice(h * DH, (h + 1) * DH), h) for h in range(h0, h0 + C_TILES)])
    else:
        def body(g, carry):
            tiles([(pl.ds(g * C_TILES + i, CB, stride=dil), slice(0, DH), 0) for i in range(C_TILES)])
            return carry
        lax.fori_loop(0, dil // C_TILES, body, 0)


def _dilated_group(proj3, gi, rel_bias, q_gain, k_gain):
    b, t, _ = proj3.shape
    window, dil = C_PAIRS[gi]
    span = window // dil
    CB = C_BLOCK
    rb = CB * dil
    hpb = HEADS if dil == 1 else 1
    bw = hpb * DH
    qi = np.arange(CB)[:, None]
    kj = np.arange(2 * CB)[None, :]
    bucket = _t5_bucket_np(np.maximum(qi + CB - kj, 0) * dil)
    onehot = jnp.asarray(np.eye(REL_BUCKETS, dtype=np.float32)[bucket])
    bias = jnp.einsum("qkb,bh->hqk", onehot, rel_bias[:, gi * HEADS:(gi + 1) * HEADS].astype(F32),
                      precision=HIGHEST)

    def seg(which, prev):
        base = (OFF_C + which * C_GROUPS * HW + gi * HW) // bw
        if prev:
            return pl.BlockSpec((1, rb, bw), lambda i, n, h: (i, jnp.maximum(n - 1, 0), base + h))
        return pl.BlockSpec((1, rb, bw), lambda i, n, h: (i, n, base + h))

    const = lambda shape: pl.BlockSpec(shape, lambda i, n, h: (0,) * len(shape))
    o, lse = pl.pallas_call(
        functools.partial(_dilated_kernel, span, dil, hpb),
        grid=(b, t // rb, HEADS // hpb),
        in_specs=[seg(0, False), seg(1, True), seg(1, False), seg(2, True), seg(2, False),
                  pl.BlockSpec((hpb, CB, 2 * CB), lambda i, n, h: (h, 0, 0)),
                  const((1, DH)), const((1, DH))],
        out_specs=[pl.BlockSpec((1, rb, bw), lambda i, n, h: (i, n, h)),
                   pl.BlockSpec((1, rb, bw), lambda i, n, h: (i, n, h))],
        out_shape=[jax.ShapeDtypeStruct((b, t, HW), F32),
                   jax.ShapeDtypeStruct((b, t, HW), F32)],
        compiler_params=_cparams(("parallel", "arbitrary", "arbitrary")),
        name=f"dilated_g{gi}",
    )(proj3, proj3, proj3, proj3, proj3, bias,
      q_gain.reshape(1, DH).astype(F32), k_gain.reshape(1, DH).astype(F32))
    return o.reshape(b * t, HW), lse.reshape(b * t, HW)


def _mix_kernel(x_ref, gate_ref, oa_ref, ob_ref, oc0_ref, oc1_ref, oc2_ref,
                l0_ref, l1_ref, l2_ref, wa_ref, wb_ref, wc_ref, wo_ref, out_ref):
    l0, l1, l2 = l0_ref[...], l1_ref[...], l2_ref[...]
    mx = jnp.maximum(jnp.maximum(l0, l1), l2)
    e0, e1, e2 = jnp.exp(l0 - mx), jnp.exp(l1 - mx), jnp.exp(l2 - mx)
    oc = (e0 * oc0_ref[...] + e1 * oc1_ref[...] + e2 * oc2_ref[...]) / (e0 + e1 + e2)
    mix = (_sigmoid(gate_ref[:, 0:D_MODEL].astype(F32)) * _dot(oa_ref[...], wa_ref[...])
           + _sigmoid(gate_ref[:, D_MODEL:2 * D_MODEL].astype(F32)) * _dot(ob_ref[...], wb_ref[...])
           + _sigmoid(gate_ref[:, 2 * D_MODEL:3 * D_MODEL].astype(F32)) * _dot(oc.astype(BF16), wc_ref[...]))
    out_ref[...] = x_ref[...] + _dot(mix.astype(BF16), wo_ref[...])


def _mix(x2, proj2, oa, ob, ocs, lses, wa, wb, wc, wo, tm=256):
    m = x2.shape[0]
    tm = min(tm, m)
    rowblk = lambda w: pl.BlockSpec((tm, w), lambda i: (i, 0))
    const = lambda shape: pl.BlockSpec(shape, lambda i: (0,) * len(shape))
    return pl.pallas_call(
        _mix_kernel,
        grid=(m // tm,),
        in_specs=[rowblk(D_MODEL), rowblk(3 * D_MODEL), rowblk(HW), rowblk(HW),
                  rowblk(HW), rowblk(HW), rowblk(HW), rowblk(HW), rowblk(HW), rowblk(HW),
                  const((HW, D_MODEL)), const((HW, D_MODEL)), const((HW, D_MODEL)),
                  const((D_MODEL, D_MODEL))],
        out_specs=rowblk(D_MODEL),
        out_shape=jax.ShapeDtypeStruct((m, D_MODEL), F32),
        compiler_params=_cparams(("parallel",)),
        name="branch_mix",
    )(x2, proj2, oa, ob, ocs[0], ocs[1], ocs[2], lses[0], lses[1], lses[2],
      wa.astype(BF16), wb.astype(BF16), wc.astype(BF16), wo.astype(BF16))


def _ffn_kernel(x_ref, g_ref, w1_ref, w3_ref, w2_ref, o_ref, h_ref, acc_ref):
    j = pl.program_id(1)

    @pl.when(j == 0)
    def _():
        x = x_ref[...]
        ms = jnp.mean(x * x, axis=-1, keepdims=True)
        h_ref[...] = (x * lax.rsqrt(ms + EPS) * g_ref[...]).astype(BF16)
        acc_ref[...] = jnp.zeros_like(acc_ref)

    h = h_ref[...]
    a = _dot(h, w1_ref[...])
    b = _dot(h, w3_ref[...])
    acc_ref[...] += _dot((_silu(a) * b).astype(BF16), w2_ref[...])

    @pl.when(j == pl.num_programs(1) - 1)
    def _():
        o_ref[...] = x_ref[...] + acc_ref[...]


def _ffn(x2, gain, w1, w3, w2, tm=512, tf=1408):
    m = x2.shape[0]
    ff = w1.shape[1]
    tm = min(tm, m)
    return pl.pallas_call(
        _ffn_kernel,
        grid=(m // tm, ff // tf),
        in_specs=[pl.BlockSpec((tm, D_MODEL), lambda i, j: (i, 0)),
                  pl.BlockSpec((1, D_MODEL), lambda i, j: (0, 0)),
                  pl.BlockSpec((D_MODEL, tf), lambda i, j: (0, j)),
                  pl.BlockSpec((D_MODEL, tf), lambda i, j: (0, j)),
                  pl.BlockSpec((tf, D_MODEL), lambda i, j: (j, 0))],
        out_specs=pl.BlockSpec((tm, D_MODEL), lambda i, j: (i, 0)),
        out_shape=jax.ShapeDtypeStruct((m, D_MODEL), F32),
        scratch_shapes=[pltpu.VMEM((tm, D_MODEL), BF16), pltpu.VMEM((tm, D_MODEL), F32)],
        compiler_params=_cparams(("parallel", "arbitrary")),
        name="ffn",
    )(x2, gain.reshape(1, D_MODEL).astype(F32), w1.astype(BF16), w3.astype(BF16), w2.astype(BF16))


def _router_kernel(x_ref, g_ref, wr_ref, h_ref, ids_ref, ps_ref):
    x = x_ref[...]
    ms = jnp.mean(x * x, axis=-1, keepdims=True)
    h = x * lax.rsqrt(ms + EPS) * g_ref[...]
    h_ref[...] = h
    logits = jnp.dot(h, wr_ref[...], precision=HIGHEST, preferred_element_type=F32)
    lane = lax.broadcasted_iota(jnp.int32, logits.shape, 1)
    neg = jnp.float32(-jnp.inf)
    l1 = jnp.where(lane < N_EXPERTS, logits, neg)
    m1 = jnp.max(l1, axis=-1, keepdims=True)
    i1 = jnp.min(jnp.where(l1 == m1, lane, 128), axis=-1, keepdims=True)
    l2 = jnp.where(lane == i1, neg, l1)
    m2 = jnp.max(l2, axis=-1, keepdims=True)
    i2 = jnp.min(jnp.where(l2 == m2, lane, 128), axis=-1, keepdims=True)
    e = jnp.exp(m2 - m1)
    p1 = 1.0 / (1.0 + e)
    p2 = e / (1.0 + e)
    ids_ref[...] = jnp.where(lane == 0, i1, jnp.where(lane == 1, i2, 0))
    ps_ref[...] = jnp.where(lane == 0, p1, jnp.where(lane == 1, p2, 0.0))


def _router(x2, gain, w_router, tm=512):
    m = x2.shape[0]
    tm = min(tm, m)
    wr = jnp.pad(w_router.astype(F32), ((0, 0), (0, 128 - N_EXPERTS)))
    return pl.pallas_call(
        _router_kernel,
        grid=(m // tm,),
        in_specs=[pl.BlockSpec((tm, D_MODEL), lambda i: (i, 0)),
                  pl.BlockSpec((1, D_MODEL), lambda i: (0, 0)),
                  pl.BlockSpec((D_MODEL, 128), lambda i: (0, 0))],
        out_specs=[pl.BlockSpec((tm, D_MODEL), lambda i: (i, 0)),
                   pl.BlockSpec((tm, 128), lambda i: (i, 0)),
                   pl.BlockSpec((tm, 128), lambda i: (i, 0))],
        out_shape=[jax.ShapeDtypeStruct((m, D_MODEL), F32),
                   jax.ShapeDtypeStruct((m, 128), jnp.int32),
                   jax.ShapeDtypeStruct((m, 128), F32)],
        compiler_params=_cparams(("parallel",)),
        name="router",
    )(x2, gain.reshape(1, D_MODEL).astype(F32), wr)


def _route_positions(ids, tm):
    m = ids.shape[0]
    e_flat = ids[:, :TOP_K].reshape(-1)
    onehot = (e_flat[:, None] == jnp.arange(N_EXPERTS)[None, :]).astype(jnp.int32)
    csum = jnp.cumsum(onehot, axis=0)
    counts = csum[-1]
    gsz = ((counts + tm - 1) // tm) * tm
    gend = jnp.cumsum(gsz)
    pos = jnp.sum(onehot * (gend - gsz + csum - 1), axis=1)
    nt = (TOP_K * m + N_EXPERTS * tm) // tm
    n_used = gend[-1] // tm
    tile_e = jnp.sum((jnp.arange(nt)[:, None] * tm >= gend[None, :]).astype(jnp.int32), axis=1)
    last_e = jnp.sum(((n_used - 1) * tm >= gend).astype(jnp.int32))
    tile_e = jnp.minimum(tile_e, last_e)
    return pos.astype(jnp.int32), tile_e.astype(jnp.int32), n_used.reshape(1).astype(jnp.int32), nt


def _dispatch_kernel(pos_ref, h_ref, init_ref, xs_ref, sem):
    del init_ref
    td = h_ref.shape[0]

    def row_copy(t, s):
        return pltpu.make_async_copy(h_ref.at[pl.ds(t, 1)],
                                     xs_ref.at[pl.ds(pos_ref[0, 0, TOP_K * t + s], 1)], sem)

    def start(t, c):
        for s in range(TOP_K):
            row_copy(t, s).start()
        return c

    def wait(t, c):
        for s in range(TOP_K):
            row_copy(t, s).wait()
        return c

    lax.fori_loop(0, td, start, 0, unroll=8)
    lax.fori_loop(0, td, wait, 0, unroll=8)


def _dispatch(h, pos, npad, td=256):
    m = h.shape[0]
    td = min(td, m)
    return pl.pallas_call(
        _dispatch_kernel,
        grid=(m // td,),
        in_specs=[pl.BlockSpec((1, 1, TOP_K * td), lambda i: (i, 0, 0), memory_space=pltpu.SMEM),
                  pl.BlockSpec((td, D_MODEL), lambda i: (i, 0)),
                  pl.BlockSpec(memory_space=pl.ANY)],
        out_specs=pl.BlockSpec(memory_space=pl.ANY),
        out_shape=jax.ShapeDtypeStruct((npad, D_MODEL), F32),
        scratch_shapes=[pltpu.SemaphoreType.DMA(())],
        input_output_aliases={2: 0},
        compiler_params=_cparams(("arbitrary",)),
        name="moe_dispatch",
    )(pos.reshape(m // td, 1, TOP_K * td), h, jnp.zeros((npad, D_MODEL), F32))


def _experts_kernel(te_ref, nu_ref, xs_ref, w1_ref, w3_ref, w2_ref, y_ref, xb_ref, acc_ref):
    del te_ref
    i = pl.program_id(0)
    j = pl.program_id(1)

    @pl.when(i < nu_ref[0])
    def _():
        @pl.when(j == 0)
        def _():
            xb_ref[...] = xs_ref[...].astype(BF16)
            acc_ref[...] = jnp.zeros_like(acc_ref)

        xb = xb_ref[...]
        a = _dot(xb, w1_ref[0])
        b = _dot(xb, w3_ref[0])
        acc_ref[...] += _dot((_silu(a) * b).astype(BF16), w2_ref[0])

        @pl.when(j == pl.num_programs(1) - 1)
        def _():
            y_ref[...] = acc_ref[...]

    @pl.when((i >= nu_ref[0]) & (j == 0))
    def _():
        y_ref[...] = jnp.zeros_like(y_ref)


def _experts(xs, tile_e, n_used, w1, w3, w2, tm, tf=896):
    npad = xs.shape[0]
    ff = w1.shape[2]
    nj = ff // tf
    row = lambda i, j, te, nu: (jnp.minimum(i, nu[0] - 1), 0)
    jj = lambda i, j, nu: jnp.where(i < nu[0], j, nj - 1)
    return pl.pallas_call(
        _experts_kernel,
        grid_spec=pltpu.PrefetchScalarGridSpec(
            num_scalar_prefetch=2,
            grid=(npad // tm, nj),
            in_specs=[pl.BlockSpec((tm, D_MODEL), row),
                      pl.BlockSpec((1, D_MODEL, tf), lambda i, j, te, nu: (te[i], 0, jj(i, j, nu))),
                      pl.BlockSpec((1, D_MODEL, tf), lambda i, j, te, nu: (te[i], 0, jj(i, j, nu))),
                      pl.BlockSpec((1, tf, D_MODEL), lambda i, j, te, nu: (te[i], jj(i, j, nu), 0))],
            out_specs=pl.BlockSpec((tm, D_MODEL), lambda i, j, te, nu: (i, 0)),
            scratch_shapes=[pltpu.VMEM((tm, D_MODEL), BF16), pltpu.VMEM((tm, D_MODEL), F32)]),
        out_shape=jax.ShapeDtypeStruct((npad, D_MODEL), F32),
        compiler_params=_cparams(("arbitrary", "arbitrary")),
        name="moe_experts",
    )(tile_e, n_used, xs, w1.astype(BF16), w3.astype(BF16), w2.astype(BF16))


def _combine_kernel(pos_ref, x_ref, ps_ref, y_ref, o_ref, buf_ref, sem):
    td = x_ref.shape[0]

    def row_copy(t, s):
        return pltpu.make_async_copy(y_ref.at[pl.ds(pos_ref[0, 0, TOP_K * t + s], 1)],
                                     buf_ref.at[s, pl.ds(t, 1)], sem.at[s])

    def start(t, c):
        for s in range(TOP_K):
            row_copy(t, s).start()
        return c

    def wait(t, c):
        for s in range(TOP_K):
            row_copy(t, s).wait()
        return c

    lax.fori_loop(0, td, start, 0, unroll=8)
    lax.fori_loop(0, td, wait, 0, unroll=8)
    ps = ps_ref[...]
    o_ref[...] = x_ref[...] + ps[:, 0:1] * buf_ref[0] + ps[:, 1:2] * buf_ref[1]


def _combine(x2, ps, pos, y, td=256):
    m = x2.shape[0]
    td = min(td, m)
    return pl.pallas_call(
        _combine_kernel,
        grid=(m // td,),
        in_specs=[pl.BlockSpec((1, 1, TOP_K * td), lambda i: (i, 0, 0), memory_space=pltpu.SMEM),
                  pl.BlockSpec((td, D_MODEL), lambda i: (i, 0)),
                  pl.BlockSpec((td, 128), lambda i: (i, 0)),
                  pl.BlockSpec(memory_space=pl.ANY)],
        out_specs=pl.BlockSpec((td, D_MODEL), lambda i: (i, 0)),
        out_shape=jax.ShapeDtypeStruct((m, D_MODEL), F32),
        scratch_shapes=[pltpu.VMEM((TOP_K, td, D_MODEL), F32), pltpu.SemaphoreType.DMA((TOP_K,))],
        compiler_params=_cparams(("arbitrary",)),
        name="moe_combine",
    )(pos.reshape(m // td, 1, TOP_K * td), x2, ps, y)


def _moe(x2, gain, w_router, w1, w3, w2, tm=1024):
    m = x2.shape[0]
    tm = min(tm, m)
    h, ids, ps = _router(x2, gain, w_router)
    pos, tile_e, n_used, nt = _route_positions(ids, tm)
    xs = _dispatch(h, pos, nt * tm)
    y = _experts(xs, tile_e, n_used, w1, w3, w2, tm)
    return _combine(x2, ps, pos, y)


def _relayout_w_in(w):
    k = w.shape[0]
    main = [w[:, _R_BRG:_R_END], w[:, _R_AQKV:_R_BETA], w[:, _R_AGATE:_R_B], w[:, _R_B:_R_C]]
    second = [w[:, _R_C:_R_BRG], w[:, _R_BETA:_R_AGATE],
              jnp.zeros((k, NP_C - OFF_BA - 2 * HEADS), w.dtype)]
    return (jnp.concatenate(main, axis=1).astype(BF16), jnp.concatenate(second, axis=1).astype(BF16))


def _layer_mixers(x2, b, t, layer, lower_bounds, w_in, norm_mix, conv_a, a_log, dt_bias,
                  gnorm_a, gnorm_b, qnorm_c, knorm_c, rel_bias, w_br_a, w_br_b, w_br_c, w_out):
    w_main, w_second = _relayout_w_in(w_in[layer])
    gain = norm_mix[layer].astype(F32)
    proj2 = _norm_proj(x2, gain, w_main, BF16)
    proj3 = proj2.reshape(b, t, NP_MAIN)
    projc3 = _norm_proj(x2, gain, w_second, F32).reshape(b, t, NP_C)
    oa = _deltanet(proj3, projc3, conv_a[layer], a_log[layer], dt_bias[layer], gnorm_a[layer])
    ob = _hgrn(proj3, lower_bounds[layer], gnorm_b[layer])
    ocs, lses = [], []
    for gi in range(C_GROUPS):
        o, lse = _dilated_group(projc3, gi, rel_bias, qnorm_c[layer, gi], knorm_c[layer, gi])
        ocs.append(o)
        lses.append(lse)
    return _mix(x2, proj2, oa.reshape(b * t, HW), ob.reshape(b * t, HW), ocs, lses,
                w_br_a[layer], w_br_b[layer], w_br_c[layer], w_out[layer])


def kernel(x, w_in, norm_mix, conv_a, a_log, dt_bias, gnorm_a, lb_logits, gnorm_b, qnorm_c, knorm_c, rel_bias, w_br_a, w_br_b, w_br_c, w_out, norm_ffn, ffn_w1, ffn_w3, ffn_w2, router, moe_w1, moe_w3, moe_w2):
    b, t, _ = x.shape
    depth = w_in.shape[0]
    p_lb = jax.nn.softmax(lb_logits.astype(F32), axis=0)
    lower_bounds = jnp.cumsum(p_lb, axis=0) - p_lb[0:1]
    x2 = x.reshape(b * t, D_MODEL).astype(F32)
    for layer in range(depth):
        x2 = _layer_mixers(x2, b, t, layer, lower_bounds, w_in, norm_mix, conv_a, a_log, dt_bias,
                           gnorm_a, gnorm_b, qnorm_c, knorm_c, rel_bias,
                           w_br_a, w_br_b, w_br_c, w_out)
        li = layer // 2
        if layer % 2 == 0:
            x2 = _ffn(x2, norm_ffn[layer], ffn_w1[li], ffn_w3[li], ffn_w2[li])
        else:
            x2 = _moe(x2, norm_ffn[layer], router[li], moe_w1[li], moe_w3[li], moe_w2[li])
    return x2.reshape(b, t, D_MODEL).astype(x.dtype)
```

```python
import functools
import math

import numpy as np
import jax
import jax.numpy as jnp
from jax import lax
from jax.experimental import pallas as pl
from jax.experimental.pallas import tpu as pltpu

F32 = jnp.float32
BF16 = jnp.bfloat16
HIGHEST = lax.Precision.HIGHEST

D_MODEL = 1024
EPS = 1e-6
HEADS = 4
DH = 128
HW = HEADS * DH
A_CONV = 4
A_CHUNK = 64
A_ROWS = 256
B_ROWS = 64
B_BLK = 16
C_PAIRS = ((128, 1), (512, 4), (2048, 16))
C_GROUPS = 3
C_BLOCK = 128
C_TILES = 4
REL_BUCKETS = 32
REL_MAX_DIST = 2048
N_EXPERTS = 8
TOP_K = 2

OFF_BRG = 0
OFF_AQKV = 3072
OFF_AGATE = 4608
OFF_B = 5120
NP_MAIN = 7168
OFF_C = 0
OFF_BA = 4608
NP_C = 5120

_R_AQKV, _R_BETA, _R_AGATE, _R_B, _R_C, _R_BRG, _R_END = 0, 1536, 1544, 2056, 4104, 8712, 11784

VMEM_LIMIT = 56 * 1024 * 1024


def _cparams(sem):
    return pltpu.CompilerParams(dimension_semantics=sem, vmem_limit_bytes=VMEM_LIMIT)


def _sigmoid(x):
    return 1.0 / (1.0 + jnp.exp(-x))


def _silu(x):
    return x * _sigmoid(x)


def _softplus(x):
    return jnp.maximum(x, 0.0) + jnp.log(1.0 + jnp.exp(-jnp.abs(x)))


def _dot(a, b):
    return jnp.dot(a, b, preferred_element_type=F32)


def _dot_nt(a, b, precision=None):
    return lax.dot_general(a, b, (((1,), (1,)), ((), ())), precision=precision,
                           preferred_element_type=F32)


def _dot_tn(a, b):
    return lax.dot_general(a, b, (((0,), (0,)), ((), ())), preferred_element_type=F32)


def _norm_proj_kernel(tn, x_ref, g_ref, w_ref, o_ref):
    x = x_ref[...]
    ms = jnp.mean(x * x, axis=-1, keepdims=True)
    h = (x * lax.rsqrt(ms + EPS) * g_ref[...]).astype(BF16)
    for c0 in range(0, w_ref.shape[1], tn):
        o_ref[:, c0:c0 + tn] = _dot(h, w_ref[:, c0:c0 + tn]).astype(o_ref.dtype)


def _norm_proj(x2, gain, w_bf16, out_dtype, tm=512, tn=1024):
    m = x2.shape[0]
    n = w_bf16.shape[1]
    tm = min(tm, m)
    return pl.pallas_call(
        functools.partial(_norm_proj_kernel, tn),
        grid=(m // tm,),
        in_specs=[pl.BlockSpec((tm, D_MODEL), lambda i: (i, 0)),
                  pl.BlockSpec((1, D_MODEL), lambda i: (0, 0)),
                  pl.BlockSpec((D_MODEL, n), lambda i: (0, 0), pipeline_mode=pl.Buffered(1))],
        out_specs=pl.BlockSpec((tm, n), lambda i: (i, 0)),
        out_shape=jax.ShapeDtypeStruct((m, n), out_dtype),
        compiler_params=_cparams(("parallel",)),
        name="norm_proj",
    )(x2, gain.reshape(1, D_MODEL), w_bf16)


def _dotb(a, b):
    return _dot(a.astype(BF16), b.astype(BF16))


def _dot_exact_lhs(a_bf16, b):
    b0 = b.astype(BF16)
    r1 = b - b0.astype(F32)
    b1 = r1.astype(BF16)
    b2 = (r1 - b1.astype(F32)).astype(BF16)
    return _dot(a_bf16, b0) + (_dot(a_bf16, b1) + _dot(a_bf16, b2))


def _deltanet_kernel(qkv_ref, gate_ref, ba_ref, convw_ref, arow_ref, dtrow_ref, gn_ref,
                     o_ref, s_ref, xe_ref):
    C = A_CHUNK
    R = A_ROWS
    HS = range(HEADS)
    IT = range((R // C) * HEADS)

    @pl.when(pl.program_id(1) == 0)
    def _():
        s_ref[...] = jnp.zeros_like(s_ref)
        xe_ref[0:8, :] = jnp.zeros((8, 3 * HW), F32)

    x = qkv_ref[0].astype(F32)
    xe_ref[8:8 + R, :] = x
    w = convw_ref[...]
    y = (w[3:4] * x + w[2:3] * xe_ref[7:7 + R, :] + w[1:2] * xe_ref[6:6 + R, :]
         + w[0:1] * xe_ref[5:5 + R, :])
    xe_ref[0:8, :] = x[R - 8:R]
    y = _silu(y)

    ba = ba_ref[0]
    beta_all = _sigmoid(ba)
    g_all = arow_ref[...] * _softplus(ba + dtrow_ref[...])
    gate = _silu(gate_ref[0].astype(F32))
    s_old = [s_ref[h] for h in HS]

    row = lax.broadcasted_iota(jnp.int32, (C, C), 0)
    col = lax.broadcasted_iota(jnp.int32, (C, C), 1)
    lmat = (col <= row).astype(BF16)
    rowx = lax.broadcasted_iota(jnp.int32, (C, DH + C), 0)
    colx = lax.broadcasted_iota(jnp.int32, (C, DH + C), 1)
    umask = (colx < DH) | (rowx > colx - DH)
    eye = (row == col).astype(F32)
    bd8 = (row >> 3) == (col >> 3)

    def merge_mask(sh):
        return (((row >> (sh + 1)) == (col >> (sh + 1)))
                & (((row >> sh) & 1) == 1) & (((col >> sh) & 1) == 0))

    rows = [slice((i // HEADS) * C, (i // HEADS + 1) * C) for i in IT]
    hd = [i % HEADS for i in IT]
    q = [y[rows[i], hd[i] * DH:(hd[i] + 1) * DH] for i in IT]
    k = [y[rows[i], HW + hd[i] * DH:HW + (hd[i] + 1) * DH] for i in IT]
    v = [y[rows[i], 2 * HW + hd[i] * DH:2 * HW + (hd[i] + 1) * DH] for i in IT]
    q = [a * (lax.rsqrt(jnp.sum(a * a, axis=-1, keepdims=True) + EPS) * (DH ** -0.5)) for a in q]
    k = [a * lax.rsqrt(jnp.sum(a * a, axis=-1, keepdims=True) + EPS) for a in k]
    beta = [beta_all[rows[i], hd[i]:hd[i] + 1] for i in IT]
    gb = [jnp.broadcast_to(g_all[rows[i], HEADS + hd[i]:HEADS + hd[i] + 1], (C, DH + C)) for i in IT]
    dext = [_dot_exact_lhs(lmat, jnp.where(umask, gb[i], 0.0)) for i in IT]
    gc = [d[:, :DH] for d in dext]
    edm = [jnp.exp(d[:, DH:]) for d in dext]
    egc = [jnp.exp(g) for g in gc]
    gl = [g[C - 1:C, :] for g in gc]
    kb = [k[i] * beta[i] for i in IT]
    m = [_dot_nt(kb[i].astype(BF16), k[i].astype(BF16)) * jnp.where(row > col, edm[i], 0.0)
         for i in IT]

    nd = [jnp.where(bd8, -a, 0.0) for a in m]
    p2 = [_dotb(a, a) for a in nd]
    p4 = [_dotb(a, a) for a in p2]
    x1 = [eye + nd[i] for i in IT]
    x1 = [x1[i] + _dotb(x1[i], p2[i]) for i in IT]
    xi = [x1[i] + _dotb(x1[i], p4[i]) for i in IT]
    for sh in (3, 4, 5):
        mm = merge_mask(sh)
        t = [_dotb(xi[i], jnp.where(mm, m[i], 0.0)) for i in IT]
        xi = [xi[i] - _dotb(t[i], xi[i]) for i in IT]

    rhs = [jnp.concatenate([v[i] * beta[i], kb[i] * egc[i]], axis=1) for i in IT]
    sol = [_dotb(xi[i], rhs[i]) for i in IT]
    attn = [(_dot_nt(q[i].astype(BF16), k[i].astype(BF16))
             * jnp.where(row >= col, edm[i], 0.0)).astype(BF16) for i in IT]
    qg = [(q[i] * egc[i]).astype(BF16) for i in IT]
    kg = [(k[i] * jnp.exp(gl[i] - gc[i])).astype(BF16) for i in IT]
    egl = [jnp.exp(a) for a in gl]

    s_cur = s_old
    for c in range(R // C):
        it = [c * HEADS + h for h in HS]
        sb = [a.astype(BF16) for a in s_cur]
        v_new = [(sol[i][:, :DH] - _dot(sol[i][:, DH:].astype(BF16), sb[h])).astype(BF16)
                 for h, i in enumerate(it)]
        o = [_dot(qg[i], sb[h]) + _dot(attn[i], v_new[h]) for h, i in enumerate(it)]
        s_cur = [s_cur[h] * egl[i] + _dot_tn(kg[i], v_new[h]) for h, i in enumerate(it)]
        o = [a * lax.rsqrt(jnp.mean(a * a, axis=-1, keepdims=True) + EPS) * gn_ref[...] for a in o]
        o_ref[0, c * C:(c + 1) * C, :] = (jnp.concatenate(o, axis=1)
                                          * gate[c * C:(c + 1) * C]).astype(o_ref.dtype)
    for h in HS:
        s_ref[h] = s_cur[h]


def _deltanet(proj3, projc3, conv_w, a_log, dt_bias, gnorm):
    b, t, _ = proj3.shape
    C = A_ROWS
    pad = jnp.zeros((HEADS,), F32)
    arow = jnp.concatenate([pad, -jnp.exp(a_log.astype(F32)), jnp.zeros((120,), F32)]).reshape(1, 128)
    dtrow = jnp.concatenate([pad, dt_bias.astype(F32), jnp.zeros((120,), F32)]).reshape(1, 128)
    const = lambda shape: pl.BlockSpec(shape, lambda i, c: (0,) * len(shape))
    return pl.pallas_call(
        _deltanet_kernel,
        grid=(b, t // C),
        in_specs=[pl.BlockSpec((1, C, 3 * HW), lambda i, c: (i, c, OFF_AQKV // (3 * HW))),
                  pl.BlockSpec((1, C, HW), lambda i, c: (i, c, OFF_AGATE // HW)),
                  pl.BlockSpec((1, C, 128), lambda i, c: (i, c, OFF_BA // 128)),
                  const((A_CONV, 3 * HW)), const((1, 128)), const((1, 128)), const((1, DH))],
        out_specs=pl.BlockSpec((1, C, HW), lambda i, c: (i, c, 0)),
        out_shape=jax.ShapeDtypeStruct((b, t, HW), BF16),
        scratch_shapes=[pltpu.VMEM((HEADS, DH, DH), F32), pltpu.VMEM((C + 8, 3 * HW), F32)],
        compiler_params=_cparams(("parallel", "arbitrary")),
        name="deltanet",
    )(proj3, proj3, projc3, conv_w.astype(F32), arow, dtrow, gnorm.reshape(1, DH).astype(F32))


def _hgrn_kernel(q_ref, f_ref, i_ref, g_ref, lb_ref, gn_ref, o_ref, st_ref):
    R, K = B_ROWS, B_BLK
    HS = range(HEADS)

    @pl.when(pl.program_id(1) == 0)
    def _():
        st_ref[...] = jnp.zeros_like(st_ref)

    row = lax.broadcasted_iota(jnp.int32, (K, K), 0)
    col = lax.broadcasted_iota(jnp.int32, (K, K), 1)
    lmat = (col <= row).astype(BF16)
    H8 = 8
    row8 = lax.broadcasted_iota(jnp.int32, (H8, DH), 0)
    sls = [slice(h * DH, (h + 1) * DH) for h in HS]
    lb = [lb_ref[:, sl] for sl in sls]
    st = [st_ref[h] for h in HS]

    for blk in range(R // K):
        rs = slice(blk * K, (blk + 1) * K)
        q = [_silu(q_ref[0, rs, sl].astype(F32)) for sl in sls]
        f = [lb[h] + (1.0 - lb[h]) * _sigmoid(f_ref[0, rs, sls[h]].astype(F32)) for h in HS]
        k = [1.0 - a for a in f]
        v = [i_ref[0, rs, sl].astype(F32) for sl in sls]
        gc = [_dot_exact_lhs(lmat, jnp.log(a)) for a in f]
        o = [_dot_nt((q[h] * jnp.exp(gc[h])).astype(BF16), st[h].astype(BF16)) for h in HS]
        otile = [[o[h][t * H8:(t + 1) * H8] for t in range(K // H8)] for h in HS]
        for j in range(K):
            for t in range(j // H8, K // H8):
                ts = slice(t * H8, (t + 1) * H8)
                for h in HS:
                    d = gc[h][ts] - gc[h][j:j + 1]
                    rel = jnp.exp(jnp.where(row8 >= j - t * H8, d, -1e30) if t == j // H8 else d)
                    sj = jnp.sum(q[h][ts] * k[h][j:j + 1] * rel, axis=-1, keepdims=True)
                    otile[h][t] = otile[h][t] + sj * v[h][j:j + 1]
        o = [jnp.concatenate(otile[h], axis=0) for h in HS]
        gl = [a[K - 1:K] for a in gc]
        kg = [(k[h] * jnp.exp(gl[h] - gc[h])).astype(BF16) for h in HS]
        st = [st[h] * jnp.exp(gl[h]) + _dot_tn(v[h].astype(BF16), kg[h]) for h in HS]
        o = [a * lax.rsqrt(jnp.mean(a * a, axis=-1, keepdims=True) + EPS) * gn_ref[...] for a in o]
        o_ref[0, rs, :] = (jnp.concatenate(o, axis=1) * _sigmoid(g_ref[0, rs, :].astype(F32))).astype(o_ref.dtype)
    for h in HS:
        st_ref[h] = st[h]


def _hgrn(proj3, lb, gnorm):
    b, t, _ = proj3.shape
    R = B_ROWS
    seg = lambda s: pl.BlockSpec((1, R, HW), lambda i, c: (i, c, OFF_B // HW + s))
    const = lambda shape: pl.BlockSpec(shape, lambda i, c: (0,) * len(shape))
    return pl.pallas_call(
        _hgrn_kernel,
        grid=(b, t // R),
        in_specs=[seg(0), seg(1), seg(2), seg(3), const((1, HW)), const((1, DH))],
        out_specs=pl.BlockSpec((1, R, HW), lambda i, c: (i, c, 0)),
        out_shape=jax.ShapeDtypeStruct((b, t, HW), BF16),
        scratch_shapes=[pltpu.VMEM((HEADS, DH, DH), F32)],
        compiler_params=_cparams(("parallel", "arbitrary")),
        name="hgrn2",
    )(proj3, proj3, proj3, proj3, lb.reshape(1, HW).astype(F32), gnorm.reshape(1, DH).astype(F32))


def _t5_bucket_np(n):
    max_exact = REL_BUCKETS // 2
    nf = np.maximum(n, 1).astype(np.float32)
    large = max_exact + (np.log(nf / max_exact) / math.log(REL_MAX_DIST / max_exact)
                         * (REL_BUCKETS - max_exact)).astype(np.int32)
    large = np.minimum(large, REL_BUCKETS - 1)
    return np.where(n < max_exact, n, large)


def _dilated_kernel(span, dil, hpb, q_ref, kp_ref, kc_ref, vp_ref, vc_ref, bias_ref, qg_ref, kg_ref,
                    o_ref, lse_ref):
    CB = C_BLOCK
    n = pl.program_id(1)
    qi = lax.broadcasted_iota(jnp.int32, (CB, 2 * CB), 0)
    kj = lax.broadcasted_iota(jnp.int32, (CB, 2 * CB), 1)
    dist = qi + CB - kj
    valid = (dist >= 0) & (dist <= span) & ((kj >= CB) | (n > 0))

    def qk_norm(x, g):
        return x * lax.rsqrt(jnp.mean(x * x, axis=-1, keepdims=True) + EPS) * g

    def tiles(items):
        n_it = range(len(items))
        q = [qk_norm(q_ref[0, rows, sl], qg_ref[...]).astype(BF16) for rows, sl, _ in items]
        k = [qk_norm(jnp.concatenate([kp_ref[0, rows, sl], kc_ref[0, rows, sl]], axis=0),
                     kg_ref[...]).astype(BF16) for rows, sl, _ in items]
        v = [jnp.concatenate([vp_ref[0, rows, sl], vc_ref[0, rows, sl]], axis=0).astype(BF16)
             for rows, sl, _ in items]
        s = [_dot_nt(q[i], k[i]) * (DH ** -0.5) + bias_ref[items[i][2]] for i in n_it]
        s = [jnp.where(valid, a, -1e30) for a in s]
        mx = [jnp.max(a, axis=-1, keepdims=True) for a in s]
        p = [jnp.exp(s[i] - mx[i]) for i in n_it]
        den = [jnp.sum(a, axis=-1, keepdims=True) for a in p]
        o = [_dot(p[i].astype(BF16), v[i]) / den[i] for i in n_it]
        for i, (rows, sl, _) in enumerate(items):
            o_ref[0, rows, sl] = o[i]
            lse_ref[0, rows, sl] = jnp.broadcast_to(mx[i] + jnp.log(den[i]), (CB, DH))

    if dil == 1:
        for h0 in range(0, hpb, C_TILES):
            tiles([(pl.ds(0, CB), slice(h * DH, (h + 1) * DH), h) for h in range(h0, h0 + C_TILES)])
    else:
        def body(g, carry):
            tiles([(pl.ds(g * C_TILES + i, CB, stride=dil), slice(0, DH), 0) for i in range(C_TILES)])
            return carry
        lax.fori_loop(0, dil // C_TILES, body, 0)


def _dilated_group(proj3, gi, rel_bias, q_gain, k_gain):
    b, t, _ = proj3.shape
    window, dil = C_PAIRS[gi]
    span = window // dil
    CB = C_BLOCK
    rb = CB * dil
    hpb = HEADS if dil == 1 else 1
    bw = hpb * DH
    qi = np.arange(CB)[:, None]
    kj = np.arange(2 * CB)[None, :]
    bucket = _t5_bucket_np(np.maximum(qi + CB - kj, 0) * dil)
    onehot = jnp.asarray(np.eye(REL_BUCKETS, dtype=np.float32)[bucket])
    bias = jnp.einsum("qkb,bh->hqk", onehot, rel_bias[:, gi * HEADS:(gi + 1) * HEADS].astype(F32),
                      precision=HIGHEST)

    def seg(which, prev):
        base = (OFF_C + which * C_GROUPS * HW + gi * HW) // bw
        if prev:
            return pl.BlockSpec((1, rb, bw), lambda i, n, h: (i, jnp.maximum(n - 1, 0), base + h))
        return pl.BlockSpec((1, rb, bw), lambda i, n, h: (i, n, base + h))

    const = lambda shape: pl.BlockSpec(shape, lambda i, n, h: (0,) * len(shape))
    o, lse = pl.pallas_call(
        functools.partial(_dilated_kernel, span, dil, hpb),
        grid=(b, t // rb, HEADS // hpb),
        in_specs=[seg(0, False), seg(1, True), seg(1, False), seg(2, True), seg(2, False),
                  pl.BlockSpec((hpb, CB, 2 * CB), lambda i, n, h: (h, 0, 0)),
                  const((1, DH)), const((1, DH))],
        out_specs=[pl.BlockSpec((1, rb, bw), lambda i, n, h: (i, n, h)),
                   pl.BlockSpec((1, rb, bw), lambda i, n, h: (i, n, h))],
        out_shape=[jax.ShapeDtypeStruct((b, t, HW), F32),
                   jax.ShapeDtypeStruct((b, t, HW), F32)],
        compiler_params=_cparams(("parallel", "arbitrary", "arbitrary")),
        name=f"dilated_g{gi}",
    )(proj3, proj3, proj3, proj3, proj3, bias,
      q_gain.reshape(1, DH).astype(F32), k_gain.reshape(1, DH).astype(F32))
    return o.reshape(b * t, HW), lse.reshape(b * t, HW)


def _mix_kernel(x_ref, gate_ref, oa_ref, ob_ref, oc0_ref, oc1_ref, oc2_ref,
                l0_ref, l1_ref, l2_ref, wa_ref, wb_ref, wc_ref, wo_ref, out_ref):
    l0, l1, l2 = l0_ref[...], l1_ref[...], l2_ref[...]
    mx = jnp.maximum(jnp.maximum(l0, l1), l2)
    e0, e1, e2 = jnp.exp(l0 - mx), jnp.exp(l1 - mx), jnp.exp(l2 - mx)
    oc = (e0 * oc0_ref[...] + e1 * oc1_ref[...] + e2 * oc2_ref[...]) / (e0 + e1 + e2)
    mix = (_sigmoid(gate_ref[:, 0:D_MODEL].astype(F32)) * _dot(oa_ref[...], wa_ref[...])
           + _sigmoid(gate_ref[:, D_MODEL:2 * D_MODEL].astype(F32)) * _dot(ob_ref[...], wb_ref[...])
           + _sigmoid(gate_ref[:, 2 * D_MODEL:3 * D_MODEL].astype(F32)) * _dot(oc.astype(BF16), wc_ref[...]))
    out_ref[...] = x_ref[...] + _dot(mix.astype(BF16), wo_ref[...])


def _mix(x2, proj2, oa, ob, ocs, lses, wa, wb, wc, wo, tm=256):
    m = x2.shape[0]
    tm = min(tm, m)
    rowblk = lambda w: pl.BlockSpec((tm, w), lambda i: (i, 0))
    const = lambda shape: pl.BlockSpec(shape, lambda i: (0,) * len(shape))
    return pl.pallas_call(
        _mix_kernel,
        grid=(m // tm,),
        in_specs=[rowblk(D_MODEL), rowblk(3 * D_MODEL), rowblk(HW), rowblk(HW),
                  rowblk(HW), rowblk(HW), rowblk(HW), rowblk(HW), rowblk(HW), rowblk(HW),
                  const((HW, D_MODEL)), const((HW, D_MODEL)), const((HW, D_MODEL)),
                  const((D_MODEL, D_MODEL))],
        out_specs=rowblk(D_MODEL),
        out_shape=jax.ShapeDtypeStruct((m, D_MODEL), F32),
        compiler_params=_cparams(("parallel",)),
        name="branch_mix",
    )(x2, proj2, oa, ob, ocs[0], ocs[1], ocs[2], lses[0], lses[1], lses[2],
      wa.astype(BF16), wb.astype(BF16), wc.astype(BF16), wo.astype(BF16))


def _ffn_kernel(x_ref, g_ref, w1_ref, w3_ref, w2_ref, o_ref):
    x = x_ref[...]
    ms = jnp.mean(x * x, axis=-1, keepdims=True)
    h = (x * lax.rsqrt(ms + EPS) * g_ref[...]).astype(BF16)
    a = _dot(h, w1_ref[...])
    b = _dot(h, w3_ref[...])
    o_ref[...] = x + _dot((_silu(a) * b).astype(BF16), w2_ref[...])


def _ffn(x2, gain, w1, w3, w2, tm=512):
    m = x2.shape[0]
    ff = w1.shape[1]
    tm = min(tm, m)
    resident = lambda shape: pl.BlockSpec(shape, lambda i: (0, 0), pipeline_mode=pl.Buffered(1))
    return pl.pallas_call(
        _ffn_kernel,
        grid=(m // tm,),
        in_specs=[pl.BlockSpec((tm, D_MODEL), lambda i: (i, 0)),
                  pl.BlockSpec((1, D_MODEL), lambda i: (0, 0)),
                  resident((D_MODEL, ff)), resident((D_MODEL, ff)), resident((ff, D_MODEL))],
        out_specs=pl.BlockSpec((tm, D_MODEL), lambda i: (i, 0)),
        out_shape=jax.ShapeDtypeStruct((m, D_MODEL), F32),
        compiler_params=_cparams(("parallel",)),
        name="ffn",
    )(x2, gain.reshape(1, D_MODEL).astype(F32), w1.astype(BF16), w3.astype(BF16), w2.astype(BF16))


def _router_kernel(x_ref, g_ref, wr_ref, h_ref, ids_ref, ps_ref):
    x = x_ref[...]
    ms = jnp.mean(x * x, axis=-1, keepdims=True)
    h = x * lax.rsqrt(ms + EPS) * g_ref[...]
    h_ref[...] = h
    logits = jnp.dot(h, wr_ref[...], precision=HIGHEST, preferred_element_type=F32)
    lane = lax.broadcasted_iota(jnp.int32, logits.shape, 1)
    neg = jnp.float32(-jnp.inf)
    l1 = jnp.where(lane < N_EXPERTS, logits, neg)
    m1 = jnp.max(l1, axis=-1, keepdims=True)
    i1 = jnp.min(jnp.where(l1 == m1, lane, 128), axis=-1, keepdims=True)
    l2 = jnp.where(lane == i1, neg, l1)
    m2 = jnp.max(l2, axis=-1, keepdims=True)
    i2 = jnp.min(jnp.where(l2 == m2, lane, 128), axis=-1, keepdims=True)
    e = jnp.exp(m2 - m1)
    p1 = 1.0 / (1.0 + e)
    p2 = e / (1.0 + e)
    ids_ref[...] = jnp.where(lane == 0, i1, jnp.where(lane == 1, i2, 0))
    ps_ref[...] = jnp.where(lane == 0, p1, jnp.where(lane == 1, p2, 0.0))


def _router(x2, gain, w_router, tm=512):
    m = x2.shape[0]
    tm = min(tm, m)
    wr = jnp.pad(w_router.astype(F32), ((0, 0), (0, 128 - N_EXPERTS)))
    return pl.pallas_call(
        _router_kernel,
        grid=(m // tm,),
        in_specs=[pl.BlockSpec((tm, D_MODEL), lambda i: (i, 0)),
                  pl.BlockSpec((1, D_MODEL), lambda i: (0, 0)),
                  pl.BlockSpec((D_MODEL, 128), lambda i: (0, 0))],
        out_specs=[pl.BlockSpec((tm, D_MODEL), lambda i: (i, 0)),
                   pl.BlockSpec((tm, 128), lambda i: (i, 0)),
                   pl.BlockSpec((tm, 128), lambda i: (i, 0))],
        out_shape=[jax.ShapeDtypeStruct((m, D_MODEL), F32),
                   jax.ShapeDtypeStruct((m, 128), jnp.int32),
                   jax.ShapeDtypeStruct((m, 128), F32)],
        compiler_params=_cparams(("parallel",)),
        name="router",
    )(x2, gain.reshape(1, D_MODEL).astype(F32), wr)


def _route_positions(ids, tm):
    m = ids.shape[0]
    e_flat = ids[:, :TOP_K].reshape(-1)
    onehot = (e_flat[:, None] == jnp.arange(N_EXPERTS)[None, :]).astype(jnp.int32)
    csum = jnp.cumsum(onehot, axis=0)
    counts = csum[-1]
    gsz = ((counts + tm - 1) // tm) * tm
    gend = jnp.cumsum(gsz)
    pos = jnp.sum(onehot * (gend - gsz + csum - 1), axis=1)
    nt = (TOP_K * m + N_EXPERTS * tm) // tm
    n_used = gend[-1] // tm
    tile_e = jnp.sum((jnp.arange(nt)[:, None] * tm >= gend[None, :]).astype(jnp.int32), axis=1)
    last_e = jnp.sum(((n_used - 1) * tm >= gend).astype(jnp.int32))
    tile_e = jnp.minimum(tile_e, last_e)
    return pos.astype(jnp.int32), tile_e.astype(jnp.int32), n_used.reshape(1).astype(jnp.int32), nt


def _dispatch_kernel(pos_ref, h_ref, init_ref, xs_ref, sem):
    del init_ref
    td = h_ref.shape[0]

    def row_copy(t, s):
        return pltpu.make_async_copy(h_ref.at[pl.ds(t, 1)],
                                     xs_ref.at[pl.ds(pos_ref[0, 0, TOP_K * t + s], 1)], sem)

    def start(t, c):
        for s in range(TOP_K):
            row_copy(t, s).start()
        return c

    def wait(t, c):
        for s in range(TOP_K):
            row_copy(t, s).wait()
        return c

    lax.fori_loop(0, td, start, 0, unroll=8)
    lax.fori_loop(0, td, wait, 0, unroll=8)


def _dispatch(h, pos, npad, td=256):
    m = h.shape[0]
    td = min(td, m)
    return pl.pallas_call(
        _dispatch_kernel,
        grid=(m // td,),
        in_specs=[pl.BlockSpec((1, 1, TOP_K * td), lambda i: (i, 0, 0), memory_space=pltpu.SMEM),
                  pl.BlockSpec((td, D_MODEL), lambda i: (i, 0)),
                  pl.BlockSpec(memory_space=pl.ANY)],
        out_specs=pl.BlockSpec(memory_space=pl.ANY),
        out_shape=jax.ShapeDtypeStruct((npad, D_MODEL), F32),
        scratch_shapes=[pltpu.SemaphoreType.DMA(())],
        input_output_aliases={2: 0},
        compiler_params=_cparams(("arbitrary",)),
        name="moe_dispatch",
    )(pos.reshape(m // td, 1, TOP_K * td), h, jnp.zeros((npad, D_MODEL), F32))


def _experts_kernel(te_ref, nu_ref, xs_ref, w1_ref, w3_ref, w2_ref, y_ref, xb_ref, acc_ref):
    del te_ref
    i = pl.program_id(0)
    j = pl.program_id(1)

    @pl.when(i < nu_ref[0])
    def _():
        @pl.when(j == 0)
        def _():
            xb_ref[...] = xs_ref[...].astype(BF16)
            acc_ref[...] = jnp.zeros_like(acc_ref)

        xb = xb_ref[...]
        a = _dot(xb, w1_ref[0])
        b = _dot(xb, w3_ref[0])
        acc_ref[...] += _dot((_silu(a) * b).astype(BF16), w2_ref[0])

        @pl.when(j == pl.num_programs(1) - 1)
        def _():
            y_ref[...] = acc_ref[...]

    @pl.when((i >= nu_ref[0]) & (j == 0))
    def _():
        y_ref[...] = jnp.zeros_like(y_ref)


def _experts(xs, tile_e, n_used, w1, w3, w2, tm, tf=1792):
    npad = xs.shape[0]
    ff = w1.shape[2]
    nj = ff // tf
    row = lambda i, j, te, nu: (jnp.minimum(i, nu[0] - 1), 0)
    jj = lambda i, j, nu: jnp.where(i < nu[0], j, nj - 1)
    return pl.pallas_call(
        _experts_kernel,
        grid_spec=pltpu.PrefetchScalarGridSpec(
            num_scalar_prefetch=2,
            grid=(npad // tm, nj),
            in_specs=[pl.BlockSpec((tm, D_MODEL), row),
                      pl.BlockSpec((1, D_MODEL, tf), lambda i, j, te, nu: (te[i], 0, jj(i, j, nu))),
                      pl.BlockSpec((1, D_MODEL, tf), lambda i, j, te, nu: (te[i], 0, jj(i, j, nu))),
                      pl.BlockSpec((1, tf, D_MODEL), lambda i, j, te, nu: (te[i], jj(i, j, nu), 0))],
            out_specs=pl.BlockSpec((tm, D_MODEL), lambda i, j, te, nu: (i, 0)),
            scratch_shapes=[pltpu.VMEM((tm, D_MODEL), BF16), pltpu.VMEM((tm, D_MODEL), F32)]),
        out_shape=jax.ShapeDtypeStruct((npad, D_MODEL), F32),
        compiler_params=_cparams(("arbitrary", "arbitrary")),
        name="moe_experts",
    )(tile_e, n_used, xs, w1.astype(BF16), w3.astype(BF16), w2.astype(BF16))


def _combine_kernel(pos_ref, x_ref, ps_ref, y_ref, o_ref, buf_ref, sem):
    td = x_ref.shape[0]

    def row_copy(t, s):
        return pltpu.make_async_copy(y_ref.at[pl.ds(pos_ref[0, 0, TOP_K * t + s], 1)],
                                     buf_ref.at[s, pl.ds(t, 1)], sem.at[s])

    def start(t, c):
        for s in range(TOP_K):
            row_copy(t, s).start()
        return c

    def wait(t, c):
        for s in range(TOP_K):
            row_copy(t, s).wait()
        return c

    lax.fori_loop(0, td, start, 0, unroll=8)
    lax.fori_loop(0, td, wait, 0, unroll=8)
    ps = ps_ref[...]
    o_ref[...] = x_ref[...] + ps[:, 0:1] * buf_ref[0] + ps[:, 1:2] * buf_ref[1]


def _combine(x2, ps, pos, y, td=256):
    m = x2.shape[0]
    td = min(td, m)
    return pl.pallas_call(
        _combine_kernel,
        grid=(m // td,),
        in_specs=[pl.BlockSpec((1, 1, TOP_K * td), lambda i: (i, 0, 0), memory_space=pltpu.SMEM),
                  pl.BlockSpec((td, D_MODEL), lambda i: (i, 0)),
                  pl.BlockSpec((td, 128), lambda i: (i, 0)),
                  pl.BlockSpec(memory_space=pl.ANY)],
        out_specs=pl.BlockSpec((td, D_MODEL), lambda i: (i, 0)),
        out_shape=jax.ShapeDtypeStruct((m, D_MODEL), F32),
        scratch_shapes=[pltpu.VMEM((TOP_K, td, D_MODEL), F32), pltpu.SemaphoreType.DMA((TOP_K,))],
        compiler_params=_cparams(("arbitrary",)),
        name="moe_combine",
    )(pos.reshape(m // td, 1, TOP_K * td), x2, ps, y)


def _moe(x2, gain, w_router, w1, w3, w2, tm=512):
    m = x2.shape[0]
    tm = min(tm, m)
    h, ids, ps = _router(x2, gain, w_router)
    pos, tile_e, n_used, nt = _route_positions(ids, tm)
    xs = _dispatch(h, pos, nt * tm)
    y = _experts(xs, tile_e, n_used, w1, w3, w2, tm)
    return _combine(x2, ps, pos, y)


def _relayout_w_in(w):
    k = w.shape[0]
    main = [w[:, _R_BRG:_R_END], w[:, _R_AQKV:_R_BETA], w[:, _R_AGATE:_R_B], w[:, _R_B:_R_C]]
    second = [w[:, _R_C:_R_BRG], w[:, _R_BETA:_R_AGATE],
              jnp.zeros((k, NP_C - OFF_BA - 2 * HEADS), w.dtype)]
    return (jnp.concatenate(main, axis=1).astype(BF16), jnp.concatenate(second, axis=1).astype(BF16))


def _layer_mixers(x2, b, t, layer, lower_bounds, w_in, norm_mix, conv_a, a_log, dt_bias,
                  gnorm_a, gnorm_b, qnorm_c, knorm_c, rel_bias, w_br_a, w_br_b, w_br_c, w_out):
    w_main, w_second = _relayout_w_in(w_in[layer])
    gain = norm_mix[layer].astype(F32)
    proj2 = _norm_proj(x2, gain, w_main, BF16)
    proj3 = proj2.reshape(b, t, NP_MAIN)
    projc3 = _norm_proj(x2, gain, w_second, F32).reshape(b, t, NP_C)
    oa = _deltanet(proj3, projc3, conv_a[layer], a_log[layer], dt_bias[layer], gnorm_a[layer])
    ob = _hgrn(proj3, lower_bounds[layer], gnorm_b[layer])
    ocs, lses = [], []
    for gi in range(C_GROUPS):
        o, lse = _dilated_group(projc3, gi, rel_bias, qnorm_c[layer, gi], knorm_c[layer, gi])
        ocs.append(o)
        lses.append(lse)
    return _mix(x2, proj2, oa.reshape(b * t, HW), ob.reshape(b * t, HW), ocs, lses,
                w_br_a[layer], w_br_b[layer], w_br_c[layer], w_out[layer])


def kernel(x, w_in, norm_mix, conv_a, a_log, dt_bias, gnorm_a, lb_logits, gnorm_b, qnorm_c, knorm_c, rel_bias, w_br_a, w_br_b, w_br_c, w_out, norm_ffn, ffn_w1, ffn_w3, ffn_w2, router, moe_w1, moe_w3, moe_w2):
    b, t, _ = x.shape
    depth = w_in.shape[0]
    p_lb = jax.nn.softmax(lb_logits.astype(F32), axis=0)
    lower_bounds = jnp.cumsum(p_lb, axis=0) - p_lb[0:1]
    x2 = x.reshape(b * t, D_MODEL).astype(F32)
    for layer in range(depth):
        x2 = _layer_mixers(x2, b, t, layer, lower_bounds, w_in, norm_mix, conv_a, a_log, dt_bias,
                           gnorm_a, gnorm_b, qnorm_c, knorm_c, rel_bias,
                           w_br_a, w_br_b, w_br_c, w_out)
        li = layer // 2
        if layer % 2 == 0:
            x2 = _ffn(x2, norm_ffn[layer], ffn_w1[li], ffn_w3[li], ffn_w2[li])
        else:
            x2 = _moe(x2, norm_ffn[layer], router[li], moe_w1[li], moe_w3[li], moe_w2[li])
    return x2.reshape(b, t, D_MODEL).astype(x.dtype)
```

```python
import functools
import math

import numpy as np
import jax
import jax.numpy as jnp
from jax import lax
from jax.experimental import pallas as pl
from jax.experimental.pallas import tpu as pltpu

F32 = jnp.float32
BF16 = jnp.bfloat16
HIGHEST = lax.Precision.HIGHEST

D_MODEL = 1024
EPS = 1e-6
HEADS = 4
DH = 128
HW = HEADS * DH
A_CONV = 4
A_CHUNK = 64
A_ROWS = 256
B_ROWS = 256
B_BLK = 16
C_PAIRS = ((128, 1), (512, 4), (2048, 16))
C_GROUPS = 3
C_BLOCK = 128
C_TILES = 4
C_NLB = 4
REL_BUCKETS = 32
REL_MAX_DIST = 2048
N_EXPERTS = 8
TOP_K = 2

OFF_BRG = 0
OFF_AQKV = 3072
OFF_AGATE = 4608
OFF_B = 5120
NP_MAIN = 7168
OFF_C = 0
OFF_BA = 4608
NP_C = 5120

_R_AQKV, _R_BETA, _R_AGATE, _R_B, _R_C, _R_BRG, _R_END = 0, 1536, 1544, 2056, 4104, 8712, 11784

VMEM_LIMIT = 56 * 1024 * 1024


def _cparams(sem):
    return pltpu.CompilerParams(dimension_semantics=sem, vmem_limit_bytes=VMEM_LIMIT)


def _sigmoid(x):
    return 1.0 / (1.0 + jnp.exp(-x))


def _silu(x):
    return x * _sigmoid(x)


def _softplus(x):
    return jnp.maximum(x, 0.0) + jnp.log(1.0 + jnp.exp(-jnp.abs(x)))


def _dot(a, b):
    return jnp.dot(a, b, preferred_element_type=F32)


def _dot_nt(a, b, precision=None):
    return lax.dot_general(a, b, (((1,), (1,)), ((), ())), precision=precision,
                           preferred_element_type=F32)


def _dot_tn(a, b):
    return lax.dot_general(a, b, (((0,), (0,)), ((), ())), preferred_element_type=F32)


def _norm_proj_kernel(tn, x_ref, g_ref, w_ref, o_ref):
    x = x_ref[...]
    ms = jnp.mean(x * x, axis=-1, keepdims=True)
    h = (x * lax.rsqrt(ms + EPS) * g_ref[...]).astype(BF16)
    for c0 in range(0, w_ref.shape[1], tn):
        o_ref[:, c0:c0 + tn] = _dot(h, w_ref[:, c0:c0 + tn]).astype(o_ref.dtype)


def _norm_proj(x2, gain, w_bf16, out_dtype, tm=512, tn=1024):
    m = x2.shape[0]
    n = w_bf16.shape[1]
    tm = min(tm, m)
    return pl.pallas_call(
        functools.partial(_norm_proj_kernel, tn),
        grid=(m // tm,),
        in_specs=[pl.BlockSpec((tm, D_MODEL), lambda i: (i, 0)),
                  pl.BlockSpec((1, D_MODEL), lambda i: (0, 0)),
                  pl.BlockSpec((D_MODEL, n), lambda i: (0, 0), pipeline_mode=pl.Buffered(1))],
        out_specs=pl.BlockSpec((tm, n), lambda i: (i, 0)),
        out_shape=jax.ShapeDtypeStruct((m, n), out_dtype),
        compiler_params=_cparams(("parallel",)),
        name="norm_proj",
    )(x2, gain.reshape(1, D_MODEL), w_bf16)


def _dotb(a, b):
    return _dot(a.astype(BF16), b.astype(BF16))


def _dot_exact_lhs(a_bf16, b):
    b0 = b.astype(BF16)
    r1 = b - b0.astype(F32)
    b1 = r1.astype(BF16)
    b2 = (r1 - b1.astype(F32)).astype(BF16)
    return _dot(a_bf16, b0) + (_dot(a_bf16, b1) + _dot(a_bf16, b2))


def _deltanet_kernel(qkv_ref, gate_ref, ba_ref, convw_ref, arow_ref, dtrow_ref, gn_ref,
                     o_ref, s_ref, xe_ref):
    C = A_CHUNK
    R = A_ROWS
    HS = range(HEADS)
    IT = range((R // C) * HEADS)

    @pl.when(pl.program_id(1) == 0)
    def _():
        s_ref[...] = jnp.zeros_like(s_ref)
        xe_ref[0:8, :] = jnp.zeros((8, 3 * HW), F32)

    x = qkv_ref[0].astype(F32)
    xe_ref[8:8 + R, :] = x
    w = convw_ref[...]
    y = (w[3:4] * x + w[2:3] * xe_ref[7:7 + R, :] + w[1:2] * xe_ref[6:6 + R, :]
         + w[0:1] * xe_ref[5:5 + R, :])
    xe_ref[0:8, :] = x[R - 8:R]
    y = _silu(y)

    ba = ba_ref[0]
    beta_all = _sigmoid(ba)
    g_all = arow_ref[...] * _softplus(ba + dtrow_ref[...])
    gate = _silu(gate_ref[0].astype(F32))
    s_old = [s_ref[h] for h in HS]

    row = lax.broadcasted_iota(jnp.int32, (C, C), 0)
    col = lax.broadcasted_iota(jnp.int32, (C, C), 1)
    lmat = (col <= row).astype(BF16)
    rowx = lax.broadcasted_iota(jnp.int32, (C, DH + C), 0)
    colx = lax.broadcasted_iota(jnp.int32, (C, DH + C), 1)
    umask = (colx < DH) | (rowx > colx - DH)
    eye = (row == col).astype(F32)
    bd8 = (row >> 3) == (col >> 3)

    def merge_mask(sh):
        return (((row >> (sh + 1)) == (col >> (sh + 1)))
                & (((row >> sh) & 1) == 1) & (((col >> sh) & 1) == 0))

    rows = [slice((i // HEADS) * C, (i // HEADS + 1) * C) for i in IT]
    hd = [i % HEADS for i in IT]
    q = [y[rows[i], hd[i] * DH:(hd[i] + 1) * DH] for i in IT]
    k = [y[rows[i], HW + hd[i] * DH:HW + (hd[i] + 1) * DH] for i in IT]
    v = [y[rows[i], 2 * HW + hd[i] * DH:2 * HW + (hd[i] + 1) * DH] for i in IT]
    q = [a * (lax.rsqrt(jnp.sum(a * a, axis=-1, keepdims=True) + EPS) * (DH ** -0.5)) for a in q]
    k = [a * lax.rsqrt(jnp.sum(a * a, axis=-1, keepdims=True) + EPS) for a in k]
    beta = [beta_all[rows[i], hd[i]:hd[i] + 1] for i in IT]
    gb = [jnp.broadcast_to(g_all[rows[i], HEADS + hd[i]:HEADS + hd[i] + 1], (C, DH + C)) for i in IT]
    dext = [_dot_exact_lhs(lmat, jnp.where(umask, gb[i], 0.0)) for i in IT]
    gc = [d[:, :DH] for d in dext]
    edm = [jnp.exp(d[:, DH:]) for d in dext]
    egc = [jnp.exp(g) for g in gc]
    gl = [g[C - 1:C, :] for g in gc]
    kb = [k[i] * beta[i] for i in IT]
    m = [_dot_nt(kb[i].astype(BF16), k[i].astype(BF16)) * jnp.where(row > col, edm[i], 0.0)
         for i in IT]

    nd = [jnp.where(bd8, -a, 0.0) for a in m]
    p2 = [_dotb(a, a) for a in nd]
    p4 = [_dotb(a, a) for a in p2]
    x1 = [eye + nd[i] for i in IT]
    x1 = [x1[i] + _dotb(x1[i], p2[i]) for i in IT]
    xi = [x1[i] + _dotb(x1[i], p4[i]) for i in IT]
    for sh in (3, 4, 5):
        mm = merge_mask(sh)
        t = [_dotb(xi[i], jnp.where(mm, m[i], 0.0)) for i in IT]
        xi = [xi[i] - _dotb(t[i], xi[i]) for i in IT]

    rhs = [jnp.concatenate([v[i] * beta[i], kb[i] * egc[i]], axis=1) for i in IT]
    sol = [_dotb(xi[i], rhs[i]) for i in IT]
    attn = [(_dot_nt(q[i].astype(BF16), k[i].astype(BF16))
             * jnp.where(row >= col, edm[i], 0.0)).astype(BF16) for i in IT]
    qg = [(q[i] * egc[i]).astype(BF16) for i in IT]
    kg = [(k[i] * jnp.exp(gl[i] - gc[i])).astype(BF16) for i in IT]
    egl = [jnp.exp(a) for a in gl]

    s_cur = s_old
    for c in range(R // C):
        it = [c * HEADS + h for h in HS]
        sb = [a.astype(BF16) for a in s_cur]
        v_new = [(sol[i][:, :DH] - _dot(sol[i][:, DH:].astype(BF16), sb[h])).astype(BF16)
                 for h, i in enumerate(it)]
        o = [_dot(qg[i], sb[h]) + _dot(attn[i], v_new[h]) for h, i in enumerate(it)]
        s_cur = [s_cur[h] * egl[i] + _dot_tn(kg[i], v_new[h]) for h, i in enumerate(it)]
        o = [a * lax.rsqrt(jnp.mean(a * a, axis=-1, keepdims=True) + EPS) * gn_ref[...] for a in o]
        o_ref[0, c * C:(c + 1) * C, :] = (jnp.concatenate(o, axis=1)
                                          * gate[c * C:(c + 1) * C]).astype(o_ref.dtype)
    for h in HS:
        s_ref[h] = s_cur[h]


def _deltanet(proj3, projc3, conv_w, a_log, dt_bias, gnorm):
    b, t, _ = proj3.shape
    C = A_ROWS
    pad = jnp.zeros((HEADS,), F32)
    arow = jnp.concatenate([pad, -jnp.exp(a_log.astype(F32)), jnp.zeros((120,), F32)]).reshape(1, 128)
    dtrow = jnp.concatenate([pad, dt_bias.astype(F32), jnp.zeros((120,), F32)]).reshape(1, 128)
    const = lambda shape: pl.BlockSpec(shape, lambda i, c: (0,) * len(shape))
    return pl.pallas_call(
        _deltanet_kernel,
        grid=(b, t // C),
        in_specs=[pl.BlockSpec((1, C, 3 * HW), lambda i, c: (i, c, OFF_AQKV // (3 * HW))),
                  pl.BlockSpec((1, C, HW), lambda i, c: (i, c, OFF_AGATE // HW)),
                  pl.BlockSpec((1, C, 128), lambda i, c: (i, c, OFF_BA // 128)),
                  const((A_CONV, 3 * HW)), const((1, 128)), const((1, 128)), const((1, DH))],
        out_specs=pl.BlockSpec((1, C, HW), lambda i, c: (i, c, 0)),
        out_shape=jax.ShapeDtypeStruct((b, t, HW), BF16),
        scratch_shapes=[pltpu.VMEM((HEADS, DH, DH), F32), pltpu.VMEM((C + 8, 3 * HW), F32)],
        compiler_params=_cparams(("parallel", "arbitrary")),
        name="deltanet",
    )(proj3, proj3, projc3, conv_w.astype(F32), arow, dtrow, gnorm.reshape(1, DH).astype(F32))


def _hgrn_kernel(q_ref, f_ref, i_ref, g_ref, lb_ref, gn_ref, o_ref, st_ref):
    R, K = B_ROWS, B_BLK
    HS = range(HEADS)

    @pl.when(pl.program_id(1) == 0)
    def _():
        st_ref[...] = jnp.zeros_like(st_ref)

    row = lax.broadcasted_iota(jnp.int32, (K, K), 0)
    col = lax.broadcasted_iota(jnp.int32, (K, K), 1)
    lmat = (col <= row).astype(BF16)
    H8 = 8
    row8 = lax.broadcasted_iota(jnp.int32, (H8, DH), 0)
    sls = [slice(h * DH, (h + 1) * DH) for h in HS]
    lb = [lb_ref[:, sl] for sl in sls]
    st = [st_ref[h] for h in HS]

    for blk in range(R // K):
        rs = slice(blk * K, (blk + 1) * K)
        q = [_silu(q_ref[0, rs, sl].astype(F32)) for sl in sls]
        f = [lb[h] + (1.0 - lb[h]) * _sigmoid(f_ref[0, rs, sls[h]].astype(F32)) for h in HS]
        k = [1.0 - a for a in f]
        v = [i_ref[0, rs, sl].astype(F32) for sl in sls]
        gc = [_dot_exact_lhs(lmat, jnp.log(a)) for a in f]
        o = [_dot_nt((q[h] * jnp.exp(gc[h])).astype(BF16), st[h].astype(BF16)) for h in HS]
        otile = [[o[h][t * H8:(t + 1) * H8] for t in range(K // H8)] for h in HS]
        for j in range(K):
            for t in range(j // H8, K // H8):
                ts = slice(t * H8, (t + 1) * H8)
                for h in HS:
                    d = gc[h][ts] - gc[h][j:j + 1]
                    rel = jnp.exp(jnp.where(row8 >= j - t * H8, d, -1e30) if t == j // H8 else d)
                    sj = jnp.sum(q[h][ts] * k[h][j:j + 1] * rel, axis=-1, keepdims=True)
                    otile[h][t] = otile[h][t] + sj * v[h][j:j + 1]
        o = [jnp.concatenate(otile[h], axis=0) for h in HS]
        gl = [a[K - 1:K] for a in gc]
        kg = [(k[h] * jnp.exp(gl[h] - gc[h])).astype(BF16) for h in HS]
        st = [st[h] * jnp.exp(gl[h]) + _dot_tn(v[h].astype(BF16), kg[h]) for h in HS]
        o = [a * lax.rsqrt(jnp.mean(a * a, axis=-1, keepdims=True) + EPS) * gn_ref[...] for a in o]
        o_ref[0, rs, :] = (jnp.concatenate(o, axis=1) * _sigmoid(g_ref[0, rs, :].astype(F32))).astype(o_ref.dtype)
    for h in HS:
        st_ref[h] = st[h]


def _hgrn(proj3, lb, gnorm):
    b, t, _ = proj3.shape
    R = B_ROWS
    seg = lambda s: pl.BlockSpec((1, R, HW), lambda i, c: (i, c, OFF_B // HW + s))
    const = lambda shape: pl.BlockSpec(shape, lambda i, c: (0,) * len(shape))
    return pl.pallas_call(
        _hgrn_kernel,
        grid=(b, t // R),
        in_specs=[seg(0), seg(1), seg(2), seg(3), const((1, HW)), const((1, DH))],
        out_specs=pl.BlockSpec((1, R, HW), lambda i, c: (i, c, 0)),
        out_shape=jax.ShapeDtypeStruct((b, t, HW), BF16),
        scratch_shapes=[pltpu.VMEM((HEADS, DH, DH), F32)],
        compiler_params=_cparams(("parallel", "arbitrary")),
        name="hgrn2",
    )(proj3, proj3, proj3, proj3, lb.reshape(1, HW).astype(F32), gnorm.reshape(1, DH).astype(F32))


def _t5_bucket_np(n):
    max_exact = REL_BUCKETS // 2
    nf = np.maximum(n, 1).astype(np.float32)
    large = max_exact + (np.log(nf / max_exact) / math.log(REL_MAX_DIST / max_exact)
                         * (REL_BUCKETS - max_exact)).astype(np.int32)
    large = np.minimum(large, REL_BUCKETS - 1)
    return np.where(n < max_exact, n, large)


def _dilated_kernel(span, dil, hpb, nlb, q_ref, kp_ref, kc_ref, vp_ref, vc_ref, bias_ref, qg_ref, kg_ref,
                    o_ref, lse_ref):
    CB = C_BLOCK
    n = pl.program_id(1)
    qi = lax.broadcasted_iota(jnp.int32, (CB, 2 * CB), 0)
    kj = lax.broadcasted_iota(jnp.int32, (CB, 2 * CB), 1)
    dist = qi + CB - kj
    band = (dist >= 0) & (dist <= span)
    band_first = band & ((kj >= CB) | (n > 0))

    def qk_norm(x, g):
        return x * lax.rsqrt(jnp.mean(x * x, axis=-1, keepdims=True) + EPS) * g

    def rows(r, jb):
        return pl.ds(jb * CB * dil + r, CB, stride=dil) if dil > 1 else pl.ds(jb * CB, CB)

    def with_prev(p_ref, c_ref, r, jb, sl):
        prev = p_ref[0, rows(r, 0), sl] if jb == 0 else c_ref[0, rows(r, jb - 1), sl]
        return jnp.concatenate([prev, c_ref[0, rows(r, jb), sl]], axis=0)

    def tiles(items):
        n_it = range(len(items))
        q = [qk_norm(q_ref[0, rows(r, jb), sl], qg_ref[...]).astype(BF16) for r, jb, sl, _ in items]
        k = [qk_norm(with_prev(kp_ref, kc_ref, r, jb, sl), kg_ref[...]).astype(BF16)
             for r, jb, sl, _ in items]
        v = [with_prev(vp_ref, vc_ref, r, jb, sl).astype(BF16) for r, jb, sl, _ in items]
        s = [_dot_nt(q[i], k[i]) * (DH ** -0.5) + bias_ref[items[i][3]] for i in n_it]
        s = [jnp.where(band_first if items[i][1] == 0 else band, s[i], -1e30) for i in n_it]
        mx = [jnp.max(a, axis=-1, keepdims=True) for a in s]
        p = [jnp.exp(s[i] - mx[i]) for i in n_it]
        den = [jnp.sum(a, axis=-1, keepdims=True) for a in p]
        o = [_dot(p[i].astype(BF16), v[i]) / den[i] for i in n_it]
        for i, (r, jb, sl, _) in enumerate(items):
            o_ref[0, rows(r, jb), sl] = o[i]
            lse_ref[0, rows(r, jb), sl] = jnp.broadcast_to(mx[i] + jnp.log(den[i]), (CB, DH))

    for jb in range(nlb):
        if dil == 1:
            for h0 in range(0, hpb, C_TILES):
                tiles([(0, jb, slice(h * DH, (h + 1) * DH), h) for h in range(h0, h0 + C_TILES)])
        elif dil == C_TILES:
            tiles([(r, jb, slice(0, DH), 0) for r in range(dil)])
        else:
            def body(g, carry, jb=jb):
                tiles([(g * C_TILES + i, jb, slice(0, DH), 0) for i in range(C_TILES)])
                return carry
            lax.fori_loop(0, dil // C_TILES, body, 0)


def _dilated_group(proj3, gi, rel_bias, q_gain, k_gain):
    b, t, _ = proj3.shape
    window, dil = C_PAIRS[gi]
    span = window // dil
    CB = C_BLOCK
    pb = CB * dil
    nlb = min(C_NLB, t // pb)
    rb = nlb * pb
    hpb = HEADS if dil == 1 else 1
    bw = hpb * DH
    qi = np.arange(CB)[:, None]
    kj = np.arange(2 * CB)[None, :]
    bucket = _t5_bucket_np(np.maximum(qi + CB - kj, 0) * dil)
    onehot = jnp.asarray(np.eye(REL_BUCKETS, dtype=np.float32)[bucket])
    bias = jnp.einsum("qkb,bh->hqk", onehot, rel_bias[:, gi * HEADS:(gi + 1) * HEADS].astype(F32),
                      precision=HIGHEST)

    def seg(which, prev):
        base = (OFF_C + which * C_GROUPS * HW + gi * HW) // bw
        if prev:
            return pl.BlockSpec((1, pb, bw), lambda i, n, h: (i, jnp.maximum(n * nlb - 1, 0), base + h))
        return pl.BlockSpec((1, rb, bw), lambda i, n, h: (i, n, base + h))

    const = lambda shape: pl.BlockSpec(shape, lambda i, n, h: (0,) * len(shape))
    o, lse = pl.pallas_call(
        functools.partial(_dilated_kernel, span, dil, hpb, nlb),
        grid=(b, t // rb, HEADS // hpb),
        in_specs=[seg(0, False), seg(1, True), seg(1, False), seg(2, True), seg(2, False),
                  pl.BlockSpec((hpb, CB, 2 * CB), lambda i, n, h: (h, 0, 0)),
                  const((1, DH)), const((1, DH))],
        out_specs=[pl.BlockSpec((1, rb, bw), lambda i, n, h: (i, n, h)),
                   pl.BlockSpec((1, rb, bw), lambda i, n, h: (i, n, h))],
        out_shape=[jax.ShapeDtypeStruct((b, t, HW), F32),
                   jax.ShapeDtypeStruct((b, t, HW), F32)],
        compiler_params=_cparams(("parallel", "arbitrary", "arbitrary")),
        name=f"dilated_g{gi}",
    )(proj3, proj3, proj3, proj3, proj3, bias,
      q_gain.reshape(1, DH).astype(F32), k_gain.reshape(1, DH).astype(F32))
    return o.reshape(b * t, HW), lse.reshape(b * t, HW)


def _mix_kernel(x_ref, gate_ref, oa_ref, ob_ref, oc0_ref, oc1_ref, oc2_ref,
                l0_ref, l1_ref, l2_ref, wa_ref, wb_ref, wc_ref, wo_ref, out_ref):
    l0, l1, l2 = l0_ref[...], l1_ref[...], l2_ref[...]
    mx = jnp.maximum(jnp.maximum(l0, l1), l2)
    e0, e1, e2 = jnp.exp(l0 - mx), jnp.exp(l1 - mx), jnp.exp(l2 - mx)
    oc = (e0 * oc0_ref[...] + e1 * oc1_ref[...] + e2 * oc2_ref[...]) / (e0 + e1 + e2)
    mix = (_sigmoid(gate_ref[:, 0:D_MODEL].astype(F32)) * _dot(oa_ref[...], wa_ref[...])
           + _sigmoid(gate_ref[:, D_MODEL:2 * D_MODEL].astype(F32)) * _dot(ob_ref[...], wb_ref[...])
           + _sigmoid(gate_ref[:, 2 * D_MODEL:3 * D_MODEL].astype(F32)) * _dot(oc.astype(BF16), wc_ref[...]))
    out_ref[...] = x_ref[...] + _dot(mix.astype(BF16), wo_ref[...])


def _mix(x2, proj2, oa, ob, ocs, lses, wa, wb, wc, wo, tm=256):
    m = x2.shape[0]
    tm = min(tm, m)
    rowblk = lambda w: pl.BlockSpec((tm, w), lambda i: (i, 0))
    const = lambda shape: pl.BlockSpec(shape, lambda i: (0,) * len(shape))
    return pl.pallas_call(
        _mix_kernel,
        grid=(m // tm,),
        in_specs=[rowblk(D_MODEL), rowblk(3 * D_MODEL), rowblk(HW), rowblk(HW),
                  rowblk(HW), rowblk(HW), rowblk(HW), rowblk(HW), rowblk(HW), rowblk(HW),
                  const((HW, D_MODEL)), const((HW, D_MODEL)), const((HW, D_MODEL)),
                  const((D_MODEL, D_MODEL))],
        out_specs=rowblk(D_MODEL),
        out_shape=jax.ShapeDtypeStruct((m, D_MODEL), F32),
        compiler_params=_cparams(("parallel",)),
        name="branch_mix",
    )(x2, proj2, oa, ob, ocs[0], ocs[1], ocs[2], lses[0], lses[1], lses[2],
      wa.astype(BF16), wb.astype(BF16), wc.astype(BF16), wo.astype(BF16))


def _ffn_kernel(x_ref, g_ref, w1_ref, w3_ref, w2_ref, o_ref):
    x = x_ref[...]
    ms = jnp.mean(x * x, axis=-1, keepdims=True)
    h = (x * lax.rsqrt(ms + EPS) * g_ref[...]).astype(BF16)
    a = _dot(h, w1_ref[...])
    b = _dot(h, w3_ref[...])
    o_ref[...] = x + _dot((_silu(a) * b).astype(BF16), w2_ref[...])


def _ffn(x2, gain, w1, w3, w2, tm=512):
    m = x2.shape[0]
    ff = w1.shape[1]
    tm = min(tm, m)
    resident = lambda shape: pl.BlockSpec(shape, lambda i: (0, 0), pipeline_mode=pl.Buffered(1))
    return pl.pallas_call(
        _ffn_kernel,
        grid=(m // tm,),
        in_specs=[pl.BlockSpec((tm, D_MODEL), lambda i: (i, 0)),
                  pl.BlockSpec((1, D_MODEL), lambda i: (0, 0)),
                  resident((D_MODEL, ff)), resident((D_MODEL, ff)), resident((ff, D_MODEL))],
        out_specs=pl.BlockSpec((tm, D_MODEL), lambda i: (i, 0)),
        out_shape=jax.ShapeDtypeStruct((m, D_MODEL), F32),
        compiler_params=_cparams(("parallel",)),
        name="ffn",
    )(x2, gain.reshape(1, D_MODEL).astype(F32), w1.astype(BF16), w3.astype(BF16), w2.astype(BF16))


def _router_kernel(x_ref, g_ref, wr_ref, h_ref, ids_ref, ps_ref):
    x = x_ref[...]
    ms = jnp.mean(x * x, axis=-1, keepdims=True)
    h = x * lax.rsqrt(ms + EPS) * g_ref[...]
    h_ref[...] = h
    logits = jnp.dot(h, wr_ref[...], precision=HIGHEST, preferred_element_type=F32)
    lane = lax.broadcasted_iota(jnp.int32, logits.shape, 1)
    neg = jnp.float32(-jnp.inf)
    l1 = jnp.where(lane < N_EXPERTS, logits, neg)
    m1 = jnp.max(l1, axis=-1, keepdims=True)
    i1 = jnp.min(jnp.where(l1 == m1, lane, 128), axis=-1, keepdims=True)
    l2 = jnp.where(lane == i1, neg, l1)
    m2 = jnp.max(l2, axis=-1, keepdims=True)
    i2 = jnp.min(jnp.where(l2 == m2, lane, 128), axis=-1, keepdims=True)
    e = jnp.exp(m2 - m1)
    p1 = 1.0 / (1.0 + e)
    p2 = e / (1.0 + e)
    ids_ref[...] = jnp.where(lane == 0, i1, jnp.where(lane == 1, i2, 0))
    ps_ref[...] = jnp.where(lane == 0, p1, jnp.where(lane == 1, p2, 0.0))


def _router(x2, gain, w_router, tm=512):
    m = x2.shape[0]
    tm = min(tm, m)
    wr = jnp.pad(w_router.astype(F32), ((0, 0), (0, 128 - N_EXPERTS)))
    return pl.pallas_call(
        _router_kernel,
        grid=(m // tm,),
        in_specs=[pl.BlockSpec((tm, D_MODEL), lambda i: (i, 0)),
                  pl.BlockSpec((1, D_MODEL), lambda i: (0, 0)),
                  pl.BlockSpec((D_MODEL, 128), lambda i: (0, 0))],
        out_specs=[pl.BlockSpec((tm, D_MODEL), lambda i: (i, 0)),
                   pl.BlockSpec((tm, 128), lambda i: (i, 0)),
                   pl.BlockSpec((tm, 128), lambda i: (i, 0))],
        out_shape=[jax.ShapeDtypeStruct((m, D_MODEL), F32),
                   jax.ShapeDtypeStruct((m, 128), jnp.int32),
                   jax.ShapeDtypeStruct((m, 128), F32)],
        compiler_params=_cparams(("parallel",)),
        name="router",
    )(x2, gain.reshape(1, D_MODEL).astype(F32), wr)


def _route_positions(ids, tm):
    m = ids.shape[0]
    e_flat = ids[:, :TOP_K].reshape(-1)
    onehot = (e_flat[:, None] == jnp.arange(N_EXPERTS)[None, :]).astype(jnp.int32)
    csum = jnp.cumsum(onehot, axis=0)
    counts = csum[-1]
    gsz = ((counts + tm - 1) // tm) * tm
    gend = jnp.cumsum(gsz)
    pos = jnp.sum(onehot * (gend - gsz + csum - 1), axis=1)
    nt = (TOP_K * m + N_EXPERTS * tm) // tm
    n_used = gend[-1] // tm
    tile_e = jnp.sum((jnp.arange(nt)[:, None] * tm >= gend[None, :]).astype(jnp.int32), axis=1)
    last_e = jnp.sum(((n_used - 1) * tm >= gend).astype(jnp.int32))
    tile_e = jnp.minimum(tile_e, last_e)
    return pos.astype(jnp.int32), tile_e.astype(jnp.int32), n_used.reshape(1).astype(jnp.int32), nt


def _dispatch_kernel(pos_ref, h_ref, init_ref, xs_ref, sem):
    del init_ref
    td = h_ref.shape[0]

    def row_copy(t, s):
        return pltpu.make_async_copy(h_ref.at[pl.ds(t, 1)],
                                     xs_ref.at[pl.ds(pos_ref[0, 0, TOP_K * t + s], 1)], sem)

    def start(t, c):
        for s in range(TOP_K):
            row_copy(t, s).start()
        return c

    def wait(t, c):
        for s in range(TOP_K):
            row_copy(t, s).wait()
        return c

    lax.fori_loop(0, td, start, 0, unroll=8)
    lax.fori_loop(0, td, wait, 0, unroll=8)


def _dispatch(h, pos, npad, td=256):
    m = h.shape[0]
    td = min(td, m)
    return pl.pallas_call(
        _dispatch_kernel,
        grid=(m // td,),
        in_specs=[pl.BlockSpec((1, 1, TOP_K * td), lambda i: (i, 0, 0), memory_space=pltpu.SMEM),
                  pl.BlockSpec((td, D_MODEL), lambda i: (i, 0)),
                  pl.BlockSpec(memory_space=pl.ANY)],
        out_specs=pl.BlockSpec(memory_space=pl.ANY),
        out_shape=jax.ShapeDtypeStruct((npad, D_MODEL), F32),
        scratch_shapes=[pltpu.SemaphoreType.DMA(())],
        input_output_aliases={2: 0},
        compiler_params=_cparams(("arbitrary",)),
        name="moe_dispatch",
    )(pos.reshape(m // td, 1, TOP_K * td), h, jnp.zeros((npad, D_MODEL), F32))


def _experts_kernel(te_ref, nu_ref, xs_ref, w1_ref, w3_ref, w2_ref, y_ref, xb_ref, acc_ref):
    del te_ref
    i = pl.program_id(0)
    j = pl.program_id(1)

    @pl.when(i < nu_ref[0])
    def _():
        @pl.when(j == 0)
        def _():
            xb_ref[...] = xs_ref[...].astype(BF16)
            acc_ref[...] = jnp.zeros_like(acc_ref)

        xb = xb_ref[...]
        a = _dot(xb, w1_ref[0])
        b = _dot(xb, w3_ref[0])
        acc_ref[...] += _dot((_silu(a) * b).astype(BF16), w2_ref[0])

        @pl.when(j == pl.num_programs(1) - 1)
        def _():
            y_ref[...] = acc_ref[...]

    @pl.when((i >= nu_ref[0]) & (j == 0))
    def _():
        y_ref[...] = jnp.zeros_like(y_ref)


def _experts(xs, tile_e, n_used, w1, w3, w2, tm, tf=1792):
    npad = xs.shape[0]
    ff = w1.shape[2]
    nj = ff // tf
    row = lambda i, j, te, nu: (jnp.minimum(i, nu[0] - 1), 0)
    jj = lambda i, j, nu: jnp.where(i < nu[0], j, nj - 1)
    return pl.pallas_call(
        _experts_kernel,
        grid_spec=pltpu.PrefetchScalarGridSpec(
            num_scalar_prefetch=2,
            grid=(npad // tm, nj),
            in_specs=[pl.BlockSpec((tm, D_MODEL), row),
                      pl.BlockSpec((1, D_MODEL, tf), lambda i, j, te, nu: (te[i], 0, jj(i, j, nu))),
                      pl.BlockSpec((1, D_MODEL, tf), lambda i, j, te, nu: (te[i], 0, jj(i, j, nu))),
                      pl.BlockSpec((1, tf, D_MODEL), lambda i, j, te, nu: (te[i], jj(i, j, nu), 0))],
            out_specs=pl.BlockSpec((tm, D_MODEL), lambda i, j, te, nu: (i, 0)),
            scratch_shapes=[pltpu.VMEM((tm, D_MODEL), BF16), pltpu.VMEM((tm, D_MODEL), F32)]),
        out_shape=jax.ShapeDtypeStruct((npad, D_MODEL), F32),
        compiler_params=_cparams(("arbitrary", "arbitrary")),
        name="moe_experts",
    )(tile_e, n_used, xs, w1.astype(BF16), w3.astype(BF16), w2.astype(BF16))


def _combine_kernel(pos_ref, x_ref, ps_ref, y_ref, o_ref, buf_ref, sem):
    td = x_ref.shape[0]

    def row_copy(t, s):
        return pltpu.make_async_copy(y_ref.at[pl.ds(pos_ref[0, 0, TOP_K * t + s], 1)],
                                     buf_ref.at[s, pl.ds(t, 1)], sem.at[s])

    def start(t, c):
        for s in range(TOP_K):
            row_copy(t, s).start()
        return c

    def wait(t, c):
        for s in range(TOP_K):
            row_copy(t, s).wait()
        return c

    lax.fori_loop(0, td, start, 0, unroll=8)
    lax.fori_loop(0, td, wait, 0, unroll=8)
    ps = ps_ref[...]
    o_ref[...] = x_ref[...] + ps[:, 0:1] * buf_ref[0] + ps[:, 1:2] * buf_ref[1]


def _combine(x2, ps, pos, y, td=256):
    m = x2.shape[0]
    td = min(td, m)
    return pl.pallas_call(
        _combine_kernel,
        grid=(m // td,),
        in_specs=[pl.BlockSpec((1, 1, TOP_K * td), lambda i: (i, 0, 0), memory_space=pltpu.SMEM),
                  pl.BlockSpec((td, D_MODEL), lambda i: (i, 0)),
                  pl.BlockSpec((td, 128), lambda i: (i, 0)),
                  pl.BlockSpec(memory_space=pl.ANY)],
        out_specs=pl.BlockSpec((td, D_MODEL), lambda i: (i, 0)),
        out_shape=jax.ShapeDtypeStruct((m, D_MODEL), F32),
        scratch_shapes=[pltpu.VMEM((TOP_K, td, D_MODEL), F32), pltpu.SemaphoreType.DMA((TOP_K,))],
        compiler_params=_cparams(("arbitrary",)),
        name="moe_combine",
    )(pos.reshape(m // td, 1, TOP_K * td), x2, ps, y)


def _moe(x2, gain, w_router, w1, w3, w2, tm=512):
    m = x2.shape[0]
    tm = min(tm, m)
    h, ids, ps = _router(x2, gain, w_router)
    pos, tile_e, n_used, nt = _route_positions(ids, tm)
    xs = _dispatch(h, pos, nt * tm)
    y = _experts(xs, tile_e, n_used, w1, w3, w2, tm)
    return _combine(x2, ps, pos, y)


def _relayout_w_in(w):
    k = w.shape[0]
    main = [w[:, _R_BRG:_R_END], w[:, _R_AQKV:_R_BETA], w[:, _R_AGATE:_R_B], w[:, _R_B:_R_C]]
    second = [w[:, _R_C:_R_BRG], w[:, _R_BETA:_R_AGATE],
              jnp.zeros((k, NP_C - OFF_BA - 2 * HEADS), w.dtype)]
    return (jnp.concatenate(main, axis=1).astype(BF16), jnp.concatenate(second, axis=1).astype(BF16))


def _layer_mixers(x2, b, t, layer, lower_bounds, w_in, norm_mix, conv_a, a_log, dt_bias,
                  gnorm_a, gnorm_b, qnorm_c, knorm_c, rel_bias, w_br_a, w_br_b, w_br_c, w_out):
    w_main, w_second = _relayout_w_in(w_in[layer])
    gain = norm_mix[layer].astype(F32)
    proj2 = _norm_proj(x2, gain, w_main, BF16)
    proj3 = proj2.reshape(b, t, NP_MAIN)
    projc3 = _norm_proj(x2, gain, w_second, F32).reshape(b, t, NP_C)
    oa = _deltanet(proj3, projc3, conv_a[layer], a_log[layer], dt_bias[layer], gnorm_a[layer])
    ob = _hgrn(proj3, lower_bounds[layer], gnorm_b[layer])
    ocs, lses = [], []
    for gi in range(C_GROUPS):
        o, lse = _dilated_group(projc3, gi, rel_bias, qnorm_c[layer, gi], knorm_c[layer, gi])
        ocs.append(o)
        lses.append(lse)
    return _mix(x2, proj2, oa.reshape(b * t, HW), ob.reshape(b * t, HW), ocs, lses,
                w_br_a[layer], w_br_b[layer], w_br_c[layer], w_out[layer])


def kernel(x, w_in, norm_mix, conv_a, a_log, dt_bias, gnorm_a, lb_logits, gnorm_b, qnorm_c, knorm_c, rel_bias, w_br_a, w_br_b, w_br_c, w_out, norm_ffn, ffn_w1, ffn_w3, ffn_w2, router, moe_w1, moe_w3, moe_w2):
    b, t, _ = x.shape
    depth = w_in.shape[0]
    p_lb = jax.nn.softmax(lb_logits.astype(F32), axis=0)
    lower_bounds = jnp.cumsum(p_lb, axis=0) - p_lb[0:1]
    x2 = x.reshape(b * t, D_MODEL).astype(F32)
    for layer in range(depth):
        x2 = _layer_mixers(x2, b, t, layer, lower_bounds, w_in, norm_mix, conv_a, a_log, dt_bias,
                           gnorm_a, gnorm_b, qnorm_c, knorm_c, rel_bias,
                           w_br_a, w_br_b, w_br_c, w_out)
        li = layer // 2
        if layer % 2 == 0:
            x2 = _ffn(x2, norm_ffn[layer], ffn_w1[li], ffn_w3[li], ffn_w2[li])
        else:
            x2 = _moe(x2, norm_ffn[layer], router[li], moe_w1[li], moe_w3[li], moe_w2[li])
    return x2.reshape(b, t, D_MODEL).astype(x.dtype)
```

```python
import functools
import math

import numpy as np
import jax
import jax.numpy as jnp
from jax import lax
from jax.experimental import pallas as pl
from jax.experimental.pallas import tpu as pltpu

F32 = jnp.float32
BF16 = jnp.bfloat16
HIGHEST = lax.Precision.HIGHEST

D_MODEL = 1024
EPS = 1e-6
HEADS = 4
DH = 128
HW = HEADS * DH
A_CONV = 4
A_CHUNK = 64
A_ROWS = 512
B_ROWS = 512
B_BLK = 16
C_PAIRS = ((128, 1), (512, 4), (2048, 16))
C_GROUPS = 3
C_BLOCK = 128
C_TILES = 4
C_NLB = 4
REL_BUCKETS = 32
REL_MAX_DIST = 2048
N_EXPERTS = 8
TOP_K = 2

OFF_BRG = 0
OFF_AQKV = 3072
OFF_AGATE = 4608
OFF_B = 5120
NP_MAIN = 7168
OFF_C = 0
OFF_BA = 4608
NP_C = 5120

_R_AQKV, _R_BETA, _R_AGATE, _R_B, _R_C, _R_BRG, _R_END = 0, 1536, 1544, 2056, 4104, 8712, 11784

VMEM_LIMIT = 56 * 1024 * 1024


def _cparams(sem):
    return pltpu.CompilerParams(dimension_semantics=sem, vmem_limit_bytes=VMEM_LIMIT)


def _sigmoid(x):
    return 1.0 / (1.0 + jnp.exp(-x))


def _silu(x):
    return x * _sigmoid(x)


def _softplus(x):
    return jnp.maximum(x, 0.0) + jnp.log(1.0 + jnp.exp(-jnp.abs(x)))


def _dot(a, b):
    return jnp.dot(a, b, preferred_element_type=F32)


def _dot_nt(a, b, precision=None):
    return lax.dot_general(a, b, (((1,), (1,)), ((), ())), precision=precision,
                           preferred_element_type=F32)


def _dot_tn(a, b):
    return lax.dot_general(a, b, (((0,), (0,)), ((), ())), preferred_element_type=F32)


def _norm_proj_kernel(tn, head_norm_cols, x_ref, g_ref, w_ref, hg_ref, o_ref):
    x = x_ref[...]
    ms = jnp.mean(x * x, axis=-1, keepdims=True)
    h = (x * lax.rsqrt(ms + EPS) * g_ref[...]).astype(BF16)
    for c0 in range(0, w_ref.shape[1], tn):
        r = _dot(h, w_ref[:, c0:c0 + tn])
        if c0 < head_norm_cols:
            heads = [r[:, d0:d0 + DH] for d0 in range(0, tn, DH)]
            heads = [a * lax.rsqrt(jnp.mean(a * a, axis=-1, keepdims=True) + EPS) for a in heads]
            r = jnp.concatenate(heads, axis=1) * hg_ref[:, c0:c0 + tn]
        o_ref[:, c0:c0 + tn] = r.astype(o_ref.dtype)


def _norm_proj(x2, gain, w_bf16, out_dtype, head_gain=None, tm=512, tn=1024):
    m = x2.shape[0]
    n = w_bf16.shape[1]
    tm = min(tm, m)
    norm_cols = 0 if head_gain is None else head_gain.shape[0]
    assert norm_cols % tn == 0
    hg = jnp.zeros((1, n), F32)
    if head_gain is not None:
        hg = hg.at[0, :norm_cols].set(head_gain.astype(F32))
    return pl.pallas_call(
        functools.partial(_norm_proj_kernel, tn, norm_cols),
        grid=(m // tm,),
        in_specs=[pl.BlockSpec((tm, D_MODEL), lambda i: (i, 0)),
                  pl.BlockSpec((1, D_MODEL), lambda i: (0, 0)),
                  pl.BlockSpec((D_MODEL, n), lambda i: (0, 0), pipeline_mode=pl.Buffered(1)),
                  pl.BlockSpec((1, n), lambda i: (0, 0))],
        out_specs=pl.BlockSpec((tm, n), lambda i: (i, 0)),
        out_shape=jax.ShapeDtypeStruct((m, n), out_dtype),
        compiler_params=_cparams(("parallel",)),
        name="norm_proj",
    )(x2, gain.reshape(1, D_MODEL), w_bf16, hg)


def _dotb(a, b):
    return _dot(a.astype(BF16), b.astype(BF16))


def _dot_exact_lhs(a_bf16, b):
    b0 = b.astype(BF16)
    r1 = b - b0.astype(F32)
    b1 = r1.astype(BF16)
    b2 = (r1 - b1.astype(F32)).astype(BF16)
    return _dot(a_bf16, b0) + (_dot(a_bf16, b1) + _dot(a_bf16, b2))


def _deltanet_kernel(qkv_ref, gate_ref, ba_ref, convw_ref, arow_ref, dtrow_ref, gn_ref,
                     o_ref, s_ref, xe_ref):
    C = A_CHUNK
    R = A_ROWS
    HS = range(HEADS)
    IT = range((R // C) * HEADS)

    @pl.when(pl.program_id(1) == 0)
    def _():
        s_ref[...] = jnp.zeros_like(s_ref)
        xe_ref[0:8, :] = jnp.zeros((8, 3 * HW), F32)

    x = qkv_ref[0].astype(F32)
    xe_ref[8:8 + R, :] = x
    w = convw_ref[...]
    y = (w[3:4] * x + w[2:3] * xe_ref[7:7 + R, :] + w[1:2] * xe_ref[6:6 + R, :]
         + w[0:1] * xe_ref[5:5 + R, :])
    xe_ref[0:8, :] = x[R - 8:R]
    y = _silu(y)

    ba = ba_ref[0]
    beta_all = _sigmoid(ba)
    g_all = arow_ref[...] * _softplus(ba + dtrow_ref[...])
    gate = _silu(gate_ref[0].astype(F32))
    s_old = [s_ref[h] for h in HS]

    row = lax.broadcasted_iota(jnp.int32, (C, C), 0)
    col = lax.broadcasted_iota(jnp.int32, (C, C), 1)
    lmat = (col <= row).astype(BF16)
    rowx = lax.broadcasted_iota(jnp.int32, (C, DH + C), 0)
    colx = lax.broadcasted_iota(jnp.int32, (C, DH + C), 1)
    umask = (colx < DH) | (rowx > colx - DH)
    eye = (row == col).astype(F32)
    bd8 = (row >> 3) == (col >> 3)

    def merge_mask(sh):
        return (((row >> (sh + 1)) == (col >> (sh + 1)))
                & (((row >> sh) & 1) == 1) & (((col >> sh) & 1) == 0))

    rows = [slice((i // HEADS) * C, (i // HEADS + 1) * C) for i in IT]
    hd = [i % HEADS for i in IT]
    q = [y[rows[i], hd[i] * DH:(hd[i] + 1) * DH] for i in IT]
    k = [y[rows[i], HW + hd[i] * DH:HW + (hd[i] + 1) * DH] for i in IT]
    v = [y[rows[i], 2 * HW + hd[i] * DH:2 * HW + (hd[i] + 1) * DH] for i in IT]
    q = [a * (lax.rsqrt(jnp.sum(a * a, axis=-1, keepdims=True) + EPS) * (DH ** -0.5)) for a in q]
    k = [a * lax.rsqrt(jnp.sum(a * a, axis=-1, keepdims=True) + EPS) for a in k]
    beta = [beta_all[rows[i], hd[i]:hd[i] + 1] for i in IT]
    gb = [jnp.broadcast_to(g_all[rows[i], HEADS + hd[i]:HEADS + hd[i] + 1], (C, DH + C)) for i in IT]
    dext = [_dot_exact_lhs(lmat, jnp.where(umask, gb[i], 0.0)) for i in IT]
    gc = [d[:, :DH] for d in dext]
    edm = [jnp.exp(d[:, DH:]) for d in dext]
    egc = [jnp.exp(g) for g in gc]
    gl = [g[C - 1:C, :] for g in gc]
    kb = [k[i] * beta[i] for i in IT]
    m = [_dot_nt(kb[i].astype(BF16), k[i].astype(BF16)) * jnp.where(row > col, edm[i], 0.0)
         for i in IT]

    nd = [jnp.where(bd8, -a, 0.0) for a in m]
    p2 = [_dotb(a, a) for a in nd]
    p4 = [_dotb(a, a) for a in p2]
    x1 = [eye + nd[i] for i in IT]
    x1 = [x1[i] + _dotb(x1[i], p2[i]) for i in IT]
    xi = [x1[i] + _dotb(x1[i], p4[i]) for i in IT]
    for sh in (3, 4, 5):
        mm = merge_mask(sh)
        t = [_dotb(xi[i], jnp.where(mm, m[i], 0.0)) for i in IT]
        xi = [xi[i] - _dotb(t[i], xi[i]) for i in IT]

    rhs = [jnp.concatenate([v[i] * beta[i], kb[i] * egc[i]], axis=1) for i in IT]
    sol = [_dotb(xi[i], rhs[i]) for i in IT]
    attn = [(_dot_nt(q[i].astype(BF16), k[i].astype(BF16))
             * jnp.where(row >= col, edm[i], 0.0)).astype(BF16) for i in IT]
    qg = [(q[i] * egc[i]).astype(BF16) for i in IT]
    kg = [(k[i] * jnp.exp(gl[i] - gc[i])).astype(BF16) for i in IT]
    egl = [jnp.exp(a) for a in gl]

    s_cur = s_old
    for c in range(R // C):
        it = [c * HEADS + h for h in HS]
        sb = [a.astype(BF16) for a in s_cur]
        v_new = [(sol[i][:, :DH] - _dot(sol[i][:, DH:].astype(BF16), sb[h])).astype(BF16)
                 for h, i in enumerate(it)]
        o = [_dot(qg[i], sb[h]) + _dot(attn[i], v_new[h]) for h, i in enumerate(it)]
        s_cur = [s_cur[h] * egl[i] + _dot_tn(kg[i], v_new[h]) for h, i in enumerate(it)]
        o = [a * lax.rsqrt(jnp.mean(a * a, axis=-1, keepdims=True) + EPS) * gn_ref[...] for a in o]
        o_ref[0, c * C:(c + 1) * C, :] = (jnp.concatenate(o, axis=1)
                                          * gate[c * C:(c + 1) * C]).astype(o_ref.dtype)
    for h in HS:
        s_ref[h] = s_cur[h]


def _deltanet(proj3, projc3, conv_w, a_log, dt_bias, gnorm):
    b, t, _ = proj3.shape
    C = A_ROWS
    pad = jnp.zeros((HEADS,), F32)
    arow = jnp.concatenate([pad, -jnp.exp(a_log.astype(F32)), jnp.zeros((120,), F32)]).reshape(1, 128)
    dtrow = jnp.concatenate([pad, dt_bias.astype(F32), jnp.zeros((120,), F32)]).reshape(1, 128)
    const = lambda shape: pl.BlockSpec(shape, lambda i, c: (0,) * len(shape))
    return pl.pallas_call(
        _deltanet_kernel,
        grid=(b, t // C),
        in_specs=[pl.BlockSpec((1, C, 3 * HW), lambda i, c: (i, c, OFF_AQKV // (3 * HW))),
                  pl.BlockSpec((1, C, HW), lambda i, c: (i, c, OFF_AGATE // HW)),
                  pl.BlockSpec((1, C, 128), lambda i, c: (i, c, OFF_BA // 128)),
                  const((A_CONV, 3 * HW)), const((1, 128)), const((1, 128)), const((1, DH))],
        out_specs=pl.BlockSpec((1, C, HW), lambda i, c: (i, c, 0)),
        out_shape=jax.ShapeDtypeStruct((b, t, HW), BF16),
        scratch_shapes=[pltpu.VMEM((HEADS, DH, DH), F32), pltpu.VMEM((C + 8, 3 * HW), F32)],
        compiler_params=_cparams(("parallel", "arbitrary")),
        name="deltanet",
    )(proj3, proj3, projc3, conv_w.astype(F32), arow, dtrow, gnorm.reshape(1, DH).astype(F32))


def _hgrn_kernel(q_ref, f_ref, i_ref, g_ref, lb_ref, gn_ref, o_ref, st_ref):
    R, K = B_ROWS, B_BLK
    HS = range(HEADS)

    @pl.when(pl.program_id(1) == 0)
    def _():
        st_ref[...] = jnp.zeros_like(st_ref)

    row = lax.broadcasted_iota(jnp.int32, (K, K), 0)
    col = lax.broadcasted_iota(jnp.int32, (K, K), 1)
    lmat = (col <= row).astype(BF16)
    H8 = 8
    row8 = lax.broadcasted_iota(jnp.int32, (H8, DH), 0)
    sls = [slice(h * DH, (h + 1) * DH) for h in HS]
    lb = [lb_ref[:, sl] for sl in sls]
    st = [st_ref[h] for h in HS]

    for blk in range(R // K):
        rs = slice(blk * K, (blk + 1) * K)
        q = [_silu(q_ref[0, rs, sl].astype(F32)) for sl in sls]
        f = [lb[h] + (1.0 - lb[h]) * _sigmoid(f_ref[0, rs, sls[h]].astype(F32)) for h in HS]
        k = [1.0 - a for a in f]
        v = [i_ref[0, rs, sl].astype(F32) for sl in sls]
        gc = [_dot_exact_lhs(lmat, jnp.log(a)) for a in f]
        o = [_dot_nt((q[h] * jnp.exp(gc[h])).astype(BF16), st[h].astype(BF16)) for h in HS]
        otile = [[o[h][t * H8:(t + 1) * H8] for t in range(K // H8)] for h in HS]
        for j in range(K):
            for t in range(j // H8, K // H8):
                ts = slice(t * H8, (t + 1) * H8)
                for h in HS:
                    d = gc[h][ts] - gc[h][j:j + 1]
                    rel = jnp.exp(jnp.where(row8 >= j - t * H8, d, -1e30) if t == j // H8 else d)
                    sj = jnp.sum(q[h][ts] * k[h][j:j + 1] * rel, axis=-1, keepdims=True)
                    otile[h][t] = otile[h][t] + sj * v[h][j:j + 1]
        o = [jnp.concatenate(otile[h], axis=0) for h in HS]
        gl = [a[K - 1:K] for a in gc]
        kg = [(k[h] * jnp.exp(gl[h] - gc[h])).astype(BF16) for h in HS]
        st = [st[h] * jnp.exp(gl[h]) + _dot_tn(v[h].astype(BF16), kg[h]) for h in HS]
        o = [a * lax.rsqrt(jnp.mean(a * a, axis=-1, keepdims=True) + EPS) * gn_ref[...] for a in o]
        o_ref[0, rs, :] = (jnp.concatenate(o, axis=1) * _sigmoid(g_ref[0, rs, :].astype(F32))).astype(o_ref.dtype)
    for h in HS:
        st_ref[h] = st[h]


def _hgrn(proj3, lb, gnorm):
    b, t, _ = proj3.shape
    R = B_ROWS
    seg = lambda s: pl.BlockSpec((1, R, HW), lambda i, c: (i, c, OFF_B // HW + s))
    const = lambda shape: pl.BlockSpec(shape, lambda i, c: (0,) * len(shape))
    return pl.pallas_call(
        _hgrn_kernel,
        grid=(b, t // R),
        in_specs=[seg(0), seg(1), seg(2), seg(3), const((1, HW)), const((1, DH))],
        out_specs=pl.BlockSpec((1, R, HW), lambda i, c: (i, c, 0)),
        out_shape=jax.ShapeDtypeStruct((b, t, HW), BF16),
        scratch_shapes=[pltpu.VMEM((HEADS, DH, DH), F32)],
        compiler_params=_cparams(("parallel", "arbitrary")),
        name="hgrn2",
    )(proj3, proj3, proj3, proj3, lb.reshape(1, HW).astype(F32), gnorm.reshape(1, DH).astype(F32))


def _t5_bucket_np(n):
    max_exact = REL_BUCKETS // 2
    nf = np.maximum(n, 1).astype(np.float32)
    large = max_exact + (np.log(nf / max_exact) / math.log(REL_MAX_DIST / max_exact)
                         * (REL_BUCKETS - max_exact)).astype(np.int32)
    large = np.minimum(large, REL_BUCKETS - 1)
    return np.where(n < max_exact, n, large)


def _dilated_kernel(span, dil, hpb, nlb, q_ref, kp_ref, kc_ref, vp_ref, vc_ref, bias_ref,
                    o_ref, lse_ref):
    CB = C_BLOCK
    n = pl.program_id(1)
    qi = lax.broadcasted_iota(jnp.int32, (CB, 2 * CB), 0)
    kj = lax.broadcasted_iota(jnp.int32, (CB, 2 * CB), 1)
    dist = qi + CB - kj
    band = (dist >= 0) & (dist <= span)
    band_first = band & ((kj >= CB) | (n > 0))

    def rows(r, jb):
        return pl.ds(jb * CB * dil + r, CB, stride=dil) if dil > 1 else pl.ds(jb * CB, CB)

    def with_prev(p_ref, c_ref, r, jb, sl):
        prev = p_ref[0, rows(r, 0), sl] if jb == 0 else c_ref[0, rows(r, jb - 1), sl]
        return jnp.concatenate([prev, c_ref[0, rows(r, jb), sl]], axis=0)

    def tiles(items):
        n_it = range(len(items))
        q = [q_ref[0, rows(r, jb), sl].astype(BF16) for r, jb, sl, _ in items]
        k = [with_prev(kp_ref, kc_ref, r, jb, sl).astype(BF16) for r, jb, sl, _ in items]
        v = [with_prev(vp_ref, vc_ref, r, jb, sl).astype(BF16) for r, jb, sl, _ in items]
        s = [_dot_nt(q[i], k[i]) * (DH ** -0.5) + bias_ref[items[i][3]] for i in n_it]
        s = [jnp.where(band_first if items[i][1] == 0 else band, s[i], -1e30) for i in n_it]
        mx = [jnp.max(a, axis=-1, keepdims=True) for a in s]
        p = [jnp.exp(s[i] - mx[i]) for i in n_it]
        den = [jnp.sum(a, axis=-1, keepdims=True) for a in p]
        o = [_dot(p[i].astype(BF16), v[i]) / den[i] for i in n_it]
        for i, (r, jb, sl, _) in enumerate(items):
            o_ref[0, rows(r, jb), sl] = o[i]
            lse_ref[0, rows(r, jb), sl] = jnp.broadcast_to(mx[i] + jnp.log(den[i]), (CB, DH))

    for jb in range(nlb):
        if dil == 1:
            for h0 in range(0, hpb, C_TILES):
                tiles([(0, jb, slice(h * DH, (h + 1) * DH), h) for h in range(h0, h0 + C_TILES)])
        elif dil == C_TILES:
            tiles([(r, jb, slice(0, DH), 0) for r in range(dil)])
        else:
            def body(g, carry, jb=jb):
                tiles([(g * C_TILES + i, jb, slice(0, DH), 0) for i in range(C_TILES)])
                return carry
            lax.fori_loop(0, dil // C_TILES, body, 0)


def _dilated_group(proj3, gi, rel_bias):
    b, t, _ = proj3.shape
    window, dil = C_PAIRS[gi]
    span = window // dil
    CB = C_BLOCK
    pb = CB * dil
    nlb = min(C_NLB, t // pb)
    rb = nlb * pb
    hpb = HEADS if dil == 1 else 1
    bw = hpb * DH
    qi = np.arange(CB)[:, None]
    kj = np.arange(2 * CB)[None, :]
    bucket = _t5_bucket_np(np.maximum(qi + CB - kj, 0) * dil)
    onehot = jnp.asarray(np.eye(REL_BUCKETS, dtype=np.float32)[bucket])
    bias = jnp.einsum("qkb,bh->hqk", onehot, rel_bias[:, gi * HEADS:(gi + 1) * HEADS].astype(F32),
                      precision=HIGHEST)

    def seg(which, prev):
        base = (OFF_C + which * C_GROUPS * HW + gi * HW) // bw
        if prev:
            return pl.BlockSpec((1, pb, bw), lambda i, n, h: (i, jnp.maximum(n * nlb - 1, 0), base + h))
        return pl.BlockSpec((1, rb, bw), lambda i, n, h: (i, n, base + h))

    o, lse = pl.pallas_call(
        functools.partial(_dilated_kernel, span, dil, hpb, nlb),
        grid=(b, t // rb, HEADS // hpb),
        in_specs=[seg(0, False), seg(1, True), seg(1, False), seg(2, True), seg(2, False),
                  pl.BlockSpec((hpb, CB, 2 * CB), lambda i, n, h: (h, 0, 0))],
        out_specs=[pl.BlockSpec((1, rb, bw), lambda i, n, h: (i, n, h)),
                   pl.BlockSpec((1, rb, bw), lambda i, n, h: (i, n, h))],
        out_shape=[jax.ShapeDtypeStruct((b, t, HW), F32),
                   jax.ShapeDtypeStruct((b, t, HW), F32)],
        compiler_params=_cparams(("parallel", "arbitrary", "arbitrary")),
        name=f"dilated_g{gi}",
    )(proj3, proj3, proj3, proj3, proj3, bias)
    return o.reshape(b * t, HW), lse.reshape(b * t, HW)


def _mix_kernel(x_ref, gate_ref, oa_ref, ob_ref, oc0_ref, oc1_ref, oc2_ref,
                l0_ref, l1_ref, l2_ref, wa_ref, wb_ref, wc_ref, wo_ref, out_ref):
    l0, l1, l2 = l0_ref[...], l1_ref[...], l2_ref[...]
    mx = jnp.maximum(jnp.maximum(l0, l1), l2)
    e0, e1, e2 = jnp.exp(l0 - mx), jnp.exp(l1 - mx), jnp.exp(l2 - mx)
    oc = (e0 * oc0_ref[...] + e1 * oc1_ref[...] + e2 * oc2_ref[...]) / (e0 + e1 + e2)
    mix = (_sigmoid(gate_ref[:, 0:D_MODEL].astype(F32)) * _dot(oa_ref[...], wa_ref[...])
           + _sigmoid(gate_ref[:, D_MODEL:2 * D_MODEL].astype(F32)) * _dot(ob_ref[...], wb_ref[...])
           + _sigmoid(gate_ref[:, 2 * D_MODEL:3 * D_MODEL].astype(F32)) * _dot(oc.astype(BF16), wc_ref[...]))
    out_ref[...] = x_ref[...] + _dot(mix.astype(BF16), wo_ref[...])


def _mix(x2, proj2, oa, ob, ocs, lses, wa, wb, wc, wo, tm=256):
    m = x2.shape[0]
    tm = min(tm, m)
    rowblk = lambda w: pl.BlockSpec((tm, w), lambda i: (i, 0))
    const = lambda shape: pl.BlockSpec(shape, lambda i: (0,) * len(shape))
    return pl.pallas_call(
        _mix_kernel,
        grid=(m // tm,),
        in_specs=[rowblk(D_MODEL), rowblk(3 * D_MODEL), rowblk(HW), rowblk(HW),
                  rowblk(HW), rowblk(HW), rowblk(HW), rowblk(HW), rowblk(HW), rowblk(HW),
                  const((HW, D_MODEL)), const((HW, D_MODEL)), const((HW, D_MODEL)),
                  const((D_MODEL, D_MODEL))],
        out_specs=rowblk(D_MODEL),
        out_shape=jax.ShapeDtypeStruct((m, D_MODEL), F32),
        compiler_params=_cparams(("parallel",)),
        name="branch_mix",
    )(x2, proj2, oa, ob, ocs[0], ocs[1], ocs[2], lses[0], lses[1], lses[2],
      wa.astype(BF16), wb.astype(BF16), wc.astype(BF16), wo.astype(BF16))


def _ffn_kernel(x_ref, g_ref, w1_ref, w3_ref, w2_ref, o_ref):
    x = x_ref[...]
    ms = jnp.mean(x * x, axis=-1, keepdims=True)
    h = (x * lax.rsqrt(ms + EPS) * g_ref[...]).astype(BF16)
    a = _dot(h, w1_ref[...])
    b = _dot(h, w3_ref[...])
    o_ref[...] = x + _dot((_silu(a) * b).astype(BF16), w2_ref[...])


def _ffn(x2, gain, w1, w3, w2, tm=512):
    m = x2.shape[0]
    ff = w1.shape[1]
    tm = min(tm, m)
    resident = lambda shape: pl.BlockSpec(shape, lambda i: (0, 0), pipeline_mode=pl.Buffered(1))
    return pl.pallas_call(
        _ffn_kernel,
        grid=(m // tm,),
        in_specs=[pl.BlockSpec((tm, D_MODEL), lambda i: (i, 0)),
                  pl.BlockSpec((1, D_MODEL), lambda i: (0, 0)),
                  resident((D_MODEL, ff)), resident((D_MODEL, ff)), resident((ff, D_MODEL))],
        out_specs=pl.BlockSpec((tm, D_MODEL), lambda i: (i, 0)),
        out_shape=jax.ShapeDtypeStruct((m, D_MODEL), F32),
        compiler_params=_cparams(("parallel",)),
        name="ffn",
    )(x2, gain.reshape(1, D_MODEL).astype(F32), w1.astype(BF16), w3.astype(BF16), w2.astype(BF16))


def _router_kernel(x_ref, g_ref, wr_ref, h_ref, ids_ref, ps_ref):
    x = x_ref[...]
    ms = jnp.mean(x * x, axis=-1, keepdims=True)
    h = x * lax.rsqrt(ms + EPS) * g_ref[...]
    h_ref[...] = h
    logits = jnp.dot(h, wr_ref[...], precision=HIGHEST, preferred_element_type=F32)
    lane = lax.broadcasted_iota(jnp.int32, logits.shape, 1)
    neg = jnp.float32(-jnp.inf)
    l1 = jnp.where(lane < N_EXPERTS, logits, neg)
    m1 = jnp.max(l1, axis=-1, keepdims=True)
    i1 = jnp.min(jnp.where(l1 == m1, lane, 128), axis=-1, keepdims=True)
    l2 = jnp.where(lane == i1, neg, l1)
    m2 = jnp.max(l2, axis=-1, keepdims=True)
    i2 = jnp.min(jnp.where(l2 == m2, lane, 128), axis=-1, keepdims=True)
    e = jnp.exp(m2 - m1)
    p1 = 1.0 / (1.0 + e)
    p2 = e / (1.0 + e)
    ids_ref[...] = jnp.where(lane == 0, i1, jnp.where(lane == 1, i2, 0))
    ps_ref[...] = jnp.where(lane == 0, p1, jnp.where(lane == 1, p2, 0.0))


def _router(x2, gain, w_router, tm=512):
    m = x2.shape[0]
    tm = min(tm, m)
    wr = jnp.pad(w_router.astype(F32), ((0, 0), (0, 128 - N_EXPERTS)))
    return pl.pallas_call(
        _router_kernel,
        grid=(m // tm,),
        in_specs=[pl.BlockSpec((tm, D_MODEL), lambda i: (i, 0)),
                  pl.BlockSpec((1, D_MODEL), lambda i: (0, 0)),
                  pl.BlockSpec((D_MODEL, 128), lambda i: (0, 0))],
        out_specs=[pl.BlockSpec((tm, D_MODEL), lambda i: (i, 0)),
                   pl.BlockSpec((tm, 128), lambda i: (i, 0)),
                   pl.BlockSpec((tm, 128), lambda i: (i, 0))],
        out_shape=[jax.ShapeDtypeStruct((m, D_MODEL), F32),
                   jax.ShapeDtypeStruct((m, 128), jnp.int32),
                   jax.ShapeDtypeStruct((m, 128), F32)],
        compiler_params=_cparams(("parallel",)),
        name="router",
    )(x2, gain.reshape(1, D_MODEL).astype(F32), wr)


def _route_positions(ids, tm):
    m = ids.shape[0]
    e_flat = ids[:, :TOP_K].reshape(-1)
    onehot = (e_flat[:, None] == jnp.arange(N_EXPERTS)[None, :]).astype(jnp.int32)
    csum = jnp.cumsum(onehot, axis=0)
    counts = csum[-1]
    gsz = ((counts + tm - 1) // tm) * tm
    gend = jnp.cumsum(gsz)
    pos = jnp.sum(onehot * (gend - gsz + csum - 1), axis=1)
    nt = (TOP_K * m + N_EXPERTS * tm) // tm
    n_used = gend[-1] // tm
    tile_e = jnp.sum((jnp.arange(nt)[:, None] * tm >= gend[None, :]).astype(jnp.int32), axis=1)
    last_e = jnp.sum(((n_used - 1) * tm >= gend).astype(jnp.int32))
    tile_e = jnp.minimum(tile_e, last_e)
    return pos.astype(jnp.int32), tile_e.astype(jnp.int32), n_used.reshape(1).astype(jnp.int32), nt


def _dispatch_kernel(pos_ref, h_ref, init_ref, xs_ref, sem):
    del init_ref
    td = h_ref.shape[0]

    def row_copy(t, s):
        return pltpu.make_async_copy(h_ref.at[pl.ds(t, 1)],
                                     xs_ref.at[pl.ds(pos_ref[0, 0, TOP_K * t + s], 1)], sem)

    def start(t, c):
        for s in range(TOP_K):
            row_copy(t, s).start()
        return c

    def wait(t, c):
        for s in range(TOP_K):
            row_copy(t, s).wait()
        return c

    lax.fori_loop(0, td, start, 0, unroll=8)
    lax.fori_loop(0, td, wait, 0, unroll=8)


def _dispatch(h, pos, npad, td=256):
    m = h.shape[0]
    td = min(td, m)
    return pl.pallas_call(
        _dispatch_kernel,
        grid=(m // td,),
        in_specs=[pl.BlockSpec((1, 1, TOP_K * td), lambda i: (i, 0, 0), memory_space=pltpu.SMEM),
                  pl.BlockSpec((td, D_MODEL), lambda i: (i, 0)),
                  pl.BlockSpec(memory_space=pl.ANY)],
        out_specs=pl.BlockSpec(memory_space=pl.ANY),
        out_shape=jax.ShapeDtypeStruct((npad, D_MODEL), F32),
        scratch_shapes=[pltpu.SemaphoreType.DMA(())],
        input_output_aliases={2: 0},
        compiler_params=_cparams(("arbitrary",)),
        name="moe_dispatch",
    )(pos.reshape(m // td, 1, TOP_K * td), h, jnp.zeros((npad, D_MODEL), F32))


def _experts_kernel(te_ref, nu_ref, xs_ref, w1_ref, w3_ref, w2_ref, y_ref, xb_ref, acc_ref):
    del te_ref
    i = pl.program_id(0)
    j = pl.program_id(1)

    @pl.when(i < nu_ref[0])
    def _():
        @pl.when(j == 0)
        def _():
            xb_ref[...] = xs_ref[...].astype(BF16)
            acc_ref[...] = jnp.zeros_like(acc_ref)

        xb = xb_ref[...]
        a = _dot(xb, w1_ref[0])
        b = _dot(xb, w3_ref[0])
        acc_ref[...] += _dot((_silu(a) * b).astype(BF16), w2_ref[0])

        @pl.when(j == pl.num_programs(1) - 1)
        def _():
            y_ref[...] = acc_ref[...]

    @pl.when((i >= nu_ref[0]) & (j == 0))
    def _():
        y_ref[...] = jnp.zeros_like(y_ref)


def _experts(xs, tile_e, n_used, w1, w3, w2, tm, tf=1792):
    npad = xs.shape[0]
    ff = w1.shape[2]
    nj = ff // tf
    row = lambda i, j, te, nu: (jnp.minimum(i, nu[0] - 1), 0)
    jj = lambda i, j, nu: jnp.where(i < nu[0], j, nj - 1)
    return pl.pallas_call(
        _experts_kernel,
        grid_spec=pltpu.PrefetchScalarGridSpec(
            num_scalar_prefetch=2,
            grid=(npad // tm, nj),
            in_specs=[pl.BlockSpec((tm, D_MODEL), row),
                      pl.BlockSpec((1, D_MODEL, tf), lambda i, j, te, nu: (te[i], 0, jj(i, j, nu))),
                      pl.BlockSpec((1, D_MODEL, tf), lambda i, j, te, nu: (te[i], 0, jj(i, j, nu))),
                      pl.BlockSpec((1, tf, D_MODEL), lambda i, j, te, nu: (te[i], jj(i, j, nu), 0))],
            out_specs=pl.BlockSpec((tm, D_MODEL), lambda i, j, te, nu: (i, 0)),
            scratch_shapes=[pltpu.VMEM((tm, D_MODEL), BF16), pltpu.VMEM((tm, D_MODEL), F32)]),
        out_shape=jax.ShapeDtypeStruct((npad, D_MODEL), F32),
        compiler_params=_cparams(("arbitrary", "arbitrary")),
        name="moe_experts",
    )(tile_e, n_used, xs, w1.astype(BF16), w3.astype(BF16), w2.astype(BF16))


def _combine_kernel(pos_ref, x_ref, ps_ref, y_ref, o_ref, buf_ref, sem):
    td = x_ref.shape[0]

    def row_copy(t, s):
        return pltpu.make_async_copy(y_ref.at[pl.ds(pos_ref[0, 0, TOP_K * t + s], 1)],
                                     buf_ref.at[s, pl.ds(t, 1)], sem.at[s])

    def start(t, c):
        for s in range(TOP_K):
            row_copy(t, s).start()
        return c

    def wait(t, c):
        for s in range(TOP_K):
            row_copy(t, s).wait()
        return c

    lax.fori_loop(0, td, start, 0, unroll=8)
    lax.fori_loop(0, td, wait, 0, unroll=8)
    ps = ps_ref[...]
    o_ref[...] = x_ref[...] + ps[:, 0:1] * buf_ref[0] + ps[:, 1:2] * buf_ref[1]


def _combine(x2, ps, pos, y, td=256):
    m = x2.shape[0]
    td = min(td, m)
    return pl.pallas_call(
        _combine_kernel,
        grid=(m // td,),
        in_specs=[pl.BlockSpec((1, 1, TOP_K * td), lambda i: (i, 0, 0), memory_space=pltpu.SMEM),
                  pl.BlockSpec((td, D_MODEL), lambda i: (i, 0)),
                  pl.BlockSpec((td, 128), lambda i: (i, 0)),
                  pl.BlockSpec(memory_space=pl.ANY)],
        out_specs=pl.BlockSpec((td, D_MODEL), lambda i: (i, 0)),
        out_shape=jax.ShapeDtypeStruct((m, D_MODEL), F32),
        scratch_shapes=[pltpu.VMEM((TOP_K, td, D_MODEL), F32), pltpu.SemaphoreType.DMA((TOP_K,))],
        compiler_params=_cparams(("arbitrary",)),
        name="moe_combine",
    )(pos.reshape(m // td, 1, TOP_K * td), x2, ps, y)


def _moe(x2, gain, w_router, w1, w3, w2, tm=512):
    m = x2.shape[0]
    tm = min(tm, m)
    h, ids, ps = _router(x2, gain, w_router)
    pos, tile_e, n_used, nt = _route_positions(ids, tm)
    xs = _dispatch(h, pos, nt * tm)
    y = _experts(xs, tile_e, n_used, w1, w3, w2, tm)
    return _combine(x2, ps, pos, y)


def _relayout_w_in(w):
    k = w.shape[0]
    main = [w[:, _R_BRG:_R_END], w[:, _R_AQKV:_R_BETA], w[:, _R_AGATE:_R_B], w[:, _R_B:_R_C]]
    second = [w[:, _R_C:_R_BRG], w[:, _R_BETA:_R_AGATE],
              jnp.zeros((k, NP_C - OFF_BA - 2 * HEADS), w.dtype)]
    return (jnp.concatenate(main, axis=1).astype(BF16), jnp.concatenate(second, axis=1).astype(BF16))


def _layer_mixers(x2, b, t, layer, lower_bounds, w_in, norm_mix, conv_a, a_log, dt_bias,
                  gnorm_a, gnorm_b, qnorm_c, knorm_c, rel_bias, w_br_a, w_br_b, w_br_c, w_out):
    w_main, w_second = _relayout_w_in(w_in[layer])
    gain = norm_mix[layer].astype(F32)
    proj2 = _norm_proj(x2, gain, w_main, BF16)
    proj3 = proj2.reshape(b, t, NP_MAIN)
    qk_gain = jnp.concatenate([jnp.tile(qnorm_c[layer].astype(F32), (1, HEADS)).reshape(-1),
                               jnp.tile(knorm_c[layer].astype(F32), (1, HEADS)).reshape(-1)])
    projc3 = _norm_proj(x2, gain, w_second, F32, head_gain=qk_gain).reshape(b, t, NP_C)
    oa = _deltanet(proj3, projc3, conv_a[layer], a_log[layer], dt_bias[layer], gnorm_a[layer])
    ob = _hgrn(proj3, lower_bounds[layer], gnorm_b[layer])
    ocs, lses = [], []
    for gi in range(C_GROUPS):
        o, lse = _dilated_group(projc3, gi, rel_bias)
        ocs.append(o)
        lses.append(lse)
    return _mix(x2, proj2, oa.reshape(b * t, HW), ob.reshape(b * t, HW), ocs, lses,
                w_br_a[layer], w_br_b[layer], w_br_c[layer], w_out[layer])


def kernel(x, w_in, norm_mix, conv_a, a_log, dt_bias, gnorm_a, lb_logits, gnorm_b, qnorm_c, knorm_c, rel_bias, w_br_a, w_br_b, w_br_c, w_out, norm_ffn, ffn_w1, ffn_w3, ffn_w2, router, moe_w1, moe_w3, moe_w2):
    b, t, _ = x.shape
    depth = w_in.shape[0]
    p_lb = jax.nn.softmax(lb_logits.astype(F32), axis=0)
    lower_bounds = jnp.cumsum(p_lb, axis=0) - p_lb[0:1]
    x2 = x.reshape(b * t, D_MODEL).astype(F32)
    for layer in range(depth):
        x2 = _layer_mixers(x2, b, t, layer, lower_bounds, w_in, norm_mix, conv_a, a_log, dt_bias,
                           gnorm_a, gnorm_b, qnorm_c, knorm_c, rel_bias,
                           w_br_a, w_br_b, w_br_c, w_out)
        li = layer // 2
        if layer % 2 == 0:
            x2 = _ffn(x2, norm_ffn[layer], ffn_w1[li], ffn_w3[li], ffn_w2[li])
        else:
            x2 = _moe(x2, norm_ffn[layer], router[li], moe_w1[li], moe_w3[li], moe_w2[li])
    return x2.reshape(b, t, D_MODEL).astype(x.dtype)
```

```python
import functools
import math

import numpy as np
import jax
import jax.numpy as jnp
from jax import lax
from jax.experimental import pallas as pl
from jax.experimental.pallas import tpu as pltpu

F32 = jnp.float32
BF16 = jnp.bfloat16
HIGHEST = lax.Precision.HIGHEST

D_MODEL = 1024
EPS = 1e-6
HEADS = 4
DH = 128
HW = HEADS * DH
A_CONV = 4
A_CHUNK = 64
A_ROWS = 512
B_ROWS = 512
B_BLK = 16
C_PAIRS = ((128, 1), (512, 4), (2048, 16))
C_GROUPS = 3
C_BLOCK = 128
C_TILES = 4
C_NLB = 4
REL_BUCKETS = 32
REL_MAX_DIST = 2048
N_EXPERTS = 8
TOP_K = 2

OFF_BRG = 0
OFF_AQKV = 3072
OFF_AGATE = 4608
OFF_B = 5120
NP_MAIN = 7168
OFF_C = 0
OFF_BA = 4608
NP_C = 5120

_R_AQKV, _R_BETA, _R_AGATE, _R_B, _R_C, _R_BRG, _R_END = 0, 1536, 1544, 2056, 4104, 8712, 11784

VMEM_LIMIT = 56 * 1024 * 1024


def _cparams(sem):
    return pltpu.CompilerParams(dimension_semantics=sem, vmem_limit_bytes=VMEM_LIMIT)


def _sigmoid(x):
    return 1.0 / (1.0 + jnp.exp(-x))


def _silu(x):
    return x * _sigmoid(x)


def _softplus(x):
    return jnp.maximum(x, 0.0) + jnp.log(1.0 + jnp.exp(-jnp.abs(x)))


def _dot(a, b):
    return jnp.dot(a, b, preferred_element_type=F32)


def _dot_nt(a, b, precision=None):
    return lax.dot_general(a, b, (((1,), (1,)), ((), ())), precision=precision,
                           preferred_element_type=F32)


def _dot_tn(a, b):
    return lax.dot_general(a, b, (((0,), (0,)), ((), ())), preferred_element_type=F32)


def _norm_proj_kernel(tn, head_norm_cols, x_ref, g_ref, w_ref, hg_ref, o_ref):
    x = x_ref[...]
    ms = jnp.mean(x * x, axis=-1, keepdims=True)
    h = (x * lax.rsqrt(ms + EPS) * g_ref[...]).astype(BF16)
    for c0 in range(0, w_ref.shape[1], tn):
        r = _dot(h, w_ref[:, c0:c0 + tn])
        if c0 < head_norm_cols:
            heads = [r[:, d0:d0 + DH] for d0 in range(0, tn, DH)]
            heads = [a * lax.rsqrt(jnp.mean(a * a, axis=-1, keepdims=True) + EPS) for a in heads]
            r = jnp.concatenate(heads, axis=1) * hg_ref[:, c0:c0 + tn]
        o_ref[:, c0:c0 + tn] = r.astype(o_ref.dtype)


def _norm_proj(x2, gain, w_bf16, out_dtype, head_gain=None, tm=512, tn=1024):
    m = x2.shape[0]
    n = w_bf16.shape[1]
    tm = min(tm, m)
    norm_cols = 0 if head_gain is None else head_gain.shape[0]
    assert norm_cols % tn == 0
    hg = jnp.zeros((1, n), F32)
    if head_gain is not None:
        hg = hg.at[0, :norm_cols].set(head_gain.astype(F32))
    return pl.pallas_call(
        functools.partial(_norm_proj_kernel, tn, norm_cols),
        grid=(m // tm,),
        in_specs=[pl.BlockSpec((tm, D_MODEL), lambda i: (i, 0)),
                  pl.BlockSpec((1, D_MODEL), lambda i: (0, 0)),
                  pl.BlockSpec((D_MODEL, n), lambda i: (0, 0), pipeline_mode=pl.Buffered(1)),
                  pl.BlockSpec((1, n), lambda i: (0, 0))],
        out_specs=pl.BlockSpec((tm, n), lambda i: (i, 0)),
        out_shape=jax.ShapeDtypeStruct((m, n), out_dtype),
        compiler_params=_cparams(("parallel",)),
        name="norm_proj",
    )(x2, gain.reshape(1, D_MODEL), w_bf16, hg)


def _dotb(a, b):
    return _dot(a.astype(BF16), b.astype(BF16))


def _dot_exact_lhs(a_bf16, b):
    b0 = b.astype(BF16)
    r1 = b - b0.astype(F32)
    b1 = r1.astype(BF16)
    b2 = (r1 - b1.astype(F32)).astype(BF16)
    return _dot(a_bf16, b0) + (_dot(a_bf16, b1) + _dot(a_bf16, b2))


def _deltanet_kernel(qkv_ref, gate_ref, ba_ref, convw_ref, arow_ref, dtrow_ref, gn_ref,
                     o_ref, s_ref, xe_ref):
    C = A_CHUNK
    R = A_ROWS
    HS = range(HEADS)
    IT = range((R // C) * HEADS)

    @pl.when(pl.program_id(1) == 0)
    def _():
        s_ref[...] = jnp.zeros_like(s_ref)
        xe_ref[0:8, :] = jnp.zeros((8, 3 * HW), F32)

    x = qkv_ref[0].astype(F32)
    xe_ref[8:8 + R, :] = x
    w = convw_ref[...]
    y = (w[3:4] * x + w[2:3] * xe_ref[7:7 + R, :] + w[1:2] * xe_ref[6:6 + R, :]
         + w[0:1] * xe_ref[5:5 + R, :])
    xe_ref[0:8, :] = x[R - 8:R]
    y = _silu(y)

    ba = ba_ref[0]
    beta_all = _sigmoid(ba)
    g_all = arow_ref[...] * _softplus(ba + dtrow_ref[...])
    gate = _silu(gate_ref[0].astype(F32))
    s_old = [s_ref[h] for h in HS]

    row = lax.broadcasted_iota(jnp.int32, (C, C), 0)
    col = lax.broadcasted_iota(jnp.int32, (C, C), 1)
    lmat = (col <= row).astype(BF16)
    rowx = lax.broadcasted_iota(jnp.int32, (C, DH + C), 0)
    colx = lax.broadcasted_iota(jnp.int32, (C, DH + C), 1)
    umask = (colx < DH) | (rowx > colx - DH)
    eye = (row == col).astype(F32)
    bd8 = (row >> 3) == (col >> 3)

    def merge_mask(sh):
        return (((row >> (sh + 1)) == (col >> (sh + 1)))
                & (((row >> sh) & 1) == 1) & (((col >> sh) & 1) == 0))

    rows = [slice((i // HEADS) * C, (i // HEADS + 1) * C) for i in IT]
    hd = [i % HEADS for i in IT]
    q = [y[rows[i], hd[i] * DH:(hd[i] + 1) * DH] for i in IT]
    k = [y[rows[i], HW + hd[i] * DH:HW + (hd[i] + 1) * DH] for i in IT]
    v = [y[rows[i], 2 * HW + hd[i] * DH:2 * HW + (hd[i] + 1) * DH] for i in IT]
    q = [a * (lax.rsqrt(jnp.sum(a * a, axis=-1, keepdims=True) + EPS) * (DH ** -0.5)) for a in q]
    k = [a * lax.rsqrt(jnp.sum(a * a, axis=-1, keepdims=True) + EPS) for a in k]
    beta = [beta_all[rows[i], hd[i]:hd[i] + 1] for i in IT]
    gb = [jnp.broadcast_to(g_all[rows[i], HEADS + hd[i]:HEADS + hd[i] + 1], (C, DH + C)) for i in IT]
    dext = [_dot_exact_lhs(lmat, jnp.where(umask, gb[i], 0.0)) for i in IT]
    gc = [d[:, :DH] for d in dext]
    edm = [jnp.exp(d[:, DH:]) for d in dext]
    egc = [jnp.exp(g) for g in gc]
    gl = [g[C - 1:C, :] for g in gc]
    kb = [k[i] * beta[i] for i in IT]
    m = [_dot_nt(kb[i].astype(BF16), k[i].astype(BF16)) * jnp.where(row > col, edm[i], 0.0)
         for i in IT]

    nd = [jnp.where(bd8, -a, 0.0) for a in m]
    p2 = [_dotb(a, a) for a in nd]
    p4 = [_dotb(a, a) for a in p2]
    x1 = [eye + nd[i] for i in IT]
    x1 = [x1[i] + _dotb(x1[i], p2[i]) for i in IT]
    xi = [x1[i] + _dotb(x1[i], p4[i]) for i in IT]
    for sh in (3, 4, 5):
        mm = merge_mask(sh)
        t = [_dotb(xi[i], jnp.where(mm, m[i], 0.0)) for i in IT]
        xi = [xi[i] - _dotb(t[i], xi[i]) for i in IT]

    rhs = [jnp.concatenate([v[i] * beta[i], kb[i] * egc[i]], axis=1) for i in IT]
    sol = [_dotb(xi[i], rhs[i]) for i in IT]
    attn = [(_dot_nt(q[i].astype(BF16), k[i].astype(BF16))
             * jnp.where(row >= col, edm[i], 0.0)).astype(BF16) for i in IT]
    qg = [(q[i] * egc[i]).astype(BF16) for i in IT]
    kg = [(k[i] * jnp.exp(gl[i] - gc[i])).astype(BF16) for i in IT]
    egl = [jnp.exp(a) for a in gl]

    s_cur = s_old
    for c in range(R // C):
        it = [c * HEADS + h for h in HS]
        sb = [a.astype(BF16) for a in s_cur]
        v_new = [(sol[i][:, :DH] - _dot(sol[i][:, DH:].astype(BF16), sb[h])).astype(BF16)
                 for h, i in enumerate(it)]
        o = [_dot(qg[i], sb[h]) + _dot(attn[i], v_new[h]) for h, i in enumerate(it)]
        s_cur = [s_cur[h] * egl[i] + _dot_tn(kg[i], v_new[h]) for h, i in enumerate(it)]
        o = [a * lax.rsqrt(jnp.mean(a * a, axis=-1, keepdims=True) + EPS) * gn_ref[...] for a in o]
        o_ref[0, c * C:(c + 1) * C, :] = (jnp.concatenate(o, axis=1)
                                          * gate[c * C:(c + 1) * C]).astype(o_ref.dtype)
    for h in HS:
        s_ref[h] = s_cur[h]


def _deltanet(proj3, projc3, conv_w, a_log, dt_bias, gnorm):
    b, t, _ = proj3.shape
    C = A_ROWS
    pad = jnp.zeros((HEADS,), F32)
    arow = jnp.concatenate([pad, -jnp.exp(a_log.astype(F32)), jnp.zeros((120,), F32)]).reshape(1, 128)
    dtrow = jnp.concatenate([pad, dt_bias.astype(F32), jnp.zeros((120,), F32)]).reshape(1, 128)
    const = lambda shape: pl.BlockSpec(shape, lambda i, c: (0,) * len(shape))
    return pl.pallas_call(
        _deltanet_kernel,
        grid=(b, t // C),
        in_specs=[pl.BlockSpec((1, C, 3 * HW), lambda i, c: (i, c, OFF_AQKV // (3 * HW))),
                  pl.BlockSpec((1, C, HW), lambda i, c: (i, c, OFF_AGATE // HW)),
                  pl.BlockSpec((1, C, 128), lambda i, c: (i, c, OFF_BA // 128)),
                  const((A_CONV, 3 * HW)), const((1, 128)), const((1, 128)), const((1, DH))],
        out_specs=pl.BlockSpec((1, C, HW), lambda i, c: (i, c, 0)),
        out_shape=jax.ShapeDtypeStruct((b, t, HW), BF16),
        scratch_shapes=[pltpu.VMEM((HEADS, DH, DH), F32), pltpu.VMEM((C + 8, 3 * HW), F32)],
        compiler_params=_cparams(("parallel", "arbitrary")),
        name="deltanet",
    )(proj3, proj3, projc3, conv_w.astype(F32), arow, dtrow, gnorm.reshape(1, DH).astype(F32))


def _hgrn_kernel(q_ref, f_ref, i_ref, g_ref, lb_ref, gn_ref, o_ref, st_ref):
    R, K = B_ROWS, B_BLK
    HS = range(HEADS)

    @pl.when(pl.program_id(1) == 0)
    def _():
        st_ref[...] = jnp.zeros_like(st_ref)

    row = lax.broadcasted_iota(jnp.int32, (K, K), 0)
    col = lax.broadcasted_iota(jnp.int32, (K, K), 1)
    lmat = (col <= row).astype(BF16)
    H8 = 8
    row8 = lax.broadcasted_iota(jnp.int32, (H8, DH), 0)
    sls = [slice(h * DH, (h + 1) * DH) for h in HS]
    lb = [lb_ref[:, sl] for sl in sls]
    st = [st_ref[h] for h in HS]

    for blk in range(R // K):
        rs = slice(blk * K, (blk + 1) * K)
        q = [_silu(q_ref[0, rs, sl].astype(F32)) for sl in sls]
        f = [lb[h] + (1.0 - lb[h]) * _sigmoid(f_ref[0, rs, sls[h]].astype(F32)) for h in HS]
        k = [1.0 - a for a in f]
        v = [i_ref[0, rs, sl].astype(F32) for sl in sls]
        gc = [_dot_exact_lhs(lmat, jnp.log(a)) for a in f]
        o = [_dot_nt((q[h] * jnp.exp(gc[h])).astype(BF16), st[h].astype(BF16)) for h in HS]
        otile = [[o[h][t * H8:(t + 1) * H8] for t in range(K // H8)] for h in HS]
        for j in range(K):
            for t in range(j // H8, K // H8):
                ts = slice(t * H8, (t + 1) * H8)
                for h in HS:
                    d = gc[h][ts] - gc[h][j:j + 1]
                    rel = jnp.exp(jnp.where(row8 >= j - t * H8, d, -1e30) if t == j // H8 else d)
                    sj = jnp.sum(q[h][ts] * k[h][j:j + 1] * rel, axis=-1, keepdims=True)
                    otile[h][t] = otile[h][t] + sj * v[h][j:j + 1]
        o = [jnp.concatenate(otile[h], axis=0) for h in HS]
        gl = [a[K - 1:K] for a in gc]
        kg = [(k[h] * jnp.exp(gl[h] - gc[h])).astype(BF16) for h in HS]
        st = [st[h] * jnp.exp(gl[h]) + _dot_tn(v[h].astype(BF16), kg[h]) for h in HS]
        o = [a * lax.rsqrt(jnp.mean(a * a, axis=-1, keepdims=True) + EPS) * gn_ref[...] for a in o]
        o_ref[0, rs, :] = (jnp.concatenate(o, axis=1) * _sigmoid(g_ref[0, rs, :].astype(F32))).astype(o_ref.dtype)
    for h in HS:
        st_ref[h] = st[h]


def _hgrn(proj3, lb, gnorm):
    b, t, _ = proj3.shape
    R = B_ROWS
    seg = lambda s: pl.BlockSpec((1, R, HW), lambda i, c: (i, c, OFF_B // HW + s))
    const = lambda shape: pl.BlockSpec(shape, lambda i, c: (0,) * len(shape))
    return pl.pallas_call(
        _hgrn_kernel,
        grid=(b, t // R),
        in_specs=[seg(0), seg(1), seg(2), seg(3), const((1, HW)), const((1, DH))],
        out_specs=pl.BlockSpec((1, R, HW), lambda i, c: (i, c, 0)),
        out_shape=jax.ShapeDtypeStruct((b, t, HW), BF16),
        scratch_shapes=[pltpu.VMEM((HEADS, DH, DH), F32)],
        compiler_params=_cparams(("parallel", "arbitrary")),
        name="hgrn2",
    )(proj3, proj3, proj3, proj3, lb.reshape(1, HW).astype(F32), gnorm.reshape(1, DH).astype(F32))


def _t5_bucket_np(n):
    max_exact = REL_BUCKETS // 2
    nf = np.maximum(n, 1).astype(np.float32)
    large = max_exact + (np.log(nf / max_exact) / math.log(REL_MAX_DIST / max_exact)
                         * (REL_BUCKETS - max_exact)).astype(np.int32)
    large = np.minimum(large, REL_BUCKETS - 1)
    return np.where(n < max_exact, n, large)


def _dilated_kernel(span, dil, hpb, nlb, q_ref, kp_ref, kc_ref, vp_ref, vc_ref, bias_ref,
                    o_ref, lse_ref):
    CB = C_BLOCK
    n = pl.program_id(1)
    qi = lax.broadcasted_iota(jnp.int32, (CB, 2 * CB), 0)
    kj = lax.broadcasted_iota(jnp.int32, (CB, 2 * CB), 1)
    dist = qi + CB - kj
    band = (dist >= 0) & (dist <= span)
    band_first = band & ((kj >= CB) | (n > 0))

    def rows(r, jb):
        return pl.ds(jb * CB * dil + r, CB, stride=dil) if dil > 1 else pl.ds(jb * CB, CB)

    def with_prev(p_ref, c_ref, r, jb, sl):
        prev = p_ref[0, rows(r, 0), sl] if jb == 0 else c_ref[0, rows(r, jb - 1), sl]
        return jnp.concatenate([prev, c_ref[0, rows(r, jb), sl]], axis=0)

    def tiles(items):
        n_it = range(len(items))
        q = [q_ref[0, rows(r, jb), sl].astype(BF16) for r, jb, sl, _ in items]
        k = [with_prev(kp_ref, kc_ref, r, jb, sl).astype(BF16) for r, jb, sl, _ in items]
        v = [with_prev(vp_ref, vc_ref, r, jb, sl).astype(BF16) for r, jb, sl, _ in items]
        s = [_dot_nt(q[i], k[i]) * (DH ** -0.5) + bias_ref[items[i][3]] for i in n_it]
        s = [jnp.where(band_first if items[i][1] == 0 else band, s[i], -1e30) for i in n_it]
        mx = [jnp.max(a, axis=-1, keepdims=True) for a in s]
        p = [jnp.exp(s[i] - mx[i]) for i in n_it]
        den = [jnp.sum(a, axis=-1, keepdims=True) for a in p]
        o = [_dot(p[i].astype(BF16), v[i]) / den[i] for i in n_it]
        for i, (r, jb, sl, _) in enumerate(items):
            o_ref[0, rows(r, jb), sl] = o[i]
            lse_ref[0, rows(r, jb), sl] = jnp.broadcast_to(mx[i] + jnp.log(den[i]), (CB, DH))

    for jb in range(nlb):
        if dil == 1:
            for h0 in range(0, hpb, C_TILES):
                tiles([(0, jb, slice(h * DH, (h + 1) * DH), h) for h in range(h0, h0 + C_TILES)])
        elif dil == C_TILES:
            tiles([(r, jb, slice(0, DH), 0) for r in range(dil)])
        else:
            def body(g, carry, jb=jb):
                tiles([(g * C_TILES + i, jb, slice(0, DH), 0) for i in range(C_TILES)])
                return carry
            lax.fori_loop(0, dil // C_TILES, body, 0)


def _dilated_group(proj3, gi, rel_bias):
    b, t, _ = proj3.shape
    window, dil = C_PAIRS[gi]
    span = window // dil
    CB = C_BLOCK
    pb = CB * dil
    nlb = min(C_NLB, t // pb)
    rb = nlb * pb
    hpb = HEADS if dil == 1 else 1
    bw = hpb * DH
    qi = np.arange(CB)[:, None]
    kj = np.arange(2 * CB)[None, :]
    bucket = _t5_bucket_np(np.maximum(qi + CB - kj, 0) * dil)
    onehot = jnp.asarray(np.eye(REL_BUCKETS, dtype=np.float32)[bucket])
    bias = jnp.einsum("qkb,bh->hqk", onehot, rel_bias[:, gi * HEADS:(gi + 1) * HEADS].astype(F32),
                      precision=HIGHEST)

    def seg(which, prev):
        base = (OFF_C + which * C_GROUPS * HW + gi * HW) // bw
        if prev:
            return pl.BlockSpec((1, pb, bw), lambda i, n, h: (i, jnp.maximum(n * nlb - 1, 0), base + h))
        return pl.BlockSpec((1, rb, bw), lambda i, n, h: (i, n, base + h))

    o, lse = pl.pallas_call(
        functools.partial(_dilated_kernel, span, dil, hpb, nlb),
        grid=(b, t // rb, HEADS // hpb),
        in_specs=[seg(0, False), seg(1, True), seg(1, False), seg(2, True), seg(2, False),
                  pl.BlockSpec((hpb, CB, 2 * CB), lambda i, n, h: (h, 0, 0))],
        out_specs=[pl.BlockSpec((1, rb, bw), lambda i, n, h: (i, n, h)),
                   pl.BlockSpec((1, rb, bw), lambda i, n, h: (i, n, h))],
        out_shape=[jax.ShapeDtypeStruct((b, t, HW), F32),
                   jax.ShapeDtypeStruct((b, t, HW), F32)],
        compiler_params=_cparams(("parallel", "arbitrary", "arbitrary")),
        name=f"dilated_g{gi}",
    )(proj3, proj3, proj3, proj3, proj3, bias)
    return o.reshape(b * t, HW), lse.reshape(b * t, HW)


def _mix_kernel(x_ref, gate_ref, oa_ref, ob_ref, oc0_ref, oc1_ref, oc2_ref,
                l0_ref, l1_ref, l2_ref, wa_ref, wb_ref, wc_ref, wo_ref, out_ref):
    l0, l1, l2 = l0_ref[...], l1_ref[...], l2_ref[...]
    mx = jnp.maximum(jnp.maximum(l0, l1), l2)
    e0, e1, e2 = jnp.exp(l0 - mx), jnp.exp(l1 - mx), jnp.exp(l2 - mx)
    oc = (e0 * oc0_ref[...] + e1 * oc1_ref[...] + e2 * oc2_ref[...]) / (e0 + e1 + e2)
    mix = (_sigmoid(gate_ref[:, 0:D_MODEL].astype(F32)) * _dot(oa_ref[...], wa_ref[...])
           + _sigmoid(gate_ref[:, D_MODEL:2 * D_MODEL].astype(F32)) * _dot(ob_ref[...], wb_ref[...])
           + _sigmoid(gate_ref[:, 2 * D_MODEL:3 * D_MODEL].astype(F32)) * _dot(oc.astype(BF16), wc_ref[...]))
    out_ref[...] = x_ref[...] + _dot(mix.astype(BF16), wo_ref[...])


def _mix(x2, proj2, oa, ob, ocs, lses, wa, wb, wc, wo, tm=256):
    m = x2.shape[0]
    tm = min(tm, m)
    rowblk = lambda w: pl.BlockSpec((tm, w), lambda i: (i, 0))
    const = lambda shape: pl.BlockSpec(shape, lambda i: (0,) * len(shape))
    return pl.pallas_call(
        _mix_kernel,
        grid=(m // tm,),
        in_specs=[rowblk(D_MODEL), rowblk(3 * D_MODEL), rowblk(HW), rowblk(HW),
                  rowblk(HW), rowblk(HW), rowblk(HW), rowblk(HW), rowblk(HW), rowblk(HW),
                  const((HW, D_MODEL)), const((HW, D_MODEL)), const((HW, D_MODEL)),
                  const((D_MODEL, D_MODEL))],
        out_specs=rowblk(D_MODEL),
        out_shape=jax.ShapeDtypeStruct((m, D_MODEL), F32),
        compiler_params=_cparams(("parallel",)),
        name="branch_mix",
    )(x2, proj2, oa, ob, ocs[0], ocs[1], ocs[2], lses[0], lses[1], lses[2],
      wa.astype(BF16), wb.astype(BF16), wc.astype(BF16), wo.astype(BF16))


def _ffn_kernel(x_ref, g_ref, w1_ref, w3_ref, w2_ref, o_ref):
    x = x_ref[...]
    ms = jnp.mean(x * x, axis=-1, keepdims=True)
    h = (x * lax.rsqrt(ms + EPS) * g_ref[...]).astype(BF16)
    a = _dot(h, w1_ref[...])
    b = _dot(h, w3_ref[...])
    o_ref[...] = x + _dot((_silu(a) * b).astype(BF16), w2_ref[...])


def _ffn(x2, gain, w1, w3, w2, tm=512):
    m = x2.shape[0]
    ff = w1.shape[1]
    tm = min(tm, m)
    resident = lambda shape: pl.BlockSpec(shape, lambda i: (0, 0), pipeline_mode=pl.Buffered(1))
    return pl.pallas_call(
        _ffn_kernel,
        grid=(m // tm,),
        in_specs=[pl.BlockSpec((tm, D_MODEL), lambda i: (i, 0)),
                  pl.BlockSpec((1, D_MODEL), lambda i: (0, 0)),
                  resident((D_MODEL, ff)), resident((D_MODEL, ff)), resident((ff, D_MODEL))],
        out_specs=pl.BlockSpec((tm, D_MODEL), lambda i: (i, 0)),
        out_shape=jax.ShapeDtypeStruct((m, D_MODEL), F32),
        compiler_params=_cparams(("parallel",)),
        name="ffn",
    )(x2, gain.reshape(1, D_MODEL).astype(F32), w1.astype(BF16), w3.astype(BF16), w2.astype(BF16))


def _router_kernel(x_ref, g_ref, wr_ref, h_ref, ids_ref, ps_ref):
    x = x_ref[...]
    ms = jnp.mean(x * x, axis=-1, keepdims=True)
    h = x * lax.rsqrt(ms + EPS) * g_ref[...]
    h_ref[...] = h
    logits = jnp.dot(h, wr_ref[...], precision=HIGHEST, preferred_element_type=F32)
    lane = lax.broadcasted_iota(jnp.int32, logits.shape, 1)
    neg = jnp.float32(-jnp.inf)
    l1 = jnp.where(lane < N_EXPERTS, logits, neg)
    m1 = jnp.max(l1, axis=-1, keepdims=True)
    i1 = jnp.min(jnp.where(l1 == m1, lane, 128), axis=-1, keepdims=True)
    l2 = jnp.where(lane == i1, neg, l1)
    m2 = jnp.max(l2, axis=-1, keepdims=True)
    i2 = jnp.min(jnp.where(l2 == m2, lane, 128), axis=-1, keepdims=True)
    e = jnp.exp(m2 - m1)
    p1 = 1.0 / (1.0 + e)
    p2 = e / (1.0 + e)
    ids_ref[...] = jnp.where(lane == 0, i1, jnp.where(lane == 1, i2, 0))
    ps_ref[...] = jnp.where(lane == 0, p1, jnp.where(lane == 1, p2, 0.0))


def _router(x2, gain, w_router, tm=512):
    m = x2.shape[0]
    tm = min(tm, m)
    wr = jnp.pad(w_router.astype(F32), ((0, 0), (0, 128 - N_EXPERTS)))
    return pl.pallas_call(
        _router_kernel,
        grid=(m // tm,),
        in_specs=[pl.BlockSpec((tm, D_MODEL), lambda i: (i, 0)),
                  pl.BlockSpec((1, D_MODEL), lambda i: (0, 0)),
                  pl.BlockSpec((D_MODEL, 128), lambda i: (0, 0))],
        out_specs=[pl.BlockSpec((tm, D_MODEL), lambda i: (i, 0)),
                   pl.BlockSpec((tm, 128), lambda i: (i, 0)),
                   pl.BlockSpec((tm, 128), lambda i: (i, 0))],
        out_shape=[jax.ShapeDtypeStruct((m, D_MODEL), F32),
                   jax.ShapeDtypeStruct((m, 128), jnp.int32),
                   jax.ShapeDtypeStruct((m, 128), F32)],
        compiler_params=_cparams(("parallel",)),
        name="router",
    )(x2, gain.reshape(1, D_MODEL).astype(F32), wr)


def _route_positions(ids, tm):
    m = ids.shape[0]
    e_flat = ids[:, :TOP_K].reshape(-1)
    onehot = (e_flat[:, None] == jnp.arange(N_EXPERTS)[None, :]).astype(jnp.int32)
    csum = jnp.cumsum(onehot, axis=0)
    counts = csum[-1]
    gsz = ((counts + tm - 1) // tm) * tm
    gend = jnp.cumsum(gsz)
    pos = jnp.sum(onehot * (gend - gsz + csum - 1), axis=1)
    nt = (TOP_K * m + N_EXPERTS * tm) // tm
    n_used = gend[-1] // tm
    tile_e = jnp.sum((jnp.arange(nt)[:, None] * tm >= gend[None, :]).astype(jnp.int32), axis=1)
    last_e = jnp.sum(((n_used - 1) * tm >= gend).astype(jnp.int32))
    tile_e = jnp.minimum(tile_e, last_e)
    return pos.astype(jnp.int32), tile_e.astype(jnp.int32), n_used.reshape(1).astype(jnp.int32), nt


def _dispatch_kernel(pos_ref, h_ref, init_ref, xs_ref, sem):
    del init_ref
    td = h_ref.shape[0]

    def row_copy(t, s):
        return pltpu.make_async_copy(h_ref.at[pl.ds(t, 1)],
                                     xs_ref.at[pl.ds(pos_ref[0, 0, TOP_K * t + s], 1)], sem)

    def start(t, c):
        for s in range(TOP_K):
            row_copy(t, s).start()
        return c

    lax.fori_loop(0, td, start, 0, unroll=8)
    for s in range(TOP_K):
        pltpu.make_async_copy(h_ref, xs_ref.at[pl.ds(0, td)], sem).wait()


def _dispatch(h, pos, npad, td=256):
    m = h.shape[0]
    td = min(td, m)
    return pl.pallas_call(
        _dispatch_kernel,
        grid=(m // td,),
        in_specs=[pl.BlockSpec((1, 1, TOP_K * td), lambda i: (i, 0, 0), memory_space=pltpu.SMEM),
                  pl.BlockSpec((td, D_MODEL), lambda i: (i, 0)),
                  pl.BlockSpec(memory_space=pl.ANY)],
        out_specs=pl.BlockSpec(memory_space=pl.ANY),
        out_shape=jax.ShapeDtypeStruct((npad, D_MODEL), F32),
        scratch_shapes=[pltpu.SemaphoreType.DMA(())],
        input_output_aliases={2: 0},
        compiler_params=_cparams(("arbitrary",)),
        name="moe_dispatch",
    )(pos.reshape(m // td, 1, TOP_K * td), h, jnp.zeros((npad, D_MODEL), F32))


def _experts_kernel(te_ref, nu_ref, xs_ref, w1_ref, w3_ref, w2_ref, y_ref, xb_ref, acc_ref):
    del te_ref
    i = pl.program_id(0)
    j = pl.program_id(1)

    @pl.when(i < nu_ref[0])
    def _():
        @pl.when(j == 0)
        def _():
            xb_ref[...] = xs_ref[...].astype(BF16)
            acc_ref[...] = jnp.zeros_like(acc_ref)

        xb = xb_ref[...]
        a = _dot(xb, w1_ref[0])
        b = _dot(xb, w3_ref[0])
        acc_ref[...] += _dot((_silu(a) * b).astype(BF16), w2_ref[0])

        @pl.when(j == pl.num_programs(1) - 1)
        def _():
            y_ref[...] = acc_ref[...]

    @pl.when((i >= nu_ref[0]) & (j == 0))
    def _():
        y_ref[...] = jnp.zeros_like(y_ref)


def _experts(xs, tile_e, n_used, w1, w3, w2, tm, tf=1792):
    npad = xs.shape[0]
    ff = w1.shape[2]
    nj = ff // tf
    row = lambda i, j, te, nu: (jnp.minimum(i, nu[0] - 1), 0)
    jj = lambda i, j, nu: jnp.where(i < nu[0], j, nj - 1)
    return pl.pallas_call(
        _experts_kernel,
        grid_spec=pltpu.PrefetchScalarGridSpec(
            num_scalar_prefetch=2,
            grid=(npad // tm, nj),
            in_specs=[pl.BlockSpec((tm, D_MODEL), row),
                      pl.BlockSpec((1, D_MODEL, tf), lambda i, j, te, nu: (te[i], 0, jj(i, j, nu))),
                      pl.BlockSpec((1, D_MODEL, tf), lambda i, j, te, nu: (te[i], 0, jj(i, j, nu))),
                      pl.BlockSpec((1, tf, D_MODEL), lambda i, j, te, nu: (te[i], jj(i, j, nu), 0))],
            out_specs=pl.BlockSpec((tm, D_MODEL), lambda i, j, te, nu: (i, 0)),
            scratch_shapes=[pltpu.VMEM((tm, D_MODEL), BF16), pltpu.VMEM((tm, D_MODEL), F32)]),
        out_shape=jax.ShapeDtypeStruct((npad, D_MODEL), F32),
        compiler_params=_cparams(("arbitrary", "arbitrary")),
        name="moe_experts",
    )(tile_e, n_used, xs, w1.astype(BF16), w3.astype(BF16), w2.astype(BF16))


def _combine_kernel(pos_ref, x_ref, ps_ref, y_ref, o_ref, buf_ref, sem):
    td = x_ref.shape[0]

    def row_copy(t, s):
        return pltpu.make_async_copy(y_ref.at[pl.ds(pos_ref[0, 0, TOP_K * t + s], 1)],
                                     buf_ref.at[s, pl.ds(t, 1)], sem.at[s])

    def start(t, c):
        for s in range(TOP_K):
            row_copy(t, s).start()
        return c

    lax.fori_loop(0, td, start, 0, unroll=8)
    for s in range(TOP_K):
        pltpu.make_async_copy(y_ref.at[pl.ds(0, td)], buf_ref.at[s], sem.at[s]).wait()
    ps = ps_ref[...]
    o_ref[...] = x_ref[...] + ps[:, 0:1] * buf_ref[0] + ps[:, 1:2] * buf_ref[1]


def _combine(x2, ps, pos, y, td=256):
    m = x2.shape[0]
    td = min(td, m)
    return pl.pallas_call(
        _combine_kernel,
        grid=(m // td,),
        in_specs=[pl.BlockSpec((1, 1, TOP_K * td), lambda i: (i, 0, 0), memory_space=pltpu.SMEM),
                  pl.BlockSpec((td, D_MODEL), lambda i: (i, 0)),
                  pl.BlockSpec((td, 128), lambda i: (i, 0)),
                  pl.BlockSpec(memory_space=pl.ANY)],
        out_specs=pl.BlockSpec((td, D_MODEL), lambda i: (i, 0)),
        out_shape=jax.ShapeDtypeStruct((m, D_MODEL), F32),
        scratch_shapes=[pltpu.VMEM((TOP_K, td, D_MODEL), F32), pltpu.SemaphoreType.DMA((TOP_K,))],
        compiler_params=_cparams(("arbitrary",)),
        name="moe_combine",
    )(pos.reshape(m // td, 1, TOP_K * td), x2, ps, y)


def _moe(x2, gain, w_router, w1, w3, w2, tm=512):
    m = x2.shape[0]
    tm = min(tm, m)
    h, ids, ps = _router(x2, gain, w_router)
    pos, tile_e, n_used, nt = _route_positions(ids, tm)
    xs = _dispatch(h, pos, nt * tm)
    y = _experts(xs, tile_e, n_used, w1, w3, w2, tm)
    return _combine(x2, ps, pos, y)


def _relayout_w_in(w):
    k = w.shape[0]
    main = [w[:, _R_BRG:_R_END], w[:, _R_AQKV:_R_BETA], w[:, _R_AGATE:_R_B], w[:, _R_B:_R_C]]
    second = [w[:, _R_C:_R_BRG], w[:, _R_BETA:_R_AGATE],
              jnp.zeros((k, NP_C - OFF_BA - 2 * HEADS), w.dtype)]
    return (jnp.concatenate(main, axis=1).astype(BF16), jnp.concatenate(second, axis=1).astype(BF16))


def _layer_mixers(x2, b, t, layer, lower_bounds, w_in, norm_mix, conv_a, a_log, dt_bias,
                  gnorm_a, gnorm_b, qnorm_c, knorm_c, rel_bias, w_br_a, w_br_b, w_br_c, w_out):
    w_main, w_second = _relayout_w_in(w_in[layer])
    gain = norm_mix[layer].astype(F32)
    proj2 = _norm_proj(x2, gain, w_main, BF16)
    proj3 = proj2.reshape(b, t, NP_MAIN)
    qk_gain = jnp.concatenate([jnp.tile(qnorm_c[layer].astype(F32), (1, HEADS)).reshape(-1),
                               jnp.tile(knorm_c[layer].astype(F32), (1, HEADS)).reshape(-1)])
    projc3 = _norm_proj(x2, gain, w_second, F32, head_gain=qk_gain).reshape(b, t, NP_C)
    oa = _deltanet(proj3, projc3, conv_a[layer], a_log[layer], dt_bias[layer], gnorm_a[layer])
    ob = _hgrn(proj3, lower_bounds[layer], gnorm_b[layer])
    ocs, lses = [], []
    for gi in range(C_GROUPS):
        o, lse = _dilated_group(projc3, gi, rel_bias)
        ocs.append(o)
        lses.append(lse)
    return _mix(x2, proj2, oa.reshape(b * t, HW), ob.reshape(b * t, HW), ocs, lses,
                w_br_a[layer], w_br_b[layer], w_br_c[layer], w_out[layer])


def kernel(x, w_in, norm_mix, conv_a, a_log, dt_bias, gnorm_a, lb_logits, gnorm_b, qnorm_c, knorm_c, rel_bias, w_br_a, w_br_b, w_br_c, w_out, norm_ffn, ffn_w1, ffn_w3, ffn_w2, router, moe_w1, moe_w3, moe_w2):
    b, t, _ = x.shape
    depth = w_in.shape[0]
    p_lb = jax.nn.softmax(lb_logits.astype(F32), axis=0)
    lower_bounds = jnp.cumsum(p_lb, axis=0) - p_lb[0:1]
    x2 = x.reshape(b * t, D_MODEL).astype(F32)
    for layer in range(depth):
        x2 = _layer_mixers(x2, b, t, layer, lower_bounds, w_in, norm_mix, conv_a, a_log, dt_bias,
                           gnorm_a, gnorm_b, qnorm_c, knorm_c, rel_bias,
                           w_br_a, w_br_b, w_br_c, w_out)
        li = layer // 2
        if layer % 2 == 0:
            x2 = _ffn(x2, norm_ffn[layer], ffn_w1[li], ffn_w3[li], ffn_w2[li])
        else:
            x2 = _moe(x2, norm_ffn[layer], router[li], moe_w1[li], moe_w3[li], moe_w2[li])
    return x2.reshape(b, t, D_MODEL).astype(x.dtype)
```

```python
import functools
import math

import numpy as np
import jax
import jax.numpy as jnp
from jax import lax
from jax.experimental import pallas as pl
from jax.experimental.pallas import tpu as pltpu

F32 = jnp.float32
BF16 = jnp.bfloat16
HIGHEST = lax.Precision.HIGHEST

D_MODEL = 1024
EPS = 1e-6
HEADS = 4
DH = 128
HW = HEADS * DH
A_CONV = 4
A_CHUNK = 64
A_ROWS = 512
B_ROWS = 512
B_BLK = 16
B_GROUP = 4
B_FAST = 32
HGRN_SAFE_DECAY = 60.0
C_PAIRS = ((128, 1), (512, 4), (2048, 16))
C_GROUPS = 3
C_BLOCK = 128
C_TILES = 4
C_NLB = 4
REL_BUCKETS = 32
REL_MAX_DIST = 2048
N_EXPERTS = 8
TOP_K = 2

OFF_BRG = 0
OFF_AQKV = 3072
OFF_AGATE = 4608
OFF_B = 5120
NP_MAIN = 7168
OFF_C = 0
OFF_BA = 4608
NP_C = 5120

_R_AQKV, _R_BETA, _R_AGATE, _R_B, _R_C, _R_BRG, _R_END = 0, 1536, 1544, 2056, 4104, 8712, 11784

VMEM_LIMIT = 56 * 1024 * 1024


def _cparams(sem):
    return pltpu.CompilerParams(dimension_semantics=sem, vmem_limit_bytes=VMEM_LIMIT)


def _sigmoid(x):
    return 1.0 / (1.0 + jnp.exp(-x))


def _silu(x):
    return x * _sigmoid(x)


def _softplus(x):
    return jnp.maximum(x, 0.0) + jnp.log(1.0 + jnp.exp(-jnp.abs(x)))


def _dot(a, b):
    return jnp.dot(a, b, preferred_element_type=F32)


def _dot_nt(a, b, precision=None):
    return lax.dot_general(a, b, (((1,), (1,)), ((), ())), precision=precision,
                           preferred_element_type=F32)


def _dot_tn(a, b):
    return lax.dot_general(a, b, (((0,), (0,)), ((), ())), preferred_element_type=F32)


def _norm_proj_kernel(tn, head_norm_cols, x_ref, g_ref, w_ref, hg_ref, o_ref):
    x = x_ref[...]
    ms = jnp.mean(x * x, axis=-1, keepdims=True)
    h = (x * lax.rsqrt(ms + EPS) * g_ref[...]).astype(BF16)
    for c0 in range(0, w_ref.shape[1], tn):
        r = _dot(h, w_ref[:, c0:c0 + tn])
        if c0 < head_norm_cols:
            heads = [r[:, d0:d0 + DH] for d0 in range(0, tn, DH)]
            heads = [a * lax.rsqrt(jnp.mean(a * a, axis=-1, keepdims=True) + EPS) for a in heads]
            r = jnp.concatenate(heads, axis=1) * hg_ref[:, c0:c0 + tn]
        o_ref[:, c0:c0 + tn] = r.astype(o_ref.dtype)


def _norm_proj(x2, gain, w_bf16, out_dtype, head_gain=None, tm=512, tn=1024):
    m = x2.shape[0]
    n = w_bf16.shape[1]
    tm = min(tm, m)
    norm_cols = 0 if head_gain is None else head_gain.shape[0]
    assert norm_cols % tn == 0
    hg = jnp.zeros((1, n), F32)
    if head_gain is not None:
        hg = hg.at[0, :norm_cols].set(head_gain.astype(F32))
    return pl.pallas_call(
        functools.partial(_norm_proj_kernel, tn, norm_cols),
        grid=(m // tm,),
        in_specs=[pl.BlockSpec((tm, D_MODEL), lambda i: (i, 0)),
                  pl.BlockSpec((1, D_MODEL), lambda i: (0, 0)),
                  pl.BlockSpec((D_MODEL, n), lambda i: (0, 0), pipeline_mode=pl.Buffered(1)),
                  pl.BlockSpec((1, n), lambda i: (0, 0))],
        out_specs=pl.BlockSpec((tm, n), lambda i: (i, 0)),
        out_shape=jax.ShapeDtypeStruct((m, n), out_dtype),
        compiler_params=_cparams(("parallel",)),
        name="norm_proj",
    )(x2, gain.reshape(1, D_MODEL), w_bf16, hg)


def _dotb(a, b):
    return _dot(a.astype(BF16), b.astype(BF16))


def _dot_exact_lhs(a_bf16, b):
    b0 = b.astype(BF16)
    r1 = b - b0.astype(F32)
    b1 = r1.astype(BF16)
    b2 = (r1 - b1.astype(F32)).astype(BF16)
    return _dot(a_bf16, b0) + (_dot(a_bf16, b1) + _dot(a_bf16, b2))


def _deltanet_kernel(qkv_ref, gate_ref, ba_ref, convw_ref, arow_ref, dtrow_ref, gn_ref,
                     o_ref, s_ref, xe_ref):
    C = A_CHUNK
    R = A_ROWS
    HS = range(HEADS)
    IT = range((R // C) * HEADS)

    @pl.when(pl.program_id(1) == 0)
    def _():
        s_ref[...] = jnp.zeros_like(s_ref)
        xe_ref[0:8, :] = jnp.zeros((8, 3 * HW), F32)

    x = qkv_ref[0].astype(F32)
    xe_ref[8:8 + R, :] = x
    w = convw_ref[...]
    y = (w[3:4] * x + w[2:3] * xe_ref[7:7 + R, :] + w[1:2] * xe_ref[6:6 + R, :]
         + w[0:1] * xe_ref[5:5 + R, :])
    xe_ref[0:8, :] = x[R - 8:R]
    y = _silu(y)

    ba = ba_ref[0]
    beta_all = _sigmoid(ba)
    g_all = arow_ref[...] * _softplus(ba + dtrow_ref[...])
    gate = _silu(gate_ref[0].astype(F32))
    s_old = [s_ref[h] for h in HS]

    row = lax.broadcasted_iota(jnp.int32, (C, C), 0)
    col = lax.broadcasted_iota(jnp.int32, (C, C), 1)
    lmat = (col <= row).astype(BF16)
    rowx = lax.broadcasted_iota(jnp.int32, (C, DH + C), 0)
    colx = lax.broadcasted_iota(jnp.int32, (C, DH + C), 1)
    umask = (colx < DH) | (rowx > colx - DH)
    eye = (row == col).astype(F32)
    bd8 = (row >> 3) == (col >> 3)

    def merge_mask(sh):
        return (((row >> (sh + 1)) == (col >> (sh + 1)))
                & (((row >> sh) & 1) == 1) & (((col >> sh) & 1) == 0))

    rows = [slice((i // HEADS) * C, (i // HEADS + 1) * C) for i in IT]
    hd = [i % HEADS for i in IT]
    q = [y[rows[i], hd[i] * DH:(hd[i] + 1) * DH] for i in IT]
    k = [y[rows[i], HW + hd[i] * DH:HW + (hd[i] + 1) * DH] for i in IT]
    v = [y[rows[i], 2 * HW + hd[i] * DH:2 * HW + (hd[i] + 1) * DH] for i in IT]
    q = [a * (lax.rsqrt(jnp.sum(a * a, axis=-1, keepdims=True) + EPS) * (DH ** -0.5)) for a in q]
    k = [a * lax.rsqrt(jnp.sum(a * a, axis=-1, keepdims=True) + EPS) for a in k]
    beta = [beta_all[rows[i], hd[i]:hd[i] + 1] for i in IT]
    gb = [jnp.broadcast_to(g_all[rows[i], HEADS + hd[i]:HEADS + hd[i] + 1], (C, DH + C)) for i in IT]
    dext = [_dot_exact_lhs(lmat, jnp.where(umask, gb[i], 0.0)) for i in IT]
    gc = [d[:, :DH] for d in dext]
    edm = [jnp.exp(d[:, DH:]) for d in dext]
    egc = [jnp.exp(g) for g in gc]
    gl = [g[C - 1:C, :] for g in gc]
    kb = [k[i] * beta[i] for i in IT]
    m = [_dot_nt(kb[i].astype(BF16), k[i].astype(BF16)) * jnp.where(row > col, edm[i], 0.0)
         for i in IT]

    nd = [jnp.where(bd8, -a, 0.0) for a in m]
    p2 = [_dotb(a, a) for a in nd]
    p4 = [_dotb(a, a) for a in p2]
    x1 = [eye + nd[i] for i in IT]
    x1 = [x1[i] + _dotb(x1[i], p2[i]) for i in IT]
    xi = [x1[i] + _dotb(x1[i], p4[i]) for i in IT]
    for sh in (3, 4, 5):
        mm = merge_mask(sh)
        t = [_dotb(xi[i], jnp.where(mm, m[i], 0.0)) for i in IT]
        xi = [xi[i] - _dotb(t[i], xi[i]) for i in IT]

    rhs = [jnp.concatenate([v[i] * beta[i], kb[i] * egc[i]], axis=1) for i in IT]
    sol = [_dotb(xi[i], rhs[i]) for i in IT]
    attn = [(_dot_nt(q[i].astype(BF16), k[i].astype(BF16))
             * jnp.where(row >= col, edm[i], 0.0)).astype(BF16) for i in IT]
    qg = [(q[i] * egc[i]).astype(BF16) for i in IT]
    kg = [(k[i] * jnp.exp(gl[i] - gc[i])).astype(BF16) for i in IT]
    egl = [jnp.exp(a) for a in gl]

    s_cur = s_old
    for c in range(R // C):
        it = [c * HEADS + h for h in HS]
        sb = [a.astype(BF16) for a in s_cur]
        v_new = [(sol[i][:, :DH] - _dot(sol[i][:, DH:].astype(BF16), sb[h])).astype(BF16)
                 for h, i in enumerate(it)]
        o = [_dot(qg[i], sb[h]) + _dot(attn[i], v_new[h]) for h, i in enumerate(it)]
        s_cur = [s_cur[h] * egl[i] + _dot_tn(kg[i], v_new[h]) for h, i in enumerate(it)]
        o = [a * lax.rsqrt(jnp.mean(a * a, axis=-1, keepdims=True) + EPS) * gn_ref[...] for a in o]
        o_ref[0, c * C:(c + 1) * C, :] = (jnp.concatenate(o, axis=1)
                                          * gate[c * C:(c + 1) * C]).astype(o_ref.dtype)
    for h in HS:
        s_ref[h] = s_cur[h]


def _deltanet(proj3, projc3, conv_w, a_log, dt_bias, gnorm):
    b, t, _ = proj3.shape
    C = A_ROWS
    pad = jnp.zeros((HEADS,), F32)
    arow = jnp.concatenate([pad, -jnp.exp(a_log.astype(F32)), jnp.zeros((120,), F32)]).reshape(1, 128)
    dtrow = jnp.concatenate([pad, dt_bias.astype(F32), jnp.zeros((120,), F32)]).reshape(1, 128)
    const = lambda shape: pl.BlockSpec(shape, lambda i, c: (0,) * len(shape))
    return pl.pallas_call(
        _deltanet_kernel,
        grid=(b, t // C),
        in_specs=[pl.BlockSpec((1, C, 3 * HW), lambda i, c: (i, c, OFF_AQKV // (3 * HW))),
                  pl.BlockSpec((1, C, HW), lambda i, c: (i, c, OFF_AGATE // HW)),
                  pl.BlockSpec((1, C, 128), lambda i, c: (i, c, OFF_BA // 128)),
                  const((A_CONV, 3 * HW)), const((1, 128)), const((1, 128)), const((1, DH))],
        out_specs=pl.BlockSpec((1, C, HW), lambda i, c: (i, c, 0)),
        out_shape=jax.ShapeDtypeStruct((b, t, HW), BF16),
        scratch_shapes=[pltpu.VMEM((HEADS, DH, DH), F32), pltpu.VMEM((C + 8, 3 * HW), F32)],
        compiler_params=_cparams(("parallel", "arbitrary")),
        name="deltanet",
    )(proj3, proj3, projc3, conv_w.astype(F32), arow, dtrow, gnorm.reshape(1, DH).astype(F32))


def _hgrn_kernel(q_ref, f_ref, i_ref, g_ref, lb_ref, gn_ref, o_ref, st_ref, lg_ref, k_ref):
    R, K = B_ROWS, B_BLK
    HS = range(HEADS)
    NB = R // K

    @pl.when(pl.program_id(1) == 0)
    def _():
        st_ref[...] = jnp.zeros_like(st_ref)

    lb_all = lb_ref[...]
    fl = f_ref[0].astype(F32)
    lg = jnp.log(lb_all + (1.0 - lb_all) * _sigmoid(fl))
    k_ref[...] = (1.0 - lb_all) * _sigmoid(-fl)
    lg_ref[...] = lg
    brow = lax.broadcasted_iota(jnp.int32, (R // B_FAST, R), 0)
    bcol = lax.broadcasted_iota(jnp.int32, (R // B_FAST, R), 1)
    chunksum = _dot((bcol // B_FAST == brow).astype(BF16), lg.astype(BF16))
    max_decay = jnp.max(-chunksum)

    H8 = 8
    row8 = lax.broadcasted_iota(jnp.int32, (H8, DH), 0)
    sls = [slice(h * DH, (h + 1) * DH) for h in HS]

    def pairs_exact(q, k, v, gc):
        n_it = range(len(q))
        otile = [[jnp.zeros((H8, DH), F32) for _ in range(K // H8)] for _ in n_it]
        for j in range(K):
            for t in range(j // H8, K // H8):
                ts = slice(t * H8, (t + 1) * H8)
                for i in n_it:
                    d = gc[i][ts] - gc[i][j:j + 1]
                    rel = jnp.exp(jnp.where(row8 >= j - t * H8, d, -1e30) if t == j // H8 else d)
                    sj = jnp.sum(q[i][ts] * k[i][j:j + 1] * rel, axis=-1, keepdims=True)
                    otile[i][t] = otile[i][t] + sj * v[i][j:j + 1]
        return [jnp.concatenate(otile[i], axis=0) for i in n_it]

    def run(exact):
        KB = K if exact else B_FAST
        GR = B_GROUP * KB
        grow = lax.broadcasted_iota(jnp.int32, (GR, GR), 0)
        gcol = lax.broadcasted_iota(jnp.int32, (GR, GR), 1)
        same_block_lower = (gcol <= grow) & (grow // KB == gcol // KB)
        lmat_g = same_block_lower.astype(BF16)
        st = [st_ref[h] for h in HS]
        for r0 in range(0, R, GR):
            gs = slice(r0, r0 + GR)
            blks = [slice(bi * KB, (bi + 1) * KB) for bi in range(B_GROUP)]
            gc_all = _dot_exact_lhs(lmat_g, lg_ref[gs, :])
            gl_all = jnp.concatenate([jnp.broadcast_to(gc_all[bs.stop - 1:bs.stop], (KB, HW)) for bs in blks],
                                     axis=0)
            q = [_silu(q_ref[0, gs, sl].astype(F32)) for sl in sls]
            k = [k_ref[gs, sl] for sl in sls]
            v = [i_ref[0, gs, sl].astype(F32) for sl in sls]
            gc = [gc_all[:, sl] for sl in sls]
            gl = [gl_all[:, sl] for sl in sls]
            kg = [(k[h] * jnp.exp(gl[h] - gc[h])).astype(BF16) for h in HS]
            qs = [(q[h] * jnp.exp(gc[h])).astype(BF16) for h in HS]
            vb = [a.astype(BF16) for a in v]
            upd = [[_dot_tn(vb[h][bs], kg[h][bs]) for h in HS] for bs in blks]
            egl = [[jnp.exp(gl[h][bs.stop - 1:bs.stop]) for h in HS] for bs in blks]
            if exact:
                items = [(h, bs) for bs in blks for h in HS]
                pe = pairs_exact([q[h][bs] for h, bs in items], [k[h][bs] for h, bs in items],
                                 [v[h][bs] for h, bs in items], [gc[h][bs] for h, bs in items])
                intra = [jnp.concatenate([pe[bi * HEADS + h] for bi in range(B_GROUP)], axis=0) for h in HS]
            else:
                qf = [(q[h] * jnp.exp(gc[h] - gl[h])).astype(BF16) for h in HS]
                s = [jnp.where(same_block_lower, _dot_nt(qf[h], kg[h]), 0.0).astype(BF16) for h in HS]
                intra = [_dot(s[h], vb[h]) for h in HS]
            for bi, bs in enumerate(blks):
                o = [_dot_nt(qs[h][bs], st[h].astype(BF16)) + intra[h][bs] for h in HS]
                st = [st[h] * egl[bi][h] + upd[bi][h] for h in HS]
                o = [a * lax.rsqrt(jnp.mean(a * a, axis=-1, keepdims=True) + EPS) * gn_ref[...] for a in o]
                rs = slice(r0 + bi * KB, r0 + (bi + 1) * KB)
                o_ref[0, rs, :] = (jnp.concatenate(o, axis=1)
                                   * _sigmoid(g_ref[0, rs, :].astype(F32))).astype(o_ref.dtype)
        for h in HS:
            st_ref[h] = st[h]

    lax.cond(max_decay > HGRN_SAFE_DECAY, lambda: run(True), lambda: run(False))


def _hgrn(proj3, lb, gnorm):
    b, t, _ = proj3.shape
    R = B_ROWS
    seg = lambda s: pl.BlockSpec((1, R, HW), lambda i, c: (i, c, OFF_B // HW + s))
    const = lambda shape: pl.BlockSpec(shape, lambda i, c: (0,) * len(shape))
    return pl.pallas_call(
        _hgrn_kernel,
        grid=(b, t // R),
        in_specs=[seg(0), seg(1), seg(2), seg(3), const((1, HW)), const((1, DH))],
        out_specs=pl.BlockSpec((1, R, HW), lambda i, c: (i, c, 0)),
        out_shape=jax.ShapeDtypeStruct((b, t, HW), BF16),
        scratch_shapes=[pltpu.VMEM((HEADS, DH, DH), F32), pltpu.VMEM((R, HW), F32),
                        pltpu.VMEM((R, HW), F32)],
        compiler_params=_cparams(("parallel", "arbitrary")),
        name="hgrn2",
    )(proj3, proj3, proj3, proj3, lb.reshape(1, HW).astype(F32), gnorm.reshape(1, DH).astype(F32))


def _t5_bucket_np(n):
    max_exact = REL_BUCKETS // 2
    nf = np.maximum(n, 1).astype(np.float32)
    large = max_exact + (np.log(nf / max_exact) / math.log(REL_MAX_DIST / max_exact)
                         * (REL_BUCKETS - max_exact)).astype(np.int32)
    large = np.minimum(large, REL_BUCKETS - 1)
    return np.where(n < max_exact, n, large)


def _dilated_kernel(span, dil, hpb, nlb, q_ref, kp_ref, kc_ref, vp_ref, vc_ref, bias_ref,
                    o_ref, lse_ref):
    CB = C_BLOCK
    n = pl.program_id(1)
    qi = lax.broadcasted_iota(jnp.int32, (CB, 2 * CB), 0)
    kj = lax.broadcasted_iota(jnp.int32, (CB, 2 * CB), 1)
    dist = qi + CB - kj
    band = (dist >= 0) & (dist <= span)
    band_first = band & ((kj >= CB) | (n > 0))

    def rows(r, jb):
        return pl.ds(jb * CB * dil + r, CB, stride=dil) if dil > 1 else pl.ds(jb * CB, CB)

    def with_prev(p_ref, c_ref, r, jb, sl):
        prev = p_ref[0, rows(r, 0), sl] if jb == 0 else c_ref[0, rows(r, jb - 1), sl]
        return jnp.concatenate([prev, c_ref[0, rows(r, jb), sl]], axis=0)

    def tiles(items):
        n_it = range(len(items))
        q = [q_ref[0, rows(r, jb), sl].astype(BF16) for r, jb, sl, _ in items]
        k = [with_prev(kp_ref, kc_ref, r, jb, sl).astype(BF16) for r, jb, sl, _ in items]
        v = [with_prev(vp_ref, vc_ref, r, jb, sl).astype(BF16) for r, jb, sl, _ in items]
        s = [_dot_nt(q[i], k[i]) * (DH ** -0.5) + bias_ref[items[i][3]] for i in n_it]
        s = [jnp.where(band_first if items[i][1] == 0 else band, s[i], -1e30) for i in n_it]
        mx = [jnp.max(a, axis=-1, keepdims=True) for a in s]
        p = [jnp.exp(s[i] - mx[i]) for i in n_it]
        den = [jnp.sum(a, axis=-1, keepdims=True) for a in p]
        o = [_dot(p[i].astype(BF16), v[i]) / den[i] for i in n_it]
        for i, (r, jb, sl, _) in enumerate(items):
            o_ref[0, rows(r, jb), sl] = o[i]
            lse_ref[0, rows(r, jb), sl] = jnp.broadcast_to(mx[i] + jnp.log(den[i]), (CB, DH))

    for jb in range(nlb):
        if dil == 1:
            for h0 in range(0, hpb, C_TILES):
                tiles([(0, jb, slice(h * DH, (h + 1) * DH), h) for h in range(h0, h0 + C_TILES)])
        elif dil == C_TILES:
            tiles([(r, jb, slice(0, DH), 0) for r in range(dil)])
        else:
            def body(g, carry, jb=jb):
                tiles([(g * C_TILES + i, jb, slice(0, DH), 0) for i in range(C_TILES)])
                return carry
            lax.fori_loop(0, dil // C_TILES, body, 0)


def _dilated_group(proj3, gi, rel_bias):
    b, t, _ = proj3.shape
    window, dil = C_PAIRS[gi]
    span = window // dil
    CB = C_BLOCK
    pb = CB * dil
    nlb = min(C_NLB, t // pb)
    rb = nlb * pb
    hpb = HEADS if dil == 1 else 1
    bw = hpb * DH
    qi = np.arange(CB)[:, None]
    kj = np.arange(2 * CB)[None, :]
    bucket = _t5_bucket_np(np.maximum(qi + CB - kj, 0) * dil)
    onehot = jnp.asarray(np.eye(REL_BUCKETS, dtype=np.float32)[bucket])
    bias = jnp.einsum("qkb,bh->hqk", onehot, rel_bias[:, gi * HEADS:(gi + 1) * HEADS].astype(F32),
                      precision=HIGHEST)

    def seg(which, prev):
        base = (OFF_C + which * C_GROUPS * HW + gi * HW) // bw
        if prev:
            return pl.BlockSpec((1, pb, bw), lambda i, n, h: (i, jnp.maximum(n * nlb - 1, 0), base + h))
        return pl.BlockSpec((1, rb, bw), lambda i, n, h: (i, n, base + h))

    o, lse = pl.pallas_call(
        functools.partial(_dilated_kernel, span, dil, hpb, nlb),
        grid=(b, t // rb, HEADS // hpb),
        in_specs=[seg(0, False), seg(1, True), seg(1, False), seg(2, True), seg(2, False),
                  pl.BlockSpec((hpb, CB, 2 * CB), lambda i, n, h: (h, 0, 0))],
        out_specs=[pl.BlockSpec((1, rb, bw), lambda i, n, h: (i, n, h)),
                   pl.BlockSpec((1, rb, bw), lambda i, n, h: (i, n, h))],
        out_shape=[jax.ShapeDtypeStruct((b, t, HW), F32),
                   jax.ShapeDtypeStruct((b, t, HW), F32)],
        compiler_params=_cparams(("parallel", "arbitrary", "arbitrary")),
        name=f"dilated_g{gi}",
    )(proj3, proj3, proj3, proj3, proj3, bias)
    return o.reshape(b * t, HW), lse.reshape(b * t, HW)


def _mix_kernel(x_ref, gate_ref, oa_ref, ob_ref, oc0_ref, oc1_ref, oc2_ref,
                l0_ref, l1_ref, l2_ref, wa_ref, wb_ref, wc_ref, wo_ref, out_ref):
    l0, l1, l2 = l0_ref[...], l1_ref[...], l2_ref[...]
    mx = jnp.maximum(jnp.maximum(l0, l1), l2)
    e0, e1, e2 = jnp.exp(l0 - mx), jnp.exp(l1 - mx), jnp.exp(l2 - mx)
    oc = (e0 * oc0_ref[...] + e1 * oc1_ref[...] + e2 * oc2_ref[...]) / (e0 + e1 + e2)
    mix = (_sigmoid(gate_ref[:, 0:D_MODEL].astype(F32)) * _dot(oa_ref[...], wa_ref[...])
           + _sigmoid(gate_ref[:, D_MODEL:2 * D_MODEL].astype(F32)) * _dot(ob_ref[...], wb_ref[...])
           + _sigmoid(gate_ref[:, 2 * D_MODEL:3 * D_MODEL].astype(F32)) * _dot(oc.astype(BF16), wc_ref[...]))
    out_ref[...] = x_ref[...] + _dot(mix.astype(BF16), wo_ref[...])


def _mix(x2, proj2, oa, ob, ocs, lses, wa, wb, wc, wo, tm=256):
    m = x2.shape[0]
    tm = min(tm, m)
    rowblk = lambda w: pl.BlockSpec((tm, w), lambda i: (i, 0))
    const = lambda shape: pl.BlockSpec(shape, lambda i: (0,) * len(shape))
    return pl.pallas_call(
        _mix_kernel,
        grid=(m // tm,),
        in_specs=[rowblk(D_MODEL), rowblk(3 * D_MODEL), rowblk(HW), rowblk(HW),
                  rowblk(HW), rowblk(HW), rowblk(HW), rowblk(HW), rowblk(HW), rowblk(HW),
                  const((HW, D_MODEL)), const((HW, D_MODEL)), const((HW, D_MODEL)),
                  const((D_MODEL, D_MODEL))],
        out_specs=rowblk(D_MODEL),
        out_shape=jax.ShapeDtypeStruct((m, D_MODEL), F32),
        compiler_params=_cparams(("parallel",)),
        name="branch_mix",
    )(x2, proj2, oa, ob, ocs[0], ocs[1], ocs[2], lses[0], lses[1], lses[2],
      wa.astype(BF16), wb.astype(BF16), wc.astype(BF16), wo.astype(BF16))


def _ffn_kernel(x_ref, g_ref, w1_ref, w3_ref, w2_ref, o_ref):
    x = x_ref[...]
    ms = jnp.mean(x * x, axis=-1, keepdims=True)
    h = (x * lax.rsqrt(ms + EPS) * g_ref[...]).astype(BF16)
    a = _dot(h, w1_ref[...])
    b = _dot(h, w3_ref[...])
    o_ref[...] = x + _dot((_silu(a) * b).astype(BF16), w2_ref[...])


def _ffn(x2, gain, w1, w3, w2, tm=512):
    m = x2.shape[0]
    ff = w1.shape[1]
    tm = min(tm, m)
    resident = lambda shape: pl.BlockSpec(shape, lambda i: (0, 0), pipeline_mode=pl.Buffered(1))
    return pl.pallas_call(
        _ffn_kernel,
        grid=(m // tm,),
        in_specs=[pl.BlockSpec((tm, D_MODEL), lambda i: (i, 0)),
                  pl.BlockSpec((1, D_MODEL), lambda i: (0, 0)),
                  resident((D_MODEL, ff)), resident((D_MODEL, ff)), resident((ff, D_MODEL))],
        out_specs=pl.BlockSpec((tm, D_MODEL), lambda i: (i, 0)),
        out_shape=jax.ShapeDtypeStruct((m, D_MODEL), F32),
        compiler_params=_cparams(("parallel",)),
        name="ffn",
    )(x2, gain.reshape(1, D_MODEL).astype(F32), w1.astype(BF16), w3.astype(BF16), w2.astype(BF16))


def _router_kernel(x_ref, g_ref, wr_ref, h_ref, ids_ref, ps_ref):
    x = x_ref[...]
    ms = jnp.mean(x * x, axis=-1, keepdims=True)
    h = x * lax.rsqrt(ms + EPS) * g_ref[...]
    h_ref[...] = h
    logits = jnp.dot(h, wr_ref[...], precision=HIGHEST, preferred_element_type=F32)
    lane = lax.broadcasted_iota(jnp.int32, logits.shape, 1)
    neg = jnp.float32(-jnp.inf)
    l1 = jnp.where(lane < N_EXPERTS, logits, neg)
    m1 = jnp.max(l1, axis=-1, keepdims=True)
    i1 = jnp.min(jnp.where(l1 == m1, lane, 128), axis=-1, keepdims=True)
    l2 = jnp.where(lane == i1, neg, l1)
    m2 = jnp.max(l2, axis=-1, keepdims=True)
    i2 = jnp.min(jnp.where(l2 == m2, lane, 128), axis=-1, keepdims=True)
    e = jnp.exp(m2 - m1)
    p1 = 1.0 / (1.0 + e)
    p2 = e / (1.0 + e)
    ids_ref[...] = jnp.where(lane == 0, i1, jnp.where(lane == 1, i2, 0))
    ps_ref[...] = jnp.where(lane == 0, p1, jnp.where(lane == 1, p2, 0.0))


def _router(x2, gain, w_router, tm=512):
    m = x2.shape[0]
    tm = min(tm, m)
    wr = jnp.pad(w_router.astype(F32), ((0, 0), (0, 128 - N_EXPERTS)))
    return pl.pallas_call(
        _router_kernel,
        grid=(m // tm,),
        in_specs=[pl.BlockSpec((tm, D_MODEL), lambda i: (i, 0)),
                  pl.BlockSpec((1, D_MODEL), lambda i: (0, 0)),
                  pl.BlockSpec((D_MODEL, 128), lambda i: (0, 0))],
        out_specs=[pl.BlockSpec((tm, D_MODEL), lambda i: (i, 0)),
                   pl.BlockSpec((tm, 128), lambda i: (i, 0)),
                   pl.BlockSpec((tm, 128), lambda i: (i, 0))],
        out_shape=[jax.ShapeDtypeStruct((m, D_MODEL), F32),
                   jax.ShapeDtypeStruct((m, 128), jnp.int32),
                   jax.ShapeDtypeStruct((m, 128), F32)],
        compiler_params=_cparams(("parallel",)),
        name="router",
    )(x2, gain.reshape(1, D_MODEL).astype(F32), wr)


def _route_positions(ids, tm):
    m = ids.shape[0]
    e_flat = ids[:, :TOP_K].reshape(-1)
    onehot = (e_flat[:, None] == jnp.arange(N_EXPERTS)[None, :]).astype(jnp.int32)
    csum = jnp.cumsum(onehot, axis=0)
    counts = csum[-1]
    gsz = ((counts + tm - 1) // tm) * tm
    gend = jnp.cumsum(gsz)
    pos = jnp.sum(onehot * (gend - gsz + csum - 1), axis=1)
    nt = (TOP_K * m + N_EXPERTS * tm) // tm
    n_used = gend[-1] // tm
    tile_e = jnp.sum((jnp.arange(nt)[:, None] * tm >= gend[None, :]).astype(jnp.int32), axis=1)
    last_e = jnp.sum(((n_used - 1) * tm >= gend).astype(jnp.int32))
    tile_e = jnp.minimum(tile_e, last_e)
    return pos.astype(jnp.int32), tile_e.astype(jnp.int32), n_used.reshape(1).astype(jnp.int32), nt


def _dispatch_kernel(pos_ref, h_ref, init_ref, xs_ref, sem):
    del init_ref
    td = h_ref.shape[0]

    def row_copy(t, s):
        return pltpu.make_async_copy(h_ref.at[pl.ds(t, 1)],
                                     xs_ref.at[pl.ds(pos_ref[0, 0, TOP_K * t + s], 1)], sem)

    def start(t, c):
        for s in range(TOP_K):
            row_copy(t, s).start()
        return c

    lax.fori_loop(0, td, start, 0, unroll=8)
    for s in range(TOP_K):
        pltpu.make_async_copy(h_ref, xs_ref.at[pl.ds(0, td)], sem).wait()


def _dispatch(h, pos, npad, td=256):
    m = h.shape[0]
    td = min(td, m)
    return pl.pallas_call(
        _dispatch_kernel,
        grid=(m // td,),
        in_specs=[pl.BlockSpec((1, 1, TOP_K * td), lambda i: (i, 0, 0), memory_space=pltpu.SMEM),
                  pl.BlockSpec((td, D_MODEL), lambda i: (i, 0)),
                  pl.BlockSpec(memory_space=pl.ANY)],
        out_specs=pl.BlockSpec(memory_space=pl.ANY),
        out_shape=jax.ShapeDtypeStruct((npad, D_MODEL), F32),
        scratch_shapes=[pltpu.SemaphoreType.DMA(())],
        input_output_aliases={2: 0},
        compiler_params=_cparams(("arbitrary",)),
        name="moe_dispatch",
    )(pos.reshape(m // td, 1, TOP_K * td), h, jnp.zeros((npad, D_MODEL), F32))


def _experts_kernel(te_ref, nu_ref, xs_ref, w1_ref, w3_ref, w2_ref, y_ref, xb_ref, acc_ref):
    del te_ref
    i = pl.program_id(0)
    j = pl.program_id(1)

    @pl.when(i < nu_ref[0])
    def _():
        @pl.when(j == 0)
        def _():
            xb_ref[...] = xs_ref[...].astype(BF16)
            acc_ref[...] = jnp.zeros_like(acc_ref)

        xb = xb_ref[...]
        a = _dot(xb, w1_ref[0])
        b = _dot(xb, w3_ref[0])
        acc_ref[...] += _dot((_silu(a) * b).astype(BF16), w2_ref[0])

        @pl.when(j == pl.num_programs(1) - 1)
        def _():
            y_ref[...] = acc_ref[...]

    @pl.when((i >= nu_ref[0]) & (j == 0))
    def _():
        y_ref[...] = jnp.zeros_like(y_ref)


def _experts(xs, tile_e, n_used, w1, w3, w2, tm, tf=1792):
    npad = xs.shape[0]
    ff = w1.shape[2]
    nj = ff // tf
    row = lambda i, j, te, nu: (jnp.minimum(i, nu[0] - 1), 0)
    jj = lambda i, j, nu: jnp.where(i < nu[0], j, nj - 1)
    return pl.pallas_call(
        _experts_kernel,
        grid_spec=pltpu.PrefetchScalarGridSpec(
            num_scalar_prefetch=2,
            grid=(npad // tm, nj),
            in_specs=[pl.BlockSpec((tm, D_MODEL), row),
                      pl.BlockSpec((1, D_MODEL, tf), lambda i, j, te, nu: (te[i], 0, jj(i, j, nu))),
                      pl.BlockSpec((1, D_MODEL, tf), lambda i, j, te, nu: (te[i], 0, jj(i, j, nu))),
                      pl.BlockSpec((1, tf, D_MODEL), lambda i, j, te, nu: (te[i], jj(i, j, nu), 0))],
            out_specs=pl.BlockSpec((tm, D_MODEL), lambda i, j, te, nu: (i, 0)),
            scratch_shapes=[pltpu.VMEM((tm, D_MODEL), BF16), pltpu.VMEM((tm, D_MODEL), F32)]),
        out_shape=jax.ShapeDtypeStruct((npad, D_MODEL), F32),
        compiler_params=_cparams(("arbitrary", "arbitrary")),
        name="moe_experts",
    )(tile_e, n_used, xs, w1.astype(BF16), w3.astype(BF16), w2.astype(BF16))


def _combine_kernel(pos_ref, x_ref, ps_ref, y_ref, o_ref, buf_ref, sem):
    td = x_ref.shape[0]

    def row_copy(t, s):
        return pltpu.make_async_copy(y_ref.at[pl.ds(pos_ref[0, 0, TOP_K * t + s], 1)],
                                     buf_ref.at[s, pl.ds(t, 1)], sem.at[s])

    def start(t, c):
        for s in range(TOP_K):
            row_copy(t, s).start()
        return c

    lax.fori_loop(0, td, start, 0, unroll=8)
    for s in range(TOP_K):
        pltpu.make_async_copy(y_ref.at[pl.ds(0, td)], buf_ref.at[s], sem.at[s]).wait()
    ps = ps_ref[...]
    o_ref[...] = x_ref[...] + ps[:, 0:1] * buf_ref[0] + ps[:, 1:2] * buf_ref[1]


def _combine(x2, ps, pos, y, td=256):
    m = x2.shape[0]
    td = min(td, m)
    return pl.pallas_call(
        _combine_kernel,
        grid=(m // td,),
        in_specs=[pl.BlockSpec((1, 1, TOP_K * td), lambda i: (i, 0, 0), memory_space=pltpu.SMEM),
                  pl.BlockSpec((td, D_MODEL), lambda i: (i, 0)),
                  pl.BlockSpec((td, 128), lambda i: (i, 0)),
                  pl.BlockSpec(memory_space=pl.ANY)],
        out_specs=pl.BlockSpec((td, D_MODEL), lambda i: (i, 0)),
        out_shape=jax.ShapeDtypeStruct((m, D_MODEL), F32),
        scratch_shapes=[pltpu.VMEM((TOP_K, td, D_MODEL), F32), pltpu.SemaphoreType.DMA((TOP_K,))],
        compiler_params=_cparams(("arbitrary",)),
        name="moe_combine",
    )(pos.reshape(m // td, 1, TOP_K * td), x2, ps, y)


def _moe(x2, gain, w_router, w1, w3, w2, tm=512):
    m = x2.shape[0]
    tm = min(tm, m)
    h, ids, ps = _router(x2, gain, w_router)
    pos, tile_e, n_used, nt = _route_positions(ids, tm)
    xs = _dispatch(h, pos, nt * tm)
    y = _experts(xs, tile_e, n_used, w1, w3, w2, tm)
    return _combine(x2, ps, pos, y)


def _relayout_w_in(w):
    k = w.shape[0]
    main = [w[:, _R_BRG:_R_END], w[:, _R_AQKV:_R_BETA], w[:, _R_AGATE:_R_B], w[:, _R_B:_R_C]]
    second = [w[:, _R_C:_R_BRG], w[:, _R_BETA:_R_AGATE],
              jnp.zeros((k, NP_C - OFF_BA - 2 * HEADS), w.dtype)]
    return (jnp.concatenate(main, axis=1).astype(BF16), jnp.concatenate(second, axis=1).astype(BF16))


def _layer_mixers(x2, b, t, layer, lower_bounds, w_in, norm_mix, conv_a, a_log, dt_bias,
                  gnorm_a, gnorm_b, qnorm_c, knorm_c, rel_bias, w_br_a, w_br_b, w_br_c, w_out):
    w_main, w_second = _relayout_w_in(w_in[layer])
    gain = norm_mix[layer].astype(F32)
    proj2 = _norm_proj(x2, gain, w_main, BF16)
    proj3 = proj2.reshape(b, t, NP_MAIN)
    qk_gain = jnp.concatenate([jnp.tile(qnorm_c[layer].astype(F32), (1, HEADS)).reshape(-1),
                               jnp.tile(knorm_c[layer].astype(F32), (1, HEADS)).reshape(-1)])
    projc3 = _norm_proj(x2, gain, w_second, F32, head_gain=qk_gain).reshape(b, t, NP_C)
    oa = _deltanet(proj3, projc3, conv_a[layer], a_log[layer], dt_bias[layer], gnorm_a[layer])
    ob = _hgrn(proj3, lower_bounds[layer], gnorm_b[layer])
    ocs, lses = [], []
    for gi in range(C_GROUPS):
        o, lse = _dilated_group(projc3, gi, rel_bias)
        ocs.append(o)
        lses.append(lse)
    return _mix(x2, proj2, oa.reshape(b * t, HW), ob.reshape(b * t, HW), ocs, lses,
                w_br_a[layer], w_br_b[layer], w_br_c[layer], w_out[layer])


def kernel(x, w_in, norm_mix, conv_a, a_log, dt_bias, gnorm_a, lb_logits, gnorm_b, qnorm_c, knorm_c, rel_bias, w_br_a, w_br_b, w_br_c, w_out, norm_ffn, ffn_w1, ffn_w3, ffn_w2, router, moe_w1, moe_w3, moe_w2):
    b, t, _ = x.shape
    depth = w_in.shape[0]
    p_lb = jax.nn.softmax(lb_logits.astype(F32), axis=0)
    lower_bounds = jnp.cumsum(p_lb, axis=0) - p_lb[0:1]
    x2 = x.reshape(b * t, D_MODEL).astype(F32)
    for layer in range(depth):
        x2 = _layer_mixers(x2, b, t, layer, lower_bounds, w_in, norm_mix, conv_a, a_log, dt_bias,
                           gnorm_a, gnorm_b, qnorm_c, knorm_c, rel_bias,
                           w_br_a, w_br_b, w_br_c, w_out)
        li = layer // 2
        if layer % 2 == 0:
            x2 = _ffn(x2, norm_ffn[layer], ffn_w1[li], ffn_w3[li], ffn_w2[li])
        else:
            x2 = _moe(x2, norm_ffn[layer], router[li], moe_w1[li], moe_w3[li], moe_w2[li])
    return x2.reshape(b, t, D_MODEL).astype(x.dtype)
```

```python
import functools
import math

import numpy as np
import jax
import jax.numpy as jnp
from jax import lax
from jax.experimental import pallas as pl
from jax.experimental.pallas import tpu as pltpu

F32 = jnp.float32
BF16 = jnp.bfloat16
HIGHEST = lax.Precision.HIGHEST

D_MODEL = 1024
EPS = 1e-6
HEADS = 4
DH = 128
HW = HEADS * DH
A_CONV = 4
A_CHUNK = 64
A_ROWS = 512
B_ROWS = 512
B_BLK = 16
B_GROUP = 4
B_FAST = 32
HGRN_SAFE_DECAY = 60.0
C_PAIRS = ((128, 1), (512, 4), (2048, 16))
C_GROUPS = 3
C_BLOCK = 128
C_TILES = 4
C_NLB = 4
REL_BUCKETS = 32
REL_MAX_DIST = 2048
N_EXPERTS = 8
TOP_K = 2

OFF_BRG = 0
OFF_AQKV = 3072
OFF_AGATE = 4608
OFF_B = 5120
NP_MAIN = 7168
OFF_C = 0
OFF_BA = 4608
NP_C = 5120

_R_AQKV, _R_BETA, _R_AGATE, _R_B, _R_C, _R_BRG, _R_END = 0, 1536, 1544, 2056, 4104, 8712, 11784

VMEM_LIMIT = 56 * 1024 * 1024


def _cparams(sem):
    return pltpu.CompilerParams(dimension_semantics=sem, vmem_limit_bytes=VMEM_LIMIT)


def _sigmoid(x):
    return 1.0 / (1.0 + jnp.exp(-x))


def _silu(x):
    return x * _sigmoid(x)


def _softplus(x):
    return jnp.maximum(x, 0.0) + jnp.log(1.0 + jnp.exp(-jnp.abs(x)))


def _dot(a, b):
    return jnp.dot(a, b, preferred_element_type=F32)


def _dot_nt(a, b, precision=None):
    return lax.dot_general(a, b, (((1,), (1,)), ((), ())), precision=precision,
                           preferred_element_type=F32)


def _dot_tn(a, b):
    return lax.dot_general(a, b, (((0,), (0,)), ((), ())), preferred_element_type=F32)


def _norm_proj_kernel(tn, head_norm_cols, x_ref, g_ref, w_ref, hg_ref, o_ref):
    x = x_ref[...]
    ms = jnp.mean(x * x, axis=-1, keepdims=True)
    h = (x * lax.rsqrt(ms + EPS) * g_ref[...]).astype(BF16)
    for c0 in range(0, w_ref.shape[1], tn):
        r = _dot(h, w_ref[:, c0:c0 + tn])
        if c0 < head_norm_cols:
            heads = [r[:, d0:d0 + DH] for d0 in range(0, tn, DH)]
            heads = [a * lax.rsqrt(jnp.mean(a * a, axis=-1, keepdims=True) + EPS) for a in heads]
            r = jnp.concatenate(heads, axis=1) * hg_ref[:, c0:c0 + tn]
        o_ref[:, c0:c0 + tn] = r.astype(o_ref.dtype)


def _norm_proj(x2, gain, w_bf16, out_dtype, head_gain=None, tm=512, tn=1024):
    m = x2.shape[0]
    n = w_bf16.shape[1]
    tm = min(tm, m)
    norm_cols = 0 if head_gain is None else head_gain.shape[0]
    assert norm_cols % tn == 0
    hg = jnp.zeros((1, n), F32)
    if head_gain is not None:
        hg = hg.at[0, :norm_cols].set(head_gain.astype(F32))
    return pl.pallas_call(
        functools.partial(_norm_proj_kernel, tn, norm_cols),
        grid=(m // tm,),
        in_specs=[pl.BlockSpec((tm, D_MODEL), lambda i: (i, 0)),
                  pl.BlockSpec((1, D_MODEL), lambda i: (0, 0)),
                  pl.BlockSpec((D_MODEL, n), lambda i: (0, 0), pipeline_mode=pl.Buffered(1)),
                  pl.BlockSpec((1, n), lambda i: (0, 0))],
        out_specs=pl.BlockSpec((tm, n), lambda i: (i, 0)),
        out_shape=jax.ShapeDtypeStruct((m, n), out_dtype),
        compiler_params=_cparams(("parallel",)),
        name="norm_proj",
    )(x2, gain.reshape(1, D_MODEL), w_bf16, hg)


def _dotb(a, b):
    return _dot(a.astype(BF16), b.astype(BF16))


def _dot_exact_lhs(a_bf16, b):
    b0 = b.astype(BF16)
    r1 = b - b0.astype(F32)
    b1 = r1.astype(BF16)
    b2 = (r1 - b1.astype(F32)).astype(BF16)
    return _dot(a_bf16, b0) + (_dot(a_bf16, b1) + _dot(a_bf16, b2))


def _deltanet_kernel(qkv_ref, gate_ref, ba_ref, convw_ref, arow_ref, dtrow_ref, gn_ref,
                     o_ref, s_ref, xe_ref):
    C = A_CHUNK
    R = A_ROWS
    HS = range(HEADS)
    IT = range((R // C) * HEADS)

    @pl.when(pl.program_id(1) == 0)
    def _():
        s_ref[...] = jnp.zeros_like(s_ref)
        xe_ref[0:8, :] = jnp.zeros((8, 3 * HW), F32)

    x = qkv_ref[0].astype(F32)
    xe_ref[8:8 + R, :] = x
    w = convw_ref[...]
    y = (w[3:4] * x + w[2:3] * xe_ref[7:7 + R, :] + w[1:2] * xe_ref[6:6 + R, :]
         + w[0:1] * xe_ref[5:5 + R, :])
    xe_ref[0:8, :] = x[R - 8:R]
    y = _silu(y)

    ba = ba_ref[0]
    beta_all = _sigmoid(ba)
    g_all = arow_ref[...] * _softplus(ba + dtrow_ref[...])
    gate = _silu(gate_ref[0].astype(F32))
    s_old = [s_ref[h] for h in HS]

    row = lax.broadcasted_iota(jnp.int32, (C, C), 0)
    col = lax.broadcasted_iota(jnp.int32, (C, C), 1)
    lmat = (col <= row).astype(BF16)
    rowx = lax.broadcasted_iota(jnp.int32, (C, DH + C), 0)
    colx = lax.broadcasted_iota(jnp.int32, (C, DH + C), 1)
    umask = (colx < DH) | (rowx > colx - DH)
    eye = (row == col).astype(F32)
    bd8 = (row >> 3) == (col >> 3)

    def merge_mask(sh):
        return (((row >> (sh + 1)) == (col >> (sh + 1)))
                & (((row >> sh) & 1) == 1) & (((col >> sh) & 1) == 0))

    rows = [slice((i // HEADS) * C, (i // HEADS + 1) * C) for i in IT]
    hd = [i % HEADS for i in IT]
    q = [y[rows[i], hd[i] * DH:(hd[i] + 1) * DH] for i in IT]
    k = [y[rows[i], HW + hd[i] * DH:HW + (hd[i] + 1) * DH] for i in IT]
    v = [y[rows[i], 2 * HW + hd[i] * DH:2 * HW + (hd[i] + 1) * DH] for i in IT]
    q = [a * (lax.rsqrt(jnp.sum(a * a, axis=-1, keepdims=True) + EPS) * (DH ** -0.5)) for a in q]
    k = [a * lax.rsqrt(jnp.sum(a * a, axis=-1, keepdims=True) + EPS) for a in k]
    beta = [beta_all[rows[i], hd[i]:hd[i] + 1] for i in IT]
    gb = [jnp.broadcast_to(g_all[rows[i], HEADS + hd[i]:HEADS + hd[i] + 1], (C, DH + C)) for i in IT]
    dext = [_dot_exact_lhs(lmat, jnp.where(umask, gb[i], 0.0)) for i in IT]
    gc = [d[:, :DH] for d in dext]
    edm = [jnp.exp(d[:, DH:]) for d in dext]
    egc = [jnp.exp(g) for g in gc]
    gl = [g[C - 1:C, :] for g in gc]
    kb = [k[i] * beta[i] for i in IT]
    m = [_dot_nt(kb[i].astype(BF16), k[i].astype(BF16)) * jnp.where(row > col, edm[i], 0.0)
         for i in IT]

    nd = [jnp.where(bd8, -a, 0.0) for a in m]
    p2 = [_dotb(a, a) for a in nd]
    p4 = [_dotb(a, a) for a in p2]
    x1 = [eye + nd[i] for i in IT]
    x1 = [x1[i] + _dotb(x1[i], p2[i]) for i in IT]
    xi = [x1[i] + _dotb(x1[i], p4[i]) for i in IT]
    for sh in (3, 4, 5):
        mm = merge_mask(sh)
        t = [_dotb(xi[i], jnp.where(mm, m[i], 0.0)) for i in IT]
        xi = [xi[i] - _dotb(t[i], xi[i]) for i in IT]

    rhs = [jnp.concatenate([v[i] * beta[i], kb[i] * egc[i]], axis=1) for i in IT]
    sol = [_dotb(xi[i], rhs[i]) for i in IT]
    attn = [(_dot_nt(q[i].astype(BF16), k[i].astype(BF16))
             * jnp.where(row >= col, edm[i], 0.0)).astype(BF16) for i in IT]
    qg = [(q[i] * egc[i]).astype(BF16) for i in IT]
    kg = [(k[i] * jnp.exp(gl[i] - gc[i])).astype(BF16) for i in IT]
    egl = [jnp.exp(a) for a in gl]

    s_cur = s_old
    for c in range(R // C):
        it = [c * HEADS + h for h in HS]
        sb = [a.astype(BF16) for a in s_cur]
        v_new = [(sol[i][:, :DH] - _dot(sol[i][:, DH:].astype(BF16), sb[h])).astype(BF16)
                 for h, i in enumerate(it)]
        o = [_dot(qg[i], sb[h]) + _dot(attn[i], v_new[h]) for h, i in enumerate(it)]
        s_cur = [s_cur[h] * egl[i] + _dot_tn(kg[i], v_new[h]) for h, i in enumerate(it)]
        o = [a * lax.rsqrt(jnp.mean(a * a, axis=-1, keepdims=True) + EPS) * gn_ref[...] for a in o]
        o_ref[0, c * C:(c + 1) * C, :] = (jnp.concatenate(o, axis=1)
                                          * gate[c * C:(c + 1) * C]).astype(o_ref.dtype)
    for h in HS:
        s_ref[h] = s_cur[h]


def _deltanet(proj3, projc3, conv_w, a_log, dt_bias, gnorm):
    b, t, _ = proj3.shape
    C = A_ROWS
    pad = jnp.zeros((HEADS,), F32)
    arow = jnp.concatenate([pad, -jnp.exp(a_log.astype(F32)), jnp.zeros((120,), F32)]).reshape(1, 128)
    dtrow = jnp.concatenate([pad, dt_bias.astype(F32), jnp.zeros((120,), F32)]).reshape(1, 128)
    const = lambda shape: pl.BlockSpec(shape, lambda i, c: (0,) * len(shape))
    return pl.pallas_call(
        _deltanet_kernel,
        grid=(b, t // C),
        in_specs=[pl.BlockSpec((1, C, 3 * HW), lambda i, c: (i, c, OFF_AQKV // (3 * HW))),
                  pl.BlockSpec((1, C, HW), lambda i, c: (i, c, OFF_AGATE // HW)),
                  pl.BlockSpec((1, C, 128), lambda i, c: (i, c, OFF_BA // 128)),
                  const((A_CONV, 3 * HW)), const((1, 128)), const((1, 128)), const((1, DH))],
        out_specs=pl.BlockSpec((1, C, HW), lambda i, c: (i, c, 0)),
        out_shape=jax.ShapeDtypeStruct((b, t, HW), BF16),
        scratch_shapes=[pltpu.VMEM((HEADS, DH, DH), F32), pltpu.VMEM((C + 8, 3 * HW), F32)],
        compiler_params=_cparams(("parallel", "arbitrary")),
        name="deltanet",
    )(proj3, proj3, projc3, conv_w.astype(F32), arow, dtrow, gnorm.reshape(1, DH).astype(F32))


def _hgrn_kernel(q_ref, f_ref, i_ref, g_ref, lb_ref, gn_ref, o_ref, st_ref, lg_ref, k_ref):
    R, K = B_ROWS, B_BLK
    HS = range(HEADS)
    NB = R // K

    @pl.when(pl.program_id(1) == 0)
    def _():
        st_ref[...] = jnp.zeros_like(st_ref)

    lb_all = lb_ref[...]
    fl = f_ref[0].astype(F32)
    lg = jnp.log(lb_all + (1.0 - lb_all) * _sigmoid(fl))
    k_ref[...] = (1.0 - lb_all) * _sigmoid(-fl)
    lg_ref[...] = lg
    brow = lax.broadcasted_iota(jnp.int32, (R // B_FAST, R), 0)
    bcol = lax.broadcasted_iota(jnp.int32, (R // B_FAST, R), 1)
    chunksum = _dot((bcol // B_FAST == brow).astype(BF16), lg.astype(BF16))
    max_decay = jnp.max(-chunksum)

    H8 = 8
    row8 = lax.broadcasted_iota(jnp.int32, (H8, DH), 0)
    sls = [slice(h * DH, (h + 1) * DH) for h in HS]

    def pairs_exact(q, k, v, gc):
        n_it = range(len(q))
        otile = [[jnp.zeros((H8, DH), F32) for _ in range(K // H8)] for _ in n_it]
        for j in range(K):
            for t in range(j // H8, K // H8):
                ts = slice(t * H8, (t + 1) * H8)
                for i in n_it:
                    d = gc[i][ts] - gc[i][j:j + 1]
                    rel = jnp.exp(jnp.where(row8 >= j - t * H8, d, -1e30) if t == j // H8 else d)
                    sj = jnp.sum(q[i][ts] * k[i][j:j + 1] * rel, axis=-1, keepdims=True)
                    otile[i][t] = otile[i][t] + sj * v[i][j:j + 1]
        return [jnp.concatenate(otile[i], axis=0) for i in n_it]

    def run(exact):
        KB = K if exact else B_FAST
        GR = B_GROUP * KB
        grow = lax.broadcasted_iota(jnp.int32, (GR, GR), 0)
        gcol = lax.broadcasted_iota(jnp.int32, (GR, GR), 1)
        same_block_lower = (gcol <= grow) & (grow // KB == gcol // KB)
        lmat_g = same_block_lower.astype(BF16)
        st = [st_ref[h] for h in HS]
        for r0 in range(0, R, GR):
            gs = slice(r0, r0 + GR)
            blks = [slice(bi * KB, (bi + 1) * KB) for bi in range(B_GROUP)]
            gc_all = _dot_exact_lhs(lmat_g, lg_ref[gs, :])
            gl_all = jnp.concatenate([jnp.broadcast_to(gc_all[bs.stop - 1:bs.stop], (KB, HW)) for bs in blks],
                                     axis=0)
            q = [_silu(q_ref[0, gs, sl].astype(F32)) for sl in sls]
            k = [k_ref[gs, sl] for sl in sls]
            v = [i_ref[0, gs, sl].astype(F32) for sl in sls]
            gc = [gc_all[:, sl] for sl in sls]
            gl = [gl_all[:, sl] for sl in sls]
            kg = [(k[h] * jnp.exp(gl[h] - gc[h])).astype(BF16) for h in HS]
            qs = [(q[h] * jnp.exp(gc[h])).astype(BF16) for h in HS]
            vb = [a.astype(BF16) for a in v]
            upd = [[_dot_tn(vb[h][bs], kg[h][bs]) for h in HS] for bs in blks]
            egl = [[jnp.exp(gl[h][bs.stop - 1:bs.stop]) for h in HS] for bs in blks]
            if exact:
                items = [(h, bs) for bs in blks for h in HS]
                pe = pairs_exact([q[h][bs] for h, bs in items], [k[h][bs] for h, bs in items],
                                 [v[h][bs] for h, bs in items], [gc[h][bs] for h, bs in items])
                intra = [jnp.concatenate([pe[bi * HEADS + h] for bi in range(B_GROUP)], axis=0) for h in HS]
            else:
                qf = [(q[h] * jnp.exp(gc[h] - gl[h])).astype(BF16) for h in HS]
                s = [jnp.where(same_block_lower, _dot_nt(qf[h], kg[h]), 0.0).astype(BF16) for h in HS]
                intra = [_dot(s[h], vb[h]) for h in HS]
            for bi, bs in enumerate(blks):
                o = [_dot_nt(qs[h][bs], st[h].astype(BF16)) + intra[h][bs] for h in HS]
                st = [st[h] * egl[bi][h] + upd[bi][h] for h in HS]
                o = [a * lax.rsqrt(jnp.mean(a * a, axis=-1, keepdims=True) + EPS) * gn_ref[...] for a in o]
                rs = slice(r0 + bi * KB, r0 + (bi + 1) * KB)
                o_ref[0, rs, :] = (jnp.concatenate(o, axis=1)
                                   * _sigmoid(g_ref[0, rs, :].astype(F32))).astype(o_ref.dtype)
        for h in HS:
            st_ref[h] = st[h]

    lax.cond(max_decay > HGRN_SAFE_DECAY, lambda: run(True), lambda: run(False))


def _hgrn(proj3, lb, gnorm):
    b, t, _ = proj3.shape
    R = B_ROWS
    seg = lambda s: pl.BlockSpec((1, R, HW), lambda i, c: (i, c, OFF_B // HW + s))
    const = lambda shape: pl.BlockSpec(shape, lambda i, c: (0,) * len(shape))
    return pl.pallas_call(
        _hgrn_kernel,
        grid=(b, t // R),
        in_specs=[seg(0), seg(1), seg(2), seg(3), const((1, HW)), const((1, DH))],
        out_specs=pl.BlockSpec((1, R, HW), lambda i, c: (i, c, 0)),
        out_shape=jax.ShapeDtypeStruct((b, t, HW), BF16),
        scratch_shapes=[pltpu.VMEM((HEADS, DH, DH), F32), pltpu.VMEM((R, HW), F32),
                        pltpu.VMEM((R, HW), F32)],
        compiler_params=_cparams(("parallel", "arbitrary")),
        name="hgrn2",
    )(proj3, proj3, proj3, proj3, lb.reshape(1, HW).astype(F32), gnorm.reshape(1, DH).astype(F32))


def _t5_bucket_np(n):
    max_exact = REL_BUCKETS // 2
    nf = np.maximum(n, 1).astype(np.float32)
    large = max_exact + (np.log(nf / max_exact) / math.log(REL_MAX_DIST / max_exact)
                         * (REL_BUCKETS - max_exact)).astype(np.int32)
    large = np.minimum(large, REL_BUCKETS - 1)
    return np.where(n < max_exact, n, large)


def _dilated_kernel(span, dil, hpb, nlb, q_ref, kp_ref, kc_ref, vp_ref, vc_ref, bias_ref,
                    o_ref, lse_ref):
    CB = C_BLOCK
    n = pl.program_id(1)
    qi = lax.broadcasted_iota(jnp.int32, (CB, 2 * CB), 0)
    kj = lax.broadcasted_iota(jnp.int32, (CB, 2 * CB), 1)
    dist = qi + CB - kj
    band = (dist >= 0) & (dist <= span)
    band_first = band & ((kj >= CB) | (n > 0))

    def rows(r, jb):
        return pl.ds(jb * CB * dil + r, CB, stride=dil) if dil > 1 else pl.ds(jb * CB, CB)

    def with_prev(p_ref, c_ref, r, jb, sl):
        prev = p_ref[0, rows(r, 0), sl] if jb == 0 else c_ref[0, rows(r, jb - 1), sl]
        return jnp.concatenate([prev, c_ref[0, rows(r, jb), sl]], axis=0)

    def tiles(items):
        n_it = range(len(items))
        q = [q_ref[0, rows(r, jb), sl].astype(BF16) for r, jb, sl, _ in items]
        k = [with_prev(kp_ref, kc_ref, r, jb, sl).astype(BF16) for r, jb, sl, _ in items]
        v = [with_prev(vp_ref, vc_ref, r, jb, sl).astype(BF16) for r, jb, sl, _ in items]
        s = [_dot_nt(q[i], k[i]) * (DH ** -0.5) + bias_ref[items[i][3]] for i in n_it]
        s = [jnp.where(band_first if items[i][1] == 0 else band, s[i], -1e30) for i in n_it]
        mx = [jnp.max(a, axis=-1, keepdims=True) for a in s]
        p = [jnp.exp(s[i] - mx[i]) for i in n_it]
        den = [jnp.sum(a, axis=-1, keepdims=True) for a in p]
        o = [_dot(p[i].astype(BF16), v[i]) / den[i] for i in n_it]
        for i, (r, jb, sl, _) in enumerate(items):
            o_ref[0, rows(r, jb), sl] = o[i]
            lse_ref[0, rows(r, jb), sl] = jnp.broadcast_to(mx[i] + jnp.log(den[i]), (CB, DH))

    for jb in range(nlb):
        if dil == 1:
            for h0 in range(0, hpb, C_TILES):
                tiles([(0, jb, slice(h * DH, (h + 1) * DH), h) for h in range(h0, h0 + C_TILES)])
        elif dil == C_TILES:
            tiles([(r, jb, slice(0, DH), 0) for r in range(dil)])
        else:
            def body(g, carry, jb=jb):
                tiles([(g * C_TILES + i, jb, slice(0, DH), 0) for i in range(C_TILES)])
                return carry
            lax.fori_loop(0, dil // C_TILES, body, 0)


def _dilated_group(proj3, gi, rel_bias):
    b, t, _ = proj3.shape
    window, dil = C_PAIRS[gi]
    span = window // dil
    CB = C_BLOCK
    pb = CB * dil
    nlb = min(C_NLB, t // pb)
    rb = nlb * pb
    hpb = HEADS if dil == 1 else 1
    bw = hpb * DH
    qi = np.arange(CB)[:, None]
    kj = np.arange(2 * CB)[None, :]
    bucket = _t5_bucket_np(np.maximum(qi + CB - kj, 0) * dil)
    onehot = jnp.asarray(np.eye(REL_BUCKETS, dtype=np.float32)[bucket])
    bias = jnp.einsum("qkb,bh->hqk", onehot, rel_bias[:, gi * HEADS:(gi + 1) * HEADS].astype(F32),
                      precision=HIGHEST)

    def seg(which, prev):
        base = (OFF_C + which * C_GROUPS * HW + gi * HW) // bw
        if prev:
            return pl.BlockSpec((1, pb, bw), lambda i, n, h: (i, jnp.maximum(n * nlb - 1, 0), base + h))
        return pl.BlockSpec((1, rb, bw), lambda i, n, h: (i, n, base + h))

    o, lse = pl.pallas_call(
        functools.partial(_dilated_kernel, span, dil, hpb, nlb),
        grid=(b, t // rb, HEADS // hpb),
        in_specs=[seg(0, False), seg(1, True), seg(1, False), seg(2, True), seg(2, False),
                  pl.BlockSpec((hpb, CB, 2 * CB), lambda i, n, h: (h, 0, 0))],
        out_specs=[pl.BlockSpec((1, rb, bw), lambda i, n, h: (i, n, h)),
                   pl.BlockSpec((1, rb, bw), lambda i, n, h: (i, n, h))],
        out_shape=[jax.ShapeDtypeStruct((b, t, HW), F32),
                   jax.ShapeDtypeStruct((b, t, HW), F32)],
        compiler_params=_cparams(("parallel", "arbitrary", "arbitrary")),
        name=f"dilated_g{gi}",
    )(proj3, proj3, proj3, proj3, proj3, bias)
    return o.reshape(b * t, HW), lse.reshape(b * t, HW)


def _mix_kernel(x_ref, gate_ref, oa_ref, ob_ref, oc0_ref, oc1_ref, oc2_ref,
                l0_ref, l1_ref, l2_ref, wa_ref, wb_ref, wc_ref, wo_ref, out_ref):
    l0, l1, l2 = l0_ref[...], l1_ref[...], l2_ref[...]
    mx = jnp.maximum(jnp.maximum(l0, l1), l2)
    e0, e1, e2 = jnp.exp(l0 - mx), jnp.exp(l1 - mx), jnp.exp(l2 - mx)
    oc = (e0 * oc0_ref[...] + e1 * oc1_ref[...] + e2 * oc2_ref[...]) / (e0 + e1 + e2)
    mix = (_sigmoid(gate_ref[:, 0:D_MODEL].astype(F32)) * _dot(oa_ref[...], wa_ref[...])
           + _sigmoid(gate_ref[:, D_MODEL:2 * D_MODEL].astype(F32)) * _dot(ob_ref[...], wb_ref[...])
           + _sigmoid(gate_ref[:, 2 * D_MODEL:3 * D_MODEL].astype(F32)) * _dot(oc.astype(BF16), wc_ref[...]))
    out_ref[...] = x_ref[...] + _dot(mix.astype(BF16), wo_ref[...])


def _mix(x2, proj2, oa, ob, ocs, lses, wa, wb, wc, wo, tm=256):
    m = x2.shape[0]
    tm = min(tm, m)
    rowblk = lambda w: pl.BlockSpec((tm, w), lambda i: (i, 0))
    const = lambda shape: pl.BlockSpec(shape, lambda i: (0,) * len(shape))
    return pl.pallas_call(
        _mix_kernel,
        grid=(m // tm,),
        in_specs=[rowblk(D_MODEL), rowblk(3 * D_MODEL), rowblk(HW), rowblk(HW),
                  rowblk(HW), rowblk(HW), rowblk(HW), rowblk(HW), rowblk(HW), rowblk(HW),
                  const((HW, D_MODEL)), const((HW, D_MODEL)), const((HW, D_MODEL)),
                  const((D_MODEL, D_MODEL))],
        out_specs=rowblk(D_MODEL),
        out_shape=jax.ShapeDtypeStruct((m, D_MODEL), F32),
        compiler_params=_cparams(("parallel",)),
        name="branch_mix",
    )(x2, proj2, oa, ob, ocs[0], ocs[1], ocs[2], lses[0], lses[1], lses[2],
      wa.astype(BF16), wb.astype(BF16), wc.astype(BF16), wo.astype(BF16))


def _ffn_kernel(x_ref, g_ref, w1_ref, w3_ref, w2_ref, o_ref):
    x = x_ref[...]
    ms = jnp.mean(x * x, axis=-1, keepdims=True)
    h = (x * lax.rsqrt(ms + EPS) * g_ref[...]).astype(BF16)
    a = _dot(h, w1_ref[...])
    b = _dot(h, w3_ref[...])
    o_ref[...] = x + _dot((_silu(a) * b).astype(BF16), w2_ref[...])


def _ffn(x2, gain, w1, w3, w2, tm=512):
    m = x2.shape[0]
    ff = w1.shape[1]
    tm = min(tm, m)
    resident = lambda shape: pl.BlockSpec(shape, lambda i: (0, 0), pipeline_mode=pl.Buffered(1))
    return pl.pallas_call(
        _ffn_kernel,
        grid=(m // tm,),
        in_specs=[pl.BlockSpec((tm, D_MODEL), lambda i: (i, 0)),
                  pl.BlockSpec((1, D_MODEL), lambda i: (0, 0)),
                  resident((D_MODEL, ff)), resident((D_MODEL, ff)), resident((ff, D_MODEL))],
        out_specs=pl.BlockSpec((tm, D_MODEL), lambda i: (i, 0)),
        out_shape=jax.ShapeDtypeStruct((m, D_MODEL), F32),
        compiler_params=_cparams(("parallel",)),
        name="ffn",
    )(x2, gain.reshape(1, D_MODEL).astype(F32), w1.astype(BF16), w3.astype(BF16), w2.astype(BF16))


HALF = D_MODEL // 2
U32 = jnp.uint32


def _pack_bf16_pairs(x):
    r = x.astype(BF16).astype(F32)
    lo = lax.bitcast_convert_type(r[:, :HALF], U32) >> 16
    hi = lax.bitcast_convert_type(r[:, HALF:], U32)
    return hi | lo


def _unpack_bf16_pairs(p):
    lo = lax.bitcast_convert_type(p << 16, F32)
    hi = lax.bitcast_convert_type(p & U32(0xFFFF0000), F32)
    return lo, hi


def _router_kernel(x_ref, g_ref, wr_ref, h_ref, ids_ref, ps_ref):
    x = x_ref[...]
    ms = jnp.mean(x * x, axis=-1, keepdims=True)
    h = x * lax.rsqrt(ms + EPS) * g_ref[...]
    h_ref[...] = _pack_bf16_pairs(h)
    logits = jnp.dot(h, wr_ref[...], precision=HIGHEST, preferred_element_type=F32)
    lane = lax.broadcasted_iota(jnp.int32, logits.shape, 1)
    neg = jnp.float32(-jnp.inf)
    l1 = jnp.where(lane < N_EXPERTS, logits, neg)
    m1 = jnp.max(l1, axis=-1, keepdims=True)
    i1 = jnp.min(jnp.where(l1 == m1, lane, 128), axis=-1, keepdims=True)
    l2 = jnp.where(lane == i1, neg, l1)
    m2 = jnp.max(l2, axis=-1, keepdims=True)
    i2 = jnp.min(jnp.where(l2 == m2, lane, 128), axis=-1, keepdims=True)
    e = jnp.exp(m2 - m1)
    p1 = 1.0 / (1.0 + e)
    p2 = e / (1.0 + e)
    ids_ref[...] = jnp.where(lane == 0, i1, jnp.where(lane == 1, i2, 0))
    ps_ref[...] = jnp.where(lane == 0, p1, jnp.where(lane == 1, p2, 0.0))


def _router(x2, gain, w_router, tm=512):
    m = x2.shape[0]
    tm = min(tm, m)
    wr = jnp.pad(w_router.astype(F32), ((0, 0), (0, 128 - N_EXPERTS)))
    return pl.pallas_call(
        _router_kernel,
        grid=(m // tm,),
        in_specs=[pl.BlockSpec((tm, D_MODEL), lambda i: (i, 0)),
                  pl.BlockSpec((1, D_MODEL), lambda i: (0, 0)),
                  pl.BlockSpec((D_MODEL, 128), lambda i: (0, 0))],
        out_specs=[pl.BlockSpec((tm, HALF), lambda i: (i, 0)),
                   pl.BlockSpec((tm, 128), lambda i: (i, 0)),
                   pl.BlockSpec((tm, 128), lambda i: (i, 0))],
        out_shape=[jax.ShapeDtypeStruct((m, HALF), U32),
                   jax.ShapeDtypeStruct((m, 128), jnp.int32),
                   jax.ShapeDtypeStruct((m, 128), F32)],
        compiler_params=_cparams(("parallel",)),
        name="router",
    )(x2, gain.reshape(1, D_MODEL).astype(F32), wr)


def _route_positions(ids, tm):
    m = ids.shape[0]
    e_flat = ids[:, :TOP_K].reshape(-1)
    onehot = (e_flat[:, None] == jnp.arange(N_EXPERTS)[None, :]).astype(jnp.int32)
    csum = jnp.cumsum(onehot, axis=0)
    counts = csum[-1]
    gsz = ((counts + tm - 1) // tm) * tm
    gend = jnp.cumsum(gsz)
    pos = jnp.sum(onehot * (gend - gsz + csum - 1), axis=1)
    nt = (TOP_K * m + N_EXPERTS * tm) // tm
    n_used = gend[-1] // tm
    tile_e = jnp.sum((jnp.arange(nt)[:, None] * tm >= gend[None, :]).astype(jnp.int32), axis=1)
    last_e = jnp.sum(((n_used - 1) * tm >= gend).astype(jnp.int32))
    tile_e = jnp.minimum(tile_e, last_e)
    return pos.astype(jnp.int32), tile_e.astype(jnp.int32), n_used.reshape(1).astype(jnp.int32), nt


def _dispatch_kernel(pos_ref, h_ref, init_ref, xs_ref, sem):
    del init_ref
    td = h_ref.shape[0]

    def row_copy(t, s):
        return pltpu.make_async_copy(h_ref.at[pl.ds(t, 1)],
                                     xs_ref.at[pl.ds(pos_ref[0, 0, TOP_K * t + s], 1)], sem)

    def start(t, c):
        for s in range(TOP_K):
            row_copy(t, s).start()
        return c

    lax.fori_loop(0, td, start, 0, unroll=8)
    for s in range(TOP_K):
        pltpu.make_async_copy(h_ref, xs_ref.at[pl.ds(0, td)], sem).wait()


def _dispatch(h, pos, npad, td=1024):
    m = h.shape[0]
    td = min(td, m)
    return pl.pallas_call(
        _dispatch_kernel,
        grid=(m // td,),
        in_specs=[pl.BlockSpec((1, 1, TOP_K * td), lambda i: (i, 0, 0), memory_space=pltpu.SMEM),
                  pl.BlockSpec((td, HALF), lambda i: (i, 0)),
                  pl.BlockSpec(memory_space=pl.ANY)],
        out_specs=pl.BlockSpec(memory_space=pl.ANY),
        out_shape=jax.ShapeDtypeStruct((npad, HALF), U32),
        scratch_shapes=[pltpu.SemaphoreType.DMA(())],
        input_output_aliases={2: 0},
        compiler_params=_cparams(("arbitrary",)),
        name="moe_dispatch",
    )(pos.reshape(m // td, 1, TOP_K * td), h, jnp.zeros((npad, HALF), U32))


def _experts_kernel(te_ref, nu_ref, xs_ref, w1_ref, w3_ref, w2_ref, y_ref, xb_ref, acc_ref):
    del te_ref
    i = pl.program_id(0)
    j = pl.program_id(1)

    @pl.when(i < nu_ref[0])
    def _():
        @pl.when(j == 0)
        def _():
            lo, hi = _unpack_bf16_pairs(xs_ref[...])
            xb_ref[...] = jnp.concatenate([lo.astype(BF16), hi.astype(BF16)], axis=1)
            acc_ref[...] = jnp.zeros_like(acc_ref)

        xb = xb_ref[...]
        a = _dot(xb, w1_ref[0])
        b = _dot(xb, w3_ref[0])
        acc_ref[...] += _dot((_silu(a) * b).astype(BF16), w2_ref[0])

        @pl.when(j == pl.num_programs(1) - 1)
        def _():
            y_ref[...] = _pack_bf16_pairs(acc_ref[...])

    @pl.when((i >= nu_ref[0]) & (j == 0))
    def _():
        y_ref[...] = jnp.zeros_like(y_ref)


def _experts(xs, tile_e, n_used, w1, w3, w2, tm, tf=1792):
    npad = xs.shape[0]
    ff = w1.shape[2]
    nj = ff // tf
    row = lambda i, j, te, nu: (jnp.minimum(i, nu[0] - 1), 0)
    jj = lambda i, j, nu: jnp.where(i < nu[0], j, nj - 1)
    return pl.pallas_call(
        _experts_kernel,
        grid_spec=pltpu.PrefetchScalarGridSpec(
            num_scalar_prefetch=2,
            grid=(npad // tm, nj),
            in_specs=[pl.BlockSpec((tm, HALF), row),
                      pl.BlockSpec((1, D_MODEL, tf), lambda i, j, te, nu: (te[i], 0, jj(i, j, nu))),
                      pl.BlockSpec((1, D_MODEL, tf), lambda i, j, te, nu: (te[i], 0, jj(i, j, nu))),
                      pl.BlockSpec((1, tf, D_MODEL), lambda i, j, te, nu: (te[i], jj(i, j, nu), 0))],
            out_specs=pl.BlockSpec((tm, HALF), lambda i, j, te, nu: (i, 0)),
            scratch_shapes=[pltpu.VMEM((tm, D_MODEL), BF16), pltpu.VMEM((tm, D_MODEL), F32)]),
        out_shape=jax.ShapeDtypeStruct((npad, HALF), U32),
        compiler_params=_cparams(("arbitrary", "arbitrary")),
        name="moe_experts",
    )(tile_e, n_used, xs, w1.astype(BF16), w3.astype(BF16), w2.astype(BF16))


def _combine_kernel(pos_ref, x_ref, ps_ref, y_ref, o_ref, buf_ref, sem):
    td = x_ref.shape[0]

    def row_copy(t, s):
        return pltpu.make_async_copy(y_ref.at[pl.ds(pos_ref[0, 0, TOP_K * t + s], 1)],
                                     buf_ref.at[s, pl.ds(t, 1)], sem.at[s])

    def start(t, c):
        for s in range(TOP_K):
            row_copy(t, s).start()
        return c

    lax.fori_loop(0, td, start, 0, unroll=8)
    for s in range(TOP_K):
        pltpu.make_async_copy(y_ref.at[pl.ds(0, td)], buf_ref.at[s], sem.at[s]).wait()
    ps = ps_ref[...]
    lo0, hi0 = _unpack_bf16_pairs(buf_ref[0])
    lo1, hi1 = _unpack_bf16_pairs(buf_ref[1])
    o_ref[:, :HALF] = x_ref[:, :HALF] + ps[:, 0:1] * lo0 + ps[:, 1:2] * lo1
    o_ref[:, HALF:] = x_ref[:, HALF:] + ps[:, 0:1] * hi0 + ps[:, 1:2] * hi1


def _combine(x2, ps, pos, y, td=1024):
    m = x2.shape[0]
    td = min(td, m)
    return pl.pallas_call(
        _combine_kernel,
        grid=(m // td,),
        in_specs=[pl.BlockSpec((1, 1, TOP_K * td), lambda i: (i, 0, 0), memory_space=pltpu.SMEM),
                  pl.BlockSpec((td, D_MODEL), lambda i: (i, 0)),
                  pl.BlockSpec((td, 128), lambda i: (i, 0)),
                  pl.BlockSpec(memory_space=pl.ANY)],
        out_specs=pl.BlockSpec((td, D_MODEL), lambda i: (i, 0)),
        out_shape=jax.ShapeDtypeStruct((m, D_MODEL), F32),
        scratch_shapes=[pltpu.VMEM((TOP_K, td, HALF), U32), pltpu.SemaphoreType.DMA((TOP_K,))],
        compiler_params=_cparams(("arbitrary",)),
        name="moe_combine",
    )(pos.reshape(m // td, 1, TOP_K * td), x2, ps, y)


def _moe(x2, gain, w_router, w1, w3, w2, tm=512):
    m = x2.shape[0]
    tm = min(tm, m)
    h, ids, ps = _router(x2, gain, w_router)
    pos, tile_e, n_used, nt = _route_positions(ids, tm)
    xs = _dispatch(h, pos, nt * tm)
    y = _experts(xs, tile_e, n_used, w1, w3, w2, tm)
    return _combine(x2, ps, pos, y)


def _relayout_w_in(w):
    k = w.shape[0]
    main = [w[:, _R_BRG:_R_END], w[:, _R_AQKV:_R_BETA], w[:, _R_AGATE:_R_B], w[:, _R_B:_R_C]]
    second = [w[:, _R_C:_R_BRG], w[:, _R_BETA:_R_AGATE],
              jnp.zeros((k, NP_C - OFF_BA - 2 * HEADS), w.dtype)]
    return (jnp.concatenate(main, axis=1).astype(BF16), jnp.concatenate(second, axis=1).astype(BF16))


def _layer_mixers(x2, b, t, layer, lower_bounds, w_in, norm_mix, conv_a, a_log, dt_bias,
                  gnorm_a, gnorm_b, qnorm_c, knorm_c, rel_bias, w_br_a, w_br_b, w_br_c, w_out):
    w_main, w_second = _relayout_w_in(w_in[layer])
    gain = norm_mix[layer].astype(F32)
    proj2 = _norm_proj(x2, gain, w_main, BF16)
    proj3 = proj2.reshape(b, t, NP_MAIN)
    qk_gain = jnp.concatenate([jnp.tile(qnorm_c[layer].astype(F32), (1, HEADS)).reshape(-1),
                               jnp.tile(knorm_c[layer].astype(F32), (1, HEADS)).reshape(-1)])
    projc3 = _norm_proj(x2, gain, w_second, F32, head_gain=qk_gain).reshape(b, t, NP_C)
    oa = _deltanet(proj3, projc3, conv_a[layer], a_log[layer], dt_bias[layer], gnorm_a[layer])
    ob = _hgrn(proj3, lower_bounds[layer], gnorm_b[layer])
    ocs, lses = [], []
    for gi in range(C_GROUPS):
        o, lse = _dilated_group(projc3, gi, rel_bias)
        ocs.append(o)
        lses.append(lse)
    return _mix(x2, proj2, oa.reshape(b * t, HW), ob.reshape(b * t, HW), ocs, lses,
                w_br_a[layer], w_br_b[layer], w_br_c[layer], w_out[layer])


def kernel(x, w_in, norm_mix, conv_a, a_log, dt_bias, gnorm_a, lb_logits, gnorm_b, qnorm_c, knorm_c, rel_bias, w_br_a, w_br_b, w_br_c, w_out, norm_ffn, ffn_w1, ffn_w3, ffn_w2, router, moe_w1, moe_w3, moe_w2):
    b, t, _ = x.shape
    depth = w_in.shape[0]
    p_lb = jax.nn.softmax(lb_logits.astype(F32), axis=0)
    lower_bounds = jnp.cumsum(p_lb, axis=0) - p_lb[0:1]
    x2 = x.reshape(b * t, D_MODEL).astype(F32)
    for layer in range(depth):
        x2 = _layer_mixers(x2, b, t, layer, lower_bounds, w_in, norm_mix, conv_a, a_log, dt_bias,
                           gnorm_a, gnorm_b, qnorm_c, knorm_c, rel_bias,
                           w_br_a, w_br_b, w_br_c, w_out)
        li = layer // 2
        if layer % 2 == 0:
            x2 = _ffn(x2, norm_ffn[layer], ffn_w1[li], ffn_w3[li], ffn_w2[li])
        else:
            x2 = _moe(x2, norm_ffn[layer], router[li], moe_w1[li], moe_w3[li], moe_w2[li])
    return x2.reshape(b, t, D_MODEL).astype(x.dtype)
```

```python
import functools
import math

import numpy as np
import jax
import jax.numpy as jnp
from jax import lax
from jax.experimental import pallas as pl
from jax.experimental.pallas import tpu as pltpu

F32 = jnp.float32
BF16 = jnp.bfloat16
HIGHEST = lax.Precision.HIGHEST

D_MODEL = 1024
EPS = 1e-6
HEADS = 4
DH = 128
HW = HEADS * DH
A_CONV = 4
A_CHUNK = 64
A_ROWS = 512
B_ROWS = 512
B_BLK = 16
B_GROUP = 4
B_FAST = 32
HGRN_SAFE_DECAY = 60.0
C_PAIRS = ((128, 1), (512, 4), (2048, 16))
C_GROUPS = 3
C_BLOCK = 128
C_TILES = 4
C_NLB = 8
REL_BUCKETS = 32
REL_MAX_DIST = 2048
N_EXPERTS = 8
TOP_K = 2

OFF_BRG = 0
OFF_AQKV = 3072
OFF_AGATE = 4608
OFF_B = 5120
NP_MAIN = 7168
OFF_C = 0
OFF_BA = 4608
NP_C = 5120

_R_AQKV, _R_BETA, _R_AGATE, _R_B, _R_C, _R_BRG, _R_END = 0, 1536, 1544, 2056, 4104, 8712, 11784

VMEM_LIMIT = 56 * 1024 * 1024


def _cparams(sem):
    return pltpu.CompilerParams(dimension_semantics=sem, vmem_limit_bytes=VMEM_LIMIT)


def _sigmoid(x):
    return 1.0 / (1.0 + jnp.exp(-x))


def _silu(x):
    return x * _sigmoid(x)


def _softplus(x):
    return jnp.maximum(x, 0.0) + jnp.log(1.0 + jnp.exp(-jnp.abs(x)))


def _dot(a, b):
    return jnp.dot(a, b, preferred_element_type=F32)


def _dot_nt(a, b, precision=None):
    return lax.dot_general(a, b, (((1,), (1,)), ((), ())), precision=precision,
                           preferred_element_type=F32)


def _dot_tn(a, b):
    return lax.dot_general(a, b, (((0,), (0,)), ((), ())), preferred_element_type=F32)


def _norm_proj_kernel(tn, head_norm_cols, x_ref, g_ref, w_ref, hg_ref, o_ref):
    x = x_ref[...]
    ms = jnp.mean(x * x, axis=-1, keepdims=True)
    h = (x * lax.rsqrt(ms + EPS) * g_ref[...]).astype(BF16)
    for c0 in range(0, w_ref.shape[1], tn):
        r = _dot(h, w_ref[:, c0:c0 + tn])
        if c0 < head_norm_cols:
            heads = [r[:, d0:d0 + DH] for d0 in range(0, tn, DH)]
            heads = [a * lax.rsqrt(jnp.mean(a * a, axis=-1, keepdims=True) + EPS) for a in heads]
            r = jnp.concatenate(heads, axis=1) * hg_ref[:, c0:c0 + tn]
        o_ref[:, c0:c0 + tn] = r.astype(o_ref.dtype)


def _norm_proj(x2, gain, w_bf16, out_dtype, head_gain=None, tm=512, tn=1024):
    m = x2.shape[0]
    n = w_bf16.shape[1]
    tm = min(tm, m)
    norm_cols = 0 if head_gain is None else head_gain.shape[0]
    assert norm_cols % tn == 0
    hg = jnp.zeros((1, n), F32)
    if head_gain is not None:
        hg = hg.at[0, :norm_cols].set(head_gain.astype(F32))
    return pl.pallas_call(
        functools.partial(_norm_proj_kernel, tn, norm_cols),
        grid=(m // tm,),
        in_specs=[pl.BlockSpec((tm, D_MODEL), lambda i: (i, 0)),
                  pl.BlockSpec((1, D_MODEL), lambda i: (0, 0)),
                  pl.BlockSpec((D_MODEL, n), lambda i: (0, 0), pipeline_mode=pl.Buffered(1)),
                  pl.BlockSpec((1, n), lambda i: (0, 0))],
        out_specs=pl.BlockSpec((tm, n), lambda i: (i, 0)),
        out_shape=jax.ShapeDtypeStruct((m, n), out_dtype),
        compiler_params=_cparams(("parallel",)),
        name="norm_proj",
    )(x2, gain.reshape(1, D_MODEL), w_bf16, hg)


def _dotb(a, b):
    return _dot(a.astype(BF16), b.astype(BF16))


def _dot_exact_lhs(a_bf16, b):
    b0 = b.astype(BF16)
    r1 = b - b0.astype(F32)
    b1 = r1.astype(BF16)
    b2 = (r1 - b1.astype(F32)).astype(BF16)
    return _dot(a_bf16, b0) + (_dot(a_bf16, b1) + _dot(a_bf16, b2))


def _deltanet_kernel(qkv_ref, gate_ref, ba_ref, convw_ref, arow_ref, dtrow_ref, gn_ref,
                     o_ref, s_ref, xe_ref):
    C = A_CHUNK
    R = A_ROWS
    HS = range(HEADS)
    IT = range((R // C) * HEADS)

    @pl.when(pl.program_id(1) == 0)
    def _():
        s_ref[...] = jnp.zeros_like(s_ref)
        xe_ref[0:8, :] = jnp.zeros((8, 3 * HW), F32)

    x = qkv_ref[0].astype(F32)
    xe_ref[8:8 + R, :] = x
    w = convw_ref[...]
    y = (w[3:4] * x + w[2:3] * xe_ref[7:7 + R, :] + w[1:2] * xe_ref[6:6 + R, :]
         + w[0:1] * xe_ref[5:5 + R, :])
    xe_ref[0:8, :] = x[R - 8:R]
    y = _silu(y)

    ba = ba_ref[0]
    beta_all = _sigmoid(ba)
    g_all = arow_ref[...] * _softplus(ba + dtrow_ref[...])
    gate = _silu(gate_ref[0].astype(F32))
    s_old = [s_ref[h] for h in HS]

    row = lax.broadcasted_iota(jnp.int32, (C, C), 0)
    col = lax.broadcasted_iota(jnp.int32, (C, C), 1)
    lmat = (col <= row).astype(BF16)
    rowx = lax.broadcasted_iota(jnp.int32, (C, DH + C), 0)
    colx = lax.broadcasted_iota(jnp.int32, (C, DH + C), 1)
    umask = (colx < DH) | (rowx > colx - DH)
    eye = (row == col).astype(F32)
    bd8 = (row >> 3) == (col >> 3)

    def merge_mask(sh):
        return (((row >> (sh + 1)) == (col >> (sh + 1)))
                & (((row >> sh) & 1) == 1) & (((col >> sh) & 1) == 0))

    rows = [slice((i // HEADS) * C, (i // HEADS + 1) * C) for i in IT]
    hd = [i % HEADS for i in IT]
    q = [y[rows[i], hd[i] * DH:(hd[i] + 1) * DH] for i in IT]
    k = [y[rows[i], HW + hd[i] * DH:HW + (hd[i] + 1) * DH] for i in IT]
    v = [y[rows[i], 2 * HW + hd[i] * DH:2 * HW + (hd[i] + 1) * DH] for i in IT]
    q = [a * (lax.rsqrt(jnp.sum(a * a, axis=-1, keepdims=True) + EPS) * (DH ** -0.5)) for a in q]
    k = [a * lax.rsqrt(jnp.sum(a * a, axis=-1, keepdims=True) + EPS) for a in k]
    beta = [beta_all[rows[i], hd[i]:hd[i] + 1] for i in IT]
    gb = [jnp.broadcast_to(g_all[rows[i], HEADS + hd[i]:HEADS + hd[i] + 1], (C, DH + C)) for i in IT]
    dext = [_dot_exact_lhs(lmat, jnp.where(umask, gb[i], 0.0)) for i in IT]
    gc = [d[:, :DH] for d in dext]
    edm = [jnp.exp(d[:, DH:]) for d in dext]
    egc = [jnp.exp(g) for g in gc]
    gl = [g[C - 1:C, :] for g in gc]
    kb = [k[i] * beta[i] for i in IT]
    m = [_dot_nt(kb[i].astype(BF16), k[i].astype(BF16)) * jnp.where(row > col, edm[i], 0.0)
         for i in IT]

    nd = [jnp.where(bd8, -a, 0.0) for a in m]
    p2 = [_dotb(a, a) for a in nd]
    p4 = [_dotb(a, a) for a in p2]
    x1 = [eye + nd[i] for i in IT]
    x1 = [x1[i] + _dotb(x1[i], p2[i]) for i in IT]
    xi = [x1[i] + _dotb(x1[i], p4[i]) for i in IT]
    for sh in (3, 4, 5):
        mm = merge_mask(sh)
        t = [_dotb(xi[i], jnp.where(mm, m[i], 0.0)) for i in IT]
        xi = [xi[i] - _dotb(t[i], xi[i]) for i in IT]

    rhs = [jnp.concatenate([v[i] * beta[i], kb[i] * egc[i]], axis=1) for i in IT]
    sol = [_dotb(xi[i], rhs[i]) for i in IT]
    attn = [(_dot_nt(q[i].astype(BF16), k[i].astype(BF16))
             * jnp.where(row >= col, edm[i], 0.0)).astype(BF16) for i in IT]
    qg = [(q[i] * egc[i]).astype(BF16) for i in IT]
    kg = [(k[i] * jnp.exp(gl[i] - gc[i])).astype(BF16) for i in IT]
    egl = [jnp.exp(a) for a in gl]

    s_cur = s_old
    for c in range(R // C):
        it = [c * HEADS + h for h in HS]
        sb = [a.astype(BF16) for a in s_cur]
        v_new = [(sol[i][:, :DH] - _dot(sol[i][:, DH:].astype(BF16), sb[h])).astype(BF16)
                 for h, i in enumerate(it)]
        o = [_dot(qg[i], sb[h]) + _dot(attn[i], v_new[h]) for h, i in enumerate(it)]
        s_cur = [s_cur[h] * egl[i] + _dot_tn(kg[i], v_new[h]) for h, i in enumerate(it)]
        o = [a * lax.rsqrt(jnp.mean(a * a, axis=-1, keepdims=True) + EPS) * gn_ref[...] for a in o]
        o_ref[0, c * C:(c + 1) * C, :] = (jnp.concatenate(o, axis=1)
                                          * gate[c * C:(c + 1) * C]).astype(o_ref.dtype)
    for h in HS:
        s_ref[h] = s_cur[h]


def _deltanet(proj3, projc3, conv_w, a_log, dt_bias, gnorm):
    b, t, _ = proj3.shape
    C = A_ROWS
    pad = jnp.zeros((HEADS,), F32)
    arow = jnp.concatenate([pad, -jnp.exp(a_log.astype(F32)), jnp.zeros((120,), F32)]).reshape(1, 128)
    dtrow = jnp.concatenate([pad, dt_bias.astype(F32), jnp.zeros((120,), F32)]).reshape(1, 128)
    const = lambda shape: pl.BlockSpec(shape, lambda i, c: (0,) * len(shape))
    return pl.pallas_call(
        _deltanet_kernel,
        grid=(b, t // C),
        in_specs=[pl.BlockSpec((1, C, 3 * HW), lambda i, c: (i, c, OFF_AQKV // (3 * HW))),
                  pl.BlockSpec((1, C, HW), lambda i, c: (i, c, OFF_AGATE // HW)),
                  pl.BlockSpec((1, C, 128), lambda i, c: (i, c, OFF_BA // 128)),
                  const((A_CONV, 3 * HW)), const((1, 128)), const((1, 128)), const((1, DH))],
        out_specs=pl.BlockSpec((1, C, HW), lambda i, c: (i, c, 0)),
        out_shape=jax.ShapeDtypeStruct((b, t, HW), BF16),
        scratch_shapes=[pltpu.VMEM((HEADS, DH, DH), F32), pltpu.VMEM((C + 8, 3 * HW), F32)],
        compiler_params=_cparams(("parallel", "arbitrary")),
        name="deltanet",
    )(proj3, proj3, projc3, conv_w.astype(F32), arow, dtrow, gnorm.reshape(1, DH).astype(F32))


def _hgrn_kernel(q_ref, f_ref, i_ref, g_ref, lb_ref, gn_ref, o_ref, st_ref, lg_ref, k_ref):
    R, K = B_ROWS, B_BLK
    HS = range(HEADS)
    NB = R // K

    @pl.when(pl.program_id(1) == 0)
    def _():
        st_ref[...] = jnp.zeros_like(st_ref)

    lb_all = lb_ref[...]
    fl = f_ref[0].astype(F32)
    lg = jnp.log(lb_all + (1.0 - lb_all) * _sigmoid(fl))
    k_ref[...] = (1.0 - lb_all) * _sigmoid(-fl)
    lg_ref[...] = lg
    brow = lax.broadcasted_iota(jnp.int32, (R // B_FAST, R), 0)
    bcol = lax.broadcasted_iota(jnp.int32, (R // B_FAST, R), 1)
    chunksum = _dot((bcol // B_FAST == brow).astype(BF16), lg.astype(BF16))
    max_decay = jnp.max(-chunksum)

    H8 = 8
    row8 = lax.broadcasted_iota(jnp.int32, (H8, DH), 0)
    sls = [slice(h * DH, (h + 1) * DH) for h in HS]

    def pairs_exact(q, k, v, gc):
        n_it = range(len(q))
        otile = [[jnp.zeros((H8, DH), F32) for _ in range(K // H8)] for _ in n_it]
        for j in range(K):
            for t in range(j // H8, K // H8):
                ts = slice(t * H8, (t + 1) * H8)
                for i in n_it:
                    d = gc[i][ts] - gc[i][j:j + 1]
                    rel = jnp.exp(jnp.where(row8 >= j - t * H8, d, -1e30) if t == j // H8 else d)
                    sj = jnp.sum(q[i][ts] * k[i][j:j + 1] * rel, axis=-1, keepdims=True)
                    otile[i][t] = otile[i][t] + sj * v[i][j:j + 1]
        return [jnp.concatenate(otile[i], axis=0) for i in n_it]

    def run(exact):
        KB = K if exact else B_FAST
        GR = B_GROUP * KB
        grow = lax.broadcasted_iota(jnp.int32, (GR, GR), 0)
        gcol = lax.broadcasted_iota(jnp.int32, (GR, GR), 1)
        same_block_lower = (gcol <= grow) & (grow // KB == gcol // KB)
        lmat_g = same_block_lower.astype(BF16)
        st = [st_ref[h] for h in HS]
        for r0 in range(0, R, GR):
            gs = slice(r0, r0 + GR)
            blks = [slice(bi * KB, (bi + 1) * KB) for bi in range(B_GROUP)]
            gc_all = _dot_exact_lhs(lmat_g, lg_ref[gs, :])
            gl_all = jnp.concatenate([jnp.broadcast_to(gc_all[bs.stop - 1:bs.stop], (KB, HW)) for bs in blks],
                                     axis=0)
            q = [_silu(q_ref[0, gs, sl].astype(F32)) for sl in sls]
            k = [k_ref[gs, sl] for sl in sls]
            v = [i_ref[0, gs, sl].astype(F32) for sl in sls]
            gc = [gc_all[:, sl] for sl in sls]
            gl = [gl_all[:, sl] for sl in sls]
            kg = [(k[h] * jnp.exp(gl[h] - gc[h])).astype(BF16) for h in HS]
            qs = [(q[h] * jnp.exp(gc[h])).astype(BF16) for h in HS]
            vb = [a.astype(BF16) for a in v]
            upd = [[_dot_tn(vb[h][bs], kg[h][bs]) for h in HS] for bs in blks]
            egl = [[jnp.exp(gl[h][bs.stop - 1:bs.stop]) for h in HS] for bs in blks]
            if exact:
                items = [(h, bs) for bs in blks for h in HS]
                pe = pairs_exact([q[h][bs] for h, bs in items], [k[h][bs] for h, bs in items],
                                 [v[h][bs] for h, bs in items], [gc[h][bs] for h, bs in items])
                intra = [jnp.concatenate([pe[bi * HEADS + h] for bi in range(B_GROUP)], axis=0) for h in HS]
            else:
                qf = [(q[h] * jnp.exp(gc[h] - gl[h])).astype(BF16) for h in HS]
                s = [jnp.where(same_block_lower, _dot_nt(qf[h], kg[h]), 0.0).astype(BF16) for h in HS]
                intra = [_dot(s[h], vb[h]) for h in HS]
            for bi, bs in enumerate(blks):
                o = [_dot_nt(qs[h][bs], st[h].astype(BF16)) + intra[h][bs] for h in HS]
                st = [st[h] * egl[bi][h] + upd[bi][h] for h in HS]
                o = [a * lax.rsqrt(jnp.mean(a * a, axis=-1, keepdims=True) + EPS) * gn_ref[...] for a in o]
                rs = slice(r0 + bi * KB, r0 + (bi + 1) * KB)
                o_ref[0, rs, :] = (jnp.concatenate(o, axis=1)
                                   * _sigmoid(g_ref[0, rs, :].astype(F32))).astype(o_ref.dtype)
        for h in HS:
            st_ref[h] = st[h]

    lax.cond(max_decay > HGRN_SAFE_DECAY, lambda: run(True), lambda: run(False))


def _hgrn(proj3, lb, gnorm):
    b, t, _ = proj3.shape
    R = B_ROWS
    seg = lambda s: pl.BlockSpec((1, R, HW), lambda i, c: (i, c, OFF_B // HW + s))
    const = lambda shape: pl.BlockSpec(shape, lambda i, c: (0,) * len(shape))
    return pl.pallas_call(
        _hgrn_kernel,
        grid=(b, t // R),
        in_specs=[seg(0), seg(1), seg(2), seg(3), const((1, HW)), const((1, DH))],
        out_specs=pl.BlockSpec((1, R, HW), lambda i, c: (i, c, 0)),
        out_shape=jax.ShapeDtypeStruct((b, t, HW), BF16),
        scratch_shapes=[pltpu.VMEM((HEADS, DH, DH), F32), pltpu.VMEM((R, HW), F32),
                        pltpu.VMEM((R, HW), F32)],
        compiler_params=_cparams(("parallel", "arbitrary")),
        name="hgrn2",
    )(proj3, proj3, proj3, proj3, lb.reshape(1, HW).astype(F32), gnorm.reshape(1, DH).astype(F32))


def _t5_bucket_np(n):
    max_exact = REL_BUCKETS // 2
    nf = np.maximum(n, 1).astype(np.float32)
    large = max_exact + (np.log(nf / max_exact) / math.log(REL_MAX_DIST / max_exact)
                         * (REL_BUCKETS - max_exact)).astype(np.int32)
    large = np.minimum(large, REL_BUCKETS - 1)
    return np.where(n < max_exact, n, large)


def _dilated_kernel(span, dil, hpb, nlb, q_ref, kp_ref, kc_ref, vp_ref, vc_ref, bias_ref,
                    o_ref, lse_ref):
    CB = C_BLOCK
    n = pl.program_id(1)
    qi = lax.broadcasted_iota(jnp.int32, (CB, 2 * CB), 0)
    kj = lax.broadcasted_iota(jnp.int32, (CB, 2 * CB), 1)
    dist = qi + CB - kj
    band = (dist >= 0) & (dist <= span)
    band_first = band & ((kj >= CB) | (n > 0))

    def rows(r, jb):
        return pl.ds(jb * CB * dil + r, CB, stride=dil) if dil > 1 else pl.ds(jb * CB, CB)

    def with_prev(p_ref, c_ref, r, jb, sl):
        prev = p_ref[0, rows(r, 0), sl] if jb == 0 else c_ref[0, rows(r, jb - 1), sl]
        return jnp.concatenate([prev, c_ref[0, rows(r, jb), sl]], axis=0)

    def tiles(items):
        n_it = range(len(items))
        q = [q_ref[0, rows(r, jb), sl].astype(BF16) for r, jb, sl, _ in items]
        k = [with_prev(kp_ref, kc_ref, r, jb, sl).astype(BF16) for r, jb, sl, _ in items]
        v = [with_prev(vp_ref, vc_ref, r, jb, sl).astype(BF16) for r, jb, sl, _ in items]
        s = [_dot_nt(q[i], k[i]) * (DH ** -0.5) + bias_ref[items[i][3]] for i in n_it]
        s = [jnp.where(band_first if items[i][1] == 0 else band, s[i], -1e30) for i in n_it]
        mx = [jnp.max(a, axis=-1, keepdims=True) for a in s]
        p = [jnp.exp(s[i] - mx[i]) for i in n_it]
        den = [jnp.sum(a, axis=-1, keepdims=True) for a in p]
        o = [_dot(p[i].astype(BF16), v[i]) / den[i] for i in n_it]
        for i, (r, jb, sl, _) in enumerate(items):
            o_ref[0, rows(r, jb), sl] = o[i]
            lse_ref[0, rows(r, jb), sl] = jnp.broadcast_to(mx[i] + jnp.log(den[i]), (CB, DH))

    for jb in range(nlb):
        if dil == 1:
            for h0 in range(0, hpb, C_TILES):
                tiles([(0, jb, slice(h * DH, (h + 1) * DH), h) for h in range(h0, h0 + C_TILES)])
        elif dil == C_TILES:
            tiles([(r, jb, slice(0, DH), 0) for r in range(dil)])
        else:
            def body(g, carry, jb=jb):
                tiles([(g * C_TILES + i, jb, slice(0, DH), 0) for i in range(C_TILES)])
                return carry
            lax.fori_loop(0, dil // C_TILES, body, 0)


def _dilated_group(proj3, gi, rel_bias):
    b, t, _ = proj3.shape
    window, dil = C_PAIRS[gi]
    span = window // dil
    CB = C_BLOCK
    pb = CB * dil
    nlb = min(C_NLB, t // pb)
    rb = nlb * pb
    hpb = HEADS if dil == 1 else 1
    bw = hpb * DH
    qi = np.arange(CB)[:, None]
    kj = np.arange(2 * CB)[None, :]
    bucket = _t5_bucket_np(np.maximum(qi + CB - kj, 0) * dil)
    onehot = jnp.asarray(np.eye(REL_BUCKETS, dtype=np.float32)[bucket])
    bias = jnp.einsum("qkb,bh->hqk", onehot, rel_bias[:, gi * HEADS:(gi + 1) * HEADS].astype(F32),
                      precision=HIGHEST)

    def seg(which, prev):
        base = (OFF_C + which * C_GROUPS * HW + gi * HW) // bw
        if prev:
            return pl.BlockSpec((1, pb, bw), lambda i, n, h: (i, jnp.maximum(n * nlb - 1, 0), base + h))
        return pl.BlockSpec((1, rb, bw), lambda i, n, h: (i, n, base + h))

    o, lse = pl.pallas_call(
        functools.partial(_dilated_kernel, span, dil, hpb, nlb),
        grid=(b, t // rb, HEADS // hpb),
        in_specs=[seg(0, False), seg(1, True), seg(1, False), seg(2, True), seg(2, False),
                  pl.BlockSpec((hpb, CB, 2 * CB), lambda i, n, h: (h, 0, 0))],
        out_specs=[pl.BlockSpec((1, rb, bw), lambda i, n, h: (i, n, h)),
                   pl.BlockSpec((1, rb, bw), lambda i, n, h: (i, n, h))],
        out_shape=[jax.ShapeDtypeStruct((b, t, HW), F32),
                   jax.ShapeDtypeStruct((b, t, HW), F32)],
        compiler_params=_cparams(("parallel", "arbitrary", "arbitrary")),
        name=f"dilated_g{gi}",
    )(proj3, proj3, proj3, proj3, proj3, bias)
    return o.reshape(b * t, HW), lse.reshape(b * t, HW)


def _mix_kernel(x_ref, gate_ref, oa_ref, ob_ref, oc0_ref, oc1_ref, oc2_ref,
                l0_ref, l1_ref, l2_ref, wa_ref, wb_ref, wc_ref, wo_ref, out_ref):
    l0, l1, l2 = l0_ref[...], l1_ref[...], l2_ref[...]
    mx = jnp.maximum(jnp.maximum(l0, l1), l2)
    e0, e1, e2 = jnp.exp(l0 - mx), jnp.exp(l1 - mx), jnp.exp(l2 - mx)
    oc = (e0 * oc0_ref[...] + e1 * oc1_ref[...] + e2 * oc2_ref[...]) / (e0 + e1 + e2)
    mix = (_sigmoid(gate_ref[:, 0:D_MODEL].astype(F32)) * _dot(oa_ref[...], wa_ref[...])
           + _sigmoid(gate_ref[:, D_MODEL:2 * D_MODEL].astype(F32)) * _dot(ob_ref[...], wb_ref[...])
           + _sigmoid(gate_ref[:, 2 * D_MODEL:3 * D_MODEL].astype(F32)) * _dot(oc.astype(BF16), wc_ref[...]))
    out_ref[...] = x_ref[...] + _dot(mix.astype(BF16), wo_ref[...])


def _mix(x2, proj2, oa, ob, ocs, lses, wa, wb, wc, wo, tm=512):
    m = x2.shape[0]
    tm = min(tm, m)
    rowblk = lambda w: pl.BlockSpec((tm, w), lambda i: (i, 0))
    const = lambda shape: pl.BlockSpec(shape, lambda i: (0,) * len(shape))
    return pl.pallas_call(
        _mix_kernel,
        grid=(m // tm,),
        in_specs=[rowblk(D_MODEL), rowblk(3 * D_MODEL), rowblk(HW), rowblk(HW),
                  rowblk(HW), rowblk(HW), rowblk(HW), rowblk(HW), rowblk(HW), rowblk(HW),
                  const((HW, D_MODEL)), const((HW, D_MODEL)), const((HW, D_MODEL)),
                  const((D_MODEL, D_MODEL))],
        out_specs=rowblk(D_MODEL),
        out_shape=jax.ShapeDtypeStruct((m, D_MODEL), F32),
        compiler_params=_cparams(("parallel",)),
        name="branch_mix",
    )(x2, proj2, oa, ob, ocs[0], ocs[1], ocs[2], lses[0], lses[1], lses[2],
      wa.astype(BF16), wb.astype(BF16), wc.astype(BF16), wo.astype(BF16))


def _ffn_kernel(x_ref, g_ref, w1_ref, w3_ref, w2_ref, o_ref):
    x = x_ref[...]
    ms = jnp.mean(x * x, axis=-1, keepdims=True)
    h = (x * lax.rsqrt(ms + EPS) * g_ref[...]).astype(BF16)
    a = _dot(h, w1_ref[...])
    b = _dot(h, w3_ref[...])
    o_ref[...] = x + _dot((_silu(a) * b).astype(BF16), w2_ref[...])


def _ffn(x2, gain, w1, w3, w2, tm=512):
    m = x2.shape[0]
    ff = w1.shape[1]
    tm = min(tm, m)
    resident = lambda shape: pl.BlockSpec(shape, lambda i: (0, 0), pipeline_mode=pl.Buffered(1))
    return pl.pallas_call(
        _ffn_kernel,
        grid=(m // tm,),
        in_specs=[pl.BlockSpec((tm, D_MODEL), lambda i: (i, 0)),
                  pl.BlockSpec((1, D_MODEL), lambda i: (0, 0)),
                  resident((D_MODEL, ff)), resident((D_MODEL, ff)), resident((ff, D_MODEL))],
        out_specs=pl.BlockSpec((tm, D_MODEL), lambda i: (i, 0)),
        out_shape=jax.ShapeDtypeStruct((m, D_MODEL), F32),
        compiler_params=_cparams(("parallel",)),
        name="ffn",
    )(x2, gain.reshape(1, D_MODEL).astype(F32), w1.astype(BF16), w3.astype(BF16), w2.astype(BF16))


HALF = D_MODEL // 2
U32 = jnp.uint32


def _pack_bf16_pairs(x):
    r = x.astype(BF16).astype(F32)
    lo = lax.bitcast_convert_type(r[:, :HALF], U32) >> 16
    hi = lax.bitcast_convert_type(r[:, HALF:], U32)
    return hi | lo


def _unpack_bf16_pairs(p):
    lo = lax.bitcast_convert_type(p << 16, F32)
    hi = lax.bitcast_convert_type(p & U32(0xFFFF0000), F32)
    return lo, hi


def _router_kernel(x_ref, g_ref, wr_ref, h_ref, ids_ref, ps_ref):
    x = x_ref[...]
    ms = jnp.mean(x * x, axis=-1, keepdims=True)
    h = x * lax.rsqrt(ms + EPS) * g_ref[...]
    h_ref[...] = _pack_bf16_pairs(h)
    logits = jnp.dot(h, wr_ref[...], precision=HIGHEST, preferred_element_type=F32)
    lane = lax.broadcasted_iota(jnp.int32, logits.shape, 1)
    neg = jnp.float32(-jnp.inf)
    l1 = jnp.where(lane < N_EXPERTS, logits, neg)
    m1 = jnp.max(l1, axis=-1, keepdims=True)
    i1 = jnp.min(jnp.where(l1 == m1, lane, 128), axis=-1, keepdims=True)
    l2 = jnp.where(lane == i1, neg, l1)
    m2 = jnp.max(l2, axis=-1, keepdims=True)
    i2 = jnp.min(jnp.where(l2 == m2, lane, 128), axis=-1, keepdims=True)
    e = jnp.exp(m2 - m1)
    p1 = 1.0 / (1.0 + e)
    p2 = e / (1.0 + e)
    ids_ref[...] = jnp.where(lane == 0, i1, jnp.where(lane == 1, i2, 0))
    ps_ref[...] = jnp.where(lane == 0, p1, jnp.where(lane == 1, p2, 0.0))


def _router(x2, gain, w_router, tm=1024):
    m = x2.shape[0]
    tm = min(tm, m)
    wr = jnp.pad(w_router.astype(F32), ((0, 0), (0, 128 - N_EXPERTS)))
    return pl.pallas_call(
        _router_kernel,
        grid=(m // tm,),
        in_specs=[pl.BlockSpec((tm, D_MODEL), lambda i: (i, 0)),
                  pl.BlockSpec((1, D_MODEL), lambda i: (0, 0)),
                  pl.BlockSpec((D_MODEL, 128), lambda i: (0, 0))],
        out_specs=[pl.BlockSpec((tm, HALF), lambda i: (i, 0)),
                   pl.BlockSpec((tm, 128), lambda i: (i, 0)),
                   pl.BlockSpec((tm, 128), lambda i: (i, 0))],
        out_shape=[jax.ShapeDtypeStruct((m, HALF), U32),
                   jax.ShapeDtypeStruct((m, 128), jnp.int32),
                   jax.ShapeDtypeStruct((m, 128), F32)],
        compiler_params=_cparams(("parallel",)),
        name="router",
    )(x2, gain.reshape(1, D_MODEL).astype(F32), wr)


def _route_positions(ids, tm):
    m = ids.shape[0]
    e_flat = ids[:, :TOP_K].reshape(-1)
    onehot = (e_flat[:, None] == jnp.arange(N_EXPERTS)[None, :]).astype(jnp.int32)
    csum = jnp.cumsum(onehot, axis=0)
    counts = csum[-1]
    gsz = ((counts + tm - 1) // tm) * tm
    gend = jnp.cumsum(gsz)
    pos = jnp.sum(onehot * (gend - gsz + csum - 1), axis=1)
    nt = (TOP_K * m + N_EXPERTS * tm) // tm
    n_used = gend[-1] // tm
    tile_e = jnp.sum((jnp.arange(nt)[:, None] * tm >= gend[None, :]).astype(jnp.int32), axis=1)
    last_e = jnp.sum(((n_used - 1) * tm >= gend).astype(jnp.int32))
    tile_e = jnp.minimum(tile_e, last_e)
    return pos.astype(jnp.int32), tile_e.astype(jnp.int32), n_used.reshape(1).astype(jnp.int32), nt


def _dispatch_kernel(pos_ref, h_ref, init_ref, xs_ref, sem):
    del init_ref
    td = h_ref.shape[0]

    def row_copy(t, s):
        return pltpu.make_async_copy(h_ref.at[pl.ds(t, 1)],
                                     xs_ref.at[pl.ds(pos_ref[0, 0, TOP_K * t + s], 1)], sem)

    def start(t, c):
        for s in range(TOP_K):
            row_copy(t, s).start()
        return c

    lax.fori_loop(0, td, start, 0, unroll=8)
    for s in range(TOP_K):
        pltpu.make_async_copy(h_ref, xs_ref.at[pl.ds(0, td)], sem).wait()


def _dispatch(h, pos, npad, td=1024):
    m = h.shape[0]
    td = min(td, m)
    return pl.pallas_call(
        _dispatch_kernel,
        grid=(m // td,),
        in_specs=[pl.BlockSpec((1, 1, TOP_K * td), lambda i: (i, 0, 0), memory_space=pltpu.SMEM),
                  pl.BlockSpec((td, HALF), lambda i: (i, 0)),
                  pl.BlockSpec(memory_space=pl.ANY)],
        out_specs=pl.BlockSpec(memory_space=pl.ANY),
        out_shape=jax.ShapeDtypeStruct((npad, HALF), U32),
        scratch_shapes=[pltpu.SemaphoreType.DMA(())],
        input_output_aliases={2: 0},
        compiler_params=_cparams(("arbitrary",)),
        name="moe_dispatch",
    )(pos.reshape(m // td, 1, TOP_K * td), h, jnp.zeros((npad, HALF), U32))


def _experts_kernel(te_ref, nu_ref, xs_ref, w1_ref, w3_ref, w2_ref, y_ref, xb_ref, acc_ref):
    del te_ref
    i = pl.program_id(0)
    j = pl.program_id(1)

    @pl.when(i < nu_ref[0])
    def _():
        @pl.when(j == 0)
        def _():
            lo, hi = _unpack_bf16_pairs(xs_ref[...])
            xb_ref[...] = jnp.concatenate([lo.astype(BF16), hi.astype(BF16)], axis=1)
            acc_ref[...] = jnp.zeros_like(acc_ref)

        xb = xb_ref[...]
        a = _dot(xb, w1_ref[0])
        b = _dot(xb, w3_ref[0])
        acc_ref[...] += _dot((_silu(a) * b).astype(BF16), w2_ref[0])

        @pl.when(j == pl.num_programs(1) - 1)
        def _():
            y_ref[...] = _pack_bf16_pairs(acc_ref[...])

    @pl.when((i >= nu_ref[0]) & (j == 0))
    def _():
        y_ref[...] = jnp.zeros_like(y_ref)


def _experts(xs, tile_e, n_used, w1, w3, w2, tm, tf=1792):
    npad = xs.shape[0]
    ff = w1.shape[2]
    nj = ff // tf
    row = lambda i, j, te, nu: (jnp.minimum(i, nu[0] - 1), 0)
    jj = lambda i, j, nu: jnp.where(i < nu[0], j, nj - 1)
    return pl.pallas_call(
        _experts_kernel,
        grid_spec=pltpu.PrefetchScalarGridSpec(
            num_scalar_prefetch=2,
            grid=(npad // tm, nj),
            in_specs=[pl.BlockSpec((tm, HALF), row),
                      pl.BlockSpec((1, D_MODEL, tf), lambda i, j, te, nu: (te[i], 0, jj(i, j, nu))),
                      pl.BlockSpec((1, D_MODEL, tf), lambda i, j, te, nu: (te[i], 0, jj(i, j, nu))),
                      pl.BlockSpec((1, tf, D_MODEL), lambda i, j, te, nu: (te[i], jj(i, j, nu), 0))],
            out_specs=pl.BlockSpec((tm, HALF), lambda i, j, te, nu: (i, 0)),
            scratch_shapes=[pltpu.VMEM((tm, D_MODEL), BF16), pltpu.VMEM((tm, D_MODEL), F32)]),
        out_shape=jax.ShapeDtypeStruct((npad, HALF), U32),
        compiler_params=_cparams(("arbitrary", "arbitrary")),
        name="moe_experts",
    )(tile_e, n_used, xs, w1.astype(BF16), w3.astype(BF16), w2.astype(BF16))


def _combine_kernel(pos_ref, x_ref, ps_ref, y_ref, o_ref, buf_ref, sem):
    td = x_ref.shape[0]

    def row_copy(t, s):
        return pltpu.make_async_copy(y_ref.at[pl.ds(pos_ref[0, 0, TOP_K * t + s], 1)],
                                     buf_ref.at[s, pl.ds(t, 1)], sem.at[s])

    def start(t, c):
        for s in range(TOP_K):
            row_copy(t, s).start()
        return c

    lax.fori_loop(0, td, start, 0, unroll=8)
    for s in range(TOP_K):
        pltpu.make_async_copy(y_ref.at[pl.ds(0, td)], buf_ref.at[s], sem.at[s]).wait()
    ps = ps_ref[...]
    lo0, hi0 = _unpack_bf16_pairs(buf_ref[0])
    lo1, hi1 = _unpack_bf16_pairs(buf_ref[1])
    o_ref[:, :HALF] = x_ref[:, :HALF] + ps[:, 0:1] * lo0 + ps[:, 1:2] * lo1
    o_ref[:, HALF:] = x_ref[:, HALF:] + ps[:, 0:1] * hi0 + ps[:, 1:2] * hi1


def _combine(x2, ps, pos, y, td=1024):
    m = x2.shape[0]
    td = min(td, m)
    return pl.pallas_call(
        _combine_kernel,
        grid=(m // td,),
        in_specs=[pl.BlockSpec((1, 1, TOP_K * td), lambda i: (i, 0, 0), memory_space=pltpu.SMEM),
                  pl.BlockSpec((td, D_MODEL), lambda i: (i, 0)),
                  pl.BlockSpec((td, 128), lambda i: (i, 0)),
                  pl.BlockSpec(memory_space=pl.ANY)],
        out_specs=pl.BlockSpec((td, D_MODEL), lambda i: (i, 0)),
        out_shape=jax.ShapeDtypeStruct((m, D_MODEL), F32),
        scratch_shapes=[pltpu.VMEM((TOP_K, td, HALF), U32), pltpu.SemaphoreType.DMA((TOP_K,))],
        compiler_params=_cparams(("arbitrary",)),
        name="moe_combine",
    )(pos.reshape(m // td, 1, TOP_K * td), x2, ps, y)


def _moe(x2, gain, w_router, w1, w3, w2, tm=512):
    m = x2.shape[0]
    tm = min(tm, m)
    h, ids, ps = _router(x2, gain, w_router)
    pos, tile_e, n_used, nt = _route_positions(ids, tm)
    xs = _dispatch(h, pos, nt * tm)
    y = _experts(xs, tile_e, n_used, w1, w3, w2, tm)
    return _combine(x2, ps, pos, y)


def _relayout_w_in(w):
    k = w.shape[0]
    main = [w[:, _R_BRG:_R_END], w[:, _R_AQKV:_R_BETA], w[:, _R_AGATE:_R_B], w[:, _R_B:_R_C]]
    second = [w[:, _R_C:_R_BRG], w[:, _R_BETA:_R_AGATE],
              jnp.zeros((k, NP_C - OFF_BA - 2 * HEADS), w.dtype)]
    return (jnp.concatenate(main, axis=1).astype(BF16), jnp.concatenate(second, axis=1).astype(BF16))


def _layer_mixers(x2, b, t, layer, lower_bounds, w_in, norm_mix, conv_a, a_log, dt_bias,
                  gnorm_a, gnorm_b, qnorm_c, knorm_c, rel_bias, w_br_a, w_br_b, w_br_c, w_out):
    w_main, w_second = _relayout_w_in(w_in[layer])
    gain = norm_mix[layer].astype(F32)
    proj2 = _norm_proj(x2, gain, w_main, BF16)
    proj3 = proj2.reshape(b, t, NP_MAIN)
    qk_gain = jnp.concatenate([jnp.tile(qnorm_c[layer].astype(F32), (1, HEADS)).reshape(-1),
                               jnp.tile(knorm_c[layer].astype(F32), (1, HEADS)).reshape(-1)])
    projc3 = _norm_proj(x2, gain, w_second, F32, head_gain=qk_gain).reshape(b, t, NP_C)
    oa = _deltanet(proj3, projc3, conv_a[layer], a_log[layer], dt_bias[layer], gnorm_a[layer])
    ob = _hgrn(proj3, lower_bounds[layer], gnorm_b[layer])
    ocs, lses = [], []
    for gi in range(C_GROUPS):
        o, lse = _dilated_group(projc3, gi, rel_bias)
        ocs.append(o)
        lses.append(lse)
    return _mix(x2, proj2, oa.reshape(b * t, HW), ob.reshape(b * t, HW), ocs, lses,
                w_br_a[layer], w_br_b[layer], w_br_c[layer], w_out[layer])


def kernel(x, w_in, norm_mix, conv_a, a_log, dt_bias, gnorm_a, lb_logits, gnorm_b, qnorm_c, knorm_c, rel_bias, w_br_a, w_br_b, w_br_c, w_out, norm_ffn, ffn_w1, ffn_w3, ffn_w2, router, moe_w1, moe_w3, moe_w2):
    b, t, _ = x.shape
    depth = w_in.shape[0]
    p_lb = jax.nn.softmax(lb_logits.astype(F32), axis=0)
    lower_bounds = jnp.cumsum(p_lb, axis=0) - p_lb[0:1]
    x2 = x.reshape(b * t, D_MODEL).astype(F32)
    for layer in range(depth):
        x2 = _layer_mixers(x2, b, t, layer, lower_bounds, w_in, norm_mix, conv_a, a_log, dt_bias,
                           gnorm_a, gnorm_b, qnorm_c, knorm_c, rel_bias,
                           w_br_a, w_br_b, w_br_c, w_out)
        li = layer // 2
        if layer % 2 == 0:
            x2 = _ffn(x2, norm_ffn[layer], ffn_w1[li], ffn_w3[li], ffn_w2[li])
        else:
            x2 = _moe(x2, norm_ffn[layer], router[li], moe_w1[li], moe_w3[li], moe_w2[li])
    return x2.reshape(b, t, D_MODEL).astype(x.dtype)
```

```python
import functools
import math

import numpy as np
import jax
import jax.numpy as jnp
from jax import lax
from jax.experimental import pallas as pl
from jax.experimental.pallas import tpu as pltpu

F32 = jnp.float32
BF16 = jnp.bfloat16
HIGHEST = lax.Precision.HIGHEST

D_MODEL = 1024
EPS = 1e-6
HEADS = 4
DH = 128
HW = HEADS * DH
A_CONV = 4
A_CHUNK = 64
A_ROWS = 512
B_ROWS = 512
B_BLK = 16
B_GROUP = 8
B_FAST = 32
HGRN_SAFE_DECAY = 60.0
C_PAIRS = ((128, 1), (512, 4), (2048, 16))
C_GROUPS = 3
C_BLOCK = 128
C_TILES = 4
C_NLB = 8
REL_BUCKETS = 32
REL_MAX_DIST = 2048
N_EXPERTS = 8
TOP_K = 2

OFF_BRG = 0
OFF_AQKV = 3072
OFF_AGATE = 4608
OFF_B = 5120
NP_MAIN = 7168
OFF_C = 0
OFF_BA = 4608
NP_C = 5120

_R_AQKV, _R_BETA, _R_AGATE, _R_B, _R_C, _R_BRG, _R_END = 0, 1536, 1544, 2056, 4104, 8712, 11784

VMEM_LIMIT = 56 * 1024 * 1024


def _cparams(sem):
    return pltpu.CompilerParams(dimension_semantics=sem, vmem_limit_bytes=VMEM_LIMIT)


def _sigmoid(x):
    return 1.0 / (1.0 + jnp.exp(-x))


def _silu(x):
    return x * (0.5 * jnp.tanh(0.5 * x) + 0.5)


def _softplus(x):
    return jnp.maximum(x, 0.0) + jnp.log(1.0 + jnp.exp(-jnp.abs(x)))


def _dot(a, b):
    return jnp.dot(a, b, preferred_element_type=F32)


def _dot_nt(a, b, precision=None):
    return lax.dot_general(a, b, (((1,), (1,)), ((), ())), precision=precision,
                           preferred_element_type=F32)


def _dot_tn(a, b):
    return lax.dot_general(a, b, (((0,), (0,)), ((), ())), preferred_element_type=F32)


def _norm_proj_kernel(tn, head_norm_cols, x_ref, g_ref, w_ref, hg_ref, o_ref):
    x = x_ref[...]
    ms = jnp.mean(x * x, axis=-1, keepdims=True)
    h = (x * lax.rsqrt(ms + EPS) * g_ref[...]).astype(BF16)
    for c0 in range(0, w_ref.shape[1], tn):
        r = _dot(h, w_ref[:, c0:c0 + tn])
        if c0 < head_norm_cols:
            heads = [r[:, d0:d0 + DH] for d0 in range(0, tn, DH)]
            heads = [a * lax.rsqrt(jnp.mean(a * a, axis=-1, keepdims=True) + EPS) for a in heads]
            r = jnp.concatenate(heads, axis=1) * hg_ref[:, c0:c0 + tn]
        o_ref[:, c0:c0 + tn] = r.astype(o_ref.dtype)


def _norm_proj(x2, gain, w_bf16, out_dtype, head_gain=None, tm=512, tn=1024):
    m = x2.shape[0]
    n = w_bf16.shape[1]
    tm = min(tm, m)
    norm_cols = 0 if head_gain is None else head_gain.shape[0]
    assert norm_cols % tn == 0
    hg = jnp.zeros((1, n), F32)
    if head_gain is not None:
        hg = hg.at[0, :norm_cols].set(head_gain.astype(F32))
    return pl.pallas_call(
        functools.partial(_norm_proj_kernel, tn, norm_cols),
        grid=(m // tm,),
        in_specs=[pl.BlockSpec((tm, D_MODEL), lambda i: (i, 0)),
                  pl.BlockSpec((1, D_MODEL), lambda i: (0, 0)),
                  pl.BlockSpec((D_MODEL, n), lambda i: (0, 0), pipeline_mode=pl.Buffered(1)),
                  pl.BlockSpec((1, n), lambda i: (0, 0))],
        out_specs=pl.BlockSpec((tm, n), lambda i: (i, 0)),
        out_shape=jax.ShapeDtypeStruct((m, n), out_dtype),
        compiler_params=_cparams(("parallel",)),
        name="norm_proj",
    )(x2, gain.reshape(1, D_MODEL), w_bf16, hg)


def _dotb(a, b):
    return _dot(a.astype(BF16), b.astype(BF16))


def _dot_exact_lhs(a_bf16, b):
    b0 = b.astype(BF16)
    r1 = b - b0.astype(F32)
    b1 = r1.astype(BF16)
    b2 = (r1 - b1.astype(F32)).astype(BF16)
    return _dot(a_bf16, b0) + (_dot(a_bf16, b1) + _dot(a_bf16, b2))


def _deltanet_kernel(qkv_ref, gate_ref, ba_ref, convw_ref, arow_ref, dtrow_ref, gn_ref,
                     o_ref, s_ref, xe_ref):
    C = A_CHUNK
    R = A_ROWS
    HS = range(HEADS)
    IT = range((R // C) * HEADS)

    @pl.when(pl.program_id(1) == 0)
    def _():
        s_ref[...] = jnp.zeros_like(s_ref)
        xe_ref[0:8, :] = jnp.zeros((8, 3 * HW), F32)

    x = qkv_ref[0].astype(F32)
    xe_ref[8:8 + R, :] = x
    w = convw_ref[...]
    y = (w[3:4] * x + w[2:3] * xe_ref[7:7 + R, :] + w[1:2] * xe_ref[6:6 + R, :]
         + w[0:1] * xe_ref[5:5 + R, :])
    xe_ref[0:8, :] = x[R - 8:R]
    y = _silu(y)

    ba = ba_ref[0]
    beta_all = _sigmoid(ba)
    g_all = arow_ref[...] * _softplus(ba + dtrow_ref[...])
    gate = _silu(gate_ref[0].astype(F32))
    s_old = [s_ref[h] for h in HS]

    row = lax.broadcasted_iota(jnp.int32, (C, C), 0)
    col = lax.broadcasted_iota(jnp.int32, (C, C), 1)
    lmat = (col <= row).astype(BF16)
    rowx = lax.broadcasted_iota(jnp.int32, (C, DH + C), 0)
    colx = lax.broadcasted_iota(jnp.int32, (C, DH + C), 1)
    umask = (colx < DH) | (rowx > colx - DH)
    eye = (row == col).astype(F32)
    bd8 = (row >> 3) == (col >> 3)

    def merge_mask(sh):
        return (((row >> (sh + 1)) == (col >> (sh + 1)))
                & (((row >> sh) & 1) == 1) & (((col >> sh) & 1) == 0))

    rows = [slice((i // HEADS) * C, (i // HEADS + 1) * C) for i in IT]
    hd = [i % HEADS for i in IT]
    q = [y[rows[i], hd[i] * DH:(hd[i] + 1) * DH] for i in IT]
    k = [y[rows[i], HW + hd[i] * DH:HW + (hd[i] + 1) * DH] for i in IT]
    v = [y[rows[i], 2 * HW + hd[i] * DH:2 * HW + (hd[i] + 1) * DH] for i in IT]
    q = [a * (lax.rsqrt(jnp.sum(a * a, axis=-1, keepdims=True) + EPS) * (DH ** -0.5)) for a in q]
    k = [a * lax.rsqrt(jnp.sum(a * a, axis=-1, keepdims=True) + EPS) for a in k]
    beta = [beta_all[rows[i], hd[i]:hd[i] + 1] for i in IT]
    gb = [jnp.broadcast_to(g_all[rows[i], HEADS + hd[i]:HEADS + hd[i] + 1], (C, DH + C)) for i in IT]
    dext = [_dot_exact_lhs(lmat, jnp.where(umask, gb[i], 0.0)) for i in IT]
    gc = [d[:, :DH] for d in dext]
    edm = [jnp.exp(d[:, DH:]) for d in dext]
    egc = [jnp.exp(g) for g in gc]
    gl = [g[C - 1:C, :] for g in gc]
    kb = [k[i] * beta[i] for i in IT]
    m = [_dot_nt(kb[i].astype(BF16), k[i].astype(BF16)) * jnp.where(row > col, edm[i], 0.0)
         for i in IT]

    nd = [jnp.where(bd8, -a, 0.0) for a in m]
    p2 = [_dotb(a, a) for a in nd]
    p4 = [_dotb(a, a) for a in p2]
    x1 = [eye + nd[i] for i in IT]
    x1 = [x1[i] + _dotb(x1[i], p2[i]) for i in IT]
    xi = [x1[i] + _dotb(x1[i], p4[i]) for i in IT]
    for sh in (3, 4, 5):
        mm = merge_mask(sh)
        t = [_dotb(xi[i], jnp.where(mm, m[i], 0.0)) for i in IT]
        xi = [xi[i] - _dotb(t[i], xi[i]) for i in IT]

    rhs = [jnp.concatenate([v[i] * beta[i], kb[i] * egc[i]], axis=1) for i in IT]
    sol = [_dotb(xi[i], rhs[i]) for i in IT]
    attn = [(_dot_nt(q[i].astype(BF16), k[i].astype(BF16))
             * jnp.where(row >= col, edm[i], 0.0)).astype(BF16) for i in IT]
    qg = [(q[i] * egc[i]).astype(BF16) for i in IT]
    kg = [(k[i] * jnp.exp(gl[i] - gc[i])).astype(BF16) for i in IT]
    egl = [jnp.exp(a) for a in gl]

    s_cur = s_old
    for c in range(R // C):
        it = [c * HEADS + h for h in HS]
        sb = [a.astype(BF16) for a in s_cur]
        v_new = [(sol[i][:, :DH] - _dot(sol[i][:, DH:].astype(BF16), sb[h])).astype(BF16)
                 for h, i in enumerate(it)]
        o = [_dot(qg[i], sb[h]) + _dot(attn[i], v_new[h]) for h, i in enumerate(it)]
        s_cur = [s_cur[h] * egl[i] + _dot_tn(kg[i], v_new[h]) for h, i in enumerate(it)]
        o = [a * lax.rsqrt(jnp.mean(a * a, axis=-1, keepdims=True) + EPS) * gn_ref[...] for a in o]
        o_ref[0, c * C:(c + 1) * C, :] = (jnp.concatenate(o, axis=1)
                                          * gate[c * C:(c + 1) * C]).astype(o_ref.dtype)
    for h in HS:
        s_ref[h] = s_cur[h]


def _deltanet(proj3, projc3, conv_w, a_log, dt_bias, gnorm):
    b, t, _ = proj3.shape
    C = A_ROWS
    pad = jnp.zeros((HEADS,), F32)
    arow = jnp.concatenate([pad, -jnp.exp(a_log.astype(F32)), jnp.zeros((120,), F32)]).reshape(1, 128)
    dtrow = jnp.concatenate([pad, dt_bias.astype(F32), jnp.zeros((120,), F32)]).reshape(1, 128)
    const = lambda shape: pl.BlockSpec(shape, lambda i, c: (0,) * len(shape))
    return pl.pallas_call(
        _deltanet_kernel,
        grid=(b, t // C),
        in_specs=[pl.BlockSpec((1, C, 3 * HW), lambda i, c: (i, c, OFF_AQKV // (3 * HW))),
                  pl.BlockSpec((1, C, HW), lambda i, c: (i, c, OFF_AGATE // HW)),
                  pl.BlockSpec((1, C, 128), lambda i, c: (i, c, OFF_BA // 128)),
                  const((A_CONV, 3 * HW)), const((1, 128)), const((1, 128)), const((1, DH))],
        out_specs=pl.BlockSpec((1, C, HW), lambda i, c: (i, c, 0)),
        out_shape=jax.ShapeDtypeStruct((b, t, HW), BF16),
        scratch_shapes=[pltpu.VMEM((HEADS, DH, DH), F32), pltpu.VMEM((C + 8, 3 * HW), F32)],
        compiler_params=_cparams(("parallel", "arbitrary")),
        name="deltanet",
    )(proj3, proj3, projc3, conv_w.astype(F32), arow, dtrow, gnorm.reshape(1, DH).astype(F32))


def _hgrn_kernel(q_ref, f_ref, i_ref, g_ref, lb_ref, gn_ref, o_ref, st_ref, lg_ref, k_ref):
    R, K = B_ROWS, B_BLK
    HS = range(HEADS)

    @pl.when(pl.program_id(1) == 0)
    def _():
        st_ref[...] = jnp.zeros_like(st_ref)

    lb_all = lb_ref[...]
    fl = f_ref[0].astype(F32)
    lg = jnp.log(lb_all + (1.0 - lb_all) * _sigmoid(fl))
    k_ref[...] = (1.0 - lb_all) * _sigmoid(-fl)
    lg_ref[...] = lg
    brow = lax.broadcasted_iota(jnp.int32, (R // B_FAST, R), 0)
    bcol = lax.broadcasted_iota(jnp.int32, (R // B_FAST, R), 1)
    chunksum = _dot((bcol // B_FAST == brow).astype(BF16), lg.astype(BF16))
    max_decay = jnp.max(-chunksum)

    H8 = 8
    row8 = lax.broadcasted_iota(jnp.int32, (H8, DH), 0)
    sls = [slice(h * DH, (h + 1) * DH) for h in HS]

    def pairs_exact(q, k, v, gc):
        n_it = range(len(q))
        otile = [[jnp.zeros((H8, DH), F32) for _ in range(K // H8)] for _ in n_it]
        for j in range(K):
            for t in range(j // H8, K // H8):
                ts = slice(t * H8, (t + 1) * H8)
                for i in n_it:
                    d = gc[i][ts] - gc[i][j:j + 1]
                    rel = jnp.exp(jnp.where(row8 >= j - t * H8, d, -1e30) if t == j // H8 else d)
                    sj = jnp.sum(q[i][ts] * k[i][j:j + 1] * rel, axis=-1, keepdims=True)
                    otile[i][t] = otile[i][t] + sj * v[i][j:j + 1]
        return [jnp.concatenate(otile[i], axis=0) for i in n_it]

    def run(exact):
        KB = K if exact else B_FAST
        GR = B_GROUP * KB
        grow = lax.broadcasted_iota(jnp.int32, (GR, GR), 0)
        gcol = lax.broadcasted_iota(jnp.int32, (GR, GR), 1)
        same_block_lower = (gcol <= grow) & (grow // KB == gcol // KB)
        lmat_g = same_block_lower.astype(BF16)
        st = [st_ref[h] for h in HS]
        for r0 in range(0, R, GR):
            gs = slice(r0, r0 + GR)
            blks = [slice(bi * KB, (bi + 1) * KB) for bi in range(B_GROUP)]
            gc_all = _dot_exact_lhs(lmat_g, lg_ref[gs, :])
            gl_all = jnp.concatenate([jnp.broadcast_to(gc_all[bs.stop - 1:bs.stop], (KB, HW)) for bs in blks],
                                     axis=0)
            q = [_silu(q_ref[0, gs, sl].astype(F32)) for sl in sls]
            k = [k_ref[gs, sl] for sl in sls]
            v = [i_ref[0, gs, sl].astype(F32) for sl in sls]
            gc = [gc_all[:, sl] for sl in sls]
            gl = [gl_all[:, sl] for sl in sls]
            kg = [(k[h] * jnp.exp(gl[h] - gc[h])).astype(BF16) for h in HS]
            qs = [(q[h] * jnp.exp(gc[h])).astype(BF16) for h in HS]
            vb = [a.astype(BF16) for a in v]
            upd = [[_dot_tn(vb[h][bs], kg[h][bs]) for h in HS] for bs in blks]
            egl = [[jnp.exp(gl[h][bs.stop - 1:bs.stop]) for h in HS] for bs in blks]
            if exact:
                items = [(h, bs) for bs in blks for h in HS]
                pe = pairs_exact([q[h][bs] for h, bs in items], [k[h][bs] for h, bs in items],
                                 [v[h][bs] for h, bs in items], [gc[h][bs] for h, bs in items])
                intra = [jnp.concatenate([pe[bi * HEADS + h] for bi in range(B_GROUP)], axis=0) for h in HS]
            else:
                qf = [(q[h] * jnp.exp(gc[h] - gl[h])).astype(BF16) for h in HS]
                s = [jnp.where(same_block_lower, _dot_nt(qf[h], kg[h]), 0.0).astype(BF16) for h in HS]
                intra = [_dot(s[h], vb[h]) for h in HS]
            for bi, bs in enumerate(blks):
                o = [_dot_nt(qs[h][bs], st[h].astype(BF16)) + intra[h][bs] for h in HS]
                st = [st[h] * egl[bi][h] + upd[bi][h] for h in HS]
                o = [a * lax.rsqrt(jnp.mean(a * a, axis=-1, keepdims=True) + EPS) * gn_ref[...] for a in o]
                rs = slice(r0 + bi * KB, r0 + (bi + 1) * KB)
                o_ref[0, rs, :] = (jnp.concatenate(o, axis=1)
                                   * _sigmoid(g_ref[0, rs, :].astype(F32))).astype(o_ref.dtype)
        for h in HS:
            st_ref[h] = st[h]

    lax.cond(max_decay > HGRN_SAFE_DECAY, lambda: run(True), lambda: run(False))


def _hgrn(proj3, lb, gnorm):
    b, t, _ = proj3.shape
    R = B_ROWS
    seg = lambda s: pl.BlockSpec((1, R, HW), lambda i, c: (i, c, OFF_B // HW + s))
    const = lambda shape: pl.BlockSpec(shape, lambda i, c: (0,) * len(shape))
    return pl.pallas_call(
        _hgrn_kernel,
        grid=(b, t // R),
        in_specs=[seg(0), seg(1), seg(2), seg(3), const((1, HW)), const((1, DH))],
        out_specs=pl.BlockSpec((1, R, HW), lambda i, c: (i, c, 0)),
        out_shape=jax.ShapeDtypeStruct((b, t, HW), BF16),
        scratch_shapes=[pltpu.VMEM((HEADS, DH, DH), F32), pltpu.VMEM((R, HW), F32),
                        pltpu.VMEM((R, HW), F32)],
        compiler_params=_cparams(("parallel", "arbitrary")),
        name="hgrn2",
    )(proj3, proj3, proj3, proj3, lb.reshape(1, HW).astype(F32), gnorm.reshape(1, DH).astype(F32))


def _t5_bucket_np(n):
    max_exact = REL_BUCKETS // 2
    nf = np.maximum(n, 1).astype(np.float32)
    large = max_exact + (np.log(nf / max_exact) / math.log(REL_MAX_DIST / max_exact)
                         * (REL_BUCKETS - max_exact)).astype(np.int32)
    large = np.minimum(large, REL_BUCKETS - 1)
    return np.where(n < max_exact, n, large)


def _dilated_kernel(span, dil, hpb, nlb, q_ref, kp_ref, kc_ref, vp_ref, vc_ref, bias_ref,
                    o_ref, lse_ref):
    CB = C_BLOCK
    n = pl.program_id(1)
    qi = lax.broadcasted_iota(jnp.int32, (CB, 2 * CB), 0)
    kj = lax.broadcasted_iota(jnp.int32, (CB, 2 * CB), 1)
    dist = qi + CB - kj
    band = (dist >= 0) & (dist <= span)
    band_first = band & ((kj >= CB) | (n > 0))

    def rows(r, jb):
        return pl.ds(jb * CB * dil + r, CB, stride=dil) if dil > 1 else pl.ds(jb * CB, CB)

    def with_prev(p_ref, c_ref, r, jb, sl):
        prev = p_ref[0, rows(r, 0), sl] if jb == 0 else c_ref[0, rows(r, jb - 1), sl]
        return jnp.concatenate([prev, c_ref[0, rows(r, jb), sl]], axis=0)

    def tiles(items):
        n_it = range(len(items))
        q = [q_ref[0, rows(r, jb), sl].astype(BF16) for r, jb, sl, _ in items]
        k = [with_prev(kp_ref, kc_ref, r, jb, sl).astype(BF16) for r, jb, sl, _ in items]
        v = [with_prev(vp_ref, vc_ref, r, jb, sl).astype(BF16) for r, jb, sl, _ in items]
        s = [_dot_nt(q[i], k[i]) * (DH ** -0.5) + bias_ref[items[i][3]] for i in n_it]
        s = [jnp.where(band_first if items[i][1] == 0 else band, s[i], -1e30) for i in n_it]
        mx = [jnp.max(a, axis=-1, keepdims=True) for a in s]
        p = [jnp.exp(s[i] - mx[i]) for i in n_it]
        den = [jnp.sum(a, axis=-1, keepdims=True) for a in p]
        o = [_dot(p[i].astype(BF16), v[i]) / den[i] for i in n_it]
        for i, (r, jb, sl, _) in enumerate(items):
            o_ref[0, rows(r, jb), sl] = o[i]
            lse_ref[0, rows(r, jb), sl] = jnp.broadcast_to(mx[i] + jnp.log(den[i]), (CB, DH))

    for jb in range(nlb):
        if dil == 1:
            for h0 in range(0, hpb, C_TILES):
                tiles([(0, jb, slice(h * DH, (h + 1) * DH), h) for h in range(h0, h0 + C_TILES)])
        elif dil == C_TILES:
            tiles([(r, jb, slice(0, DH), 0) for r in range(dil)])
        else:
            def body(g, carry, jb=jb):
                tiles([(g * C_TILES + i, jb, slice(0, DH), 0) for i in range(C_TILES)])
                return carry
            lax.fori_loop(0, dil // C_TILES, body, 0)


def _dilated_group(proj3, gi, rel_bias):
    b, t, _ = proj3.shape
    window, dil = C_PAIRS[gi]
    span = window // dil
    CB = C_BLOCK
    pb = CB * dil
    nlb = min(C_NLB, t // pb)
    rb = nlb * pb
    hpb = HEADS if dil == 1 else 1
    bw = hpb * DH
    qi = np.arange(CB)[:, None]
    kj = np.arange(2 * CB)[None, :]
    bucket = _t5_bucket_np(np.maximum(qi + CB - kj, 0) * dil)
    onehot = jnp.asarray(np.eye(REL_BUCKETS, dtype=np.float32)[bucket])
    bias = jnp.einsum("qkb,bh->hqk", onehot, rel_bias[:, gi * HEADS:(gi + 1) * HEADS].astype(F32),
                      precision=HIGHEST)

    def seg(which, prev):
        base = (OFF_C + which * C_GROUPS * HW + gi * HW) // bw
        if prev:
            return pl.BlockSpec((1, pb, bw), lambda i, n, h: (i, jnp.maximum(n * nlb - 1, 0), base + h))
        return pl.BlockSpec((1, rb, bw), lambda i, n, h: (i, n, base + h))

    o, lse = pl.pallas_call(
        functools.partial(_dilated_kernel, span, dil, hpb, nlb),
        grid=(b, t // rb, HEADS // hpb),
        in_specs=[seg(0, False), seg(1, True), seg(1, False), seg(2, True), seg(2, False),
                  pl.BlockSpec((hpb, CB, 2 * CB), lambda i, n, h: (h, 0, 0))],
        out_specs=[pl.BlockSpec((1, rb, bw), lambda i, n, h: (i, n, h)),
                   pl.BlockSpec((1, rb, bw), lambda i, n, h: (i, n, h))],
        out_shape=[jax.ShapeDtypeStruct((b, t, HW), F32),
                   jax.ShapeDtypeStruct((b, t, HW), F32)],
        compiler_params=_cparams(("parallel", "arbitrary", "arbitrary")),
        name=f"dilated_g{gi}",
    )(proj3, proj3, proj3, proj3, proj3, bias)
    return o.reshape(b * t, HW), lse.reshape(b * t, HW)


def _mix_kernel(x_ref, gate_ref, oa_ref, ob_ref, oc0_ref, oc1_ref, oc2_ref,
                l0_ref, l1_ref, l2_ref, wa_ref, wb_ref, wc_ref, wo_ref, out_ref):
    l0, l1, l2 = l0_ref[...], l1_ref[...], l2_ref[...]
    mx = jnp.maximum(jnp.maximum(l0, l1), l2)
    e0, e1, e2 = jnp.exp(l0 - mx), jnp.exp(l1 - mx), jnp.exp(l2 - mx)
    oc = (e0 * oc0_ref[...] + e1 * oc1_ref[...] + e2 * oc2_ref[...]) / (e0 + e1 + e2)
    mix = (_sigmoid(gate_ref[:, 0:D_MODEL].astype(F32)) * _dot(oa_ref[...], wa_ref[...])
           + _sigmoid(gate_ref[:, D_MODEL:2 * D_MODEL].astype(F32)) * _dot(ob_ref[...], wb_ref[...])
           + _sigmoid(gate_ref[:, 2 * D_MODEL:3 * D_MODEL].astype(F32)) * _dot(oc.astype(BF16), wc_ref[...]))
    out_ref[...] = x_ref[...] + _dot(mix.astype(BF16), wo_ref[...])


def _mix(x2, proj2, oa, ob, ocs, lses, wa, wb, wc, wo, tm=512):
    m = x2.shape[0]
    tm = min(tm, m)
    rowblk = lambda w: pl.BlockSpec((tm, w), lambda i: (i, 0))
    const = lambda shape: pl.BlockSpec(shape, lambda i: (0,) * len(shape))
    return pl.pallas_call(
        _mix_kernel,
        grid=(m // tm,),
        in_specs=[rowblk(D_MODEL),
                  pl.BlockSpec((tm, 3 * D_MODEL), lambda i: (i, OFF_BRG // (3 * D_MODEL))),
                  rowblk(HW), rowblk(HW),
                  rowblk(HW), rowblk(HW), rowblk(HW), rowblk(HW), rowblk(HW), rowblk(HW),
                  const((HW, D_MODEL)), const((HW, D_MODEL)), const((HW, D_MODEL)),
                  const((D_MODEL, D_MODEL))],
        out_specs=rowblk(D_MODEL),
        out_shape=jax.ShapeDtypeStruct((m, D_MODEL), F32),
        compiler_params=_cparams(("parallel",)),
        name="branch_mix",
    )(x2, proj2, oa, ob, ocs[0], ocs[1], ocs[2], lses[0], lses[1], lses[2],
      wa.astype(BF16), wb.astype(BF16), wc.astype(BF16), wo.astype(BF16))


def _ffn_kernel(x_ref, g_ref, w1_ref, w3_ref, w2_ref, o_ref):
    x = x_ref[...]
    ms = jnp.mean(x * x, axis=-1, keepdims=True)
    h = (x * lax.rsqrt(ms + EPS) * g_ref[...]).astype(BF16)
    a = _dot(h, w1_ref[...])
    b = _dot(h, w3_ref[...])
    o_ref[...] = x + _dot((_silu(a) * b).astype(BF16), w2_ref[...])


def _ffn(x2, gain, w1, w3, w2, tm=512):
    m = x2.shape[0]
    ff = w1.shape[1]
    tm = min(tm, m)
    resident = lambda shape: pl.BlockSpec(shape, lambda i: (0, 0), pipeline_mode=pl.Buffered(1))
    return pl.pallas_call(
        _ffn_kernel,
        grid=(m // tm,),
        in_specs=[pl.BlockSpec((tm, D_MODEL), lambda i: (i, 0)),
                  pl.BlockSpec((1, D_MODEL), lambda i: (0, 0)),
                  resident((D_MODEL, ff)), resident((D_MODEL, ff)), resident((ff, D_MODEL))],
        out_specs=pl.BlockSpec((tm, D_MODEL), lambda i: (i, 0)),
        out_shape=jax.ShapeDtypeStruct((m, D_MODEL), F32),
        compiler_params=_cparams(("parallel",)),
        name="ffn",
    )(x2, gain.reshape(1, D_MODEL).astype(F32), w1.astype(BF16), w3.astype(BF16), w2.astype(BF16))


HALF = D_MODEL // 2
U32 = jnp.uint32


def _pack_bf16_pairs(x):
    r = x.astype(BF16).astype(F32)
    lo = lax.bitcast_convert_type(r[:, :HALF], U32) >> 16
    hi = lax.bitcast_convert_type(r[:, HALF:], U32)
    return hi | lo


def _unpack_bf16_pairs(p):
    lo = lax.bitcast_convert_type(p << 16, F32)
    hi = lax.bitcast_convert_type(p & U32(0xFFFF0000), F32)
    return lo, hi


def _router_kernel(x_ref, g_ref, wr_ref, h_ref, ids_ref, ps_ref):
    x = x_ref[...]
    ms = jnp.mean(x * x, axis=-1, keepdims=True)
    h = x * lax.rsqrt(ms + EPS) * g_ref[...]
    h_ref[...] = _pack_bf16_pairs(h)
    hh = h.astype(BF16)
    hl = (h - hh.astype(F32)).astype(BF16)
    w = wr_ref[...]
    wh = w.astype(BF16)
    wl = (w - wh.astype(F32)).astype(BF16)
    logits = _dot(hh, wh) + (_dot(hh, wl) + _dot(hl, wh))
    lane = lax.broadcasted_iota(jnp.int32, logits.shape, 1)
    neg = jnp.float32(-jnp.inf)
    l1 = jnp.where(lane < N_EXPERTS, logits, neg)
    m1 = jnp.max(l1, axis=-1, keepdims=True)
    i1 = jnp.min(jnp.where(l1 == m1, lane, 128), axis=-1, keepdims=True)
    l2 = jnp.where(lane == i1, neg, l1)
    m2 = jnp.max(l2, axis=-1, keepdims=True)
    i2 = jnp.min(jnp.where(l2 == m2, lane, 128), axis=-1, keepdims=True)
    e = jnp.exp(m2 - m1)
    p1 = 1.0 / (1.0 + e)
    p2 = e / (1.0 + e)
    ids_ref[...] = jnp.where(lane == 0, i1, jnp.where(lane == 1, i2, 0))
    ps_ref[...] = jnp.where(lane == 0, p1, jnp.where(lane == 1, p2, 0.0))


def _router(x2, gain, w_router, tm=1024):
    m = x2.shape[0]
    tm = min(tm, m)
    wr = jnp.pad(w_router.astype(F32), ((0, 0), (0, 128 - N_EXPERTS)))
    return pl.pallas_call(
        _router_kernel,
        grid=(m // tm,),
        in_specs=[pl.BlockSpec((tm, D_MODEL), lambda i: (i, 0)),
                  pl.BlockSpec((1, D_MODEL), lambda i: (0, 0)),
                  pl.BlockSpec((D_MODEL, 128), lambda i: (0, 0))],
        out_specs=[pl.BlockSpec((tm, HALF), lambda i: (i, 0)),
                   pl.BlockSpec((tm, 128), lambda i: (i, 0)),
                   pl.BlockSpec((tm, 128), lambda i: (i, 0))],
        out_shape=[jax.ShapeDtypeStruct((m, HALF), U32),
                   jax.ShapeDtypeStruct((m, 128), jnp.int32),
                   jax.ShapeDtypeStruct((m, 128), F32)],
        compiler_params=_cparams(("parallel",)),
        name="router",
    )(x2, gain.reshape(1, D_MODEL).astype(F32), wr)


def _route_positions(ids, tm):
    m = ids.shape[0]
    e_flat = ids[:, :TOP_K].reshape(-1)
    onehot = (e_flat[:, None] == jnp.arange(N_EXPERTS)[None, :]).astype(jnp.int32)
    csum = jnp.cumsum(onehot, axis=0)
    counts = csum[-1]
    gsz = ((counts + tm - 1) // tm) * tm
    gend = jnp.cumsum(gsz)
    pos = jnp.sum(onehot * (gend - gsz + csum - 1), axis=1)
    nt = (TOP_K * m + N_EXPERTS * tm) // tm
    n_used = gend[-1] // tm
    tile_e = jnp.sum((jnp.arange(nt)[:, None] * tm >= gend[None, :]).astype(jnp.int32), axis=1)
    last_e = jnp.sum(((n_used - 1) * tm >= gend).astype(jnp.int32))
    tile_e = jnp.minimum(tile_e, last_e)
    return pos.astype(jnp.int32), tile_e.astype(jnp.int32), n_used.reshape(1).astype(jnp.int32), nt


def _dispatch_kernel(pos_ref, h_ref, init_ref, xs_ref, sem):
    del init_ref
    td = h_ref.shape[0]

    def row_copy(t, s):
        return pltpu.make_async_copy(h_ref.at[pl.ds(t, 1)],
                                     xs_ref.at[pl.ds(pos_ref[0, 0, TOP_K * t + s], 1)], sem)

    def start(t, c):
        for s in range(TOP_K):
            row_copy(t, s).start()
        return c

    lax.fori_loop(0, td, start, 0, unroll=8)
    for s in range(TOP_K):
        pltpu.make_async_copy(h_ref, xs_ref.at[pl.ds(0, td)], sem).wait()


def _dispatch(h, pos, npad, td=1024):
    m = h.shape[0]
    td = min(td, m)
    return pl.pallas_call(
        _dispatch_kernel,
        grid=(m // td,),
        in_specs=[pl.BlockSpec((1, 1, TOP_K * td), lambda i: (i, 0, 0), memory_space=pltpu.SMEM),
                  pl.BlockSpec((td, HALF), lambda i: (i, 0)),
                  pl.BlockSpec(memory_space=pl.ANY)],
        out_specs=pl.BlockSpec(memory_space=pl.ANY),
        out_shape=jax.ShapeDtypeStruct((npad, HALF), U32),
        scratch_shapes=[pltpu.SemaphoreType.DMA(())],
        input_output_aliases={2: 0},
        compiler_params=_cparams(("arbitrary",)),
        name="moe_dispatch",
    )(pos.reshape(m // td, 1, TOP_K * td), h, jnp.zeros((npad, HALF), U32))


def _experts_kernel(te_ref, nu_ref, xs_ref, w1_ref, w3_ref, w2_ref, y_ref, xb_ref, acc_ref):
    del te_ref
    i = pl.program_id(0)
    j = pl.program_id(1)

    @pl.when(i < nu_ref[0])
    def _():
        @pl.when(j == 0)
        def _():
            lo, hi = _unpack_bf16_pairs(xs_ref[...])
            xb_ref[...] = jnp.concatenate([lo.astype(BF16), hi.astype(BF16)], axis=1)
            acc_ref[...] = jnp.zeros_like(acc_ref)

        xb = xb_ref[...]
        a = _dot(xb, w1_ref[0])
        b = _dot(xb, w3_ref[0])
        acc_ref[...] += _dot((_silu(a) * b).astype(BF16), w2_ref[0])

        @pl.when(j == pl.num_programs(1) - 1)
        def _():
            y_ref[...] = _pack_bf16_pairs(acc_ref[...])

    @pl.when((i >= nu_ref[0]) & (j == 0))
    def _():
        y_ref[...] = jnp.zeros_like(y_ref)


def _experts(xs, tile_e, n_used, w1, w3, w2, tm, tf=1792):
    npad = xs.shape[0]
    ff = w1.shape[2]
    nj = ff // tf
    row = lambda i, j, te, nu: (jnp.minimum(i, nu[0] - 1), 0)
    jj = lambda i, j, nu: jnp.where(i < nu[0], j, nj - 1)
    return pl.pallas_call(
        _experts_kernel,
        grid_spec=pltpu.PrefetchScalarGridSpec(
            num_scalar_prefetch=2,
            grid=(npad // tm, nj),
            in_specs=[pl.BlockSpec((tm, HALF), row),
                      pl.BlockSpec((1, D_MODEL, tf), lambda i, j, te, nu: (te[i], 0, jj(i, j, nu))),
                      pl.BlockSpec((1, D_MODEL, tf), lambda i, j, te, nu: (te[i], 0, jj(i, j, nu))),
                      pl.BlockSpec((1, tf, D_MODEL), lambda i, j, te, nu: (te[i], jj(i, j, nu), 0))],
            out_specs=pl.BlockSpec((tm, HALF), lambda i, j, te, nu: (i, 0)),
            scratch_shapes=[pltpu.VMEM((tm, D_MODEL), BF16), pltpu.VMEM((tm, D_MODEL), F32)]),
        out_shape=jax.ShapeDtypeStruct((npad, HALF), U32),
        compiler_params=_cparams(("arbitrary", "arbitrary")),
        name="moe_experts",
    )(tile_e, n_used, xs, w1.astype(BF16), w3.astype(BF16), w2.astype(BF16))


def _combine_kernel(pos_ref, x_ref, ps_ref, y_ref, o_ref, buf_ref, sem):
    td = x_ref.shape[0]

    def row_copy(t, s):
        return pltpu.make_async_copy(y_ref.at[pl.ds(pos_ref[0, 0, TOP_K * t + s], 1)],
                                     buf_ref.at[s, pl.ds(t, 1)], sem.at[s])

    def start(t, c):
        for s in range(TOP_K):
            row_copy(t, s).start()
        return c

    lax.fori_loop(0, td, start, 0, unroll=8)
    for s in range(TOP_K):
        pltpu.make_async_copy(y_ref.at[pl.ds(0, td)], buf_ref.at[s], sem.at[s]).wait()
    ps = ps_ref[...]
    lo0, hi0 = _unpack_bf16_pairs(buf_ref[0])
    lo1, hi1 = _unpack_bf16_pairs(buf_ref[1])
    o_ref[:, :HALF] = x_ref[:, :HALF] + ps[:, 0:1] * lo0 + ps[:, 1:2] * lo1
    o_ref[:, HALF:] = x_ref[:, HALF:] + ps[:, 0:1] * hi0 + ps[:, 1:2] * hi1


def _combine(x2, ps, pos, y, td=1024):
    m = x2.shape[0]
    td = min(td, m)
    return pl.pallas_call(
        _combine_kernel,
        grid=(m // td,),
        in_specs=[pl.BlockSpec((1, 1, TOP_K * td), lambda i: (i, 0, 0), memory_space=pltpu.SMEM),
                  pl.BlockSpec((td, D_MODEL), lambda i: (i, 0)),
                  pl.BlockSpec((td, 128), lambda i: (i, 0)),
                  pl.BlockSpec(memory_space=pl.ANY)],
        out_specs=pl.BlockSpec((td, D_MODEL), lambda i: (i, 0)),
        out_shape=jax.ShapeDtypeStruct((m, D_MODEL), F32),
        scratch_shapes=[pltpu.VMEM((TOP_K, td, HALF), U32), pltpu.SemaphoreType.DMA((TOP_K,))],
        compiler_params=_cparams(("arbitrary",)),
        name="moe_combine",
    )(pos.reshape(m // td, 1, TOP_K * td), x2, ps, y)


def _moe(x2, gain, w_router, w1, w3, w2, tm=512):
    m = x2.shape[0]
    tm = min(tm, m)
    h, ids, ps = _router(x2, gain, w_router)
    pos, tile_e, n_used, nt = _route_positions(ids, tm)
    xs = _dispatch(h, pos, nt * tm)
    y = _experts(xs, tile_e, n_used, w1, w3, w2, tm)
    return _combine(x2, ps, pos, y)


def _relayout_w_in(w):
    k = w.shape[0]
    main = [w[:, _R_BRG:_R_END], w[:, _R_AQKV:_R_BETA], w[:, _R_AGATE:_R_B], w[:, _R_B:_R_C]]
    second = [w[:, _R_C:_R_BRG], w[:, _R_BETA:_R_AGATE],
              jnp.zeros((k, NP_C - OFF_BA - 2 * HEADS), w.dtype)]
    return (jnp.concatenate(main, axis=1).astype(BF16), jnp.concatenate(second, axis=1).astype(BF16))


def _layer_mixers(x2, b, t, layer, lower_bounds, w_in, norm_mix, conv_a, a_log, dt_bias,
                  gnorm_a, gnorm_b, qnorm_c, knorm_c, rel_bias, w_br_a, w_br_b, w_br_c, w_out):
    w_main, w_second = _relayout_w_in(w_in[layer])
    gain = norm_mix[layer].astype(F32)
    proj2 = _norm_proj(x2, gain, w_main, BF16)
    proj3 = proj2.reshape(b, t, NP_MAIN)
    qk_gain = jnp.concatenate([jnp.tile(qnorm_c[layer].astype(F32), (1, HEADS)).reshape(-1),
                               jnp.tile(knorm_c[layer].astype(F32), (1, HEADS)).reshape(-1)])
    projc3 = _norm_proj(x2, gain, w_second, F32, head_gain=qk_gain).reshape(b, t, NP_C)
    oa = _deltanet(proj3, projc3, conv_a[layer], a_log[layer], dt_bias[layer], gnorm_a[layer])
    ob = _hgrn(proj3, lower_bounds[layer], gnorm_b[layer])
    ocs, lses = [], []
    for gi in range(C_GROUPS):
        o, lse = _dilated_group(projc3, gi, rel_bias)
        ocs.append(o)
        lses.append(lse)
    return _mix(x2, proj2, oa.reshape(b * t, HW), ob.reshape(b * t, HW), ocs, lses,
                w_br_a[layer], w_br_b[layer], w_br_c[layer], w_out[layer])


def kernel(x, w_in, norm_mix, conv_a, a_log, dt_bias, gnorm_a, lb_logits, gnorm_b, qnorm_c, knorm_c, rel_bias, w_br_a, w_br_b, w_br_c, w_out, norm_ffn, ffn_w1, ffn_w3, ffn_w2, router, moe_w1, moe_w3, moe_w2):
    b, t, _ = x.shape
    depth = w_in.shape[0]
    p_lb = jax.nn.softmax(lb_logits.astype(F32), axis=0)
    lower_bounds = jnp.cumsum(p_lb, axis=0) - p_lb[0:1]
    x2 = x.reshape(b * t, D_MODEL).astype(F32)
    for layer in range(depth):
        x2 = _layer_mixers(x2, b, t, layer, lower_bounds, w_in, norm_mix, conv_a, a_log, dt_bias,
                           gnorm_a, gnorm_b, qnorm_c, knorm_c, rel_bias,
                           w_br_a, w_br_b, w_br_c, w_out)
        li = layer // 2
        if layer % 2 == 0:
            x2 = _ffn(x2, norm_ffn[layer], ffn_w1[li], ffn_w3[li], ffn_w2[li])
        else:
            x2 = _moe(x2, norm_ffn[layer], router[li], moe_w1[li], moe_w3[li], moe_w2[li])
    return x2.reshape(b, t, D_MODEL).astype(x.dtype)
```

```python
import functools
import math

import numpy as np
import jax
import jax.numpy as jnp
from jax import lax
from jax.experimental import pallas as pl
from jax.experimental.pallas import tpu as pltpu

F32 = jnp.float32
BF16 = jnp.bfloat16
HIGHEST = lax.Precision.HIGHEST

D_MODEL = 1024
EPS = 1e-6
HEADS = 4
DH = 128
HW = HEADS * DH
A_CONV = 4
A_CHUNK = 64
A_ROWS = 512
B_ROWS = 512
B_BLK = 16
B_GROUP = 8
B_FAST = 32
HGRN_SAFE_DECAY = 60.0
C_PAIRS = ((128, 1), (512, 4), (2048, 16))
C_GROUPS = 3
C_BLOCK = 128
C_TILES = 4
C_LSEG = DH // HEADS
C_NLB = 8
REL_BUCKETS = 32
REL_MAX_DIST = 2048
N_EXPERTS = 8
TOP_K = 2

OFF_BRG = 0
OFF_AQKV = 3072
OFF_AGATE = 4608
OFF_B = 5120
NP_MAIN = 7168
OFF_C = 0
OFF_BA = 4608
NP_C = 5120

_R_AQKV, _R_BETA, _R_AGATE, _R_B, _R_C, _R_BRG, _R_END = 0, 1536, 1544, 2056, 4104, 8712, 11784

VMEM_LIMIT = 56 * 1024 * 1024


def _cparams(sem):
    return pltpu.CompilerParams(dimension_semantics=sem, vmem_limit_bytes=VMEM_LIMIT)


def _sigmoid(x):
    return 1.0 / (1.0 + jnp.exp(-x))


def _silu(x):
    return x * (0.5 * jnp.tanh(0.5 * x) + 0.5)


def _softplus(x):
    return jnp.maximum(x, 0.0) + jnp.log(1.0 + jnp.exp(-jnp.abs(x)))


def _dot(a, b):
    return jnp.dot(a, b, preferred_element_type=F32)


def _dot_nt(a, b, precision=None):
    return lax.dot_general(a, b, (((1,), (1,)), ((), ())), precision=precision,
                           preferred_element_type=F32)


def _dot_tn(a, b):
    return lax.dot_general(a, b, (((0,), (0,)), ((), ())), preferred_element_type=F32)


def _norm_proj_kernel(tn, head_norm_cols, x_ref, g_ref, w_ref, hg_ref, o_ref):
    x = x_ref[...]
    ms = jnp.mean(x * x, axis=-1, keepdims=True)
    h = (x * lax.rsqrt(ms + EPS) * g_ref[...]).astype(BF16)
    for c0 in range(0, w_ref.shape[1], tn):
        r = _dot(h, w_ref[:, c0:c0 + tn])
        if c0 < head_norm_cols:
            heads = [r[:, d0:d0 + DH] for d0 in range(0, tn, DH)]
            heads = [a * lax.rsqrt(jnp.mean(a * a, axis=-1, keepdims=True) + EPS) for a in heads]
            r = jnp.concatenate(heads, axis=1) * hg_ref[:, c0:c0 + tn]
        o_ref[:, c0:c0 + tn] = r.astype(o_ref.dtype)


def _norm_proj(x2, gain, w_bf16, out_dtype, head_gain=None, tm=512, tn=1024):
    m = x2.shape[0]
    n = w_bf16.shape[1]
    tm = min(tm, m)
    norm_cols = 0 if head_gain is None else head_gain.shape[0]
    assert norm_cols % tn == 0
    hg = jnp.zeros((1, n), F32)
    if head_gain is not None:
        hg = hg.at[0, :norm_cols].set(head_gain.astype(F32))
    return pl.pallas_call(
        functools.partial(_norm_proj_kernel, tn, norm_cols),
        grid=(m // tm,),
        in_specs=[pl.BlockSpec((tm, D_MODEL), lambda i: (i, 0)),
                  pl.BlockSpec((1, D_MODEL), lambda i: (0, 0)),
                  pl.BlockSpec((D_MODEL, n), lambda i: (0, 0), pipeline_mode=pl.Buffered(1)),
                  pl.BlockSpec((1, n), lambda i: (0, 0))],
        out_specs=pl.BlockSpec((tm, n), lambda i: (i, 0)),
        out_shape=jax.ShapeDtypeStruct((m, n), out_dtype),
        compiler_params=_cparams(("parallel",)),
        name="norm_proj",
    )(x2, gain.reshape(1, D_MODEL), w_bf16, hg)


def _dotb(a, b):
    return _dot(a.astype(BF16), b.astype(BF16))


def _split3(x):
    x0 = x.astype(BF16)
    r1 = x - x0.astype(F32)
    x1 = r1.astype(BF16)
    return x0, x1, (r1 - x1.astype(F32)).astype(BF16)


def _dot_exact_lhs(a_bf16, b):
    b0, b1, b2 = _split3(b)
    return _dot(a_bf16, b0) + (_dot(a_bf16, b1) + _dot(a_bf16, b2))


def _dot_exact_rhs(a, b_bf16):
    a0, a1, a2 = _split3(a)
    return _dot(a0, b_bf16) + (_dot(a1, b_bf16) + _dot(a2, b_bf16))


def _deltanet_kernel(qkv_ref, gate_ref, ba_ref, convw_ref, arow_ref, dtrow_ref, gn_ref,
                     o_ref, s_ref, xe_ref):
    C = A_CHUNK
    R = A_ROWS
    HS = range(HEADS)
    IT = range((R // C) * HEADS)

    @pl.when(pl.program_id(1) == 0)
    def _():
        s_ref[...] = jnp.zeros_like(s_ref)
        xe_ref[0:8, :] = jnp.zeros((8, 3 * HW), F32)

    x = qkv_ref[0].astype(F32)
    xe_ref[8:8 + R, :] = x
    w = convw_ref[...]
    y = (w[3:4] * x + w[2:3] * xe_ref[7:7 + R, :] + w[1:2] * xe_ref[6:6 + R, :]
         + w[0:1] * xe_ref[5:5 + R, :])
    xe_ref[0:8, :] = x[R - 8:R]
    y = _silu(y)

    ba = ba_ref[0]
    beta_all = _sigmoid(ba)
    g_all = arow_ref[...] * _softplus(ba + dtrow_ref[...])
    gate = _silu(gate_ref[0].astype(F32))
    s_old = [s_ref[h] for h in HS]

    row = lax.broadcasted_iota(jnp.int32, (C, C), 0)
    col = lax.broadcasted_iota(jnp.int32, (C, C), 1)
    lmat = (col <= row).astype(BF16)
    rowx = lax.broadcasted_iota(jnp.int32, (C, DH + C), 0)
    colx = lax.broadcasted_iota(jnp.int32, (C, DH + C), 1)
    umask = (colx < DH) | (rowx > colx - DH)
    eye = (row == col).astype(F32)
    bd8 = (row >> 3) == (col >> 3)

    def merge_mask(sh):
        return (((row >> (sh + 1)) == (col >> (sh + 1)))
                & (((row >> sh) & 1) == 1) & (((col >> sh) & 1) == 0))

    rows = [slice((i // HEADS) * C, (i // HEADS + 1) * C) for i in IT]
    hd = [i % HEADS for i in IT]
    q = [y[rows[i], hd[i] * DH:(hd[i] + 1) * DH] for i in IT]
    k = [y[rows[i], HW + hd[i] * DH:HW + (hd[i] + 1) * DH] for i in IT]
    v = [y[rows[i], 2 * HW + hd[i] * DH:2 * HW + (hd[i] + 1) * DH] for i in IT]
    q = [a * (lax.rsqrt(jnp.sum(a * a, axis=-1, keepdims=True) + EPS) * (DH ** -0.5)) for a in q]
    k = [a * lax.rsqrt(jnp.sum(a * a, axis=-1, keepdims=True) + EPS) for a in k]
    beta = [beta_all[rows[i], hd[i]:hd[i] + 1] for i in IT]
    gb = [jnp.broadcast_to(g_all[rows[i], HEADS + hd[i]:HEADS + hd[i] + 1], (C, DH + C)) for i in IT]
    dext = [_dot_exact_lhs(lmat, jnp.where(umask, gb[i], 0.0)) for i in IT]
    gc = [d[:, :DH] for d in dext]
    edm = [jnp.exp(d[:, DH:]) for d in dext]
    egc = [jnp.exp(g) for g in gc]
    gl = [g[C - 1:C, :] for g in gc]
    kb = [k[i] * beta[i] for i in IT]
    m = [_dot_nt(kb[i].astype(BF16), k[i].astype(BF16)) * jnp.where(row > col, edm[i], 0.0)
         for i in IT]

    nd = [jnp.where(bd8, -a, 0.0) for a in m]
    p2 = [_dotb(a, a) for a in nd]
    p4 = [_dotb(a, a) for a in p2]
    x1 = [eye + nd[i] for i in IT]
    x1 = [x1[i] + _dotb(x1[i], p2[i]) for i in IT]
    xi = [x1[i] + _dotb(x1[i], p4[i]) for i in IT]
    for sh in (3, 4, 5):
        mm = merge_mask(sh)
        t = [_dotb(xi[i], jnp.where(mm, m[i], 0.0)) for i in IT]
        xi = [xi[i] - _dotb(t[i], xi[i]) for i in IT]

    rhs = [jnp.concatenate([v[i] * beta[i], kb[i] * egc[i]], axis=1) for i in IT]
    sol = [_dotb(xi[i], rhs[i]) for i in IT]
    attn = [(_dot_nt(q[i].astype(BF16), k[i].astype(BF16))
             * jnp.where(row >= col, edm[i], 0.0)).astype(BF16) for i in IT]
    qg = [(q[i] * egc[i]).astype(BF16) for i in IT]
    kg = [(k[i] * jnp.exp(gl[i] - gc[i])).astype(BF16) for i in IT]
    egl = [jnp.exp(a) for a in gl]

    s_cur = s_old
    for c in range(R // C):
        it = [c * HEADS + h for h in HS]
        sb = [a.astype(BF16) for a in s_cur]
        v_new = [(sol[i][:, :DH] - _dot(sol[i][:, DH:].astype(BF16), sb[h])).astype(BF16)
                 for h, i in enumerate(it)]
        o = [_dot(qg[i], sb[h]) + _dot(attn[i], v_new[h]) for h, i in enumerate(it)]
        s_cur = [s_cur[h] * egl[i] + _dot_tn(kg[i], v_new[h]) for h, i in enumerate(it)]
        o = [a * lax.rsqrt(jnp.mean(a * a, axis=-1, keepdims=True) + EPS) * gn_ref[...] for a in o]
        o_ref[0, c * C:(c + 1) * C, :] = (jnp.concatenate(o, axis=1)
                                          * gate[c * C:(c + 1) * C]).astype(o_ref.dtype)
    for h in HS:
        s_ref[h] = s_cur[h]


def _deltanet(proj3, projc3, conv_w, a_log, dt_bias, gnorm):
    b, t, _ = proj3.shape
    C = A_ROWS
    pad = jnp.zeros((HEADS,), F32)
    arow = jnp.concatenate([pad, -jnp.exp(a_log.astype(F32)), jnp.zeros((120,), F32)]).reshape(1, 128)
    dtrow = jnp.concatenate([pad, dt_bias.astype(F32), jnp.zeros((120,), F32)]).reshape(1, 128)
    const = lambda shape: pl.BlockSpec(shape, lambda i, c: (0,) * len(shape))
    return pl.pallas_call(
        _deltanet_kernel,
        grid=(b, t // C),
        in_specs=[pl.BlockSpec((1, C, 3 * HW), lambda i, c: (i, c, OFF_AQKV // (3 * HW))),
                  pl.BlockSpec((1, C, HW), lambda i, c: (i, c, OFF_AGATE // HW)),
                  pl.BlockSpec((1, C, 128), lambda i, c: (i, c, OFF_BA // 128)),
                  const((A_CONV, 3 * HW)), const((1, 128)), const((1, 128)), const((1, DH))],
        out_specs=pl.BlockSpec((1, C, HW), lambda i, c: (i, c, 0)),
        out_shape=jax.ShapeDtypeStruct((b, t, HW), BF16),
        scratch_shapes=[pltpu.VMEM((HEADS, DH, DH), F32), pltpu.VMEM((C + 8, 3 * HW), F32)],
        compiler_params=_cparams(("parallel", "arbitrary")),
        name="deltanet",
    )(proj3, proj3, projc3, conv_w.astype(F32), arow, dtrow, gnorm.reshape(1, DH).astype(F32))


def _hgrn_kernel(q_ref, f_ref, i_ref, g_ref, lb_ref, gn_ref, o_ref, st_ref, lg_ref, k_ref):
    R, K = B_ROWS, B_BLK
    HS = range(HEADS)

    @pl.when(pl.program_id(1) == 0)
    def _():
        st_ref[...] = jnp.zeros_like(st_ref)

    lb_all = lb_ref[...]
    fl = f_ref[0].astype(F32)
    lg = jnp.log(lb_all + (1.0 - lb_all) * _sigmoid(fl))
    k_ref[...] = (1.0 - lb_all) * _sigmoid(-fl)
    lg_ref[...] = lg
    brow = lax.broadcasted_iota(jnp.int32, (R // B_FAST, R), 0)
    bcol = lax.broadcasted_iota(jnp.int32, (R // B_FAST, R), 1)
    chunksum = _dot((bcol // B_FAST == brow).astype(BF16), lg.astype(BF16))
    max_decay = jnp.max(-chunksum)

    H8 = 8
    row8 = lax.broadcasted_iota(jnp.int32, (H8, DH), 0)
    sls = [slice(h * DH, (h + 1) * DH) for h in HS]

    def pairs_exact(q, k, v, gc):
        n_it = range(len(q))
        otile = [[jnp.zeros((H8, DH), F32) for _ in range(K // H8)] for _ in n_it]
        for j in range(K):
            for t in range(j // H8, K // H8):
                ts = slice(t * H8, (t + 1) * H8)
                for i in n_it:
                    d = gc[i][ts] - gc[i][j:j + 1]
                    rel = jnp.exp(jnp.where(row8 >= j - t * H8, d, -1e30) if t == j // H8 else d)
                    sj = jnp.sum(q[i][ts] * k[i][j:j + 1] * rel, axis=-1, keepdims=True)
                    otile[i][t] = otile[i][t] + sj * v[i][j:j + 1]
        return [jnp.concatenate(otile[i], axis=0) for i in n_it]

    def run(exact):
        KB = K if exact else B_FAST
        GR = B_GROUP * KB
        grow = lax.broadcasted_iota(jnp.int32, (GR, GR), 0)
        gcol = lax.broadcasted_iota(jnp.int32, (GR, GR), 1)
        same_block_lower = (gcol <= grow) & (grow // KB == gcol // KB)
        lmat_g = same_block_lower.astype(BF16)
        st = [st_ref[h] for h in HS]
        for r0 in range(0, R, GR):
            gs = slice(r0, r0 + GR)
            blks = [slice(bi * KB, (bi + 1) * KB) for bi in range(B_GROUP)]
            gc_all = _dot_exact_lhs(lmat_g, lg_ref[gs, :])
            gl_all = jnp.concatenate([jnp.broadcast_to(gc_all[bs.stop - 1:bs.stop], (KB, HW)) for bs in blks],
                                     axis=0)
            q = [_silu(q_ref[0, gs, sl].astype(F32)) for sl in sls]
            k = [k_ref[gs, sl] for sl in sls]
            v = [i_ref[0, gs, sl].astype(F32) for sl in sls]
            gc = [gc_all[:, sl] for sl in sls]
            gl = [gl_all[:, sl] for sl in sls]
            kg = [(k[h] * jnp.exp(gl[h] - gc[h])).astype(BF16) for h in HS]
            qs = [(q[h] * jnp.exp(gc[h])).astype(BF16) for h in HS]
            vb = [a.astype(BF16) for a in v]
            upd = [[_dot_tn(vb[h][bs], kg[h][bs]) for h in HS] for bs in blks]
            egl = [[jnp.exp(gl[h][bs.stop - 1:bs.stop]) for h in HS] for bs in blks]
            if exact:
                items = [(h, bs) for bs in blks for h in HS]
                pe = pairs_exact([q[h][bs] for h, bs in items], [k[h][bs] for h, bs in items],
                                 [v[h][bs] for h, bs in items], [gc[h][bs] for h, bs in items])
                intra = [jnp.concatenate([pe[bi * HEADS + h] for bi in range(B_GROUP)], axis=0) for h in HS]
            else:
                qf = [(q[h] * jnp.exp(gc[h] - gl[h])).astype(BF16) for h in HS]
                s = [jnp.where(same_block_lower, _dot_nt(qf[h], kg[h]), 0.0).astype(BF16) for h in HS]
                intra = [_dot(s[h], vb[h]) for h in HS]
            for bi, bs in enumerate(blks):
                o = [_dot_nt(qs[h][bs], st[h].astype(BF16)) + intra[h][bs] for h in HS]
                st = [st[h] * egl[bi][h] + upd[bi][h] for h in HS]
                o = [a * lax.rsqrt(jnp.mean(a * a, axis=-1, keepdims=True) + EPS) * gn_ref[...] for a in o]
                rs = slice(r0 + bi * KB, r0 + (bi + 1) * KB)
                o_ref[0, rs, :] = (jnp.concatenate(o, axis=1)
                                   * _sigmoid(g_ref[0, rs, :].astype(F32))).astype(o_ref.dtype)
        for h in HS:
            st_ref[h] = st[h]

    lax.cond(max_decay > HGRN_SAFE_DECAY, lambda: run(True), lambda: run(False))


def _hgrn(proj3, lb, gnorm):
    b, t, _ = proj3.shape
    R = B_ROWS
    seg = lambda s: pl.BlockSpec((1, R, HW), lambda i, c: (i, c, OFF_B // HW + s))
    const = lambda shape: pl.BlockSpec(shape, lambda i, c: (0,) * len(shape))
    return pl.pallas_call(
        _hgrn_kernel,
        grid=(b, t // R),
        in_specs=[seg(0), seg(1), seg(2), seg(3), const((1, HW)), const((1, DH))],
        out_specs=pl.BlockSpec((1, R, HW), lambda i, c: (i, c, 0)),
        out_shape=jax.ShapeDtypeStruct((b, t, HW), BF16),
        scratch_shapes=[pltpu.VMEM((HEADS, DH, DH), F32), pltpu.VMEM((R, HW), F32),
                        pltpu.VMEM((R, HW), F32)],
        compiler_params=_cparams(("parallel", "arbitrary")),
        name="hgrn2",
    )(proj3, proj3, proj3, proj3, lb.reshape(1, HW).astype(F32), gnorm.reshape(1, DH).astype(F32))


def _t5_bucket_np(n):
    max_exact = REL_BUCKETS // 2
    nf = np.maximum(n, 1).astype(np.float32)
    large = max_exact + (np.log(nf / max_exact) / math.log(REL_MAX_DIST / max_exact)
                         * (REL_BUCKETS - max_exact)).astype(np.int32)
    large = np.minimum(large, REL_BUCKETS - 1)
    return np.where(n < max_exact, n, large)


def _dilated_kernel(span, dil, hpb, nlb, q_ref, kp_ref, kc_ref, vp_ref, vc_ref, bias_ref,
                    o_ref, lse_ref):
    CB = C_BLOCK
    n = pl.program_id(1)
    qi = lax.broadcasted_iota(jnp.int32, (CB, 2 * CB), 0)
    kj = lax.broadcasted_iota(jnp.int32, (CB, 2 * CB), 1)
    dist = qi + CB - kj
    band = (dist >= 0) & (dist <= span)
    band_first = band & ((kj >= CB) | (n > 0))
    seg_lane = lax.broadcasted_iota(jnp.int32, (CB, DH), 1)
    if hpb == 1:
        own_seg = seg_lane // C_LSEG == pl.program_id(2)

        @pl.when(pl.program_id(2) == 0)
        def _():
            lse_ref[...] = jnp.zeros_like(lse_ref)
    else:
        assert hpb == C_TILES == HEADS

    def rows(r, jb):
        return pl.ds(jb * CB * dil + r, CB, stride=dil) if dil > 1 else pl.ds(jb * CB, CB)

    def with_prev(p_ref, c_ref, r, jb, sl):
        prev = p_ref[0, rows(r, 0), sl] if jb == 0 else c_ref[0, rows(r, jb - 1), sl]
        return jnp.concatenate([prev, c_ref[0, rows(r, jb), sl]], axis=0)

    def tiles(items):
        n_it = range(len(items))
        q = [q_ref[0, rows(r, jb), sl].astype(BF16) for r, jb, sl, _ in items]
        k = [with_prev(kp_ref, kc_ref, r, jb, sl).astype(BF16) for r, jb, sl, _ in items]
        v = [with_prev(vp_ref, vc_ref, r, jb, sl).astype(BF16) for r, jb, sl, _ in items]
        s = [_dot_nt(q[i], k[i]) * (DH ** -0.5) + bias_ref[items[i][3]] for i in n_it]
        s = [jnp.where(band_first if items[i][1] == 0 else band, s[i], -1e30) for i in n_it]
        mx = [jnp.max(a, axis=-1, keepdims=True) for a in s]
        p = [jnp.exp(s[i] - mx[i]) for i in n_it]
        den = [jnp.sum(a, axis=-1, keepdims=True) for a in p]
        o = [_dot(p[i].astype(BF16), v[i]) / den[i] for i in n_it]
        lse = [mx[i] + jnp.log(den[i]) for i in n_it]
        for i, (r, jb, sl, _) in enumerate(items):
            o_ref[0, rows(r, jb), sl] = o[i]
        if hpb == 1:
            for i, (r, jb, _, _) in enumerate(items):
                lse_ref[0, rows(r, jb), :] = jnp.where(own_seg, lse[i], lse_ref[0, rows(r, jb), :])
        else:
            row = jnp.zeros((CB, DH), F32)
            for i, (_, _, _, h) in enumerate(items):
                row = jnp.where(seg_lane // C_LSEG == h, lse[i], row)
            lse_ref[0, rows(items[0][0], items[0][1]), :] = row

    for jb in range(nlb):
        if dil == 1:
            for h0 in range(0, hpb, C_TILES):
                tiles([(0, jb, slice(h * DH, (h + 1) * DH), h) for h in range(h0, h0 + C_TILES)])
        elif dil == C_TILES:
            tiles([(r, jb, slice(0, DH), 0) for r in range(dil)])
        else:
            def body(g, carry, jb=jb):
                tiles([(g * C_TILES + i, jb, slice(0, DH), 0) for i in range(C_TILES)])
                return carry
            lax.fori_loop(0, dil // C_TILES, body, 0)


def _dilated_group(proj3, gi, rel_bias):
    b, t, _ = proj3.shape
    window, dil = C_PAIRS[gi]
    span = window // dil
    CB = C_BLOCK
    pb = CB * dil
    nlb = min(C_NLB, t // pb)
    rb = nlb * pb
    hpb = HEADS if dil == 1 else 1
    bw = hpb * DH
    qi = np.arange(CB)[:, None]
    kj = np.arange(2 * CB)[None, :]
    bucket = _t5_bucket_np(np.maximum(qi + CB - kj, 0) * dil)
    onehot = jnp.asarray(np.eye(REL_BUCKETS, dtype=np.float32)[bucket])
    bias = jnp.einsum("qkb,bh->hqk", onehot, rel_bias[:, gi * HEADS:(gi + 1) * HEADS].astype(F32),
                      precision=HIGHEST)

    def seg(which, prev):
        base = (OFF_C + which * C_GROUPS * HW + gi * HW) // bw
        if prev:
            return pl.BlockSpec((1, pb, bw), lambda i, n, h: (i, jnp.maximum(n * nlb - 1, 0), base + h))
        return pl.BlockSpec((1, rb, bw), lambda i, n, h: (i, n, base + h))

    o, lse = pl.pallas_call(
        functools.partial(_dilated_kernel, span, dil, hpb, nlb),
        grid=(b, t // rb, HEADS // hpb),
        in_specs=[seg(0, False), seg(1, True), seg(1, False), seg(2, True), seg(2, False),
                  pl.BlockSpec((hpb, CB, 2 * CB), lambda i, n, h: (h, 0, 0))],
        out_specs=[pl.BlockSpec((1, rb, bw), lambda i, n, h: (i, n, h)),
                   pl.BlockSpec((1, rb, DH), lambda i, n, h: (i, n, 0))],
        out_shape=[jax.ShapeDtypeStruct((b, t, HW), F32),
                   jax.ShapeDtypeStruct((b, t, DH), F32)],
        compiler_params=_cparams(("parallel", "arbitrary", "arbitrary")),
        name=f"dilated_g{gi}",
    )(proj3, proj3, proj3, proj3, proj3, bias)
    return o.reshape(b * t, HW), lse.reshape(b * t, DH)


def _mix_kernel(x_ref, gate_ref, oa_ref, ob_ref, oc0_ref, oc1_ref, oc2_ref,
                l0_ref, l1_ref, l2_ref, wa_ref, wb_ref, wc_ref, wo_ref, out_ref):
    l0, l1, l2 = l0_ref[...], l1_ref[...], l2_ref[...]
    mx = jnp.maximum(jnp.maximum(l0, l1), l2)
    e0, e1, e2 = jnp.exp(l0 - mx), jnp.exp(l1 - mx), jnp.exp(l2 - mx)
    inv = 1.0 / (e0 + e1 + e2)
    srow = lax.broadcasted_iota(jnp.int32, (DH, HW), 0)
    scol = lax.broadcasted_iota(jnp.int32, (DH, HW), 1)
    spread = (srow == (scol // DH) * C_LSEG).astype(BF16)
    oc = (_dot_exact_rhs(e0 * inv, spread) * oc0_ref[...] + _dot_exact_rhs(e1 * inv, spread) * oc1_ref[...]
          + _dot_exact_rhs(e2 * inv, spread) * oc2_ref[...])
    mix = (_sigmoid(gate_ref[:, 0:D_MODEL].astype(F32)) * _dot(oa_ref[...], wa_ref[...])
           + _sigmoid(gate_ref[:, D_MODEL:2 * D_MODEL].astype(F32)) * _dot(ob_ref[...], wb_ref[...])
           + _sigmoid(gate_ref[:, 2 * D_MODEL:3 * D_MODEL].astype(F32)) * _dot(oc.astype(BF16), wc_ref[...]))
    out_ref[...] = x_ref[...] + _dot(mix.astype(BF16), wo_ref[...])


def _mix(x2, proj2, oa, ob, ocs, lses, wa, wb, wc, wo, tm=512):
    m = x2.shape[0]
    tm = min(tm, m)
    rowblk = lambda w: pl.BlockSpec((tm, w), lambda i: (i, 0))
    const = lambda shape: pl.BlockSpec(shape, lambda i: (0,) * len(shape))
    return pl.pallas_call(
        _mix_kernel,
        grid=(m // tm,),
        in_specs=[rowblk(D_MODEL),
                  pl.BlockSpec((tm, 3 * D_MODEL), lambda i: (i, OFF_BRG // (3 * D_MODEL))),
                  rowblk(HW), rowblk(HW),
                  rowblk(HW), rowblk(HW), rowblk(HW), rowblk(DH), rowblk(DH), rowblk(DH),
                  const((HW, D_MODEL)), const((HW, D_MODEL)), const((HW, D_MODEL)),
                  const((D_MODEL, D_MODEL))],
        out_specs=rowblk(D_MODEL),
        out_shape=jax.ShapeDtypeStruct((m, D_MODEL), F32),
        compiler_params=_cparams(("parallel",)),
        name="branch_mix",
    )(x2, proj2, oa, ob, ocs[0], ocs[1], ocs[2], lses[0], lses[1], lses[2],
      wa.astype(BF16), wb.astype(BF16), wc.astype(BF16), wo.astype(BF16))


def _ffn_kernel(x_ref, g_ref, w1_ref, w3_ref, w2_ref, o_ref):
    x = x_ref[...]
    ms = jnp.mean(x * x, axis=-1, keepdims=True)
    h = (x * lax.rsqrt(ms + EPS) * g_ref[...]).astype(BF16)
    a = _dot(h, w1_ref[...])
    b = _dot(h, w3_ref[...])
    o_ref[...] = x + _dot((_silu(a) * b).astype(BF16), w2_ref[...])


def _ffn(x2, gain, w1, w3, w2, tm=512):
    m = x2.shape[0]
    ff = w1.shape[1]
    tm = min(tm, m)
    resident = lambda shape: pl.BlockSpec(shape, lambda i: (0, 0), pipeline_mode=pl.Buffered(1))
    return pl.pallas_call(
        _ffn_kernel,
        grid=(m // tm,),
        in_specs=[pl.BlockSpec((tm, D_MODEL), lambda i: (i, 0)),
                  pl.BlockSpec((1, D_MODEL), lambda i: (0, 0)),
                  resident((D_MODEL, ff)), resident((D_MODEL, ff)), resident((ff, D_MODEL))],
        out_specs=pl.BlockSpec((tm, D_MODEL), lambda i: (i, 0)),
        out_shape=jax.ShapeDtypeStruct((m, D_MODEL), F32),
        compiler_params=_cparams(("parallel",)),
        name="ffn",
    )(x2, gain.reshape(1, D_MODEL).astype(F32), w1.astype(BF16), w3.astype(BF16), w2.astype(BF16))


HALF = D_MODEL // 2
U32 = jnp.uint32


def _pack_bf16_pairs(x):
    r = x.astype(BF16).astype(F32)
    lo = lax.bitcast_convert_type(r[:, :HALF], U32) >> 16
    hi = lax.bitcast_convert_type(r[:, HALF:], U32)
    return hi | lo


def _unpack_bf16_pairs(p):
    lo = lax.bitcast_convert_type(p << 16, F32)
    hi = lax.bitcast_convert_type(p & U32(0xFFFF0000), F32)
    return lo, hi


def _router_kernel(x_ref, g_ref, wr_ref, h_ref, ids_ref, ps_ref):
    x = x_ref[...]
    ms = jnp.mean(x * x, axis=-1, keepdims=True)
    h = x * lax.rsqrt(ms + EPS) * g_ref[...]
    h_ref[...] = _pack_bf16_pairs(h)
    hh = h.astype(BF16)
    hl = (h - hh.astype(F32)).astype(BF16)
    w = wr_ref[...]
    wh = w.astype(BF16)
    wl = (w - wh.astype(F32)).astype(BF16)
    logits = _dot(hh, wh) + (_dot(hh, wl) + _dot(hl, wh))
    lane = lax.broadcasted_iota(jnp.int32, logits.shape, 1)
    neg = jnp.float32(-jnp.inf)
    l1 = jnp.where(lane < N_EXPERTS, logits, neg)
    m1 = jnp.max(l1, axis=-1, keepdims=True)
    i1 = jnp.min(jnp.where(l1 == m1, lane, 128), axis=-1, keepdims=True)
    l2 = jnp.where(lane == i1, neg, l1)
    m2 = jnp.max(l2, axis=-1, keepdims=True)
    i2 = jnp.min(jnp.where(l2 == m2, lane, 128), axis=-1, keepdims=True)
    e = jnp.exp(m2 - m1)
    p1 = 1.0 / (1.0 + e)
    p2 = e / (1.0 + e)
    ids_ref[...] = jnp.where(lane == 0, i1, jnp.where(lane == 1, i2, 0))
    ps_ref[...] = jnp.where(lane == 0, p1, jnp.where(lane == 1, p2, 0.0))


def _router(x2, gain, w_router, tm=1024):
    m = x2.shape[0]
    tm = min(tm, m)
    wr = jnp.pad(w_router.astype(F32), ((0, 0), (0, 128 - N_EXPERTS)))
    return pl.pallas_call(
        _router_kernel,
        grid=(m // tm,),
        in_specs=[pl.BlockSpec((tm, D_MODEL), lambda i: (i, 0)),
                  pl.BlockSpec((1, D_MODEL), lambda i: (0, 0)),
                  pl.BlockSpec((D_MODEL, 128), lambda i: (0, 0))],
        out_specs=[pl.BlockSpec((tm, HALF), lambda i: (i, 0)),
                   pl.BlockSpec((tm, 128), lambda i: (i, 0)),
                   pl.BlockSpec((tm, 128), lambda i: (i, 0))],
        out_shape=[jax.ShapeDtypeStruct((m, HALF), U32),
                   jax.ShapeDtypeStruct((m, 128), jnp.int32),
                   jax.ShapeDtypeStruct((m, 128), F32)],
        compiler_params=_cparams(("parallel",)),
        name="router",
    )(x2, gain.reshape(1, D_MODEL).astype(F32), wr)


def _route_positions(ids, tm):
    m = ids.shape[0]
    e_flat = ids[:, :TOP_K].reshape(-1)
    onehot = (e_flat[:, None] == jnp.arange(N_EXPERTS)[None, :]).astype(jnp.int32)
    csum = jnp.cumsum(onehot, axis=0)
    counts = csum[-1]
    gsz = ((counts + tm - 1) // tm) * tm
    gend = jnp.cumsum(gsz)
    pos = jnp.sum(onehot * (gend - gsz + csum - 1), axis=1)
    nt = (TOP_K * m + N_EXPERTS * tm) // tm
    n_used = gend[-1] // tm
    tile_e = jnp.sum((jnp.arange(nt)[:, None] * tm >= gend[None, :]).astype(jnp.int32), axis=1)
    last_e = jnp.sum(((n_used - 1) * tm >= gend).astype(jnp.int32))
    tile_e = jnp.minimum(tile_e, last_e)
    return pos.astype(jnp.int32), tile_e.astype(jnp.int32), n_used.reshape(1).astype(jnp.int32), nt


def _dispatch_kernel(pos_ref, h_ref, init_ref, xs_ref, sem):
    del init_ref
    td = h_ref.shape[0]

    def row_copy(t, s):
        return pltpu.make_async_copy(h_ref.at[pl.ds(t, 1)],
                                     xs_ref.at[pl.ds(pos_ref[0, 0, TOP_K * t + s], 1)], sem)

    def start(t, c):
        for s in range(TOP_K):
            row_copy(t, s).start()
        return c

    lax.fori_loop(0, td, start, 0, unroll=8)
    for s in range(TOP_K):
        pltpu.make_async_copy(h_ref, xs_ref.at[pl.ds(0, td)], sem).wait()


def _dispatch(h, pos, npad, td=1024):
    m = h.shape[0]
    td = min(td, m)
    return pl.pallas_call(
        _dispatch_kernel,
        grid=(m // td,),
        in_specs=[pl.BlockSpec((1, 1, TOP_K * td), lambda i: (i, 0, 0), memory_space=pltpu.SMEM),
                  pl.BlockSpec((td, HALF), lambda i: (i, 0)),
                  pl.BlockSpec(memory_space=pl.ANY)],
        out_specs=pl.BlockSpec(memory_space=pl.ANY),
        out_shape=jax.ShapeDtypeStruct((npad, HALF), U32),
        scratch_shapes=[pltpu.SemaphoreType.DMA(())],
        input_output_aliases={2: 0},
        compiler_params=_cparams(("arbitrary",)),
        name="moe_dispatch",
    )(pos.reshape(m // td, 1, TOP_K * td), h, jnp.zeros((npad, HALF), U32))


def _experts_kernel(te_ref, nu_ref, xs_ref, w1_ref, w3_ref, w2_ref, y_ref, xb_ref, acc_ref):
    del te_ref
    i = pl.program_id(0)
    j = pl.program_id(1)

    @pl.when(i < nu_ref[0])
    def _():
        @pl.when(j == 0)
        def _():
            lo, hi = _unpack_bf16_pairs(xs_ref[...])
            xb_ref[...] = jnp.concatenate([lo.astype(BF16), hi.astype(BF16)], axis=1)
            acc_ref[...] = jnp.zeros_like(acc_ref)

        xb = xb_ref[...]
        a = _dot(xb, w1_ref[0])
        b = _dot(xb, w3_ref[0])
        acc_ref[...] += _dot((_silu(a) * b).astype(BF16), w2_ref[0])

        @pl.when(j == pl.num_programs(1) - 1)
        def _():
            y_ref[...] = _pack_bf16_pairs(acc_ref[...])

    @pl.when((i >= nu_ref[0]) & (j == 0))
    def _():
        y_ref[...] = jnp.zeros_like(y_ref)


def _experts(xs, tile_e, n_used, w1, w3, w2, tm, tf=1792):
    npad = xs.shape[0]
    ff = w1.shape[2]
    nj = ff // tf
    row = lambda i, j, te, nu: (jnp.minimum(i, nu[0] - 1), 0)
    jj = lambda i, j, nu: jnp.where(i < nu[0], j, nj - 1)
    return pl.pallas_call(
        _experts_kernel,
        grid_spec=pltpu.PrefetchScalarGridSpec(
            num_scalar_prefetch=2,
            grid=(npad // tm, nj),
            in_specs=[pl.BlockSpec((tm, HALF), row),
                      pl.BlockSpec((1, D_MODEL, tf), lambda i, j, te, nu: (te[i], 0, jj(i, j, nu))),
                      pl.BlockSpec((1, D_MODEL, tf), lambda i, j, te, nu: (te[i], 0, jj(i, j, nu))),
                      pl.BlockSpec((1, tf, D_MODEL), lambda i, j, te, nu: (te[i], jj(i, j, nu), 0))],
            out_specs=pl.BlockSpec((tm, HALF), lambda i, j, te, nu: (i, 0)),
            scratch_shapes=[pltpu.VMEM((tm, D_MODEL), BF16), pltpu.VMEM((tm, D_MODEL), F32)]),
        out_shape=jax.ShapeDtypeStruct((npad, HALF), U32),
        compiler_params=_cparams(("arbitrary", "arbitrary")),
        name="moe_experts",
    )(tile_e, n_used, xs, w1.astype(BF16), w3.astype(BF16), w2.astype(BF16))


def _combine_kernel(pos_ref, x_ref, ps_ref, y_ref, o_ref, buf_ref, sem):
    td = x_ref.shape[0]

    def row_copy(t, s):
        return pltpu.make_async_copy(y_ref.at[pl.ds(pos_ref[0, 0, TOP_K * t + s], 1)],
                                     buf_ref.at[s, pl.ds(t, 1)], sem.at[s])

    def start(t, c):
        for s in range(TOP_K):
            row_copy(t, s).start()
        return c

    lax.fori_loop(0, td, start, 0, unroll=8)
    for s in range(TOP_K):
        pltpu.make_async_copy(y_ref.at[pl.ds(0, td)], buf_ref.at[s], sem.at[s]).wait()
    ps = ps_ref[...]
    lo0, hi0 = _unpack_bf16_pairs(buf_ref[0])
    lo1, hi1 = _unpack_bf16_pairs(buf_ref[1])
    o_ref[:, :HALF] = x_ref[:, :HALF] + ps[:, 0:1] * lo0 + ps[:, 1:2] * lo1
    o_ref[:, HALF:] = x_ref[:, HALF:] + ps[:, 0:1] * hi0 + ps[:, 1:2] * hi1


def _combine(x2, ps, pos, y, td=1024):
    m = x2.shape[0]
    td = min(td, m)
    return pl.pallas_call(
        _combine_kernel,
        grid=(m // td,),
        in_specs=[pl.BlockSpec((1, 1, TOP_K * td), lambda i: (i, 0, 0), memory_space=pltpu.SMEM),
                  pl.BlockSpec((td, D_MODEL), lambda i: (i, 0)),
                  pl.BlockSpec((td, 128), lambda i: (i, 0)),
                  pl.BlockSpec(memory_space=pl.ANY)],
        out_specs=pl.BlockSpec((td, D_MODEL), lambda i: (i, 0)),
        out_shape=jax.ShapeDtypeStruct((m, D_MODEL), F32),
        scratch_shapes=[pltpu.VMEM((TOP_K, td, HALF), U32), pltpu.SemaphoreType.DMA((TOP_K,))],
        compiler_params=_cparams(("arbitrary",)),
        name="moe_combine",
    )(pos.reshape(m // td, 1, TOP_K * td), x2, ps, y)


def _moe(x2, gain, w_router, w1, w3, w2, tm=512):
    m = x2.shape[0]
    tm = min(tm, m)
    h, ids, ps = _router(x2, gain, w_router)
    pos, tile_e, n_used, nt = _route_positions(ids, tm)
    xs = _dispatch(h, pos, nt * tm)
    y = _experts(xs, tile_e, n_used, w1, w3, w2, tm)
    return _combine(x2, ps, pos, y)


def _relayout_w_in(w):
    k = w.shape[0]
    main = [w[:, _R_BRG:_R_END], w[:, _R_AQKV:_R_BETA], w[:, _R_AGATE:_R_B], w[:, _R_B:_R_C]]
    second = [w[:, _R_C:_R_BRG], w[:, _R_BETA:_R_AGATE],
              jnp.zeros((k, NP_C - OFF_BA - 2 * HEADS), w.dtype)]
    return (jnp.concatenate(main, axis=1).astype(BF16), jnp.concatenate(second, axis=1).astype(BF16))


def _layer_mixers(x2, b, t, layer, lower_bounds, w_in, norm_mix, conv_a, a_log, dt_bias,
                  gnorm_a, gnorm_b, qnorm_c, knorm_c, rel_bias, w_br_a, w_br_b, w_br_c, w_out):
    w_main, w_second = _relayout_w_in(w_in[layer])
    gain = norm_mix[layer].astype(F32)
    proj2 = _norm_proj(x2, gain, w_main, BF16)
    proj3 = proj2.reshape(b, t, NP_MAIN)
    qk_gain = jnp.concatenate([jnp.tile(qnorm_c[layer].astype(F32), (1, HEADS)).reshape(-1),
                               jnp.tile(knorm_c[layer].astype(F32), (1, HEADS)).reshape(-1)])
    projc3 = _norm_proj(x2, gain, w_second, F32, head_gain=qk_gain).reshape(b, t, NP_C)
    oa = _deltanet(proj3, projc3, conv_a[layer], a_log[layer], dt_bias[layer], gnorm_a[layer])
    ob = _hgrn(proj3, lower_bounds[layer], gnorm_b[layer])
    ocs, lses = [], []
    for gi in range(C_GROUPS):
        o, lse = _dilated_group(projc3, gi, rel_bias)
        ocs.append(o)
        lses.append(lse)
    return _mix(x2, proj2, oa.reshape(b * t, HW), ob.reshape(b * t, HW), ocs, lses,
                w_br_a[layer], w_br_b[layer], w_br_c[layer], w_out[layer])


def kernel(x, w_in, norm_mix, conv_a, a_log, dt_bias, gnorm_a, lb_logits, gnorm_b, qnorm_c, knorm_c, rel_bias, w_br_a, w_br_b, w_br_c, w_out, norm_ffn, ffn_w1, ffn_w3, ffn_w2, router, moe_w1, moe_w3, moe_w2):
    b, t, _ = x.shape
    depth = w_in.shape[0]
    p_lb = jax.nn.softmax(lb_logits.astype(F32), axis=0)
    lower_bounds = jnp.cumsum(p_lb, axis=0) - p_lb[0:1]
    x2 = x.reshape(b * t, D_MODEL).astype(F32)
    for layer in range(depth):
        x2 = _layer_mixers(x2, b, t, layer, lower_bounds, w_in, norm_mix, conv_a, a_log, dt_bias,
                           gnorm_a, gnorm_b, qnorm_c, knorm_c, rel_bias,
                           w_br_a, w_br_b, w_br_c, w_out)
        li = layer // 2
        if layer % 2 == 0:
            x2 = _ffn(x2, norm_ffn[layer], ffn_w1[li], ffn_w3[li], ffn_w2[li])
        else:
            x2 = _moe(x2, norm_ffn[layer], router[li], moe_w1[li], moe_w3[li], moe_w2[li])
    return x2.reshape(b, t, D_MODEL).astype(x.dtype)
```

```python
import functools
import math

import numpy as np
import jax
import jax.numpy as jnp
from jax import lax
from jax.experimental import pallas as pl
from jax.experimental.pallas import tpu as pltpu

F32 = jnp.float32
BF16 = jnp.bfloat16
HIGHEST = lax.Precision.HIGHEST

D_MODEL = 1024
EPS = 1e-6
HEADS = 4
DH = 128
HW = HEADS * DH
A_CONV = 4
A_CHUNK = 64
A_ROWS = 512
B_ROWS = 512
B_BLK = 16
B_GROUP = 8
B_FAST = 32
HGRN_SAFE_DECAY = 60.0
C_PAIRS = ((128, 1), (512, 4), (2048, 16))
C_GROUPS = 3
C_BLOCK = 128
C_TILES = 4
C_NLB = 8
REL_BUCKETS = 32
REL_MAX_DIST = 2048
N_EXPERTS = 8
TOP_K = 2

OFF_BRG = 0
OFF_AQKV = 3072
OFF_AGATE = 4608
OFF_B = 5120
NP_MAIN = 7168
OFF_C = 0
OFF_BA = 4608
NP_C = 5120

_R_AQKV, _R_BETA, _R_AGATE, _R_B, _R_C, _R_BRG, _R_END = 0, 1536, 1544, 2056, 4104, 8712, 11784

VMEM_LIMIT = 56 * 1024 * 1024


def _cparams(sem):
    return pltpu.CompilerParams(dimension_semantics=sem, vmem_limit_bytes=VMEM_LIMIT)


def _sigmoid(x):
    return 1.0 / (1.0 + jnp.exp(-x))


def _silu(x):
    return x * (0.5 * jnp.tanh(0.5 * x) + 0.5)


def _softplus(x):
    return jnp.maximum(x, 0.0) + jnp.log(1.0 + jnp.exp(-jnp.abs(x)))


def _dot(a, b):
    return jnp.dot(a, b, preferred_element_type=F32)


def _dot_nt(a, b, precision=None):
    return lax.dot_general(a, b, (((1,), (1,)), ((), ())), precision=precision,
                           preferred_element_type=F32)


def _dot_tn(a, b):
    return lax.dot_general(a, b, (((0,), (0,)), ((), ())), preferred_element_type=F32)


def _norm_proj_kernel(tn, head_norm_cols, x_ref, g_ref, w_ref, hg_ref, o_ref):
    x = x_ref[...]
    ms = jnp.mean(x * x, axis=-1, keepdims=True)
    h = (x * lax.rsqrt(ms + EPS) * g_ref[...]).astype(BF16)
    for c0 in range(0, w_ref.shape[1], tn):
        r = _dot(h, w_ref[:, c0:c0 + tn])
        if c0 < head_norm_cols:
            heads = [r[:, d0:d0 + DH] for d0 in range(0, tn, DH)]
            heads = [a * lax.rsqrt(jnp.mean(a * a, axis=-1, keepdims=True) + EPS) for a in heads]
            r = jnp.concatenate(heads, axis=1) * hg_ref[:, c0:c0 + tn]
        o_ref[:, c0:c0 + tn] = r.astype(o_ref.dtype)


def _norm_proj(x2, gain, w_bf16, out_dtype, head_gain=None, tm=512, tn=1024):
    m = x2.shape[0]
    n = w_bf16.shape[1]
    tm = min(tm, m)
    norm_cols = 0 if head_gain is None else head_gain.shape[0]
    assert norm_cols % tn == 0
    hg = jnp.zeros((1, n), F32)
    if head_gain is not None:
        hg = hg.at[0, :norm_cols].set(head_gain.astype(F32))
    return pl.pallas_call(
        functools.partial(_norm_proj_kernel, tn, norm_cols),
        grid=(m // tm,),
        in_specs=[pl.BlockSpec((tm, D_MODEL), lambda i: (i, 0)),
                  pl.BlockSpec((1, D_MODEL), lambda i: (0, 0)),
                  pl.BlockSpec((D_MODEL, n), lambda i: (0, 0), pipeline_mode=pl.Buffered(1)),
                  pl.BlockSpec((1, n), lambda i: (0, 0))],
        out_specs=pl.BlockSpec((tm, n), lambda i: (i, 0)),
        out_shape=jax.ShapeDtypeStruct((m, n), out_dtype),
        compiler_params=_cparams(("parallel",)),
        name="norm_proj",
    )(x2, gain.reshape(1, D_MODEL), w_bf16, hg)


def _dotb(a, b):
    return _dot(a.astype(BF16), b.astype(BF16))


def _dot_exact_lhs(a_bf16, b):
    b0 = b.astype(BF16)
    r1 = b - b0.astype(F32)
    b1 = r1.astype(BF16)
    b2 = (r1 - b1.astype(F32)).astype(BF16)
    return _dot(a_bf16, b0) + (_dot(a_bf16, b1) + _dot(a_bf16, b2))


def _deltanet_kernel(qkv_ref, gate_ref, ba_ref, convw_ref, arow_ref, dtrow_ref, gn_ref,
                     o_ref, s_ref, xe_ref):
    C = A_CHUNK
    R = A_ROWS
    HS = range(HEADS)
    IT = range((R // C) * HEADS)

    @pl.when(pl.program_id(1) == 0)
    def _():
        s_ref[...] = jnp.zeros_like(s_ref)
        xe_ref[0:8, :] = jnp.zeros((8, 3 * HW), F32)

    x = qkv_ref[0].astype(F32)
    xe_ref[8:8 + R, :] = x
    w = convw_ref[...]
    y = (w[3:4] * x + w[2:3] * xe_ref[7:7 + R, :] + w[1:2] * xe_ref[6:6 + R, :]
         + w[0:1] * xe_ref[5:5 + R, :])
    xe_ref[0:8, :] = x[R - 8:R]
    y = _silu(y)

    ba = ba_ref[0]
    beta_all = _sigmoid(ba)
    g_all = arow_ref[...] * _softplus(ba + dtrow_ref[...])
    gate = _silu(gate_ref[0].astype(F32))
    s_old = [s_ref[h] for h in HS]

    row = lax.broadcasted_iota(jnp.int32, (C, C), 0)
    col = lax.broadcasted_iota(jnp.int32, (C, C), 1)
    lmat = (col <= row).astype(BF16)
    rowx = lax.broadcasted_iota(jnp.int32, (C, DH + C), 0)
    colx = lax.broadcasted_iota(jnp.int32, (C, DH + C), 1)
    umask = (colx < DH) | (rowx > colx - DH)
    eye = (row == col).astype(F32)
    bd8 = (row >> 3) == (col >> 3)

    def merge_mask(sh):
        return (((row >> (sh + 1)) == (col >> (sh + 1)))
                & (((row >> sh) & 1) == 1) & (((col >> sh) & 1) == 0))

    rows = [slice((i // HEADS) * C, (i // HEADS + 1) * C) for i in IT]
    hd = [i % HEADS for i in IT]
    q = [y[rows[i], hd[i] * DH:(hd[i] + 1) * DH] for i in IT]
    k = [y[rows[i], HW + hd[i] * DH:HW + (hd[i] + 1) * DH] for i in IT]
    v = [y[rows[i], 2 * HW + hd[i] * DH:2 * HW + (hd[i] + 1) * DH] for i in IT]
    q = [a * (lax.rsqrt(jnp.sum(a * a, axis=-1, keepdims=True) + EPS) * (DH ** -0.5)) for a in q]
    k = [a * lax.rsqrt(jnp.sum(a * a, axis=-1, keepdims=True) + EPS) for a in k]
    beta = [beta_all[rows[i], hd[i]:hd[i] + 1] for i in IT]
    gb = [jnp.broadcast_to(g_all[rows[i], HEADS + hd[i]:HEADS + hd[i] + 1], (C, DH + C)) for i in IT]
    dext = [_dot_exact_lhs(lmat, jnp.where(umask, gb[i], 0.0)) for i in IT]
    gc = [d[:, :DH] for d in dext]
    edm = [jnp.exp(d[:, DH:]) for d in dext]
    egc = [jnp.exp(g) for g in gc]
    gl = [g[C - 1:C, :] for g in gc]
    kb = [k[i] * beta[i] for i in IT]
    m = [_dot_nt(kb[i].astype(BF16), k[i].astype(BF16)) * jnp.where(row > col, edm[i], 0.0)
         for i in IT]

    nd = [jnp.where(bd8, -a, 0.0) for a in m]
    p2 = [_dotb(a, a) for a in nd]
    p4 = [_dotb(a, a) for a in p2]
    x1 = [eye + nd[i] for i in IT]
    x1 = [x1[i] + _dotb(x1[i], p2[i]) for i in IT]
    xi = [x1[i] + _dotb(x1[i], p4[i]) for i in IT]
    for sh in (3, 4, 5):
        mm = merge_mask(sh)
        t = [_dotb(xi[i], jnp.where(mm, m[i], 0.0)) for i in IT]
        xi = [xi[i] - _dotb(t[i], xi[i]) for i in IT]

    rhs = [jnp.concatenate([v[i] * beta[i], kb[i] * egc[i]], axis=1) for i in IT]
    sol = [_dotb(xi[i], rhs[i]) for i in IT]
    attn = [(_dot_nt(q[i].astype(BF16), k[i].astype(BF16))
             * jnp.where(row >= col, edm[i], 0.0)).astype(BF16) for i in IT]
    qg = [(q[i] * egc[i]).astype(BF16) for i in IT]
    kg = [(k[i] * jnp.exp(gl[i] - gc[i])).astype(BF16) for i in IT]
    egl = [jnp.exp(a) for a in gl]

    s_cur = s_old
    for c in range(R // C):
        it = [c * HEADS + h for h in HS]
        sb = [a.astype(BF16) for a in s_cur]
        v_new = [(sol[i][:, :DH] - _dot(sol[i][:, DH:].astype(BF16), sb[h])).astype(BF16)
                 for h, i in enumerate(it)]
        o = [_dot(qg[i], sb[h]) + _dot(attn[i], v_new[h]) for h, i in enumerate(it)]
        s_cur = [s_cur[h] * egl[i] + _dot_tn(kg[i], v_new[h]) for h, i in enumerate(it)]
        o = [a * lax.rsqrt(jnp.mean(a * a, axis=-1, keepdims=True) + EPS) * gn_ref[...] for a in o]
        o_ref[0, c * C:(c + 1) * C, :] = (jnp.concatenate(o, axis=1)
                                          * gate[c * C:(c + 1) * C]).astype(o_ref.dtype)
    for h in HS:
        s_ref[h] = s_cur[h]


def _deltanet(proj3, projc3, conv_w, a_log, dt_bias, gnorm):
    b, t, _ = proj3.shape
    C = A_ROWS
    pad = jnp.zeros((HEADS,), F32)
    arow = jnp.concatenate([pad, -jnp.exp(a_log.astype(F32)), jnp.zeros((120,), F32)]).reshape(1, 128)
    dtrow = jnp.concatenate([pad, dt_bias.astype(F32), jnp.zeros((120,), F32)]).reshape(1, 128)
    const = lambda shape: pl.BlockSpec(shape, lambda i, c: (0,) * len(shape))
    return pl.pallas_call(
        _deltanet_kernel,
        grid=(b, t // C),
        in_specs=[pl.BlockSpec((1, C, 3 * HW), lambda i, c: (i, c, OFF_AQKV // (3 * HW))),
                  pl.BlockSpec((1, C, HW), lambda i, c: (i, c, OFF_AGATE // HW)),
                  pl.BlockSpec((1, C, 128), lambda i, c: (i, c, OFF_BA // 128)),
                  const((A_CONV, 3 * HW)), const((1, 128)), const((1, 128)), const((1, DH))],
        out_specs=pl.BlockSpec((1, C, HW), lambda i, c: (i, c, 0)),
        out_shape=jax.ShapeDtypeStruct((b, t, HW), BF16),
        scratch_shapes=[pltpu.VMEM((HEADS, DH, DH), F32), pltpu.VMEM((C + 8, 3 * HW), F32)],
        compiler_params=_cparams(("parallel", "arbitrary")),
        name="deltanet",
    )(proj3, proj3, projc3, conv_w.astype(F32), arow, dtrow, gnorm.reshape(1, DH).astype(F32))


def _hgrn_kernel(q_ref, f_ref, i_ref, g_ref, lb_ref, gn_ref, o_ref, st_ref, lg_ref, k_ref):
    R, K = B_ROWS, B_BLK
    HS = range(HEADS)

    @pl.when(pl.program_id(1) == 0)
    def _():
        st_ref[...] = jnp.zeros_like(st_ref)

    lb_all = lb_ref[...]
    fl = f_ref[0].astype(F32)
    lg = jnp.log(lb_all + (1.0 - lb_all) * _sigmoid(fl))
    k_ref[...] = (1.0 - lb_all) * _sigmoid(-fl)
    lg_ref[...] = lg
    brow = lax.broadcasted_iota(jnp.int32, (R // B_FAST, R), 0)
    bcol = lax.broadcasted_iota(jnp.int32, (R // B_FAST, R), 1)
    chunksum = _dot((bcol // B_FAST == brow).astype(BF16), lg.astype(BF16))
    max_decay = jnp.max(-chunksum)

    H8 = 8
    row8 = lax.broadcasted_iota(jnp.int32, (H8, DH), 0)
    sls = [slice(h * DH, (h + 1) * DH) for h in HS]

    def pairs_exact(q, k, v, gc):
        n_it = range(len(q))
        otile = [[jnp.zeros((H8, DH), F32) for _ in range(K // H8)] for _ in n_it]
        for j in range(K):
            for t in range(j // H8, K // H8):
                ts = slice(t * H8, (t + 1) * H8)
                for i in n_it:
                    d = gc[i][ts] - gc[i][j:j + 1]
                    rel = jnp.exp(jnp.where(row8 >= j - t * H8, d, -1e30) if t == j // H8 else d)
                    sj = jnp.sum(q[i][ts] * k[i][j:j + 1] * rel, axis=-1, keepdims=True)
                    otile[i][t] = otile[i][t] + sj * v[i][j:j + 1]
        return [jnp.concatenate(otile[i], axis=0) for i in n_it]

    def run(exact):
        KB = K if exact else B_FAST
        GR = B_GROUP * KB
        grow = lax.broadcasted_iota(jnp.int32, (GR, GR), 0)
        gcol = lax.broadcasted_iota(jnp.int32, (GR, GR), 1)
        same_block_lower = (gcol <= grow) & (grow // KB == gcol // KB)
        lmat_g = same_block_lower.astype(BF16)
        st = [st_ref[h] for h in HS]
        for r0 in range(0, R, GR):
            gs = slice(r0, r0 + GR)
            blks = [slice(bi * KB, (bi + 1) * KB) for bi in range(B_GROUP)]
            gc_all = _dot_exact_lhs(lmat_g, lg_ref[gs, :])
            gl_all = jnp.concatenate([jnp.broadcast_to(gc_all[bs.stop - 1:bs.stop], (KB, HW)) for bs in blks],
                                     axis=0)
            q = [_silu(q_ref[0, gs, sl].astype(F32)) for sl in sls]
            k = [k_ref[gs, sl] for sl in sls]
            v = [i_ref[0, gs, sl].astype(F32) for sl in sls]
            gc = [gc_all[:, sl] for sl in sls]
            gl = [gl_all[:, sl] for sl in sls]
            kg = [(k[h] * jnp.exp(gl[h] - gc[h])).astype(BF16) for h in HS]
            qs = [(q[h] * jnp.exp(gc[h])).astype(BF16) for h in HS]
            vb = [a.astype(BF16) for a in v]
            upd = [[_dot_tn(vb[h][bs], kg[h][bs]) for h in HS] for bs in blks]
            egl = [[jnp.exp(gl[h][bs.stop - 1:bs.stop]) for h in HS] for bs in blks]
            if exact:
                items = [(h, bs) for bs in blks for h in HS]
                pe = pairs_exact([q[h][bs] for h, bs in items], [k[h][bs] for h, bs in items],
                                 [v[h][bs] for h, bs in items], [gc[h][bs] for h, bs in items])
                intra = [jnp.concatenate([pe[bi * HEADS + h] for bi in range(B_GROUP)], axis=0) for h in HS]
            else:
                qf = [(q[h] * jnp.exp(gc[h] - gl[h])).astype(BF16) for h in HS]
                s = [jnp.where(same_block_lower, _dot_nt(qf[h], kg[h]), 0.0).astype(BF16) for h in HS]
                intra = [_dot(s[h], vb[h]) for h in HS]
            for bi, bs in enumerate(blks):
                o = [_dot_nt(qs[h][bs], st[h].astype(BF16)) + intra[h][bs] for h in HS]
                st = [st[h] * egl[bi][h] + upd[bi][h] for h in HS]
                o = [a * lax.rsqrt(jnp.mean(a * a, axis=-1, keepdims=True) + EPS) * gn_ref[...] for a in o]
                rs = slice(r0 + bi * KB, r0 + (bi + 1) * KB)
                o_ref[0, rs, :] = (jnp.concatenate(o, axis=1)
                                   * _sigmoid(g_ref[0, rs, :].astype(F32))).astype(o_ref.dtype)
        for h in HS:
            st_ref[h] = st[h]

    lax.cond(max_decay > HGRN_SAFE_DECAY, lambda: run(True), lambda: run(False))


def _hgrn(proj3, lb, gnorm):
    b, t, _ = proj3.shape
    R = B_ROWS
    seg = lambda s: pl.BlockSpec((1, R, HW), lambda i, c: (i, c, OFF_B // HW + s))
    const = lambda shape: pl.BlockSpec(shape, lambda i, c: (0,) * len(shape))
    return pl.pallas_call(
        _hgrn_kernel,
        grid=(b, t // R),
        in_specs=[seg(0), seg(1), seg(2), seg(3), const((1, HW)), const((1, DH))],
        out_specs=pl.BlockSpec((1, R, HW), lambda i, c: (i, c, 0)),
        out_shape=jax.ShapeDtypeStruct((b, t, HW), BF16),
        scratch_shapes=[pltpu.VMEM((HEADS, DH, DH), F32), pltpu.VMEM((R, HW), F32),
                        pltpu.VMEM((R, HW), F32)],
        compiler_params=_cparams(("parallel", "arbitrary")),
        name="hgrn2",
    )(proj3, proj3, proj3, proj3, lb.reshape(1, HW).astype(F32), gnorm.reshape(1, DH).astype(F32))


def _t5_bucket_np(n):
    max_exact = REL_BUCKETS // 2
    nf = np.maximum(n, 1).astype(np.float32)
    large = max_exact + (np.log(nf / max_exact) / math.log(REL_MAX_DIST / max_exact)
                         * (REL_BUCKETS - max_exact)).astype(np.int32)
    large = np.minimum(large, REL_BUCKETS - 1)
    return np.where(n < max_exact, n, large)


def _dilated_kernel(span, dil, hpb, nlb, q_ref, kp_ref, kc_ref, vp_ref, vc_ref, bias_ref,
                    o_ref, lse_ref):
    CB = C_BLOCK
    n = pl.program_id(1)
    qi = lax.broadcasted_iota(jnp.int32, (CB, 2 * CB), 0)
    kj = lax.broadcasted_iota(jnp.int32, (CB, 2 * CB), 1)
    dist = qi + CB - kj
    band = (dist >= 0) & (dist <= span)
    band_first = band & ((kj >= CB) | (n > 0))

    def rows(r, jb):
        return pl.ds(jb * CB * dil + r, CB, stride=dil) if dil > 1 else pl.ds(jb * CB, CB)

    def with_prev(p_ref, c_ref, r, jb, sl):
        prev = p_ref[0, rows(r, 0), sl] if jb == 0 else c_ref[0, rows(r, jb - 1), sl]
        return jnp.concatenate([prev, c_ref[0, rows(r, jb), sl]], axis=0)

    def tiles(items):
        n_it = range(len(items))
        q = [q_ref[0, rows(r, jb), sl].astype(BF16) for r, jb, sl, _ in items]
        k = [with_prev(kp_ref, kc_ref, r, jb, sl).astype(BF16) for r, jb, sl, _ in items]
        v = [with_prev(vp_ref, vc_ref, r, jb, sl).astype(BF16) for r, jb, sl, _ in items]
        s = [_dot_nt(q[i], k[i]) * (DH ** -0.5) + bias_ref[items[i][3]] for i in n_it]
        s = [jnp.where(band_first if items[i][1] == 0 else band, s[i], -1e30) for i in n_it]
        mx = [jnp.max(a, axis=-1, keepdims=True) for a in s]
        p = [jnp.exp(s[i] - mx[i]) for i in n_it]
        den = [jnp.sum(a, axis=-1, keepdims=True) for a in p]
        o = [_dot(p[i].astype(BF16), v[i]) / den[i] for i in n_it]
        for i, (r, jb, sl, _) in enumerate(items):
            o_ref[0, rows(r, jb), sl] = o[i]
            lse_ref[0, rows(r, jb), sl] = jnp.broadcast_to(mx[i] + jnp.log(den[i]), (CB, DH))

    for jb in range(nlb):
        if dil == 1:
            for h0 in range(0, hpb, C_TILES):
                tiles([(0, jb, slice(h * DH, (h + 1) * DH), h) for h in range(h0, h0 + C_TILES)])
        elif dil == C_TILES:
            tiles([(r, jb, slice(0, DH), 0) for r in range(dil)])
        else:
            def body(g, carry, jb=jb):
                tiles([(g * C_TILES + i, jb, slice(0, DH), 0) for i in range(C_TILES)])
                return carry
            lax.fori_loop(0, dil // C_TILES, body, 0)


def _dilated_group(proj3, gi, rel_bias):
    b, t, _ = proj3.shape
    window, dil = C_PAIRS[gi]
    span = window // dil
    CB = C_BLOCK
    pb = CB * dil
    nlb = min(C_NLB, t // pb)
    rb = nlb * pb
    hpb = HEADS if dil == 1 else 1
    bw = hpb * DH
    qi = np.arange(CB)[:, None]
    kj = np.arange(2 * CB)[None, :]
    bucket = _t5_bucket_np(np.maximum(qi + CB - kj, 0) * dil)
    onehot = jnp.asarray(np.eye(REL_BUCKETS, dtype=np.float32)[bucket])
    bias = jnp.einsum("qkb,bh->hqk", onehot, rel_bias[:, gi * HEADS:(gi + 1) * HEADS].astype(F32),
                      precision=HIGHEST)

    def seg(which, prev):
        base = (OFF_C + which * C_GROUPS * HW + gi * HW) // bw
        if prev:
            return pl.BlockSpec((1, pb, bw), lambda i, n, h: (i, jnp.maximum(n * nlb - 1, 0), base + h))
        return pl.BlockSpec((1, rb, bw), lambda i, n, h: (i, n, base + h))

    o, lse = pl.pallas_call(
        functools.partial(_dilated_kernel, span, dil, hpb, nlb),
        grid=(b, t // rb, HEADS // hpb),
        in_specs=[seg(0, False), seg(1, True), seg(1, False), seg(2, True), seg(2, False),
                  pl.BlockSpec((hpb, CB, 2 * CB), lambda i, n, h: (h, 0, 0))],
        out_specs=[pl.BlockSpec((1, rb, bw), lambda i, n, h: (i, n, h)),
                   pl.BlockSpec((1, rb, bw), lambda i, n, h: (i, n, h))],
        out_shape=[jax.ShapeDtypeStruct((b, t, HW), F32),
                   jax.ShapeDtypeStruct((b, t, HW), F32)],
        compiler_params=_cparams(("parallel", "arbitrary", "arbitrary")),
        name=f"dilated_g{gi}",
    )(proj3, proj3, proj3, proj3, proj3, bias)
    return o.reshape(b * t, HW), lse.reshape(b * t, HW)


def _mix_kernel(x_ref, gate_ref, oa_ref, ob_ref, oc0_ref, oc1_ref, oc2_ref,
                l0_ref, l1_ref, l2_ref, wa_ref, wb_ref, wc_ref, wo_ref, out_ref):
    l0, l1, l2 = l0_ref[...], l1_ref[...], l2_ref[...]
    mx = jnp.maximum(jnp.maximum(l0, l1), l2)
    e0, e1, e2 = jnp.exp(l0 - mx), jnp.exp(l1 - mx), jnp.exp(l2 - mx)
    oc = (e0 * oc0_ref[...] + e1 * oc1_ref[...] + e2 * oc2_ref[...]) / (e0 + e1 + e2)
    mix = (_sigmoid(gate_ref[:, 0:D_MODEL].astype(F32)) * _dot(oa_ref[...], wa_ref[...])
           + _sigmoid(gate_ref[:, D_MODEL:2 * D_MODEL].astype(F32)) * _dot(ob_ref[...], wb_ref[...])
           + _sigmoid(gate_ref[:, 2 * D_MODEL:3 * D_MODEL].astype(F32)) * _dot(oc.astype(BF16), wc_ref[...]))
    out_ref[...] = x_ref[...] + _dot(mix.astype(BF16), wo_ref[...])


def _mix(x2, proj2, oa, ob, ocs, lses, wa, wb, wc, wo, tm=512):
    m = x2.shape[0]
    tm = min(tm, m)
    rowblk = lambda w: pl.BlockSpec((tm, w), lambda i: (i, 0))
    const = lambda shape: pl.BlockSpec(shape, lambda i: (0,) * len(shape))
    return pl.pallas_call(
        _mix_kernel,
        grid=(m // tm,),
        in_specs=[rowblk(D_MODEL),
                  pl.BlockSpec((tm, 3 * D_MODEL), lambda i: (i, OFF_BRG // (3 * D_MODEL))),
                  rowblk(HW), rowblk(HW),
                  rowblk(HW), rowblk(HW), rowblk(HW), rowblk(HW), rowblk(HW), rowblk(HW),
                  const((HW, D_MODEL)), const((HW, D_MODEL)), const((HW, D_MODEL)),
                  const((D_MODEL, D_MODEL))],
        out_specs=rowblk(D_MODEL),
        out_shape=jax.ShapeDtypeStruct((m, D_MODEL), F32),
        compiler_params=_cparams(("parallel",)),
        name="branch_mix",
    )(x2, proj2, oa, ob, ocs[0], ocs[1], ocs[2], lses[0], lses[1], lses[2],
      wa.astype(BF16), wb.astype(BF16), wc.astype(BF16), wo.astype(BF16))


def _ffn_kernel(x_ref, g_ref, w1_ref, w3_ref, w2_ref, o_ref):
    x = x_ref[...]
    ms = jnp.mean(x * x, axis=-1, keepdims=True)
    h = (x * lax.rsqrt(ms + EPS) * g_ref[...]).astype(BF16)
    a = _dot(h, w1_ref[...])
    b = _dot(h, w3_ref[...])
    o_ref[...] = x + _dot((_silu(a) * b).astype(BF16), w2_ref[...])


def _ffn(x2, gain, w1, w3, w2, tm=512):
    m = x2.shape[0]
    ff = w1.shape[1]
    tm = min(tm, m)
    resident = lambda shape: pl.BlockSpec(shape, lambda i: (0, 0), pipeline_mode=pl.Buffered(1))
    return pl.pallas_call(
        _ffn_kernel,
        grid=(m // tm,),
        in_specs=[pl.BlockSpec((tm, D_MODEL), lambda i: (i, 0)),
                  pl.BlockSpec((1, D_MODEL), lambda i: (0, 0)),
                  resident((D_MODEL, ff)), resident((D_MODEL, ff)), resident((ff, D_MODEL))],
        out_specs=pl.BlockSpec((tm, D_MODEL), lambda i: (i, 0)),
        out_shape=jax.ShapeDtypeStruct((m, D_MODEL), F32),
        compiler_params=_cparams(("parallel",)),
        name="ffn",
    )(x2, gain.reshape(1, D_MODEL).astype(F32), w1.astype(BF16), w3.astype(BF16), w2.astype(BF16))


HALF = D_MODEL // 2
U32 = jnp.uint32


def _pack_bf16_pairs(x):
    r = x.astype(BF16).astype(F32)
    lo = lax.bitcast_convert_type(r[:, :HALF], U32) >> 16
    hi = lax.bitcast_convert_type(r[:, HALF:], U32)
    return hi | lo


def _unpack_bf16_pairs(p):
    lo = lax.bitcast_convert_type(p << 16, F32)
    hi = lax.bitcast_convert_type(p & U32(0xFFFF0000), F32)
    return lo, hi


def _router_kernel(x_ref, g_ref, wr_ref, h_ref, ids_ref, ps_ref):
    x = x_ref[...]
    ms = jnp.mean(x * x, axis=-1, keepdims=True)
    h = x * lax.rsqrt(ms + EPS) * g_ref[...]
    h_ref[...] = _pack_bf16_pairs(h)
    hh = h.astype(BF16)
    hl = (h - hh.astype(F32)).astype(BF16)
    w = wr_ref[...]
    wh = w.astype(BF16)
    wl = (w - wh.astype(F32)).astype(BF16)
    logits = _dot(hh, wh) + (_dot(hh, wl) + _dot(hl, wh))
    lane = lax.broadcasted_iota(jnp.int32, logits.shape, 1)
    neg = jnp.float32(-jnp.inf)
    l1 = jnp.where(lane < N_EXPERTS, logits, neg)
    m1 = jnp.max(l1, axis=-1, keepdims=True)
    i1 = jnp.min(jnp.where(l1 == m1, lane, 128), axis=-1, keepdims=True)
    l2 = jnp.where(lane == i1, neg, l1)
    m2 = jnp.max(l2, axis=-1, keepdims=True)
    i2 = jnp.min(jnp.where(l2 == m2, lane, 128), axis=-1, keepdims=True)
    e = jnp.exp(m2 - m1)
    p1 = 1.0 / (1.0 + e)
    p2 = e / (1.0 + e)
    ids_ref[...] = jnp.where(lane == 0, i1, jnp.where(lane == 1, i2, 0))
    ps_ref[...] = jnp.where(lane == 0, p1, jnp.where(lane == 1, p2, 0.0))


def _router(x2, gain, w_router, tm=1024):
    m = x2.shape[0]
    tm = min(tm, m)
    wr = jnp.pad(w_router.astype(F32), ((0, 0), (0, 128 - N_EXPERTS)))
    return pl.pallas_call(
        _router_kernel,
        grid=(m // tm,),
        in_specs=[pl.BlockSpec((tm, D_MODEL), lambda i: (i, 0)),
                  pl.BlockSpec((1, D_MODEL), lambda i: (0, 0)),
                  pl.BlockSpec((D_MODEL, 128), lambda i: (0, 0))],
        out_specs=[pl.BlockSpec((tm, HALF), lambda i: (i, 0)),
                   pl.BlockSpec((tm, 128), lambda i: (i, 0)),
                   pl.BlockSpec((tm, 128), lambda i: (i, 0))],
        out_shape=[jax.ShapeDtypeStruct((m, HALF), U32),
                   jax.ShapeDtypeStruct((m, 128), jnp.int32),
                   jax.ShapeDtypeStruct((m, 128), F32)],
        compiler_params=_cparams(("parallel",)),
        name="router",
    )(x2, gain.reshape(1, D_MODEL).astype(F32), wr)


def _route_positions(ids, tm):
    m = ids.shape[0]
    e_flat = ids[:, :TOP_K].reshape(-1)
    onehot = (e_flat[:, None] == jnp.arange(N_EXPERTS)[None, :]).astype(jnp.int32)
    csum = jnp.cumsum(onehot, axis=0)
    counts = csum[-1]
    gsz = ((counts + tm - 1) // tm) * tm
    gend = jnp.cumsum(gsz)
    pos = jnp.sum(onehot * (gend - gsz + csum - 1), axis=1)
    nt = (TOP_K * m + N_EXPERTS * tm) // tm
    n_used = gend[-1] // tm
    tile_e = jnp.sum((jnp.arange(nt)[:, None] * tm >= gend[None, :]).astype(jnp.int32), axis=1)
    last_e = jnp.sum(((n_used - 1) * tm >= gend).astype(jnp.int32))
    tile_e = jnp.minimum(tile_e, last_e)
    return pos.astype(jnp.int32), tile_e.astype(jnp.int32), n_used.reshape(1).astype(jnp.int32), nt


def _dispatch_kernel(pos_ref, h_ref, init_ref, xs_ref, sem):
    del init_ref
    td = h_ref.shape[0]

    def row_copy(t, s):
        return pltpu.make_async_copy(h_ref.at[pl.ds(t, 1)],
                                     xs_ref.at[pl.ds(pos_ref[0, 0, TOP_K * t + s], 1)], sem)

    def start(t, c):
        for s in range(TOP_K):
            row_copy(t, s).start()
        return c

    lax.fori_loop(0, td, start, 0, unroll=8)
    for s in range(TOP_K):
        pltpu.make_async_copy(h_ref, xs_ref.at[pl.ds(0, td)], sem).wait()


def _dispatch(h, pos, npad, td=1024):
    m = h.shape[0]
    td = min(td, m)
    return pl.pallas_call(
        _dispatch_kernel,
        grid=(m // td,),
        in_specs=[pl.BlockSpec((1, 1, TOP_K * td), lambda i: (i, 0, 0), memory_space=pltpu.SMEM),
                  pl.BlockSpec((td, HALF), lambda i: (i, 0)),
                  pl.BlockSpec(memory_space=pl.ANY)],
        out_specs=pl.BlockSpec(memory_space=pl.ANY),
        out_shape=jax.ShapeDtypeStruct((npad, HALF), U32),
        scratch_shapes=[pltpu.SemaphoreType.DMA(())],
        input_output_aliases={2: 0},
        compiler_params=_cparams(("arbitrary",)),
        name="moe_dispatch",
    )(pos.reshape(m // td, 1, TOP_K * td), h, jnp.zeros((npad, HALF), U32))


def _experts_kernel(te_ref, nu_ref, xs_ref, w1_ref, w3_ref, w2_ref, y_ref, xb_ref, acc_ref):
    del te_ref
    i = pl.program_id(0)
    j = pl.program_id(1)

    @pl.when(i < nu_ref[0])
    def _():
        @pl.when(j == 0)
        def _():
            lo, hi = _unpack_bf16_pairs(xs_ref[...])
            xb_ref[...] = jnp.concatenate([lo.astype(BF16), hi.astype(BF16)], axis=1)
            acc_ref[...] = jnp.zeros_like(acc_ref)

        xb = xb_ref[...]
        a = _dot(xb, w1_ref[0])
        b = _dot(xb, w3_ref[0])
        acc_ref[...] += _dot((_silu(a) * b).astype(BF16), w2_ref[0])

        @pl.when(j == pl.num_programs(1) - 1)
        def _():
            y_ref[...] = _pack_bf16_pairs(acc_ref[...])

    @pl.when((i >= nu_ref[0]) & (j == 0))
    def _():
        y_ref[...] = jnp.zeros_like(y_ref)


def _experts(xs, tile_e, n_used, w1, w3, w2, tm, tf=1792):
    npad = xs.shape[0]
    ff = w1.shape[2]
    nj = ff // tf
    row = lambda i, j, te, nu: (jnp.minimum(i, nu[0] - 1), 0)
    jj = lambda i, j, nu: jnp.where(i < nu[0], j, nj - 1)
    return pl.pallas_call(
        _experts_kernel,
        grid_spec=pltpu.PrefetchScalarGridSpec(
            num_scalar_prefetch=2,
            grid=(npad // tm, nj),
            in_specs=[pl.BlockSpec((tm, HALF), row),
                      pl.BlockSpec((1, D_MODEL, tf), lambda i, j, te, nu: (te[i], 0, jj(i, j, nu))),
                      pl.BlockSpec((1, D_MODEL, tf), lambda i, j, te, nu: (te[i], 0, jj(i, j, nu))),
                      pl.BlockSpec((1, tf, D_MODEL), lambda i, j, te, nu: (te[i], jj(i, j, nu), 0))],
            out_specs=pl.BlockSpec((tm, HALF), lambda i, j, te, nu: (i, 0)),
            scratch_shapes=[pltpu.VMEM((tm, D_MODEL), BF16), pltpu.VMEM((tm, D_MODEL), F32)]),
        out_shape=jax.ShapeDtypeStruct((npad, HALF), U32),
        compiler_params=_cparams(("arbitrary", "arbitrary")),
        name="moe_experts",
    )(tile_e, n_used, xs, w1.astype(BF16), w3.astype(BF16), w2.astype(BF16))


def _combine_kernel(pos_ref, x_ref, ps_ref, y_ref, o_ref, buf_ref, sem):
    td = x_ref.shape[0]

    def row_copy(t, s):
        return pltpu.make_async_copy(y_ref.at[pl.ds(pos_ref[0, 0, TOP_K * t + s], 1)],
                                     buf_ref.at[s, pl.ds(t, 1)], sem.at[s])

    def start(t, c):
        for s in range(TOP_K):
            row_copy(t, s).start()
        return c

    lax.fori_loop(0, td, start, 0, unroll=8)
    for s in range(TOP_K):
        pltpu.make_async_copy(y_ref.at[pl.ds(0, td)], buf_ref.at[s], sem.at[s]).wait()
    ps = ps_ref[...]
    lo0, hi0 = _unpack_bf16_pairs(buf_ref[0])
    lo1, hi1 = _unpack_bf16_pairs(buf_ref[1])
    o_ref[:, :HALF] = x_ref[:, :HALF] + ps[:, 0:1] * lo0 + ps[:, 1:2] * lo1
    o_ref[:, HALF:] = x_ref[:, HALF:] + ps[:, 0:1] * hi0 + ps[:, 1:2] * hi1


def _combine(x2, ps, pos, y, td=1024):
    m = x2.shape[0]
    td = min(td, m)
    return pl.pallas_call(
        _combine_kernel,
        grid=(m // td,),
        in_specs=[pl.BlockSpec((1, 1, TOP_K * td), lambda i: (i, 0, 0), memory_space=pltpu.SMEM),
                  pl.BlockSpec((td, D_MODEL), lambda i: (i, 0)),
                  pl.BlockSpec((td, 128), lambda i: (i, 0)),
                  pl.BlockSpec(memory_space=pl.ANY)],
        out_specs=pl.BlockSpec((td, D_MODEL), lambda i: (i, 0)),
        out_shape=jax.ShapeDtypeStruct((m, D_MODEL), F32),
        scratch_shapes=[pltpu.VMEM((TOP_K, td, HALF), U32), pltpu.SemaphoreType.DMA((TOP_K,))],
        compiler_params=_cparams(("arbitrary",)),
        name="moe_combine",
    )(pos.reshape(m // td, 1, TOP_K * td), x2, ps, y)


def _moe(x2, gain, w_router, w1, w3, w2, tm=1024):
    m = x2.shape[0]
    tm = min(tm, m)
    h, ids, ps = _router(x2, gain, w_router)
    pos, tile_e, n_used, nt = _route_positions(ids, tm)
    xs = _dispatch(h, pos, nt * tm)
    y = _experts(xs, tile_e, n_used, w1, w3, w2, tm)
    return _combine(x2, ps, pos, y)


def _relayout_w_in(w):
    k = w.shape[0]
    main = [w[:, _R_BRG:_R_END], w[:, _R_AQKV:_R_BETA], w[:, _R_AGATE:_R_B], w[:, _R_B:_R_C]]
    second = [w[:, _R_C:_R_BRG], w[:, _R_BETA:_R_AGATE],
              jnp.zeros((k, NP_C - OFF_BA - 2 * HEADS), w.dtype)]
    return (jnp.concatenate(main, axis=1).astype(BF16), jnp.concatenate(second, axis=1).astype(BF16))


def _layer_mixers(x2, b, t, layer, lower_bounds, w_in, norm_mix, conv_a, a_log, dt_bias,
                  gnorm_a, gnorm_b, qnorm_c, knorm_c, rel_bias, w_br_a, w_br_b, w_br_c, w_out):
    w_main, w_second = _relayout_w_in(w_in[layer])
    gain = norm_mix[layer].astype(F32)
    proj2 = _norm_proj(x2, gain, w_main, BF16)
    proj3 = proj2.reshape(b, t, NP_MAIN)
    qk_gain = jnp.concatenate([jnp.tile(qnorm_c[layer].astype(F32), (1, HEADS)).reshape(-1),
                               jnp.tile(knorm_c[layer].astype(F32), (1, HEADS)).reshape(-1)])
    projc3 = _norm_proj(x2, gain, w_second, F32, head_gain=qk_gain).reshape(b, t, NP_C)
    oa = _deltanet(proj3, projc3, conv_a[layer], a_log[layer], dt_bias[layer], gnorm_a[layer])
    ob = _hgrn(proj3, lower_bounds[layer], gnorm_b[layer])
    ocs, lses = [], []
    for gi in range(C_GROUPS):
        o, lse = _dilated_group(projc3, gi, rel_bias)
        ocs.append(o)
        lses.append(lse)
    return _mix(x2, proj2, oa.reshape(b * t, HW), ob.reshape(b * t, HW), ocs, lses,
                w_br_a[layer], w_br_b[layer], w_br_c[layer], w_out[layer])


def kernel(x, w_in, norm_mix, conv_a, a_log, dt_bias, gnorm_a, lb_logits, gnorm_b, qnorm_c, knorm_c, rel_bias, w_br_a, w_br_b, w_br_c, w_out, norm_ffn, ffn_w1, ffn_w3, ffn_w2, router, moe_w1, moe_w3, moe_w2):
    b, t, _ = x.shape
    depth = w_in.shape[0]
    p_lb = jax.nn.softmax(lb_logits.astype(F32), axis=0)
    lower_bounds = jnp.cumsum(p_lb, axis=0) - p_lb[0:1]
    x2 = x.reshape(b * t, D_MODEL).astype(F32)
    for layer in range(depth):
        x2 = _layer_mixers(x2, b, t, layer, lower_bounds, w_in, norm_mix, conv_a, a_log, dt_bias,
                           gnorm_a, gnorm_b, qnorm_c, knorm_c, rel_bias,
                           w_br_a, w_br_b, w_br_c, w_out)
        li = layer // 2
        if layer % 2 == 0:
            x2 = _ffn(x2, norm_ffn[layer], ffn_w1[li], ffn_w3[li], ffn_w2[li])
        else:
            x2 = _moe(x2, norm_ffn[layer], router[li], moe_w1[li], moe_w3[li], moe_w2[li])
    return x2.reshape(b, t, D_MODEL).astype(x.dtype)
```

```python
import functools
import math

import numpy as np
import jax
import jax.numpy as jnp
from jax import lax
from jax.experimental import pallas as pl
from jax.experimental.pallas import tpu as pltpu

F32 = jnp.float32
BF16 = jnp.bfloat16
HIGHEST = lax.Precision.HIGHEST

LANES = 128
SUBLANES = 8

D_MODEL = 1024
EPS = 1e-6
HEADS = 4
DH = 128
HW = HEADS * DH
A_CONV = 4
A_CHUNK = 64
A_ROWS = 512
B_ROWS = 512
B_BLK = 16
B_GROUP = 8
B_FAST = 32
HGRN_SAFE_DECAY = 60.0
C_PAIRS = ((128, 1), (512, 4), (2048, 16))
C_GROUPS = 3
C_BLOCK = 128
C_TILES = 4
C_NLB = 8
REL_BUCKETS = 32
REL_MAX_DIST = 2048
N_EXPERTS = 8
TOP_K = 2

OFF_BRG = 0
OFF_AQKV = 3072
OFF_AGATE = 4608
OFF_B = 5120
NP_MAIN = 7168
OFF_C = 0
OFF_BA = 4608
NP_C = 5120

_R_AQKV, _R_BETA, _R_AGATE, _R_B, _R_C, _R_BRG, _R_END = 0, 1536, 1544, 2056, 4104, 8712, 11784

VMEM_LIMIT = 56 * 1024 * 1024


def _cparams(sem):
    return pltpu.CompilerParams(dimension_semantics=sem, vmem_limit_bytes=VMEM_LIMIT)


def _sigmoid(x):
    return 1.0 / (1.0 + jnp.exp(-x))


def _silu(x):
    return x * (0.5 * jnp.tanh(0.5 * x) + 0.5)


def _softplus(x):
    return jnp.maximum(x, 0.0) + jnp.log(1.0 + jnp.exp(-jnp.abs(x)))


def _dot(a, b):
    return jnp.dot(a, b, preferred_element_type=F32)


def _dot_nt(a, b, precision=None):
    return lax.dot_general(a, b, (((1,), (1,)), ((), ())), precision=precision,
                           preferred_element_type=F32)


def _dot_tn(a, b):
    return lax.dot_general(a, b, (((0,), (0,)), ((), ())), preferred_element_type=F32)


def _norm_proj_kernel(tn, head_norm_cols, x_ref, g_ref, w_ref, hg_ref, o_ref):
    x = x_ref[...]
    ms = jnp.mean(x * x, axis=-1, keepdims=True)
    h = (x * lax.rsqrt(ms + EPS) * g_ref[...]).astype(BF16)
    for c0 in range(0, w_ref.shape[1], tn):
        r = _dot(h, w_ref[:, c0:c0 + tn])
        if c0 < head_norm_cols:
            heads = [r[:, d0:d0 + DH] for d0 in range(0, tn, DH)]
            heads = [a * lax.rsqrt(jnp.mean(a * a, axis=-1, keepdims=True) + EPS) for a in heads]
            r = jnp.concatenate(heads, axis=1) * hg_ref[:, c0:c0 + tn]
        o_ref[:, c0:c0 + tn] = r.astype(o_ref.dtype)


def _norm_proj(x2, gain, w_bf16, out_dtype, head_gain=None, tm=512, tn=1024):
    m = x2.shape[0]
    n = w_bf16.shape[1]
    tm = min(tm, m)
    norm_cols = 0 if head_gain is None else head_gain.shape[0]
    assert norm_cols % tn == 0
    hg = jnp.zeros((1, n), F32)
    if head_gain is not None:
        hg = hg.at[0, :norm_cols].set(head_gain.astype(F32))
    return pl.pallas_call(
        functools.partial(_norm_proj_kernel, tn, norm_cols),
        grid=(m // tm,),
        in_specs=[pl.BlockSpec((tm, D_MODEL), lambda i: (i, 0)),
                  pl.BlockSpec((1, D_MODEL), lambda i: (0, 0)),
                  pl.BlockSpec((D_MODEL, n), lambda i: (0, 0), pipeline_mode=pl.Buffered(1)),
                  pl.BlockSpec((1, n), lambda i: (0, 0))],
        out_specs=pl.BlockSpec((tm, n), lambda i: (i, 0)),
        out_shape=jax.ShapeDtypeStruct((m, n), out_dtype),
        compiler_params=_cparams(("parallel",)),
        name="norm_proj",
    )(x2, gain.reshape(1, D_MODEL), w_bf16, hg)


def _dotb(a, b):
    return _dot(a.astype(BF16), b.astype(BF16))


def _dot_exact_lhs(a_bf16, b):
    b0 = b.astype(BF16)
    r1 = b - b0.astype(F32)
    b1 = r1.astype(BF16)
    b2 = (r1 - b1.astype(F32)).astype(BF16)
    return _dot(a_bf16, b0) + (_dot(a_bf16, b1) + _dot(a_bf16, b2))


def _deltanet_kernel(qkv_ref, gate_ref, ba_ref, convw_ref, arow_ref, dtrow_ref, gn_ref,
                     o_ref, s_ref, xe_ref):
    C = A_CHUNK
    R = A_ROWS
    HS = range(HEADS)
    IT = range((R // C) * HEADS)

    @pl.when(pl.program_id(1) == 0)
    def _():
        s_ref[...] = jnp.zeros_like(s_ref)
        xe_ref[0:SUBLANES, :] = jnp.zeros((SUBLANES, 3 * HW), F32)

    x = qkv_ref[0].astype(F32)
    xe_ref[SUBLANES:SUBLANES + R, :] = x
    w = convw_ref[...]
    y = w[A_CONV - 1:A_CONV] * x
    for d in range(1, A_CONV):
        y = y + w[A_CONV - 1 - d:A_CONV - d] * xe_ref[SUBLANES - d:SUBLANES - d + R, :]
    xe_ref[0:SUBLANES, :] = x[R - SUBLANES:R]
    y = _silu(y)

    ba = ba_ref[0]
    beta_all = _sigmoid(ba)
    g_all = arow_ref[...] * _softplus(ba + dtrow_ref[...])
    gate = _silu(gate_ref[0].astype(F32))
    s_old = [s_ref[h] for h in HS]

    row = lax.broadcasted_iota(jnp.int32, (C, C), 0)
    col = lax.broadcasted_iota(jnp.int32, (C, C), 1)
    lmat = (col <= row).astype(BF16)
    rowx = lax.broadcasted_iota(jnp.int32, (C, DH + C), 0)
    colx = lax.broadcasted_iota(jnp.int32, (C, DH + C), 1)
    umask = (colx < DH) | (rowx > colx - DH)
    eye = (row == col).astype(F32)
    bd8 = (row >> 3) == (col >> 3)

    def merge_mask(sh):
        return (((row >> (sh + 1)) == (col >> (sh + 1)))
                & (((row >> sh) & 1) == 1) & (((col >> sh) & 1) == 0))

    rows = [slice((i // HEADS) * C, (i // HEADS + 1) * C) for i in IT]
    hd = [i % HEADS for i in IT]
    q = [y[rows[i], hd[i] * DH:(hd[i] + 1) * DH] for i in IT]
    k = [y[rows[i], HW + hd[i] * DH:HW + (hd[i] + 1) * DH] for i in IT]
    v = [y[rows[i], 2 * HW + hd[i] * DH:2 * HW + (hd[i] + 1) * DH] for i in IT]
    q = [a * (lax.rsqrt(jnp.sum(a * a, axis=-1, keepdims=True) + EPS) * (DH ** -0.5)) for a in q]
    k = [a * lax.rsqrt(jnp.sum(a * a, axis=-1, keepdims=True) + EPS) for a in k]
    beta = [beta_all[rows[i], hd[i]:hd[i] + 1] for i in IT]
    gb = [jnp.broadcast_to(g_all[rows[i], HEADS + hd[i]:HEADS + hd[i] + 1], (C, DH + C)) for i in IT]
    dext = [_dot_exact_lhs(lmat, jnp.where(umask, gb[i], 0.0)) for i in IT]
    gc = [d[:, :DH] for d in dext]
    edm = [jnp.exp(d[:, DH:]) for d in dext]
    egc = [jnp.exp(g) for g in gc]
    gl = [g[C - 1:C, :] for g in gc]
    kb = [k[i] * beta[i] for i in IT]
    m = [_dot_nt(kb[i].astype(BF16), k[i].astype(BF16)) * jnp.where(row > col, edm[i], 0.0)
         for i in IT]

    nd = [jnp.where(bd8, -a, 0.0) for a in m]
    p2 = [_dotb(a, a) for a in nd]
    p4 = [_dotb(a, a) for a in p2]
    x1 = [eye + nd[i] for i in IT]
    x1 = [x1[i] + _dotb(x1[i], p2[i]) for i in IT]
    xi = [x1[i] + _dotb(x1[i], p4[i]) for i in IT]
    for sh in (3, 4, 5):
        mm = merge_mask(sh)
        t = [_dotb(xi[i], jnp.where(mm, m[i], 0.0)) for i in IT]
        xi = [xi[i] - _dotb(t[i], xi[i]) for i in IT]

    rhs = [jnp.concatenate([v[i] * beta[i], kb[i] * egc[i]], axis=1) for i in IT]
    sol = [_dotb(xi[i], rhs[i]) for i in IT]
    attn = [(_dot_nt(q[i].astype(BF16), k[i].astype(BF16))
             * jnp.where(row >= col, edm[i], 0.0)).astype(BF16) for i in IT]
    qg = [(q[i] * egc[i]).astype(BF16) for i in IT]
    kg = [(k[i] * jnp.exp(gl[i] - gc[i])).astype(BF16) for i in IT]
    egl = [jnp.exp(a) for a in gl]

    s_cur = s_old
    for c in range(R // C):
        it = [c * HEADS + h for h in HS]
        sb = [a.astype(BF16) for a in s_cur]
        v_new = [(sol[i][:, :DH] - _dot(sol[i][:, DH:].astype(BF16), sb[h])).astype(BF16)
                 for h, i in enumerate(it)]
        o = [_dot(qg[i], sb[h]) + _dot(attn[i], v_new[h]) for h, i in enumerate(it)]
        s_cur = [s_cur[h] * egl[i] + _dot_tn(kg[i], v_new[h]) for h, i in enumerate(it)]
        o = [a * lax.rsqrt(jnp.mean(a * a, axis=-1, keepdims=True) + EPS) * gn_ref[...] for a in o]
        o_ref[0, c * C:(c + 1) * C, :] = (jnp.concatenate(o, axis=1)
                                          * gate[c * C:(c + 1) * C]).astype(o_ref.dtype)
    for h in HS:
        s_ref[h] = s_cur[h]


def _deltanet(proj3, projc3, conv_w, a_log, dt_bias, gnorm):
    b, t, _ = proj3.shape
    C = A_ROWS
    lane_row = lambda v: jnp.pad(v.astype(F32), (HEADS, LANES - 2 * HEADS)).reshape(1, LANES)
    arow = lane_row(-jnp.exp(a_log.astype(F32)))
    dtrow = lane_row(dt_bias)
    const = lambda shape: pl.BlockSpec(shape, lambda i, c: (0,) * len(shape))
    return pl.pallas_call(
        _deltanet_kernel,
        grid=(b, t // C),
        in_specs=[pl.BlockSpec((1, C, 3 * HW), lambda i, c: (i, c, OFF_AQKV // (3 * HW))),
                  pl.BlockSpec((1, C, HW), lambda i, c: (i, c, OFF_AGATE // HW)),
                  pl.BlockSpec((1, C, LANES), lambda i, c: (i, c, OFF_BA // LANES)),
                  const((A_CONV, 3 * HW)), const((1, LANES)), const((1, LANES)), const((1, DH))],
        out_specs=pl.BlockSpec((1, C, HW), lambda i, c: (i, c, 0)),
        out_shape=jax.ShapeDtypeStruct((b, t, HW), BF16),
        scratch_shapes=[pltpu.VMEM((HEADS, DH, DH), F32), pltpu.VMEM((C + SUBLANES, 3 * HW), F32)],
        compiler_params=_cparams(("parallel", "arbitrary")),
        name="deltanet",
    )(proj3, proj3, projc3, conv_w.astype(F32), arow, dtrow, gnorm.reshape(1, DH).astype(F32))


def _hgrn_kernel(q_ref, f_ref, i_ref, g_ref, lb_ref, gn_ref, o_ref, st_ref, lg_ref, k_ref):
    R, K = B_ROWS, B_BLK
    HS = range(HEADS)

    @pl.when(pl.program_id(1) == 0)
    def _():
        st_ref[...] = jnp.zeros_like(st_ref)

    lb_all = lb_ref[...]
    fl = f_ref[0].astype(F32)
    lg = jnp.log(lb_all + (1.0 - lb_all) * _sigmoid(fl))
    k_ref[...] = (1.0 - lb_all) * _sigmoid(-fl)
    lg_ref[...] = lg
    brow = lax.broadcasted_iota(jnp.int32, (R // B_FAST, R), 0)
    bcol = lax.broadcasted_iota(jnp.int32, (R // B_FAST, R), 1)
    chunksum = _dot((bcol // B_FAST == brow).astype(BF16), lg.astype(BF16))
    max_decay = jnp.max(-chunksum)

    H8 = SUBLANES
    row8 = lax.broadcasted_iota(jnp.int32, (H8, DH), 0)
    sls = [slice(h * DH, (h + 1) * DH) for h in HS]

    def pairs_exact(q, k, v, gc):
        n_it = range(len(q))
        otile = [[jnp.zeros((H8, DH), F32) for _ in range(K // H8)] for _ in n_it]
        for j in range(K):
            for t in range(j // H8, K // H8):
                ts = slice(t * H8, (t + 1) * H8)
                for i in n_it:
                    d = gc[i][ts] - gc[i][j:j + 1]
                    rel = jnp.exp(jnp.where(row8 >= j - t * H8, d, -1e30) if t == j // H8 else d)
                    sj = jnp.sum(q[i][ts] * k[i][j:j + 1] * rel, axis=-1, keepdims=True)
                    otile[i][t] = otile[i][t] + sj * v[i][j:j + 1]
        return [jnp.concatenate(otile[i], axis=0) for i in n_it]

    def run(exact):
        KB = K if exact else B_FAST
        GR = B_GROUP * KB
        grow = lax.broadcasted_iota(jnp.int32, (GR, GR), 0)
        gcol = lax.broadcasted_iota(jnp.int32, (GR, GR), 1)
        same_block_lower = (gcol <= grow) & (grow // KB == gcol // KB)
        lmat_g = same_block_lower.astype(BF16)
        st = [st_ref[h] for h in HS]
        for r0 in range(0, R, GR):
            gs = slice(r0, r0 + GR)
            blks = [slice(bi * KB, (bi + 1) * KB) for bi in range(B_GROUP)]
            gc_all = _dot_exact_lhs(lmat_g, lg_ref[gs, :])
            gl_all = jnp.concatenate([jnp.broadcast_to(gc_all[bs.stop - 1:bs.stop], (KB, HW)) for bs in blks],
                                     axis=0)
            q = [_silu(q_ref[0, gs, sl].astype(F32)) for sl in sls]
            k = [k_ref[gs, sl] for sl in sls]
            v = [i_ref[0, gs, sl].astype(F32) for sl in sls]
            gc = [gc_all[:, sl] for sl in sls]
            gl = [gl_all[:, sl] for sl in sls]
            kg = [(k[h] * jnp.exp(gl[h] - gc[h])).astype(BF16) for h in HS]
            qs = [(q[h] * jnp.exp(gc[h])).astype(BF16) for h in HS]
            vb = [a.astype(BF16) for a in v]
            upd = [[_dot_tn(vb[h][bs], kg[h][bs]) for h in HS] for bs in blks]
            egl = [[jnp.exp(gl[h][bs.stop - 1:bs.stop]) for h in HS] for bs in blks]
            if exact:
                items = [(h, bs) for bs in blks for h in HS]
                pe = pairs_exact([q[h][bs] for h, bs in items], [k[h][bs] for h, bs in items],
                                 [v[h][bs] for h, bs in items], [gc[h][bs] for h, bs in items])
                intra = [jnp.concatenate([pe[bi * HEADS + h] for bi in range(B_GROUP)], axis=0) for h in HS]
            else:
                qf = [(q[h] * jnp.exp(gc[h] - gl[h])).astype(BF16) for h in HS]
                s = [jnp.where(same_block_lower, _dot_nt(qf[h], kg[h]), 0.0).astype(BF16) for h in HS]
                intra = [_dot(s[h], vb[h]) for h in HS]
            for bi, bs in enumerate(blks):
                o = [_dot_nt(qs[h][bs], st[h].astype(BF16)) + intra[h][bs] for h in HS]
                st = [st[h] * egl[bi][h] + upd[bi][h] for h in HS]
                o = [a * lax.rsqrt(jnp.mean(a * a, axis=-1, keepdims=True) + EPS) * gn_ref[...] for a in o]
                rs = slice(r0 + bi * KB, r0 + (bi + 1) * KB)
                o_ref[0, rs, :] = (jnp.concatenate(o, axis=1)
                                   * _sigmoid(g_ref[0, rs, :].astype(F32))).astype(o_ref.dtype)
        for h in HS:
            st_ref[h] = st[h]

    lax.cond(max_decay > HGRN_SAFE_DECAY, lambda: run(True), lambda: run(False))


def _hgrn(proj3, lb, gnorm):
    b, t, _ = proj3.shape
    R = B_ROWS
    seg = lambda s: pl.BlockSpec((1, R, HW), lambda i, c: (i, c, OFF_B // HW + s))
    const = lambda shape: pl.BlockSpec(shape, lambda i, c: (0,) * len(shape))
    return pl.pallas_call(
        _hgrn_kernel,
        grid=(b, t // R),
        in_specs=[seg(0), seg(1), seg(2), seg(3), const((1, HW)), const((1, DH))],
        out_specs=pl.BlockSpec((1, R, HW), lambda i, c: (i, c, 0)),
        out_shape=jax.ShapeDtypeStruct((b, t, HW), BF16),
        scratch_shapes=[pltpu.VMEM((HEADS, DH, DH), F32), pltpu.VMEM((R, HW), F32),
                        pltpu.VMEM((R, HW), F32)],
        compiler_params=_cparams(("parallel", "arbitrary")),
        name="hgrn2",
    )(proj3, proj3, proj3, proj3, lb.reshape(1, HW).astype(F32), gnorm.reshape(1, DH).astype(F32))


def _t5_bucket_np(n):
    max_exact = REL_BUCKETS // 2
    nf = np.maximum(n, 1).astype(np.float32)
    large = max_exact + (np.log(nf / max_exact) / math.log(REL_MAX_DIST / max_exact)
                         * (REL_BUCKETS - max_exact)).astype(np.int32)
    large = np.minimum(large, REL_BUCKETS - 1)
    return np.where(n < max_exact, n, large)


def _dilated_kernel(span, dil, hpb, nlb, q_ref, kp_ref, kc_ref, vp_ref, vc_ref, bias_ref,
                    o_ref, lse_ref):
    CB = C_BLOCK
    n = pl.program_id(1)
    qi = lax.broadcasted_iota(jnp.int32, (CB, 2 * CB), 0)
    kj = lax.broadcasted_iota(jnp.int32, (CB, 2 * CB), 1)
    dist = qi + CB - kj
    band = (dist >= 0) & (dist <= span)
    band_first = band & ((kj >= CB) | (n > 0))

    def rows(r, jb):
        return pl.ds(jb * CB * dil + r, CB, stride=dil) if dil > 1 else pl.ds(jb * CB, CB)

    def with_prev(p_ref, c_ref, r, jb, sl):
        prev = p_ref[0, rows(r, 0), sl] if jb == 0 else c_ref[0, rows(r, jb - 1), sl]
        return jnp.concatenate([prev, c_ref[0, rows(r, jb), sl]], axis=0)

    def tiles(items):
        n_it = range(len(items))
        q = [q_ref[0, rows(r, jb), sl].astype(BF16) for r, jb, sl, _ in items]
        k = [with_prev(kp_ref, kc_ref, r, jb, sl).astype(BF16) for r, jb, sl, _ in items]
        v = [with_prev(vp_ref, vc_ref, r, jb, sl).astype(BF16) for r, jb, sl, _ in items]
        s = [_dot_nt(q[i], k[i]) * (DH ** -0.5) + bias_ref[items[i][3]] for i in n_it]
        s = [jnp.where(band_first if items[i][1] == 0 else band, s[i], -1e30) for i in n_it]
        mx = [jnp.max(a, axis=-1, keepdims=True) for a in s]
        p = [jnp.exp(s[i] - mx[i]) for i in n_it]
        den = [jnp.sum(a, axis=-1, keepdims=True) for a in p]
        o = [_dot(p[i].astype(BF16), v[i]) / den[i] for i in n_it]
        for i, (r, jb, sl, _) in enumerate(items):
            o_ref[0, rows(r, jb), sl] = o[i]
            lse_ref[0, rows(r, jb), sl] = jnp.broadcast_to(mx[i] + jnp.log(den[i]), (CB, DH))

    for jb in range(nlb):
        if dil == 1:
            for h0 in range(0, hpb, C_TILES):
                tiles([(0, jb, slice(h * DH, (h + 1) * DH), h) for h in range(h0, h0 + C_TILES)])
        elif dil == C_TILES:
            tiles([(r, jb, slice(0, DH), 0) for r in range(dil)])
        else:
            def body(g, carry, jb=jb):
                tiles([(g * C_TILES + i, jb, slice(0, DH), 0) for i in range(C_TILES)])
                return carry
            lax.fori_loop(0, dil // C_TILES, body, 0)


def _dilated_group(proj3, gi, rel_bias):
    b, t, _ = proj3.shape
    window, dil = C_PAIRS[gi]
    span = window // dil
    CB = C_BLOCK
    pb = CB * dil
    nlb = min(C_NLB, t // pb)
    rb = nlb * pb
    hpb = HEADS if dil == 1 else 1
    bw = hpb * DH
    qi = np.arange(CB)[:, None]
    kj = np.arange(2 * CB)[None, :]
    bucket = _t5_bucket_np(np.maximum(qi + CB - kj, 0) * dil)
    onehot = jnp.asarray(np.eye(REL_BUCKETS, dtype=np.float32)[bucket])
    bias = jnp.einsum("qkb,bh->hqk", onehot, rel_bias[:, gi * HEADS:(gi + 1) * HEADS].astype(F32),
                      precision=HIGHEST)

    def seg(which, prev):
        base = (OFF_C + which * C_GROUPS * HW + gi * HW) // bw
        if prev:
            return pl.BlockSpec((1, pb, bw), lambda i, n, h: (i, jnp.maximum(n * nlb - 1, 0), base + h))
        return pl.BlockSpec((1, rb, bw), lambda i, n, h: (i, n, base + h))

    o, lse = pl.pallas_call(
        functools.partial(_dilated_kernel, span, dil, hpb, nlb),
        grid=(b, t // rb, HEADS // hpb),
        in_specs=[seg(0, False), seg(1, True), seg(1, False), seg(2, True), seg(2, False),
                  pl.BlockSpec((hpb, CB, 2 * CB), lambda i, n, h: (h, 0, 0))],
        out_specs=[pl.BlockSpec((1, rb, bw), lambda i, n, h: (i, n, h)),
                   pl.BlockSpec((1, rb, bw), lambda i, n, h: (i, n, h))],
        out_shape=[jax.ShapeDtypeStruct((b, t, HW), F32),
                   jax.ShapeDtypeStruct((b, t, HW), F32)],
        compiler_params=_cparams(("parallel", "arbitrary", "arbitrary")),
        name=f"dilated_g{gi}",
    )(proj3, proj3, proj3, proj3, proj3, bias)
    return o.reshape(b * t, HW), lse.reshape(b * t, HW)


def _mix_kernel(x_ref, gate_ref, oa_ref, ob_ref, oc0_ref, oc1_ref, oc2_ref,
                l0_ref, l1_ref, l2_ref, wa_ref, wb_ref, wc_ref, wo_ref, out_ref):
    l0, l1, l2 = l0_ref[...], l1_ref[...], l2_ref[...]
    mx = jnp.maximum(jnp.maximum(l0, l1), l2)
    e0, e1, e2 = jnp.exp(l0 - mx), jnp.exp(l1 - mx), jnp.exp(l2 - mx)
    oc = (e0 * oc0_ref[...] + e1 * oc1_ref[...] + e2 * oc2_ref[...]) / (e0 + e1 + e2)
    mix = (_sigmoid(gate_ref[:, 0:D_MODEL].astype(F32)) * _dot(oa_ref[...], wa_ref[...])
           + _sigmoid(gate_ref[:, D_MODEL:2 * D_MODEL].astype(F32)) * _dot(ob_ref[...], wb_ref[...])
           + _sigmoid(gate_ref[:, 2 * D_MODEL:3 * D_MODEL].astype(F32)) * _dot(oc.astype(BF16), wc_ref[...]))
    out_ref[...] = x_ref[...] + _dot(mix.astype(BF16), wo_ref[...])


def _mix(x2, proj2, oa, ob, ocs, lses, wa, wb, wc, wo, tm=512):
    m = x2.shape[0]
    tm = min(tm, m)
    rowblk = lambda w: pl.BlockSpec((tm, w), lambda i: (i, 0))
    const = lambda shape: pl.BlockSpec(shape, lambda i: (0,) * len(shape))
    return pl.pallas_call(
        _mix_kernel,
        grid=(m // tm,),
        in_specs=[rowblk(D_MODEL),
                  pl.BlockSpec((tm, 3 * D_MODEL), lambda i: (i, OFF_BRG // (3 * D_MODEL))),
                  rowblk(HW), rowblk(HW),
                  rowblk(HW), rowblk(HW), rowblk(HW), rowblk(HW), rowblk(HW), rowblk(HW),
                  const((HW, D_MODEL)), const((HW, D_MODEL)), const((HW, D_MODEL)),
                  const((D_MODEL, D_MODEL))],
        out_specs=rowblk(D_MODEL),
        out_shape=jax.ShapeDtypeStruct((m, D_MODEL), F32),
        compiler_params=_cparams(("parallel",)),
        name="branch_mix",
    )(x2, proj2, oa, ob, ocs[0], ocs[1], ocs[2], lses[0], lses[1], lses[2],
      wa.astype(BF16), wb.astype(BF16), wc.astype(BF16), wo.astype(BF16))


def _ffn_kernel(x_ref, g_ref, w1_ref, w3_ref, w2_ref, o_ref):
    x = x_ref[...]
    ms = jnp.mean(x * x, axis=-1, keepdims=True)
    h = (x * lax.rsqrt(ms + EPS) * g_ref[...]).astype(BF16)
    a = _dot(h, w1_ref[...])
    b = _dot(h, w3_ref[...])
    o_ref[...] = x + _dot((_silu(a) * b).astype(BF16), w2_ref[...])


def _ffn(x2, gain, w1, w3, w2, tm=512):
    m = x2.shape[0]
    ff = w1.shape[1]
    tm = min(tm, m)
    resident = lambda shape: pl.BlockSpec(shape, lambda i: (0, 0), pipeline_mode=pl.Buffered(1))
    return pl.pallas_call(
        _ffn_kernel,
        grid=(m // tm,),
        in_specs=[pl.BlockSpec((tm, D_MODEL), lambda i: (i, 0)),
                  pl.BlockSpec((1, D_MODEL), lambda i: (0, 0)),
                  resident((D_MODEL, ff)), resident((D_MODEL, ff)), resident((ff, D_MODEL))],
        out_specs=pl.BlockSpec((tm, D_MODEL), lambda i: (i, 0)),
        out_shape=jax.ShapeDtypeStruct((m, D_MODEL), F32),
        compiler_params=_cparams(("parallel",)),
        name="ffn",
    )(x2, gain.reshape(1, D_MODEL).astype(F32), w1.astype(BF16), w3.astype(BF16), w2.astype(BF16))


HALF = D_MODEL // 2
U32 = jnp.uint32


def _pack_bf16_pairs(x):
    r = x.astype(BF16).astype(F32)
    lo = lax.bitcast_convert_type(r[:, :HALF], U32) >> 16
    hi = lax.bitcast_convert_type(r[:, HALF:], U32)
    return hi | lo


def _unpack_bf16_pairs(p):
    lo = lax.bitcast_convert_type(p << 16, F32)
    hi = lax.bitcast_convert_type(p & U32(0xFFFF0000), F32)
    return lo, hi


def _router_kernel(x_ref, g_ref, wr_ref, h_ref, ids_ref, ps_ref):
    x = x_ref[...]
    ms = jnp.mean(x * x, axis=-1, keepdims=True)
    h = x * lax.rsqrt(ms + EPS) * g_ref[...]
    h_ref[...] = _pack_bf16_pairs(h)
    hh = h.astype(BF16)
    hl = (h - hh.astype(F32)).astype(BF16)
    w = wr_ref[...]
    wh = w.astype(BF16)
    wl = (w - wh.astype(F32)).astype(BF16)
    logits = _dot(hh, wh) + (_dot(hh, wl) + _dot(hl, wh))
    lane = lax.broadcasted_iota(jnp.int32, logits.shape, 1)
    neg = jnp.float32(-jnp.inf)
    l1 = jnp.where(lane < N_EXPERTS, logits, neg)
    m1 = jnp.max(l1, axis=-1, keepdims=True)
    i1 = jnp.min(jnp.where(l1 == m1, lane, LANES), axis=-1, keepdims=True)
    l2 = jnp.where(lane == i1, neg, l1)
    m2 = jnp.max(l2, axis=-1, keepdims=True)
    i2 = jnp.min(jnp.where(l2 == m2, lane, LANES), axis=-1, keepdims=True)
    e = jnp.exp(m2 - m1)
    p1 = 1.0 / (1.0 + e)
    p2 = e / (1.0 + e)
    ids_ref[...] = jnp.where(lane == 0, i1, jnp.where(lane == 1, i2, 0))
    ps_ref[...] = jnp.where(lane == 0, p1, jnp.where(lane == 1, p2, 0.0))


def _router(x2, gain, w_router, tm=1024):
    m = x2.shape[0]
    tm = min(tm, m)
    wr = jnp.pad(w_router.astype(F32), ((0, 0), (0, LANES - N_EXPERTS)))
    return pl.pallas_call(
        _router_kernel,
        grid=(m // tm,),
        in_specs=[pl.BlockSpec((tm, D_MODEL), lambda i: (i, 0)),
                  pl.BlockSpec((1, D_MODEL), lambda i: (0, 0)),
                  pl.BlockSpec((D_MODEL, LANES), lambda i: (0, 0))],
        out_specs=[pl.BlockSpec((tm, HALF), lambda i: (i, 0)),
                   pl.BlockSpec((tm, LANES), lambda i: (i, 0)),
                   pl.BlockSpec((tm, LANES), lambda i: (i, 0))],
        out_shape=[jax.ShapeDtypeStruct((m, HALF), U32),
                   jax.ShapeDtypeStruct((m, LANES), jnp.int32),
                   jax.ShapeDtypeStruct((m, LANES), F32)],
        compiler_params=_cparams(("parallel",)),
        name="router",
    )(x2, gain.reshape(1, D_MODEL).astype(F32), wr)


def _route_positions(ids, tm):
    m = ids.shape[0]
    e_flat = ids[:, :TOP_K].reshape(-1)
    onehot = (e_flat[:, None] == jnp.arange(N_EXPERTS)[None, :]).astype(jnp.int32)
    csum = jnp.cumsum(onehot, axis=0)
    counts = csum[-1]
    gsz = ((counts + tm - 1) // tm) * tm
    gend = jnp.cumsum(gsz)
    pos = jnp.sum(onehot * (gend - gsz + csum - 1), axis=1)
    nt = (TOP_K * m + N_EXPERTS * tm) // tm
    n_used = gend[-1] // tm
    tile_e = jnp.sum((jnp.arange(nt)[:, None] * tm >= gend[None, :]).astype(jnp.int32), axis=1)
    last_e = jnp.sum(((n_used - 1) * tm >= gend).astype(jnp.int32))
    tile_e = jnp.minimum(tile_e, last_e)
    return pos.astype(jnp.int32), tile_e.astype(jnp.int32), n_used.reshape(1).astype(jnp.int32), nt


def _dispatch_kernel(pos_ref, h_ref, init_ref, xs_ref, sem):
    del init_ref
    td = h_ref.shape[0]

    def row_copy(t, s):
        return pltpu.make_async_copy(h_ref.at[pl.ds(t, 1)],
                                     xs_ref.at[pl.ds(pos_ref[0, 0, TOP_K * t + s], 1)], sem)

    def start(t, c):
        for s in range(TOP_K):
            row_copy(t, s).start()
        return c

    lax.fori_loop(0, td, start, 0, unroll=8)
    for s in range(TOP_K):
        pltpu.make_async_copy(h_ref, xs_ref.at[pl.ds(0, td)], sem).wait()


def _dispatch(h, pos, npad, td=1024):
    m = h.shape[0]
    td = min(td, m)
    return pl.pallas_call(
        _dispatch_kernel,
        grid=(m // td,),
        in_specs=[pl.BlockSpec((1, 1, TOP_K * td), lambda i: (i, 0, 0), memory_space=pltpu.SMEM),
                  pl.BlockSpec((td, HALF), lambda i: (i, 0)),
                  pl.BlockSpec(memory_space=pl.ANY)],
        out_specs=pl.BlockSpec(memory_space=pl.ANY),
        out_shape=jax.ShapeDtypeStruct((npad, HALF), U32),
        scratch_shapes=[pltpu.SemaphoreType.DMA(())],
        input_output_aliases={2: 0},
        compiler_params=_cparams(("arbitrary",)),
        name="moe_dispatch",
    )(pos.reshape(m // td, 1, TOP_K * td), h, jnp.zeros((npad, HALF), U32))


def _experts_kernel(te_ref, nu_ref, xs_ref, w1_ref, w3_ref, w2_ref, y_ref, xb_ref, acc_ref):
    del te_ref
    i = pl.program_id(0)
    j = pl.program_id(1)

    @pl.when(i < nu_ref[0])
    def _():
        @pl.when(j == 0)
        def _():
            lo, hi = _unpack_bf16_pairs(xs_ref[...])
            xb_ref[...] = jnp.concatenate([lo.astype(BF16), hi.astype(BF16)], axis=1)
            acc_ref[...] = jnp.zeros_like(acc_ref)

        xb = xb_ref[...]
        a = _dot(xb, w1_ref[0])
        b = _dot(xb, w3_ref[0])
        acc_ref[...] += _dot((_silu(a) * b).astype(BF16), w2_ref[0])

        @pl.when(j == pl.num_programs(1) - 1)
        def _():
            y_ref[...] = _pack_bf16_pairs(acc_ref[...])

    @pl.when((i >= nu_ref[0]) & (j == 0))
    def _():
        y_ref[...] = jnp.zeros_like(y_ref)


def _experts(xs, tile_e, n_used, w1, w3, w2, tm, tf=1792):
    npad = xs.shape[0]
    ff = w1.shape[2]
    nj = ff // tf
    row = lambda i, j, te, nu: (jnp.minimum(i, nu[0] - 1), 0)
    jj = lambda i, j, nu: jnp.where(i < nu[0], j, nj - 1)
    return pl.pallas_call(
        _experts_kernel,
        grid_spec=pltpu.PrefetchScalarGridSpec(
            num_scalar_prefetch=2,
            grid=(npad // tm, nj),
            in_specs=[pl.BlockSpec((tm, HALF), row),
                      pl.BlockSpec((1, D_MODEL, tf), lambda i, j, te, nu: (te[i], 0, jj(i, j, nu))),
                      pl.BlockSpec((1, D_MODEL, tf), lambda i, j, te, nu: (te[i], 0, jj(i, j, nu))),
                      pl.BlockSpec((1, tf, D_MODEL), lambda i, j, te, nu: (te[i], jj(i, j, nu), 0))],
            out_specs=pl.BlockSpec((tm, HALF), lambda i, j, te, nu: (i, 0)),
            scratch_shapes=[pltpu.VMEM((tm, D_MODEL), BF16), pltpu.VMEM((tm, D_MODEL), F32)]),
        out_shape=jax.ShapeDtypeStruct((npad, HALF), U32),
        compiler_params=_cparams(("arbitrary", "arbitrary")),
        name="moe_experts",
    )(tile_e, n_used, xs, w1.astype(BF16), w3.astype(BF16), w2.astype(BF16))


def _combine_kernel(pos_ref, x_ref, ps_ref, y_ref, o_ref, buf_ref, sem):
    td = x_ref.shape[0]

    def row_copy(t, s):
        return pltpu.make_async_copy(y_ref.at[pl.ds(pos_ref[0, 0, TOP_K * t + s], 1)],
                                     buf_ref.at[s, pl.ds(t, 1)], sem.at[s])

    def start(t, c):
        for s in range(TOP_K):
            row_copy(t, s).start()
        return c

    lax.fori_loop(0, td, start, 0, unroll=8)
    for s in range(TOP_K):
        pltpu.make_async_copy(y_ref.at[pl.ds(0, td)], buf_ref.at[s], sem.at[s]).wait()
    ps = ps_ref[...]
    lo0, hi0 = _unpack_bf16_pairs(buf_ref[0])
    lo1, hi1 = _unpack_bf16_pairs(buf_ref[1])
    o_ref[:, :HALF] = x_ref[:, :HALF] + ps[:, 0:1] * lo0 + ps[:, 1:2] * lo1
    o_ref[:, HALF:] = x_ref[:, HALF:] + ps[:, 0:1] * hi0 + ps[:, 1:2] * hi1


def _combine(x2, ps, pos, y, td=1024):
    m = x2.shape[0]
    td = min(td, m)
    return pl.pallas_call(
        _combine_kernel,
        grid=(m // td,),
        in_specs=[pl.BlockSpec((1, 1, TOP_K * td), lambda i: (i, 0, 0), memory_space=pltpu.SMEM),
                  pl.BlockSpec((td, D_MODEL), lambda i: (i, 0)),
                  pl.BlockSpec((td, LANES), lambda i: (i, 0)),
                  pl.BlockSpec(memory_space=pl.ANY)],
        out_specs=pl.BlockSpec((td, D_MODEL), lambda i: (i, 0)),
        out_shape=jax.ShapeDtypeStruct((m, D_MODEL), F32),
        scratch_shapes=[pltpu.VMEM((TOP_K, td, HALF), U32), pltpu.SemaphoreType.DMA((TOP_K,))],
        compiler_params=_cparams(("arbitrary",)),
        name="moe_combine",
    )(pos.reshape(m // td, 1, TOP_K * td), x2, ps, y)


def _moe(x2, gain, w_router, w1, w3, w2, tm=1024):
    m = x2.shape[0]
    tm = min(tm, m)
    h, ids, ps = _router(x2, gain, w_router)
    pos, tile_e, n_used, nt = _route_positions(ids, tm)
    xs = _dispatch(h, pos, nt * tm)
    y = _experts(xs, tile_e, n_used, w1, w3, w2, tm)
    return _combine(x2, ps, pos, y)


def _relayout_w_in(w):
    k = w.shape[0]
    main = [w[:, _R_BRG:_R_END], w[:, _R_AQKV:_R_BETA], w[:, _R_AGATE:_R_B], w[:, _R_B:_R_C]]
    second = [w[:, _R_C:_R_BRG], w[:, _R_BETA:_R_AGATE],
              jnp.zeros((k, NP_C - OFF_BA - 2 * HEADS), w.dtype)]
    return (jnp.concatenate(main, axis=1).astype(BF16), jnp.concatenate(second, axis=1).astype(BF16))


def _layer_mixers(x2, b, t, layer, lower_bounds, w_in, norm_mix, conv_a, a_log, dt_bias,
                  gnorm_a, gnorm_b, qnorm_c, knorm_c, rel_bias, w_br_a, w_br_b, w_br_c, w_out):
    w_main, w_second = _relayout_w_in(w_in[layer])
    gain = norm_mix[layer].astype(F32)
    proj2 = _norm_proj(x2, gain, w_main, BF16)
    proj3 = proj2.reshape(b, t, NP_MAIN)
    qk_gain = jnp.concatenate([jnp.tile(qnorm_c[layer].astype(F32), (1, HEADS)).reshape(-1),
                               jnp.tile(knorm_c[layer].astype(F32), (1, HEADS)).reshape(-1)])
    projc3 = _norm_proj(x2, gain, w_second, F32, head_gain=qk_gain).reshape(b, t, NP_C)
    oa = _deltanet(proj3, projc3, conv_a[layer], a_log[layer], dt_bias[layer], gnorm_a[layer])
    ob = _hgrn(proj3, lower_bounds[layer], gnorm_b[layer])
    ocs, lses = [], []
    for gi in range(C_GROUPS):
        o, lse = _dilated_group(projc3, gi, rel_bias)
        ocs.append(o)
        lses.append(lse)
    return _mix(x2, proj2, oa.reshape(b * t, HW), ob.reshape(b * t, HW), ocs, lses,
                w_br_a[layer], w_br_b[layer], w_br_c[layer], w_out[layer])


def kernel(x, w_in, norm_mix, conv_a, a_log, dt_bias, gnorm_a, lb_logits, gnorm_b, qnorm_c, knorm_c, rel_bias, w_br_a, w_br_b, w_br_c, w_out, norm_ffn, ffn_w1, ffn_w3, ffn_w2, router, moe_w1, moe_w3, moe_w2):
    b, t, _ = x.shape
    depth = w_in.shape[0]
    p_lb = jax.nn.softmax(lb_logits.astype(F32), axis=0)
    lower_bounds = jnp.cumsum(p_lb, axis=0) - p_lb[0:1]
    x2 = x.reshape(b * t, D_MODEL).astype(F32)
    for layer in range(depth):
        x2 = _layer_mixers(x2, b, t, layer, lower_bounds, w_in, norm_mix, conv_a, a_log, dt_bias,
                           gnorm_a, gnorm_b, qnorm_c, knorm_c, rel_bias,
                           w_br_a, w_br_b, w_br_c, w_out)
        li = layer // 2
        if layer % 2 == 0:
            x2 = _ffn(x2, norm_ffn[layer], ffn_w1[li], ffn_w3[li], ffn_w2[li])
        else:
            x2 = _moe(x2, norm_ffn[layer], router[li], moe_w1[li], moe_w3[li], moe_w2[li])
    return x2.reshape(b, t, D_MODEL).astype(x.dtype)
```

```python
import functools
import math

import numpy as np
import jax
import jax.numpy as jnp
from jax import lax
from jax.experimental import pallas as pl
from jax.experimental.pallas import tpu as pltpu

F32 = jnp.float32
BF16 = jnp.bfloat16
HIGHEST = lax.Precision.HIGHEST

LANES = 128
SUBLANES = 8

D_MODEL = 1024
EPS = 1e-6
HEADS = 4
DH = 128
HW = HEADS * DH
A_CONV = 4
A_CHUNK = 64
A_ROWS = 512
B_ROWS = 512
B_BLK = 16
B_GROUP = 8
B_FAST = 32
HGRN_SAFE_DECAY = 60.0
C_PAIRS = ((128, 1), (512, 4), (2048, 16))
C_GROUPS = 3
C_BLOCK = 128
C_TILES = 4
C_NLB = 8
REL_BUCKETS = 32
REL_MAX_DIST = 2048
N_EXPERTS = 8
TOP_K = 2

OFF_BRG = 0
OFF_AQKV = 3072
OFF_AGATE = 4608
OFF_B = 5120
NP_MAIN = 7168
OFF_C = 0
OFF_BA = 4608
NP_C = 5120

_R_AQKV, _R_BETA, _R_AGATE, _R_B, _R_C, _R_BRG, _R_END = 0, 1536, 1544, 2056, 4104, 8712, 11784

VMEM_LIMIT = 56 * 1024 * 1024


def _cparams(sem):
    return pltpu.CompilerParams(dimension_semantics=sem, vmem_limit_bytes=VMEM_LIMIT)


def _sigmoid(x):
    return 1.0 / (1.0 + jnp.exp(-x))


def _silu(x):
    return x * (0.5 * jnp.tanh(0.5 * x) + 0.5)


def _softplus(x):
    return jnp.maximum(x, 0.0) + jnp.log(1.0 + jnp.exp(-jnp.abs(x)))


def _dot(a, b):
    return jnp.dot(a, b, preferred_element_type=F32)


def _dot_nt(a, b, precision=None):
    return lax.dot_general(a, b, (((1,), (1,)), ((), ())), precision=precision,
                           preferred_element_type=F32)


def _dot_tn(a, b):
    return lax.dot_general(a, b, (((0,), (0,)), ((), ())), preferred_element_type=F32)


def _norm_proj_kernel(tn, head_norm_cols, x_ref, g_ref, w_ref, hg_ref, o_ref):
    x = x_ref[...]
    ms = jnp.mean(x * x, axis=-1, keepdims=True)
    h = (x * lax.rsqrt(ms + EPS) * g_ref[...]).astype(BF16)
    for c0 in range(0, w_ref.shape[1], tn):
        r = _dot(h, w_ref[:, c0:c0 + tn])
        if c0 < head_norm_cols:
            heads = [r[:, d0:d0 + DH] for d0 in range(0, tn, DH)]
            heads = [a * lax.rsqrt(jnp.mean(a * a, axis=-1, keepdims=True) + EPS) for a in heads]
            r = jnp.concatenate(heads, axis=1) * hg_ref[:, c0:c0 + tn]
        o_ref[:, c0:c0 + tn] = r.astype(o_ref.dtype)


def _norm_proj(x2, gain, w_bf16, out_dtype, head_gain=None, tm=512, tn=1024):
    m = x2.shape[0]
    n = w_bf16.shape[1]
    tm = min(tm, m)
    norm_cols = 0 if head_gain is None else head_gain.shape[0]
    assert norm_cols % tn == 0
    hg = jnp.zeros((1, n), F32)
    if head_gain is not None:
        hg = hg.at[0, :norm_cols].set(head_gain.astype(F32))
    return pl.pallas_call(
        functools.partial(_norm_proj_kernel, tn, norm_cols),
        grid=(m // tm,),
        in_specs=[pl.BlockSpec((tm, D_MODEL), lambda i: (i, 0)),
                  pl.BlockSpec((1, D_MODEL), lambda i: (0, 0)),
                  pl.BlockSpec((D_MODEL, n), lambda i: (0, 0), pipeline_mode=pl.Buffered(1)),
                  pl.BlockSpec((1, n), lambda i: (0, 0))],
        out_specs=pl.BlockSpec((tm, n), lambda i: (i, 0)),
        out_shape=jax.ShapeDtypeStruct((m, n), out_dtype),
        compiler_params=_cparams(("parallel",)),
        name="norm_proj",
    )(x2, gain.reshape(1, D_MODEL), w_bf16, hg)


def _dotb(a, b):
    return _dot(a.astype(BF16), b.astype(BF16))


def _dot_exact_lhs(a_bf16, b):
    b0 = b.astype(BF16)
    r1 = b - b0.astype(F32)
    b1 = r1.astype(BF16)
    b2 = (r1 - b1.astype(F32)).astype(BF16)
    return _dot(a_bf16, b0) + (_dot(a_bf16, b1) + _dot(a_bf16, b2))


def _deltanet_kernel(qkv_ref, gate_ref, ba_ref, convw_ref, arow_ref, dtrow_ref, gn_ref,
                     o_ref, s_ref, xe_ref):
    C = A_CHUNK
    R = A_ROWS
    HS = range(HEADS)
    IT = range((R // C) * HEADS)

    @pl.when(pl.program_id(1) == 0)
    def _():
        s_ref[...] = jnp.zeros_like(s_ref)
        xe_ref[0:SUBLANES, :] = jnp.zeros((SUBLANES, 3 * HW), F32)

    x = qkv_ref[0].astype(F32)
    xe_ref[SUBLANES:SUBLANES + R, :] = x
    w = convw_ref[...]
    y = w[A_CONV - 1:A_CONV] * x
    for d in range(1, A_CONV):
        y = y + w[A_CONV - 1 - d:A_CONV - d] * xe_ref[SUBLANES - d:SUBLANES - d + R, :]
    xe_ref[0:SUBLANES, :] = x[R - SUBLANES:R]
    y = _silu(y)

    ba = ba_ref[0]
    beta_all = _sigmoid(ba)
    g_all = arow_ref[...] * _softplus(ba + dtrow_ref[...])
    gate = _silu(gate_ref[0].astype(F32))
    s_old = [s_ref[h] for h in HS]

    row = lax.broadcasted_iota(jnp.int32, (C, C), 0)
    col = lax.broadcasted_iota(jnp.int32, (C, C), 1)
    lmat = (col <= row).astype(BF16)
    rowx = lax.broadcasted_iota(jnp.int32, (C, DH + C), 0)
    colx = lax.broadcasted_iota(jnp.int32, (C, DH + C), 1)
    umask = (colx < DH) | (rowx > colx - DH)
    eye = (row == col).astype(F32)
    bd8 = (row >> 3) == (col >> 3)

    def merge_mask(sh):
        return (((row >> (sh + 1)) == (col >> (sh + 1)))
                & (((row >> sh) & 1) == 1) & (((col >> sh) & 1) == 0))

    rows = [slice((i // HEADS) * C, (i // HEADS + 1) * C) for i in IT]
    hd = [i % HEADS for i in IT]
    q = [y[rows[i], hd[i] * DH:(hd[i] + 1) * DH] for i in IT]
    k = [y[rows[i], HW + hd[i] * DH:HW + (hd[i] + 1) * DH] for i in IT]
    v = [y[rows[i], 2 * HW + hd[i] * DH:2 * HW + (hd[i] + 1) * DH] for i in IT]
    q = [a * (lax.rsqrt(jnp.sum(a * a, axis=-1, keepdims=True) + EPS) * (DH ** -0.5)) for a in q]
    k = [a * lax.rsqrt(jnp.sum(a * a, axis=-1, keepdims=True) + EPS) for a in k]
    beta = [beta_all[rows[i], hd[i]:hd[i] + 1] for i in IT]
    gb = [jnp.broadcast_to(g_all[rows[i], HEADS + hd[i]:HEADS + hd[i] + 1], (C, DH + C)) for i in IT]
    dext = [_dot_exact_lhs(lmat, jnp.where(umask, gb[i], 0.0)) for i in IT]
    gc = [d[:, :DH] for d in dext]
    edm = [jnp.exp(d[:, DH:]) for d in dext]
    egc = [jnp.exp(g) for g in gc]
    gl = [g[C - 1:C, :] for g in gc]
    kb = [k[i] * beta[i] for i in IT]
    m = [_dot_nt(kb[i].astype(BF16), k[i].astype(BF16)) * jnp.where(row > col, edm[i], 0.0)
         for i in IT]

    nd = [jnp.where(bd8, -a, 0.0) for a in m]
    p2 = [_dotb(a, a) for a in nd]
    p4 = [_dotb(a, a) for a in p2]
    x1 = [eye + nd[i] for i in IT]
    x1 = [x1[i] + _dotb(x1[i], p2[i]) for i in IT]
    xi = [x1[i] + _dotb(x1[i], p4[i]) for i in IT]
    for sh in (3, 4, 5):
        mm = merge_mask(sh)
        t = [_dotb(xi[i], jnp.where(mm, m[i], 0.0)) for i in IT]
        xi = [xi[i] - _dotb(t[i], xi[i]) for i in IT]

    rhs = [jnp.concatenate([v[i] * beta[i], kb[i] * egc[i]], axis=1) for i in IT]
    sol = [_dotb(xi[i], rhs[i]) for i in IT]
    attn = [(_dot_nt(q[i].astype(BF16), k[i].astype(BF16))
             * jnp.where(row >= col, edm[i], 0.0)).astype(BF16) for i in IT]
    qg = [(q[i] * egc[i]).astype(BF16) for i in IT]
    kg = [(k[i] * jnp.exp(gl[i] - gc[i])).astype(BF16) for i in IT]
    egl = [jnp.exp(a) for a in gl]

    s_cur = s_old
    for c in range(R // C):
        it = [c * HEADS + h for h in HS]
        sb = [a.astype(BF16) for a in s_cur]
        v_new = [(sol[i][:, :DH] - _dot(sol[i][:, DH:].astype(BF16), sb[h])).astype(BF16)
                 for h, i in enumerate(it)]
        o = [_dot(qg[i], sb[h]) + _dot(attn[i], v_new[h]) for h, i in enumerate(it)]
        s_cur = [s_cur[h] * egl[i] + _dot_tn(kg[i], v_new[h]) for h, i in enumerate(it)]
        o = [a * lax.rsqrt(jnp.mean(a * a, axis=-1, keepdims=True) + EPS) * gn_ref[...] for a in o]
        o_ref[0, c * C:(c + 1) * C, :] = (jnp.concatenate(o, axis=1)
                                          * gate[c * C:(c + 1) * C]).astype(o_ref.dtype)
    for h in HS:
        s_ref[h] = s_cur[h]


def _deltanet(proj3, projc3, conv_w, a_log, dt_bias, gnorm):
    b, t, _ = proj3.shape
    C = A_ROWS
    lane_row = lambda v: jnp.pad(v.astype(F32), (HEADS, LANES - 2 * HEADS)).reshape(1, LANES)
    arow = lane_row(-jnp.exp(a_log.astype(F32)))
    dtrow = lane_row(dt_bias)
    const = lambda shape: pl.BlockSpec(shape, lambda i, c: (0,) * len(shape))
    return pl.pallas_call(
        _deltanet_kernel,
        grid=(b, t // C),
        in_specs=[pl.BlockSpec((1, C, 3 * HW), lambda i, c: (i, c, OFF_AQKV // (3 * HW))),
                  pl.BlockSpec((1, C, HW), lambda i, c: (i, c, OFF_AGATE // HW)),
                  pl.BlockSpec((1, C, LANES), lambda i, c: (i, c, OFF_BA // LANES)),
                  const((A_CONV, 3 * HW)), const((1, LANES)), const((1, LANES)), const((1, DH))],
        out_specs=pl.BlockSpec((1, C, HW), lambda i, c: (i, c, 0)),
        out_shape=jax.ShapeDtypeStruct((b, t, HW), BF16),
        scratch_shapes=[pltpu.VMEM((HEADS, DH, DH), F32), pltpu.VMEM((C + SUBLANES, 3 * HW), F32)],
        compiler_params=_cparams(("parallel", "arbitrary")),
        name="deltanet",
    )(proj3, proj3, projc3, conv_w.astype(F32), arow, dtrow, gnorm.reshape(1, DH).astype(F32))


def _hgrn_kernel(q_ref, f_ref, i_ref, g_ref, lb_ref, gn_ref, o_ref, st_ref, lg_ref, k_ref):
    R, K = B_ROWS, B_BLK
    HS = range(HEADS)

    @pl.when(pl.program_id(1) == 0)
    def _():
        st_ref[...] = jnp.zeros_like(st_ref)

    lb_all = lb_ref[...]
    fl = f_ref[0].astype(F32)
    lg = jnp.log(lb_all + (1.0 - lb_all) * _sigmoid(fl))
    k_ref[...] = (1.0 - lb_all) * _sigmoid(-fl)
    lg_ref[...] = lg
    brow = lax.broadcasted_iota(jnp.int32, (R // B_FAST, R), 0)
    bcol = lax.broadcasted_iota(jnp.int32, (R // B_FAST, R), 1)
    chunksum = _dot((bcol // B_FAST == brow).astype(BF16), lg.astype(BF16))
    max_decay = jnp.max(-chunksum)

    H8 = SUBLANES
    row8 = lax.broadcasted_iota(jnp.int32, (H8, DH), 0)
    sls = [slice(h * DH, (h + 1) * DH) for h in HS]

    def pairs_exact(q, k, v, gc):
        n_it = range(len(q))
        otile = [[jnp.zeros((H8, DH), F32) for _ in range(K // H8)] for _ in n_it]
        for j in range(K):
            for t in range(j // H8, K // H8):
                ts = slice(t * H8, (t + 1) * H8)
                for i in n_it:
                    d = gc[i][ts] - gc[i][j:j + 1]
                    rel = jnp.exp(jnp.where(row8 >= j - t * H8, d, -1e30) if t == j // H8 else d)
                    sj = jnp.sum(q[i][ts] * k[i][j:j + 1] * rel, axis=-1, keepdims=True)
                    otile[i][t] = otile[i][t] + sj * v[i][j:j + 1]
        return [jnp.concatenate(otile[i], axis=0) for i in n_it]

    def run(exact):
        KB = K if exact else B_FAST
        GR = B_GROUP * KB
        grow = lax.broadcasted_iota(jnp.int32, (GR, GR), 0)
        gcol = lax.broadcasted_iota(jnp.int32, (GR, GR), 1)
        same_block_lower = (gcol <= grow) & (grow // KB == gcol // KB)
        lmat_g = same_block_lower.astype(BF16)
        st = [st_ref[h] for h in HS]
        for r0 in range(0, R, GR):
            gs = slice(r0, r0 + GR)
            blks = [slice(bi * KB, (bi + 1) * KB) for bi in range(B_GROUP)]
            gc_all = _dot_exact_lhs(lmat_g, lg_ref[gs, :])
            gl_all = jnp.concatenate([jnp.broadcast_to(gc_all[bs.stop - 1:bs.stop], (KB, HW)) for bs in blks],
                                     axis=0)
            q = [_silu(q_ref[0, gs, sl].astype(F32)) for sl in sls]
            k = [k_ref[gs, sl] for sl in sls]
            v = [i_ref[0, gs, sl].astype(F32) for sl in sls]
            gc = [gc_all[:, sl] for sl in sls]
            gl = [gl_all[:, sl] for sl in sls]
            kg = [(k[h] * jnp.exp(gl[h] - gc[h])).astype(BF16) for h in HS]
            qs = [(q[h] * jnp.exp(gc[h])).astype(BF16) for h in HS]
            vb = [a.astype(BF16) for a in v]
            upd = [[_dot_tn(vb[h][bs], kg[h][bs]) for h in HS] for bs in blks]
            egl = [[jnp.exp(gl[h][bs.stop - 1:bs.stop]) for h in HS] for bs in blks]
            if exact:
                items = [(h, bs) for bs in blks for h in HS]
                pe = pairs_exact([q[h][bs] for h, bs in items], [k[h][bs] for h, bs in items],
                                 [v[h][bs] for h, bs in items], [gc[h][bs] for h, bs in items])
                intra = [jnp.concatenate([pe[bi * HEADS + h] for bi in range(B_GROUP)], axis=0) for h in HS]
            else:
                qf = [(q[h] * jnp.exp(gc[h] - gl[h])).astype(BF16) for h in HS]
                s = [jnp.where(same_block_lower, _dot_nt(qf[h], kg[h]), 0.0).astype(BF16) for h in HS]
                intra = [_dot(s[h], vb[h]) for h in HS]
            for bi, bs in enumerate(blks):
                o = [_dot_nt(qs[h][bs], st[h].astype(BF16)) + intra[h][bs] for h in HS]
                st = [st[h] * egl[bi][h] + upd[bi][h] for h in HS]
                o = [a * lax.rsqrt(jnp.mean(a * a, axis=-1, keepdims=True) + EPS) * gn_ref[...] for a in o]
                rs = slice(r0 + bi * KB, r0 + (bi + 1) * KB)
                o_ref[0, rs, :] = (jnp.concatenate(o, axis=1)
                                   * _sigmoid(g_ref[0, rs, :].astype(F32))).astype(o_ref.dtype)
        for h in HS:
            st_ref[h] = st[h]

    lax.cond(max_decay > HGRN_SAFE_DECAY, lambda: run(True), lambda: run(False))


def _hgrn(proj3, lb, gnorm):
    b, t, _ = proj3.shape
    R = B_ROWS
    seg = lambda s: pl.BlockSpec((1, R, HW), lambda i, c: (i, c, OFF_B // HW + s))
    const = lambda shape: pl.BlockSpec(shape, lambda i, c: (0,) * len(shape))
    return pl.pallas_call(
        _hgrn_kernel,
        grid=(b, t // R),
        in_specs=[seg(0), seg(1), seg(2), seg(3), const((1, HW)), const((1, DH))],
        out_specs=pl.BlockSpec((1, R, HW), lambda i, c: (i, c, 0)),
        out_shape=jax.ShapeDtypeStruct((b, t, HW), BF16),
        scratch_shapes=[pltpu.VMEM((HEADS, DH, DH), F32), pltpu.VMEM((R, HW), F32),
                        pltpu.VMEM((R, HW), F32)],
        compiler_params=_cparams(("parallel", "arbitrary")),
        name="hgrn2",
    )(proj3, proj3, proj3, proj3, lb.reshape(1, HW).astype(F32), gnorm.reshape(1, DH).astype(F32))


def _t5_bucket_np(n):
    max_exact = REL_BUCKETS // 2
    nf = np.maximum(n, 1).astype(np.float32)
    large = max_exact + (np.log(nf / max_exact) / math.log(REL_MAX_DIST / max_exact)
                         * (REL_BUCKETS - max_exact)).astype(np.int32)
    large = np.minimum(large, REL_BUCKETS - 1)
    return np.where(n < max_exact, n, large)


def _dilated_kernel(span, dil, hpb, nlb, q_ref, kp_ref, kc_ref, vp_ref, vc_ref, bias_ref,
                    o_ref, lse_ref):
    CB = C_BLOCK
    n = pl.program_id(1)
    qi = lax.broadcasted_iota(jnp.int32, (CB, 2 * CB), 0)
    kj = lax.broadcasted_iota(jnp.int32, (CB, 2 * CB), 1)
    dist = qi + CB - kj
    band = (dist >= 0) & (dist <= span)
    band_first = band & ((kj >= CB) | (n > 0))

    def rows(r, jb):
        return pl.ds(jb * CB * dil + r, CB, stride=dil) if dil > 1 else pl.ds(jb * CB, CB)

    def with_prev(p_ref, c_ref, r, jb, sl):
        prev = p_ref[0, rows(r, 0), sl] if jb == 0 else c_ref[0, rows(r, jb - 1), sl]
        return jnp.concatenate([prev, c_ref[0, rows(r, jb), sl]], axis=0)

    def tiles(items):
        n_it = range(len(items))
        q = [q_ref[0, rows(r, jb), sl].astype(BF16) for r, jb, sl, _ in items]
        k = [with_prev(kp_ref, kc_ref, r, jb, sl).astype(BF16) for r, jb, sl, _ in items]
        v = [with_prev(vp_ref, vc_ref, r, jb, sl).astype(BF16) for r, jb, sl, _ in items]
        s = [_dot_nt(q[i], k[i]) * (DH ** -0.5) + bias_ref[items[i][3]] for i in n_it]
        s = [jnp.where(band_first if items[i][1] == 0 else band, s[i], -1e30) for i in n_it]
        mx = [jnp.max(a, axis=-1, keepdims=True) for a in s]
        p = [jnp.exp(s[i] - mx[i]) for i in n_it]
        den = [jnp.sum(a, axis=-1, keepdims=True) for a in p]
        o = [_dot(p[i].astype(BF16), v[i]) / den[i] for i in n_it]
        for i, (r, jb, sl, _) in enumerate(items):
            o_ref[0, rows(r, jb), sl] = o[i]
            lse_ref[0, rows(r, jb), sl] = jnp.broadcast_to(mx[i] + jnp.log(den[i]), (CB, DH))

    for jb in range(nlb):
        if dil == 1:
            for h0 in range(0, hpb, C_TILES):
                tiles([(0, jb, slice(h * DH, (h + 1) * DH), h) for h in range(h0, h0 + C_TILES)])
        elif dil == C_TILES:
            tiles([(r, jb, slice(0, DH), 0) for r in range(dil)])
        else:
            def body(g, carry, jb=jb):
                tiles([(g * C_TILES + i, jb, slice(0, DH), 0) for i in range(C_TILES)])
                return carry
            lax.fori_loop(0, dil // C_TILES, body, 0)


def _dilated_group(proj3, gi, rel_bias):
    b, t, _ = proj3.shape
    window, dil = C_PAIRS[gi]
    span = window // dil
    CB = C_BLOCK
    pb = CB * dil
    nlb = min(C_NLB, t // pb)
    rb = nlb * pb
    hpb = HEADS if dil == 1 else 1
    bw = hpb * DH
    qi = np.arange(CB)[:, None]
    kj = np.arange(2 * CB)[None, :]
    bucket = _t5_bucket_np(np.maximum(qi + CB - kj, 0) * dil)
    onehot = jnp.asarray(np.eye(REL_BUCKETS, dtype=np.float32)[bucket])
    bias = jnp.einsum("qkb,bh->hqk", onehot, rel_bias[:, gi * HEADS:(gi + 1) * HEADS].astype(F32),
                      precision=HIGHEST)

    def seg(which, prev):
        base = (OFF_C + which * C_GROUPS * HW + gi * HW) // bw
        if prev:
            return pl.BlockSpec((1, pb, bw), lambda i, n, h: (i, jnp.maximum(n * nlb - 1, 0), base + h))
        return pl.BlockSpec((1, rb, bw), lambda i, n, h: (i, n, base + h))

    o, lse = pl.pallas_call(
        functools.partial(_dilated_kernel, span, dil, hpb, nlb),
        grid=(b, t // rb, HEADS // hpb),
        in_specs=[seg(0, False), seg(1, True), seg(1, False), seg(2, True), seg(2, False),
                  pl.BlockSpec((hpb, CB, 2 * CB), lambda i, n, h: (h, 0, 0))],
        out_specs=[pl.BlockSpec((1, rb, bw), lambda i, n, h: (i, n, h)),
                   pl.BlockSpec((1, rb, bw), lambda i, n, h: (i, n, h))],
        out_shape=[jax.ShapeDtypeStruct((b, t, HW), F32),
                   jax.ShapeDtypeStruct((b, t, HW), F32)],
        compiler_params=_cparams(("parallel", "arbitrary", "arbitrary")),
        name=f"dilated_g{gi}",
    )(proj3, proj3, proj3, proj3, proj3, bias)
    return o.reshape(b * t, HW), lse.reshape(b * t, HW)


def _mix_kernel(x_ref, gate_ref, oa_ref, ob_ref, oc0_ref, oc1_ref, oc2_ref,
                l0_ref, l1_ref, l2_ref, wa_ref, wb_ref, wc_ref, wo_ref, out_ref):
    l0, l1, l2 = l0_ref[...], l1_ref[...], l2_ref[...]
    mx = jnp.maximum(jnp.maximum(l0, l1), l2)
    e0, e1, e2 = jnp.exp(l0 - mx), jnp.exp(l1 - mx), jnp.exp(l2 - mx)
    oc = (e0 * oc0_ref[...] + e1 * oc1_ref[...] + e2 * oc2_ref[...]) / (e0 + e1 + e2)
    mix = (_sigmoid(gate_ref[:, 0:D_MODEL].astype(F32)) * _dot(oa_ref[...], wa_ref[...])
           + _sigmoid(gate_ref[:, D_MODEL:2 * D_MODEL].astype(F32)) * _dot(ob_ref[...], wb_ref[...])
           + _sigmoid(gate_ref[:, 2 * D_MODEL:3 * D_MODEL].astype(F32)) * _dot(oc.astype(BF16), wc_ref[...]))
    out_ref[...] = x_ref[...] + _dot(mix.astype(BF16), wo_ref[...])


def _mix(x2, proj2, oa, ob, ocs, lses, wa, wb, wc, wo, tm=512):
    m = x2.shape[0]
    tm = min(tm, m)
    rowblk = lambda w: pl.BlockSpec((tm, w), lambda i: (i, 0))
    const = lambda shape: pl.BlockSpec(shape, lambda i: (0,) * len(shape))
    return pl.pallas_call(
        _mix_kernel,
        grid=(m // tm,),
        in_specs=[rowblk(D_MODEL),
                  pl.BlockSpec((tm, 3 * D_MODEL), lambda i: (i, OFF_BRG // (3 * D_MODEL))),
                  rowblk(HW), rowblk(HW),
                  rowblk(HW), rowblk(HW), rowblk(HW), rowblk(HW), rowblk(HW), rowblk(HW),
                  const((HW, D_MODEL)), const((HW, D_MODEL)), const((HW, D_MODEL)),
                  const((D_MODEL, D_MODEL))],
        out_specs=rowblk(D_MODEL),
        out_shape=jax.ShapeDtypeStruct((m, D_MODEL), F32),
        compiler_params=_cparams(("parallel",)),
        name="branch_mix",
    )(x2, proj2, oa, ob, ocs[0], ocs[1], ocs[2], lses[0], lses[1], lses[2],
      wa.astype(BF16), wb.astype(BF16), wc.astype(BF16), wo.astype(BF16))


def _ffn_kernel(x_ref, g_ref, w1_ref, w3_ref, w2_ref, o_ref):
    x = x_ref[...]
    ms = jnp.mean(x * x, axis=-1, keepdims=True)
    h = (x * lax.rsqrt(ms + EPS) * g_ref[...]).astype(BF16)
    a = _dot(h, w1_ref[...])
    b = _dot(h, w3_ref[...])
    o_ref[...] = x + _dot((_silu(a) * b).astype(BF16), w2_ref[...])


def _ffn(x2, gain, w1, w3, w2, tm=512):
    m = x2.shape[0]
    ff = w1.shape[1]
    tm = min(tm, m)
    resident = lambda shape: pl.BlockSpec(shape, lambda i: (0, 0), pipeline_mode=pl.Buffered(1))
    return pl.pallas_call(
        _ffn_kernel,
        grid=(m // tm,),
        in_specs=[pl.BlockSpec((tm, D_MODEL), lambda i: (i, 0)),
                  pl.BlockSpec((1, D_MODEL), lambda i: (0, 0)),
                  resident((D_MODEL, ff)), resident((D_MODEL, ff)), resident((ff, D_MODEL))],
        out_specs=pl.BlockSpec((tm, D_MODEL), lambda i: (i, 0)),
        out_shape=jax.ShapeDtypeStruct((m, D_MODEL), F32),
        compiler_params=_cparams(("parallel",)),
        name="ffn",
    )(x2, gain.reshape(1, D_MODEL).astype(F32), w1.astype(BF16), w3.astype(BF16), w2.astype(BF16))


HALF = D_MODEL // 2
U32 = jnp.uint32


def _pack_bf16_pairs(x):
    r = x.astype(BF16).astype(F32)
    lo = lax.bitcast_convert_type(r[:, :HALF], U32) >> 16
    hi = lax.bitcast_convert_type(r[:, HALF:], U32)
    return hi | lo


def _unpack_bf16_pairs(p):
    lo = lax.bitcast_convert_type(p << 16, F32)
    hi = lax.bitcast_convert_type(p & U32(0xFFFF0000), F32)
    return lo, hi


def _router_kernel(x_ref, g_ref, wr_ref, h_ref, ids_ref, ps_ref):
    x = x_ref[...]
    ms = jnp.mean(x * x, axis=-1, keepdims=True)
    h = x * lax.rsqrt(ms + EPS) * g_ref[...]
    h_ref[...] = _pack_bf16_pairs(h)
    hh = h.astype(BF16)
    hl = (h - hh.astype(F32)).astype(BF16)
    w = wr_ref[...]
    wh = w.astype(BF16)
    wl = (w - wh.astype(F32)).astype(BF16)
    logits = _dot(hh, wh) + (_dot(hh, wl) + _dot(hl, wh))
    lane = lax.broadcasted_iota(jnp.int32, logits.shape, 1)
    neg = jnp.float32(-jnp.inf)
    l1 = jnp.where(lane < N_EXPERTS, logits, neg)
    m1 = jnp.max(l1, axis=-1, keepdims=True)
    i1 = jnp.min(jnp.where(l1 == m1, lane, LANES), axis=-1, keepdims=True)
    l2 = jnp.where(lane == i1, neg, l1)
    m2 = jnp.max(l2, axis=-1, keepdims=True)
    i2 = jnp.min(jnp.where(l2 == m2, lane, LANES), axis=-1, keepdims=True)
    e = jnp.exp(m2 - m1)
    p1 = 1.0 / (1.0 + e)
    p2 = e / (1.0 + e)
    ids_ref[...] = jnp.where(lane == 0, i1, jnp.where(lane == 1, i2, 0))
    ps_ref[...] = jnp.where(lane == 0, p1, jnp.where(lane == 1, p2, 0.0))


def _router(x2, gain, w_router, tm=1024):
    m = x2.shape[0]
    tm = min(tm, m)
    wr = jnp.pad(w_router.astype(F32), ((0, 0), (0, LANES - N_EXPERTS)))
    return pl.pallas_call(
        _router_kernel,
        grid=(m // tm,),
        in_specs=[pl.BlockSpec((tm, D_MODEL), lambda i: (i, 0)),
                  pl.BlockSpec((1, D_MODEL), lambda i: (0, 0)),
                  pl.BlockSpec((D_MODEL, LANES), lambda i: (0, 0))],
        out_specs=[pl.BlockSpec((tm, HALF), lambda i: (i, 0)),
                   pl.BlockSpec((tm, LANES), lambda i: (i, 0)),
                   pl.BlockSpec((tm, LANES), lambda i: (i, 0))],
        out_shape=[jax.ShapeDtypeStruct((m, HALF), U32),
                   jax.ShapeDtypeStruct((m, LANES), jnp.int32),
                   jax.ShapeDtypeStruct((m, LANES), F32)],
        compiler_params=_cparams(("parallel",)),
        name="router",
    )(x2, gain.reshape(1, D_MODEL).astype(F32), wr)


def _route_positions(ids, tm):
    m = ids.shape[0]
    e_flat = ids[:, :TOP_K].reshape(-1)
    onehot = (e_flat[:, None] == jnp.arange(N_EXPERTS)[None, :]).astype(jnp.int32)
    csum = jnp.cumsum(onehot, axis=0)
    counts = csum[-1]
    gsz = ((counts + tm - 1) // tm) * tm
    gend = jnp.cumsum(gsz)
    pos = jnp.sum(onehot * (gend - gsz + csum - 1), axis=1)
    nt = (TOP_K * m + N_EXPERTS * tm) // tm
    n_used = gend[-1] // tm
    tile_e = jnp.sum((jnp.arange(nt)[:, None] * tm >= gend[None, :]).astype(jnp.int32), axis=1)
    last_e = jnp.sum(((n_used - 1) * tm >= gend).astype(jnp.int32))
    tile_e = jnp.minimum(tile_e, last_e)
    return pos.astype(jnp.int32), tile_e.astype(jnp.int32), n_used.reshape(1).astype(jnp.int32), nt


def _dispatch_kernel(pos_ref, h_ref, init_ref, xs_ref, sem):
    del init_ref
    td = h_ref.shape[0]

    def row_copy(t, s):
        return pltpu.make_async_copy(h_ref.at[pl.ds(t, 1)],
                                     xs_ref.at[pl.ds(pos_ref[0, 0, TOP_K * t + s], 1)], sem)

    def start(t, c):
        for s in range(TOP_K):
            row_copy(t, s).start()
        return c

    lax.fori_loop(0, td, start, 0, unroll=8)
    for s in range(TOP_K):
        pltpu.make_async_copy(h_ref, xs_ref.at[pl.ds(0, td)], sem).wait()


def _dispatch(h, pos, npad, td=1024):
    m = h.shape[0]
    td = min(td, m)
    return pl.pallas_call(
        _dispatch_kernel,
        grid=(m // td,),
        in_specs=[pl.BlockSpec((1, 1, TOP_K * td), lambda i: (i, 0, 0), memory_space=pltpu.SMEM),
                  pl.BlockSpec((td, HALF), lambda i: (i, 0)),
                  pl.BlockSpec(memory_space=pl.ANY)],
        out_specs=pl.BlockSpec(memory_space=pl.ANY),
        out_shape=jax.ShapeDtypeStruct((npad, HALF), U32),
        scratch_shapes=[pltpu.SemaphoreType.DMA(())],
        input_output_aliases={2: 0},
        compiler_params=_cparams(("arbitrary",)),
        name="moe_dispatch",
    )(pos.reshape(m // td, 1, TOP_K * td), h, jnp.zeros((npad, HALF), U32))


def _experts_kernel(nj, te_ref, nu_ref, xs_ref, w1_ref, w3_ref, w2_ref, y_ref, xb_ref, acc_ref):
    del te_ref
    i = pl.program_id(0)
    j = pl.program_id(1)

    @pl.when(i < nu_ref[0])
    def _():
        @pl.when(j == 0)
        def _():
            lo, hi = _unpack_bf16_pairs(xs_ref[...])
            xb_ref[...] = jnp.concatenate([lo.astype(BF16), hi.astype(BF16)], axis=1)

        xb = xb_ref[...]
        a = _dot(xb, w1_ref[0])
        b = _dot(xb, w3_ref[0])
        part = _dot((_silu(a) * b).astype(BF16), w2_ref[0])

        @pl.when(j == 0)
        def _():
            acc_ref[...] = part

        if nj > 2:
            @pl.when((j > 0) & (j < nj - 1))
            def _():
                acc_ref[...] += part

        @pl.when(j == nj - 1)
        def _():
            y_ref[...] = _pack_bf16_pairs(acc_ref[...] + part)

    @pl.when((i >= nu_ref[0]) & (j == 0))
    def _():
        y_ref[...] = jnp.zeros_like(y_ref)


def _experts(xs, tile_e, n_used, w1, w3, w2, tm, tf=1792):
    npad = xs.shape[0]
    ff = w1.shape[2]
    nj = ff // tf
    assert nj >= 2 and ff % tf == 0
    row = lambda i, j, te, nu: (jnp.minimum(i, nu[0] - 1), 0)
    jj = lambda i, j, nu: jnp.where(i < nu[0], j, nj - 1)
    return pl.pallas_call(
        functools.partial(_experts_kernel, nj),
        grid_spec=pltpu.PrefetchScalarGridSpec(
            num_scalar_prefetch=2,
            grid=(npad // tm, nj),
            in_specs=[pl.BlockSpec((tm, HALF), row),
                      pl.BlockSpec((1, D_MODEL, tf), lambda i, j, te, nu: (te[i], 0, jj(i, j, nu))),
                      pl.BlockSpec((1, D_MODEL, tf), lambda i, j, te, nu: (te[i], 0, jj(i, j, nu))),
                      pl.BlockSpec((1, tf, D_MODEL), lambda i, j, te, nu: (te[i], jj(i, j, nu), 0))],
            out_specs=pl.BlockSpec((tm, HALF), lambda i, j, te, nu: (i, 0)),
            scratch_shapes=[pltpu.VMEM((tm, D_MODEL), BF16), pltpu.VMEM((tm, D_MODEL), F32)]),
        out_shape=jax.ShapeDtypeStruct((npad, HALF), U32),
        compiler_params=_cparams(("arbitrary", "arbitrary")),
        name="moe_experts",
    )(tile_e, n_used, xs, w1.astype(BF16), w3.astype(BF16), w2.astype(BF16))


def _combine_kernel(pos_ref, x_ref, ps_ref, y_ref, o_ref, buf_ref, sem):
    td = x_ref.shape[0]

    def row_copy(t, s):
        return pltpu.make_async_copy(y_ref.at[pl.ds(pos_ref[0, 0, TOP_K * t + s], 1)],
                                     buf_ref.at[s, pl.ds(t, 1)], sem.at[s])

    def start(t, c):
        for s in range(TOP_K):
            row_copy(t, s).start()
        return c

    lax.fori_loop(0, td, start, 0, unroll=8)
    for s in range(TOP_K):
        pltpu.make_async_copy(y_ref.at[pl.ds(0, td)], buf_ref.at[s], sem.at[s]).wait()
    ps = ps_ref[...]
    lo0, hi0 = _unpack_bf16_pairs(buf_ref[0])
    lo1, hi1 = _unpack_bf16_pairs(buf_ref[1])
    o_ref[:, :HALF] = x_ref[:, :HALF] + ps[:, 0:1] * lo0 + ps[:, 1:2] * lo1
    o_ref[:, HALF:] = x_ref[:, HALF:] + ps[:, 0:1] * hi0 + ps[:, 1:2] * hi1


def _combine(x2, ps, pos, y, td=1024):
    m = x2.shape[0]
    td = min(td, m)
    return pl.pallas_call(
        _combine_kernel,
        grid=(m // td,),
        in_specs=[pl.BlockSpec((1, 1, TOP_K * td), lambda i: (i, 0, 0), memory_space=pltpu.SMEM),
                  pl.BlockSpec((td, D_MODEL), lambda i: (i, 0)),
                  pl.BlockSpec((td, LANES), lambda i: (i, 0)),
                  pl.BlockSpec(memory_space=pl.ANY)],
        out_specs=pl.BlockSpec((td, D_MODEL), lambda i: (i, 0)),
        out_shape=jax.ShapeDtypeStruct((m, D_MODEL), F32),
        scratch_shapes=[pltpu.VMEM((TOP_K, td, HALF), U32), pltpu.SemaphoreType.DMA((TOP_K,))],
        compiler_params=_cparams(("arbitrary",)),
        name="moe_combine",
    )(pos.reshape(m // td, 1, TOP_K * td), x2, ps, y)


def _moe(x2, gain, w_router, w1, w3, w2, tm=1024):
    m = x2.shape[0]
    tm = min(tm, m)
    h, ids, ps = _router(x2, gain, w_router)
    pos, tile_e, n_used, nt = _route_positions(ids, tm)
    xs = _dispatch(h, pos, nt * tm)
    y = _experts(xs, tile_e, n_used, w1, w3, w2, tm)
    return _combine(x2, ps, pos, y)


def _relayout_w_in(w):
    k = w.shape[0]
    main = [w[:, _R_BRG:_R_END], w[:, _R_AQKV:_R_BETA], w[:, _R_AGATE:_R_B], w[:, _R_B:_R_C]]
    second = [w[:, _R_C:_R_BRG], w[:, _R_BETA:_R_AGATE],
              jnp.zeros((k, NP_C - OFF_BA - 2 * HEADS), w.dtype)]
    return (jnp.concatenate(main, axis=1).astype(BF16), jnp.concatenate(second, axis=1).astype(BF16))


def _layer_mixers(x2, b, t, layer, lower_bounds, w_in, norm_mix, conv_a, a_log, dt_bias,
                  gnorm_a, gnorm_b, qnorm_c, knorm_c, rel_bias, w_br_a, w_br_b, w_br_c, w_out):
    w_main, w_second = _relayout_w_in(w_in[layer])
    gain = norm_mix[layer].astype(F32)
    proj2 = _norm_proj(x2, gain, w_main, BF16)
    proj3 = proj2.reshape(b, t, NP_MAIN)
    qk_gain = jnp.concatenate([jnp.tile(qnorm_c[layer].astype(F32), (1, HEADS)).reshape(-1),
                               jnp.tile(knorm_c[layer].astype(F32), (1, HEADS)).reshape(-1)])
    projc3 = _norm_proj(x2, gain, w_second, F32, head_gain=qk_gain).reshape(b, t, NP_C)
    oa = _deltanet(proj3, projc3, conv_a[layer], a_log[layer], dt_bias[layer], gnorm_a[layer])
    ob = _hgrn(proj3, lower_bounds[layer], gnorm_b[layer])
    ocs, lses = [], []
    for gi in range(C_GROUPS):
        o, lse = _dilated_group(projc3, gi, rel_bias)
        ocs.append(o)
        lses.append(lse)
    return _mix(x2, proj2, oa.reshape(b * t, HW), ob.reshape(b * t, HW), ocs, lses,
                w_br_a[layer], w_br_b[layer], w_br_c[layer], w_out[layer])


def kernel(x, w_in, norm_mix, conv_a, a_log, dt_bias, gnorm_a, lb_logits, gnorm_b, qnorm_c, knorm_c, rel_bias, w_br_a, w_br_b, w_br_c, w_out, norm_ffn, ffn_w1, ffn_w3, ffn_w2, router, moe_w1, moe_w3, moe_w2):
    b, t, _ = x.shape
    depth = w_in.shape[0]
    p_lb = jax.nn.softmax(lb_logits.astype(F32), axis=0)
    lower_bounds = jnp.cumsum(p_lb, axis=0) - p_lb[0:1]
    x2 = x.reshape(b * t, D_MODEL).astype(F32)
    for layer in range(depth):
        x2 = _layer_mixers(x2, b, t, layer, lower_bounds, w_in, norm_mix, conv_a, a_log, dt_bias,
                           gnorm_a, gnorm_b, qnorm_c, knorm_c, rel_bias,
                           w_br_a, w_br_b, w_br_c, w_out)
        li = layer // 2
        if layer % 2 == 0:
            x2 = _ffn(x2, norm_ffn[layer], ffn_w1[li], ffn_w3[li], ffn_w2[li])
        else:
            x2 = _moe(x2, norm_ffn[layer], router[li], moe_w1[li], moe_w3[li], moe_w2[li])
    return x2.reshape(b, t, D_MODEL).astype(x.dtype)
```

```python
import functools
import math

import numpy as np
import jax
import jax.numpy as jnp
from jax import lax
from jax.experimental import pallas as pl
from jax.experimental.pallas import tpu as pltpu

F32 = jnp.float32
BF16 = jnp.bfloat16
HIGHEST = lax.Precision.HIGHEST

LANES = 128
SUBLANES = 8

D_MODEL = 1024
EPS = 1e-6
HEADS = 4
DH = 128
HW = HEADS * DH
A_CONV = 4
A_CHUNK = 64
A_ROWS = 512
B_ROWS = 512
B_BLK = 16
B_GROUP = 8
B_FAST = 32
HGRN_SAFE_DECAY = 60.0
C_PAIRS = ((128, 1), (512, 4), (2048, 16))
C_GROUPS = 3
C_BLOCK = 128
C_TILES = 4
C_NLB = 8
REL_BUCKETS = 32
REL_MAX_DIST = 2048
N_EXPERTS = 8
TOP_K = 2

OFF_BRG = 0
OFF_AQKV = 3072
OFF_AGATE = 4608
OFF_B = 5120
NP_MAIN = 7168
OFF_C = 0
OFF_BA = 4608
NP_C = 5120

_R_AQKV, _R_BETA, _R_AGATE, _R_B, _R_C, _R_BRG, _R_END = 0, 1536, 1544, 2056, 4104, 8712, 11784

VMEM_LIMIT = 56 * 1024 * 1024


def _cparams(sem):
    return pltpu.CompilerParams(dimension_semantics=sem, vmem_limit_bytes=VMEM_LIMIT)


def _sigmoid(x):
    return 1.0 / (1.0 + jnp.exp(-x))


def _silu(x):
    return x * (0.5 * jnp.tanh(0.5 * x) + 0.5)


def _softplus(x):
    return jnp.maximum(x, 0.0) + jnp.log(1.0 + jnp.exp(-jnp.abs(x)))


def _dot(a, b):
    return jnp.dot(a, b, preferred_element_type=F32)


def _dot_nt(a, b, precision=None):
    return lax.dot_general(a, b, (((1,), (1,)), ((), ())), precision=precision,
                           preferred_element_type=F32)


def _dot_tn(a, b):
    return lax.dot_general(a, b, (((0,), (0,)), ((), ())), preferred_element_type=F32)


def _norm_proj_kernel(tn, head_norm_cols, x_ref, g_ref, w_ref, hg_ref, o_ref):
    x = x_ref[...]
    ms = jnp.mean(x * x, axis=-1, keepdims=True)
    h = (x * lax.rsqrt(ms + EPS) * g_ref[...]).astype(BF16)
    for c0 in range(0, w_ref.shape[1], tn):
        r = _dot(h, w_ref[:, c0:c0 + tn])
        if c0 < head_norm_cols:
            heads = [r[:, d0:d0 + DH] for d0 in range(0, tn, DH)]
            heads = [a * lax.rsqrt(jnp.mean(a * a, axis=-1, keepdims=True) + EPS) for a in heads]
            r = jnp.concatenate(heads, axis=1) * hg_ref[:, c0:c0 + tn]
        o_ref[:, c0:c0 + tn] = r.astype(o_ref.dtype)


def _norm_proj(x2, gain, w_bf16, out_dtype, head_gain=None, tm=512, tn=1024):
    m = x2.shape[0]
    n = w_bf16.shape[1]
    tm = min(tm, m)
    norm_cols = 0 if head_gain is None else head_gain.shape[0]
    assert norm_cols % tn == 0
    hg = jnp.zeros((1, n), F32)
    if head_gain is not None:
        hg = hg.at[0, :norm_cols].set(head_gain.astype(F32))
    return pl.pallas_call(
        functools.partial(_norm_proj_kernel, tn, norm_cols),
        grid=(m // tm,),
        in_specs=[pl.BlockSpec((tm, D_MODEL), lambda i: (i, 0)),
                  pl.BlockSpec((1, D_MODEL), lambda i: (0, 0)),
                  pl.BlockSpec((D_MODEL, n), lambda i: (0, 0), pipeline_mode=pl.Buffered(1)),
                  pl.BlockSpec((1, n), lambda i: (0, 0))],
        out_specs=pl.BlockSpec((tm, n), lambda i: (i, 0)),
        out_shape=jax.ShapeDtypeStruct((m, n), out_dtype),
        compiler_params=_cparams(("parallel",)),
        name="norm_proj",
    )(x2, gain.reshape(1, D_MODEL), w_bf16, hg)


def _dotb(a, b):
    return _dot(a.astype(BF16), b.astype(BF16))


def _dot_exact_lhs(a_bf16, b):
    b0 = b.astype(BF16)
    r1 = b - b0.astype(F32)
    b1 = r1.astype(BF16)
    b2 = (r1 - b1.astype(F32)).astype(BF16)
    return _dot(a_bf16, b0) + (_dot(a_bf16, b1) + _dot(a_bf16, b2))


def _deltanet_kernel(qkv_ref, gate_ref, ba_ref, convw_ref, arow_ref, dtrow_ref, gn_ref,
                     o_ref, s_ref, xe_ref):
    C = A_CHUNK
    R = A_ROWS
    HS = range(HEADS)
    IT = range((R // C) * HEADS)

    @pl.when(pl.program_id(1) == 0)
    def _():
        s_ref[...] = jnp.zeros_like(s_ref)
        xe_ref[0:SUBLANES, :] = jnp.zeros((SUBLANES, 3 * HW), F32)

    x = qkv_ref[0].astype(F32)
    xe_ref[SUBLANES:SUBLANES + R, :] = x
    w = convw_ref[...]
    y = w[A_CONV - 1:A_CONV] * x
    for d in range(1, A_CONV):
        y = y + w[A_CONV - 1 - d:A_CONV - d] * xe_ref[SUBLANES - d:SUBLANES - d + R, :]
    xe_ref[0:SUBLANES, :] = x[R - SUBLANES:R]
    y = _silu(y)

    ba = ba_ref[0]
    beta_all = _sigmoid(ba)
    g_all = arow_ref[...] * _softplus(ba + dtrow_ref[...])
    gate = _silu(gate_ref[0].astype(F32))
    s_old = [s_ref[h] for h in HS]

    row = lax.broadcasted_iota(jnp.int32, (C, C), 0)
    col = lax.broadcasted_iota(jnp.int32, (C, C), 1)
    lmat = (col <= row).astype(BF16)
    rowx = lax.broadcasted_iota(jnp.int32, (C, DH + C), 0)
    colx = lax.broadcasted_iota(jnp.int32, (C, DH + C), 1)
    umask = (colx < DH) | (rowx > colx - DH)
    eye = (row == col).astype(F32)
    bd8 = (row >> 3) == (col >> 3)

    def merge_mask(sh):
        return (((row >> (sh + 1)) == (col >> (sh + 1)))
                & (((row >> sh) & 1) == 1) & (((col >> sh) & 1) == 0))

    rows = [slice((i // HEADS) * C, (i // HEADS + 1) * C) for i in IT]
    hd = [i % HEADS for i in IT]
    q = [y[rows[i], hd[i] * DH:(hd[i] + 1) * DH] for i in IT]
    k = [y[rows[i], HW + hd[i] * DH:HW + (hd[i] + 1) * DH] for i in IT]
    v = [y[rows[i], 2 * HW + hd[i] * DH:2 * HW + (hd[i] + 1) * DH] for i in IT]
    q = [a * (lax.rsqrt(jnp.sum(a * a, axis=-1, keepdims=True) + EPS) * (DH ** -0.5)) for a in q]
    k = [a * lax.rsqrt(jnp.sum(a * a, axis=-1, keepdims=True) + EPS) for a in k]
    beta = [beta_all[rows[i], hd[i]:hd[i] + 1] for i in IT]
    gb = [jnp.broadcast_to(g_all[rows[i], HEADS + hd[i]:HEADS + hd[i] + 1], (C, DH + C)) for i in IT]
    dext = [_dot_exact_lhs(lmat, jnp.where(umask, gb[i], 0.0)) for i in IT]
    gc = [d[:, :DH] for d in dext]
    edm = [jnp.exp(d[:, DH:]) for d in dext]
    egc = [jnp.exp(g) for g in gc]
    gl = [g[C - 1:C, :] for g in gc]
    kb = [k[i] * beta[i] for i in IT]
    m = [_dot_nt(kb[i].astype(BF16), k[i].astype(BF16)) * jnp.where(row > col, edm[i], 0.0)
         for i in IT]

    nd = [jnp.where(bd8, -a, 0.0) for a in m]
    p2 = [_dotb(a, a) for a in nd]
    p4 = [_dotb(a, a) for a in p2]
    x1 = [eye + nd[i] for i in IT]
    x1 = [x1[i] + _dotb(x1[i], p2[i]) for i in IT]
    xi = [x1[i] + _dotb(x1[i], p4[i]) for i in IT]
    for sh in (3, 4, 5):
        mm = merge_mask(sh)
        t = [_dotb(xi[i], jnp.where(mm, m[i], 0.0)) for i in IT]
        xi = [xi[i] - _dotb(t[i], xi[i]) for i in IT]

    rhs = [jnp.concatenate([v[i] * beta[i], kb[i] * egc[i]], axis=1) for i in IT]
    sol = [_dotb(xi[i], rhs[i]) for i in IT]
    attn = [(_dot_nt(q[i].astype(BF16), k[i].astype(BF16))
             * jnp.where(row >= col, edm[i], 0.0)).astype(BF16) for i in IT]
    qg = [(q[i] * egc[i]).astype(BF16) for i in IT]
    kg = [(k[i] * jnp.exp(gl[i] - gc[i])).astype(BF16) for i in IT]
    egl = [jnp.exp(a) for a in gl]

    s_cur = s_old
    for c in range(R // C):
        it = [c * HEADS + h for h in HS]
        sb = [a.astype(BF16) for a in s_cur]
        v_new = [(sol[i][:, :DH] - _dot(sol[i][:, DH:].astype(BF16), sb[h])).astype(BF16)
                 for h, i in enumerate(it)]
        o = [_dot(qg[i], sb[h]) + _dot(attn[i], v_new[h]) for h, i in enumerate(it)]
        s_cur = [s_cur[h] * egl[i] + _dot_tn(kg[i], v_new[h]) for h, i in enumerate(it)]
        o = [a * lax.rsqrt(jnp.mean(a * a, axis=-1, keepdims=True) + EPS) * gn_ref[...] for a in o]
        o_ref[0, c * C:(c + 1) * C, :] = (jnp.concatenate(o, axis=1)
                                          * gate[c * C:(c + 1) * C]).astype(o_ref.dtype)
    for h in HS:
        s_ref[h] = s_cur[h]


def _deltanet(proj3, projc3, conv_w, a_log, dt_bias, gnorm):
    b, t, _ = proj3.shape
    C = A_ROWS
    lane_row = lambda v: jnp.pad(v.astype(F32), (HEADS, LANES - 2 * HEADS)).reshape(1, LANES)
    arow = lane_row(-jnp.exp(a_log.astype(F32)))
    dtrow = lane_row(dt_bias)
    const = lambda shape: pl.BlockSpec(shape, lambda i, c: (0,) * len(shape))
    return pl.pallas_call(
        _deltanet_kernel,
        grid=(b, t // C),
        in_specs=[pl.BlockSpec((1, C, 3 * HW), lambda i, c: (i, c, OFF_AQKV // (3 * HW))),
                  pl.BlockSpec((1, C, HW), lambda i, c: (i, c, OFF_AGATE // HW)),
                  pl.BlockSpec((1, C, LANES), lambda i, c: (i, c, OFF_BA // LANES)),
                  const((A_CONV, 3 * HW)), const((1, LANES)), const((1, LANES)), const((1, DH))],
        out_specs=pl.BlockSpec((1, C, HW), lambda i, c: (i, c, 0)),
        out_shape=jax.ShapeDtypeStruct((b, t, HW), BF16),
        scratch_shapes=[pltpu.VMEM((HEADS, DH, DH), F32), pltpu.VMEM((C + SUBLANES, 3 * HW), F32)],
        compiler_params=_cparams(("parallel", "arbitrary")),
        name="deltanet",
    )(proj3, proj3, projc3, conv_w.astype(F32), arow, dtrow, gnorm.reshape(1, DH).astype(F32))


def _hgrn_kernel(q_ref, f_ref, i_ref, g_ref, lb_ref, gn_ref, o_ref, st_ref, lg_ref, k_ref):
    R, K = B_ROWS, B_BLK
    HS = range(HEADS)

    @pl.when(pl.program_id(1) == 0)
    def _():
        st_ref[...] = jnp.zeros_like(st_ref)

    lb_all = lb_ref[...]
    fl = f_ref[0].astype(F32)
    lg = jnp.log(lb_all + (1.0 - lb_all) * _sigmoid(fl))
    k_ref[...] = (1.0 - lb_all) * _sigmoid(-fl)
    lg_ref[...] = lg
    brow = lax.broadcasted_iota(jnp.int32, (R // B_FAST, R), 0)
    bcol = lax.broadcasted_iota(jnp.int32, (R // B_FAST, R), 1)
    chunksum = _dot((bcol // B_FAST == brow).astype(BF16), lg.astype(BF16))
    max_decay = jnp.max(-chunksum)

    H8 = SUBLANES
    row8 = lax.broadcasted_iota(jnp.int32, (H8, DH), 0)
    sls = [slice(h * DH, (h + 1) * DH) for h in HS]

    def pairs_exact(q, k, v, gc):
        n_it = range(len(q))
        otile = [[jnp.zeros((H8, DH), F32) for _ in range(K // H8)] for _ in n_it]
        for j in range(K):
            for t in range(j // H8, K // H8):
                ts = slice(t * H8, (t + 1) * H8)
                for i in n_it:
                    d = gc[i][ts] - gc[i][j:j + 1]
                    rel = jnp.exp(jnp.where(row8 >= j - t * H8, d, -1e30) if t == j // H8 else d)
                    sj = jnp.sum(q[i][ts] * k[i][j:j + 1] * rel, axis=-1, keepdims=True)
                    otile[i][t] = otile[i][t] + sj * v[i][j:j + 1]
        return [jnp.concatenate(otile[i], axis=0) for i in n_it]

    def run(exact):
        KB = K if exact else B_FAST
        GR = B_GROUP * KB
        grow = lax.broadcasted_iota(jnp.int32, (GR, GR), 0)
        gcol = lax.broadcasted_iota(jnp.int32, (GR, GR), 1)
        same_block_lower = (gcol <= grow) & (grow // KB == gcol // KB)
        lmat_g = same_block_lower.astype(BF16)
        st = [st_ref[h] for h in HS]
        for r0 in range(0, R, GR):
            gs = slice(r0, r0 + GR)
            blks = [slice(bi * KB, (bi + 1) * KB) for bi in range(B_GROUP)]
            gc_all = _dot_exact_lhs(lmat_g, lg_ref[gs, :])
            gl_all = jnp.concatenate([jnp.broadcast_to(gc_all[bs.stop - 1:bs.stop], (KB, HW)) for bs in blks],
                                     axis=0)
            q = [_silu(q_ref[0, gs, sl].astype(F32)) for sl in sls]
            k = [k_ref[gs, sl] for sl in sls]
            v = [i_ref[0, gs, sl].astype(F32) for sl in sls]
            gc = [gc_all[:, sl] for sl in sls]
            gl = [gl_all[:, sl] for sl in sls]
            kg = [(k[h] * jnp.exp(gl[h] - gc[h])).astype(BF16) for h in HS]
            qs = [(q[h] * jnp.exp(gc[h])).astype(BF16) for h in HS]
            vb = [a.astype(BF16) for a in v]
            upd = [[_dot_tn(vb[h][bs], kg[h][bs]) for h in HS] for bs in blks]
            egl = [[jnp.exp(gl[h][bs.stop - 1:bs.stop]) for h in HS] for bs in blks]
            if exact:
                items = [(h, bs) for bs in blks for h in HS]
                pe = pairs_exact([q[h][bs] for h, bs in items], [k[h][bs] for h, bs in items],
                                 [v[h][bs] for h, bs in items], [gc[h][bs] for h, bs in items])
                intra = [jnp.concatenate([pe[bi * HEADS + h] for bi in range(B_GROUP)], axis=0) for h in HS]
            else:
                qf = [(q[h] * jnp.exp(gc[h] - gl[h])).astype(BF16) for h in HS]
                s = [jnp.where(same_block_lower, _dot_nt(qf[h], kg[h]), 0.0).astype(BF16) for h in HS]
                intra = [_dot(s[h], vb[h]) for h in HS]
            for bi, bs in enumerate(blks):
                o = [_dot_nt(qs[h][bs], st[h].astype(BF16)) + intra[h][bs] for h in HS]
                st = [st[h] * egl[bi][h] + upd[bi][h] for h in HS]
                o = [a * lax.rsqrt(jnp.mean(a * a, axis=-1, keepdims=True) + EPS) * gn_ref[...] for a in o]
                rs = slice(r0 + bi * KB, r0 + (bi + 1) * KB)
                o_ref[0, rs, :] = (jnp.concatenate(o, axis=1)
                                   * _sigmoid(g_ref[0, rs, :].astype(F32))).astype(o_ref.dtype)
        for h in HS:
            st_ref[h] = st[h]

    lax.cond(max_decay > HGRN_SAFE_DECAY, lambda: run(True), lambda: run(False))


def _hgrn(proj3, lb, gnorm):
    b, t, _ = proj3.shape
    R = B_ROWS
    seg = lambda s: pl.BlockSpec((1, R, HW), lambda i, c: (i, c, OFF_B // HW + s))
    const = lambda shape: pl.BlockSpec(shape, lambda i, c: (0,) * len(shape))
    return pl.pallas_call(
        _hgrn_kernel,
        grid=(b, t // R),
        in_specs=[seg(0), seg(1), seg(2), seg(3), const((1, HW)), const((1, DH))],
        out_specs=pl.BlockSpec((1, R, HW), lambda i, c: (i, c, 0)),
        out_shape=jax.ShapeDtypeStruct((b, t, HW), BF16),
        scratch_shapes=[pltpu.VMEM((HEADS, DH, DH), F32), pltpu.VMEM((R, HW), F32),
                        pltpu.VMEM((R, HW), F32)],
        compiler_params=_cparams(("parallel", "arbitrary")),
        name="hgrn2",
    )(proj3, proj3, proj3, proj3, lb.reshape(1, HW).astype(F32), gnorm.reshape(1, DH).astype(F32))


def _t5_bucket_np(n):
    max_exact = REL_BUCKETS // 2
    nf = np.maximum(n, 1).astype(np.float32)
    large = max_exact + (np.log(nf / max_exact) / math.log(REL_MAX_DIST / max_exact)
                         * (REL_BUCKETS - max_exact)).astype(np.int32)
    large = np.minimum(large, REL_BUCKETS - 1)
    return np.where(n < max_exact, n, large)


def _dilated_kernel(span, dil, hpb, nlb, q_ref, kp_ref, kc_ref, vp_ref, vc_ref, bias_ref,
                    o_ref, lse_ref):
    CB = C_BLOCK
    n = pl.program_id(1)
    qi = lax.broadcasted_iota(jnp.int32, (CB, 2 * CB), 0)
    kj = lax.broadcasted_iota(jnp.int32, (CB, 2 * CB), 1)
    dist = qi + CB - kj
    band = (dist >= 0) & (dist <= span)
    band_first = band & ((kj >= CB) | (n > 0))

    def rows(r, jb):
        return pl.ds(jb * CB * dil + r, CB, stride=dil) if dil > 1 else pl.ds(jb * CB, CB)

    def with_prev(p_ref, c_ref, r, jb, sl):
        prev = p_ref[0, rows(r, 0), sl] if jb == 0 else c_ref[0, rows(r, jb - 1), sl]
        return jnp.concatenate([prev, c_ref[0, rows(r, jb), sl]], axis=0)

    def tiles(items):
        n_it = range(len(items))
        q = [q_ref[0, rows(r, jb), sl].astype(BF16) for r, jb, sl, _ in items]
        k = [with_prev(kp_ref, kc_ref, r, jb, sl).astype(BF16) for r, jb, sl, _ in items]
        v = [with_prev(vp_ref, vc_ref, r, jb, sl).astype(BF16) for r, jb, sl, _ in items]
        s = [_dot_nt(q[i], k[i]) * (DH ** -0.5) + bias_ref[items[i][3]] for i in n_it]
        s = [jnp.where(band_first if items[i][1] == 0 else band, s[i], -1e30) for i in n_it]
        mx = [jnp.max(a, axis=-1, keepdims=True) for a in s]
        p = [jnp.exp(s[i] - mx[i]) for i in n_it]
        den = [jnp.sum(a, axis=-1, keepdims=True) for a in p]
        o = [_dot(p[i].astype(BF16), v[i]) / den[i] for i in n_it]
        for i, (r, jb, sl, _) in enumerate(items):
            o_ref[0, rows(r, jb), sl] = o[i]
            lse_ref[0, rows(r, jb), sl] = jnp.broadcast_to(mx[i] + jnp.log(den[i]), (CB, DH))

    for jb in range(nlb):
        if dil == 1:
            for h0 in range(0, hpb, C_TILES):
                tiles([(0, jb, slice(h * DH, (h + 1) * DH), h) for h in range(h0, h0 + C_TILES)])
        elif dil == C_TILES:
            tiles([(r, jb, slice(0, DH), 0) for r in range(dil)])
        else:
            def body(g, carry, jb=jb):
                tiles([(g * C_TILES + i, jb, slice(0, DH), 0) for i in range(C_TILES)])
                return carry
            lax.fori_loop(0, dil // C_TILES, body, 0)


def _dilated_group(proj3, gi, rel_bias):
    b, t, _ = proj3.shape
    window, dil = C_PAIRS[gi]
    span = window // dil
    CB = C_BLOCK
    pb = CB * dil
    nlb = min(C_NLB, t // pb)
    rb = nlb * pb
    hpb = HEADS if dil == 1 else 1
    bw = hpb * DH
    qi = np.arange(CB)[:, None]
    kj = np.arange(2 * CB)[None, :]
    bucket = _t5_bucket_np(np.maximum(qi + CB - kj, 0) * dil)
    onehot = jnp.asarray(np.eye(REL_BUCKETS, dtype=np.float32)[bucket])
    bias = jnp.einsum("qkb,bh->hqk", onehot, rel_bias[:, gi * HEADS:(gi + 1) * HEADS].astype(F32),
                      precision=HIGHEST)

    def seg(which, prev):
        base = (OFF_C + which * C_GROUPS * HW + gi * HW) // bw
        if prev:
            return pl.BlockSpec((1, pb, bw), lambda i, n, h: (i, jnp.maximum(n * nlb - 1, 0), base + h))
        return pl.BlockSpec((1, rb, bw), lambda i, n, h: (i, n, base + h))

    o, lse = pl.pallas_call(
        functools.partial(_dilated_kernel, span, dil, hpb, nlb),
        grid=(b, t // rb, HEADS // hpb),
        in_specs=[seg(0, False), seg(1, True), seg(1, False), seg(2, True), seg(2, False),
                  pl.BlockSpec((hpb, CB, 2 * CB), lambda i, n, h: (h, 0, 0))],
        out_specs=[pl.BlockSpec((1, rb, bw), lambda i, n, h: (i, n, h)),
                   pl.BlockSpec((1, rb, bw), lambda i, n, h: (i, n, h))],
        out_shape=[jax.ShapeDtypeStruct((b, t, HW), F32),
                   jax.ShapeDtypeStruct((b, t, HW), F32)],
        compiler_params=_cparams(("parallel", "arbitrary", "arbitrary")),
        name=f"dilated_g{gi}",
    )(proj3, proj3, proj3, proj3, proj3, bias)
    return o.reshape(b * t, HW), lse.reshape(b * t, HW)


def _mix_kernel(x_ref, gate_ref, oa_ref, ob_ref, oc0_ref, oc1_ref, oc2_ref,
                l0_ref, l1_ref, l2_ref, wa_ref, wb_ref, wc_ref, wo_ref, out_ref):
    l0, l1, l2 = l0_ref[...], l1_ref[...], l2_ref[...]
    mx = jnp.maximum(jnp.maximum(l0, l1), l2)
    e0, e1, e2 = jnp.exp(l0 - mx), jnp.exp(l1 - mx), jnp.exp(l2 - mx)
    oc = (e0 * oc0_ref[...] + e1 * oc1_ref[...] + e2 * oc2_ref[...]) / (e0 + e1 + e2)
    mix = (_sigmoid(gate_ref[:, 0:D_MODEL].astype(F32)) * _dot(oa_ref[...], wa_ref[...])
           + _sigmoid(gate_ref[:, D_MODEL:2 * D_MODEL].astype(F32)) * _dot(ob_ref[...], wb_ref[...])
           + _sigmoid(gate_ref[:, 2 * D_MODEL:3 * D_MODEL].astype(F32)) * _dot(oc.astype(BF16), wc_ref[...]))
    out_ref[...] = x_ref[...] + _dot(mix.astype(BF16), wo_ref[...])


def _mix(x2, proj2, oa, ob, ocs, lses, wa, wb, wc, wo, tm=512):
    m = x2.shape[0]
    tm = min(tm, m)
    rowblk = lambda w: pl.BlockSpec((tm, w), lambda i: (i, 0))
    const = lambda shape: pl.BlockSpec(shape, lambda i: (0,) * len(shape))
    return pl.pallas_call(
        _mix_kernel,
        grid=(m // tm,),
        in_specs=[rowblk(D_MODEL),
                  pl.BlockSpec((tm, 3 * D_MODEL), lambda i: (i, OFF_BRG // (3 * D_MODEL))),
                  rowblk(HW), rowblk(HW),
                  rowblk(HW), rowblk(HW), rowblk(HW), rowblk(HW), rowblk(HW), rowblk(HW),
                  const((HW, D_MODEL)), const((HW, D_MODEL)), const((HW, D_MODEL)),
                  const((D_MODEL, D_MODEL))],
        out_specs=rowblk(D_MODEL),
        out_shape=jax.ShapeDtypeStruct((m, D_MODEL), F32),
        compiler_params=_cparams(("parallel",)),
        name="branch_mix",
    )(x2, proj2, oa, ob, ocs[0], ocs[1], ocs[2], lses[0], lses[1], lses[2],
      wa.astype(BF16), wb.astype(BF16), wc.astype(BF16), wo.astype(BF16))


def _ffn_kernel(x_ref, g_ref, w1_ref, w3_ref, w2_ref, o_ref):
    x = x_ref[...]
    ms = jnp.mean(x * x, axis=-1, keepdims=True)
    h = (x * lax.rsqrt(ms + EPS) * g_ref[...]).astype(BF16)
    a = _dot(h, w1_ref[...])
    b = _dot(h, w3_ref[...])
    o_ref[...] = x + _dot((_silu(a) * b).astype(BF16), w2_ref[...])


def _ffn(x2, gain, w1, w3, w2, tm=512):
    m = x2.shape[0]
    ff = w1.shape[1]
    tm = min(tm, m)
    resident = lambda shape: pl.BlockSpec(shape, lambda i: (0, 0), pipeline_mode=pl.Buffered(1))
    return pl.pallas_call(
        _ffn_kernel,
        grid=(m // tm,),
        in_specs=[pl.BlockSpec((tm, D_MODEL), lambda i: (i, 0)),
                  pl.BlockSpec((1, D_MODEL), lambda i: (0, 0)),
                  resident((D_MODEL, ff)), resident((D_MODEL, ff)), resident((ff, D_MODEL))],
        out_specs=pl.BlockSpec((tm, D_MODEL), lambda i: (i, 0)),
        out_shape=jax.ShapeDtypeStruct((m, D_MODEL), F32),
        compiler_params=_cparams(("parallel",)),
        name="ffn",
    )(x2, gain.reshape(1, D_MODEL).astype(F32), w1.astype(BF16), w3.astype(BF16), w2.astype(BF16))


HALF = D_MODEL // 2
U32 = jnp.uint32


def _pack_bf16_pairs(x):
    r = x.astype(BF16).astype(F32)
    lo = lax.bitcast_convert_type(r[:, :HALF], U32) >> 16
    hi = lax.bitcast_convert_type(r[:, HALF:], U32)
    return hi | lo


def _unpack_bf16_pairs(p):
    lo = lax.bitcast_convert_type(p << 16, F32)
    hi = lax.bitcast_convert_type(p & U32(0xFFFF0000), F32)
    return lo, hi


def _router_kernel(x_ref, g_ref, wr_ref, h_ref, ids_ref, ps_ref):
    x = x_ref[...]
    ms = jnp.mean(x * x, axis=-1, keepdims=True)
    h = x * lax.rsqrt(ms + EPS) * g_ref[...]
    h_ref[...] = _pack_bf16_pairs(h)
    hh = h.astype(BF16)
    hl = (h - hh.astype(F32)).astype(BF16)
    w = wr_ref[...]
    wh = w.astype(BF16)
    wl = (w - wh.astype(F32)).astype(BF16)
    logits = _dot(hh, wh) + (_dot(hh, wl) + _dot(hl, wh))
    lane = lax.broadcasted_iota(jnp.int32, logits.shape, 1)
    neg = jnp.float32(-jnp.inf)
    l1 = jnp.where(lane < N_EXPERTS, logits, neg)
    m1 = jnp.max(l1, axis=-1, keepdims=True)
    i1 = jnp.min(jnp.where(l1 == m1, lane, LANES), axis=-1, keepdims=True)
    l2 = jnp.where(lane == i1, neg, l1)
    m2 = jnp.max(l2, axis=-1, keepdims=True)
    i2 = jnp.min(jnp.where(l2 == m2, lane, LANES), axis=-1, keepdims=True)
    e = jnp.exp(m2 - m1)
    p1 = 1.0 / (1.0 + e)
    p2 = e / (1.0 + e)
    ids_ref[...] = jnp.where(lane == 0, i1, jnp.where(lane == 1, i2, 0))
    ps_ref[...] = jnp.where(lane == 0, p1, jnp.where(lane == 1, p2, 0.0))


def _router(x2, gain, w_router, tm=1024):
    m = x2.shape[0]
    tm = min(tm, m)
    wr = jnp.pad(w_router.astype(F32), ((0, 0), (0, LANES - N_EXPERTS)))
    return pl.pallas_call(
        _router_kernel,
        grid=(m // tm,),
        in_specs=[pl.BlockSpec((tm, D_MODEL), lambda i: (i, 0)),
                  pl.BlockSpec((1, D_MODEL), lambda i: (0, 0)),
                  pl.BlockSpec((D_MODEL, LANES), lambda i: (0, 0))],
        out_specs=[pl.BlockSpec((tm, HALF), lambda i: (i, 0)),
                   pl.BlockSpec((tm, LANES), lambda i: (i, 0)),
                   pl.BlockSpec((tm, LANES), lambda i: (i, 0))],
        out_shape=[jax.ShapeDtypeStruct((m, HALF), U32),
                   jax.ShapeDtypeStruct((m, LANES), jnp.int32),
                   jax.ShapeDtypeStruct((m, LANES), F32)],
        compiler_params=_cparams(("parallel",)),
        name="router",
    )(x2, gain.reshape(1, D_MODEL).astype(F32), wr)


def _route_positions(ids, tm):
    m = ids.shape[0]
    e_flat = ids[:, :TOP_K].reshape(-1)
    onehot = (e_flat[:, None] == jnp.arange(N_EXPERTS)[None, :]).astype(jnp.int32)
    csum = jnp.cumsum(onehot, axis=0)
    counts = csum[-1]
    gsz = ((counts + tm - 1) // tm) * tm
    gend = jnp.cumsum(gsz)
    pos = jnp.sum(onehot * (gend - gsz + csum - 1), axis=1)
    nt = (TOP_K * m + N_EXPERTS * tm) // tm
    n_used = gend[-1] // tm
    tile_e = jnp.sum((jnp.arange(nt)[:, None] * tm >= gend[None, :]).astype(jnp.int32), axis=1)
    last_e = jnp.sum(((n_used - 1) * tm >= gend).astype(jnp.int32))
    tile_e = jnp.minimum(tile_e, last_e)
    return pos.astype(jnp.int32), tile_e.astype(jnp.int32), n_used.reshape(1).astype(jnp.int32), nt


def _dispatch_kernel(pos_ref, h_ref, init_ref, xs_ref, sem):
    del init_ref
    td = h_ref.shape[0]

    def row_copy(t, s):
        return pltpu.make_async_copy(h_ref.at[pl.ds(t, 1)],
                                     xs_ref.at[pl.ds(pos_ref[0, 0, TOP_K * t + s], 1)], sem)

    def start(t, c):
        for s in range(TOP_K):
            row_copy(t, s).start(priority=s)
        return c

    lax.fori_loop(0, td, start, 0, unroll=8)
    for s in range(TOP_K):
        pltpu.make_async_copy(h_ref, xs_ref.at[pl.ds(0, td)], sem).wait()


def _dispatch(h, pos, npad, td=1024):
    m = h.shape[0]
    td = min(td, m)
    return pl.pallas_call(
        _dispatch_kernel,
        grid=(m // td,),
        in_specs=[pl.BlockSpec((1, 1, TOP_K * td), lambda i: (i, 0, 0), memory_space=pltpu.SMEM),
                  pl.BlockSpec((td, HALF), lambda i: (i, 0)),
                  pl.BlockSpec(memory_space=pl.ANY)],
        out_specs=pl.BlockSpec(memory_space=pl.ANY),
        out_shape=jax.ShapeDtypeStruct((npad, HALF), U32),
        scratch_shapes=[pltpu.SemaphoreType.DMA(())],
        input_output_aliases={2: 0},
        compiler_params=_cparams(("arbitrary",)),
        name="moe_dispatch",
    )(pos.reshape(m // td, 1, TOP_K * td), h, jnp.zeros((npad, HALF), U32))


def _experts_kernel(te_ref, nu_ref, xs_ref, w1_ref, w3_ref, w2_ref, y_ref, xb_ref, acc_ref):
    del te_ref
    i = pl.program_id(0)
    j = pl.program_id(1)

    @pl.when(i < nu_ref[0])
    def _():
        @pl.when(j == 0)
        def _():
            lo, hi = _unpack_bf16_pairs(xs_ref[...])
            xb_ref[...] = jnp.concatenate([lo.astype(BF16), hi.astype(BF16)], axis=1)
            acc_ref[...] = jnp.zeros_like(acc_ref)

        xb = xb_ref[...]
        a = _dot(xb, w1_ref[0])
        b = _dot(xb, w3_ref[0])
        acc_ref[...] += _dot((_silu(a) * b).astype(BF16), w2_ref[0])

        @pl.when(j == pl.num_programs(1) - 1)
        def _():
            y_ref[...] = _pack_bf16_pairs(acc_ref[...])

    @pl.when((i >= nu_ref[0]) & (j == 0))
    def _():
        y_ref[...] = jnp.zeros_like(y_ref)


def _experts(xs, tile_e, n_used, w1, w3, w2, tm, tf=1792):
    npad = xs.shape[0]
    ff = w1.shape[2]
    nj = ff // tf
    row = lambda i, j, te, nu: (jnp.minimum(i, nu[0] - 1), 0)
    jj = lambda i, j, nu: jnp.where(i < nu[0], j, nj - 1)
    return pl.pallas_call(
        _experts_kernel,
        grid_spec=pltpu.PrefetchScalarGridSpec(
            num_scalar_prefetch=2,
            grid=(npad // tm, nj),
            in_specs=[pl.BlockSpec((tm, HALF), row),
                      pl.BlockSpec((1, D_MODEL, tf), lambda i, j, te, nu: (te[i], 0, jj(i, j, nu))),
                      pl.BlockSpec((1, D_MODEL, tf), lambda i, j, te, nu: (te[i], 0, jj(i, j, nu))),
                      pl.BlockSpec((1, tf, D_MODEL), lambda i, j, te, nu: (te[i], jj(i, j, nu), 0))],
            out_specs=pl.BlockSpec((tm, HALF), lambda i, j, te, nu: (i, 0)),
            scratch_shapes=[pltpu.VMEM((tm, D_MODEL), BF16), pltpu.VMEM((tm, D_MODEL), F32)]),
        out_shape=jax.ShapeDtypeStruct((npad, HALF), U32),
        compiler_params=_cparams(("arbitrary", "arbitrary")),
        name="moe_experts",
    )(tile_e, n_used, xs, w1.astype(BF16), w3.astype(BF16), w2.astype(BF16))


def _combine_kernel(pos_ref, x_ref, ps_ref, y_ref, o_ref, buf_ref, sem):
    td = x_ref.shape[0]

    def row_copy(t, s):
        return pltpu.make_async_copy(y_ref.at[pl.ds(pos_ref[0, 0, TOP_K * t + s], 1)],
                                     buf_ref.at[s, pl.ds(t, 1)], sem.at[s])

    def start(t, c):
        for s in range(TOP_K):
            row_copy(t, s).start(priority=s)
        return c

    lax.fori_loop(0, td, start, 0, unroll=8)
    for s in range(TOP_K):
        pltpu.make_async_copy(y_ref.at[pl.ds(0, td)], buf_ref.at[s], sem.at[s]).wait()
    ps = ps_ref[...]
    lo0, hi0 = _unpack_bf16_pairs(buf_ref[0])
    lo1, hi1 = _unpack_bf16_pairs(buf_ref[1])
    o_ref[:, :HALF] = x_ref[:, :HALF] + ps[:, 0:1] * lo0 + ps[:, 1:2] * lo1
    o_ref[:, HALF:] = x_ref[:, HALF:] + ps[:, 0:1] * hi0 + ps[:, 1:2] * hi1


def _combine(x2, ps, pos, y, td=1024):
    m = x2.shape[0]
    td = min(td, m)
    return pl.pallas_call(
        _combine_kernel,
        grid=(m // td,),
        in_specs=[pl.BlockSpec((1, 1, TOP_K * td), lambda i: (i, 0, 0), memory_space=pltpu.SMEM),
                  pl.BlockSpec((td, D_MODEL), lambda i: (i, 0)),
                  pl.BlockSpec((td, LANES), lambda i: (i, 0)),
                  pl.BlockSpec(memory_space=pl.ANY)],
        out_specs=pl.BlockSpec((td, D_MODEL), lambda i: (i, 0)),
        out_shape=jax.ShapeDtypeStruct((m, D_MODEL), F32),
        scratch_shapes=[pltpu.VMEM((TOP_K, td, HALF), U32), pltpu.SemaphoreType.DMA((TOP_K,))],
        compiler_params=_cparams(("arbitrary",)),
        name="moe_combine",
    )(pos.reshape(m // td, 1, TOP_K * td), x2, ps, y)


def _moe(x2, gain, w_router, w1, w3, w2, tm=1024):
    m = x2.shape[0]
    tm = min(tm, m)
    h, ids, ps = _router(x2, gain, w_router)
    pos, tile_e, n_used, nt = _route_positions(ids, tm)
    xs = _dispatch(h, pos, nt * tm)
    y = _experts(xs, tile_e, n_used, w1, w3, w2, tm)
    return _combine(x2, ps, pos, y)


def _relayout_w_in(w):
    k = w.shape[0]
    main = [w[:, _R_BRG:_R_END], w[:, _R_AQKV:_R_BETA], w[:, _R_AGATE:_R_B], w[:, _R_B:_R_C]]
    second = [w[:, _R_C:_R_BRG], w[:, _R_BETA:_R_AGATE],
              jnp.zeros((k, NP_C - OFF_BA - 2 * HEADS), w.dtype)]
    return (jnp.concatenate(main, axis=1).astype(BF16), jnp.concatenate(second, axis=1).astype(BF16))


def _layer_mixers(x2, b, t, layer, lower_bounds, w_in, norm_mix, conv_a, a_log, dt_bias,
                  gnorm_a, gnorm_b, qnorm_c, knorm_c, rel_bias, w_br_a, w_br_b, w_br_c, w_out):
    w_main, w_second = _relayout_w_in(w_in[layer])
    gain = norm_mix[layer].astype(F32)
    proj2 = _norm_proj(x2, gain, w_main, BF16)
    proj3 = proj2.reshape(b, t, NP_MAIN)
    qk_gain = jnp.concatenate([jnp.tile(qnorm_c[layer].astype(F32), (1, HEADS)).reshape(-1),
                               jnp.tile(knorm_c[layer].astype(F32), (1, HEADS)).reshape(-1)])
    projc3 = _norm_proj(x2, gain, w_second, F32, head_gain=qk_gain).reshape(b, t, NP_C)
    oa = _deltanet(proj3, projc3, conv_a[layer], a_log[layer], dt_bias[layer], gnorm_a[layer])
    ob = _hgrn(proj3, lower_bounds[layer], gnorm_b[layer])
    ocs, lses = [], []
    for gi in range(C_GROUPS):
        o, lse = _dilated_group(projc3, gi, rel_bias)
        ocs.append(o)
        lses.append(lse)
    return _mix(x2, proj2, oa.reshape(b * t, HW), ob.reshape(b * t, HW), ocs, lses,
                w_br_a[layer], w_br_b[layer], w_br_c[layer], w_out[layer])


def kernel(x, w_in, norm_mix, conv_a, a_log, dt_bias, gnorm_a, lb_logits, gnorm_b, qnorm_c, knorm_c, rel_bias, w_br_a, w_br_b, w_br_c, w_out, norm_ffn, ffn_w1, ffn_w3, ffn_w2, router, moe_w1, moe_w3, moe_w2):
    b, t, _ = x.shape
    depth = w_in.shape[0]
    p_lb = jax.nn.softmax(lb_logits.astype(F32), axis=0)
    lower_bounds = jnp.cumsum(p_lb, axis=0) - p_lb[0:1]
    x2 = x.reshape(b * t, D_MODEL).astype(F32)
    for layer in range(depth):
        x2 = _layer_mixers(x2, b, t, layer, lower_bounds, w_in, norm_mix, conv_a, a_log, dt_bias,
                           gnorm_a, gnorm_b, qnorm_c, knorm_c, rel_bias,
                           w_br_a, w_br_b, w_br_c, w_out)
        li = layer // 2
        if layer % 2 == 0:
            x2 = _ffn(x2, norm_ffn[layer], ffn_w1[li], ffn_w3[li], ffn_w2[li])
        else:
            x2 = _moe(x2, norm_ffn[layer], router[li], moe_w1[li], moe_w3[li], moe_w2[li])
    return x2.reshape(b, t, D_MODEL).astype(x.dtype)
```

```python
import functools
import math

import numpy as np
import jax
import jax.numpy as jnp
from jax import lax
from jax.experimental import pallas as pl
from jax.experimental.pallas import tpu as pltpu

F32 = jnp.float32
BF16 = jnp.bfloat16
HIGHEST = lax.Precision.HIGHEST

LANES = 128
SUBLANES = 8

D_MODEL = 1024
EPS = 1e-6
HEADS = 4
DH = 128
HW = HEADS * DH
A_CONV = 4
A_CHUNK = 64
A_ROWS = 512
B_ROWS = 512
B_BLK = 16
B_GROUP = 8
B_FAST = 32
HGRN_SAFE_DECAY = 60.0
C_PAIRS = ((128, 1), (512, 4), (2048, 16))
C_GROUPS = 3
C_BLOCK = 128
C_TILES = 4
C_NLB = 8
REL_BUCKETS = 32
REL_MAX_DIST = 2048
N_EXPERTS = 8
TOP_K = 2

OFF_BRG = 0
OFF_AQKV = 3072
OFF_AGATE = 4608
OFF_B = 5120
NP_MAIN = 7168
OFF_C = 0
OFF_BA = 4608
NP_C = 5120

_R_AQKV, _R_BETA, _R_AGATE, _R_B, _R_C, _R_BRG, _R_END = 0, 1536, 1544, 2056, 4104, 8712, 11784

VMEM_LIMIT = 56 * 1024 * 1024


def _cparams(sem):
    return pltpu.CompilerParams(dimension_semantics=sem, vmem_limit_bytes=VMEM_LIMIT)


def _sigmoid(x):
    return 1.0 / (1.0 + jnp.exp(-x))


def _silu(x):
    return x * (0.5 * jnp.tanh(0.5 * x) + 0.5)


def _softplus(x):
    return jnp.maximum(x, 0.0) + jnp.log(1.0 + jnp.exp(-jnp.abs(x)))


def _dot(a, b):
    return jnp.dot(a, b, preferred_element_type=F32)


def _dot_nt(a, b, precision=None):
    return lax.dot_general(a, b, (((1,), (1,)), ((), ())), precision=precision,
                           preferred_element_type=F32)


def _dot_tn(a, b):
    return lax.dot_general(a, b, (((0,), (0,)), ((), ())), preferred_element_type=F32)


def _norm_proj_kernel(tn, head_norm_cols, x_ref, g_ref, w_ref, hg_ref, o_ref):
    x = x_ref[...]
    ms = jnp.mean(x * x, axis=-1, keepdims=True)
    h = (x * lax.rsqrt(ms + EPS) * g_ref[...]).astype(BF16)
    for c0 in range(0, w_ref.shape[1], tn):
        r = _dot(h, w_ref[:, c0:c0 + tn])
        if c0 < head_norm_cols:
            heads = [r[:, d0:d0 + DH] for d0 in range(0, tn, DH)]
            heads = [a * lax.rsqrt(jnp.mean(a * a, axis=-1, keepdims=True) + EPS) for a in heads]
            r = jnp.concatenate(heads, axis=1) * hg_ref[:, c0:c0 + tn]
        o_ref[:, c0:c0 + tn] = r.astype(o_ref.dtype)


def _norm_proj(x2, gain, w_bf16, out_dtype, head_gain=None, tm=512, tn=1024):
    m = x2.shape[0]
    n = w_bf16.shape[1]
    tm = min(tm, m)
    norm_cols = 0 if head_gain is None else head_gain.shape[0]
    assert norm_cols % tn == 0
    hg = jnp.zeros((1, n), F32)
    if head_gain is not None:
        hg = hg.at[0, :norm_cols].set(head_gain.astype(F32))
    return pl.pallas_call(
        functools.partial(_norm_proj_kernel, tn, norm_cols),
        grid=(m // tm,),
        in_specs=[pl.BlockSpec((tm, D_MODEL), lambda i: (i, 0)),
                  pl.BlockSpec((1, D_MODEL), lambda i: (0, 0)),
                  pl.BlockSpec((D_MODEL, n), lambda i: (0, 0), pipeline_mode=pl.Buffered(1)),
                  pl.BlockSpec((1, n), lambda i: (0, 0))],
        out_specs=pl.BlockSpec((tm, n), lambda i: (i, 0)),
        out_shape=jax.ShapeDtypeStruct((m, n), out_dtype),
        compiler_params=_cparams(("parallel",)),
        name="norm_proj",
    )(x2, gain.reshape(1, D_MODEL), w_bf16, hg)


def _dotb(a, b):
    return _dot(a.astype(BF16), b.astype(BF16))


def _dot_exact_lhs(a_bf16, b):
    b0 = b.astype(BF16)
    r1 = b - b0.astype(F32)
    b1 = r1.astype(BF16)
    b2 = (r1 - b1.astype(F32)).astype(BF16)
    return _dot(a_bf16, b0) + (_dot(a_bf16, b1) + _dot(a_bf16, b2))


def _deltanet_kernel(qkv_ref, gate_ref, ba_ref, convw_ref, arow_ref, dtrow_ref, gn_ref,
                     o_ref, s_ref, xe_ref):
    C = A_CHUNK
    R = A_ROWS
    HS = range(HEADS)
    IT = range((R // C) * HEADS)

    @pl.when(pl.program_id(1) == 0)
    def _():
        s_ref[...] = jnp.zeros_like(s_ref)
        xe_ref[0:SUBLANES, :] = jnp.zeros((SUBLANES, 3 * HW), F32)

    x = qkv_ref[0].astype(F32)
    xe_ref[SUBLANES:SUBLANES + R, :] = x
    w = convw_ref[...]
    y = w[A_CONV - 1:A_CONV] * x
    for d in range(1, A_CONV):
        y = y + w[A_CONV - 1 - d:A_CONV - d] * xe_ref[SUBLANES - d:SUBLANES - d + R, :]
    xe_ref[0:SUBLANES, :] = x[R - SUBLANES:R]
    y = _silu(y)

    ba = ba_ref[0]
    beta_all = _sigmoid(ba)
    g_all = arow_ref[...] * _softplus(ba + dtrow_ref[...])
    gate = _silu(gate_ref[0].astype(F32))
    s_old = [s_ref[h] for h in HS]

    row = lax.broadcasted_iota(jnp.int32, (C, C), 0)
    col = lax.broadcasted_iota(jnp.int32, (C, C), 1)
    lmat = (col <= row).astype(BF16)
    rowx = lax.broadcasted_iota(jnp.int32, (C, DH + C), 0)
    colx = lax.broadcasted_iota(jnp.int32, (C, DH + C), 1)
    umask = (colx < DH) | (rowx > colx - DH)
    eye = (row == col).astype(F32)
    bd8 = (row >> 3) == (col >> 3)

    def merge_mask(sh):
        return (((row >> (sh + 1)) == (col >> (sh + 1)))
                & (((row >> sh) & 1) == 1) & (((col >> sh) & 1) == 0))

    rows = [slice((i // HEADS) * C, (i // HEADS + 1) * C) for i in IT]
    hd = [i % HEADS for i in IT]
    q = [y[rows[i], hd[i] * DH:(hd[i] + 1) * DH] for i in IT]
    k = [y[rows[i], HW + hd[i] * DH:HW + (hd[i] + 1) * DH] for i in IT]
    v = [y[rows[i], 2 * HW + hd[i] * DH:2 * HW + (hd[i] + 1) * DH] for i in IT]
    q = [a * (lax.rsqrt(jnp.sum(a * a, axis=-1, keepdims=True) + EPS) * (DH ** -0.5)) for a in q]
    k = [a * lax.rsqrt(jnp.sum(a * a, axis=-1, keepdims=True) + EPS) for a in k]
    beta = [beta_all[rows[i], hd[i]:hd[i] + 1] for i in IT]
    gb = [jnp.broadcast_to(g_all[rows[i], HEADS + hd[i]:HEADS + hd[i] + 1], (C, DH + C)) for i in IT]
    dext = [_dot_exact_lhs(lmat, jnp.where(umask, gb[i], 0.0)) for i in IT]
    gc = [d[:, :DH] for d in dext]
    edm = [jnp.exp(d[:, DH:]) for d in dext]
    egc = [jnp.exp(g) for g in gc]
    gl = [g[C - 1:C, :] for g in gc]
    kb = [k[i] * beta[i] for i in IT]
    m = [_dot_nt(kb[i].astype(BF16), k[i].astype(BF16)) * jnp.where(row > col, edm[i], 0.0)
         for i in IT]

    nd = [jnp.where(bd8, -a, 0.0) for a in m]
    p2 = [_dotb(a, a) for a in nd]
    p4 = [_dotb(a, a) for a in p2]
    x1 = [eye + nd[i] for i in IT]
    x1 = [x1[i] + _dotb(x1[i], p2[i]) for i in IT]
    xi = [x1[i] + _dotb(x1[i], p4[i]) for i in IT]
    for sh in (3, 4, 5):
        mm = merge_mask(sh)
        t = [_dotb(xi[i], jnp.where(mm, m[i], 0.0)) for i in IT]
        xi = [xi[i] - _dotb(t[i], xi[i]) for i in IT]

    rhs = [jnp.concatenate([v[i] * beta[i], kb[i] * egc[i]], axis=1) for i in IT]
    sol = [_dotb(xi[i], rhs[i]) for i in IT]
    attn = [(_dot_nt(q[i].astype(BF16), k[i].astype(BF16))
             * jnp.where(row >= col, edm[i], 0.0)).astype(BF16) for i in IT]
    qg = [(q[i] * egc[i]).astype(BF16) for i in IT]
    kg = [(k[i] * jnp.exp(gl[i] - gc[i])).astype(BF16) for i in IT]
    egl = [jnp.exp(a) for a in gl]

    s_cur = s_old
    for c in range(R // C):
        it = [c * HEADS + h for h in HS]
        sb = [a.astype(BF16) for a in s_cur]
        v_new = [(sol[i][:, :DH] - _dot(sol[i][:, DH:].astype(BF16), sb[h])).astype(BF16)
                 for h, i in enumerate(it)]
        o = [_dot(qg[i], sb[h]) + _dot(attn[i], v_new[h]) for h, i in enumerate(it)]
        s_cur = [s_cur[h] * egl[i] + _dot_tn(kg[i], v_new[h]) for h, i in enumerate(it)]
        o = [a * lax.rsqrt(jnp.mean(a * a, axis=-1, keepdims=True) + EPS) * gn_ref[...] for a in o]
        o_ref[0, c * C:(c + 1) * C, :] = (jnp.concatenate(o, axis=1)
                                          * gate[c * C:(c + 1) * C]).astype(o_ref.dtype)
    for h in HS:
        s_ref[h] = s_cur[h]


def _deltanet(proj3, projc3, conv_w, a_log, dt_bias, gnorm):
    b, t, _ = proj3.shape
    C = A_ROWS
    lane_row = lambda v: jnp.pad(v.astype(F32), (HEADS, LANES - 2 * HEADS)).reshape(1, LANES)
    arow = lane_row(-jnp.exp(a_log.astype(F32)))
    dtrow = lane_row(dt_bias)
    const = lambda shape: pl.BlockSpec(shape, lambda i, c: (0,) * len(shape))
    return pl.pallas_call(
        _deltanet_kernel,
        grid=(b, t // C),
        in_specs=[pl.BlockSpec((1, C, 3 * HW), lambda i, c: (i, c, OFF_AQKV // (3 * HW))),
                  pl.BlockSpec((1, C, HW), lambda i, c: (i, c, OFF_AGATE // HW)),
                  pl.BlockSpec((1, C, LANES), lambda i, c: (i, c, OFF_BA // LANES)),
                  const((A_CONV, 3 * HW)), const((1, LANES)), const((1, LANES)), const((1, DH))],
        out_specs=pl.BlockSpec((1, C, HW), lambda i, c: (i, c, 0)),
        out_shape=jax.ShapeDtypeStruct((b, t, HW), BF16),
        scratch_shapes=[pltpu.VMEM((HEADS, DH, DH), F32), pltpu.VMEM((C + SUBLANES, 3 * HW), F32)],
        compiler_params=_cparams(("parallel", "arbitrary")),
        name="deltanet",
    )(proj3, proj3, projc3, conv_w.astype(F32), arow, dtrow, gnorm.reshape(1, DH).astype(F32))


def _hgrn_kernel(q_ref, f_ref, i_ref, g_ref, lb_ref, gn_ref, o_ref, st_ref, lg_ref, k_ref):
    R, K = B_ROWS, B_BLK
    HS = range(HEADS)

    @pl.when(pl.program_id(1) == 0)
    def _():
        st_ref[...] = jnp.zeros_like(st_ref)

    lb_all = lb_ref[...]
    fl = f_ref[0].astype(F32)
    lg = jnp.log(lb_all + (1.0 - lb_all) * _sigmoid(fl))
    k_ref[...] = (1.0 - lb_all) * _sigmoid(-fl)
    lg_ref[...] = lg
    brow = lax.broadcasted_iota(jnp.int32, (R // B_FAST, R), 0)
    bcol = lax.broadcasted_iota(jnp.int32, (R // B_FAST, R), 1)
    chunksum = _dot((bcol // B_FAST == brow).astype(BF16), lg.astype(BF16))
    max_decay = jnp.max(-chunksum)

    H8 = SUBLANES
    row8 = lax.broadcasted_iota(jnp.int32, (H8, DH), 0)
    sls = [slice(h * DH, (h + 1) * DH) for h in HS]

    def pairs_exact(q, k, v, gc):
        n_it = range(len(q))
        otile = [[jnp.zeros((H8, DH), F32) for _ in range(K // H8)] for _ in n_it]
        for j in range(K):
            for t in range(j // H8, K // H8):
                ts = slice(t * H8, (t + 1) * H8)
                for i in n_it:
                    d = gc[i][ts] - gc[i][j:j + 1]
                    rel = jnp.exp(jnp.where(row8 >= j - t * H8, d, -1e30) if t == j // H8 else d)
                    sj = jnp.sum(q[i][ts] * k[i][j:j + 1] * rel, axis=-1, keepdims=True)
                    otile[i][t] = otile[i][t] + sj * v[i][j:j + 1]
        return [jnp.concatenate(otile[i], axis=0) for i in n_it]

    def run(exact):
        KB = K if exact else B_FAST
        GR = B_GROUP * KB
        grow = lax.broadcasted_iota(jnp.int32, (GR, GR), 0)
        gcol = lax.broadcasted_iota(jnp.int32, (GR, GR), 1)
        same_block_lower = (gcol <= grow) & (grow // KB == gcol // KB)
        lmat_g = same_block_lower.astype(BF16)
        st = [st_ref[h] for h in HS]
        for r0 in range(0, R, GR):
            gs = slice(r0, r0 + GR)
            blks = [slice(bi * KB, (bi + 1) * KB) for bi in range(B_GROUP)]
            gc_all = _dot_exact_lhs(lmat_g, lg_ref[gs, :])
            gl_all = jnp.concatenate([jnp.broadcast_to(gc_all[bs.stop - 1:bs.stop], (KB, HW)) for bs in blks],
                                     axis=0)
            q = [_silu(q_ref[0, gs, sl].astype(F32)) for sl in sls]
            k = [k_ref[gs, sl] for sl in sls]
            v = [i_ref[0, gs, sl].astype(F32) for sl in sls]
            gc = [gc_all[:, sl] for sl in sls]
            gl = [gl_all[:, sl] for sl in sls]
            kg = [(k[h] * jnp.exp(gl[h] - gc[h])).astype(BF16) for h in HS]
            qs = [(q[h] * jnp.exp(gc[h])).astype(BF16) for h in HS]
            vb = [a.astype(BF16) for a in v]
            upd = [[_dot_tn(vb[h][bs], kg[h][bs]) for h in HS] for bs in blks]
            egl = [[jnp.exp(gl[h][bs.stop - 1:bs.stop]) for h in HS] for bs in blks]
            if exact:
                items = [(h, bs) for bs in blks for h in HS]
                pe = pairs_exact([q[h][bs] for h, bs in items], [k[h][bs] for h, bs in items],
                                 [v[h][bs] for h, bs in items], [gc[h][bs] for h, bs in items])
                intra = [jnp.concatenate([pe[bi * HEADS + h] for bi in range(B_GROUP)], axis=0) for h in HS]
            else:
                qf = [(q[h] * jnp.exp(gc[h] - gl[h])).astype(BF16) for h in HS]
                s = [jnp.where(same_block_lower, _dot_nt(qf[h], kg[h]), 0.0).astype(BF16) for h in HS]
                intra = [_dot(s[h], vb[h]) for h in HS]
            for bi, bs in enumerate(blks):
                o = [_dot_nt(qs[h][bs], st[h].astype(BF16)) + intra[h][bs] for h in HS]
                st = [st[h] * egl[bi][h] + upd[bi][h] for h in HS]
                o = [a * lax.rsqrt(jnp.mean(a * a, axis=-1, keepdims=True) + EPS) * gn_ref[...] for a in o]
                rs = slice(r0 + bi * KB, r0 + (bi + 1) * KB)
                o_ref[0, rs, :] = (jnp.concatenate(o, axis=1)
                                   * _sigmoid(g_ref[0, rs, :].astype(F32))).astype(o_ref.dtype)
        for h in HS:
            st_ref[h] = st[h]

    lax.cond(max_decay > HGRN_SAFE_DECAY, lambda: run(True), lambda: run(False))


def _hgrn(proj3, lb, gnorm):
    b, t, _ = proj3.shape
    R = B_ROWS
    seg = lambda s: pl.BlockSpec((1, R, HW), lambda i, c: (i, c, OFF_B // HW + s))
    const = lambda shape: pl.BlockSpec(shape, lambda i, c: (0,) * len(shape))
    return pl.pallas_call(
        _hgrn_kernel,
        grid=(b, t // R),
        in_specs=[seg(0), seg(1), seg(2), seg(3), const((1, HW)), const((1, DH))],
        out_specs=pl.BlockSpec((1, R, HW), lambda i, c: (i, c, 0)),
        out_shape=jax.ShapeDtypeStruct((b, t, HW), BF16),
        scratch_shapes=[pltpu.VMEM((HEADS, DH, DH), F32), pltpu.VMEM((R, HW), F32),
                        pltpu.VMEM((R, HW), F32)],
        compiler_params=_cparams(("parallel", "arbitrary")),
        name="hgrn2",
    )(proj3, proj3, proj3, proj3, lb.reshape(1, HW).astype(F32), gnorm.reshape(1, DH).astype(F32))


def _t5_bucket_np(n):
    max_exact = REL_BUCKETS // 2
    nf = np.maximum(n, 1).astype(np.float32)
    large = max_exact + (np.log(nf / max_exact) / math.log(REL_MAX_DIST / max_exact)
                         * (REL_BUCKETS - max_exact)).astype(np.int32)
    large = np.minimum(large, REL_BUCKETS - 1)
    return np.where(n < max_exact, n, large)


def _dilated_kernel(span, dil, hpb, nlb, q_ref, kp_ref, kc_ref, vp_ref, vc_ref, bias_ref,
                    o_ref, lse_ref):
    CB = C_BLOCK
    n = pl.program_id(1)
    qi = lax.broadcasted_iota(jnp.int32, (CB, 2 * CB), 0)
    kj = lax.broadcasted_iota(jnp.int32, (CB, 2 * CB), 1)
    dist = qi + CB - kj
    band = (dist >= 0) & (dist <= span)
    band_first = band & ((kj >= CB) | (n > 0))

    def rows(r, jb):
        return pl.ds(jb * CB * dil + r, CB, stride=dil) if dil > 1 else pl.ds(jb * CB, CB)

    def with_prev(p_ref, c_ref, r, jb, sl):
        prev = p_ref[0, rows(r, 0), sl] if jb == 0 else c_ref[0, rows(r, jb - 1), sl]
        return jnp.concatenate([prev, c_ref[0, rows(r, jb), sl]], axis=0)

    def tiles(items):
        n_it = range(len(items))
        q = [q_ref[0, rows(r, jb), sl].astype(BF16) for r, jb, sl, _ in items]
        k = [with_prev(kp_ref, kc_ref, r, jb, sl).astype(BF16) for r, jb, sl, _ in items]
        v = [with_prev(vp_ref, vc_ref, r, jb, sl).astype(BF16) for r, jb, sl, _ in items]
        s = [_dot_nt(q[i], k[i]) * (DH ** -0.5) + bias_ref[items[i][3]] for i in n_it]
        s = [jnp.where(band_first if items[i][1] == 0 else band, s[i], -1e30) for i in n_it]
        mx = [jnp.max(a, axis=-1, keepdims=True) for a in s]
        p = [jnp.exp(s[i] - mx[i]) for i in n_it]
        den = [jnp.sum(a, axis=-1, keepdims=True) for a in p]
        o = [_dot(p[i].astype(BF16), v[i]) / den[i] for i in n_it]
        for i, (r, jb, sl, _) in enumerate(items):
            o_ref[0, rows(r, jb), sl] = o[i]
            lse_ref[0, rows(r, jb), sl] = jnp.broadcast_to(mx[i] + jnp.log(den[i]), (CB, DH))

    for jb in range(nlb):
        if dil == 1:
            for h0 in range(0, hpb, C_TILES):
                tiles([(0, jb, slice(h * DH, (h + 1) * DH), h) for h in range(h0, h0 + C_TILES)])
        elif dil == C_TILES:
            tiles([(r, jb, slice(0, DH), 0) for r in range(dil)])
        else:
            def body(g, carry, jb=jb):
                tiles([(g * C_TILES + i, jb, slice(0, DH), 0) for i in range(C_TILES)])
                return carry
            lax.fori_loop(0, dil // C_TILES, body, 0)


def _dilated_group(proj3, gi, rel_bias):
    b, t, _ = proj3.shape
    window, dil = C_PAIRS[gi]
    span = window // dil
    CB = C_BLOCK
    pb = CB * dil
    nlb = min(C_NLB, t // pb)
    rb = nlb * pb
    hpb = HEADS if dil == 1 else 1
    bw = hpb * DH
    qi = np.arange(CB)[:, None]
    kj = np.arange(2 * CB)[None, :]
    bucket = _t5_bucket_np(np.maximum(qi + CB - kj, 0) * dil)
    onehot = jnp.asarray(np.eye(REL_BUCKETS, dtype=np.float32)[bucket])
    bias = jnp.einsum("qkb,bh->hqk", onehot, rel_bias[:, gi * HEADS:(gi + 1) * HEADS].astype(F32),
                      precision=HIGHEST)

    def seg(which, prev):
        base = (OFF_C + which * C_GROUPS * HW + gi * HW) // bw
        if prev:
            return pl.BlockSpec((1, pb, bw), lambda i, n, h: (i, jnp.maximum(n * nlb - 1, 0), base + h))
        return pl.BlockSpec((1, rb, bw), lambda i, n, h: (i, n, base + h))

    o, lse = pl.pallas_call(
        functools.partial(_dilated_kernel, span, dil, hpb, nlb),
        grid=(b, t // rb, HEADS // hpb),
        in_specs=[seg(0, False), seg(1, True), seg(1, False), seg(2, True), seg(2, False),
                  pl.BlockSpec((hpb, CB, 2 * CB), lambda i, n, h: (h, 0, 0))],
        out_specs=[pl.BlockSpec((1, rb, bw), lambda i, n, h: (i, n, h)),
                   pl.BlockSpec((1, rb, bw), lambda i, n, h: (i, n, h))],
        out_shape=[jax.ShapeDtypeStruct((b, t, HW), F32),
                   jax.ShapeDtypeStruct((b, t, HW), F32)],
        compiler_params=_cparams(("parallel", "arbitrary", "arbitrary")),
        name=f"dilated_g{gi}",
    )(proj3, proj3, proj3, proj3, proj3, bias)
    return o.reshape(b * t, HW), lse.reshape(b * t, HW)


def _mix_kernel(x_ref, gate_ref, oa_ref, ob_ref, oc0_ref, oc1_ref, oc2_ref,
                l0_ref, l1_ref, l2_ref, wa_ref, wb_ref, wc_ref, wo_ref, out_ref):
    l0, l1, l2 = l0_ref[...], l1_ref[...], l2_ref[...]
    mx = jnp.maximum(jnp.maximum(l0, l1), l2)
    e0, e1, e2 = jnp.exp(l0 - mx), jnp.exp(l1 - mx), jnp.exp(l2 - mx)
    oc = (e0 * oc0_ref[...] + e1 * oc1_ref[...] + e2 * oc2_ref[...]) / (e0 + e1 + e2)
    mix = (_sigmoid(gate_ref[:, 0:D_MODEL].astype(F32)) * _dot(oa_ref[...], wa_ref[...])
           + _sigmoid(gate_ref[:, D_MODEL:2 * D_MODEL].astype(F32)) * _dot(ob_ref[...], wb_ref[...])
           + _sigmoid(gate_ref[:, 2 * D_MODEL:3 * D_MODEL].astype(F32)) * _dot(oc.astype(BF16), wc_ref[...]))
    out_ref[...] = x_ref[...] + _dot(mix.astype(BF16), wo_ref[...])


def _mix(x2, proj2, oa, ob, ocs, lses, wa, wb, wc, wo, tm=512):
    m = x2.shape[0]
    tm = min(tm, m)
    rowblk = lambda w: pl.BlockSpec((tm, w), lambda i: (i, 0))
    const = lambda shape: pl.BlockSpec(shape, lambda i: (0,) * len(shape))
    return pl.pallas_call(
        _mix_kernel,
        grid=(m // tm,),
        in_specs=[rowblk(D_MODEL),
                  pl.BlockSpec((tm, 3 * D_MODEL), lambda i: (i, OFF_BRG // (3 * D_MODEL))),
                  rowblk(HW), rowblk(HW),
                  rowblk(HW), rowblk(HW), rowblk(HW), rowblk(HW), rowblk(HW), rowblk(HW),
                  const((HW, D_MODEL)), const((HW, D_MODEL)), const((HW, D_MODEL)),
                  const((D_MODEL, D_MODEL))],
        out_specs=rowblk(D_MODEL),
        out_shape=jax.ShapeDtypeStruct((m, D_MODEL), F32),
        compiler_params=_cparams(("parallel",)),
        name="branch_mix",
    )(x2, proj2, oa, ob, ocs[0], ocs[1], ocs[2], lses[0], lses[1], lses[2],
      wa.astype(BF16), wb.astype(BF16), wc.astype(BF16), wo.astype(BF16))


def _ffn_kernel(x_ref, g_ref, w1_ref, w3_ref, w2_ref, o_ref):
    x = x_ref[...]
    ms = jnp.mean(x * x, axis=-1, keepdims=True)
    h = (x * lax.rsqrt(ms + EPS) * g_ref[...]).astype(BF16)
    a = _dot(h, w1_ref[...])
    b = _dot(h, w3_ref[...])
    o_ref[...] = x + _dot((_silu(a) * b).astype(BF16), w2_ref[...])


def _ffn(x2, gain, w1, w3, w2, tm=512):
    m = x2.shape[0]
    ff = w1.shape[1]
    tm = min(tm, m)
    resident = lambda shape: pl.BlockSpec(shape, lambda i: (0, 0), pipeline_mode=pl.Buffered(1))
    return pl.pallas_call(
        _ffn_kernel,
        grid=(m // tm,),
        in_specs=[pl.BlockSpec((tm, D_MODEL), lambda i: (i, 0)),
                  pl.BlockSpec((1, D_MODEL), lambda i: (0, 0)),
                  resident((D_MODEL, ff)), resident((D_MODEL, ff)), resident((ff, D_MODEL))],
        out_specs=pl.BlockSpec((tm, D_MODEL), lambda i: (i, 0)),
        out_shape=jax.ShapeDtypeStruct((m, D_MODEL), F32),
        compiler_params=_cparams(("parallel",)),
        name="ffn",
    )(x2, gain.reshape(1, D_MODEL).astype(F32), w1.astype(BF16), w3.astype(BF16), w2.astype(BF16))


HALF = D_MODEL // 2
U32 = jnp.uint32


def _pack_bf16_pairs(x):
    r = x.astype(BF16).astype(F32)
    lo = lax.bitcast_convert_type(r[:, :HALF], U32) >> 16
    hi = lax.bitcast_convert_type(r[:, HALF:], U32)
    return hi | lo


def _unpack_bf16_pairs(p):
    lo = lax.bitcast_convert_type(p << 16, F32)
    hi = lax.bitcast_convert_type(p & U32(0xFFFF0000), F32)
    return lo, hi


def _router_kernel(x_ref, g_ref, wr_ref, h_ref, ids_ref, ps_ref):
    x = x_ref[...]
    ms = jnp.mean(x * x, axis=-1, keepdims=True)
    h = x * lax.rsqrt(ms + EPS) * g_ref[...]
    h_ref[...] = _pack_bf16_pairs(h)
    hh = h.astype(BF16)
    hl = (h - hh.astype(F32)).astype(BF16)
    w = wr_ref[...]
    wh = w.astype(BF16)
    wl = (w - wh.astype(F32)).astype(BF16)
    logits = _dot(hh, wh) + (_dot(hh, wl) + _dot(hl, wh))
    lane = lax.broadcasted_iota(jnp.int32, logits.shape, 1)
    neg = jnp.float32(-jnp.inf)
    l1 = jnp.where(lane < N_EXPERTS, logits, neg)
    m1 = jnp.max(l1, axis=-1, keepdims=True)
    i1 = jnp.min(jnp.where(l1 == m1, lane, LANES), axis=-1, keepdims=True)
    l2 = jnp.where(lane == i1, neg, l1)
    m2 = jnp.max(l2, axis=-1, keepdims=True)
    i2 = jnp.min(jnp.where(l2 == m2, lane, LANES), axis=-1, keepdims=True)
    e = jnp.exp(m2 - m1)
    p1 = 1.0 / (1.0 + e)
    p2 = e / (1.0 + e)
    ids_ref[...] = jnp.where(lane == 0, i1, jnp.where(lane == 1, i2, 0))
    ps_ref[...] = jnp.where(lane == 0, p1, jnp.where(lane == 1, p2, 0.0))


def _router(x2, gain, w_router, tm=1024):
    m = x2.shape[0]
    tm = min(tm, m)
    wr = jnp.pad(w_router.astype(F32), ((0, 0), (0, LANES - N_EXPERTS)))
    return pl.pallas_call(
        _router_kernel,
        grid=(m // tm,),
        in_specs=[pl.BlockSpec((tm, D_MODEL), lambda i: (i, 0)),
                  pl.BlockSpec((1, D_MODEL), lambda i: (0, 0)),
                  pl.BlockSpec((D_MODEL, LANES), lambda i: (0, 0))],
        out_specs=[pl.BlockSpec((tm, HALF), lambda i: (i, 0)),
                   pl.BlockSpec((tm, LANES), lambda i: (i, 0)),
                   pl.BlockSpec((tm, LANES), lambda i: (i, 0))],
        out_shape=[jax.ShapeDtypeStruct((m, HALF), U32),
                   jax.ShapeDtypeStruct((m, LANES), jnp.int32),
                   jax.ShapeDtypeStruct((m, LANES), F32)],
        compiler_params=_cparams(("parallel",)),
        name="router",
    )(x2, gain.reshape(1, D_MODEL).astype(F32), wr)


def _route_positions(ids, tm):
    m = ids.shape[0]
    e_flat = ids[:, :TOP_K].reshape(-1)
    onehot = (e_flat[:, None] == jnp.arange(N_EXPERTS)[None, :]).astype(jnp.int32)
    csum = jnp.cumsum(onehot, axis=0)
    counts = csum[-1]
    gsz = ((counts + tm - 1) // tm) * tm
    gend = jnp.cumsum(gsz)
    pos = jnp.sum(onehot * (gend - gsz + csum - 1), axis=1)
    nt = (TOP_K * m + N_EXPERTS * tm) // tm
    n_used = gend[-1] // tm
    tile_e = jnp.sum((jnp.arange(nt)[:, None] * tm >= gend[None, :]).astype(jnp.int32), axis=1)
    last_e = jnp.sum(((n_used - 1) * tm >= gend).astype(jnp.int32))
    tile_e = jnp.minimum(tile_e, last_e)
    return pos.astype(jnp.int32), tile_e.astype(jnp.int32), n_used.reshape(1).astype(jnp.int32), nt


def _dispatch_kernel(pos_ref, h_ref, init_ref, xs_ref, sem):
    del init_ref
    td = h_ref.shape[0]

    def row_copy(t, s):
        return pltpu.make_async_copy(h_ref.at[pl.ds(t, 1)],
                                     xs_ref.at[pl.ds(pos_ref[0, 0, TOP_K * t + s], 1)], sem)

    def start(t, c):
        for s in range(TOP_K):
            row_copy(t, s).start()
        return c

    lax.fori_loop(0, td, start, 0, unroll=8)
    for s in range(TOP_K):
        pltpu.make_async_copy(h_ref, xs_ref.at[pl.ds(0, td)], sem).wait()


def _dispatch(h, pos, npad, td=1024):
    m = h.shape[0]
    td = min(td, m)
    return pl.pallas_call(
        _dispatch_kernel,
        grid=(m // td,),
        in_specs=[pl.BlockSpec((1, 1, TOP_K * td), lambda i: (i, 0, 0), memory_space=pltpu.SMEM),
                  pl.BlockSpec((td, HALF), lambda i: (i, 0)),
                  pl.BlockSpec(memory_space=pl.ANY)],
        out_specs=pl.BlockSpec(memory_space=pl.ANY),
        out_shape=jax.ShapeDtypeStruct((npad, HALF), U32),
        scratch_shapes=[pltpu.SemaphoreType.DMA(())],
        input_output_aliases={2: 0},
        compiler_params=_cparams(("arbitrary",)),
        name="moe_dispatch",
    )(pos.reshape(m // td, 1, TOP_K * td), h, jnp.zeros((npad, HALF), U32))


def _experts_kernel(te_ref, nu_ref, xs_ref, w1_ref, w3_ref, w2_ref, y_ref, xb_ref, acc_ref):
    del te_ref
    i = pl.program_id(0)
    j = pl.program_id(1)

    @pl.when(i < nu_ref[0])
    def _():
        @pl.when(j == 0)
        def _():
            lo, hi = _unpack_bf16_pairs(xs_ref[...])
            xb_ref[...] = jnp.concatenate([lo.astype(BF16), hi.astype(BF16)], axis=1)
            acc_ref[...] = jnp.zeros_like(acc_ref)

        xb = xb_ref[...]
        a = _dot(xb, w1_ref[0])
        b = _dot(xb, w3_ref[0])
        acc_ref[...] += _dot((_silu(a) * b).astype(BF16), w2_ref[0])

        @pl.when(j == pl.num_programs(1) - 1)
        def _():
            y_ref[...] = _pack_bf16_pairs(acc_ref[...])

    @pl.when((i >= nu_ref[0]) & (j == 0))
    def _():
        y_ref[...] = jnp.zeros_like(y_ref)


def _experts(xs, tile_e, n_used, w1, w3, w2, tm, tf=1792):
    npad = xs.shape[0]
    ff = w1.shape[2]
    nj = ff // tf
    row = lambda i, j, te, nu: (jnp.minimum(i, nu[0] - 1), 0)
    jj = lambda i, j, nu: jnp.where(i < nu[0], j, nj - 1)
    return pl.pallas_call(
        _experts_kernel,
        grid_spec=pltpu.PrefetchScalarGridSpec(
            num_scalar_prefetch=2,
            grid=(npad // tm, nj),
            in_specs=[pl.BlockSpec((tm, HALF), row),
                      pl.BlockSpec((1, D_MODEL, tf), lambda i, j, te, nu: (te[i], 0, jj(i, j, nu))),
                      pl.BlockSpec((1, D_MODEL, tf), lambda i, j, te, nu: (te[i], 0, jj(i, j, nu))),
                      pl.BlockSpec((1, tf, D_MODEL), lambda i, j, te, nu: (te[i], jj(i, j, nu), 0))],
            out_specs=pl.BlockSpec((tm, HALF), lambda i, j, te, nu: (i, 0)),
            scratch_shapes=[pltpu.VMEM((tm, D_MODEL), BF16), pltpu.VMEM((tm, D_MODEL), F32)]),
        out_shape=jax.ShapeDtypeStruct((npad, HALF), U32),
        compiler_params=_cparams(("arbitrary", "arbitrary")),
        name="moe_experts",
    )(tile_e, n_used, xs, w1.astype(BF16), w3.astype(BF16), w2.astype(BF16))


def _combine_kernel(pos_ref, x_ref, ps_ref, y_ref, o_ref, buf_ref, sem):
    td = x_ref.shape[0]

    def row_copy(t, s):
        return pltpu.make_async_copy(y_ref.at[pl.ds(pos_ref[0, 0, TOP_K * t + s], 1)],
                                     buf_ref.at[s, pl.ds(t, 1)], sem.at[s])

    def start(t, c):
        for s in range(TOP_K):
            row_copy(t, s).start()
        return c

    lax.fori_loop(0, td, start, 0, unroll=8)
    for s in range(TOP_K):
        pltpu.make_async_copy(y_ref.at[pl.ds(0, td)], buf_ref.at[s], sem.at[s]).wait()
    ps = ps_ref[...]
    lo0, hi0 = _unpack_bf16_pairs(buf_ref[0])
    lo1, hi1 = _unpack_bf16_pairs(buf_ref[1])
    o_ref[:, :HALF] = x_ref[:, :HALF] + ps[:, 0:1] * lo0 + ps[:, 1:2] * lo1
    o_ref[:, HALF:] = x_ref[:, HALF:] + ps[:, 0:1] * hi0 + ps[:, 1:2] * hi1


def _combine(x2, ps, pos, y, td=1024):
    m = x2.shape[0]
    td = min(td, m)
    return pl.pallas_call(
        _combine_kernel,
        grid=(m // td,),
        in_specs=[pl.BlockSpec((1, 1, TOP_K * td), lambda i: (i, 0, 0), memory_space=pltpu.SMEM),
                  pl.BlockSpec((td, D_MODEL), lambda i: (i, 0)),
                  pl.BlockSpec((td, LANES), lambda i: (i, 0)),
                  pl.BlockSpec(memory_space=pl.ANY)],
        out_specs=pl.BlockSpec((td, D_MODEL), lambda i: (i, 0)),
        out_shape=jax.ShapeDtypeStruct((m, D_MODEL), F32),
        scratch_shapes=[pltpu.VMEM((TOP_K, td, HALF), U32), pltpu.SemaphoreType.DMA((TOP_K,))],
        compiler_params=_cparams(("arbitrary",)),
        name="moe_combine",
    )(pos.reshape(m // td, 1, TOP_K * td), x2, ps, y)


def _moe(x2, gain, w_router, w1, w3, w2, tm=1024):
    m = x2.shape[0]
    tm = min(tm, m)
    h, ids, ps = _router(x2, gain, w_router)
    pos, tile_e, n_used, nt = _route_positions(ids, tm)
    xs = _dispatch(h, pos, nt * tm)
    y = _experts(xs, tile_e, n_used, w1, w3, w2, tm)
    return _combine(x2, ps, pos, y)


def _relayout_w_in(w):
    k = w.shape[0]
    w = w.astype(BF16)
    main = [w[:, _R_BRG:_R_END], w[:, _R_AQKV:_R_BETA], w[:, _R_AGATE:_R_B], w[:, _R_B:_R_C]]
    second = [w[:, _R_C:_R_BRG], w[:, _R_BETA:_R_AGATE],
              jnp.zeros((k, NP_C - OFF_BA - 2 * HEADS), BF16)]
    return jnp.concatenate(main, axis=1), jnp.concatenate(second, axis=1)


def _layer_mixers(x2, b, t, layer, lower_bounds, w_in, norm_mix, conv_a, a_log, dt_bias,
                  gnorm_a, gnorm_b, qnorm_c, knorm_c, rel_bias, w_br_a, w_br_b, w_br_c, w_out):
    w_main, w_second = _relayout_w_in(w_in[layer])
    gain = norm_mix[layer].astype(F32)
    proj2 = _norm_proj(x2, gain, w_main, BF16)
    proj3 = proj2.reshape(b, t, NP_MAIN)
    qk_gain = jnp.concatenate([jnp.tile(qnorm_c[layer].astype(F32), (1, HEADS)).reshape(-1),
                               jnp.tile(knorm_c[layer].astype(F32), (1, HEADS)).reshape(-1)])
    projc3 = _norm_proj(x2, gain, w_second, F32, head_gain=qk_gain).reshape(b, t, NP_C)
    oa = _deltanet(proj3, projc3, conv_a[layer], a_log[layer], dt_bias[layer], gnorm_a[layer])
    ob = _hgrn(proj3, lower_bounds[layer], gnorm_b[layer])
    ocs, lses = [], []
    for gi in range(C_GROUPS):
        o, lse = _dilated_group(projc3, gi, rel_bias)
        ocs.append(o)
        lses.append(lse)
    return _mix(x2, proj2, oa.reshape(b * t, HW), ob.reshape(b * t, HW), ocs, lses,
                w_br_a[layer], w_br_b[layer], w_br_c[layer], w_out[layer])


def kernel(x, w_in, norm_mix, conv_a, a_log, dt_bias, gnorm_a, lb_logits, gnorm_b, qnorm_c, knorm_c, rel_bias, w_br_a, w_br_b, w_br_c, w_out, norm_ffn, ffn_w1, ffn_w3, ffn_w2, router, moe_w1, moe_w3, moe_w2):
    b, t, _ = x.shape
    depth = w_in.shape[0]
    p_lb = jax.nn.softmax(lb_logits.astype(F32), axis=0)
    lower_bounds = jnp.cumsum(p_lb, axis=0) - p_lb[0:1]
    x2 = x.reshape(b * t, D_MODEL).astype(F32)
    for layer in range(depth):
        x2 = _layer_mixers(x2, b, t, layer, lower_bounds, w_in, norm_mix, conv_a, a_log, dt_bias,
                           gnorm_a, gnorm_b, qnorm_c, knorm_c, rel_bias,
                           w_br_a, w_br_b, w_br_c, w_out)
        li = layer // 2
        if layer % 2 == 0:
            x2 = _ffn(x2, norm_ffn[layer], ffn_w1[li], ffn_w3[li], ffn_w2[li])
        else:
            x2 = _moe(x2, norm_ffn[layer], router[li], moe_w1[li], moe_w3[li], moe_w2[li])
    return x2.reshape(b, t, D_MODEL).astype(x.dtype)
```

```python
import functools
import math

import numpy as np
import jax
import jax.numpy as jnp
from jax import lax
from jax.experimental import pallas as pl
from jax.experimental.pallas import tpu as pltpu

F32 = jnp.float32
BF16 = jnp.bfloat16
HIGHEST = lax.Precision.HIGHEST

LANES = 128
SUBLANES = 8

D_MODEL = 1024
EPS = 1e-6
HEADS = 4
DH = 128
HW = HEADS * DH
A_CONV = 4
A_CHUNK = 64
A_ROWS = 512
B_ROWS = 512
B_BLK = 16
B_GROUP = 8
B_FAST = 32
HGRN_SAFE_DECAY = 60.0
C_PAIRS = ((128, 1), (512, 4), (2048, 16))
C_GROUPS = 3
C_BLOCK = 128
C_TILES = 4
C_LSEG = DH // HEADS
C_NLB = 8
REL_BUCKETS = 32
REL_MAX_DIST = 2048
N_EXPERTS = 8
TOP_K = 2

OFF_BRG = 0
OFF_AQKV = 3072
OFF_AGATE = 4608
OFF_B = 5120
NP_MAIN = 7168
OFF_C = 0
OFF_BA = 4608
NP_C = 5120

_R_AQKV, _R_BETA, _R_AGATE, _R_B, _R_C, _R_BRG, _R_END = 0, 1536, 1544, 2056, 4104, 8712, 11784

VMEM_LIMIT = 56 * 1024 * 1024


def _cparams(sem):
    return pltpu.CompilerParams(dimension_semantics=sem, vmem_limit_bytes=VMEM_LIMIT)


def _sigmoid(x):
    return 1.0 / (1.0 + jnp.exp(-x))


def _silu(x):
    return x * (0.5 * jnp.tanh(0.5 * x) + 0.5)


def _softplus(x):
    return jnp.maximum(x, 0.0) + jnp.log(1.0 + jnp.exp(-jnp.abs(x)))


def _dot(a, b):
    return jnp.dot(a, b, preferred_element_type=F32)


def _dot_nt(a, b, precision=None):
    return lax.dot_general(a, b, (((1,), (1,)), ((), ())), precision=precision,
                           preferred_element_type=F32)


def _dot_tn(a, b):
    return lax.dot_general(a, b, (((0,), (0,)), ((), ())), preferred_element_type=F32)


def _norm_proj_kernel(tn, head_norm_cols, x_ref, g_ref, w_ref, hg_ref, o_ref):
    x = x_ref[...]
    ms = jnp.mean(x * x, axis=-1, keepdims=True)
    h = (x * lax.rsqrt(ms + EPS) * g_ref[...]).astype(BF16)
    for c0 in range(0, w_ref.shape[1], tn):
        r = _dot(h, w_ref[:, c0:c0 + tn])
        if c0 < head_norm_cols:
            heads = [r[:, d0:d0 + DH] for d0 in range(0, tn, DH)]
            heads = [a * lax.rsqrt(jnp.mean(a * a, axis=-1, keepdims=True) + EPS) for a in heads]
            r = jnp.concatenate(heads, axis=1) * hg_ref[:, c0:c0 + tn]
        o_ref[:, c0:c0 + tn] = r.astype(o_ref.dtype)


def _norm_proj(x2, gain, w_bf16, out_dtype, head_gain=None, tm=512, tn=1024):
    m = x2.shape[0]
    n = w_bf16.shape[1]
    tm = min(tm, m)
    norm_cols = 0 if head_gain is None else head_gain.shape[0]
    assert norm_cols % tn == 0
    hg = jnp.zeros((1, n), F32)
    if head_gain is not None:
        hg = hg.at[0, :norm_cols].set(head_gain.astype(F32))
    return pl.pallas_call(
        functools.partial(_norm_proj_kernel, tn, norm_cols),
        grid=(m // tm,),
        in_specs=[pl.BlockSpec((tm, D_MODEL), lambda i: (i, 0)),
                  pl.BlockSpec((1, D_MODEL), lambda i: (0, 0)),
                  pl.BlockSpec((D_MODEL, n), lambda i: (0, 0), pipeline_mode=pl.Buffered(1)),
                  pl.BlockSpec((1, n), lambda i: (0, 0))],
        out_specs=pl.BlockSpec((tm, n), lambda i: (i, 0)),
        out_shape=jax.ShapeDtypeStruct((m, n), out_dtype),
        compiler_params=_cparams(("parallel",)),
        name="norm_proj",
    )(x2, gain.reshape(1, D_MODEL), w_bf16, hg)


def _dotb(a, b):
    return _dot(a.astype(BF16), b.astype(BF16))


def _dot_exact_lhs(a_bf16, b):
    b0 = b.astype(BF16)
    r1 = b - b0.astype(F32)
    b1 = r1.astype(BF16)
    b2 = (r1 - b1.astype(F32)).astype(BF16)
    return _dot(a_bf16, b0) + (_dot(a_bf16, b1) + _dot(a_bf16, b2))


def _deltanet_kernel(qkv_ref, gate_ref, ba_ref, convw_ref, arow_ref, dtrow_ref, gn_ref,
                     o_ref, s_ref, xe_ref):
    C = A_CHUNK
    R = A_ROWS
    HS = range(HEADS)
    IT = range((R // C) * HEADS)

    @pl.when(pl.program_id(1) == 0)
    def _():
        s_ref[...] = jnp.zeros_like(s_ref)
        xe_ref[0:SUBLANES, :] = jnp.zeros((SUBLANES, 3 * HW), F32)

    x = qkv_ref[0].astype(F32)
    xe_ref[SUBLANES:SUBLANES + R, :] = x
    w = convw_ref[...]
    y = w[A_CONV - 1:A_CONV] * x
    for d in range(1, A_CONV):
        y = y + w[A_CONV - 1 - d:A_CONV - d] * xe_ref[SUBLANES - d:SUBLANES - d + R, :]
    xe_ref[0:SUBLANES, :] = x[R - SUBLANES:R]
    y = _silu(y)

    ba = ba_ref[0]
    beta_all = _sigmoid(ba)
    g_all = arow_ref[...] * _softplus(ba + dtrow_ref[...])
    gate = _silu(gate_ref[0].astype(F32))
    s_old = [s_ref[h] for h in HS]

    row = lax.broadcasted_iota(jnp.int32, (C, C), 0)
    col = lax.broadcasted_iota(jnp.int32, (C, C), 1)
    lmat = (col <= row).astype(BF16)
    rowx = lax.broadcasted_iota(jnp.int32, (C, DH + C), 0)
    colx = lax.broadcasted_iota(jnp.int32, (C, DH + C), 1)
    umask = (colx < DH) | (rowx > colx - DH)
    eye = (row == col).astype(F32)
    bd8 = (row >> 3) == (col >> 3)

    def merge_mask(sh):
        return (((row >> (sh + 1)) == (col >> (sh + 1)))
                & (((row >> sh) & 1) == 1) & (((col >> sh) & 1) == 0))

    rows = [slice((i // HEADS) * C, (i // HEADS + 1) * C) for i in IT]
    hd = [i % HEADS for i in IT]
    q = [y[rows[i], hd[i] * DH:(hd[i] + 1) * DH] for i in IT]
    k = [y[rows[i], HW + hd[i] * DH:HW + (hd[i] + 1) * DH] for i in IT]
    v = [y[rows[i], 2 * HW + hd[i] * DH:2 * HW + (hd[i] + 1) * DH] for i in IT]
    q = [a * (lax.rsqrt(jnp.sum(a * a, axis=-1, keepdims=True) + EPS) * (DH ** -0.5)) for a in q]
    k = [a * lax.rsqrt(jnp.sum(a * a, axis=-1, keepdims=True) + EPS) for a in k]
    beta = [beta_all[rows[i], hd[i]:hd[i] + 1] for i in IT]
    gb = [jnp.broadcast_to(g_all[rows[i], HEADS + hd[i]:HEADS + hd[i] + 1], (C, DH + C)) for i in IT]
    dext = [_dot_exact_lhs(lmat, jnp.where(umask, gb[i], 0.0)) for i in IT]
    gc = [d[:, :DH] for d in dext]
    edm = [jnp.exp(d[:, DH:]) for d in dext]
    egc = [jnp.exp(g) for g in gc]
    gl = [g[C - 1:C, :] for g in gc]
    kb = [k[i] * beta[i] for i in IT]
    m = [_dot_nt(kb[i].astype(BF16), k[i].astype(BF16)) * jnp.where(row > col, edm[i], 0.0)
         for i in IT]

    nd = [jnp.where(bd8, -a, 0.0) for a in m]
    p2 = [_dotb(a, a) for a in nd]
    p4 = [_dotb(a, a) for a in p2]
    x1 = [eye + nd[i] for i in IT]
    x1 = [x1[i] + _dotb(x1[i], p2[i]) for i in IT]
    xi = [x1[i] + _dotb(x1[i], p4[i]) for i in IT]
    for sh in (3, 4, 5):
        mm = merge_mask(sh)
        t = [_dotb(xi[i], jnp.where(mm, m[i], 0.0)) for i in IT]
        xi = [xi[i] - _dotb(t[i], xi[i]) for i in IT]

    rhs = [jnp.concatenate([v[i] * beta[i], kb[i] * egc[i]], axis=1) for i in IT]
    sol = [_dotb(xi[i], rhs[i]) for i in IT]
    attn = [(_dot_nt(q[i].astype(BF16), k[i].astype(BF16))
             * jnp.where(row >= col, edm[i], 0.0)).astype(BF16) for i in IT]
    qg = [(q[i] * egc[i]).astype(BF16) for i in IT]
    kg = [(k[i] * jnp.exp(gl[i] - gc[i])).astype(BF16) for i in IT]
    egl = [jnp.exp(a) for a in gl]

    s_cur = s_old
    for c in range(R // C):
        it = [c * HEADS + h for h in HS]
        sb = [a.astype(BF16) for a in s_cur]
        v_new = [(sol[i][:, :DH] - _dot(sol[i][:, DH:].astype(BF16), sb[h])).astype(BF16)
                 for h, i in enumerate(it)]
        o = [_dot(qg[i], sb[h]) + _dot(attn[i], v_new[h]) for h, i in enumerate(it)]
        s_cur = [s_cur[h] * egl[i] + _dot_tn(kg[i], v_new[h]) for h, i in enumerate(it)]
        o = [a * lax.rsqrt(jnp.mean(a * a, axis=-1, keepdims=True) + EPS) * gn_ref[...] for a in o]
        o_ref[0, c * C:(c + 1) * C, :] = (jnp.concatenate(o, axis=1)
                                          * gate[c * C:(c + 1) * C]).astype(o_ref.dtype)
    for h in HS:
        s_ref[h] = s_cur[h]


def _deltanet(proj3, projc3, conv_w, a_log, dt_bias, gnorm):
    b, t, _ = proj3.shape
    C = A_ROWS
    lane_row = lambda v: jnp.pad(v.astype(F32), (HEADS, LANES - 2 * HEADS)).reshape(1, LANES)
    arow = lane_row(-jnp.exp(a_log.astype(F32)))
    dtrow = lane_row(dt_bias)
    const = lambda shape: pl.BlockSpec(shape, lambda i, c: (0,) * len(shape))
    return pl.pallas_call(
        _deltanet_kernel,
        grid=(b, t // C),
        in_specs=[pl.BlockSpec((1, C, 3 * HW), lambda i, c: (i, c, OFF_AQKV // (3 * HW))),
                  pl.BlockSpec((1, C, HW), lambda i, c: (i, c, OFF_AGATE // HW)),
                  pl.BlockSpec((1, C, LANES), lambda i, c: (i, c, OFF_BA // LANES)),
                  const((A_CONV, 3 * HW)), const((1, LANES)), const((1, LANES)), const((1, DH))],
        out_specs=pl.BlockSpec((1, C, HW), lambda i, c: (i, c, 0)),
        out_shape=jax.ShapeDtypeStruct((b, t, HW), BF16),
        scratch_shapes=[pltpu.VMEM((HEADS, DH, DH), F32), pltpu.VMEM((C + SUBLANES, 3 * HW), F32)],
        compiler_params=_cparams(("parallel", "arbitrary")),
        name="deltanet",
    )(proj3, proj3, projc3, conv_w.astype(F32), arow, dtrow, gnorm.reshape(1, DH).astype(F32))


def _hgrn_kernel(q_ref, f_ref, i_ref, g_ref, lb_ref, gn_ref, o_ref, st_ref, lg_ref, k_ref):
    R, K = B_ROWS, B_BLK
    HS = range(HEADS)

    @pl.when(pl.program_id(1) == 0)
    def _():
        st_ref[...] = jnp.zeros_like(st_ref)

    lb_all = lb_ref[...]
    fl = f_ref[0].astype(F32)
    lg = jnp.log(lb_all + (1.0 - lb_all) * _sigmoid(fl))
    k_ref[...] = (1.0 - lb_all) * _sigmoid(-fl)
    lg_ref[...] = lg
    brow = lax.broadcasted_iota(jnp.int32, (R // B_FAST, R), 0)
    bcol = lax.broadcasted_iota(jnp.int32, (R // B_FAST, R), 1)
    chunksum = _dot((bcol // B_FAST == brow).astype(BF16), lg.astype(BF16))
    max_decay = jnp.max(-chunksum)

    H8 = SUBLANES
    row8 = lax.broadcasted_iota(jnp.int32, (H8, DH), 0)
    sls = [slice(h * DH, (h + 1) * DH) for h in HS]

    def pairs_exact(q, k, v, gc):
        n_it = range(len(q))
        otile = [[jnp.zeros((H8, DH), F32) for _ in range(K // H8)] for _ in n_it]
        for j in range(K):
            for t in range(j // H8, K // H8):
                ts = slice(t * H8, (t + 1) * H8)
                for i in n_it:
                    d = gc[i][ts] - gc[i][j:j + 1]
                    rel = jnp.exp(jnp.where(row8 >= j - t * H8, d, -1e30) if t == j // H8 else d)
                    sj = jnp.sum(q[i][ts] * k[i][j:j + 1] * rel, axis=-1, keepdims=True)
                    otile[i][t] = otile[i][t] + sj * v[i][j:j + 1]
        return [jnp.concatenate(otile[i], axis=0) for i in n_it]

    def run(exact):
        KB = K if exact else B_FAST
        GR = B_GROUP * KB
        grow = lax.broadcasted_iota(jnp.int32, (GR, GR), 0)
        gcol = lax.broadcasted_iota(jnp.int32, (GR, GR), 1)
        same_block_lower = (gcol <= grow) & (grow // KB == gcol // KB)
        lmat_g = same_block_lower.astype(BF16)
        st = [st_ref[h] for h in HS]
        for r0 in range(0, R, GR):
            gs = slice(r0, r0 + GR)
            blks = [slice(bi * KB, (bi + 1) * KB) for bi in range(B_GROUP)]
            gc_all = _dot_exact_lhs(lmat_g, lg_ref[gs, :])
            gl_all = jnp.concatenate([jnp.broadcast_to(gc_all[bs.stop - 1:bs.stop], (KB, HW)) for bs in blks],
                                     axis=0)
            q = [_silu(q_ref[0, gs, sl].astype(F32)) for sl in sls]
            k = [k_ref[gs, sl] for sl in sls]
            v = [i_ref[0, gs, sl].astype(F32) for sl in sls]
            gc = [gc_all[:, sl] for sl in sls]
            gl = [gl_all[:, sl] for sl in sls]
            kg = [(k[h] * jnp.exp(gl[h] - gc[h])).astype(BF16) for h in HS]
            qs = [(q[h] * jnp.exp(gc[h])).astype(BF16) for h in HS]
            vb = [a.astype(BF16) for a in v]
            upd = [[_dot_tn(vb[h][bs], kg[h][bs]) for h in HS] for bs in blks]
            egl = [[jnp.exp(gl[h][bs.stop - 1:bs.stop]) for h in HS] for bs in blks]
            if exact:
                items = [(h, bs) for bs in blks for h in HS]
                pe = pairs_exact([q[h][bs] for h, bs in items], [k[h][bs] for h, bs in items],
                                 [v[h][bs] for h, bs in items], [gc[h][bs] for h, bs in items])
                intra = [jnp.concatenate([pe[bi * HEADS + h] for bi in range(B_GROUP)], axis=0) for h in HS]
            else:
                qf = [(q[h] * jnp.exp(gc[h] - gl[h])).astype(BF16) for h in HS]
                s = [jnp.where(same_block_lower, _dot_nt(qf[h], kg[h]), 0.0).astype(BF16) for h in HS]
                intra = [_dot(s[h], vb[h]) for h in HS]
            for bi, bs in enumerate(blks):
                o = [_dot_nt(qs[h][bs], st[h].astype(BF16)) + intra[h][bs] for h in HS]
                st = [st[h] * egl[bi][h] + upd[bi][h] for h in HS]
                o = [a * lax.rsqrt(jnp.mean(a * a, axis=-1, keepdims=True) + EPS) * gn_ref[...] for a in o]
                rs = slice(r0 + bi * KB, r0 + (bi + 1) * KB)
                o_ref[0, rs, :] = (jnp.concatenate(o, axis=1)
                                   * _sigmoid(g_ref[0, rs, :].astype(F32))).astype(o_ref.dtype)
        for h in HS:
            st_ref[h] = st[h]

    lax.cond(max_decay > HGRN_SAFE_DECAY, lambda: run(True), lambda: run(False))


def _hgrn(proj3, lb, gnorm):
    b, t, _ = proj3.shape
    R = B_ROWS
    seg = lambda s: pl.BlockSpec((1, R, HW), lambda i, c: (i, c, OFF_B // HW + s))
    const = lambda shape: pl.BlockSpec(shape, lambda i, c: (0,) * len(shape))
    return pl.pallas_call(
        _hgrn_kernel,
        grid=(b, t // R),
        in_specs=[seg(0), seg(1), seg(2), seg(3), const((1, HW)), const((1, DH))],
        out_specs=pl.BlockSpec((1, R, HW), lambda i, c: (i, c, 0)),
        out_shape=jax.ShapeDtypeStruct((b, t, HW), BF16),
        scratch_shapes=[pltpu.VMEM((HEADS, DH, DH), F32), pltpu.VMEM((R, HW), F32),
                        pltpu.VMEM((R, HW), F32)],
        compiler_params=_cparams(("parallel", "arbitrary")),
        name="hgrn2",
    )(proj3, proj3, proj3, proj3, lb.reshape(1, HW).astype(F32), gnorm.reshape(1, DH).astype(F32))


def _t5_bucket_np(n):
    max_exact = REL_BUCKETS // 2
    nf = np.maximum(n, 1).astype(np.float32)
    large = max_exact + (np.log(nf / max_exact) / math.log(REL_MAX_DIST / max_exact)
                         * (REL_BUCKETS - max_exact)).astype(np.int32)
    large = np.minimum(large, REL_BUCKETS - 1)
    return np.where(n < max_exact, n, large)


def _dilated_kernel(span, dil, hpb, nlb, q_ref, kp_ref, kc_ref, vp_ref, vc_ref, bias_ref,
                    o_ref, lse_ref):
    CB = C_BLOCK
    n = pl.program_id(1)
    qi = lax.broadcasted_iota(jnp.int32, (CB, 2 * CB), 0)
    kj = lax.broadcasted_iota(jnp.int32, (CB, 2 * CB), 1)
    dist = qi + CB - kj
    band = (dist >= 0) & (dist <= span)
    band_first = band & ((kj >= CB) | (n > 0))
    seg_lane = lax.broadcasted_iota(jnp.int32, (CB, DH), 1)
    if hpb == 1:
        own_seg = seg_lane // C_LSEG == pl.program_id(2)

        @pl.when(pl.program_id(2) == 0)
        def _():
            lse_ref[...] = jnp.zeros_like(lse_ref)
    else:
        assert hpb == C_TILES == HEADS

    def rows(r, jb):
        return pl.ds(jb * CB * dil + r, CB, stride=dil) if dil > 1 else pl.ds(jb * CB, CB)

    def with_prev(p_ref, c_ref, r, jb, sl):
        prev = p_ref[0, rows(r, 0), sl] if jb == 0 else c_ref[0, rows(r, jb - 1), sl]
        return jnp.concatenate([prev, c_ref[0, rows(r, jb), sl]], axis=0)

    def tiles(items):
        n_it = range(len(items))
        q = [q_ref[0, rows(r, jb), sl].astype(BF16) for r, jb, sl, _ in items]
        k = [with_prev(kp_ref, kc_ref, r, jb, sl).astype(BF16) for r, jb, sl, _ in items]
        v = [with_prev(vp_ref, vc_ref, r, jb, sl).astype(BF16) for r, jb, sl, _ in items]
        s = [_dot_nt(q[i], k[i]) * (DH ** -0.5) + bias_ref[items[i][3]] for i in n_it]
        s = [jnp.where(band_first if items[i][1] == 0 else band, s[i], -1e30) for i in n_it]
        mx = [jnp.max(a, axis=-1, keepdims=True) for a in s]
        p = [jnp.exp(s[i] - mx[i]) for i in n_it]
        den = [jnp.sum(a, axis=-1, keepdims=True) for a in p]
        o = [_dot(p[i].astype(BF16), v[i]) / den[i] for i in n_it]
        lse = [mx[i] + jnp.log(den[i]) for i in n_it]
        for i, (r, jb, sl, _) in enumerate(items):
            o_ref[0, rows(r, jb), sl] = o[i]
        if hpb == 1:
            for i, (r, jb, _, _) in enumerate(items):
                lse_ref[0, rows(r, jb), :] = jnp.where(own_seg, lse[i], lse_ref[0, rows(r, jb), :])
        else:
            row = jnp.zeros((CB, DH), F32)
            for i, (_, _, _, h) in enumerate(items):
                row = jnp.where(seg_lane // C_LSEG == h, lse[i], row)
            lse_ref[0, rows(items[0][0], items[0][1]), :] = row

    for jb in range(nlb):
        if dil == 1:
            for h0 in range(0, hpb, C_TILES):
                tiles([(0, jb, slice(h * DH, (h + 1) * DH), h) for h in range(h0, h0 + C_TILES)])
        elif dil == C_TILES:
            tiles([(r, jb, slice(0, DH), 0) for r in range(dil)])
        else:
            def body(g, carry, jb=jb):
                tiles([(g * C_TILES + i, jb, slice(0, DH), 0) for i in range(C_TILES)])
                return carry
            lax.fori_loop(0, dil // C_TILES, body, 0)


def _dilated_group(proj3, gi, rel_bias):
    b, t, _ = proj3.shape
    window, dil = C_PAIRS[gi]
    span = window // dil
    CB = C_BLOCK
    pb = CB * dil
    nlb = min(C_NLB, t // pb)
    rb = nlb * pb
    hpb = HEADS if dil == 1 else 1
    bw = hpb * DH
    qi = np.arange(CB)[:, None]
    kj = np.arange(2 * CB)[None, :]
    bucket = _t5_bucket_np(np.maximum(qi + CB - kj, 0) * dil)
    onehot = jnp.asarray(np.eye(REL_BUCKETS, dtype=np.float32)[bucket])
    bias = jnp.einsum("qkb,bh->hqk", onehot, rel_bias[:, gi * HEADS:(gi + 1) * HEADS].astype(F32),
                      precision=HIGHEST)

    def seg(which, prev):
        base = (OFF_C + which * C_GROUPS * HW + gi * HW) // bw
        if prev:
            return pl.BlockSpec((1, pb, bw), lambda i, n, h: (i, jnp.maximum(n * nlb - 1, 0), base + h))
        return pl.BlockSpec((1, rb, bw), lambda i, n, h: (i, n, base + h))

    o, lse = pl.pallas_call(
        functools.partial(_dilated_kernel, span, dil, hpb, nlb),
        grid=(b, t // rb, HEADS // hpb),
        in_specs=[seg(0, False), seg(1, True), seg(1, False), seg(2, True), seg(2, False),
                  pl.BlockSpec((hpb, CB, 2 * CB), lambda i, n, h: (h, 0, 0))],
        out_specs=[pl.BlockSpec((1, rb, bw), lambda i, n, h: (i, n, h)),
                   pl.BlockSpec((1, rb, DH), lambda i, n, h: (i, n, 0))],
        out_shape=[jax.ShapeDtypeStruct((b, t, HW), F32),
                   jax.ShapeDtypeStruct((b, t, DH), F32)],
        compiler_params=_cparams(("parallel", "arbitrary", "arbitrary")),
        name=f"dilated_g{gi}",
    )(proj3, proj3, proj3, proj3, proj3, bias)
    return o.reshape(b * t, HW), lse.reshape(b * t, DH)


def _mix_kernel(x_ref, gate_ref, oa_ref, ob_ref, oc0_ref, oc1_ref, oc2_ref,
                l0_ref, l1_ref, l2_ref, wa_ref, wb_ref, wc_ref, wo_ref, out_ref):
    l0, l1, l2 = l0_ref[...], l1_ref[...], l2_ref[...]
    mx = jnp.maximum(jnp.maximum(l0, l1), l2)
    e0, e1, e2 = jnp.exp(l0 - mx), jnp.exp(l1 - mx), jnp.exp(l2 - mx)
    inv = 1.0 / (e0 + e1 + e2)

    def spread(w):
        return jnp.concatenate([jnp.broadcast_to(w[:, h * C_LSEG:h * C_LSEG + 1], (w.shape[0], DH))
                                for h in range(HEADS)], axis=1)

    oc = (spread(e0 * inv) * oc0_ref[...] + spread(e1 * inv) * oc1_ref[...]
          + spread(e2 * inv) * oc2_ref[...])
    mix = (_sigmoid(gate_ref[:, 0:D_MODEL].astype(F32)) * _dot(oa_ref[...], wa_ref[...])
           + _sigmoid(gate_ref[:, D_MODEL:2 * D_MODEL].astype(F32)) * _dot(ob_ref[...], wb_ref[...])
           + _sigmoid(gate_ref[:, 2 * D_MODEL:3 * D_MODEL].astype(F32)) * _dot(oc.astype(BF16), wc_ref[...]))
    out_ref[...] = x_ref[...] + _dot(mix.astype(BF16), wo_ref[...])


def _mix(x2, proj2, oa, ob, ocs, lses, wa, wb, wc, wo, tm=512):
    m = x2.shape[0]
    tm = min(tm, m)
    rowblk = lambda w: pl.BlockSpec((tm, w), lambda i: (i, 0))
    const = lambda shape: pl.BlockSpec(shape, lambda i: (0,) * len(shape))
    return pl.pallas_call(
        _mix_kernel,
        grid=(m // tm,),
        in_specs=[rowblk(D_MODEL),
                  pl.BlockSpec((tm, 3 * D_MODEL), lambda i: (i, OFF_BRG // (3 * D_MODEL))),
                  rowblk(HW), rowblk(HW),
                  rowblk(HW), rowblk(HW), rowblk(HW), rowblk(DH), rowblk(DH), rowblk(DH),
                  const((HW, D_MODEL)), const((HW, D_MODEL)), const((HW, D_MODEL)),
                  const((D_MODEL, D_MODEL))],
        out_specs=rowblk(D_MODEL),
        out_shape=jax.ShapeDtypeStruct((m, D_MODEL), F32),
        compiler_params=_cparams(("parallel",)),
        name="branch_mix",
    )(x2, proj2, oa, ob, ocs[0], ocs[1], ocs[2], lses[0], lses[1], lses[2],
      wa.astype(BF16), wb.astype(BF16), wc.astype(BF16), wo.astype(BF16))


def _ffn_kernel(x_ref, g_ref, w1_ref, w3_ref, w2_ref, o_ref):
    x = x_ref[...]
    ms = jnp.mean(x * x, axis=-1, keepdims=True)
    h = (x * lax.rsqrt(ms + EPS) * g_ref[...]).astype(BF16)
    a = _dot(h, w1_ref[...])
    b = _dot(h, w3_ref[...])
    o_ref[...] = x + _dot((_silu(a) * b).astype(BF16), w2_ref[...])


def _ffn(x2, gain, w1, w3, w2, tm=512):
    m = x2.shape[0]
    ff = w1.shape[1]
    tm = min(tm, m)
    resident = lambda shape: pl.BlockSpec(shape, lambda i: (0, 0), pipeline_mode=pl.Buffered(1))
    return pl.pallas_call(
        _ffn_kernel,
        grid=(m // tm,),
        in_specs=[pl.BlockSpec((tm, D_MODEL), lambda i: (i, 0)),
                  pl.BlockSpec((1, D_MODEL), lambda i: (0, 0)),
                  resident((D_MODEL, ff)), resident((D_MODEL, ff)), resident((ff, D_MODEL))],
        out_specs=pl.BlockSpec((tm, D_MODEL), lambda i: (i, 0)),
        out_shape=jax.ShapeDtypeStruct((m, D_MODEL), F32),
        compiler_params=_cparams(("parallel",)),
        name="ffn",
    )(x2, gain.reshape(1, D_MODEL).astype(F32), w1.astype(BF16), w3.astype(BF16), w2.astype(BF16))


HALF = D_MODEL // 2
U32 = jnp.uint32


def _pack_bf16_pairs(x):
    r = x.astype(BF16).astype(F32)
    lo = lax.bitcast_convert_type(r[:, :HALF], U32) >> 16
    hi = lax.bitcast_convert_type(r[:, HALF:], U32)
    return hi | lo


def _unpack_bf16_pairs(p):
    lo = lax.bitcast_convert_type(p << 16, F32)
    hi = lax.bitcast_convert_type(p & U32(0xFFFF0000), F32)
    return lo, hi


def _router_kernel(x_ref, g_ref, wr_ref, h_ref, ids_ref, ps_ref):
    x = x_ref[...]
    ms = jnp.mean(x * x, axis=-1, keepdims=True)
    h = x * lax.rsqrt(ms + EPS) * g_ref[...]
    h_ref[...] = _pack_bf16_pairs(h)
    hh = h.astype(BF16)
    hl = (h - hh.astype(F32)).astype(BF16)
    w = wr_ref[...]
    wh = w.astype(BF16)
    wl = (w - wh.astype(F32)).astype(BF16)
    logits = _dot(hh, wh) + (_dot(hh, wl) + _dot(hl, wh))
    lane = lax.broadcasted_iota(jnp.int32, logits.shape, 1)
    neg = jnp.float32(-jnp.inf)
    l1 = jnp.where(lane < N_EXPERTS, logits, neg)
    m1 = jnp.max(l1, axis=-1, keepdims=True)
    i1 = jnp.min(jnp.where(l1 == m1, lane, LANES), axis=-1, keepdims=True)
    l2 = jnp.where(lane == i1, neg, l1)
    m2 = jnp.max(l2, axis=-1, keepdims=True)
    i2 = jnp.min(jnp.where(l2 == m2, lane, LANES), axis=-1, keepdims=True)
    e = jnp.exp(m2 - m1)
    p1 = 1.0 / (1.0 + e)
    p2 = e / (1.0 + e)
    ids_ref[...] = jnp.where(lane == 0, i1, jnp.where(lane == 1, i2, 0))
    ps_ref[...] = jnp.where(lane == 0, p1, jnp.where(lane == 1, p2, 0.0))


def _router(x2, gain, w_router, tm=1024):
    m = x2.shape[0]
    tm = min(tm, m)
    wr = jnp.pad(w_router.astype(F32), ((0, 0), (0, LANES - N_EXPERTS)))
    return pl.pallas_call(
        _router_kernel,
        grid=(m // tm,),
        in_specs=[pl.BlockSpec((tm, D_MODEL), lambda i: (i, 0)),
                  pl.BlockSpec((1, D_MODEL), lambda i: (0, 0)),
                  pl.BlockSpec((D_MODEL, LANES), lambda i: (0, 0))],
        out_specs=[pl.BlockSpec((tm, HALF), lambda i: (i, 0)),
                   pl.BlockSpec((tm, LANES), lambda i: (i, 0)),
                   pl.BlockSpec((tm, LANES), lambda i: (i, 0))],
        out_shape=[jax.ShapeDtypeStruct((m, HALF), U32),
                   jax.ShapeDtypeStruct((m, LANES), jnp.int32),
                   jax.ShapeDtypeStruct((m, LANES), F32)],
        compiler_params=_cparams(("parallel",)),
        name="router",
    )(x2, gain.reshape(1, D_MODEL).astype(F32), wr)


def _route_positions(ids, tm):
    m = ids.shape[0]
    e_flat = ids[:, :TOP_K].reshape(-1)
    onehot = (e_flat[:, None] == jnp.arange(N_EXPERTS)[None, :]).astype(jnp.int32)
    csum = jnp.cumsum(onehot, axis=0)
    counts = csum[-1]
    gsz = ((counts + tm - 1) // tm) * tm
    gend = jnp.cumsum(gsz)
    pos = jnp.sum(onehot * (gend - gsz + csum - 1), axis=1)
    nt = (TOP_K * m + N_EXPERTS * tm) // tm
    n_used = gend[-1] // tm
    tile_e = jnp.sum((jnp.arange(nt)[:, None] * tm >= gend[None, :]).astype(jnp.int32), axis=1)
    last_e = jnp.sum(((n_used - 1) * tm >= gend).astype(jnp.int32))
    tile_e = jnp.minimum(tile_e, last_e)
    return pos.astype(jnp.int32), tile_e.astype(jnp.int32), n_used.reshape(1).astype(jnp.int32), nt


def _dispatch_kernel(pos_ref, h_ref, init_ref, xs_ref, sem):
    del init_ref
    td = h_ref.shape[0]

    def row_copy(t, s):
        return pltpu.make_async_copy(h_ref.at[pl.ds(t, 1)],
                                     xs_ref.at[pl.ds(pos_ref[0, 0, TOP_K * t + s], 1)], sem)

    def start(t, c):
        for s in range(TOP_K):
            row_copy(t, s).start()
        return c

    lax.fori_loop(0, td, start, 0, unroll=8)
    for s in range(TOP_K):
        pltpu.make_async_copy(h_ref, xs_ref.at[pl.ds(0, td)], sem).wait()


def _dispatch(h, pos, npad, td=1024):
    m = h.shape[0]
    td = min(td, m)
    return pl.pallas_call(
        _dispatch_kernel,
        grid=(m // td,),
        in_specs=[pl.BlockSpec((1, 1, TOP_K * td), lambda i: (i, 0, 0), memory_space=pltpu.SMEM),
                  pl.BlockSpec((td, HALF), lambda i: (i, 0)),
                  pl.BlockSpec(memory_space=pl.ANY)],
        out_specs=pl.BlockSpec(memory_space=pl.ANY),
        out_shape=jax.ShapeDtypeStruct((npad, HALF), U32),
        scratch_shapes=[pltpu.SemaphoreType.DMA(())],
        input_output_aliases={2: 0},
        compiler_params=_cparams(("arbitrary",)),
        name="moe_dispatch",
    )(pos.reshape(m // td, 1, TOP_K * td), h, jnp.zeros((npad, HALF), U32))


def _experts_kernel(te_ref, nu_ref, xs_ref, w1_ref, w3_ref, w2_ref, y_ref, xb_ref, acc_ref):
    del te_ref
    i = pl.program_id(0)
    j = pl.program_id(1)

    @pl.when(i < nu_ref[0])
    def _():
        @pl.when(j == 0)
        def _():
            lo, hi = _unpack_bf16_pairs(xs_ref[...])
            xb_ref[...] = jnp.concatenate([lo.astype(BF16), hi.astype(BF16)], axis=1)
            acc_ref[...] = jnp.zeros_like(acc_ref)

        xb = xb_ref[...]
        a = _dot(xb, w1_ref[0])
        b = _dot(xb, w3_ref[0])
        acc_ref[...] += _dot((_silu(a) * b).astype(BF16), w2_ref[0])

        @pl.when(j == pl.num_programs(1) - 1)
        def _():
            y_ref[...] = _pack_bf16_pairs(acc_ref[...])

    @pl.when((i >= nu_ref[0]) & (j == 0))
    def _():
        y_ref[...] = jnp.zeros_like(y_ref)


def _experts(xs, tile_e, n_used, w1, w3, w2, tm, tf=1792):
    npad = xs.shape[0]
    ff = w1.shape[2]
    nj = ff // tf
    row = lambda i, j, te, nu: (jnp.minimum(i, nu[0] - 1), 0)
    jj = lambda i, j, nu: jnp.where(i < nu[0], j, nj - 1)
    return pl.pallas_call(
        _experts_kernel,
        grid_spec=pltpu.PrefetchScalarGridSpec(
            num_scalar_prefetch=2,
            grid=(npad // tm, nj),
            in_specs=[pl.BlockSpec((tm, HALF), row),
                      pl.BlockSpec((1, D_MODEL, tf), lambda i, j, te, nu: (te[i], 0, jj(i, j, nu))),
                      pl.BlockSpec((1, D_MODEL, tf), lambda i, j, te, nu: (te[i], 0, jj(i, j, nu))),
                      pl.BlockSpec((1, tf, D_MODEL), lambda i, j, te, nu: (te[i], jj(i, j, nu), 0))],
            out_specs=pl.BlockSpec((tm, HALF), lambda i, j, te, nu: (i, 0)),
            scratch_shapes=[pltpu.VMEM((tm, D_MODEL), BF16), pltpu.VMEM((tm, D_MODEL), F32)]),
        out_shape=jax.ShapeDtypeStruct((npad, HALF), U32),
        compiler_params=_cparams(("arbitrary", "arbitrary")),
        name="moe_experts",
    )(tile_e, n_used, xs, w1.astype(BF16), w3.astype(BF16), w2.astype(BF16))


def _combine_kernel(pos_ref, x_ref, ps_ref, y_ref, o_ref, buf_ref, sem):
    td = x_ref.shape[0]

    def row_copy(t, s):
        return pltpu.make_async_copy(y_ref.at[pl.ds(pos_ref[0, 0, TOP_K * t + s], 1)],
                                     buf_ref.at[s, pl.ds(t, 1)], sem.at[s])

    def start(t, c):
        for s in range(TOP_K):
            row_copy(t, s).start()
        return c

    lax.fori_loop(0, td, start, 0, unroll=8)
    for s in range(TOP_K):
        pltpu.make_async_copy(y_ref.at[pl.ds(0, td)], buf_ref.at[s], sem.at[s]).wait()
    ps = ps_ref[...]
    lo0, hi0 = _unpack_bf16_pairs(buf_ref[0])
    lo1, hi1 = _unpack_bf16_pairs(buf_ref[1])
    o_ref[:, :HALF] = x_ref[:, :HALF] + ps[:, 0:1] * lo0 + ps[:, 1:2] * lo1
    o_ref[:, HALF:] = x_ref[:, HALF:] + ps[:, 0:1] * hi0 + ps[:, 1:2] * hi1


def _combine(x2, ps, pos, y, td=1024):
    m = x2.shape[0]
    td = min(td, m)
    return pl.pallas_call(
        _combine_kernel,
        grid=(m // td,),
        in_specs=[pl.BlockSpec((1, 1, TOP_K * td), lambda i: (i, 0, 0), memory_space=pltpu.SMEM),
                  pl.BlockSpec((td, D_MODEL), lambda i: (i, 0)),
                  pl.BlockSpec((td, LANES), lambda i: (i, 0)),
                  pl.BlockSpec(memory_space=pl.ANY)],
        out_specs=pl.BlockSpec((td, D_MODEL), lambda i: (i, 0)),
        out_shape=jax.ShapeDtypeStruct((m, D_MODEL), F32),
        scratch_shapes=[pltpu.VMEM((TOP_K, td, HALF), U32), pltpu.SemaphoreType.DMA((TOP_K,))],
        compiler_params=_cparams(("arbitrary",)),
        name="moe_combine",
    )(pos.reshape(m // td, 1, TOP_K * td), x2, ps, y)


def _moe(x2, gain, w_router, w1, w3, w2, tm=1024):
    m = x2.shape[0]
    tm = min(tm, m)
    h, ids, ps = _router(x2, gain, w_router)
    pos, tile_e, n_used, nt = _route_positions(ids, tm)
    xs = _dispatch(h, pos, nt * tm)
    y = _experts(xs, tile_e, n_used, w1, w3, w2, tm)
    return _combine(x2, ps, pos, y)


def _relayout_w_in(w):
    k = w.shape[0]
    main = [w[:, _R_BRG:_R_END], w[:, _R_AQKV:_R_BETA], w[:, _R_AGATE:_R_B], w[:, _R_B:_R_C]]
    second = [w[:, _R_C:_R_BRG], w[:, _R_BETA:_R_AGATE],
              jnp.zeros((k, NP_C - OFF_BA - 2 * HEADS), w.dtype)]
    return (jnp.concatenate(main, axis=1).astype(BF16), jnp.concatenate(second, axis=1).astype(BF16))


def _layer_mixers(x2, b, t, layer, lower_bounds, w_in, norm_mix, conv_a, a_log, dt_bias,
                  gnorm_a, gnorm_b, qnorm_c, knorm_c, rel_bias, w_br_a, w_br_b, w_br_c, w_out):
    w_main, w_second = _relayout_w_in(w_in[layer])
    gain = norm_mix[layer].astype(F32)
    proj2 = _norm_proj(x2, gain, w_main, BF16)
    proj3 = proj2.reshape(b, t, NP_MAIN)
    qk_gain = jnp.concatenate([jnp.tile(qnorm_c[layer].astype(F32), (1, HEADS)).reshape(-1),
                               jnp.tile(knorm_c[layer].astype(F32), (1, HEADS)).reshape(-1)])
    projc3 = _norm_proj(x2, gain, w_second, F32, head_gain=qk_gain).reshape(b, t, NP_C)
    oa = _deltanet(proj3, projc3, conv_a[layer], a_log[layer], dt_bias[layer], gnorm_a[layer])
    ob = _hgrn(proj3, lower_bounds[layer], gnorm_b[layer])
    ocs, lses = [], []
    for gi in range(C_GROUPS):
        o, lse = _dilated_group(projc3, gi, rel_bias)
        ocs.append(o)
        lses.append(lse)
    return _mix(x2, proj2, oa.reshape(b * t, HW), ob.reshape(b * t, HW), ocs, lses,
                w_br_a[layer], w_br_b[layer], w_br_c[layer], w_out[layer])


def kernel(x, w_in, norm_mix, conv_a, a_log, dt_bias, gnorm_a, lb_logits, gnorm_b, qnorm_c, knorm_c, rel_bias, w_br_a, w_br_b, w_br_c, w_out, norm_ffn, ffn_w1, ffn_w3, ffn_w2, router, moe_w1, moe_w3, moe_w2):
    b, t, _ = x.shape
    depth = w_in.shape[0]
    p_lb = jax.nn.softmax(lb_logits.astype(F32), axis=0)
    lower_bounds = jnp.cumsum(p_lb, axis=0) - p_lb[0:1]
    x2 = x.reshape(b * t, D_MODEL).astype(F32)
    for layer in range(depth):
        x2 = _layer_mixers(x2, b, t, layer, lower_bounds, w_in, norm_mix, conv_a, a_log, dt_bias,
                           gnorm_a, gnorm_b, qnorm_c, knorm_c, rel_bias,
                           w_br_a, w_br_b, w_br_c, w_out)
        li = layer // 2
        if layer % 2 == 0:
            x2 = _ffn(x2, norm_ffn[layer], ffn_w1[li], ffn_w3[li], ffn_w2[li])
        else:
            x2 = _moe(x2, norm_ffn[layer], router[li], moe_w1[li], moe_w3[li], moe_w2[li])
    return x2.reshape(b, t, D_MODEL).astype(x.dtype)
```

```python
import functools
import math

import numpy as np
import jax
import jax.numpy as jnp
from jax import lax
from jax.experimental import pallas as pl
from jax.experimental.pallas import tpu as pltpu

F32 = jnp.float32
BF16 = jnp.bfloat16
HIGHEST = lax.Precision.HIGHEST

LANES = 128
SUBLANES = 8

D_MODEL = 1024
EPS = 1e-6
HEADS = 4
DH = 128
HW = HEADS * DH
A_CONV = 4
A_CHUNK = 64
A_ROWS = 512
B_ROWS = 512
B_BLK = 16
B_GROUP = 8
B_FAST = 32
HGRN_SAFE_DECAY = 60.0
C_PAIRS = ((128, 1), (512, 4), (2048, 16))
C_GROUPS = 3
C_BLOCK = 128
C_TILES = 4
C_LSEG = DH // HEADS
C_NLB = 8
REL_BUCKETS = 32
REL_MAX_DIST = 2048
N_EXPERTS = 8
TOP_K = 2

OFF_BRG = 0
OFF_AQKV = 3072
OFF_AGATE = 4608
OFF_B = 5120
NP_MAIN = 7168
OFF_C = 0
OFF_BA = 4608
NP_C = 5120

_R_AQKV, _R_BETA, _R_AGATE, _R_B, _R_C, _R_BRG, _R_END = 0, 1536, 1544, 2056, 4104, 8712, 11784

VMEM_LIMIT = 56 * 1024 * 1024


def _cparams(sem):
    return pltpu.CompilerParams(dimension_semantics=sem, vmem_limit_bytes=VMEM_LIMIT)


def _sigmoid(x):
    return 1.0 / (1.0 + jnp.exp(-x))


def _silu(x):
    return x * (0.5 * jnp.tanh(0.5 * x) + 0.5)


def _softplus(x):
    return jnp.maximum(x, 0.0) + jnp.log(1.0 + jnp.exp(-jnp.abs(x)))


def _dot(a, b):
    return jnp.dot(a, b, preferred_element_type=F32)


def _dot_nt(a, b, precision=None):
    return lax.dot_general(a, b, (((1,), (1,)), ((), ())), precision=precision,
                           preferred_element_type=F32)


def _dot_tn(a, b):
    return lax.dot_general(a, b, (((0,), (0,)), ((), ())), preferred_element_type=F32)


def _norm_proj_kernel(tn, head_norm_cols, x_ref, g_ref, w_ref, hg_ref, o_ref):
    x = x_ref[...]
    ms = jnp.mean(x * x, axis=-1, keepdims=True)
    h = (x * lax.rsqrt(ms + EPS) * g_ref[...]).astype(BF16)
    for c0 in range(0, w_ref.shape[1], tn):
        r = _dot(h, w_ref[:, c0:c0 + tn])
        if c0 < head_norm_cols:
            heads = [r[:, d0:d0 + DH] for d0 in range(0, tn, DH)]
            heads = [a * lax.rsqrt(jnp.mean(a * a, axis=-1, keepdims=True) + EPS) for a in heads]
            r = jnp.concatenate(heads, axis=1) * hg_ref[:, c0:c0 + tn]
        o_ref[:, c0:c0 + tn] = r.astype(o_ref.dtype)


def _norm_proj(x2, gain, w_bf16, out_dtype, head_gain=None, tm=512, tn=1024):
    m = x2.shape[0]
    n = w_bf16.shape[1]
    tm = min(tm, m)
    norm_cols = 0 if head_gain is None else head_gain.shape[0]
    assert norm_cols % tn == 0
    hg = jnp.zeros((1, n), F32)
    if head_gain is not None:
        hg = hg.at[0, :norm_cols].set(head_gain.astype(F32))
    return pl.pallas_call(
        functools.partial(_norm_proj_kernel, tn, norm_cols),
        grid=(m // tm,),
        in_specs=[pl.BlockSpec((tm, D_MODEL), lambda i: (i, 0)),
                  pl.BlockSpec((1, D_MODEL), lambda i: (0, 0)),
                  pl.BlockSpec((D_MODEL, n), lambda i: (0, 0), pipeline_mode=pl.Buffered(1)),
                  pl.BlockSpec((1, n), lambda i: (0, 0))],
        out_specs=pl.BlockSpec((tm, n), lambda i: (i, 0)),
        out_shape=jax.ShapeDtypeStruct((m, n), out_dtype),
        compiler_params=_cparams(("parallel",)),
        name="norm_proj",
    )(x2, gain.reshape(1, D_MODEL), w_bf16, hg)


def _dotb(a, b):
    return _dot(a.astype(BF16), b.astype(BF16))


def _dot_exact_lhs(a_bf16, b):
    b0 = b.astype(BF16)
    r1 = b - b0.astype(F32)
    b1 = r1.astype(BF16)
    b2 = (r1 - b1.astype(F32)).astype(BF16)
    return _dot(a_bf16, b0) + (_dot(a_bf16, b1) + _dot(a_bf16, b2))


def _deltanet_kernel(qkv_ref, gate_ref, ba_ref, convw_ref, arow_ref, dtrow_ref, gn_ref,
                     o_ref, s_ref, xe_ref):
    C = A_CHUNK
    R = A_ROWS
    HS = range(HEADS)
    IT = range((R // C) * HEADS)

    @pl.when(pl.program_id(1) == 0)
    def _():
        s_ref[...] = jnp.zeros_like(s_ref)
        xe_ref[0:SUBLANES, :] = jnp.zeros((SUBLANES, 3 * HW), F32)

    x = qkv_ref[0].astype(F32)
    xe_ref[SUBLANES:SUBLANES + R, :] = x
    w = convw_ref[...]
    y = w[A_CONV - 1:A_CONV] * x
    for d in range(1, A_CONV):
        y = y + w[A_CONV - 1 - d:A_CONV - d] * xe_ref[SUBLANES - d:SUBLANES - d + R, :]
    xe_ref[0:SUBLANES, :] = x[R - SUBLANES:R]
    y = _silu(y)

    ba = ba_ref[0]
    beta_all = _sigmoid(ba)
    g_all = arow_ref[...] * _softplus(ba + dtrow_ref[...])
    gate = _silu(gate_ref[0].astype(F32))
    s_old = [s_ref[h] for h in HS]

    row = lax.broadcasted_iota(jnp.int32, (C, C), 0)
    col = lax.broadcasted_iota(jnp.int32, (C, C), 1)
    lmat = (col <= row).astype(BF16)
    rowx = lax.broadcasted_iota(jnp.int32, (C, DH + C), 0)
    colx = lax.broadcasted_iota(jnp.int32, (C, DH + C), 1)
    umask = (colx < DH) | (rowx > colx - DH)
    eye = (row == col).astype(F32)
    bd8 = (row >> 3) == (col >> 3)

    def merge_mask(sh):
        return (((row >> (sh + 1)) == (col >> (sh + 1)))
                & (((row >> sh) & 1) == 1) & (((col >> sh) & 1) == 0))

    rows = [slice((i // HEADS) * C, (i // HEADS + 1) * C) for i in IT]
    hd = [i % HEADS for i in IT]
    q = [y[rows[i], hd[i] * DH:(hd[i] + 1) * DH] for i in IT]
    k = [y[rows[i], HW + hd[i] * DH:HW + (hd[i] + 1) * DH] for i in IT]
    v = [y[rows[i], 2 * HW + hd[i] * DH:2 * HW + (hd[i] + 1) * DH] for i in IT]
    q = [a * (lax.rsqrt(jnp.sum(a * a, axis=-1, keepdims=True) + EPS) * (DH ** -0.5)) for a in q]
    k = [a * lax.rsqrt(jnp.sum(a * a, axis=-1, keepdims=True) + EPS) for a in k]
    beta = [beta_all[rows[i], hd[i]:hd[i] + 1] for i in IT]
    gb = [jnp.broadcast_to(g_all[rows[i], HEADS + hd[i]:HEADS + hd[i] + 1], (C, DH + C)) for i in IT]
    dext = [_dot_exact_lhs(lmat, jnp.where(umask, gb[i], 0.0)) for i in IT]
    gc = [d[:, :DH] for d in dext]
    edm = [jnp.exp(d[:, DH:]) for d in dext]
    egc = [jnp.exp(g) for g in gc]
    gl = [g[C - 1:C, :] for g in gc]
    kb = [k[i] * beta[i] for i in IT]
    m = [_dot_nt(kb[i].astype(BF16), k[i].astype(BF16)) * jnp.where(row > col, edm[i], 0.0)
         for i in IT]

    nd = [jnp.where(bd8, -a, 0.0) for a in m]
    p2 = [_dotb(a, a) for a in nd]
    p4 = [_dotb(a, a) for a in p2]
    x1 = [eye + nd[i] for i in IT]
    x1 = [x1[i] + _dotb(x1[i], p2[i]) for i in IT]
    xi = [x1[i] + _dotb(x1[i], p4[i]) for i in IT]
    for sh in (3, 4, 5):
        mm = merge_mask(sh)
        t = [_dotb(xi[i], jnp.where(mm, m[i], 0.0)) for i in IT]
        xi = [xi[i] - _dotb(t[i], xi[i]) for i in IT]

    rhs = [jnp.concatenate([v[i] * beta[i], kb[i] * egc[i]], axis=1) for i in IT]
    sol = [_dotb(xi[i], rhs[i]) for i in IT]
    attn = [(_dot_nt(q[i].astype(BF16), k[i].astype(BF16))
             * jnp.where(row >= col, edm[i], 0.0)).astype(BF16) for i in IT]
    qg = [(q[i] * egc[i]).astype(BF16) for i in IT]
    kg = [(k[i] * jnp.exp(gl[i] - gc[i])).astype(BF16) for i in IT]
    egl = [jnp.exp(a) for a in gl]

    s_cur = s_old
    for c in range(R // C):
        it = [c * HEADS + h for h in HS]
        sb = [a.astype(BF16) for a in s_cur]
        v_new = [(sol[i][:, :DH] - _dot(sol[i][:, DH:].astype(BF16), sb[h])).astype(BF16)
                 for h, i in enumerate(it)]
        o = [_dot(qg[i], sb[h]) + _dot(attn[i], v_new[h]) for h, i in enumerate(it)]
        s_cur = [s_cur[h] * egl[i] + _dot_tn(kg[i], v_new[h]) for h, i in enumerate(it)]
        o = [a * lax.rsqrt(jnp.mean(a * a, axis=-1, keepdims=True) + EPS) * gn_ref[...] for a in o]
        o_ref[0, c * C:(c + 1) * C, :] = (jnp.concatenate(o, axis=1)
                                          * gate[c * C:(c + 1) * C]).astype(o_ref.dtype)
    for h in HS:
        s_ref[h] = s_cur[h]


def _deltanet(proj3, projc3, conv_w, a_log, dt_bias, gnorm):
    b, t, _ = proj3.shape
    C = A_ROWS
    lane_row = lambda v: jnp.pad(v.astype(F32), (HEADS, LANES - 2 * HEADS)).reshape(1, LANES)
    arow = lane_row(-jnp.exp(a_log.astype(F32)))
    dtrow = lane_row(dt_bias)
    const = lambda shape: pl.BlockSpec(shape, lambda i, c: (0,) * len(shape))
    return pl.pallas_call(
        _deltanet_kernel,
        grid=(b, t // C),
        in_specs=[pl.BlockSpec((1, C, 3 * HW), lambda i, c: (i, c, OFF_AQKV // (3 * HW))),
                  pl.BlockSpec((1, C, HW), lambda i, c: (i, c, OFF_AGATE // HW)),
                  pl.BlockSpec((1, C, LANES), lambda i, c: (i, c, OFF_BA // LANES)),
                  const((A_CONV, 3 * HW)), const((1, LANES)), const((1, LANES)), const((1, DH))],
        out_specs=pl.BlockSpec((1, C, HW), lambda i, c: (i, c, 0)),
        out_shape=jax.ShapeDtypeStruct((b, t, HW), BF16),
        scratch_shapes=[pltpu.VMEM((HEADS, DH, DH), F32), pltpu.VMEM((C + SUBLANES, 3 * HW), F32)],
        compiler_params=_cparams(("parallel", "arbitrary")),
        name="deltanet",
    )(proj3, proj3, projc3, conv_w.astype(F32), arow, dtrow, gnorm.reshape(1, DH).astype(F32))


def _hgrn_kernel(q_ref, f_ref, i_ref, g_ref, lb_ref, gn_ref, o_ref, st_ref, lg_ref, k_ref):
    R, K = B_ROWS, B_BLK
    HS = range(HEADS)

    @pl.when(pl.program_id(1) == 0)
    def _():
        st_ref[...] = jnp.zeros_like(st_ref)

    lb_all = lb_ref[...]
    fl = f_ref[0].astype(F32)
    lg = jnp.log(lb_all + (1.0 - lb_all) * _sigmoid(fl))
    k_ref[...] = (1.0 - lb_all) * _sigmoid(-fl)
    lg_ref[...] = lg
    brow = lax.broadcasted_iota(jnp.int32, (R // B_FAST, R), 0)
    bcol = lax.broadcasted_iota(jnp.int32, (R // B_FAST, R), 1)
    chunksum = _dot((bcol // B_FAST == brow).astype(BF16), lg.astype(BF16))
    max_decay = jnp.max(-chunksum)

    H8 = SUBLANES
    row8 = lax.broadcasted_iota(jnp.int32, (H8, DH), 0)
    sls = [slice(h * DH, (h + 1) * DH) for h in HS]

    def pairs_exact(q, k, v, gc):
        n_it = range(len(q))
        otile = [[jnp.zeros((H8, DH), F32) for _ in range(K // H8)] for _ in n_it]
        for j in range(K):
            for t in range(j // H8, K // H8):
                ts = slice(t * H8, (t + 1) * H8)
                for i in n_it:
                    d = gc[i][ts] - gc[i][j:j + 1]
                    rel = jnp.exp(jnp.where(row8 >= j - t * H8, d, -1e30) if t == j // H8 else d)
                    sj = jnp.sum(q[i][ts] * k[i][j:j + 1] * rel, axis=-1, keepdims=True)
                    otile[i][t] = otile[i][t] + sj * v[i][j:j + 1]
        return [jnp.concatenate(otile[i], axis=0) for i in n_it]

    def run(exact):
        KB = K if exact else B_FAST
        GR = B_GROUP * KB
        grow = lax.broadcasted_iota(jnp.int32, (GR, GR), 0)
        gcol = lax.broadcasted_iota(jnp.int32, (GR, GR), 1)
        same_block_lower = (gcol <= grow) & (grow // KB == gcol // KB)
        lmat_g = same_block_lower.astype(BF16)
        st = [st_ref[h] for h in HS]
        for r0 in range(0, R, GR):
            gs = slice(r0, r0 + GR)
            blks = [slice(bi * KB, (bi + 1) * KB) for bi in range(B_GROUP)]
            gc_all = _dot_exact_lhs(lmat_g, lg_ref[gs, :])
            gl_all = jnp.concatenate([jnp.broadcast_to(gc_all[bs.stop - 1:bs.stop], (KB, HW)) for bs in blks],
                                     axis=0)
            q = [_silu(q_ref[0, gs, sl].astype(F32)) for sl in sls]
            k = [k_ref[gs, sl] for sl in sls]
            v = [i_ref[0, gs, sl].astype(F32) for sl in sls]
            gc = [gc_all[:, sl] for sl in sls]
            gl = [gl_all[:, sl] for sl in sls]
            kg = [(k[h] * jnp.exp(gl[h] - gc[h])).astype(BF16) for h in HS]
            qs = [(q[h] * jnp.exp(gc[h])).astype(BF16) for h in HS]
            vb = [a.astype(BF16) for a in v]
            upd = [[_dot_tn(vb[h][bs], kg[h][bs]) for h in HS] for bs in blks]
            egl = [[jnp.exp(gl[h][bs.stop - 1:bs.stop]) for h in HS] for bs in blks]
            if exact:
                items = [(h, bs) for bs in blks for h in HS]
                pe = pairs_exact([q[h][bs] for h, bs in items], [k[h][bs] for h, bs in items],
                                 [v[h][bs] for h, bs in items], [gc[h][bs] for h, bs in items])
                intra = [jnp.concatenate([pe[bi * HEADS + h] for bi in range(B_GROUP)], axis=0) for h in HS]
            else:
                qf = [(q[h] * jnp.exp(gc[h] - gl[h])).astype(BF16) for h in HS]
                s = [jnp.where(same_block_lower, _dot_nt(qf[h], kg[h]), 0.0).astype(BF16) for h in HS]
                intra = [_dot(s[h], vb[h]) for h in HS]
            for bi, bs in enumerate(blks):
                o = [_dot_nt(qs[h][bs], st[h].astype(BF16)) + intra[h][bs] for h in HS]
                st = [st[h] * egl[bi][h] + upd[bi][h] for h in HS]
                o = [a * lax.rsqrt(jnp.mean(a * a, axis=-1, keepdims=True) + EPS) * gn_ref[...] for a in o]
                rs = slice(r0 + bi * KB, r0 + (bi + 1) * KB)
                o_ref[0, rs, :] = (jnp.concatenate(o, axis=1)
                                   * _sigmoid(g_ref[0, rs, :].astype(F32))).astype(o_ref.dtype)
        for h in HS:
            st_ref[h] = st[h]

    lax.cond(max_decay > HGRN_SAFE_DECAY, lambda: run(True), lambda: run(False))


def _hgrn(proj3, lb, gnorm):
    b, t, _ = proj3.shape
    R = B_ROWS
    seg = lambda s: pl.BlockSpec((1, R, HW), lambda i, c: (i, c, OFF_B // HW + s))
    const = lambda shape: pl.BlockSpec(shape, lambda i, c: (0,) * len(shape))
    return pl.pallas_call(
        _hgrn_kernel,
        grid=(b, t // R),
        in_specs=[seg(0), seg(1), seg(2), seg(3), const((1, HW)), const((1, DH))],
        out_specs=pl.BlockSpec((1, R, HW), lambda i, c: (i, c, 0)),
        out_shape=jax.ShapeDtypeStruct((b, t, HW), BF16),
        scratch_shapes=[pltpu.VMEM((HEADS, DH, DH), F32), pltpu.VMEM((R, HW), F32),
                        pltpu.VMEM((R, HW), F32)],
        compiler_params=_cparams(("parallel", "arbitrary")),
        name="hgrn2",
    )(proj3, proj3, proj3, proj3, lb.reshape(1, HW).astype(F32), gnorm.reshape(1, DH).astype(F32))


def _t5_bucket_np(n):
    max_exact = REL_BUCKETS // 2
    nf = np.maximum(n, 1).astype(np.float32)
    large = max_exact + (np.log(nf / max_exact) / math.log(REL_MAX_DIST / max_exact)
                         * (REL_BUCKETS - max_exact)).astype(np.int32)
    large = np.minimum(large, REL_BUCKETS - 1)
    return np.where(n < max_exact, n, large)


def _dilated_kernel(span, dil, hpb, nlb, q_ref, kp_ref, kc_ref, vp_ref, vc_ref, bias_ref,
                    o_ref, lse_ref):
    CB = C_BLOCK
    n = pl.program_id(1)
    qi = lax.broadcasted_iota(jnp.int32, (CB, 2 * CB), 0)
    kj = lax.broadcasted_iota(jnp.int32, (CB, 2 * CB), 1)
    dist = qi + CB - kj
    band = (dist >= 0) & (dist <= span)
    band_first = band & ((kj >= CB) | (n > 0))
    seg_lane = lax.broadcasted_iota(jnp.int32, (CB, DH), 1)
    if hpb == 1:
        own_seg = seg_lane // C_LSEG == pl.program_id(2)

        @pl.when(pl.program_id(2) == 0)
        def _():
            lse_ref[...] = jnp.zeros_like(lse_ref)
    else:
        assert hpb == C_TILES == HEADS

    def rows(r, jb):
        return pl.ds(jb * CB * dil + r, CB, stride=dil) if dil > 1 else pl.ds(jb * CB, CB)

    def with_prev(p_ref, c_ref, r, jb, sl):
        prev = p_ref[0, rows(r, 0), sl] if jb == 0 else c_ref[0, rows(r, jb - 1), sl]
        return jnp.concatenate([prev, c_ref[0, rows(r, jb), sl]], axis=0)

    def tiles(items):
        n_it = range(len(items))
        q = [q_ref[0, rows(r, jb), sl].astype(BF16) for r, jb, sl, _ in items]
        k = [with_prev(kp_ref, kc_ref, r, jb, sl).astype(BF16) for r, jb, sl, _ in items]
        v = [with_prev(vp_ref, vc_ref, r, jb, sl).astype(BF16) for r, jb, sl, _ in items]
        s = [_dot_nt(q[i], k[i]) * (DH ** -0.5) + bias_ref[items[i][3]] for i in n_it]
        s = [jnp.where(band_first if items[i][1] == 0 else band, s[i], -1e30) for i in n_it]
        mx = [jnp.max(a, axis=-1, keepdims=True) for a in s]
        p = [jnp.exp(s[i] - mx[i]) for i in n_it]
        den = [jnp.sum(a, axis=-1, keepdims=True) for a in p]
        o = [_dot(p[i].astype(BF16), v[i]) / den[i] for i in n_it]
        lse = [mx[i] + jnp.log(den[i]) for i in n_it]
        for i, (r, jb, sl, _) in enumerate(items):
            o_ref[0, rows(r, jb), sl] = o[i].astype(o_ref.dtype)
        if hpb == 1:
            for i, (r, jb, _, _) in enumerate(items):
                lse_ref[0, rows(r, jb), :] = jnp.where(own_seg, lse[i], lse_ref[0, rows(r, jb), :])
        else:
            row = jnp.zeros((CB, DH), F32)
            for i, (_, _, _, h) in enumerate(items):
                row = jnp.where(seg_lane // C_LSEG == h, lse[i], row)
            lse_ref[0, rows(items[0][0], items[0][1]), :] = row

    for jb in range(nlb):
        if dil == 1:
            for h0 in range(0, hpb, C_TILES):
                tiles([(0, jb, slice(h * DH, (h + 1) * DH), h) for h in range(h0, h0 + C_TILES)])
        elif dil == C_TILES:
            tiles([(r, jb, slice(0, DH), 0) for r in range(dil)])
        else:
            def body(g, carry, jb=jb):
                tiles([(g * C_TILES + i, jb, slice(0, DH), 0) for i in range(C_TILES)])
                return carry
            lax.fori_loop(0, dil // C_TILES, body, 0)


def _dilated_group(proj3, gi, rel_bias):
    b, t, _ = proj3.shape
    window, dil = C_PAIRS[gi]
    span = window // dil
    CB = C_BLOCK
    pb = CB * dil
    nlb = min(C_NLB, t // pb)
    rb = nlb * pb
    hpb = HEADS if dil == 1 else 1
    bw = hpb * DH
    qi = np.arange(CB)[:, None]
    kj = np.arange(2 * CB)[None, :]
    bucket = _t5_bucket_np(np.maximum(qi + CB - kj, 0) * dil)
    onehot = jnp.asarray(np.eye(REL_BUCKETS, dtype=np.float32)[bucket])
    bias = jnp.einsum("qkb,bh->hqk", onehot, rel_bias[:, gi * HEADS:(gi + 1) * HEADS].astype(F32),
                      precision=HIGHEST)

    def seg(which, prev):
        base = (OFF_C + which * C_GROUPS * HW + gi * HW) // bw
        if prev:
            return pl.BlockSpec((1, pb, bw), lambda i, n, h: (i, jnp.maximum(n * nlb - 1, 0), base + h))
        return pl.BlockSpec((1, rb, bw), lambda i, n, h: (i, n, base + h))

    o, lse = pl.pallas_call(
        functools.partial(_dilated_kernel, span, dil, hpb, nlb),
        grid=(b, t // rb, HEADS // hpb),
        in_specs=[seg(0, False), seg(1, True), seg(1, False), seg(2, True), seg(2, False),
                  pl.BlockSpec((hpb, CB, 2 * CB), lambda i, n, h: (h, 0, 0))],
        out_specs=[pl.BlockSpec((1, rb, bw), lambda i, n, h: (i, n, h)),
                   pl.BlockSpec((1, rb, DH), lambda i, n, h: (i, n, 0))],
        out_shape=[jax.ShapeDtypeStruct((b, t, HW), BF16 if dil == 1 else F32),
                   jax.ShapeDtypeStruct((b, t, DH), F32)],
        compiler_params=_cparams(("parallel", "arbitrary", "arbitrary")),
        name=f"dilated_g{gi}",
    )(proj3, proj3, proj3, proj3, proj3, bias)
    return o.reshape(b * t, HW), lse.reshape(b * t, DH)


def _mix_kernel(x_ref, gate_ref, oa_ref, ob_ref, oc0_ref, oc1_ref, oc2_ref,
                l0_ref, l1_ref, l2_ref, wa_ref, wb_ref, wc_ref, wo_ref, out_ref):
    l0, l1, l2 = l0_ref[...], l1_ref[...], l2_ref[...]
    mx = jnp.maximum(jnp.maximum(l0, l1), l2)
    e0, e1, e2 = jnp.exp(l0 - mx), jnp.exp(l1 - mx), jnp.exp(l2 - mx)
    inv = 1.0 / (e0 + e1 + e2)

    def spread(w):
        return jnp.concatenate([jnp.broadcast_to(w[:, h * C_LSEG:h * C_LSEG + 1], (w.shape[0], DH))
                                for h in range(HEADS)], axis=1)

    oc = (spread(e0 * inv) * oc0_ref[...] + spread(e1 * inv) * oc1_ref[...]
          + spread(e2 * inv) * oc2_ref[...])
    mix = (_sigmoid(gate_ref[:, 0:D_MODEL].astype(F32)) * _dot(oa_ref[...], wa_ref[...])
           + _sigmoid(gate_ref[:, D_MODEL:2 * D_MODEL].astype(F32)) * _dot(ob_ref[...], wb_ref[...])
           + _sigmoid(gate_ref[:, 2 * D_MODEL:3 * D_MODEL].astype(F32)) * _dot(oc.astype(BF16), wc_ref[...]))
    out_ref[...] = x_ref[...] + _dot(mix.astype(BF16), wo_ref[...])


def _mix(x2, proj2, oa, ob, ocs, lses, wa, wb, wc, wo, tm=512):
    m = x2.shape[0]
    tm = min(tm, m)
    rowblk = lambda w: pl.BlockSpec((tm, w), lambda i: (i, 0))
    const = lambda shape: pl.BlockSpec(shape, lambda i: (0,) * len(shape))
    return pl.pallas_call(
        _mix_kernel,
        grid=(m // tm,),
        in_specs=[rowblk(D_MODEL),
                  pl.BlockSpec((tm, 3 * D_MODEL), lambda i: (i, OFF_BRG // (3 * D_MODEL))),
                  rowblk(HW), rowblk(HW),
                  rowblk(HW), rowblk(HW), rowblk(HW), rowblk(DH), rowblk(DH), rowblk(DH),
                  const((HW, D_MODEL)), const((HW, D_MODEL)), const((HW, D_MODEL)),
                  const((D_MODEL, D_MODEL))],
        out_specs=rowblk(D_MODEL),
        out_shape=jax.ShapeDtypeStruct((m, D_MODEL), F32),
        compiler_params=_cparams(("parallel",)),
        name="branch_mix",
    )(x2, proj2, oa, ob, ocs[0], ocs[1], ocs[2], lses[0], lses[1], lses[2],
      wa.astype(BF16), wb.astype(BF16), wc.astype(BF16), wo.astype(BF16))


def _ffn_kernel(x_ref, g_ref, w1_ref, w3_ref, w2_ref, o_ref):
    x = x_ref[...]
    ms = jnp.mean(x * x, axis=-1, keepdims=True)
    h = (x * lax.rsqrt(ms + EPS) * g_ref[...]).astype(BF16)
    a = _dot(h, w1_ref[...])
    b = _dot(h, w3_ref[...])
    o_ref[...] = x + _dot((_silu(a) * b).astype(BF16), w2_ref[...])


def _ffn(x2, gain, w1, w3, w2, tm=512):
    m = x2.shape[0]
    ff = w1.shape[1]
    tm = min(tm, m)
    resident = lambda shape: pl.BlockSpec(shape, lambda i: (0, 0), pipeline_mode=pl.Buffered(1))
    return pl.pallas_call(
        _ffn_kernel,
        grid=(m // tm,),
        in_specs=[pl.BlockSpec((tm, D_MODEL), lambda i: (i, 0)),
                  pl.BlockSpec((1, D_MODEL), lambda i: (0, 0)),
                  resident((D_MODEL, ff)), resident((D_MODEL, ff)), resident((ff, D_MODEL))],
        out_specs=pl.BlockSpec((tm, D_MODEL), lambda i: (i, 0)),
        out_shape=jax.ShapeDtypeStruct((m, D_MODEL), F32),
        compiler_params=_cparams(("parallel",)),
        name="ffn",
    )(x2, gain.reshape(1, D_MODEL).astype(F32), w1.astype(BF16), w3.astype(BF16), w2.astype(BF16))


HALF = D_MODEL // 2
U32 = jnp.uint32


def _pack_bf16_pairs(x):
    r = x.astype(BF16).astype(F32)
    lo = lax.bitcast_convert_type(r[:, :HALF], U32) >> 16
    hi = lax.bitcast_convert_type(r[:, HALF:], U32)
    return hi | lo


def _unpack_bf16_pairs(p):
    lo = lax.bitcast_convert_type(p << 16, F32)
    hi = lax.bitcast_convert_type(p & U32(0xFFFF0000), F32)
    return lo, hi


def _router_kernel(x_ref, g_ref, wr_ref, h_ref, ids_ref, ps_ref):
    x = x_ref[...]
    ms = jnp.mean(x * x, axis=-1, keepdims=True)
    h = x * lax.rsqrt(ms + EPS) * g_ref[...]
    h_ref[...] = _pack_bf16_pairs(h)
    hh = h.astype(BF16)
    hl = (h - hh.astype(F32)).astype(BF16)
    w = wr_ref[...]
    wh = w.astype(BF16)
    wl = (w - wh.astype(F32)).astype(BF16)
    logits = _dot(hh, wh) + (_dot(hh, wl) + _dot(hl, wh))
    lane = lax.broadcasted_iota(jnp.int32, logits.shape, 1)
    neg = jnp.float32(-jnp.inf)
    l1 = jnp.where(lane < N_EXPERTS, logits, neg)
    m1 = jnp.max(l1, axis=-1, keepdims=True)
    i1 = jnp.min(jnp.where(l1 == m1, lane, LANES), axis=-1, keepdims=True)
    l2 = jnp.where(lane == i1, neg, l1)
    m2 = jnp.max(l2, axis=-1, keepdims=True)
    i2 = jnp.min(jnp.where(l2 == m2, lane, LANES), axis=-1, keepdims=True)
    e = jnp.exp(m2 - m1)
    p1 = 1.0 / (1.0 + e)
    p2 = e / (1.0 + e)
    ids_ref[...] = jnp.where(lane == 0, i1, jnp.where(lane == 1, i2, 0))
    ps_ref[...] = jnp.where(lane == 0, p1, jnp.where(lane == 1, p2, 0.0))


def _router(x2, gain, w_router, tm=1024):
    m = x2.shape[0]
    tm = min(tm, m)
    wr = jnp.pad(w_router.astype(F32), ((0, 0), (0, LANES - N_EXPERTS)))
    return pl.pallas_call(
        _router_kernel,
        grid=(m // tm,),
        in_specs=[pl.BlockSpec((tm, D_MODEL), lambda i: (i, 0)),
                  pl.BlockSpec((1, D_MODEL), lambda i: (0, 0)),
                  pl.BlockSpec((D_MODEL, LANES), lambda i: (0, 0))],
        out_specs=[pl.BlockSpec((tm, HALF), lambda i: (i, 0)),
                   pl.BlockSpec((tm, LANES), lambda i: (i, 0)),
                   pl.BlockSpec((tm, LANES), lambda i: (i, 0))],
        out_shape=[jax.ShapeDtypeStruct((m, HALF), U32),
                   jax.ShapeDtypeStruct((m, LANES), jnp.int32),
                   jax.ShapeDtypeStruct((m, LANES), F32)],
        compiler_params=_cparams(("parallel",)),
        name="router",
    )(x2, gain.reshape(1, D_MODEL).astype(F32), wr)


def _route_positions(ids, tm):
    m = ids.shape[0]
    e_flat = ids[:, :TOP_K].reshape(-1)
    onehot = (e_flat[:, None] == jnp.arange(N_EXPERTS)[None, :]).astype(jnp.int32)
    csum = jnp.cumsum(onehot, axis=0)
    counts = csum[-1]
    gsz = ((counts + tm - 1) // tm) * tm
    gend = jnp.cumsum(gsz)
    pos = jnp.sum(onehot * (gend - gsz + csum - 1), axis=1)
    nt = (TOP_K * m + N_EXPERTS * tm) // tm
    n_used = gend[-1] // tm
    tile_e = jnp.sum((jnp.arange(nt)[:, None] * tm >= gend[None, :]).astype(jnp.int32), axis=1)
    last_e = jnp.sum(((n_used - 1) * tm >= gend).astype(jnp.int32))
    tile_e = jnp.minimum(tile_e, last_e)
    return pos.astype(jnp.int32), tile_e.astype(jnp.int32), n_used.reshape(1).astype(jnp.int32), nt


def _dispatch_kernel(pos_ref, h_ref, init_ref, xs_ref, sem):
    del init_ref
    td = h_ref.shape[0]

    def row_copy(t, s):
        return pltpu.make_async_copy(h_ref.at[pl.ds(t, 1)],
                                     xs_ref.at[pl.ds(pos_ref[0, 0, TOP_K * t + s], 1)], sem)

    def start(t, c):
        for s in range(TOP_K):
            row_copy(t, s).start()
        return c

    lax.fori_loop(0, td, start, 0, unroll=8)
    for s in range(TOP_K):
        pltpu.make_async_copy(h_ref, xs_ref.at[pl.ds(0, td)], sem).wait()


def _dispatch(h, pos, npad, td=1024):
    m = h.shape[0]
    td = min(td, m)
    return pl.pallas_call(
        _dispatch_kernel,
        grid=(m // td,),
        in_specs=[pl.BlockSpec((1, 1, TOP_K * td), lambda i: (i, 0, 0), memory_space=pltpu.SMEM),
                  pl.BlockSpec((td, HALF), lambda i: (i, 0)),
                  pl.BlockSpec(memory_space=pl.ANY)],
        out_specs=pl.BlockSpec(memory_space=pl.ANY),
        out_shape=jax.ShapeDtypeStruct((npad, HALF), U32),
        scratch_shapes=[pltpu.SemaphoreType.DMA(())],
        input_output_aliases={2: 0},
        compiler_params=_cparams(("arbitrary",)),
        name="moe_dispatch",
    )(pos.reshape(m // td, 1, TOP_K * td), h, jnp.zeros((npad, HALF), U32))


def _experts_kernel(te_ref, nu_ref, xs_ref, w1_ref, w3_ref, w2_ref, y_ref, xb_ref, acc_ref):
    del te_ref
    i = pl.program_id(0)
    j = pl.program_id(1)

    @pl.when(i < nu_ref[0])
    def _():
        @pl.when(j == 0)
        def _():
            lo, hi = _unpack_bf16_pairs(xs_ref[...])
            xb_ref[...] = jnp.concatenate([lo.astype(BF16), hi.astype(BF16)], axis=1)
            acc_ref[...] = jnp.zeros_like(acc_ref)

        xb = xb_ref[...]
        a = _dot(xb, w1_ref[0])
        b = _dot(xb, w3_ref[0])
        acc_ref[...] += _dot((_silu(a) * b).astype(BF16), w2_ref[0])

        @pl.when(j == pl.num_programs(1) - 1)
        def _():
            y_ref[...] = _pack_bf16_pairs(acc_ref[...])

    @pl.when((i >= nu_ref[0]) & (j == 0))
    def _():
        y_ref[...] = jnp.zeros_like(y_ref)


def _experts(xs, tile_e, n_used, w1, w3, w2, tm, tf=1792):
    npad = xs.shape[0]
    ff = w1.shape[2]
    nj = ff // tf
    row = lambda i, j, te, nu: (jnp.minimum(i, nu[0] - 1), 0)
    jj = lambda i, j, nu: jnp.where(i < nu[0], j, nj - 1)
    return pl.pallas_call(
        _experts_kernel,
        grid_spec=pltpu.PrefetchScalarGridSpec(
            num_scalar_prefetch=2,
            grid=(npad // tm, nj),
            in_specs=[pl.BlockSpec((tm, HALF), row),
                      pl.BlockSpec((1, D_MODEL, tf), lambda i, j, te, nu: (te[i], 0, jj(i, j, nu))),
                      pl.BlockSpec((1, D_MODEL, tf), lambda i, j, te, nu: (te[i], 0, jj(i, j, nu))),
                      pl.BlockSpec((1, tf, D_MODEL), lambda i, j, te, nu: (te[i], jj(i, j, nu), 0))],
            out_specs=pl.BlockSpec((tm, HALF), lambda i, j, te, nu: (i, 0)),
            scratch_shapes=[pltpu.VMEM((tm, D_MODEL), BF16), pltpu.VMEM((tm, D_MODEL), F32)]),
        out_shape=jax.ShapeDtypeStruct((npad, HALF), U32),
        compiler_params=_cparams(("arbitrary", "arbitrary")),
        name="moe_experts",
    )(tile_e, n_used, xs, w1.astype(BF16), w3.astype(BF16), w2.astype(BF16))


def _combine_kernel(pos_ref, x_ref, ps_ref, y_ref, o_ref, buf_ref, sem):
    td = x_ref.shape[0]

    def row_copy(t, s):
        return pltpu.make_async_copy(y_ref.at[pl.ds(pos_ref[0, 0, TOP_K * t + s], 1)],
                                     buf_ref.at[s, pl.ds(t, 1)], sem.at[s])

    def start(t, c):
        for s in range(TOP_K):
            row_copy(t, s).start()
        return c

    lax.fori_loop(0, td, start, 0, unroll=8)
    for s in range(TOP_K):
        pltpu.make_async_copy(y_ref.at[pl.ds(0, td)], buf_ref.at[s], sem.at[s]).wait()
    ps = ps_ref[...]
    lo0, hi0 = _unpack_bf16_pairs(buf_ref[0])
    lo1, hi1 = _unpack_bf16_pairs(buf_ref[1])
    o_ref[:, :HALF] = x_ref[:, :HALF] + ps[:, 0:1] * lo0 + ps[:, 1:2] * lo1
    o_ref[:, HALF:] = x_ref[:, HALF:] + ps[:, 0:1] * hi0 + ps[:, 1:2] * hi1


def _combine(x2, ps, pos, y, td=1024):
    m = x2.shape[0]
    td = min(td, m)
    return pl.pallas_call(
        _combine_kernel,
        grid=(m // td,),
        in_specs=[pl.BlockSpec((1, 1, TOP_K * td), lambda i: (i, 0, 0), memory_space=pltpu.SMEM),
                  pl.BlockSpec((td, D_MODEL), lambda i: (i, 0)),
                  pl.BlockSpec((td, LANES), lambda i: (i, 0)),
                  pl.BlockSpec(memory_space=pl.ANY)],
        out_specs=pl.BlockSpec((td, D_MODEL), lambda i: (i, 0)),
        out_shape=jax.ShapeDtypeStruct((m, D_MODEL), F32),
        scratch_shapes=[pltpu.VMEM((TOP_K, td, HALF), U32), pltpu.SemaphoreType.DMA((TOP_K,))],
        compiler_params=_cparams(("arbitrary",)),
        name="moe_combine",
    )(pos.reshape(m // td, 1, TOP_K * td), x2, ps, y)


def _moe(x2, gain, w_router, w1, w3, w2, tm=1024):
    m = x2.shape[0]
    tm = min(tm, m)
    h, ids, ps = _router(x2, gain, w_router)
    pos, tile_e, n_used, nt = _route_positions(ids, tm)
    xs = _dispatch(h, pos, nt * tm)
    y = _experts(xs, tile_e, n_used, w1, w3, w2, tm)
    return _combine(x2, ps, pos, y)


def _relayout_w_in(w):
    k = w.shape[0]
    main = [w[:, _R_BRG:_R_END], w[:, _R_AQKV:_R_BETA], w[:, _R_AGATE:_R_B], w[:, _R_B:_R_C]]
    second = [w[:, _R_C:_R_BRG], w[:, _R_BETA:_R_AGATE],
              jnp.zeros((k, NP_C - OFF_BA - 2 * HEADS), w.dtype)]
    return (jnp.concatenate(main, axis=1).astype(BF16), jnp.concatenate(second, axis=1).astype(BF16))


def _layer_mixers(x2, b, t, layer, lower_bounds, w_in, norm_mix, conv_a, a_log, dt_bias,
                  gnorm_a, gnorm_b, qnorm_c, knorm_c, rel_bias, w_br_a, w_br_b, w_br_c, w_out):
    w_main, w_second = _relayout_w_in(w_in[layer])
    gain = norm_mix[layer].astype(F32)
    proj2 = _norm_proj(x2, gain, w_main, BF16)
    proj3 = proj2.reshape(b, t, NP_MAIN)
    qk_gain = jnp.concatenate([jnp.tile(qnorm_c[layer].astype(F32), (1, HEADS)).reshape(-1),
                               jnp.tile(knorm_c[layer].astype(F32), (1, HEADS)).reshape(-1)])
    projc3 = _norm_proj(x2, gain, w_second, F32, head_gain=qk_gain).reshape(b, t, NP_C)
    oa = _deltanet(proj3, projc3, conv_a[layer], a_log[layer], dt_bias[layer], gnorm_a[layer])
    ob = _hgrn(proj3, lower_bounds[layer], gnorm_b[layer])
    ocs, lses = [], []
    for gi in range(C_GROUPS):
        o, lse = _dilated_group(projc3, gi, rel_bias)
        ocs.append(o)
        lses.append(lse)
    return _mix(x2, proj2, oa.reshape(b * t, HW), ob.reshape(b * t, HW), ocs, lses,
                w_br_a[layer], w_br_b[layer], w_br_c[layer], w_out[layer])


def kernel(x, w_in, norm_mix, conv_a, a_log, dt_bias, gnorm_a, lb_logits, gnorm_b, qnorm_c, knorm_c, rel_bias, w_br_a, w_br_b, w_br_c, w_out, norm_ffn, ffn_w1, ffn_w3, ffn_w2, router, moe_w1, moe_w3, moe_w2):
    b, t, _ = x.shape
    depth = w_in.shape[0]
    p_lb = jax.nn.softmax(lb_logits.astype(F32), axis=0)
    lower_bounds = jnp.cumsum(p_lb, axis=0) - p_lb[0:1]
    x2 = x.reshape(b * t, D_MODEL).astype(F32)
    for layer in range(depth):
        x2 = _layer_mixers(x2, b, t, layer, lower_bounds, w_in, norm_mix, conv_a, a_log, dt_bias,
                           gnorm_a, gnorm_b, qnorm_c, knorm_c, rel_bias,
                           w_br_a, w_br_b, w_br_c, w_out)
        li = layer // 2
        if layer % 2 == 0:
            x2 = _ffn(x2, norm_ffn[layer], ffn_w1[li], ffn_w3[li], ffn_w2[li])
        else:
            x2 = _moe(x2, norm_ffn[layer], router[li], moe_w1[li], moe_w3[li], moe_w2[li])
    return x2.reshape(b, t, D_MODEL).astype(x.dtype)
```
